```python
import jax, jax.numpy as jnp
from jax import lax
import numpy as np

D_MODEL = 1024
BATCH = 8
SEQ = 4096
DEPTH = 1

N_MEM = 256
CONV_WIDTH = D_MODEL
CONV_K = 3
POOL_WINDOWS = (2, 4, 8, 16)
POOL_GROUPS = len(POOL_WINDOWS)
POOL_WIDTH = D_MODEL
POOL_GROUP_DIM = POOL_WIDTH // POOL_GROUPS
X_HEADS = 4
X_HEAD_DIM = D_MODEL // X_HEADS
X_WIDTH = X_HEADS * X_HEAD_DIM
N_BRANCH = 3
D_FF = ((8 * D_MODEL // 3 + 255) // 256) * 256
EPS = 1e-6
IN_SPLITS = (CONV_WIDTH, CONV_WIDTH, CONV_WIDTH, POOL_WIDTH, X_WIDTH, D_MODEL, D_MODEL, D_MODEL)
D_IN = sum(IN_SPLITS)

kernel_name = "hybrid_gated_conv_pool_memxattn_block"


def rms_norm(x, g):
    xf = x.astype(jnp.float32)
    y = xf * lax.rsqrt(jnp.mean(xf * xf, axis=-1, keepdims=True) + EPS)
    return (y * g.astype(jnp.float32)).astype(x.dtype)


def causal_depthwise_conv(u, w):
    c = u.shape[-1]
    return lax.conv_general_dilated(
        u, w[:, None, :].astype(u.dtype), window_strides=(1,), padding=[(CONV_K - 1, 0)],
        dimension_numbers=("NWC", "WIO", "NWC"), feature_group_count=c)


def multiscale_causal_pool(u):
    b, s, _ = u.shape
    uf = u.astype(jnp.float32).reshape(b, s, POOL_GROUPS, POOL_GROUP_DIM)
    csum = jnp.cumsum(uf, axis=1)
    pos = jnp.arange(1, s + 1, dtype=jnp.int32)
    outs = []
    for g, w in enumerate(POOL_WINDOWS):
        cg = csum[:, :, g]
        lower = jnp.pad(cg, ((0, 0), (w, 0), (0, 0)))[:, :s]
        cnt = jnp.minimum(pos, w).astype(jnp.float32)[None, :, None]
        outs.append((cg - lower) / cnt - uf[:, :, g])
    return jnp.stack(outs, axis=2).astype(u.dtype)


def memory_cross_attention(q, mem_n, w_kv):
    b, s, _ = q.shape
    m = mem_n.shape[1]
    qh = q.reshape(b, s, X_HEADS, X_HEAD_DIM)
    kv = mem_n @ w_kv
    k, v = jnp.split(kv, 2, axis=-1)
    k = k.reshape(b, m, X_HEADS, X_HEAD_DIM)
    v = v.reshape(b, m, X_HEADS, X_HEAD_DIM)
    scores = jnp.einsum("bshd,bmhd->bhsm", qh, k).astype(jnp.float32) * (X_HEAD_DIM ** -0.5)
    probs = jax.nn.softmax(scores, axis=-1).astype(v.dtype)
    o = jnp.einsum("bhsm,bmhd->bshd", probs, v)
    return o.reshape(b, s, X_WIDTH)


def _fwd_setup_inputs(seed: int = 0) -> dict:
    key = jax.random.key(seed)
    ks = jax.random.split(key, 20)
    f32 = jnp.float32
    L = DEPTH

    def nrm(k, shape, fan_in):
        return jax.random.normal(k, shape, f32) * (fan_in ** -0.5)

    def gain(k, shape):
        return 1.0 + 0.05 * jax.random.normal(k, shape, f32)

    return {
        "x": jax.random.normal(ks[0], (BATCH, SEQ, D_MODEL), f32),
        "mem": jax.random.normal(ks[1], (BATCH, N_MEM, D_MODEL), f32),
        "norm_mix": gain(ks[2], (L, D_MODEL)),
        "w_in": nrm(ks[3], (L, D_MODEL, D_IN), D_MODEL),
        "conv_w": nrm(ks[4], (L, CONV_K, CONV_WIDTH), CONV_K),
        "w_conv_out": nrm(ks[5], (L, CONV_WIDTH, D_MODEL), CONV_WIDTH),
        "w_pool": nrm(ks[6], (L, POOL_GROUPS, POOL_GROUP_DIM, POOL_GROUP_DIM), POOL_GROUP_DIM),
        "pool_scale": gain(ks[7], (L, POOL_WIDTH)),
        "norm_mem": gain(ks[8], (L, D_MODEL)),
        "w_kv": nrm(ks[9], (L, D_MODEL, 2 * X_WIDTH), D_MODEL),
        "w_xattn_out": nrm(ks[10], (L, X_WIDTH, D_MODEL), X_WIDTH),
        "w_out": nrm(ks[11], (L, D_MODEL, D_MODEL), D_MODEL),
        "norm_ffn": gain(ks[12], (L, D_MODEL)),
        "w_gate": nrm(ks[13], (L, D_MODEL, D_FF), D_MODEL),
        "w_up": nrm(ks[14], (L, D_MODEL, D_FF), D_MODEL),
        "w_down": nrm(ks[15], (L, D_FF, D_MODEL), D_FF),
        "norm_final": gain(ks[16], (D_MODEL,)),
    }


def _fwd_reference(x, mem, norm_mix, w_in, conv_w, w_conv_out, w_pool, pool_scale, norm_mem,
              w_kv, w_xattn_out, w_out, norm_ffn, w_gate, w_up, w_down, norm_final):
    b, s, _ = x.shape
    offsets = np.cumsum((0,) + IN_SPLITS)
    for l in range(DEPTH):
        h = rms_norm(x, norm_mix[l])
        proj = h @ w_in[l]
        b_a, c_a, u_a, u_p, q_x, g_a, g_p, g_x = [
            proj[..., int(offsets[i]):int(offsets[i + 1])] for i in range(len(IN_SPLITS))]

        y_a = (b_a * causal_depthwise_conv(c_a * u_a, conv_w[l])) @ w_conv_out[l]

        pooled = multiscale_causal_pool(u_p)
        y_p = jnp.einsum("bsgc,gcd->bsgd", pooled, w_pool[l]).reshape(b, s, POOL_WIDTH) * pool_scale[l]

        mem_n = rms_norm(mem, norm_mem[l])
        y_x = memory_cross_attention(q_x, mem_n, w_kv[l]) @ w_xattn_out[l]

        merged = (jax.nn.sigmoid(g_a) * y_a + jax.nn.sigmoid(g_p) * y_p
                  + jax.nn.sigmoid(g_x) * y_x)
        x = x + merged @ w_out[l]

        h = rms_norm(x, norm_ffn[l])
        x = x + (jax.nn.silu(h @ w_gate[l]) * (h @ w_up[l])) @ w_down[l]
    return rms_norm(x, norm_final)


import jax as _jax
import jax.numpy as _jnp

TWIN_FORMAT = 'train_step'
FWD_PARAMS = ['x', 'mem', 'norm_mix', 'w_in', 'conv_w', 'w_conv_out', 'w_pool', 'pool_scale', 'norm_mem', 'w_kv', 'w_xattn_out', 'w_out', 'norm_ffn', 'w_gate', 'w_up', 'w_down', 'norm_final']
TWIN_WEIGHTS = ['norm_mix', 'w_in', 'conv_w', 'w_conv_out', 'w_pool', 'pool_scale', 'norm_mem', 'w_kv', 'w_xattn_out', 'w_out', 'norm_ffn', 'w_gate', 'w_up', 'w_down', 'norm_final']
TWIN_DIFF_INPUT = 'x'
TWIN_INPUTS = ['x', 'mem', 'norm_mix', 'w_in', 'conv_w', 'w_conv_out', 'w_pool', 'pool_scale', 'norm_mem', 'w_kv', 'w_xattn_out', 'w_out', 'norm_ffn', 'w_gate', 'w_up', 'w_down', 'norm_final', 'loss_target', 'm_norm_mix', 'm_w_in', 'm_conv_w', 'm_w_conv_out', 'm_w_pool', 'm_pool_scale', 'm_norm_mem', 'm_w_kv', 'm_w_xattn_out', 'm_w_out', 'm_norm_ffn', 'm_w_gate', 'm_w_up', 'm_w_down', 'm_norm_final', 'v_norm_mix', 'v_w_in', 'v_conv_w', 'v_w_conv_out', 'v_w_pool', 'v_pool_scale', 'v_norm_mem', 'v_w_kv', 'v_w_xattn_out', 'v_w_out', 'v_norm_ffn', 'v_w_gate', 'v_w_up', 'v_w_down', 'v_norm_final']
TWIN_OUTPUTS = ['loss', 'grad_x', 'grad_norm_mix', 'grad_w_in', 'grad_conv_w', 'grad_w_conv_out', 'grad_w_pool', 'grad_pool_scale', 'grad_norm_mem', 'grad_w_kv', 'grad_w_xattn_out', 'grad_w_out', 'grad_norm_ffn', 'grad_w_gate', 'grad_w_up', 'grad_w_down', 'grad_norm_final', 'delta_norm_mix', 'delta_w_in', 'delta_conv_w', 'delta_w_conv_out', 'delta_w_pool', 'delta_pool_scale', 'delta_norm_mem', 'delta_w_kv', 'delta_w_xattn_out', 'delta_w_out', 'delta_norm_ffn', 'delta_w_gate', 'delta_w_up', 'delta_w_down', 'delta_norm_final', 'new_m_norm_mix', 'new_m_w_in', 'new_m_conv_w', 'new_m_w_conv_out', 'new_m_w_pool', 'new_m_pool_scale', 'new_m_norm_mem', 'new_m_w_kv', 'new_m_w_xattn_out', 'new_m_w_out', 'new_m_norm_ffn', 'new_m_w_gate', 'new_m_w_up', 'new_m_w_down', 'new_m_norm_final', 'new_v_norm_mix', 'new_v_w_in', 'new_v_conv_w', 'new_v_w_conv_out', 'new_v_w_pool', 'new_v_pool_scale', 'new_v_norm_mem', 'new_v_w_kv', 'new_v_w_xattn_out', 'new_v_w_out', 'new_v_norm_ffn', 'new_v_w_gate', 'new_v_w_up', 'new_v_w_down', 'new_v_norm_final']
TWIN_LEAF_KINDS = {'loss': 'loss', 'grad_x': 'grad_x', 'grad_norm_mix': 'grad_w', 'grad_w_in': 'grad_w', 'grad_conv_w': 'grad_w', 'grad_w_conv_out': 'grad_w', 'grad_w_pool': 'grad_w', 'grad_pool_scale': 'grad_w', 'grad_norm_mem': 'grad_w', 'grad_w_kv': 'grad_w', 'grad_w_xattn_out': 'grad_w', 'grad_w_out': 'grad_w', 'grad_norm_ffn': 'grad_w', 'grad_w_gate': 'grad_w', 'grad_w_up': 'grad_w', 'grad_w_down': 'grad_w', 'grad_norm_final': 'grad_w', 'delta_norm_mix': 'delta_w', 'delta_w_in': 'delta_w', 'delta_conv_w': 'delta_w', 'delta_w_conv_out': 'delta_w', 'delta_w_pool': 'delta_w', 'delta_pool_scale': 'delta_w', 'delta_norm_mem': 'delta_w', 'delta_w_kv': 'delta_w', 'delta_w_xattn_out': 'delta_w', 'delta_w_out': 'delta_w', 'delta_norm_ffn': 'delta_w', 'delta_w_gate': 'delta_w', 'delta_w_up': 'delta_w', 'delta_w_down': 'delta_w', 'delta_norm_final': 'delta_w', 'new_m_norm_mix': 'new_m', 'new_m_w_in': 'new_m', 'new_m_conv_w': 'new_m', 'new_m_w_conv_out': 'new_m', 'new_m_w_pool': 'new_m', 'new_m_pool_scale': 'new_m', 'new_m_norm_mem': 'new_m', 'new_m_w_kv': 'new_m', 'new_m_w_xattn_out': 'new_m', 'new_m_w_out': 'new_m', 'new_m_norm_ffn': 'new_m', 'new_m_w_gate': 'new_m', 'new_m_w_up': 'new_m', 'new_m_w_down': 'new_m', 'new_m_norm_final': 'new_m', 'new_v_norm_mix': 'new_v', 'new_v_w_in': 'new_v', 'new_v_conv_w': 'new_v', 'new_v_w_conv_out': 'new_v', 'new_v_w_pool': 'new_v', 'new_v_pool_scale': 'new_v', 'new_v_norm_mem': 'new_v', 'new_v_w_kv': 'new_v', 'new_v_w_xattn_out': 'new_v', 'new_v_w_out': 'new_v', 'new_v_norm_ffn': 'new_v', 'new_v_w_gate': 'new_v', 'new_v_w_up': 'new_v', 'new_v_w_down': 'new_v', 'new_v_norm_final': 'new_v'}


def _forward(args):
    return _fwd_reference(*[args[k] for k in FWD_PARAMS])


def _output_shape():
    def fwd():
        inp = _fwd_setup_inputs(0)
        return _fwd_reference(*[inp[k] for k in FWD_PARAMS])
    out = _jax.eval_shape(fwd)
    return out.shape, out.dtype

N_MICROBATCH = 1
ADAM_LR = 0.001
ADAM_B1 = 0.9
ADAM_B2 = 0.999
ADAM_EPS = 1e-08
ADAM_WD = 0.01
ADAM_STEP = 10
PER_EXAMPLE_BATCH_AXIS = {'x': 0, 'mem': 0, 'loss_target': 0}
SHARED_INPUTS = []
_WEIGHT_DTYPES = {'norm_mix': _jnp.float32, 'w_in': _jnp.float32, 'conv_w': _jnp.float32, 'w_conv_out': _jnp.float32, 'w_pool': _jnp.float32, 'pool_scale': _jnp.float32, 'norm_mem': _jnp.float32, 'w_kv': _jnp.float32, 'w_xattn_out': _jnp.float32, 'w_out': _jnp.float32, 'norm_ffn': _jnp.float32, 'w_gate': _jnp.float32, 'w_up': _jnp.float32, 'w_down': _jnp.float32, 'norm_final': _jnp.float32}
MOMENT_SCALE = {'norm_mix': 1.755754e-01, 'w_in': 6.165413e-02, 'conv_w': 8.842069e-02, 'w_conv_out': 8.675187e-02, 'w_pool': 7.769634e-02, 'pool_scale': 7.751294e-02, 'norm_mem': 1.357964e-02, 'w_kv': 9.126781e-03, 'w_xattn_out': 9.183902e-03, 'w_out': 1.178372e-01, 'norm_ffn': 1.148013e-01, 'w_gate': 4.874681e-02, 'w_up': 4.760578e-02, 'w_down': 7.935243e-02, 'norm_final': 3.210993e+01}


def _to_microbatches(a, axis):
    t = _jnp.moveaxis(a, axis, 0)
    t = t.reshape((N_MICROBATCH, t.shape[0] // N_MICROBATCH) + t.shape[1:])
    return _jnp.moveaxis(t, 1, axis + 1)


def setup_inputs(seed: int = 0) -> dict:
    inp = _fwd_setup_inputs(seed)
    key = _jax.random.fold_in(_jax.random.key(seed), 7919)
    shape, _ = _output_shape()
    out = dict(inp)
    out["loss_target"] = _jax.random.normal(_jax.random.fold_in(key, 0), shape, _jnp.float32)
    for i, name in enumerate(TWIN_WEIGHTS):
        w = inp[name].astype(_jnp.float32)
        if MOMENT_SCALE is None:
            s = _jnp.sqrt(_jnp.mean(_jnp.square(w)) + 1e-30)
        else:
            s = MOMENT_SCALE[name]
        km, kv = _jax.random.split(_jax.random.fold_in(key, i + 1))
        out[name] = w
        out["m_" + name] = s * _jax.random.normal(km, w.shape, _jnp.float32)
        out["v_" + name] = (s * s) * _jax.random.uniform(kv, w.shape, _jnp.float32, 0.5, 1.5)
    if N_MICROBATCH > 1:
        for name, axis in PER_EXAMPLE_BATCH_AXIS.items():
            out[name] = _to_microbatches(out[name], axis)
    return {'x': out['x'], 'mem': out['mem'], 'norm_mix': out['norm_mix'], 'w_in': out['w_in'], 'conv_w': out['conv_w'], 'w_conv_out': out['w_conv_out'], 'w_pool': out['w_pool'], 'pool_scale': out['pool_scale'], 'norm_mem': out['norm_mem'], 'w_kv': out['w_kv'], 'w_xattn_out': out['w_xattn_out'], 'w_out': out['w_out'], 'norm_ffn': out['norm_ffn'], 'w_gate': out['w_gate'], 'w_up': out['w_up'], 'w_down': out['w_down'], 'norm_final': out['norm_final'], 'loss_target': out['loss_target'], 'm_norm_mix': out['m_norm_mix'], 'm_w_in': out['m_w_in'], 'm_conv_w': out['m_conv_w'], 'm_w_conv_out': out['m_w_conv_out'], 'm_w_pool': out['m_w_pool'], 'm_pool_scale': out['m_pool_scale'], 'm_norm_mem': out['m_norm_mem'], 'm_w_kv': out['m_w_kv'], 'm_w_xattn_out': out['m_w_xattn_out'], 'm_w_out': out['m_w_out'], 'm_norm_ffn': out['m_norm_ffn'], 'm_w_gate': out['m_w_gate'], 'm_w_up': out['m_w_up'], 'm_w_down': out['m_w_down'], 'm_norm_final': out['m_norm_final'], 'v_norm_mix': out['v_norm_mix'], 'v_w_in': out['v_w_in'], 'v_conv_w': out['v_conv_w'], 'v_w_conv_out': out['v_w_conv_out'], 'v_w_pool': out['v_w_pool'], 'v_pool_scale': out['v_pool_scale'], 'v_norm_mem': out['v_norm_mem'], 'v_w_kv': out['v_w_kv'], 'v_w_xattn_out': out['v_w_xattn_out'], 'v_w_out': out['v_w_out'], 'v_norm_ffn': out['v_norm_ffn'], 'v_w_gate': out['v_w_gate'], 'v_w_up': out['v_w_up'], 'v_w_down': out['v_w_down'], 'v_norm_final': out['v_norm_final']}


def _loss(weights, diff, rest, loss_target):
    with _jax.named_scope("forward"):
        args = {**rest, TWIN_DIFF_INPUT: diff, **{k: w.astype(_WEIGHT_DTYPES[k]) for k, w in weights.items()}}
        y = _forward(args)
    with _jax.named_scope("loss_head"):
        err = _jnp.square(y.astype(_jnp.float32) - loss_target)
        return 0.5 * _jnp.sum(_jnp.mean(err, axis=-1)) if err.ndim else 0.5 * err


def _adamw(w, g, m, v):
    m = ADAM_B1 * m + (1.0 - ADAM_B1) * g
    v = ADAM_B2 * v + (1.0 - ADAM_B2) * _jnp.square(g)
    m_hat = m / (1.0 - ADAM_B1 ** ADAM_STEP)
    v_hat = v / (1.0 - ADAM_B2 ** ADAM_STEP)
    delta = -ADAM_LR * (m_hat / (_jnp.sqrt(v_hat) + ADAM_EPS) + ADAM_WD * w)
    return delta, m, v


def reference(x, mem, norm_mix, w_in, conv_w, w_conv_out, w_pool, pool_scale, norm_mem, w_kv, w_xattn_out, w_out, norm_ffn, w_gate, w_up, w_down, norm_final, loss_target, m_norm_mix, m_w_in, m_conv_w, m_w_conv_out, m_w_pool, m_pool_scale, m_norm_mem, m_w_kv, m_w_xattn_out, m_w_out, m_norm_ffn, m_w_gate, m_w_up, m_w_down, m_norm_final, v_norm_mix, v_w_in, v_conv_w, v_w_conv_out, v_w_pool, v_pool_scale, v_norm_mem, v_w_kv, v_w_xattn_out, v_w_out, v_norm_ffn, v_w_gate, v_w_up, v_w_down, v_norm_final):
    given = dict(x=x, mem=mem, norm_mix=norm_mix, w_in=w_in, conv_w=conv_w, w_conv_out=w_conv_out, w_pool=w_pool, pool_scale=pool_scale, norm_mem=norm_mem, w_kv=w_kv, w_xattn_out=w_xattn_out, w_out=w_out, norm_ffn=norm_ffn, w_gate=w_gate, w_up=w_up, w_down=w_down, norm_final=norm_final, loss_target=loss_target, m_norm_mix=m_norm_mix, m_w_in=m_w_in, m_conv_w=m_conv_w, m_w_conv_out=m_w_conv_out, m_w_pool=m_w_pool, m_pool_scale=m_pool_scale, m_norm_mem=m_norm_mem, m_w_kv=m_w_kv, m_w_xattn_out=m_w_xattn_out, m_w_out=m_w_out, m_norm_ffn=m_norm_ffn, m_w_gate=m_w_gate, m_w_up=m_w_up, m_w_down=m_w_down, m_norm_final=m_norm_final, v_norm_mix=v_norm_mix, v_w_in=v_w_in, v_conv_w=v_conv_w, v_w_conv_out=v_w_conv_out, v_w_pool=v_w_pool, v_pool_scale=v_pool_scale, v_norm_mem=v_norm_mem, v_w_kv=v_w_kv, v_w_xattn_out=v_w_xattn_out, v_w_out=v_w_out, v_norm_ffn=v_norm_ffn, v_w_gate=v_w_gate, v_w_up=v_w_up, v_w_down=v_w_down, v_norm_final=v_norm_final)
    weights = {n: given[n] for n in TWIN_WEIGHTS}
    shared = {n: given[n] for n in SHARED_INPUTS}
    per_example = {n: given[n] for n in ['x', 'mem']}
    grad_fn = _jax.value_and_grad(_loss, argnums=(0, 1))

    def one_microbatch(ex, loss_target):
        ex = dict(ex)
        diff = ex.pop(TWIN_DIFF_INPUT)
        return grad_fn(weights, diff, {**shared, **ex}, loss_target)

    if N_MICROBATCH == 1:
        loss, (grad_w, grad_x) = one_microbatch(per_example, given["loss_target"])
    else:
        def body(carry, xs):
            loss_sum, grad_sum = carry
            l_k, (gw_k, gx_k) = one_microbatch(xs[0], xs[1])
            with _jax.named_scope("update"):
                return (loss_sum + l_k, _jax.tree.map(_jnp.add, grad_sum, gw_k)), gx_k

        init = (_jnp.zeros((), _jnp.float32), _jax.tree.map(_jnp.zeros_like, weights))
        (loss, grad_w), grad_x = _jax.lax.scan(body, init, (per_example, given["loss_target"]))
    with _jax.named_scope("update"):
        delta_w, new_m, new_v = {}, {}, {}
        for n in TWIN_WEIGHTS:
            delta_w[n], new_m[n], new_v[n] = _adamw(weights[n], grad_w[n], given["m_" + n], given["v_" + n])
    return (loss, grad_x, *[grad_w[n] for n in TWIN_WEIGHTS], *[delta_w[n] for n in TWIN_WEIGHTS],
            *[new_m[n] for n in TWIN_WEIGHTS], *[new_v[n] for n in TWIN_WEIGHTS])
```

```python
import jax
import jax.numpy as jnp
from jax import lax
from jax.experimental import pallas as pl
from jax.experimental.pallas import tpu as pltpu

F32 = jnp.float32
BF16 = jnp.bfloat16
SDS = jax.ShapeDtypeStruct

AXES = ("x", "y", "c")
NDEV = 8
D = 1024
NSPLIT = 8
NH = 4
HD = D // NH
NPOOL = 4
DFF = 2816
EPS = 1e-6
ATT_SCALE = HD ** -0.5
HALO = 16


def _slot_group(s):
    return jnp.where(s < 3, s + 5, jnp.where(s == 3, 4, s - 4))


ADAM_LR = 0.001
ADAM_B1 = 0.9
ADAM_B2 = 0.999
ADAM_EPS = 1e-08
ADAM_WD = 0.01
ADAM_STEP = 10

V7X_VMEM_BYTES = 64 * 1024 * 1024
VMEM_LIMIT = V7X_VMEM_BYTES - 8 * 1024 * 1024
HBM = pl.BlockSpec(memory_space=pl.ANY)


def _params(n_grid):
    return pltpu.CompilerParams(dimension_semantics=("arbitrary",) * n_grid, vmem_limit_bytes=VMEM_LIMIT)


def _mm(a, b):
    return jnp.dot(a, b, preferred_element_type=F32)


def _mm_nt(a, b):
    return lax.dot_general(a, b, (((1,), (1,)), ((), ())), preferred_element_type=F32)


def _mm_tn(a, b):
    return lax.dot_general(a, b, (((0,), (0,)), ((), ())), preferred_element_type=F32)


def _sigmoid(x):
    return 1.0 / (1.0 + jnp.exp(-x))


def _rms(x):
    return lax.rsqrt(jnp.mean(x * x, axis=-1, keepdims=True) + EPS)


def _norm_bwd(dh, x, gain):
    r = _rms(x)
    xh = x * r
    dxh = dh * gain
    dx = r * (dxh - xh * jnp.mean(dxh * xh, axis=-1, keepdims=True))
    return dx, jnp.sum(dh * xh, axis=0, keepdims=True)


def _shift_down(v, k):
    return pltpu.roll(v, k, 0)


def _shift_up(v, k):
    return pltpu.roll(v, v.shape[0] - k, 0)


def _fwd_proj(x, gain, w_in_g, tm):
    T = x.shape[0]

    def body(x_ref, g_ref, w_ref, proj_ref, h_ref):
        @pl.when(pl.program_id(1) == 0)
        def _():
            xf = x_ref[...]
            h_ref[...] = (xf * _rms(xf) * g_ref[...]).astype(BF16)
        proj_ref[...] = _mm(h_ref[...], w_ref[...]).astype(BF16)

    return pl.pallas_call(
        body, name="fwd_proj", grid=(T // tm, NSPLIT),
        in_specs=[pl.BlockSpec((tm, D), lambda i, j: (i, 0)), pl.BlockSpec((1, D), lambda i, j: (0, 0)),
                  pl.BlockSpec((None, D, D), lambda i, j: (j, 0, 0))],
        out_specs=[pl.BlockSpec((None, tm, D), lambda i, j: (j, i, 0)), pl.BlockSpec((tm, D), lambda i, j: (i, 0))],
        out_shape=[SDS((NSPLIT, T, D), BF16), SDS((T, D), BF16)],
        compiler_params=_params(2))(x, gain, w_in_g)


def _fwd_kv(mem, gain, w_kv_g):
    M = mem.shape[0]

    def body(mem_ref, g_ref, w_ref, kv_ref, memn_ref):
        @pl.when(pl.program_id(0) == 0)
        def _():
            m = mem_ref[...]
            memn_ref[...] = (m * _rms(m) * g_ref[...]).astype(BF16)
        kv_ref[...] = _mm(memn_ref[...], w_ref[...]).astype(BF16)

    return pl.pallas_call(
        body, name="fwd_kv", grid=(2 * NH,),
        in_specs=[pl.BlockSpec((M, D), lambda j: (0, 0)), pl.BlockSpec((1, D), lambda j: (0, 0)),
                  pl.BlockSpec((None, D, HD), lambda j: (j, 0, 0))],
        out_specs=[pl.BlockSpec((None, M, HD), lambda j: (j, 0, 0)), pl.BlockSpec((M, D), lambda j: (0, 0))],
        out_shape=[SDS((2 * NH, M, HD), BF16), SDS((M, D), BF16)],
        compiler_params=_params(1))(mem, gain, w_kv_g)


def _halo_before(split, tm):
    return pl.BlockSpec((None, HALO, D), lambda i: (split, jnp.maximum(i * (tm // HALO) - 1, 0), 0))


def _fwd_mix(proj, cw0, cw1, cw2, w_co, w_pool, tm):
    T = proj.shape[1]

    def body(b_ref, c_ref, ua_ref, up_ref, ch_ref, uah_ref, uph_ref, cw0_ref, cw1_ref, cw2_ref, wco_ref, wp_ref,
             za_ref, conv_ref, pooled_ref, ya_ref, yp_ref):
        i = pl.program_id(0)
        keep = jnp.where(i > 0, 1.0, 0.0).astype(F32)
        cu = c_ref[...].astype(F32) * ua_ref[...].astype(F32)
        cu_h = ch_ref[...].astype(F32) * uah_ref[...].astype(F32) * keep
        ext = jnp.concatenate([cu_h, cu], axis=0)
        conv = (cw2_ref[...] * ext + cw1_ref[...] * _shift_down(ext, 1) + cw0_ref[...] * _shift_down(ext, 2))[HALO:]
        za = (b_ref[...].astype(F32) * conv).astype(BF16)
        conv_ref[...] = conv.astype(BF16)
        za_ref[...] = za
        ya_ref[...] = _mm(za, wco_ref[...]).astype(BF16)

        up = up_ref[...].astype(F32)
        ext_u = jnp.concatenate([uph_ref[...].astype(F32) * keep, up], axis=0)
        pos = i * tm + lax.broadcasted_iota(jnp.int32, (tm, HD), 0)
        for g in range(NPOOL):
            cols = slice(g * HD, (g + 1) * HD)
            s = ext_u[:, cols]
            for k in range(g + 1):
                s = s + _shift_down(s, 1 << k)
            cnt = jnp.minimum(pos + 1, 2 << g).astype(F32)
            pooled = (s[HALO:] / cnt - up[:, cols]).astype(BF16)
            pooled_ref[:, cols] = pooled
            yp_ref[:, cols] = _mm(pooled, wp_ref[g]).astype(BF16)

    tile = lambda s: pl.BlockSpec((None, tm, D), lambda i: (s, i, 0))
    row = pl.BlockSpec((1, D), lambda i: (0, 0))
    out = pl.BlockSpec((tm, D), lambda i: (i, 0))
    return pl.pallas_call(
        body, name="fwd_mix", grid=(T // tm,),
        in_specs=[tile(0), tile(1), tile(2), tile(3), _halo_before(1, tm), _halo_before(2, tm), _halo_before(3, tm),
                  row, row, row, pl.BlockSpec((D, D), lambda i: (0, 0)), pl.BlockSpec((NPOOL, HD, HD), lambda i: (0, 0, 0))],
        out_specs=[out] * 5,
        out_shape=[SDS((T, D), BF16)] * 5,
        compiler_params=_params(1))(proj, proj, proj, proj, proj, proj, proj, cw0, cw1, cw2, w_co, w_pool)


def _softmax_rows(s):
    e = jnp.exp(s - jnp.max(s, axis=-1, keepdims=True))
    return e / jnp.sum(e, axis=-1, keepdims=True)


def _fwd_merge(proj, ya, yp, x, kv, w_xo, w_o, pscale, gain_ffn, tm):
    T = x.shape[0]

    def body(q_ref, ga_ref, gp_ref, gx_ref, ya_ref, yp_ref, x_ref, kv_ref, wxo_ref, wo_ref, ps_ref, gf_ref,
             o_ref, yx_ref, merged_ref, x1_ref, h2_ref):
        for h in range(NH):
            cols = slice(h * HD, (h + 1) * HD)
            p = _softmax_rows(_mm_nt(q_ref[:, cols], kv_ref[h]) * ATT_SCALE)
            o_ref[:, cols] = _mm(p.astype(BF16), kv_ref[NH + h]).astype(BF16)
        yx = _mm(o_ref[...], wxo_ref[...])
        yx_ref[...] = yx.astype(BF16)
        merged = (_sigmoid(ga_ref[...].astype(F32)) * ya_ref[...].astype(F32)
                  + _sigmoid(gp_ref[...].astype(F32)) * (yp_ref[...].astype(F32) * ps_ref[...])
                  + _sigmoid(gx_ref[...].astype(F32)) * yx).astype(BF16)
        merged_ref[...] = merged
        x1 = x_ref[...] + _mm(merged, wo_ref[...])
        x1_ref[...] = x1
        h2_ref[...] = (x1 * _rms(x1) * gf_ref[...]).astype(BF16)

    tile = lambda s: pl.BlockSpec((None, tm, D), lambda i: (s, i, 0))
    row = pl.BlockSpec((1, D), lambda i: (0, 0))
    act = pl.BlockSpec((tm, D), lambda i: (i, 0))
    full = pl.BlockSpec((D, D), lambda i: (0, 0))
    return pl.pallas_call(
        body, name="fwd_merge", grid=(T // tm,),
        in_specs=[tile(4), tile(5), tile(6), tile(7), act, act, act,
                  pl.BlockSpec((2 * NH, kv.shape[1], HD), lambda i: (0, 0, 0)), full, full, row, row],
        out_specs=[act] * 5,
        out_shape=[SDS((T, D), BF16), SDS((T, D), BF16), SDS((T, D), BF16), SDS((T, D), F32), SDS((T, D), BF16)],
        compiler_params=_params(1))(proj, proj, proj, proj, ya, yp, x, kv, w_xo, w_o, pscale, gain_ffn)


def _fwd_ffn_up(h2, wg_t, wu_t, tm, tn):
    T = h2.shape[0]

    def body(h_ref, wg_ref, wu_ref, gate_ref, up_ref, act_ref):
        gate = _mm_nt(h_ref[...], wg_ref[...])
        up = _mm_nt(h_ref[...], wu_ref[...])
        gate_ref[...] = gate.astype(BF16)
        up_ref[...] = up.astype(BF16)
        act_ref[...] = (gate * _sigmoid(gate) * up).astype(BF16)

    w = pl.BlockSpec((tn, D), lambda i, n: (n, 0))
    o = pl.BlockSpec((tm, tn), lambda i, n: (i, n))
    return pl.pallas_call(
        body, name="fwd_ffn_up", grid=(T // tm, DFF // tn),
        in_specs=[pl.BlockSpec((tm, D), lambda i, n: (i, 0)), w, w],
        out_specs=[o] * 3, out_shape=[SDS((T, DFF), BF16)] * 3,
        compiler_params=_params(2))(h2, wg_t, wu_t)


def _fwd_ffn_down_loss(act, w_d, x1, target, gain_final, tm):
    T = x1.shape[0]

    def body(act_ref, wd_ref, x1_ref, tgt_ref, g_ref, dx2_ref, loss_ref, dgain_ref):
        @pl.when(pl.program_id(0) == 0)
        def _():
            loss_ref[...] = jnp.zeros_like(loss_ref)
            dgain_ref[...] = jnp.zeros_like(dgain_ref)
        x2 = x1_ref[...] + _mm(act_ref[...], wd_ref[...])
        gain = g_ref[...]
        y = x2 * _rms(x2) * gain
        err = y - tgt_ref[...]
        loss_ref[...] += 0.5 * jnp.sum(jnp.mean(err * err, axis=-1, keepdims=True))
        dx2, dgain = _norm_bwd(err * (1.0 / D), x2, gain)
        dx2_ref[...] = dx2
        dgain_ref[...] += dgain

    act_spec = pl.BlockSpec((tm, D), lambda i: (i, 0))
    row = pl.BlockSpec((1, D), lambda i: (0, 0))
    return pl.pallas_call(
        body, name="fwd_ffn_down_loss", grid=(T // tm,),
        in_specs=[pl.BlockSpec((tm, DFF), lambda i: (i, 0)), pl.BlockSpec((DFF, D), lambda i: (0, 0)), act_spec, act_spec, row],
        out_specs=[act_spec, pl.BlockSpec((8, 128), lambda i: (0, 0)), row],
        out_shape=[SDS((T, D), F32), SDS((8, 128), F32), SDS((1, D), F32)],
        compiler_params=_params(1))(act, w_d, x1, target, gain_final)


def _bwd_ffn_down(dx2, w_d, gate, up, tm, tn):
    T = dx2.shape[0]

    def body(dx_ref, wd_ref, gate_ref, up_ref, dgate_ref, dup_ref):
        dact = _mm_nt(dx_ref[...].astype(BF16), wd_ref[...])
        gate = gate_ref[...].astype(F32)
        sg = _sigmoid(gate)
        dgate_ref[...] = (dact * up_ref[...].astype(F32) * (sg * (1.0 + gate * (1.0 - sg)))).astype(BF16)
        dup_ref[...] = (dact * gate * sg).astype(BF16)

    o = pl.BlockSpec((tm, tn), lambda i, n: (i, n))
    return pl.pallas_call(
        body, name="bwd_ffn_down", grid=(T // tm, DFF // tn),
        in_specs=[pl.BlockSpec((tm, D), lambda i, n: (i, 0)), pl.BlockSpec((tn, D), lambda i, n: (n, 0)), o, o],
        out_specs=[o] * 2, out_shape=[SDS((T, DFF), BF16)] * 2,
        compiler_params=_params(2))(dx2, w_d, gate, up)


def _bwd_ffn_up(dgate, dup, wg_t, wu_t, x1, dx2, gain_ffn, tm):
    T = x1.shape[0]

    def body(dg_ref, du_ref, wg_ref, wu_ref, x1_ref, dx2_ref, g_ref, dx1_ref, dgain_ref):
        @pl.when(pl.program_id(0) == 0)
        def _():
            dgain_ref[...] = jnp.zeros_like(dgain_ref)
        dh2 = _mm(dg_ref[...], wg_ref[...]) + _mm(du_ref[...], wu_ref[...])
        dx, dgain = _norm_bwd(dh2, x1_ref[...], g_ref[...])
        dx1_ref[...] = dx2_ref[...] + dx
        dgain_ref[...] += dgain

    wide = pl.BlockSpec((tm, DFF), lambda i: (i, 0))
    w = pl.BlockSpec((DFF, D), lambda i: (0, 0))
    act = pl.BlockSpec((tm, D), lambda i: (i, 0))
    row = pl.BlockSpec((1, D), lambda i: (0, 0))
    return pl.pallas_call(
        body, name="bwd_ffn_up", grid=(T // tm,),
        in_specs=[wide, wide, w, w, act, act, row], out_specs=[act, row],
        out_shape=[SDS((T, D), F32), SDS((1, D), F32)],
        compiler_params=_params(1))(dgate, dup, wg_t, wu_t, x1, dx2, gain_ffn)


def _wgrad(a, b, *, name, groups, a_cols, b_cols, tt, a_index, b_index, o_index, out_shape):
    T = a.shape[0]
    nt = T // tt
    n_a = a.shape[1] // a_cols if groups == 1 else 1

    def body(a_ref, b_ref, o_ref, acc_ref):
        t = pl.program_id(2)

        @pl.when(t == 0)
        def _():
            acc_ref[...] = jnp.zeros_like(acc_ref)
        acc_ref[...] += _mm_tn(a_ref[...].astype(BF16), b_ref[...].astype(BF16))

        @pl.when(t == nt - 1)
        def _():
            o_ref[...] = acc_ref[...].astype(o_ref.dtype)

    return pl.pallas_call(
        body, name=name, grid=(groups, n_a, nt),
        in_specs=[pl.BlockSpec((tt, a_cols), a_index), pl.BlockSpec((None, tt, b_cols), b_index)],
        out_specs=pl.BlockSpec((None, a_cols, b_cols), o_index),
        out_shape=SDS(out_shape, BF16),
        scratch_shapes=[pltpu.VMEM((a_cols, b_cols), F32)],
        compiler_params=_params(3))(a, b)


def _wgrad_dense(a, b, name, tt, a_cols=None):
    ka, nb = a.shape[1], b.shape[1]
    a_cols = ka if a_cols is None else a_cols
    out = _wgrad(a, b[None], name=name, groups=1, a_cols=a_cols, b_cols=nb, tt=tt,
                 a_index=lambda g, k, t: (t, k), b_index=lambda g, k, t: (0, t, 0),
                 o_index=lambda g, k, t: (k, 0, 0), out_shape=(ka // a_cols, a_cols, nb))
    return out.reshape(ka, nb)


def _bwd_merge(dx1, proj, ya, yp, yx, pscale, w_o, w_co, w_xo, w_pool, tm):
    T = dx1.shape[0]

    def body(dx1_ref, ga_ref, gp_ref, gx_ref, ya_ref, yp_ref, yx_ref, ps_ref, wo_ref, wco_ref, wxo_ref, wp_ref,
             dgates_ref, dya_ref, dyx_ref, dyps_ref, dza_ref, do_ref, dpooled_ref, dps_ref):
        @pl.when(pl.program_id(0) == 0)
        def _():
            dps_ref[...] = jnp.zeros_like(dps_ref)
        dmerged = _mm_nt(dx1_ref[...].astype(BF16), wo_ref[...])
        scale = ps_ref[...]
        sa, sp, sx = (_sigmoid(r[...].astype(F32)) for r in (ga_ref, gp_ref, gx_ref))
        ya, yp_pre, yx = (r[...].astype(F32) for r in (ya_ref, yp_ref, yx_ref))
        dgates_ref[0] = (dmerged * ya * sa * (1.0 - sa)).astype(BF16)
        dgates_ref[1] = (dmerged * (yp_pre * scale) * sp * (1.0 - sp)).astype(BF16)
        dgates_ref[2] = (dmerged * yx * sx * (1.0 - sx)).astype(BF16)
        dya = (dmerged * sa).astype(BF16)
        dyx = (dmerged * sx).astype(BF16)
        dyp = dmerged * sp
        dyps = (dyp * scale).astype(BF16)
        dps_ref[...] += jnp.sum(dyp * yp_pre, axis=0, keepdims=True)
        dya_ref[...] = dya
        dyx_ref[...] = dyx
        dyps_ref[...] = dyps
        dza_ref[...] = _mm_nt(dya, wco_ref[...]).astype(BF16)
        do_ref[...] = _mm_nt(dyx, wxo_ref[...]).astype(BF16)
        for g in range(NPOOL):
            cols = slice(g * HD, (g + 1) * HD)
            dpooled_ref[:, cols] = _mm_nt(dyps[:, cols], wp_ref[g]).astype(BF16)

    tile = lambda s: pl.BlockSpec((None, tm, D), lambda i: (s, i, 0))
    row = pl.BlockSpec((1, D), lambda i: (0, 0))
    act = pl.BlockSpec((tm, D), lambda i: (i, 0))
    full = pl.BlockSpec((D, D), lambda i: (0, 0))
    return pl.pallas_call(
        body, name="bwd_merge", grid=(T // tm,),
        in_specs=[act, tile(5), tile(6), tile(7), act, act, act, row, full, full, full,
                  pl.BlockSpec((NPOOL, HD, HD), lambda i: (0, 0, 0))],
        out_specs=[pl.BlockSpec((3, tm, D), lambda i: (0, i, 0))] + [act] * 6 + [row],
        out_shape=[SDS((NSPLIT, T, D), BF16)] + [SDS((T, D), BF16)] * 6 + [SDS((1, D), F32)],
        compiler_params=_params(1))(dx1, proj, proj, proj, ya, yp, yx, pscale, w_o, w_co, w_xo, w_pool)


def _bwd_attn(dproj, proj, do, kv, tm):
    T = do.shape[0]
    M = kv.shape[1]

    def body(dproj_hbm, q_ref, do_ref, kv_ref, dq_ref, dkv_ref):
        del dproj_hbm

        @pl.when(pl.program_id(0) == 0)
        def _():
            dkv_ref[...] = jnp.zeros_like(dkv_ref)
        for h in range(NH):
            cols = slice(h * HD, (h + 1) * HD)
            q = q_ref[:, cols]
            do_h = do_ref[:, cols]
            p = _softmax_rows(_mm_nt(q, kv_ref[h]) * ATT_SCALE)
            dp = _mm_nt(do_h, kv_ref[NH + h])
            ds = (p * (dp - jnp.sum(dp * p, axis=-1, keepdims=True)) * ATT_SCALE).astype(BF16)
            dq_ref[:, cols] = _mm(ds, kv_ref[h]).astype(BF16)
            dkv_ref[h] += _mm_tn(ds, q)
            dkv_ref[NH + h] += _mm_tn(p.astype(BF16), do_h)

    kv_spec = pl.BlockSpec((2 * NH, M, HD), lambda i: (0, 0, 0))
    return pl.pallas_call(
        body, name="bwd_attn", grid=(T // tm,),
        in_specs=[HBM, pl.BlockSpec((None, tm, D), lambda i: (4, i, 0)), pl.BlockSpec((tm, D), lambda i: (i, 0)), kv_spec],
        out_specs=[pl.BlockSpec((None, tm, D), lambda i: (3, i, 0)), kv_spec],
        out_shape=[SDS(dproj.shape, BF16), SDS((2 * NH, M, HD), F32)],
        input_output_aliases={0: 0},
        compiler_params=_params(1))(dproj, proj, do, kv)


def _bwd_mix(dproj, proj, conv, dza, dpooled, cw0, cw1, cw2, tm):
    T = dza.shape[0]
    nt = T // tm

    def halo_after(split_or_none):
        idx = lambda i: jnp.minimum((i + 1) * (tm // HALO), T // HALO - 1)
        if split_or_none is None:
            return pl.BlockSpec((HALO, D), lambda i: (idx(i), 0))
        return pl.BlockSpec((None, HALO, D), lambda i: (split_or_none, idx(i), 0))

    def body(dproj_hbm, b_ref, c_ref, ua_ref, conv_ref, dza_ref, dpo_ref, bn_ref, dzan_ref, dpon_ref, ch_ref, uah_ref,
             cw0_ref, cw1_ref, cw2_ref, dabcu_ref, dcw_ref):
        del dproj_hbm
        i = pl.program_id(0)

        @pl.when(i == 0)
        def _():
            dcw_ref[...] = jnp.zeros_like(dcw_ref)
        keep_prev = jnp.where(i > 0, 1.0, 0.0).astype(F32)
        keep_next = jnp.where(i < nt - 1, 1.0, 0.0).astype(F32)
        dza = dza_ref[...].astype(F32)
        c = c_ref[...].astype(F32)
        ua = ua_ref[...].astype(F32)
        dconv = dza * b_ref[...].astype(F32)
        dconv_n = dzan_ref[...].astype(F32) * bn_ref[...].astype(F32) * keep_next
        ext = jnp.concatenate([dconv, dconv_n], axis=0)
        dcu = (cw2_ref[...] * ext + cw1_ref[...] * _shift_up(ext, 1) + cw0_ref[...] * _shift_up(ext, 2))[:tm]
        dabcu_ref[0] = (dza * conv_ref[...].astype(F32)).astype(BF16)
        dabcu_ref[1] = (dcu * ua).astype(BF16)
        dabcu_ref[2] = (dcu * c).astype(BF16)

        cu = c * ua
        ext_cu = jnp.concatenate([ch_ref[...].astype(F32) * uah_ref[...].astype(F32) * keep_prev, cu], axis=0)
        dcw_ref[2:3, :] += jnp.sum(dconv * cu, axis=0, keepdims=True)
        dcw_ref[1:2, :] += jnp.sum(dconv * _shift_down(ext_cu, 1)[HALO:], axis=0, keepdims=True)
        dcw_ref[0:1, :] += jnp.sum(dconv * _shift_down(ext_cu, 2)[HALO:], axis=0, keepdims=True)

        dpo = dpo_ref[...].astype(F32)
        ext_dpo = jnp.concatenate([dpo, dpon_ref[...].astype(F32) * keep_next], axis=0)
        pos = i * tm + lax.broadcasted_iota(jnp.int32, (tm + HALO, HD), 0)
        for g in range(NPOOL):
            cols = slice(g * HD, (g + 1) * HD)
            s = ext_dpo[:, cols] / jnp.minimum(pos + 1, 2 << g).astype(F32)
            for k in range(g + 1):
                s = s + _shift_up(s, 1 << k)
            dabcu_ref[3, :, cols] = (s[:tm] - dpo[:, cols]).astype(BF16)

    tile = lambda s: pl.BlockSpec((None, tm, D), lambda i: (s, i, 0))
    act = pl.BlockSpec((tm, D), lambda i: (i, 0))
    row = pl.BlockSpec((1, D), lambda i: (0, 0))
    return pl.pallas_call(
        body, name="bwd_mix", grid=(nt,),
        in_specs=[HBM, tile(0), tile(1), tile(2), act, act, act, halo_after(0), halo_after(None), halo_after(None),
                  _halo_before(1, tm), _halo_before(2, tm), row, row, row],
        out_specs=[pl.BlockSpec((4, tm, D), lambda i: (1, i, 0)), pl.BlockSpec((8, D), lambda i: (0, 0))],
        out_shape=[SDS(dproj.shape, BF16), SDS((8, D), F32)],
        input_output_aliases={0: 0},
        compiler_params=_params(1))(dproj, proj, proj, proj, conv, dza, dpooled, proj, dza, dpooled, proj, proj, cw0, cw1, cw2)


def _bwd_proj(dproj, w_in_g, x, dx1, gain, tm):
    T = x.shape[0]

    def body(dp_ref, w_ref, x_ref, dx1_ref, g_ref, dx_ref, dgain_ref, acc_ref):
        i, s = pl.program_id(0), pl.program_id(1)

        @pl.when((i == 0) & (s == 0))
        def _():
            dgain_ref[...] = jnp.zeros_like(dgain_ref)

        @pl.when(s == 0)
        def _():
            acc_ref[...] = jnp.zeros_like(acc_ref)
        acc_ref[...] += _mm_nt(dp_ref[...], w_ref[...])

        @pl.when(s == NSPLIT - 1)
        def _():
            dx, dgain = _norm_bwd(acc_ref[...], x_ref[...], g_ref[...])
            dx_ref[...] = dx1_ref[...] + dx
            dgain_ref[...] += dgain

    act = pl.BlockSpec((tm, D), lambda i, s: (i, 0))
    row = pl.BlockSpec((1, D), lambda i, s: (0, 0))
    return pl.pallas_call(
        body, name="bwd_proj", grid=(T // tm, NSPLIT),
        in_specs=[pl.BlockSpec((None, tm, D), lambda i, s: (s, i, 0)),
                  pl.BlockSpec((None, D, D), lambda i, s: (_slot_group(s), 0, 0)), act, act, row],
        out_specs=[act, row], out_shape=[SDS((T, D), F32), SDS((1, D), F32)],
        scratch_shapes=[pltpu.VMEM((tm, D), F32)],
        compiler_params=_params(2))(dproj, w_in_g, x, dx1, gain)


def _bwd_kv(dkv, memn, w_kv_g, mem, gain):
    M = mem.shape[0]

    def body(dkv_ref, memn_ref, w_ref, mem_ref, g_ref, dw_ref, dgain_ref, acc_ref):
        j = pl.program_id(0)

        @pl.when(j == 0)
        def _():
            acc_ref[...] = jnp.zeros_like(acc_ref)
        dkv_j = dkv_ref[...].astype(BF16)
        dw_ref[...] = _mm_tn(memn_ref[...], dkv_j).astype(BF16)
        acc_ref[...] += _mm_nt(dkv_j, w_ref[...])

        @pl.when(j == 2 * NH - 1)
        def _():
            dgain_ref[...] = _norm_bwd(acc_ref[...], mem_ref[...], g_ref[...])[1]

    row = pl.BlockSpec((1, D), lambda j: (0, 0))
    return pl.pallas_call(
        body, name="bwd_kv", grid=(2 * NH,),
        in_specs=[pl.BlockSpec((None, M, HD), lambda j: (j, 0, 0)), pl.BlockSpec((M, D), lambda j: (0, 0)),
                  pl.BlockSpec((None, D, HD), lambda j: (j, 0, 0)), pl.BlockSpec((M, D), lambda j: (0, 0)), row],
        out_specs=[pl.BlockSpec((None, D, HD), lambda j: (j, 0, 0)), row],
        out_shape=[SDS((2 * NH, D, HD), BF16), SDS((1, D), F32)],
        scratch_shapes=[pltpu.VMEM((M, D), F32)],
        compiler_params=_params(1))(dkv, memn, w_kv_g, mem, gain)


def _adamw_math(w, g, m, v):
    m = ADAM_B1 * m + (1.0 - ADAM_B1) * g
    v = ADAM_B2 * v + (1.0 - ADAM_B2) * (g * g)
    m_hat = m / (1.0 - ADAM_B1 ** ADAM_STEP)
    v_hat = v / (1.0 - ADAM_B2 ** ADAM_STEP)
    delta = -ADAM_LR * (m_hat / (jnp.sqrt(v_hat) + ADAM_EPS) + ADAM_WD * w)
    return delta, m, v


def _row_tile(rows):
    return 256 if rows % 256 == 0 else rows


def _sum_parts(parts, name):
    _, rows, cols = parts.shape
    tr = _row_tile(rows)

    def body(p_ref, g_ref):
        g = p_ref[0].astype(F32)
        for k in range(1, NDEV):
            g = g + p_ref[k].astype(F32)
        g_ref[...] = g

    blk = pl.BlockSpec((tr, cols), lambda i: (i, 0))
    return pl.pallas_call(
        body, name=name, grid=(rows // tr,),
        in_specs=[pl.BlockSpec((NDEV, tr, cols), lambda i: (0, i, 0))], out_specs=blk,
        out_shape=SDS((rows, cols), F32), compiler_params=_params(1))(parts)


def _adamw(w, g, m, v, name, from_parts):
    rows, cols = w.shape
    tr = _row_tile(rows)

    def body(w_ref, g_ref, m_ref, v_ref, go_ref, d_ref, mo_ref, vo_ref):
        if from_parts:
            g = g_ref[0].astype(F32)
            for k in range(1, NDEV):
                g = g + g_ref[k].astype(F32)
        else:
            g = g_ref[...]
        go_ref[...] = g
        d_ref[...], mo_ref[...], vo_ref[...] = _adamw_math(w_ref[...], g, m_ref[...], v_ref[...])

    blk = pl.BlockSpec((tr, cols), lambda i: (i, 0))
    g_spec = pl.BlockSpec((NDEV, tr, cols), lambda i: (0, i, 0)) if from_parts else blk
    return pl.pallas_call(
        body, name=name, grid=(rows // tr,),
        in_specs=[blk, g_spec, blk, blk], out_specs=[blk] * 4,
        out_shape=[SDS((rows, cols), F32)] * 4, compiler_params=_params(1))(w, g, m, v)


def _peer(k, x, y, c):
    return ((1 - x) if k & 4 else x, (1 - y) if k & 2 else y, (1 - c) if k & 1 else c)


def _exchange(arrays, name, scatter):
    n = len(arrays)

    def body(*refs):
        ins, outs = refs[:n], refs[n:2 * n]
        send_sems, recv_sems, local_sems = refs[2 * n:]
        x, y, c = (lax.axis_index(a) for a in AXES)
        me = 4 * x + 2 * y + c

        def remote(a, k):
            px, py, pc = _peer(k, x, y, c)
            there = 4 * px + 2 * py + pc
            return pltpu.make_async_remote_copy(
                src_ref=ins[a].at[there] if scatter else ins[a], dst_ref=outs[a].at[me],
                send_sem=send_sems.at[a, k - 1], recv_sem=recv_sems.at[a, k - 1],
                device_id=(px, py, pc), device_id_type=pl.DeviceIdType.MESH)

        def arrival(a, k):
            px, py, pc = _peer(k, x, y, c)
            there = 4 * px + 2 * py + pc
            return pltpu.make_async_remote_copy(
                src_ref=ins[a].at[there] if scatter else ins[a], dst_ref=outs[a].at[there],
                send_sem=send_sems.at[a, k - 1], recv_sem=recv_sems.at[a, k - 1],
                device_id=(px, py, pc), device_id_type=pl.DeviceIdType.MESH)

        own = [pltpu.make_async_copy(ins[a].at[me] if scatter else ins[a], outs[a].at[me], local_sems.at[a]) for a in range(n)]
        for a in range(n):
            own[a].start()
            for k in range(1, NDEV):
                remote(a, k).start()
        for a in range(n):
            for k in range(1, NDEV):
                arrival(a, k).wait_recv()
        for a in range(n):
            for k in range(1, NDEV):
                remote(a, k).wait_send()
            own[a].wait()

    out_shape = [SDS(a.shape if scatter else (NDEV,) + a.shape, a.dtype) for a in arrays]
    return pl.pallas_call(
        body, name=name, in_specs=[HBM] * n, out_specs=[HBM] * n, out_shape=out_shape,
        scratch_shapes=[pltpu.SemaphoreType.DMA((n, NDEV - 1)), pltpu.SemaphoreType.DMA((n, NDEV - 1)),
                        pltpu.SemaphoreType.DMA((n,))],
        compiler_params=pltpu.CompilerParams(has_side_effects=True))(*arrays)


def _local_step(x, mem, target, gains, wts, tm_big=1024, tm_mid=512, tm_small=256):
    g_mix, pscale, g_mem, g_ffn, g_fin = gains
    w_in, cw0, cw1, cw2, w_co, w_xo, w_o, w_pool, w_kv, wg_t, wu_t, w_d = wts
    T = x.shape[0]
    tm_big, tm_mid, tm_small = min(tm_big, T), min(tm_mid, T), min(tm_small, T)
    tn = DFF // 2

    proj, h = _fwd_proj(x, g_mix, w_in, tm_big)
    kv, memn = _fwd_kv(mem, g_mem, w_kv)
    za, conv, pooled, ya, yp = _fwd_mix(proj, cw0, cw1, cw2, w_co, w_pool, tm_small)
    o, yx, merged, x1, h2 = _fwd_merge(proj, ya, yp, x, kv, w_xo, w_o, pscale, g_ffn, tm_small)
    gate, up, act = _fwd_ffn_up(h2, wg_t, wu_t, tm_mid, tn)
    dx2, loss, dg_fin = _fwd_ffn_down_loss(act, w_d, x1, target, g_fin, tm_small)

    dgate, dup = _bwd_ffn_down(dx2, w_d, gate, up, tm_mid, tn)
    dx1, dg_ffn = _bwd_ffn_up(dgate, dup, wg_t, wu_t, x1, dx2, g_ffn, tm_small)
    dw_d = _wgrad_dense(act, dx2, "wgrad_down", tm_mid, a_cols=tn)
    dwg_t = _wgrad_dense(dgate, h2, "wgrad_gate", tm_mid, a_cols=tn)
    dwu_t = _wgrad_dense(dup, h2, "wgrad_up", tm_mid, a_cols=tn)

    dproj, dya, dyx, dyps, dza, do, dpooled, dpscale = _bwd_merge(dx1, proj, ya, yp, yx, pscale, w_o, w_co, w_xo, w_pool, tm_small)
    dw_o = _wgrad_dense(merged, dx1, "wgrad_out", tm_mid)
    dw_co = _wgrad_dense(za, dya, "wgrad_conv_out", tm_mid)
    dw_xo = _wgrad_dense(o, dyx, "wgrad_xattn_out", tm_mid)
    dw_pool = _wgrad(pooled, dyps[None], name="wgrad_pool", groups=NPOOL, a_cols=HD, b_cols=HD, tt=tm_mid,
                     a_index=lambda g, k, t: (t, g), b_index=lambda g, k, t: (0, t, g),
                     o_index=lambda g, k, t: (g, 0, 0), out_shape=(NPOOL, HD, HD))
    dproj, dkv = _bwd_attn(dproj, proj, do, kv, tm_small)
    dw_kv, dg_mem = _bwd_kv(dkv, memn, w_kv, mem, g_mem)
    dproj, dcw = _bwd_mix(dproj, proj, conv, dza, dpooled, cw0, cw1, cw2, tm_small)
    grad_x, dg_mix = _bwd_proj(dproj, w_in, x, dx1, g_mix, tm_mid)
    dw_in = _wgrad(h, dproj, name="wgrad_in", groups=NSPLIT, a_cols=D, b_cols=D, tt=tm_mid,
                   a_index=lambda g, k, t: (t, 0), b_index=lambda g, k, t: (g, t, 0),
                   o_index=lambda g, k, t: (_slot_group(g), 0, 0), out_shape=(NSPLIT, D, D))

    small = jnp.concatenate([dg_mix, dpscale, dg_mem, dg_ffn, dg_fin, dcw[0:3]], axis=0)
    return loss[0, 0], grad_x, small, (dw_in, dw_co, dw_xo, dw_o, dw_pool, dw_kv, dwg_t, dwu_t, dw_d)


def kernel(x, mem, norm_mix, w_in, conv_w, w_conv_out, w_pool, pool_scale, norm_mem, w_kv, w_xattn_out, w_out, norm_ffn, w_gate, w_up, w_down, norm_final, loss_target, m_norm_mix, m_w_in, m_conv_w, m_w_conv_out, m_w_pool, m_pool_scale, m_norm_mem, m_w_kv, m_w_xattn_out, m_w_out, m_norm_ffn, m_w_gate, m_w_up, m_w_down, m_norm_final, v_norm_mix, v_w_in, v_conv_w, v_w_conv_out, v_w_pool, v_pool_scale, v_norm_mem, v_w_kv, v_w_xattn_out, v_w_out, v_norm_ffn, v_w_gate, v_w_up, v_w_down, v_norm_final):
    T = x.shape[1]
    rows = D // NDEV
    ffb = DFF // NDEV
    prow = HD // NDEV
    me = 4 * lax.axis_index("x") + 2 * lax.axis_index("y") + lax.axis_index("c")

    shards = [w_in[0].astype(BF16), w_conv_out[0].astype(BF16), w_xattn_out[0].astype(BF16), w_out[0].astype(BF16),
              w_pool[0].astype(BF16).reshape(NPOOL * prow, HD), w_kv[0].astype(BF16),
              w_gate[0].T.astype(BF16), w_up[0].T.astype(BF16), w_down[0].astype(BF16),
              jnp.pad(conv_w[0], ((0, 5), (0, 0)))]
    w_in_g, w_co_g, w_xo_g, w_o_g, w_pool_g, w_kv_g, wg_g, wu_g, wd_g, cw_g = _exchange(shards, "gather_weights", scatter=False)
    w_pool_full = w_pool_g.reshape(NDEV, NPOOL, prow, HD).transpose(1, 0, 2, 3).reshape(NPOOL, HD, HD)
    cw_full = cw_g.transpose(1, 0, 2).reshape(8, D)
    wts = (w_in_g, cw_full[0:1], cw_full[1:2], cw_full[2:3], w_co_g.reshape(D, D), w_xo_g.reshape(D, D), w_o_g.reshape(D, D),
           w_pool_full, w_kv_g, wg_g.reshape(DFF, D), wu_g.reshape(DFF, D), wd_g.reshape(DFF, D))
    gains = (norm_mix, pool_scale, norm_mem, norm_ffn, norm_final.reshape(1, D))

    loss_part, grad_x, small, dws = _local_step(x[0], mem[0], loss_target[0], gains, wts)
    dw_in, dw_co, dw_xo, dw_o, dw_pool, dw_kv, dwg_t, dwu_t, dw_d = dws

    sends = [dw_in, dw_co.reshape(NDEV, rows, D), dw_xo.reshape(NDEV, rows, D), dw_o.reshape(NDEV, rows, D),
             dw_pool.reshape(NPOOL, NDEV, prow, HD).transpose(1, 0, 2, 3).reshape(NDEV, NPOOL * prow, HD), dw_kv,
             dwg_t.reshape(NDEV, ffb, D), dwu_t.reshape(NDEV, ffb, D), dw_d.reshape(NDEV, ffb, D)]
    p_in, p_co, p_xo, p_o, p_pool, p_kv, p_g, p_u, p_d = _exchange(sends, "scatter_grads", scatter=True)
    (small_all,) = _exchange([small], "gather_small", scatter=False)
    small_sum = _sum_parts(small_all, "sum_small")

    loss = lax.psum(loss_part, AXES)

    def sharded(name, w, parts, m, v):
        shape = w.shape
        flat = lambda a: a.reshape(parts.shape[1], parts.shape[2])
        outs = _adamw(flat(w), parts, flat(m), flat(v), "adamw_" + name, from_parts=True)
        return [o.reshape(shape) for o in outs]

    def transposed(name, w, parts, m, v):
        g = _sum_parts(parts, "sum_" + name).T
        outs = _adamw(w[0], g, m[0], v[0], "adamw_" + name, from_parts=False)
        return [o[None] for o in outs]

    def replicated(name, w, g, m, v):
        shape = w.shape
        flat = lambda a: a.reshape(g.shape)
        outs = _adamw(flat(w), g, flat(m), flat(v), "adamw_" + name, from_parts=False)
        return [o.reshape(shape) for o in outs]

    g_cw = lax.dynamic_slice_in_dim(small_sum[5:8], me * rows, rows, axis=1)
    res = {
        "norm_mix": replicated("norm_mix", norm_mix, small_sum[0:1], m_norm_mix, v_norm_mix),
        "w_in": sharded("w_in", w_in, p_in, m_w_in, v_w_in),
        "conv_w": replicated("conv_w", conv_w, g_cw, m_conv_w, v_conv_w),
        "w_conv_out": sharded("w_conv_out", w_conv_out, p_co, m_w_conv_out, v_w_conv_out),
        "w_pool": sharded("w_pool", w_pool, p_pool, m_w_pool, v_w_pool),
        "pool_scale": replicated("pool_scale", pool_scale, small_sum[1:2], m_pool_scale, v_pool_scale),
        "norm_mem": replicated("norm_mem", norm_mem, small_sum[2:3], m_norm_mem, v_norm_mem),
        "w_kv": sharded("w_kv", w_kv, p_kv, m_w_kv, v_w_kv),
        "w_xattn_out": sharded("w_xattn_out", w_xattn_out, p_xo, m_w_xattn_out, v_w_xattn_out),
        "w_out": sharded("w_out", w_out, p_o, m_w_out, v_w_out),
        "norm_ffn": replicated("norm_ffn", norm_ffn, small_sum[3:4], m_norm_ffn, v_norm_ffn),
        "w_gate": transposed("w_gate", w_gate, p_g, m_w_gate, v_w_gate),
        "w_up": transposed("w_up", w_up, p_u, m_w_up, v_w_up),
        "w_down": sharded("w_down", w_down, p_d, m_w_down, v_w_down),
        "norm_final": replicated("norm_final", norm_final, small_sum[4:5], m_norm_final, v_norm_final),
    }
    order = ["norm_mix", "w_in", "conv_w", "w_conv_out", "w_pool", "pool_scale", "norm_mem", "w_kv", "w_xattn_out", "w_out",
             "norm_ffn", "w_gate", "w_up", "w_down", "norm_final"]
    return (loss, grad_x[None], *[res[n][0] for n in order], *[res[n][1] for n in order],
            *[res[n][2] for n in order], *[res[n][3] for n in order])
```

```python
import jax
import jax.numpy as jnp
from jax import lax
from jax.experimental import pallas as pl
from jax.experimental.pallas import tpu as pltpu

F32 = jnp.float32
BF16 = jnp.bfloat16
SDS = jax.ShapeDtypeStruct

AXES = ("x", "y", "c")
NDEV = 8
D = 1024
NSPLIT = 8
NH = 4
HD = D // NH
NPOOL = 4
DFF = 2816
EPS = 1e-6
ATT_SCALE = HD ** -0.5
HALO = 16


def _slot_group(s):
    return jnp.where(s < 3, s + 5, jnp.where(s == 3, 4, s - 4))


ADAM_LR = 0.001
ADAM_B1 = 0.9
ADAM_B2 = 0.999
ADAM_EPS = 1e-08
ADAM_WD = 0.01
ADAM_STEP = 10

V7X_VMEM_BYTES = 64 * 1024 * 1024
VMEM_LIMIT = V7X_VMEM_BYTES - 8 * 1024 * 1024
HBM = pl.BlockSpec(memory_space=pl.ANY)


def _params(n_grid):
    return pltpu.CompilerParams(dimension_semantics=("arbitrary",) * n_grid, vmem_limit_bytes=VMEM_LIMIT)


def _mm(a, b):
    return jnp.dot(a, b, preferred_element_type=F32)


def _mm_nt(a, b):
    return lax.dot_general(a, b, (((1,), (1,)), ((), ())), preferred_element_type=F32)


def _mm_tn(a, b):
    return lax.dot_general(a, b, (((0,), (0,)), ((), ())), preferred_element_type=F32)


def _sigmoid(x):
    return 1.0 / (1.0 + jnp.exp(-x))


def _rms(x):
    return lax.rsqrt(jnp.mean(x * x, axis=-1, keepdims=True) + EPS)


def _norm_bwd(dh, x, gain):
    r = _rms(x)
    xh = x * r
    dxh = dh * gain
    dx = r * (dxh - xh * jnp.mean(dxh * xh, axis=-1, keepdims=True))
    return dx, jnp.sum(dh * xh, axis=0, keepdims=True)


def _shift_down(v, k):
    return pltpu.roll(v, k, 0)


def _shift_up(v, k):
    return pltpu.roll(v, v.shape[0] - k, 0)


def _fwd_proj(x, gain, w_in_g, tm):
    T = x.shape[0]

    def body(x_ref, g_ref, w_ref, proj_ref, h_ref):
        @pl.when(pl.program_id(1) == 0)
        def _():
            xf = x_ref[...]
            h_ref[...] = (xf * _rms(xf) * g_ref[...]).astype(BF16)
        proj_ref[...] = _mm(h_ref[...], w_ref[...]).astype(BF16)

    return pl.pallas_call(
        body, name="fwd_proj", grid=(T // tm, NSPLIT),
        in_specs=[pl.BlockSpec((tm, D), lambda i, j: (i, 0)), pl.BlockSpec((1, D), lambda i, j: (0, 0)),
                  pl.BlockSpec((None, D, D), lambda i, j: (j, 0, 0))],
        out_specs=[pl.BlockSpec((None, tm, D), lambda i, j: (j, i, 0)), pl.BlockSpec((tm, D), lambda i, j: (i, 0))],
        out_shape=[SDS((NSPLIT, T, D), BF16), SDS((T, D), BF16)],
        compiler_params=_params(2))(x, gain, w_in_g)


def _fwd_kv(mem, gain, w_kv_g):
    M = mem.shape[0]

    def body(mem_ref, g_ref, w_ref, kv_ref, memn_ref):
        @pl.when(pl.program_id(0) == 0)
        def _():
            m = mem_ref[...]
            memn_ref[...] = (m * _rms(m) * g_ref[...]).astype(BF16)
        kv_ref[...] = _mm(memn_ref[...], w_ref[...]).astype(BF16)

    return pl.pallas_call(
        body, name="fwd_kv", grid=(2 * NH,),
        in_specs=[pl.BlockSpec((M, D), lambda j: (0, 0)), pl.BlockSpec((1, D), lambda j: (0, 0)),
                  pl.BlockSpec((None, D, HD), lambda j: (j, 0, 0))],
        out_specs=[pl.BlockSpec((None, M, HD), lambda j: (j, 0, 0)), pl.BlockSpec((M, D), lambda j: (0, 0))],
        out_shape=[SDS((2 * NH, M, HD), BF16), SDS((M, D), BF16)],
        compiler_params=_params(1))(mem, gain, w_kv_g)


def _halo_before(split, tm):
    return pl.BlockSpec((None, HALO, D), lambda i: (split, jnp.maximum(i * (tm // HALO) - 1, 0), 0))


def _fwd_mix(proj, cw0, cw1, cw2, w_co, w_pool, tm):
    T = proj.shape[1]

    def body(b_ref, c_ref, ua_ref, up_ref, ch_ref, uah_ref, uph_ref, cw0_ref, cw1_ref, cw2_ref, wco_ref, wp_ref,
             za_ref, conv_ref, pooled_ref, ya_ref, yp_ref):
        i = pl.program_id(0)
        keep = jnp.where(i > 0, 1.0, 0.0).astype(F32)
        cu = c_ref[...].astype(F32) * ua_ref[...].astype(F32)
        cu_h = ch_ref[...].astype(F32) * uah_ref[...].astype(F32) * keep
        ext = jnp.concatenate([cu_h, cu], axis=0)
        conv = (cw2_ref[...] * ext + cw1_ref[...] * _shift_down(ext, 1) + cw0_ref[...] * _shift_down(ext, 2))[HALO:]
        za = (b_ref[...].astype(F32) * conv).astype(BF16)
        conv_ref[...] = conv.astype(BF16)
        za_ref[...] = za
        ya_ref[...] = _mm(za, wco_ref[...]).astype(BF16)

        up = up_ref[...].astype(F32)
        ext_u = jnp.concatenate([uph_ref[...].astype(F32) * keep, up], axis=0)
        pos = i * tm + lax.broadcasted_iota(jnp.int32, (tm, HD), 0)
        for g in range(NPOOL):
            cols = slice(g * HD, (g + 1) * HD)
            s = ext_u[:, cols]
            for k in range(g + 1):
                s = s + _shift_down(s, 1 << k)
            cnt = jnp.minimum(pos + 1, 2 << g).astype(F32)
            pooled = (s[HALO:] / cnt - up[:, cols]).astype(BF16)
            pooled_ref[:, cols] = pooled
            yp_ref[:, cols] = _mm(pooled, wp_ref[g]).astype(BF16)

    tile = lambda s: pl.BlockSpec((None, tm, D), lambda i: (s, i, 0))
    row = pl.BlockSpec((1, D), lambda i: (0, 0))
    out = pl.BlockSpec((tm, D), lambda i: (i, 0))
    return pl.pallas_call(
        body, name="fwd_mix", grid=(T // tm,),
        in_specs=[tile(0), tile(1), tile(2), tile(3), _halo_before(1, tm), _halo_before(2, tm), _halo_before(3, tm),
                  row, row, row, pl.BlockSpec((D, D), lambda i: (0, 0)), pl.BlockSpec((NPOOL, HD, HD), lambda i: (0, 0, 0))],
        out_specs=[out] * 5,
        out_shape=[SDS((T, D), BF16)] * 5,
        compiler_params=_params(1))(proj, proj, proj, proj, proj, proj, proj, cw0, cw1, cw2, w_co, w_pool)


def _softmax_rows(s):
    e = jnp.exp(s - jnp.max(s, axis=-1, keepdims=True))
    return e / jnp.sum(e, axis=-1, keepdims=True)


def _fwd_merge(proj, ya, yp, x, kv, w_xo, w_o, pscale, gain_ffn, tm):
    T = x.shape[0]

    def body(q_ref, ga_ref, gp_ref, gx_ref, ya_ref, yp_ref, x_ref, kv_ref, wxo_ref, wo_ref, ps_ref, gf_ref,
             o_ref, yx_ref, merged_ref, x1_ref, h2_ref):
        for h in range(NH):
            cols = slice(h * HD, (h + 1) * HD)
            p = _softmax_rows(_mm_nt(q_ref[:, cols], kv_ref[h]) * ATT_SCALE)
            o_ref[:, cols] = _mm(p.astype(BF16), kv_ref[NH + h]).astype(BF16)
        yx = _mm(o_ref[...], wxo_ref[...])
        yx_ref[...] = yx.astype(BF16)
        merged = (_sigmoid(ga_ref[...].astype(F32)) * ya_ref[...].astype(F32)
                  + _sigmoid(gp_ref[...].astype(F32)) * (yp_ref[...].astype(F32) * ps_ref[...])
                  + _sigmoid(gx_ref[...].astype(F32)) * yx).astype(BF16)
        merged_ref[...] = merged
        x1 = x_ref[...] + _mm(merged, wo_ref[...])
        x1_ref[...] = x1
        h2_ref[...] = (x1 * _rms(x1) * gf_ref[...]).astype(BF16)

    tile = lambda s: pl.BlockSpec((None, tm, D), lambda i: (s, i, 0))
    row = pl.BlockSpec((1, D), lambda i: (0, 0))
    act = pl.BlockSpec((tm, D), lambda i: (i, 0))
    full = pl.BlockSpec((D, D), lambda i: (0, 0))
    return pl.pallas_call(
        body, name="fwd_merge", grid=(T // tm,),
        in_specs=[tile(4), tile(5), tile(6), tile(7), act, act, act,
                  pl.BlockSpec((2 * NH, kv.shape[1], HD), lambda i: (0, 0, 0)), full, full, row, row],
        out_specs=[act] * 5,
        out_shape=[SDS((T, D), BF16), SDS((T, D), BF16), SDS((T, D), BF16), SDS((T, D), F32), SDS((T, D), BF16)],
        compiler_params=_params(1))(proj, proj, proj, proj, ya, yp, x, kv, w_xo, w_o, pscale, gain_ffn)


def _fwd_ffn_up(h2, wg_t, wu_t, tm, tn):
    T = h2.shape[0]

    def body(h_ref, wg_ref, wu_ref, gate_ref, up_ref, act_ref):
        gate = _mm_nt(h_ref[...], wg_ref[...])
        up = _mm_nt(h_ref[...], wu_ref[...])
        gate_ref[...] = gate.astype(BF16)
        up_ref[...] = up.astype(BF16)
        act_ref[...] = (gate * _sigmoid(gate) * up).astype(BF16)

    w = pl.BlockSpec((tn, D), lambda i, n: (n, 0))
    o = pl.BlockSpec((tm, tn), lambda i, n: (i, n))
    return pl.pallas_call(
        body, name="fwd_ffn_up", grid=(T // tm, DFF // tn),
        in_specs=[pl.BlockSpec((tm, D), lambda i, n: (i, 0)), w, w],
        out_specs=[o] * 3, out_shape=[SDS((T, DFF), BF16)] * 3,
        compiler_params=_params(2))(h2, wg_t, wu_t)


def _fwd_ffn_down_loss(act, w_d, x1, target, gain_final, tm):
    T = x1.shape[0]

    def body(act_ref, wd_ref, x1_ref, tgt_ref, g_ref, dx2_ref, loss_ref, dgain_ref):
        @pl.when(pl.program_id(0) == 0)
        def _():
            loss_ref[...] = jnp.zeros_like(loss_ref)
            dgain_ref[...] = jnp.zeros_like(dgain_ref)
        x2 = x1_ref[...] + _mm(act_ref[...], wd_ref[...])
        gain = g_ref[...]
        y = x2 * _rms(x2) * gain
        err = y - tgt_ref[...]
        loss_ref[...] += 0.5 * jnp.sum(jnp.mean(err * err, axis=-1, keepdims=True))
        dx2, dgain = _norm_bwd(err * (1.0 / D), x2, gain)
        dx2_ref[...] = dx2
        dgain_ref[...] += dgain

    act_spec = pl.BlockSpec((tm, D), lambda i: (i, 0))
    row = pl.BlockSpec((1, D), lambda i: (0, 0))
    return pl.pallas_call(
        body, name="fwd_ffn_down_loss", grid=(T // tm,),
        in_specs=[pl.BlockSpec((tm, DFF), lambda i: (i, 0)), pl.BlockSpec((DFF, D), lambda i: (0, 0)), act_spec, act_spec, row],
        out_specs=[act_spec, pl.BlockSpec((8, 128), lambda i: (0, 0)), row],
        out_shape=[SDS((T, D), F32), SDS((8, 128), F32), SDS((1, D), F32)],
        compiler_params=_params(1))(act, w_d, x1, target, gain_final)


def _bwd_ffn_down(dx2, w_d, gate, up, tm, tn):
    T = dx2.shape[0]

    def body(dx_ref, wd_ref, gate_ref, up_ref, dgate_ref, dup_ref):
        dact = _mm_nt(dx_ref[...].astype(BF16), wd_ref[...])
        gate = gate_ref[...].astype(F32)
        sg = _sigmoid(gate)
        dgate_ref[...] = (dact * up_ref[...].astype(F32) * (sg * (1.0 + gate * (1.0 - sg)))).astype(BF16)
        dup_ref[...] = (dact * gate * sg).astype(BF16)

    o = pl.BlockSpec((tm, tn), lambda i, n: (i, n))
    return pl.pallas_call(
        body, name="bwd_ffn_down", grid=(T // tm, DFF // tn),
        in_specs=[pl.BlockSpec((tm, D), lambda i, n: (i, 0)), pl.BlockSpec((tn, D), lambda i, n: (n, 0)), o, o],
        out_specs=[o] * 2, out_shape=[SDS((T, DFF), BF16)] * 2,
        compiler_params=_params(2))(dx2, w_d, gate, up)


def _bwd_ffn_up(dgate, dup, wg_t, wu_t, x1, dx2, gain_ffn, tm):
    T = x1.shape[0]

    def body(dg_ref, du_ref, wg_ref, wu_ref, x1_ref, dx2_ref, g_ref, dx1_ref, dgain_ref):
        @pl.when(pl.program_id(0) == 0)
        def _():
            dgain_ref[...] = jnp.zeros_like(dgain_ref)
        dh2 = _mm(dg_ref[...], wg_ref[...]) + _mm(du_ref[...], wu_ref[...])
        dx, dgain = _norm_bwd(dh2, x1_ref[...], g_ref[...])
        dx1_ref[...] = dx2_ref[...] + dx
        dgain_ref[...] += dgain

    wide = pl.BlockSpec((tm, DFF), lambda i: (i, 0))
    w = pl.BlockSpec((DFF, D), lambda i: (0, 0))
    act = pl.BlockSpec((tm, D), lambda i: (i, 0))
    row = pl.BlockSpec((1, D), lambda i: (0, 0))
    return pl.pallas_call(
        body, name="bwd_ffn_up", grid=(T // tm,),
        in_specs=[wide, wide, w, w, act, act, row], out_specs=[act, row],
        out_shape=[SDS((T, D), F32), SDS((1, D), F32)],
        compiler_params=_params(1))(dgate, dup, wg_t, wu_t, x1, dx2, gain_ffn)


def _wgrad(a, b, *, name, groups, a_cols, b_cols, tt, a_index, b_index, o_index, out_shape):
    T = a.shape[0]
    nt = T // tt
    n_a = a.shape[1] // a_cols if groups == 1 else 1

    def body(a_ref, b_ref, o_ref, acc_ref):
        t = pl.program_id(2)

        @pl.when(t == 0)
        def _():
            acc_ref[...] = jnp.zeros_like(acc_ref)
        acc_ref[...] += _mm_tn(a_ref[...].astype(BF16), b_ref[...].astype(BF16))

        @pl.when(t == nt - 1)
        def _():
            o_ref[...] = acc_ref[...].astype(o_ref.dtype)

    return pl.pallas_call(
        body, name=name, grid=(groups, n_a, nt),
        in_specs=[pl.BlockSpec((tt, a_cols), a_index), pl.BlockSpec((None, tt, b_cols), b_index)],
        out_specs=pl.BlockSpec((None, a_cols, b_cols), o_index),
        out_shape=SDS(out_shape, BF16),
        scratch_shapes=[pltpu.VMEM((a_cols, b_cols), F32)],
        compiler_params=_params(3))(a, b)


def _wgrad_dense(a, b, name, tt, a_cols=None):
    ka, nb = a.shape[1], b.shape[1]
    a_cols = ka if a_cols is None else a_cols
    out = _wgrad(a, b[None], name=name, groups=1, a_cols=a_cols, b_cols=nb, tt=tt,
                 a_index=lambda g, k, t: (t, k), b_index=lambda g, k, t: (0, t, 0),
                 o_index=lambda g, k, t: (k, 0, 0), out_shape=(ka // a_cols, a_cols, nb))
    return out.reshape(ka, nb)


def _bwd_merge(dx1, proj, ya, yp, yx, pscale, w_o, w_co, w_xo, w_pool, tm):
    T = dx1.shape[0]

    def body(dx1_ref, ga_ref, gp_ref, gx_ref, ya_ref, yp_ref, yx_ref, ps_ref, wo_ref, wco_ref, wxo_ref, wp_ref,
             dgates_ref, dya_ref, dyx_ref, dyps_ref, dza_ref, do_ref, dpooled_ref, dps_ref):
        @pl.when(pl.program_id(0) == 0)
        def _():
            dps_ref[...] = jnp.zeros_like(dps_ref)
        dmerged = _mm_nt(dx1_ref[...].astype(BF16), wo_ref[...])
        scale = ps_ref[...]
        sa, sp, sx = (_sigmoid(r[...].astype(F32)) for r in (ga_ref, gp_ref, gx_ref))
        ya, yp_pre, yx = (r[...].astype(F32) for r in (ya_ref, yp_ref, yx_ref))
        dgates_ref[0] = (dmerged * ya * sa * (1.0 - sa)).astype(BF16)
        dgates_ref[1] = (dmerged * (yp_pre * scale) * sp * (1.0 - sp)).astype(BF16)
        dgates_ref[2] = (dmerged * yx * sx * (1.0 - sx)).astype(BF16)
        dya = (dmerged * sa).astype(BF16)
        dyx = (dmerged * sx).astype(BF16)
        dyp = dmerged * sp
        dyps = (dyp * scale).astype(BF16)
        dps_ref[...] += jnp.sum(dyp * yp_pre, axis=0, keepdims=True)
        dya_ref[...] = dya
        dyx_ref[...] = dyx
        dyps_ref[...] = dyps
        dza_ref[...] = _mm_nt(dya, wco_ref[...]).astype(BF16)
        do_ref[...] = _mm_nt(dyx, wxo_ref[...]).astype(BF16)
        for g in range(NPOOL):
            cols = slice(g * HD, (g + 1) * HD)
            dpooled_ref[:, cols] = _mm_nt(dyps[:, cols], wp_ref[g]).astype(BF16)

    tile = lambda s: pl.BlockSpec((None, tm, D), lambda i: (s, i, 0))
    row = pl.BlockSpec((1, D), lambda i: (0, 0))
    act = pl.BlockSpec((tm, D), lambda i: (i, 0))
    full = pl.BlockSpec((D, D), lambda i: (0, 0))
    return pl.pallas_call(
        body, name="bwd_merge", grid=(T // tm,),
        in_specs=[act, tile(5), tile(6), tile(7), act, act, act, row, full, full, full,
                  pl.BlockSpec((NPOOL, HD, HD), lambda i: (0, 0, 0))],
        out_specs=[pl.BlockSpec((3, tm, D), lambda i: (0, i, 0))] + [act] * 6 + [row],
        out_shape=[SDS((NSPLIT, T, D), BF16)] + [SDS((T, D), BF16)] * 6 + [SDS((1, D), F32)],
        compiler_params=_params(1))(dx1, proj, proj, proj, ya, yp, yx, pscale, w_o, w_co, w_xo, w_pool)


def _bwd_attn(dproj, proj, do, kv, tm):
    T = do.shape[0]
    M = kv.shape[1]

    def body(dproj_hbm, q_ref, do_ref, kv_ref, dq_ref, dkv_ref):
        del dproj_hbm

        @pl.when(pl.program_id(0) == 0)
        def _():
            dkv_ref[...] = jnp.zeros_like(dkv_ref)
        for h in range(NH):
            cols = slice(h * HD, (h + 1) * HD)
            q = q_ref[:, cols]
            do_h = do_ref[:, cols]
            p = _softmax_rows(_mm_nt(q, kv_ref[h]) * ATT_SCALE)
            dp = _mm_nt(do_h, kv_ref[NH + h])
            ds = (p * (dp - jnp.sum(dp * p, axis=-1, keepdims=True)) * ATT_SCALE).astype(BF16)
            dq_ref[:, cols] = _mm(ds, kv_ref[h]).astype(BF16)
            dkv_ref[h] += _mm_tn(ds, q)
            dkv_ref[NH + h] += _mm_tn(p.astype(BF16), do_h)

    kv_spec = pl.BlockSpec((2 * NH, M, HD), lambda i: (0, 0, 0))
    return pl.pallas_call(
        body, name="bwd_attn", grid=(T // tm,),
        in_specs=[HBM, pl.BlockSpec((None, tm, D), lambda i: (4, i, 0)), pl.BlockSpec((tm, D), lambda i: (i, 0)), kv_spec],
        out_specs=[pl.BlockSpec((None, tm, D), lambda i: (3, i, 0)), kv_spec],
        out_shape=[SDS(dproj.shape, BF16), SDS((2 * NH, M, HD), F32)],
        input_output_aliases={0: 0},
        compiler_params=_params(1))(dproj, proj, do, kv)


def _bwd_mix(dproj, proj, conv, dza, dpooled, cw0, cw1, cw2, tm):
    T = dza.shape[0]
    nt = T // tm

    def halo_after(split_or_none):
        idx = lambda i: jnp.minimum((i + 1) * (tm // HALO), T // HALO - 1)
        if split_or_none is None:
            return pl.BlockSpec((HALO, D), lambda i: (idx(i), 0))
        return pl.BlockSpec((None, HALO, D), lambda i: (split_or_none, idx(i), 0))

    def body(dproj_hbm, b_ref, c_ref, ua_ref, conv_ref, dza_ref, dpo_ref, bn_ref, dzan_ref, dpon_ref, ch_ref, uah_ref,
             cw0_ref, cw1_ref, cw2_ref, dabcu_ref, dcw_ref):
        del dproj_hbm
        i = pl.program_id(0)

        @pl.when(i == 0)
        def _():
            dcw_ref[...] = jnp.zeros_like(dcw_ref)
        keep_prev = jnp.where(i > 0, 1.0, 0.0).astype(F32)
        keep_next = jnp.where(i < nt - 1, 1.0, 0.0).astype(F32)
        dza = dza_ref[...].astype(F32)
        c = c_ref[...].astype(F32)
        ua = ua_ref[...].astype(F32)
        dconv = dza * b_ref[...].astype(F32)
        dconv_n = dzan_ref[...].astype(F32) * bn_ref[...].astype(F32) * keep_next
        ext = jnp.concatenate([dconv, dconv_n], axis=0)
        dcu = (cw2_ref[...] * ext + cw1_ref[...] * _shift_up(ext, 1) + cw0_ref[...] * _shift_up(ext, 2))[:tm]
        dabcu_ref[0] = (dza * conv_ref[...].astype(F32)).astype(BF16)
        dabcu_ref[1] = (dcu * ua).astype(BF16)
        dabcu_ref[2] = (dcu * c).astype(BF16)

        cu = c * ua
        ext_cu = jnp.concatenate([ch_ref[...].astype(F32) * uah_ref[...].astype(F32) * keep_prev, cu], axis=0)
        dcw_ref[2:3, :] += jnp.sum(dconv * cu, axis=0, keepdims=True)
        dcw_ref[1:2, :] += jnp.sum(dconv * _shift_down(ext_cu, 1)[HALO:], axis=0, keepdims=True)
        dcw_ref[0:1, :] += jnp.sum(dconv * _shift_down(ext_cu, 2)[HALO:], axis=0, keepdims=True)

        dpo = dpo_ref[...].astype(F32)
        ext_dpo = jnp.concatenate([dpo, dpon_ref[...].astype(F32) * keep_next], axis=0)
        pos = i * tm + lax.broadcasted_iota(jnp.int32, (tm + HALO, HD), 0)
        for g in range(NPOOL):
            cols = slice(g * HD, (g + 1) * HD)
            s = ext_dpo[:, cols] / jnp.minimum(pos + 1, 2 << g).astype(F32)
            for k in range(g + 1):
                s = s + _shift_up(s, 1 << k)
            dabcu_ref[3, :, cols] = (s[:tm] - dpo[:, cols]).astype(BF16)

    tile = lambda s: pl.BlockSpec((None, tm, D), lambda i: (s, i, 0))
    act = pl.BlockSpec((tm, D), lambda i: (i, 0))
    row = pl.BlockSpec((1, D), lambda i: (0, 0))
    return pl.pallas_call(
        body, name="bwd_mix", grid=(nt,),
        in_specs=[HBM, tile(0), tile(1), tile(2), act, act, act, halo_after(0), halo_after(None), halo_after(None),
                  _halo_before(1, tm), _halo_before(2, tm), row, row, row],
        out_specs=[pl.BlockSpec((4, tm, D), lambda i: (1, i, 0)), pl.BlockSpec((8, D), lambda i: (0, 0))],
        out_shape=[SDS(dproj.shape, BF16), SDS((8, D), F32)],
        input_output_aliases={0: 0},
        compiler_params=_params(1))(dproj, proj, proj, proj, conv, dza, dpooled, proj, dza, dpooled, proj, proj, cw0, cw1, cw2)


def _bwd_proj(dproj, w_in_g, x, dx1, gain, tm):
    T = x.shape[0]

    def body(dp_ref, w_ref, x_ref, dx1_ref, g_ref, dx_ref, dgain_ref, acc_ref):
        i, s = pl.program_id(0), pl.program_id(1)

        @pl.when((i == 0) & (s == 0))
        def _():
            dgain_ref[...] = jnp.zeros_like(dgain_ref)

        @pl.when(s == 0)
        def _():
            acc_ref[...] = jnp.zeros_like(acc_ref)
        acc_ref[...] += _mm_nt(dp_ref[...], w_ref[...])

        @pl.when(s == NSPLIT - 1)
        def _():
            dx, dgain = _norm_bwd(acc_ref[...], x_ref[...], g_ref[...])
            dx_ref[...] = dx1_ref[...] + dx
            dgain_ref[...] += dgain

    act = pl.BlockSpec((tm, D), lambda i, s: (i, 0))
    row = pl.BlockSpec((1, D), lambda i, s: (0, 0))
    return pl.pallas_call(
        body, name="bwd_proj", grid=(T // tm, NSPLIT),
        in_specs=[pl.BlockSpec((None, tm, D), lambda i, s: (s, i, 0)),
                  pl.BlockSpec((None, D, D), lambda i, s: (_slot_group(s), 0, 0)), act, act, row],
        out_specs=[act, row], out_shape=[SDS((T, D), F32), SDS((1, D), F32)],
        scratch_shapes=[pltpu.VMEM((tm, D), F32)],
        compiler_params=_params(2))(dproj, w_in_g, x, dx1, gain)


def _bwd_kv(dkv, memn, w_kv_g, mem, gain):
    M = mem.shape[0]

    def body(dkv_ref, memn_ref, w_ref, mem_ref, g_ref, dw_ref, dgain_ref, acc_ref):
        j = pl.program_id(0)

        @pl.when(j == 0)
        def _():
            acc_ref[...] = jnp.zeros_like(acc_ref)
        dkv_j = dkv_ref[...].astype(BF16)
        dw_ref[...] = _mm_tn(memn_ref[...], dkv_j).astype(BF16)
        acc_ref[...] += _mm_nt(dkv_j, w_ref[...])

        @pl.when(j == 2 * NH - 1)
        def _():
            dgain_ref[...] = _norm_bwd(acc_ref[...], mem_ref[...], g_ref[...])[1]

    row = pl.BlockSpec((1, D), lambda j: (0, 0))
    return pl.pallas_call(
        body, name="bwd_kv", grid=(2 * NH,),
        in_specs=[pl.BlockSpec((None, M, HD), lambda j: (j, 0, 0)), pl.BlockSpec((M, D), lambda j: (0, 0)),
                  pl.BlockSpec((None, D, HD), lambda j: (j, 0, 0)), pl.BlockSpec((M, D), lambda j: (0, 0)), row],
        out_specs=[pl.BlockSpec((None, D, HD), lambda j: (j, 0, 0)), row],
        out_shape=[SDS((2 * NH, D, HD), BF16), SDS((1, D), F32)],
        scratch_shapes=[pltpu.VMEM((M, D), F32)],
        compiler_params=_params(1))(dkv, memn, w_kv_g, mem, gain)


def _adamw_math(w, g, m, v):
    m = ADAM_B1 * m + (1.0 - ADAM_B1) * g
    v = ADAM_B2 * v + (1.0 - ADAM_B2) * (g * g)
    m_hat = m / (1.0 - ADAM_B1 ** ADAM_STEP)
    v_hat = v / (1.0 - ADAM_B2 ** ADAM_STEP)
    delta = -ADAM_LR * (m_hat / (jnp.sqrt(v_hat) + ADAM_EPS) + ADAM_WD * w)
    return delta, m, v


def _row_tile(rows):
    return 256 if rows % 256 == 0 else rows


def _sum_parts(parts, name):
    _, rows, cols = parts.shape
    tr = _row_tile(rows)

    def body(p_ref, g_ref):
        g = p_ref[0].astype(F32)
        for k in range(1, NDEV):
            g = g + p_ref[k].astype(F32)
        g_ref[...] = g

    blk = pl.BlockSpec((tr, cols), lambda i: (i, 0))
    return pl.pallas_call(
        body, name=name, grid=(rows // tr,),
        in_specs=[pl.BlockSpec((NDEV, tr, cols), lambda i: (0, i, 0))], out_specs=blk,
        out_shape=SDS((rows, cols), F32), compiler_params=_params(1))(parts)


def _adamw(w, g, m, v, name, from_parts):
    rows, cols = w.shape
    tr = _row_tile(rows)

    def body(w_ref, g_ref, m_ref, v_ref, go_ref, d_ref, mo_ref, vo_ref):
        if from_parts:
            g = g_ref[0].astype(F32)
            for k in range(1, NDEV):
                g = g + g_ref[k].astype(F32)
        else:
            g = g_ref[...]
        go_ref[...] = g
        d_ref[...], mo_ref[...], vo_ref[...] = _adamw_math(w_ref[...], g, m_ref[...], v_ref[...])

    blk = pl.BlockSpec((tr, cols), lambda i: (i, 0))
    g_spec = pl.BlockSpec((NDEV, tr, cols), lambda i: (0, i, 0)) if from_parts else blk
    return pl.pallas_call(
        body, name=name, grid=(rows // tr,),
        in_specs=[blk, g_spec, blk, blk], out_specs=[blk] * 4,
        out_shape=[SDS((rows, cols), F32)] * 4, compiler_params=_params(1))(w, g, m, v)


def _peer(k, x, y, c):
    return ((1 - x) if k & 4 else x, (1 - y) if k & 2 else y, (1 - c) if k & 1 else c)


def _exchange(arrays, name, scatter):
    n = len(arrays)

    def body(*refs):
        ins, outs = refs[:n], refs[n:2 * n]
        send_sems, recv_sems, local_sems = refs[2 * n:]
        x, y, c = (lax.axis_index(a) for a in AXES)
        me = 4 * x + 2 * y + c

        def remote(a, k):
            px, py, pc = _peer(k, x, y, c)
            there = 4 * px + 2 * py + pc
            return pltpu.make_async_remote_copy(
                src_ref=ins[a].at[there] if scatter else ins[a], dst_ref=outs[a].at[me],
                send_sem=send_sems.at[a, k - 1], recv_sem=recv_sems.at[a, k - 1],
                device_id=(px, py, pc), device_id_type=pl.DeviceIdType.MESH)

        def arrival(a, k):
            px, py, pc = _peer(k, x, y, c)
            there = 4 * px + 2 * py + pc
            return pltpu.make_async_remote_copy(
                src_ref=ins[a].at[there] if scatter else ins[a], dst_ref=outs[a].at[there],
                send_sem=send_sems.at[a, k - 1], recv_sem=recv_sems.at[a, k - 1],
                device_id=(px, py, pc), device_id_type=pl.DeviceIdType.MESH)

        own = [pltpu.make_async_copy(ins[a].at[me] if scatter else ins[a], outs[a].at[me], local_sems.at[a]) for a in range(n)]
        for a in range(n):
            own[a].start()
            for k in range(1, NDEV):
                remote(a, k).start()
        for a in range(n):
            for k in range(1, NDEV):
                arrival(a, k).wait_recv()
        for a in range(n):
            for k in range(1, NDEV):
                remote(a, k).wait_send()
            own[a].wait()

    out_shape = [SDS(a.shape if scatter else (NDEV,) + a.shape, a.dtype) for a in arrays]
    return pl.pallas_call(
        body, name=name, in_specs=[HBM] * n, out_specs=[HBM] * n, out_shape=out_shape,
        scratch_shapes=[pltpu.SemaphoreType.DMA((n, NDEV - 1)), pltpu.SemaphoreType.DMA((n, NDEV - 1)),
                        pltpu.SemaphoreType.DMA((n,))],
        compiler_params=pltpu.CompilerParams(has_side_effects=True))(*arrays)


SEM = pl.BlockSpec(memory_space=pltpu.SEMAPHORE)
IN_HBM = pl.BlockSpec(memory_space=pltpu.HBM)
DATAFLOW = pltpu.SideEffectType.DATAFLOW_SIDE_EFFECTING
TOKEN_SHAPE = (8, 128)


def _split_copy(ins, lnd, send_sems, recv_sems, a, slot, k, scatter, arriving):
    x, y, c = (lax.axis_index(n) for n in AXES)
    me = 4 * x + 2 * y + c
    px, py, pc = _peer(k, x, y, c)
    there = 4 * px + 2 * py + pc
    return pltpu.make_async_remote_copy(
        src_ref=ins[a].at[there] if scatter else ins[a], dst_ref=lnd[a].at[there if arriving else me],
        send_sem=send_sems.at[slot], recv_sem=recv_sems.at[slot],
        device_id=(px, py, pc), device_id_type=pl.DeviceIdType.MESH)


def _exchange_start(srcs, lands, groups, name, scatter, after):
    n, ng = len(srcs), len(groups)

    def body(*refs):
        ins, lnd = refs[1:1 + n], refs[1 + n:1 + 2 * n]
        sems = refs[1 + 2 * n:1 + 2 * n + 2 * ng]
        token_ref = refs[-1]
        for gi, grp in enumerate(groups):
            for ai, a in enumerate(grp):
                for k in range(1, NDEV):
                    _split_copy(ins, lnd, sems[2 * gi], sems[2 * gi + 1], a, ai * (NDEV - 1) + k - 1, k, scatter, False).start()
        token_ref[...] = jnp.zeros(TOKEN_SHAPE, F32)

    sem_shapes = [pltpu.SemaphoreType.DMA((len(grp) * (NDEV - 1),)) for grp in groups for _ in range(2)]
    bufs = list(srcs) + list(lands)
    outs = pl.pallas_call(
        body, name=name,
        in_specs=[HBM] + [IN_HBM] * (2 * n),
        out_specs=[SEM] * (2 * ng) + [IN_HBM] * (2 * n) + [pl.BlockSpec(memory_space=pltpu.VMEM)],
        out_shape=sem_shapes + [pltpu.HBM(b.shape, b.dtype) for b in bufs] + [SDS(TOKEN_SHAPE, F32)],
        input_output_aliases={1 + i: 2 * ng + i for i in range(2 * n)},
        compiler_params=pltpu.CompilerParams(has_side_effects=DATAFLOW),
    )(after, *[pltpu.with_memory_space_constraint(b, pltpu.HBM) for b in bufs])
    sems = [(outs[2 * gi], outs[2 * gi + 1]) for gi in range(ng)]
    thru = outs[2 * ng:2 * ng + 2 * n]
    return sems, thru[:n], thru[n:], outs[-1]


def _exchange_wait(sems, srcs, lands, name, scatter, after):
    n = len(srcs)

    def body(*refs):
        ins, lnd = refs[:n], refs[n:2 * n]
        send_sems, recv_sems = refs[2 * n], refs[2 * n + 1]
        for a in range(n):
            for k in range(1, NDEV):
                copy = _split_copy(ins, lnd, send_sems, recv_sems, a, a * (NDEV - 1) + k - 1, k, scatter, True)
                copy.wait_send()
                copy.wait_recv()

    bufs = list(srcs) + list(lands)
    outs = pl.pallas_call(
        body, name=name,
        in_specs=[IN_HBM] * (2 * n) + [SEM, SEM, HBM], out_specs=[IN_HBM] * (2 * n),
        out_shape=[pltpu.HBM(b.shape, b.dtype) for b in bufs],
        input_output_aliases={i: i for i in range(2 * n)},
        compiler_params=pltpu.CompilerParams(has_side_effects=DATAFLOW),
    )(*bufs, sems[0], sems[1], after)
    return outs[n:]


def _local_step(x, mem, target, gains, get, put, tm_big=1024, tm_mid=512, tm_small=256):
    g_mix, pscale, g_mem, g_ffn, g_fin = gains
    T = x.shape[0]
    tm_big, tm_mid, tm_small = min(tm_big, T), min(tm_mid, T), min(tm_small, T)
    tn = DFF // 2

    w_in = get("in", x)
    proj, h = _fwd_proj(x, g_mix, w_in, tm_big)
    cw0, cw1, cw2, w_co, w_pool, w_kv = get("mix", proj)
    kv, memn = _fwd_kv(mem, g_mem, w_kv)
    za, conv, pooled, ya, yp = _fwd_mix(proj, cw0, cw1, cw2, w_co, w_pool, tm_small)
    w_xo, w_o = get("merge", ya)
    o, yx, merged, x1, h2 = _fwd_merge(proj, ya, yp, x, kv, w_xo, w_o, pscale, g_ffn, tm_small)
    wg_t, wu_t, w_d = get("ffn", x1)
    gate, up, act = _fwd_ffn_up(h2, wg_t, wu_t, tm_mid, tn)
    dx2, loss, dg_fin = _fwd_ffn_down_loss(act, w_d, x1, target, g_fin, tm_small)

    dgate, dup = _bwd_ffn_down(dx2, w_d, gate, up, tm_mid, tn)
    dx1, dg_ffn = _bwd_ffn_up(dgate, dup, wg_t, wu_t, x1, dx2, g_ffn, tm_small)
    dw_d = _wgrad_dense(act, dx2, "wgrad_down", tm_mid, a_cols=tn)
    dwg_t = _wgrad_dense(dgate, h2, "wgrad_gate", tm_mid, a_cols=tn)
    dwu_t = _wgrad_dense(dup, h2, "wgrad_up", tm_mid, a_cols=tn)
    zero = put("ffn", (dwg_t, dwu_t, dw_d))

    dproj, dya, dyx, dyps, dza, do, dpooled, dpscale = _bwd_merge(
        dx1, proj, ya, yp, yx, pscale + zero, w_o, w_co, w_xo, w_pool, tm_small)
    dw_o = _wgrad_dense(merged, dx1, "wgrad_out", tm_mid)
    dw_co = _wgrad_dense(za, dya, "wgrad_conv_out", tm_mid)
    dw_xo = _wgrad_dense(o, dyx, "wgrad_xattn_out", tm_mid)
    dw_pool = _wgrad(pooled, dyps[None], name="wgrad_pool", groups=NPOOL, a_cols=HD, b_cols=HD, tt=tm_mid,
                     a_index=lambda g, k, t: (t, g), b_index=lambda g, k, t: (0, t, g),
                     o_index=lambda g, k, t: (g, 0, 0), out_shape=(NPOOL, HD, HD))
    dproj, dkv = _bwd_attn(dproj, proj, do, kv, tm_small)
    dw_kv, dg_mem = _bwd_kv(dkv, memn, w_kv, mem, g_mem)
    zero = put("mix", (dw_co, dw_xo, dw_o, dw_pool, dw_kv))

    dproj, dcw = _bwd_mix(dproj, proj, conv, dza, dpooled, cw0 + zero, cw1, cw2, tm_small)
    dw_in = _wgrad(h, dproj, name="wgrad_in", groups=NSPLIT, a_cols=D, b_cols=D, tt=tm_mid,
                   a_index=lambda g, k, t: (t, 0), b_index=lambda g, k, t: (g, t, 0),
                   o_index=lambda g, k, t: (_slot_group(g), 0, 0), out_shape=(NSPLIT, D, D))
    zero = put("in", (dw_in,))
    grad_x, dg_mix = _bwd_proj(dproj, w_in, x, dx1, g_mix + zero, tm_mid)

    small = jnp.concatenate([dg_mix, dpscale, dg_mem, dg_ffn, dg_fin, dcw[0:3]], axis=0)
    return loss[0, 0], grad_x, small


def kernel(x, mem, norm_mix, w_in, conv_w, w_conv_out, w_pool, pool_scale, norm_mem, w_kv, w_xattn_out, w_out, norm_ffn, w_gate, w_up, w_down, norm_final, loss_target, m_norm_mix, m_w_in, m_conv_w, m_w_conv_out, m_w_pool, m_pool_scale, m_norm_mem, m_w_kv, m_w_xattn_out, m_w_out, m_norm_ffn, m_w_gate, m_w_up, m_w_down, m_norm_final, v_norm_mix, v_w_in, v_conv_w, v_w_conv_out, v_w_pool, v_pool_scale, v_norm_mem, v_w_kv, v_w_xattn_out, v_w_out, v_norm_ffn, v_w_gate, v_w_up, v_w_down, v_norm_final):
    T = x.shape[1]
    rows = D // NDEV
    ffb = DFF // NDEV
    prow = HD // NDEV
    me = 4 * lax.axis_index("x") + 2 * lax.axis_index("y") + lax.axis_index("c")

    shards = [w_in[0].astype(BF16), w_conv_out[0].astype(BF16), w_xattn_out[0].astype(BF16), w_out[0].astype(BF16),
              w_pool[0].astype(BF16).reshape(NPOOL * prow, HD), w_kv[0].astype(BF16),
              w_gate[0].T.astype(BF16), w_up[0].T.astype(BF16), w_down[0].astype(BF16),
              jnp.pad(conv_w[0], ((0, 5), (0, 0)))]

    def land(own):
        return lax.dynamic_update_index_in_dim(lax.empty((NDEV,) + own.shape, own.dtype), own, me, 0)

    needed = ["in", "mix", "merge", "ffn"]
    members = {"in": [0], "mix": [9, 1, 4, 5], "merge": [2, 3], "ffn": [6, 7, 8]}
    g_sems, g_srcs, g_lands, _ = _exchange_start(shards, [land(s) for s in shards], [members[n] for n in needed],
                                                 "gather_start", False, x)

    def get(group, after):
        idx = members[group]
        got = _exchange_wait(g_sems[needed.index(group)], [g_srcs[i] for i in idx], [g_lands[i] for i in idx],
                             "gather_wait_" + group, False, after)
        if group == "in":
            return got[0]
        if group == "mix":
            cw_g, w_co_g, w_pool_g, w_kv_g = got
            cw_full = cw_g.transpose(1, 0, 2).reshape(8, D)
            w_pool_full = w_pool_g.reshape(NDEV, NPOOL, prow, HD).transpose(1, 0, 2, 3).reshape(NPOOL, HD, HD)
            return cw_full[0:1], cw_full[1:2], cw_full[2:3], w_co_g.reshape(D, D), w_pool_full, w_kv_g
        if group == "merge":
            return got[0].reshape(D, D), got[1].reshape(D, D)
        return got[0].reshape(DFF, D), got[1].reshape(DFF, D), got[2].reshape(DFF, D)

    started = {}

    def put(group, grads):
        if group == "ffn":
            sends = [g.reshape(NDEV, ffb, D) for g in grads]
        elif group == "mix":
            dw_co, dw_xo, dw_o, dw_pool, dw_kv = grads
            sends = [dw_co.reshape(NDEV, rows, D), dw_xo.reshape(NDEV, rows, D), dw_o.reshape(NDEV, rows, D),
                     dw_pool.reshape(NPOOL, NDEV, prow, HD).transpose(1, 0, 2, 3).reshape(NDEV, NPOOL * prow, HD), dw_kv]
        else:
            sends = list(grads)
        lands = [land(lax.dynamic_index_in_dim(s, me, 0, keepdims=False)) for s in sends]
        sems, srcs, lands, token = _exchange_start(sends, lands, [list(range(len(sends)))], "scatter_start_" + group, True, sends[0])
        started[group] = (sems[0], srcs, lands)
        return token[0:1, 0:1]

    def take(group, after):
        return _exchange_wait(*started[group], "scatter_wait_" + group, True, after)

    gains = (norm_mix, pool_scale, norm_mem, norm_ffn, norm_final.reshape(1, D))
    loss_part, grad_x, small = _local_step(x[0], mem[0], loss_target[0], gains, get, put)

    (small_all,) = _exchange([small], "gather_small", scatter=False)
    small_sum = _sum_parts(small_all, "sum_small")

    loss = lax.psum(loss_part, AXES)

    def sharded(name, w, parts, m, v):
        shape = w.shape
        flat = lambda a: a.reshape(parts.shape[1], parts.shape[2])
        outs = _adamw(flat(w), parts, flat(m), flat(v), "adamw_" + name, from_parts=True)
        return [o.reshape(shape) for o in outs]

    def transposed(name, w, parts, m, v):
        g = _sum_parts(parts, "sum_" + name).T
        outs = _adamw(w[0], g, m[0], v[0], "adamw_" + name, from_parts=False)
        return [o[None] for o in outs]

    def replicated(name, w, g, m, v):
        shape = w.shape
        flat = lambda a: a.reshape(g.shape)
        outs = _adamw(flat(w), g, flat(m), flat(v), "adamw_" + name, from_parts=False)
        return [o.reshape(shape) for o in outs]

    g_cw = lax.dynamic_slice_in_dim(small_sum[5:8], me * rows, rows, axis=1)
    res = {
        "norm_mix": replicated("norm_mix", norm_mix, small_sum[0:1], m_norm_mix, v_norm_mix),
        "conv_w": replicated("conv_w", conv_w, g_cw, m_conv_w, v_conv_w),
        "pool_scale": replicated("pool_scale", pool_scale, small_sum[1:2], m_pool_scale, v_pool_scale),
        "norm_mem": replicated("norm_mem", norm_mem, small_sum[2:3], m_norm_mem, v_norm_mem),
        "norm_ffn": replicated("norm_ffn", norm_ffn, small_sum[3:4], m_norm_ffn, v_norm_ffn),
        "norm_final": replicated("norm_final", norm_final, small_sum[4:5], m_norm_final, v_norm_final),
    }
    p_g, p_u, p_d = take("ffn", res["norm_final"][1])
    res["w_gate"] = transposed("w_gate", w_gate, p_g, m_w_gate, v_w_gate)
    res["w_up"] = transposed("w_up", w_up, p_u, m_w_up, v_w_up)
    res["w_down"] = sharded("w_down", w_down, p_d, m_w_down, v_w_down)
    p_co, p_xo, p_o, p_pool, p_kv = take("mix", res["w_down"][1])
    res["w_conv_out"] = sharded("w_conv_out", w_conv_out, p_co, m_w_conv_out, v_w_conv_out)
    res["w_pool"] = sharded("w_pool", w_pool, p_pool, m_w_pool, v_w_pool)
    res["w_kv"] = sharded("w_kv", w_kv, p_kv, m_w_kv, v_w_kv)
    res["w_xattn_out"] = sharded("w_xattn_out", w_xattn_out, p_xo, m_w_xattn_out, v_w_xattn_out)
    res["w_out"] = sharded("w_out", w_out, p_o, m_w_out, v_w_out)
    (p_in,) = take("in", res["w_out"][1])
    res["w_in"] = sharded("w_in", w_in, p_in, m_w_in, v_w_in)
    order = ["norm_mix", "w_in", "conv_w", "w_conv_out", "w_pool", "pool_scale", "norm_mem", "w_kv", "w_xattn_out", "w_out",
             "norm_ffn", "w_gate", "w_up", "w_down", "norm_final"]
    return (loss, grad_x[None], *[res[n][0] for n in order], *[res[n][1] for n in order],
            *[res[n][2] for n in order], *[res[n][3] for n in order])
```

```python
import jax
import jax.numpy as jnp
from jax import lax
from jax.experimental import pallas as pl
from jax.experimental.pallas import tpu as pltpu

F32 = jnp.float32
BF16 = jnp.bfloat16
SDS = jax.ShapeDtypeStruct

AXES = ("x", "y", "c")
NDEV = 8
D = 1024
NSPLIT = 8
NH = 4
HD = D // NH
NPOOL = 4
DFF = 2816
EPS = 1e-6
ATT_SCALE = HD ** -0.5
HALO = 16


def _slot_group(s):
    return jnp.where(s < 3, s + 5, jnp.where(s == 3, 4, s - 4))


ADAM_LR = 0.001
ADAM_B1 = 0.9
ADAM_B2 = 0.999
ADAM_EPS = 1e-08
ADAM_WD = 0.01
ADAM_STEP = 10

V7X_VMEM_BYTES = 64 * 1024 * 1024
VMEM_LIMIT = V7X_VMEM_BYTES - 8 * 1024 * 1024
HBM = pl.BlockSpec(memory_space=pl.ANY)


def _params(n_grid):
    return pltpu.CompilerParams(dimension_semantics=("arbitrary",) * n_grid, vmem_limit_bytes=VMEM_LIMIT)


def _mm(a, b):
    return jnp.dot(a, b, preferred_element_type=F32)


def _mm_nt(a, b):
    return lax.dot_general(a, b, (((1,), (1,)), ((), ())), preferred_element_type=F32)


def _mm_tn(a, b):
    return lax.dot_general(a, b, (((0,), (0,)), ((), ())), preferred_element_type=F32)


def _sigmoid(x):
    return 1.0 / (1.0 + jnp.exp(-x))


def _rms(x):
    return lax.rsqrt(jnp.mean(x * x, axis=-1, keepdims=True) + EPS)


def _norm_bwd(dh, x, gain):
    r = _rms(x)
    xh = x * r
    dxh = dh * gain
    dx = r * (dxh - xh * jnp.mean(dxh * xh, axis=-1, keepdims=True))
    return dx, jnp.sum(dh * xh, axis=0, keepdims=True)


def _shift_down(v, k):
    return pltpu.roll(v, k, 0)


def _shift_up(v, k):
    return pltpu.roll(v, v.shape[0] - k, 0)


def _fwd_proj(x, gain, w_in_g, tm):
    T = x.shape[0]

    def body(x_ref, g_ref, w_ref, proj_ref, h_ref):
        @pl.when(pl.program_id(1) == 0)
        def _():
            xf = x_ref[...]
            h_ref[...] = (xf * _rms(xf) * g_ref[...]).astype(BF16)
        proj_ref[...] = _mm(h_ref[...], w_ref[...]).astype(BF16)

    return pl.pallas_call(
        body, name="fwd_proj", grid=(T // tm, NSPLIT),
        in_specs=[pl.BlockSpec((tm, D), lambda i, j: (i, 0)), pl.BlockSpec((1, D), lambda i, j: (0, 0)),
                  pl.BlockSpec((None, D, D), lambda i, j: (j, 0, 0))],
        out_specs=[pl.BlockSpec((None, tm, D), lambda i, j: (j, i, 0)), pl.BlockSpec((tm, D), lambda i, j: (i, 0))],
        out_shape=[SDS((NSPLIT, T, D), BF16), SDS((T, D), BF16)],
        compiler_params=_params(2))(x, gain, w_in_g)


def _fwd_kv(mem, gain, w_kv_g):
    M = mem.shape[0]

    def body(mem_ref, g_ref, w_ref, kv_ref, memn_ref):
        @pl.when(pl.program_id(0) == 0)
        def _():
            m = mem_ref[...]
            memn_ref[...] = (m * _rms(m) * g_ref[...]).astype(BF16)
        kv_ref[...] = _mm(memn_ref[...], w_ref[...]).astype(BF16)

    return pl.pallas_call(
        body, name="fwd_kv", grid=(2 * NH,),
        in_specs=[pl.BlockSpec((M, D), lambda j: (0, 0)), pl.BlockSpec((1, D), lambda j: (0, 0)),
                  pl.BlockSpec((None, D, HD), lambda j: (j, 0, 0))],
        out_specs=[pl.BlockSpec((None, M, HD), lambda j: (j, 0, 0)), pl.BlockSpec((M, D), lambda j: (0, 0))],
        out_shape=[SDS((2 * NH, M, HD), BF16), SDS((M, D), BF16)],
        compiler_params=_params(1))(mem, gain, w_kv_g)


def _halo_before(split, tm):
    return pl.BlockSpec((None, HALO, D), lambda i: (split, jnp.maximum(i * (tm // HALO) - 1, 0), 0))


def _fwd_mix(proj, cw0, cw1, cw2, w_co, w_pool, tm):
    T = proj.shape[1]

    def body(b_ref, c_ref, ua_ref, up_ref, ch_ref, uah_ref, uph_ref, cw0_ref, cw1_ref, cw2_ref, wco_ref, wp_ref,
             za_ref, conv_ref, pooled_ref, ya_ref, yp_ref):
        i = pl.program_id(0)
        keep = jnp.where(i > 0, 1.0, 0.0).astype(F32)
        cu = c_ref[...].astype(F32) * ua_ref[...].astype(F32)
        cu_h = ch_ref[...].astype(F32) * uah_ref[...].astype(F32) * keep
        ext = jnp.concatenate([cu_h, cu], axis=0)
        conv = (cw2_ref[...] * ext + cw1_ref[...] * _shift_down(ext, 1) + cw0_ref[...] * _shift_down(ext, 2))[HALO:]
        za = (b_ref[...].astype(F32) * conv).astype(BF16)
        conv_ref[...] = conv.astype(BF16)
        za_ref[...] = za
        ya_ref[...] = _mm(za, wco_ref[...]).astype(BF16)

        up = up_ref[...].astype(F32)
        ext_u = jnp.concatenate([uph_ref[...].astype(F32) * keep, up], axis=0)
        pos = i * tm + lax.broadcasted_iota(jnp.int32, (tm, HD), 0)
        for g in range(NPOOL):
            cols = slice(g * HD, (g + 1) * HD)
            s = ext_u[:, cols]
            for k in range(g + 1):
                s = s + _shift_down(s, 1 << k)
            cnt = jnp.minimum(pos + 1, 2 << g).astype(F32)
            pooled = (s[HALO:] / cnt - up[:, cols]).astype(BF16)
            pooled_ref[:, cols] = pooled
            yp_ref[:, cols] = _mm(pooled, wp_ref[g]).astype(BF16)

    tile = lambda s: pl.BlockSpec((None, tm, D), lambda i: (s, i, 0))
    row = pl.BlockSpec((1, D), lambda i: (0, 0))
    out = pl.BlockSpec((tm, D), lambda i: (i, 0))
    return pl.pallas_call(
        body, name="fwd_mix", grid=(T // tm,),
        in_specs=[tile(0), tile(1), tile(2), tile(3), _halo_before(1, tm), _halo_before(2, tm), _halo_before(3, tm),
                  row, row, row, pl.BlockSpec((D, D), lambda i: (0, 0)), pl.BlockSpec((NPOOL, HD, HD), lambda i: (0, 0, 0))],
        out_specs=[out] * 5,
        out_shape=[SDS((T, D), BF16)] * 5,
        compiler_params=_params(1))(proj, proj, proj, proj, proj, proj, proj, cw0, cw1, cw2, w_co, w_pool)


def _softmax_rows(s):
    e = jnp.exp(s - jnp.max(s, axis=-1, keepdims=True))
    return e / jnp.sum(e, axis=-1, keepdims=True)


def _fwd_merge(proj, ya, yp, x, kv, w_xo, w_o, pscale, gain_ffn, tm):
    T = x.shape[0]

    def body(q_ref, ga_ref, gp_ref, gx_ref, ya_ref, yp_ref, x_ref, kv_ref, wxo_ref, wo_ref, ps_ref, gf_ref,
             o_ref, yx_ref, merged_ref, x1_ref, h2_ref):
        for h in range(NH):
            cols = slice(h * HD, (h + 1) * HD)
            p = _softmax_rows(_mm_nt(q_ref[:, cols], kv_ref[h]) * ATT_SCALE)
            o_ref[:, cols] = _mm(p.astype(BF16), kv_ref[NH + h]).astype(BF16)
        yx = _mm(o_ref[...], wxo_ref[...])
        yx_ref[...] = yx.astype(BF16)
        merged = (_sigmoid(ga_ref[...].astype(F32)) * ya_ref[...].astype(F32)
                  + _sigmoid(gp_ref[...].astype(F32)) * (yp_ref[...].astype(F32) * ps_ref[...])
                  + _sigmoid(gx_ref[...].astype(F32)) * yx).astype(BF16)
        merged_ref[...] = merged
        x1 = x_ref[...] + _mm(merged, wo_ref[...])
        x1_ref[...] = x1
        h2_ref[...] = (x1 * _rms(x1) * gf_ref[...]).astype(BF16)

    tile = lambda s: pl.BlockSpec((None, tm, D), lambda i: (s, i, 0))
    row = pl.BlockSpec((1, D), lambda i: (0, 0))
    act = pl.BlockSpec((tm, D), lambda i: (i, 0))
    full = pl.BlockSpec((D, D), lambda i: (0, 0))
    return pl.pallas_call(
        body, name="fwd_merge", grid=(T // tm,),
        in_specs=[tile(4), tile(5), tile(6), tile(7), act, act, act,
                  pl.BlockSpec((2 * NH, kv.shape[1], HD), lambda i: (0, 0, 0)), full, full, row, row],
        out_specs=[act] * 5,
        out_shape=[SDS((T, D), BF16), SDS((T, D), BF16), SDS((T, D), BF16), SDS((T, D), F32), SDS((T, D), BF16)],
        compiler_params=_params(1))(proj, proj, proj, proj, ya, yp, x, kv, w_xo, w_o, pscale, gain_ffn)


def _fwd_ffn_up(h2, wg_t, wu_t, tm, tn):
    T = h2.shape[0]

    def body(h_ref, wg_ref, wu_ref, gate_ref, up_ref, act_ref):
        gate = _mm_nt(h_ref[...], wg_ref[...])
        up = _mm_nt(h_ref[...], wu_ref[...])
        gate_ref[...] = gate.astype(BF16)
        up_ref[...] = up.astype(BF16)
        act_ref[...] = (gate * _sigmoid(gate) * up).astype(BF16)

    w = pl.BlockSpec((tn, D), lambda i, n: (n, 0))
    o = pl.BlockSpec((tm, tn), lambda i, n: (i, n))
    return pl.pallas_call(
        body, name="fwd_ffn_up", grid=(T // tm, DFF // tn),
        in_specs=[pl.BlockSpec((tm, D), lambda i, n: (i, 0)), w, w],
        out_specs=[o] * 3, out_shape=[SDS((T, DFF), BF16)] * 3,
        compiler_params=_params(2))(h2, wg_t, wu_t)


def _fwd_ffn_down_loss(act, w_d, x1, target, gain_final, tm):
    T = x1.shape[0]

    def body(act_ref, wd_ref, x1_ref, tgt_ref, g_ref, dx2_ref, loss_ref, dgain_ref):
        @pl.when(pl.program_id(0) == 0)
        def _():
            loss_ref[...] = jnp.zeros_like(loss_ref)
            dgain_ref[...] = jnp.zeros_like(dgain_ref)
        x2 = x1_ref[...] + _mm(act_ref[...], wd_ref[...])
        gain = g_ref[...]
        y = x2 * _rms(x2) * gain
        err = y - tgt_ref[...]
        loss_ref[...] += 0.5 * jnp.sum(jnp.mean(err * err, axis=-1, keepdims=True))
        dx2, dgain = _norm_bwd(err * (1.0 / D), x2, gain)
        dx2_ref[...] = dx2
        dgain_ref[...] += dgain

    act_spec = pl.BlockSpec((tm, D), lambda i: (i, 0))
    row = pl.BlockSpec((1, D), lambda i: (0, 0))
    return pl.pallas_call(
        body, name="fwd_ffn_down_loss", grid=(T // tm,),
        in_specs=[pl.BlockSpec((tm, DFF), lambda i: (i, 0)), pl.BlockSpec((DFF, D), lambda i: (0, 0)), act_spec, act_spec, row],
        out_specs=[act_spec, pl.BlockSpec((8, 128), lambda i: (0, 0)), row],
        out_shape=[SDS((T, D), F32), SDS((8, 128), F32), SDS((1, D), F32)],
        compiler_params=_params(1))(act, w_d, x1, target, gain_final)


def _bwd_ffn_down(dx2, w_d, gate, up, tm, tn):
    T = dx2.shape[0]

    def body(dx_ref, wd_ref, gate_ref, up_ref, dgate_ref, dup_ref):
        dact = _mm_nt(dx_ref[...].astype(BF16), wd_ref[...])
        gate = gate_ref[...].astype(F32)
        sg = _sigmoid(gate)
        dgate_ref[...] = (dact * up_ref[...].astype(F32) * (sg * (1.0 + gate * (1.0 - sg)))).astype(BF16)
        dup_ref[...] = (dact * gate * sg).astype(BF16)

    o = pl.BlockSpec((tm, tn), lambda i, n: (i, n))
    return pl.pallas_call(
        body, name="bwd_ffn_down", grid=(T // tm, DFF // tn),
        in_specs=[pl.BlockSpec((tm, D), lambda i, n: (i, 0)), pl.BlockSpec((tn, D), lambda i, n: (n, 0)), o, o],
        out_specs=[o] * 2, out_shape=[SDS((T, DFF), BF16)] * 2,
        compiler_params=_params(2))(dx2, w_d, gate, up)


def _bwd_ffn_up(dgate, dup, wg_t, wu_t, x1, dx2, gain_ffn, tm):
    T = x1.shape[0]

    def body(dg_ref, du_ref, wg_ref, wu_ref, x1_ref, dx2_ref, g_ref, dx1_ref, dgain_ref):
        @pl.when(pl.program_id(0) == 0)
        def _():
            dgain_ref[...] = jnp.zeros_like(dgain_ref)
        dh2 = _mm(dg_ref[...], wg_ref[...]) + _mm(du_ref[...], wu_ref[...])
        dx, dgain = _norm_bwd(dh2, x1_ref[...], g_ref[...])
        dx1_ref[...] = dx2_ref[...] + dx
        dgain_ref[...] += dgain

    wide = pl.BlockSpec((tm, DFF), lambda i: (i, 0))
    w = pl.BlockSpec((DFF, D), lambda i: (0, 0))
    act = pl.BlockSpec((tm, D), lambda i: (i, 0))
    row = pl.BlockSpec((1, D), lambda i: (0, 0))
    return pl.pallas_call(
        body, name="bwd_ffn_up", grid=(T // tm,),
        in_specs=[wide, wide, w, w, act, act, row], out_specs=[act, row],
        out_shape=[SDS((T, D), F32), SDS((1, D), F32)],
        compiler_params=_params(1))(dgate, dup, wg_t, wu_t, x1, dx2, gain_ffn)


def _wgrad(a, b, *, name, groups, a_cols, b_cols, tt, a_index, b_index, o_index, out_shape):
    T = a.shape[0]
    nt = T // tt
    n_a = a.shape[1] // a_cols if groups == 1 else 1

    def body(a_ref, b_ref, o_ref, acc_ref):
        t = pl.program_id(2)

        @pl.when(t == 0)
        def _():
            acc_ref[...] = jnp.zeros_like(acc_ref)
        acc_ref[...] += _mm_tn(a_ref[...].astype(BF16), b_ref[...].astype(BF16))

        @pl.when(t == nt - 1)
        def _():
            o_ref[...] = acc_ref[...].astype(o_ref.dtype)

    return pl.pallas_call(
        body, name=name, grid=(groups, n_a, nt),
        in_specs=[pl.BlockSpec((tt, a_cols), a_index), pl.BlockSpec((None, tt, b_cols), b_index)],
        out_specs=pl.BlockSpec((None, a_cols, b_cols), o_index),
        out_shape=SDS(out_shape, BF16),
        scratch_shapes=[pltpu.VMEM((a_cols, b_cols), F32)],
        compiler_params=_params(3))(a, b)


def _wgrad_dense(a, b, name, tt, a_cols=None):
    ka, nb = a.shape[1], b.shape[1]
    a_cols = ka if a_cols is None else a_cols
    out = _wgrad(a, b[None], name=name, groups=1, a_cols=a_cols, b_cols=nb, tt=tt,
                 a_index=lambda g, k, t: (t, k), b_index=lambda g, k, t: (0, t, 0),
                 o_index=lambda g, k, t: (k, 0, 0), out_shape=(ka // a_cols, a_cols, nb))
    return out.reshape(ka, nb)


def _bwd_merge(dx1, proj, ya, yp, yx, pscale, w_o, w_co, w_xo, w_pool, tm):
    T = dx1.shape[0]

    def body(dx1_ref, ga_ref, gp_ref, gx_ref, ya_ref, yp_ref, yx_ref, ps_ref, wo_ref, wco_ref, wxo_ref, wp_ref,
             dgates_ref, dya_ref, dyx_ref, dyps_ref, dza_ref, do_ref, dpooled_ref, dps_ref):
        @pl.when(pl.program_id(0) == 0)
        def _():
            dps_ref[...] = jnp.zeros_like(dps_ref)
        dmerged = _mm_nt(dx1_ref[...].astype(BF16), wo_ref[...])
        scale = ps_ref[...]
        sa, sp, sx = (_sigmoid(r[...].astype(F32)) for r in (ga_ref, gp_ref, gx_ref))
        ya, yp_pre, yx = (r[...].astype(F32) for r in (ya_ref, yp_ref, yx_ref))
        dgates_ref[0] = (dmerged * ya * sa * (1.0 - sa)).astype(BF16)
        dgates_ref[1] = (dmerged * (yp_pre * scale) * sp * (1.0 - sp)).astype(BF16)
        dgates_ref[2] = (dmerged * yx * sx * (1.0 - sx)).astype(BF16)
        dya = (dmerged * sa).astype(BF16)
        dyx = (dmerged * sx).astype(BF16)
        dyp = dmerged * sp
        dyps = (dyp * scale).astype(BF16)
        dps_ref[...] += jnp.sum(dyp * yp_pre, axis=0, keepdims=True)
        dya_ref[...] = dya
        dyx_ref[...] = dyx
        dyps_ref[...] = dyps
        dza_ref[...] = _mm_nt(dya, wco_ref[...]).astype(BF16)
        do_ref[...] = _mm_nt(dyx, wxo_ref[...]).astype(BF16)
        for g in range(NPOOL):
            cols = slice(g * HD, (g + 1) * HD)
            dpooled_ref[:, cols] = _mm_nt(dyps[:, cols], wp_ref[g]).astype(BF16)

    tile = lambda s: pl.BlockSpec((None, tm, D), lambda i: (s, i, 0))
    row = pl.BlockSpec((1, D), lambda i: (0, 0))
    act = pl.BlockSpec((tm, D), lambda i: (i, 0))
    full = pl.BlockSpec((D, D), lambda i: (0, 0))
    return pl.pallas_call(
        body, name="bwd_merge", grid=(T // tm,),
        in_specs=[act, tile(5), tile(6), tile(7), act, act, act, row, full, full, full,
                  pl.BlockSpec((NPOOL, HD, HD), lambda i: (0, 0, 0))],
        out_specs=[pl.BlockSpec((3, tm, D), lambda i: (0, i, 0))] + [act] * 6 + [row],
        out_shape=[SDS((NSPLIT, T, D), BF16)] + [SDS((T, D), BF16)] * 6 + [SDS((1, D), F32)],
        compiler_params=_params(1))(dx1, proj, proj, proj, ya, yp, yx, pscale, w_o, w_co, w_xo, w_pool)


def _bwd_attn(dproj, proj, do, kv, tm):
    T = do.shape[0]
    M = kv.shape[1]

    def body(dproj_hbm, q_ref, do_ref, kv_ref, dq_ref, dkv_ref):
        del dproj_hbm

        @pl.when(pl.program_id(0) == 0)
        def _():
            dkv_ref[...] = jnp.zeros_like(dkv_ref)
        for h in range(NH):
            cols = slice(h * HD, (h + 1) * HD)
            q = q_ref[:, cols]
            do_h = do_ref[:, cols]
            p = _softmax_rows(_mm_nt(q, kv_ref[h]) * ATT_SCALE)
            dp = _mm_nt(do_h, kv_ref[NH + h])
            ds = (p * (dp - jnp.sum(dp * p, axis=-1, keepdims=True)) * ATT_SCALE).astype(BF16)
            dq_ref[:, cols] = _mm(ds, kv_ref[h]).astype(BF16)
            dkv_ref[h] += _mm_tn(ds, q)
            dkv_ref[NH + h] += _mm_tn(p.astype(BF16), do_h)

    kv_spec = pl.BlockSpec((2 * NH, M, HD), lambda i: (0, 0, 0))
    return pl.pallas_call(
        body, name="bwd_attn", grid=(T // tm,),
        in_specs=[HBM, pl.BlockSpec((None, tm, D), lambda i: (4, i, 0)), pl.BlockSpec((tm, D), lambda i: (i, 0)), kv_spec],
        out_specs=[pl.BlockSpec((None, tm, D), lambda i: (3, i, 0)), kv_spec],
        out_shape=[SDS(dproj.shape, BF16), SDS((2 * NH, M, HD), F32)],
        input_output_aliases={0: 0},
        compiler_params=_params(1))(dproj, proj, do, kv)


def _bwd_mix(dproj, proj, conv, dza, dpooled, cw0, cw1, cw2, tm):
    T = dza.shape[0]
    nt = T // tm

    def halo_after(split_or_none):
        idx = lambda i: jnp.minimum((i + 1) * (tm // HALO), T // HALO - 1)
        if split_or_none is None:
            return pl.BlockSpec((HALO, D), lambda i: (idx(i), 0))
        return pl.BlockSpec((None, HALO, D), lambda i: (split_or_none, idx(i), 0))

    def body(dproj_hbm, b_ref, c_ref, ua_ref, conv_ref, dza_ref, dpo_ref, bn_ref, dzan_ref, dpon_ref, ch_ref, uah_ref,
             cw0_ref, cw1_ref, cw2_ref, dabcu_ref, dcw_ref):
        del dproj_hbm
        i = pl.program_id(0)

        @pl.when(i == 0)
        def _():
            dcw_ref[...] = jnp.zeros_like(dcw_ref)
        keep_prev = jnp.where(i > 0, 1.0, 0.0).astype(F32)
        keep_next = jnp.where(i < nt - 1, 1.0, 0.0).astype(F32)
        dza = dza_ref[...].astype(F32)
        c = c_ref[...].astype(F32)
        ua = ua_ref[...].astype(F32)
        dconv = dza * b_ref[...].astype(F32)
        dconv_n = dzan_ref[...].astype(F32) * bn_ref[...].astype(F32) * keep_next
        ext = jnp.concatenate([dconv, dconv_n], axis=0)
        dcu = (cw2_ref[...] * ext + cw1_ref[...] * _shift_up(ext, 1) + cw0_ref[...] * _shift_up(ext, 2))[:tm]
        dabcu_ref[0] = (dza * conv_ref[...].astype(F32)).astype(BF16)
        dabcu_ref[1] = (dcu * ua).astype(BF16)
        dabcu_ref[2] = (dcu * c).astype(BF16)

        cu = c * ua
        ext_cu = jnp.concatenate([ch_ref[...].astype(F32) * uah_ref[...].astype(F32) * keep_prev, cu], axis=0)
        dcw_ref[2:3, :] += jnp.sum(dconv * cu, axis=0, keepdims=True)
        dcw_ref[1:2, :] += jnp.sum(dconv * _shift_down(ext_cu, 1)[HALO:], axis=0, keepdims=True)
        dcw_ref[0:1, :] += jnp.sum(dconv * _shift_down(ext_cu, 2)[HALO:], axis=0, keepdims=True)

        dpo = dpo_ref[...].astype(F32)
        ext_dpo = jnp.concatenate([dpo, dpon_ref[...].astype(F32) * keep_next], axis=0)
        pos = i * tm + lax.broadcasted_iota(jnp.int32, (tm + HALO, HD), 0)
        for g in range(NPOOL):
            cols = slice(g * HD, (g + 1) * HD)
            s = ext_dpo[:, cols] / jnp.minimum(pos + 1, 2 << g).astype(F32)
            for k in range(g + 1):
                s = s + _shift_up(s, 1 << k)
            dabcu_ref[3, :, cols] = (s[:tm] - dpo[:, cols]).astype(BF16)

    tile = lambda s: pl.BlockSpec((None, tm, D), lambda i: (s, i, 0))
    act = pl.BlockSpec((tm, D), lambda i: (i, 0))
    row = pl.BlockSpec((1, D), lambda i: (0, 0))
    return pl.pallas_call(
        body, name="bwd_mix", grid=(nt,),
        in_specs=[HBM, tile(0), tile(1), tile(2), act, act, act, halo_after(0), halo_after(None), halo_after(None),
                  _halo_before(1, tm), _halo_before(2, tm), row, row, row],
        out_specs=[pl.BlockSpec((4, tm, D), lambda i: (1, i, 0)), pl.BlockSpec((8, D), lambda i: (0, 0))],
        out_shape=[SDS(dproj.shape, BF16), SDS((8, D), F32)],
        input_output_aliases={0: 0},
        compiler_params=_params(1))(dproj, proj, proj, proj, conv, dza, dpooled, proj, dza, dpooled, proj, proj, cw0, cw1, cw2)


def _bwd_proj(dproj, w_in_g, x, dx1, gain, tm):
    T = x.shape[0]

    def body(dp_ref, w_ref, x_ref, dx1_ref, g_ref, dx_ref, dgain_ref, acc_ref):
        i, s = pl.program_id(0), pl.program_id(1)

        @pl.when((i == 0) & (s == 0))
        def _():
            dgain_ref[...] = jnp.zeros_like(dgain_ref)

        @pl.when(s == 0)
        def _():
            acc_ref[...] = jnp.zeros_like(acc_ref)
        acc_ref[...] += _mm_nt(dp_ref[...], w_ref[...])

        @pl.when(s == NSPLIT - 1)
        def _():
            dx, dgain = _norm_bwd(acc_ref[...], x_ref[...], g_ref[...])
            dx_ref[...] = dx1_ref[...] + dx
            dgain_ref[...] += dgain

    act = pl.BlockSpec((tm, D), lambda i, s: (i, 0))
    row = pl.BlockSpec((1, D), lambda i, s: (0, 0))
    return pl.pallas_call(
        body, name="bwd_proj", grid=(T // tm, NSPLIT),
        in_specs=[pl.BlockSpec((None, tm, D), lambda i, s: (s, i, 0)),
                  pl.BlockSpec((None, D, D), lambda i, s: (_slot_group(s), 0, 0)), act, act, row],
        out_specs=[act, row], out_shape=[SDS((T, D), F32), SDS((1, D), F32)],
        scratch_shapes=[pltpu.VMEM((tm, D), F32)],
        compiler_params=_params(2))(dproj, w_in_g, x, dx1, gain)


def _bwd_kv(dkv, memn, w_kv_g, mem, gain):
    M = mem.shape[0]

    def body(dkv_ref, memn_ref, w_ref, mem_ref, g_ref, dw_ref, dgain_ref, acc_ref):
        j = pl.program_id(0)

        @pl.when(j == 0)
        def _():
            acc_ref[...] = jnp.zeros_like(acc_ref)
        dkv_j = dkv_ref[...].astype(BF16)
        dw_ref[...] = _mm_tn(memn_ref[...], dkv_j).astype(BF16)
        acc_ref[...] += _mm_nt(dkv_j, w_ref[...])

        @pl.when(j == 2 * NH - 1)
        def _():
            dgain_ref[...] = _norm_bwd(acc_ref[...], mem_ref[...], g_ref[...])[1]

    row = pl.BlockSpec((1, D), lambda j: (0, 0))
    return pl.pallas_call(
        body, name="bwd_kv", grid=(2 * NH,),
        in_specs=[pl.BlockSpec((None, M, HD), lambda j: (j, 0, 0)), pl.BlockSpec((M, D), lambda j: (0, 0)),
                  pl.BlockSpec((None, D, HD), lambda j: (j, 0, 0)), pl.BlockSpec((M, D), lambda j: (0, 0)), row],
        out_specs=[pl.BlockSpec((None, D, HD), lambda j: (j, 0, 0)), row],
        out_shape=[SDS((2 * NH, D, HD), BF16), SDS((1, D), F32)],
        scratch_shapes=[pltpu.VMEM((M, D), F32)],
        compiler_params=_params(1))(dkv, memn, w_kv_g, mem, gain)


def _adamw_math(w, g, m, v):
    m = ADAM_B1 * m + (1.0 - ADAM_B1) * g
    v = ADAM_B2 * v + (1.0 - ADAM_B2) * (g * g)
    m_hat = m / (1.0 - ADAM_B1 ** ADAM_STEP)
    v_hat = v / (1.0 - ADAM_B2 ** ADAM_STEP)
    delta = -ADAM_LR * (m_hat / (jnp.sqrt(v_hat) + ADAM_EPS) + ADAM_WD * w)
    return delta, m, v


def _row_tile(rows):
    return 256 if rows % 256 == 0 else rows


def _sum_parts(parts, name):
    n_parts, rows, cols = parts.shape
    tr = _row_tile(rows)

    def body(p_ref, g_ref):
        g = p_ref[0].astype(F32)
        for k in range(1, n_parts):
            g = g + p_ref[k].astype(F32)
        g_ref[...] = g

    blk = pl.BlockSpec((tr, cols), lambda i: (i, 0))
    return pl.pallas_call(
        body, name=name, grid=(rows // tr,),
        in_specs=[pl.BlockSpec((n_parts, tr, cols), lambda i: (0, i, 0))], out_specs=blk,
        out_shape=SDS((rows, cols), F32), compiler_params=_params(1))(parts)


def _adamw(w, g, m, v, name, from_parts):
    rows, cols = w.shape
    tr = _row_tile(rows)

    def body(w_ref, g_ref, m_ref, v_ref, go_ref, d_ref, mo_ref, vo_ref):
        if from_parts:
            g = g_ref[0].astype(F32)
            for k in range(1, g_ref.shape[0]):
                g = g + g_ref[k].astype(F32)
        else:
            g = g_ref[...]
        go_ref[...] = g
        d_ref[...], mo_ref[...], vo_ref[...] = _adamw_math(w_ref[...], g, m_ref[...], v_ref[...])

    blk = pl.BlockSpec((tr, cols), lambda i: (i, 0))
    g_spec = pl.BlockSpec((g.shape[0], tr, cols), lambda i: (0, i, 0)) if from_parts else blk
    return pl.pallas_call(
        body, name=name, grid=(rows // tr,),
        in_specs=[blk, g_spec, blk, blk], out_specs=[blk] * 4,
        out_shape=[SDS((rows, cols), F32)] * 4, compiler_params=_params(1))(w, g, m, v)


def _peer(k, x, y, c):
    return ((1 - x) if k & 4 else x, (1 - y) if k & 2 else y, (1 - c) if k & 1 else c)


def _exchange(arrays, name, scatter):
    n = len(arrays)

    def body(*refs):
        ins, outs = refs[:n], refs[n:2 * n]
        send_sems, recv_sems, local_sems = refs[2 * n:]
        x, y, c = (lax.axis_index(a) for a in AXES)
        me = 4 * x + 2 * y + c

        def remote(a, k):
            px, py, pc = _peer(k, x, y, c)
            there = 4 * px + 2 * py + pc
            return pltpu.make_async_remote_copy(
                src_ref=ins[a].at[there] if scatter else ins[a], dst_ref=outs[a].at[me],
                send_sem=send_sems.at[a, k - 1], recv_sem=recv_sems.at[a, k - 1],
                device_id=(px, py, pc), device_id_type=pl.DeviceIdType.MESH)

        def arrival(a, k):
            px, py, pc = _peer(k, x, y, c)
            there = 4 * px + 2 * py + pc
            return pltpu.make_async_remote_copy(
                src_ref=ins[a].at[there] if scatter else ins[a], dst_ref=outs[a].at[there],
                send_sem=send_sems.at[a, k - 1], recv_sem=recv_sems.at[a, k - 1],
                device_id=(px, py, pc), device_id_type=pl.DeviceIdType.MESH)

        own = [pltpu.make_async_copy(ins[a].at[me] if scatter else ins[a], outs[a].at[me], local_sems.at[a]) for a in range(n)]
        for a in range(n):
            own[a].start()
            for k in range(1, NDEV):
                remote(a, k).start()
        for a in range(n):
            for k in range(1, NDEV):
                arrival(a, k).wait_recv()
        for a in range(n):
            for k in range(1, NDEV):
                remote(a, k).wait_send()
            own[a].wait()

    out_shape = [SDS(a.shape if scatter else (NDEV,) + a.shape, a.dtype) for a in arrays]
    return pl.pallas_call(
        body, name=name, in_specs=[HBM] * n, out_specs=[HBM] * n, out_shape=out_shape,
        scratch_shapes=[pltpu.SemaphoreType.DMA((n, NDEV - 1)), pltpu.SemaphoreType.DMA((n, NDEV - 1)),
                        pltpu.SemaphoreType.DMA((n,))],
        compiler_params=pltpu.CompilerParams(has_side_effects=True))(*arrays)


SEM = pl.BlockSpec(memory_space=pltpu.SEMAPHORE)
IN_HBM = pl.BlockSpec(memory_space=pltpu.HBM)
DATAFLOW = pltpu.SideEffectType.DATAFLOW_SIDE_EFFECTING
TOKEN_SHAPE = (8, 128)


OTHER_CHIPS = (2, 4, 6)


def _place(x, y, c):
    return 4 * x + 2 * y + c


def _plan_gather_chips(n):
    def plan(refs, x, y, c, arriving):
        out = []
        for a in range(n):
            for k in (1,) + OTHER_CHIPS:
                there = _place(*_peer(k, x, y, c))
                out.append((refs[a], refs[n + a].at[there if arriving else _place(x, y, c)], k))
        return out
    return plan, n * 4


def _plan_gather_sibling(n):
    def plan(refs, x, y, c, arriving):
        out = []
        for a in range(n):
            for k in OTHER_CHIPS:
                px, py, pc = _peer(k, x, y, c)
                mine, theirs = _place(px, py, pc), _place(px, py, 1 - pc)
                out.append((refs[a].at[mine], refs[a].at[theirs if arriving else mine], 1))
        return out
    return plan, n * 3


def _plan_scatter_sibling(n):
    def plan(refs, x, y, c, arriving):
        out = []
        for a in range(n):
            for q in range(4):
                out.append((refs[a].at[2 * q + (1 - c)], refs[n + a].at[q], 1))
        return out
    return plan, n * 4


def _plan_scatter_chips(n):
    def plan(refs, x, y, c, arriving):
        out = []
        for a in range(n):
            for k in OTHER_CHIPS:
                px, py, _ = _peer(k, x, y, c)
                out.append((refs[a].at[2 * px + py], refs[n + a].at[(2 * px + py) if arriving else (2 * x + y)], k))
        return out
    return plan, n * 3


def _remote(src, dst, send_sems, recv_sems, i, k):
    x, y, c = (lax.axis_index(n) for n in AXES)
    return pltpu.make_async_remote_copy(src_ref=src, dst_ref=dst, send_sem=send_sems.at[i], recv_sem=recv_sems.at[i],
                                        device_id=_peer(k, x, y, c), device_id_type=pl.DeviceIdType.MESH)


def _copies_start(groups, name, after):
    ng = len(groups)
    total = sum(len(bufs) for bufs, _ in groups)

    def body(*refs):
        sems = refs[1 + total:1 + total + 2 * ng]
        x, y, c = (lax.axis_index(n) for n in AXES)
        off = 1
        for gi, (bufs, (plan, _)) in enumerate(groups):
            for i, (src, dst, k) in enumerate(plan(refs[off:off + len(bufs)], x, y, c, False)):
                _remote(src, dst, sems[2 * gi], sems[2 * gi + 1], i, k).start()
            off += len(bufs)
        refs[-1][...] = jnp.zeros(TOKEN_SHAPE, F32)

    sem_shapes = [pltpu.SemaphoreType.DMA((count,)) for _, (_, count) in groups for _ in range(2)]
    flat = [b for bufs, _ in groups for b in bufs]
    outs = pl.pallas_call(
        body, name=name,
        in_specs=[HBM] + [IN_HBM] * total,
        out_specs=[SEM] * (2 * ng) + [IN_HBM] * total + [pl.BlockSpec(memory_space=pltpu.VMEM)],
        out_shape=sem_shapes + [pltpu.HBM(b.shape, b.dtype) for b in flat] + [SDS(TOKEN_SHAPE, F32)],
        input_output_aliases={1 + i: 2 * ng + i for i in range(total)},
        compiler_params=pltpu.CompilerParams(has_side_effects=DATAFLOW),
    )(after, *[pltpu.with_memory_space_constraint(b, pltpu.HBM) for b in flat])
    handles, off = [], 2 * ng
    for gi, (bufs, _) in enumerate(groups):
        handles.append((outs[2 * gi], outs[2 * gi + 1], list(outs[off:off + len(bufs)])))
        off += len(bufs)
    return handles, outs[-1]


def _copies_wait(handle, plan, name, after):
    send_sems, recv_sems, bufs = handle
    n = len(bufs)

    def body(*refs):
        x, y, c = (lax.axis_index(a) for a in AXES)
        for i, (src, dst, k) in enumerate(plan[0](refs[:n], x, y, c, True)):
            copy = _remote(src, dst, refs[n], refs[n + 1], i, k)
            copy.wait_send()
            copy.wait_recv()

    return pl.pallas_call(
        body, name=name,
        in_specs=[IN_HBM] * n + [SEM, SEM, HBM], out_specs=[IN_HBM] * n,
        out_shape=[pltpu.HBM(b.shape, b.dtype) for b in bufs],
        input_output_aliases={i: i for i in range(n)},
        compiler_params=pltpu.CompilerParams(has_side_effects=DATAFLOW),
    )(*bufs, send_sems, recv_sems, after)


def _pair_sum(mine, theirs, c, name):
    _, rows, cols = theirs.shape
    tr = _row_tile(rows)

    def body(c_ref, m_ref, t_ref, o_ref):
        del c_ref
        o_ref[...] = (m_ref[...].astype(F32) + t_ref[...].astype(F32)).astype(BF16)

    blk = pl.BlockSpec((None, tr, cols), lambda q, i, c_ref: (q, i, 0))
    return pl.pallas_call(
        body, name=name,
        grid_spec=pltpu.PrefetchScalarGridSpec(
            num_scalar_prefetch=1, grid=(4, rows // tr),
            in_specs=[pl.BlockSpec((None, tr, cols), lambda q, i, c_ref: (2 * q + c_ref[0], i, 0)), blk], out_specs=blk),
        out_shape=SDS(theirs.shape, BF16), compiler_params=_params(2))(c.reshape(1), mine, theirs)


def _local_step(x, mem, target, gains, get, put, tm_big=1024, tm_mid=512, tm_small=256):
    g_mix, pscale, g_mem, g_ffn, g_fin = gains
    T = x.shape[0]
    tm_big, tm_mid, tm_small = min(tm_big, T), min(tm_mid, T), min(tm_small, T)
    tn = DFF // 2

    w_in = get("in", x)
    proj, h = _fwd_proj(x, g_mix, w_in, tm_big)
    cw0, cw1, cw2, w_co, w_pool, w_kv = get("mix", proj)
    kv, memn = _fwd_kv(mem, g_mem, w_kv)
    za, conv, pooled, ya, yp = _fwd_mix(proj, cw0, cw1, cw2, w_co, w_pool, tm_small)
    w_xo, w_o = get("merge", ya)
    o, yx, merged, x1, h2 = _fwd_merge(proj, ya, yp, x, kv, w_xo, w_o, pscale, g_ffn, tm_small)
    wg_t, wu_t, w_d = get("ffn", x1)
    gate, up, act = _fwd_ffn_up(h2, wg_t, wu_t, tm_mid, tn)
    dx2, loss, dg_fin = _fwd_ffn_down_loss(act, w_d, x1, target, g_fin, tm_small)

    dgate, dup = _bwd_ffn_down(dx2, w_d, gate, up, tm_mid, tn)
    dx1, dg_ffn = _bwd_ffn_up(dgate, dup, wg_t, wu_t, x1, dx2, g_ffn, tm_small)
    dw_d = _wgrad_dense(act, dx2, "wgrad_down", tm_mid, a_cols=tn)
    dwg_t = _wgrad_dense(dgate, h2, "wgrad_gate", tm_mid, a_cols=tn)
    dwu_t = _wgrad_dense(dup, h2, "wgrad_up", tm_mid, a_cols=tn)
    zero = put("ffn", (dwg_t, dwu_t, dw_d))

    dproj, dya, dyx, dyps, dza, do, dpooled, dpscale = _bwd_merge(
        dx1, proj, ya, yp, yx, pscale + zero, w_o, w_co, w_xo, w_pool, tm_small)
    dw_o = _wgrad_dense(merged, dx1, "wgrad_out", tm_mid)
    dw_co = _wgrad_dense(za, dya, "wgrad_conv_out", tm_mid)
    dw_xo = _wgrad_dense(o, dyx, "wgrad_xattn_out", tm_mid)
    dw_pool = _wgrad(pooled, dyps[None], name="wgrad_pool", groups=NPOOL, a_cols=HD, b_cols=HD, tt=tm_mid,
                     a_index=lambda g, k, t: (t, g), b_index=lambda g, k, t: (0, t, g),
                     o_index=lambda g, k, t: (g, 0, 0), out_shape=(NPOOL, HD, HD))
    dproj, dkv = _bwd_attn(dproj, proj, do, kv, tm_small)
    dw_kv, dg_mem = _bwd_kv(dkv, memn, w_kv, mem, g_mem)
    zero = put("mix", (dw_co, dw_xo, dw_o, dw_pool, dw_kv))

    dproj, dcw = _bwd_mix(dproj, proj, conv, dza, dpooled, cw0 + zero, cw1, cw2, tm_small)
    dw_in = _wgrad(h, dproj, name="wgrad_in", groups=NSPLIT, a_cols=D, b_cols=D, tt=tm_mid,
                   a_index=lambda g, k, t: (t, 0), b_index=lambda g, k, t: (g, t, 0),
                   o_index=lambda g, k, t: (_slot_group(g), 0, 0), out_shape=(NSPLIT, D, D))
    zero = put("in", (dw_in,))
    grad_x, dg_mix = _bwd_proj(dproj, w_in, x, dx1, g_mix + zero, tm_mid)

    small = jnp.concatenate([dg_mix, dpscale, dg_mem, dg_ffn, dg_fin, dcw[0:3]], axis=0)
    return loss[0, 0], grad_x, small


def kernel(x, mem, norm_mix, w_in, conv_w, w_conv_out, w_pool, pool_scale, norm_mem, w_kv, w_xattn_out, w_out, norm_ffn, w_gate, w_up, w_down, norm_final, loss_target, m_norm_mix, m_w_in, m_conv_w, m_w_conv_out, m_w_pool, m_pool_scale, m_norm_mem, m_w_kv, m_w_xattn_out, m_w_out, m_norm_ffn, m_w_gate, m_w_up, m_w_down, m_norm_final, v_norm_mix, v_w_in, v_conv_w, v_w_conv_out, v_w_pool, v_pool_scale, v_norm_mem, v_w_kv, v_w_xattn_out, v_w_out, v_norm_ffn, v_w_gate, v_w_up, v_w_down, v_norm_final):
    T = x.shape[1]
    rows = D // NDEV
    ffb = DFF // NDEV
    prow = HD // NDEV
    me = 4 * lax.axis_index("x") + 2 * lax.axis_index("y") + lax.axis_index("c")

    shards = [w_in[0].astype(BF16), w_conv_out[0].astype(BF16), w_xattn_out[0].astype(BF16), w_out[0].astype(BF16),
              w_pool[0].astype(BF16).reshape(NPOOL * prow, HD), w_kv[0].astype(BF16),
              w_gate[0].T.astype(BF16), w_up[0].T.astype(BF16), w_down[0].astype(BF16),
              jnp.pad(conv_w[0], ((0, 5), (0, 0)))]

    cx, cy, cc = (lax.axis_index(n) for n in AXES)
    chip = 2 * cx + cy

    def land(own, index, slots):
        return lax.dynamic_update_index_in_dim(lax.empty((slots,) + own.shape, own.dtype), own, index, 0)

    needed = ["in", "mix", "merge", "ffn"]
    members = {"in": [0], "mix": [9, 1, 4, 5], "merge": [2, 3], "ffn": [6, 7, 8]}
    g_handles, _ = _copies_start(
        [([shards[i] for i in members[n]] + [land(shards[i], me, NDEV) for i in members[n]], _plan_gather_chips(len(members[n])))
         for n in needed], "gather_start", x)

    def get(group, after):
        n = len(members[group])
        bufs = _copies_wait(g_handles[needed.index(group)], _plan_gather_chips(n), "gather_wait_" + group, after)
        (handle,), token = _copies_start([(list(bufs[n:]), _plan_gather_sibling(n))], "gather_pass_" + group, bufs[n])
        got = _copies_wait(handle, _plan_gather_sibling(n), "gather_passed_" + group, token)
        if group == "in":
            return got[0]
        if group == "mix":
            cw_g, w_co_g, w_pool_g, w_kv_g = got
            cw_full = cw_g.transpose(1, 0, 2).reshape(8, D)
            w_pool_full = w_pool_g.reshape(NDEV, NPOOL, prow, HD).transpose(1, 0, 2, 3).reshape(NPOOL, HD, HD)
            return cw_full[0:1], cw_full[1:2], cw_full[2:3], w_co_g.reshape(D, D), w_pool_full, w_kv_g
        if group == "merge":
            return got[0].reshape(D, D), got[1].reshape(D, D)
        return got[0].reshape(DFF, D), got[1].reshape(DFF, D), got[2].reshape(DFF, D)

    started = {}

    def put(group, grads):
        if group == "ffn":
            sends = [g.reshape(NDEV, ffb, D) for g in grads]
        elif group == "mix":
            dw_co, dw_xo, dw_o, dw_pool, dw_kv = grads
            sends = [dw_co.reshape(NDEV, rows, D), dw_xo.reshape(NDEV, rows, D), dw_o.reshape(NDEV, rows, D),
                     dw_pool.reshape(NPOOL, NDEV, prow, HD).transpose(1, 0, 2, 3).reshape(NDEV, NPOOL * prow, HD), dw_kv]
        else:
            sends = list(grads)
        n = len(sends)
        halves = [lax.empty((4,) + s.shape[1:], s.dtype) for s in sends]
        (handle,), token = _copies_start([(sends + halves, _plan_scatter_sibling(n))], "scatter_swap_" + group, sends[0])
        bufs = _copies_wait(handle, _plan_scatter_sibling(n), "scatter_swapped_" + group, token)
        sums = [_pair_sum(bufs[a], bufs[n + a], cc, "pair_sum_%s_%d" % (group, a)) for a in range(n)]
        lands = [land(lax.dynamic_index_in_dim(s, chip, 0, keepdims=False), chip, 4) for s in sums]
        (handle,), token = _copies_start([(sums + lands, _plan_scatter_chips(n))], "scatter_start_" + group, sums[0])
        started[group] = (handle, _plan_scatter_chips(n))
        return token[0:1, 0:1]

    def take(group, after):
        handle, plan = started[group]
        return _copies_wait(handle, plan, "scatter_wait_" + group, after)[len(handle[2]) // 2:]

    gains = (norm_mix, pool_scale, norm_mem, norm_ffn, norm_final.reshape(1, D))
    loss_part, grad_x, small = _local_step(x[0], mem[0], loss_target[0], gains, get, put)

    (small_all,) = _exchange([small], "gather_small", scatter=False)
    small_sum = _sum_parts(small_all, "sum_small")

    loss = lax.psum(loss_part, AXES)

    def sharded(name, w, parts, m, v):
        shape = w.shape
        flat = lambda a: a.reshape(parts.shape[1], parts.shape[2])
        outs = _adamw(flat(w), parts, flat(m), flat(v), "adamw_" + name, from_parts=True)
        return [o.reshape(shape) for o in outs]

    def transposed(name, w, parts, m, v):
        g = _sum_parts(parts, "sum_" + name).T
        outs = _adamw(w[0], g, m[0], v[0], "adamw_" + name, from_parts=False)
        return [o[None] for o in outs]

    def replicated(name, w, g, m, v):
        shape = w.shape
        flat = lambda a: a.reshape(g.shape)
        outs = _adamw(flat(w), g, flat(m), flat(v), "adamw_" + name, from_parts=False)
        return [o.reshape(shape) for o in outs]

    g_cw = lax.dynamic_slice_in_dim(small_sum[5:8], me * rows, rows, axis=1)
    res = {
        "norm_mix": replicated("norm_mix", norm_mix, small_sum[0:1], m_norm_mix, v_norm_mix),
        "conv_w": replicated("conv_w", conv_w, g_cw, m_conv_w, v_conv_w),
        "pool_scale": replicated("pool_scale", pool_scale, small_sum[1:2], m_pool_scale, v_pool_scale),
        "norm_mem": replicated("norm_mem", norm_mem, small_sum[2:3], m_norm_mem, v_norm_mem),
        "norm_ffn": replicated("norm_ffn", norm_ffn, small_sum[3:4], m_norm_ffn, v_norm_ffn),
        "norm_final": replicated("norm_final", norm_final, small_sum[4:5], m_norm_final, v_norm_final),
    }
    p_g, p_u, p_d = take("ffn", res["norm_final"][1])
    res["w_gate"] = transposed("w_gate", w_gate, p_g, m_w_gate, v_w_gate)
    res["w_up"] = transposed("w_up", w_up, p_u, m_w_up, v_w_up)
    res["w_down"] = sharded("w_down", w_down, p_d, m_w_down, v_w_down)
    p_co, p_xo, p_o, p_pool, p_kv = take("mix", res["w_down"][1])
    res["w_conv_out"] = sharded("w_conv_out", w_conv_out, p_co, m_w_conv_out, v_w_conv_out)
    res["w_pool"] = sharded("w_pool", w_pool, p_pool, m_w_pool, v_w_pool)
    res["w_kv"] = sharded("w_kv", w_kv, p_kv, m_w_kv, v_w_kv)
    res["w_xattn_out"] = sharded("w_xattn_out", w_xattn_out, p_xo, m_w_xattn_out, v_w_xattn_out)
    res["w_out"] = sharded("w_out", w_out, p_o, m_w_out, v_w_out)
    (p_in,) = take("in", res["w_out"][1])
    res["w_in"] = sharded("w_in", w_in, p_in, m_w_in, v_w_in)
    order = ["norm_mix", "w_in", "conv_w", "w_conv_out", "w_pool", "pool_scale", "norm_mem", "w_kv", "w_xattn_out", "w_out",
             "norm_ffn", "w_gate", "w_up", "w_down", "norm_final"]
    return (loss, grad_x[None], *[res[n][0] for n in order], *[res[n][1] for n in order],
            *[res[n][2] for n in order], *[res[n][3] for n in order])
```

```python
import jax
import jax.numpy as jnp
from jax import lax
from jax.experimental import pallas as pl
from jax.experimental.pallas import tpu as pltpu

F32 = jnp.float32
BF16 = jnp.bfloat16
SDS = jax.ShapeDtypeStruct

AXES = ("x", "y", "c")
NDEV = 8
D = 1024
NSPLIT = 8
NH = 4
HD = D // NH
NPOOL = 4
DFF = 2816
EPS = 1e-6
ATT_SCALE = HD ** -0.5
HALO = 16


def _slot_group(s):
    return jnp.where(s < 3, s + 5, jnp.where(s == 3, 4, s - 4))


ADAM_LR = 0.001
ADAM_B1 = 0.9
ADAM_B2 = 0.999
ADAM_EPS = 1e-08
ADAM_WD = 0.01
ADAM_STEP = 10

V7X_VMEM_BYTES = 64 * 1024 * 1024
VMEM_LIMIT = V7X_VMEM_BYTES - 8 * 1024 * 1024
HBM = pl.BlockSpec(memory_space=pl.ANY)


def _params(n_grid):
    return pltpu.CompilerParams(dimension_semantics=("arbitrary",) * n_grid, vmem_limit_bytes=VMEM_LIMIT)


def _mm(a, b):
    return jnp.dot(a, b, preferred_element_type=F32)


def _mm_nt(a, b):
    return lax.dot_general(a, b, (((1,), (1,)), ((), ())), preferred_element_type=F32)


def _mm_tn(a, b):
    return lax.dot_general(a, b, (((0,), (0,)), ((), ())), preferred_element_type=F32)


def _sigmoid(x):
    return 1.0 / (1.0 + jnp.exp(-x))


def _rms(x):
    return lax.rsqrt(jnp.mean(x * x, axis=-1, keepdims=True) + EPS)


def _norm_bwd(dh, x, gain):
    r = _rms(x)
    xh = x * r
    dxh = dh * gain
    dx = r * (dxh - xh * jnp.mean(dxh * xh, axis=-1, keepdims=True))
    return dx, jnp.sum(dh * xh, axis=0, keepdims=True)


def _shift_down(v, k):
    return pltpu.roll(v, k, 0)


def _shift_up(v, k):
    return pltpu.roll(v, v.shape[0] - k, 0)


def _fwd_proj(x, gain, w_in_g, tm):
    T = x.shape[0]

    def body(x_ref, g_ref, w_ref, proj_ref, h_ref):
        @pl.when(pl.program_id(1) == 0)
        def _():
            xf = x_ref[...]
            h_ref[...] = (xf * _rms(xf) * g_ref[...]).astype(BF16)
        proj_ref[...] = _mm(h_ref[...], w_ref[...]).astype(BF16)

    return pl.pallas_call(
        body, name="fwd_proj", grid=(T // tm, NSPLIT),
        in_specs=[pl.BlockSpec((tm, D), lambda i, j: (i, 0)), pl.BlockSpec((1, D), lambda i, j: (0, 0)),
                  pl.BlockSpec((None, D, D), lambda i, j: (j, 0, 0))],
        out_specs=[pl.BlockSpec((None, tm, D), lambda i, j: (j, i, 0)), pl.BlockSpec((tm, D), lambda i, j: (i, 0))],
        out_shape=[SDS((NSPLIT, T, D), BF16), SDS((T, D), BF16)],
        compiler_params=_params(2))(x, gain, w_in_g)


def _fwd_kv(mem, gain, w_kv_g):
    M = mem.shape[0]

    def body(mem_ref, g_ref, w_ref, kv_ref, memn_ref):
        @pl.when(pl.program_id(0) == 0)
        def _():
            m = mem_ref[...]
            memn_ref[...] = (m * _rms(m) * g_ref[...]).astype(BF16)
        kv_ref[...] = _mm(memn_ref[...], w_ref[...]).astype(BF16)

    return pl.pallas_call(
        body, name="fwd_kv", grid=(2 * NH,),
        in_specs=[pl.BlockSpec((M, D), lambda j: (0, 0)), pl.BlockSpec((1, D), lambda j: (0, 0)),
                  pl.BlockSpec((None, D, HD), lambda j: (j, 0, 0))],
        out_specs=[pl.BlockSpec((None, M, HD), lambda j: (j, 0, 0)), pl.BlockSpec((M, D), lambda j: (0, 0))],
        out_shape=[SDS((2 * NH, M, HD), BF16), SDS((M, D), BF16)],
        compiler_params=_params(1))(mem, gain, w_kv_g)


def _halo_before(split, tm):
    return pl.BlockSpec((None, HALO, D), lambda i: (split, jnp.maximum(i * (tm // HALO) - 1, 0), 0))


def _fwd_mix(proj, cw0, cw1, cw2, w_co, w_pool, tm):
    T = proj.shape[1]

    def body(b_ref, c_ref, ua_ref, up_ref, ch_ref, uah_ref, uph_ref, cw0_ref, cw1_ref, cw2_ref, wco_ref, wp_ref,
             za_ref, conv_ref, pooled_ref, ya_ref, yp_ref):
        i = pl.program_id(0)
        keep = jnp.where(i > 0, 1.0, 0.0).astype(F32)
        cu = c_ref[...].astype(F32) * ua_ref[...].astype(F32)
        cu_h = ch_ref[...].astype(F32) * uah_ref[...].astype(F32) * keep
        ext = jnp.concatenate([cu_h, cu], axis=0)
        conv = (cw2_ref[...] * ext + cw1_ref[...] * _shift_down(ext, 1) + cw0_ref[...] * _shift_down(ext, 2))[HALO:]
        za = (b_ref[...].astype(F32) * conv).astype(BF16)
        conv_ref[...] = conv.astype(BF16)
        za_ref[...] = za
        ya_ref[...] = _mm(za, wco_ref[...]).astype(BF16)

        up = up_ref[...].astype(F32)
        ext_u = jnp.concatenate([uph_ref[...].astype(F32) * keep, up], axis=0)
        pos = i * tm + lax.broadcasted_iota(jnp.int32, (tm, HD), 0)
        for g in range(NPOOL):
            cols = slice(g * HD, (g + 1) * HD)
            s = ext_u[:, cols]
            for k in range(g + 1):
                s = s + _shift_down(s, 1 << k)
            cnt = jnp.minimum(pos + 1, 2 << g).astype(F32)
            pooled = (s[HALO:] / cnt - up[:, cols]).astype(BF16)
            pooled_ref[:, cols] = pooled
            yp_ref[:, cols] = _mm(pooled, wp_ref[g]).astype(BF16)

    tile = lambda s: pl.BlockSpec((None, tm, D), lambda i: (s, i, 0))
    row = pl.BlockSpec((1, D), lambda i: (0, 0))
    out = pl.BlockSpec((tm, D), lambda i: (i, 0))
    return pl.pallas_call(
        body, name="fwd_mix", grid=(T // tm,),
        in_specs=[tile(0), tile(1), tile(2), tile(3), _halo_before(1, tm), _halo_before(2, tm), _halo_before(3, tm),
                  row, row, row, pl.BlockSpec((D, D), lambda i: (0, 0)), pl.BlockSpec((NPOOL, HD, HD), lambda i: (0, 0, 0))],
        out_specs=[out] * 5,
        out_shape=[SDS((T, D), BF16)] * 5,
        compiler_params=_params(1))(proj, proj, proj, proj, proj, proj, proj, cw0, cw1, cw2, w_co, w_pool)


def _softmax_rows(s):
    e = jnp.exp(s - jnp.max(s, axis=-1, keepdims=True))
    return e / jnp.sum(e, axis=-1, keepdims=True)


def _fwd_merge(proj, ya, yp, x, kv, w_xo, w_o, pscale, gain_ffn, tm):
    T = x.shape[0]

    def body(q_ref, ga_ref, gp_ref, gx_ref, ya_ref, yp_ref, x_ref, kv_ref, wxo_ref, wo_ref, ps_ref, gf_ref,
             o_ref, yx_ref, merged_ref, x1_ref, h2_ref):
        for h in range(NH):
            cols = slice(h * HD, (h + 1) * HD)
            p = _softmax_rows(_mm_nt(q_ref[:, cols], kv_ref[h]) * ATT_SCALE)
            o_ref[:, cols] = _mm(p.astype(BF16), kv_ref[NH + h]).astype(BF16)
        yx = _mm(o_ref[...], wxo_ref[...])
        yx_ref[...] = yx.astype(BF16)
        merged = (_sigmoid(ga_ref[...].astype(F32)) * ya_ref[...].astype(F32)
                  + _sigmoid(gp_ref[...].astype(F32)) * (yp_ref[...].astype(F32) * ps_ref[...])
                  + _sigmoid(gx_ref[...].astype(F32)) * yx).astype(BF16)
        merged_ref[...] = merged
        x1 = x_ref[...] + _mm(merged, wo_ref[...])
        x1_ref[...] = x1
        h2_ref[...] = (x1 * _rms(x1) * gf_ref[...]).astype(BF16)

    tile = lambda s: pl.BlockSpec((None, tm, D), lambda i: (s, i, 0))
    row = pl.BlockSpec((1, D), lambda i: (0, 0))
    act = pl.BlockSpec((tm, D), lambda i: (i, 0))
    full = pl.BlockSpec((D, D), lambda i: (0, 0))
    return pl.pallas_call(
        body, name="fwd_merge", grid=(T // tm,),
        in_specs=[tile(4), tile(5), tile(6), tile(7), act, act, act,
                  pl.BlockSpec((2 * NH, kv.shape[1], HD), lambda i: (0, 0, 0)), full, full, row, row],
        out_specs=[act] * 5,
        out_shape=[SDS((T, D), BF16), SDS((T, D), BF16), SDS((T, D), BF16), SDS((T, D), F32), SDS((T, D), BF16)],
        compiler_params=_params(1))(proj, proj, proj, proj, ya, yp, x, kv, w_xo, w_o, pscale, gain_ffn)


def _fwd_ffn_up(h2, wg_t, wu_t, tm, tn):
    T = h2.shape[0]

    def body(h_ref, wg_ref, wu_ref, gate_ref, up_ref, act_ref):
        gate = _mm_nt(h_ref[...], wg_ref[...])
        up = _mm_nt(h_ref[...], wu_ref[...])
        gate_ref[...] = gate.astype(BF16)
        up_ref[...] = up.astype(BF16)
        act_ref[...] = (gate * _sigmoid(gate) * up).astype(BF16)

    w = pl.BlockSpec((tn, D), lambda i, n: (n, 0))
    o = pl.BlockSpec((tm, tn), lambda i, n: (i, n))
    return pl.pallas_call(
        body, name="fwd_ffn_up", grid=(T // tm, DFF // tn),
        in_specs=[pl.BlockSpec((tm, D), lambda i, n: (i, 0)), w, w],
        out_specs=[o] * 3, out_shape=[SDS((T, DFF), BF16)] * 3,
        compiler_params=_params(2))(h2, wg_t, wu_t)


def _fwd_ffn_down_loss(act, w_d, x1, target, gain_final, tm):
    T = x1.shape[0]

    def body(act_ref, wd_ref, x1_ref, tgt_ref, g_ref, dx2_ref, loss_ref, dgain_ref):
        @pl.when(pl.program_id(0) == 0)
        def _():
            loss_ref[...] = jnp.zeros_like(loss_ref)
            dgain_ref[...] = jnp.zeros_like(dgain_ref)
        x2 = x1_ref[...] + _mm(act_ref[...], wd_ref[...])
        gain = g_ref[...]
        y = x2 * _rms(x2) * gain
        err = y - tgt_ref[...]
        loss_ref[...] += 0.5 * jnp.sum(jnp.mean(err * err, axis=-1, keepdims=True))
        dx2, dgain = _norm_bwd(err * (1.0 / D), x2, gain)
        dx2_ref[...] = dx2
        dgain_ref[...] += dgain

    act_spec = pl.BlockSpec((tm, D), lambda i: (i, 0))
    row = pl.BlockSpec((1, D), lambda i: (0, 0))
    return pl.pallas_call(
        body, name="fwd_ffn_down_loss", grid=(T // tm,),
        in_specs=[pl.BlockSpec((tm, DFF), lambda i: (i, 0)), pl.BlockSpec((DFF, D), lambda i: (0, 0)), act_spec, act_spec, row],
        out_specs=[act_spec, pl.BlockSpec((8, 128), lambda i: (0, 0)), row],
        out_shape=[SDS((T, D), F32), SDS((8, 128), F32), SDS((1, D), F32)],
        compiler_params=_params(1))(act, w_d, x1, target, gain_final)


def _bwd_ffn_down(dx2, w_d, gate, up, tm, tn):
    T = dx2.shape[0]

    def body(dx_ref, wd_ref, gate_ref, up_ref, dgate_ref, dup_ref):
        dact = _mm_nt(dx_ref[...].astype(BF16), wd_ref[...])
        gate = gate_ref[...].astype(F32)
        sg = _sigmoid(gate)
        dgate_ref[...] = (dact * up_ref[...].astype(F32) * (sg * (1.0 + gate * (1.0 - sg)))).astype(BF16)
        dup_ref[...] = (dact * gate * sg).astype(BF16)

    o = pl.BlockSpec((tm, tn), lambda i, n: (i, n))
    return pl.pallas_call(
        body, name="bwd_ffn_down", grid=(T // tm, DFF // tn),
        in_specs=[pl.BlockSpec((tm, D), lambda i, n: (i, 0)), pl.BlockSpec((tn, D), lambda i, n: (n, 0)), o, o],
        out_specs=[o] * 2, out_shape=[SDS((T, DFF), BF16)] * 2,
        compiler_params=_params(2))(dx2, w_d, gate, up)


def _bwd_ffn_up(dgate, dup, wg_t, wu_t, x1, dx2, gain_ffn, tm):
    T = x1.shape[0]

    def body(dg_ref, du_ref, wg_ref, wu_ref, x1_ref, dx2_ref, g_ref, dx1_ref, dgain_ref):
        @pl.when(pl.program_id(0) == 0)
        def _():
            dgain_ref[...] = jnp.zeros_like(dgain_ref)
        dh2 = _mm(dg_ref[...], wg_ref[...]) + _mm(du_ref[...], wu_ref[...])
        dx, dgain = _norm_bwd(dh2, x1_ref[...], g_ref[...])
        dx1_ref[...] = dx2_ref[...] + dx
        dgain_ref[...] += dgain

    wide = pl.BlockSpec((tm, DFF), lambda i: (i, 0))
    w = pl.BlockSpec((DFF, D), lambda i: (0, 0))
    act = pl.BlockSpec((tm, D), lambda i: (i, 0))
    row = pl.BlockSpec((1, D), lambda i: (0, 0))
    return pl.pallas_call(
        body, name="bwd_ffn_up", grid=(T // tm,),
        in_specs=[wide, wide, w, w, act, act, row], out_specs=[act, row],
        out_shape=[SDS((T, D), F32), SDS((1, D), F32)],
        compiler_params=_params(1))(dgate, dup, wg_t, wu_t, x1, dx2, gain_ffn)


def _wgrad(a, b, *, name, groups, a_cols, b_cols, tt, a_index, b_index, o_index, out_shape):
    T = a.shape[0]
    nt = T // tt
    n_a = a.shape[1] // a_cols if groups == 1 else 1

    def body(a_ref, b_ref, o_ref, acc_ref):
        t = pl.program_id(2)

        @pl.when(t == 0)
        def _():
            acc_ref[...] = jnp.zeros_like(acc_ref)
        acc_ref[...] += _mm_tn(a_ref[...].astype(BF16), b_ref[...].astype(BF16))

        @pl.when(t == nt - 1)
        def _():
            o_ref[...] = acc_ref[...].astype(o_ref.dtype)

    return pl.pallas_call(
        body, name=name, grid=(groups, n_a, nt),
        in_specs=[pl.BlockSpec((tt, a_cols), a_index), pl.BlockSpec((None, tt, b_cols), b_index)],
        out_specs=pl.BlockSpec((None, a_cols, b_cols), o_index),
        out_shape=SDS(out_shape, BF16),
        scratch_shapes=[pltpu.VMEM((a_cols, b_cols), F32)],
        compiler_params=_params(3))(a, b)


def _wgrad_dense(a, b, name, tt, a_cols=None):
    ka, nb = a.shape[1], b.shape[1]
    a_cols = ka if a_cols is None else a_cols
    out = _wgrad(a, b[None], name=name, groups=1, a_cols=a_cols, b_cols=nb, tt=tt,
                 a_index=lambda g, k, t: (t, k), b_index=lambda g, k, t: (0, t, 0),
                 o_index=lambda g, k, t: (k, 0, 0), out_shape=(ka // a_cols, a_cols, nb))
    return out.reshape(ka, nb)


def _bwd_merge(dx1, proj, ya, yp, yx, pscale, w_o, w_co, w_xo, w_pool, tm):
    T = dx1.shape[0]

    def body(dx1_ref, ga_ref, gp_ref, gx_ref, ya_ref, yp_ref, yx_ref, ps_ref, wo_ref, wco_ref, wxo_ref, wp_ref,
             dgates_ref, dya_ref, dyx_ref, dyps_ref, dza_ref, do_ref, dpooled_ref, dps_ref):
        @pl.when(pl.program_id(0) == 0)
        def _():
            dps_ref[...] = jnp.zeros_like(dps_ref)
        dmerged = _mm_nt(dx1_ref[...].astype(BF16), wo_ref[...])
        scale = ps_ref[...]
        sa, sp, sx = (_sigmoid(r[...].astype(F32)) for r in (ga_ref, gp_ref, gx_ref))
        ya, yp_pre, yx = (r[...].astype(F32) for r in (ya_ref, yp_ref, yx_ref))
        dgates_ref[0] = (dmerged * ya * sa * (1.0 - sa)).astype(BF16)
        dgates_ref[1] = (dmerged * (yp_pre * scale) * sp * (1.0 - sp)).astype(BF16)
        dgates_ref[2] = (dmerged * yx * sx * (1.0 - sx)).astype(BF16)
        dya = (dmerged * sa).astype(BF16)
        dyx = (dmerged * sx).astype(BF16)
        dyp = dmerged * sp
        dyps = (dyp * scale).astype(BF16)
        dps_ref[...] += jnp.sum(dyp * yp_pre, axis=0, keepdims=True)
        dya_ref[...] = dya
        dyx_ref[...] = dyx
        dyps_ref[...] = dyps
        dza_ref[...] = _mm_nt(dya, wco_ref[...]).astype(BF16)
        do_ref[...] = _mm_nt(dyx, wxo_ref[...]).astype(BF16)
        for g in range(NPOOL):
            cols = slice(g * HD, (g + 1) * HD)
            dpooled_ref[:, cols] = _mm_nt(dyps[:, cols], wp_ref[g]).astype(BF16)

    tile = lambda s: pl.BlockSpec((None, tm, D), lambda i: (s, i, 0))
    row = pl.BlockSpec((1, D), lambda i: (0, 0))
    act = pl.BlockSpec((tm, D), lambda i: (i, 0))
    full = pl.BlockSpec((D, D), lambda i: (0, 0))
    return pl.pallas_call(
        body, name="bwd_merge", grid=(T // tm,),
        in_specs=[act, tile(5), tile(6), tile(7), act, act, act, row, full, full, full,
                  pl.BlockSpec((NPOOL, HD, HD), lambda i: (0, 0, 0))],
        out_specs=[pl.BlockSpec((3, tm, D), lambda i: (0, i, 0))] + [act] * 6 + [row],
        out_shape=[SDS((NSPLIT, T, D), BF16)] + [SDS((T, D), BF16)] * 6 + [SDS((1, D), F32)],
        compiler_params=_params(1))(dx1, proj, proj, proj, ya, yp, yx, pscale, w_o, w_co, w_xo, w_pool)


def _bwd_attn(dproj, proj, do, kv, tm):
    T = do.shape[0]
    M = kv.shape[1]

    def body(dproj_hbm, q_ref, do_ref, kv_ref, dq_ref, dkv_ref):
        del dproj_hbm

        @pl.when(pl.program_id(0) == 0)
        def _():
            dkv_ref[...] = jnp.zeros_like(dkv_ref)
        for h in range(NH):
            cols = slice(h * HD, (h + 1) * HD)
            q = q_ref[:, cols]
            do_h = do_ref[:, cols]
            p = _softmax_rows(_mm_nt(q, kv_ref[h]) * ATT_SCALE)
            dp = _mm_nt(do_h, kv_ref[NH + h])
            ds = (p * (dp - jnp.sum(dp * p, axis=-1, keepdims=True)) * ATT_SCALE).astype(BF16)
            dq_ref[:, cols] = _mm(ds, kv_ref[h]).astype(BF16)
            dkv_ref[h] += _mm_tn(ds, q)
            dkv_ref[NH + h] += _mm_tn(p.astype(BF16), do_h)

    kv_spec = pl.BlockSpec((2 * NH, M, HD), lambda i: (0, 0, 0))
    return pl.pallas_call(
        body, name="bwd_attn", grid=(T // tm,),
        in_specs=[HBM, pl.BlockSpec((None, tm, D), lambda i: (4, i, 0)), pl.BlockSpec((tm, D), lambda i: (i, 0)), kv_spec],
        out_specs=[pl.BlockSpec((None, tm, D), lambda i: (3, i, 0)), kv_spec],
        out_shape=[SDS(dproj.shape, BF16), SDS((2 * NH, M, HD), F32)],
        input_output_aliases={0: 0},
        compiler_params=_params(1))(dproj, proj, do, kv)


def _bwd_mix(dproj, proj, conv, dza, dpooled, cw0, cw1, cw2, tm):
    T = dza.shape[0]
    nt = T // tm

    def halo_after(split_or_none):
        idx = lambda i: jnp.minimum((i + 1) * (tm // HALO), T // HALO - 1)
        if split_or_none is None:
            return pl.BlockSpec((HALO, D), lambda i: (idx(i), 0))
        return pl.BlockSpec((None, HALO, D), lambda i: (split_or_none, idx(i), 0))

    def body(dproj_hbm, b_ref, c_ref, ua_ref, conv_ref, dza_ref, dpo_ref, bn_ref, dzan_ref, dpon_ref, ch_ref, uah_ref,
             cw0_ref, cw1_ref, cw2_ref, dabcu_ref, dcw_ref):
        del dproj_hbm
        i = pl.program_id(0)

        @pl.when(i == 0)
        def _():
            dcw_ref[...] = jnp.zeros_like(dcw_ref)
        keep_prev = jnp.where(i > 0, 1.0, 0.0).astype(F32)
        keep_next = jnp.where(i < nt - 1, 1.0, 0.0).astype(F32)
        dza = dza_ref[...].astype(F32)
        c = c_ref[...].astype(F32)
        ua = ua_ref[...].astype(F32)
        dconv = dza * b_ref[...].astype(F32)
        dconv_n = dzan_ref[...].astype(F32) * bn_ref[...].astype(F32) * keep_next
        ext = jnp.concatenate([dconv, dconv_n], axis=0)
        dcu = (cw2_ref[...] * ext + cw1_ref[...] * _shift_up(ext, 1) + cw0_ref[...] * _shift_up(ext, 2))[:tm]
        dabcu_ref[0] = (dza * conv_ref[...].astype(F32)).astype(BF16)
        dabcu_ref[1] = (dcu * ua).astype(BF16)
        dabcu_ref[2] = (dcu * c).astype(BF16)

        cu = c * ua
        ext_cu = jnp.concatenate([ch_ref[...].astype(F32) * uah_ref[...].astype(F32) * keep_prev, cu], axis=0)
        dcw_ref[2:3, :] += jnp.sum(dconv * cu, axis=0, keepdims=True)
        dcw_ref[1:2, :] += jnp.sum(dconv * _shift_down(ext_cu, 1)[HALO:], axis=0, keepdims=True)
        dcw_ref[0:1, :] += jnp.sum(dconv * _shift_down(ext_cu, 2)[HALO:], axis=0, keepdims=True)

        dpo = dpo_ref[...].astype(F32)
        ext_dpo = jnp.concatenate([dpo, dpon_ref[...].astype(F32) * keep_next], axis=0)
        pos = i * tm + lax.broadcasted_iota(jnp.int32, (tm + HALO, HD), 0)
        for g in range(NPOOL):
            cols = slice(g * HD, (g + 1) * HD)
            s = ext_dpo[:, cols] / jnp.minimum(pos + 1, 2 << g).astype(F32)
            for k in range(g + 1):
                s = s + _shift_up(s, 1 << k)
            dabcu_ref[3, :, cols] = (s[:tm] - dpo[:, cols]).astype(BF16)

    tile = lambda s: pl.BlockSpec((None, tm, D), lambda i: (s, i, 0))
    act = pl.BlockSpec((tm, D), lambda i: (i, 0))
    row = pl.BlockSpec((1, D), lambda i: (0, 0))
    return pl.pallas_call(
        body, name="bwd_mix", grid=(nt,),
        in_specs=[HBM, tile(0), tile(1), tile(2), act, act, act, halo_after(0), halo_after(None), halo_after(None),
                  _halo_before(1, tm), _halo_before(2, tm), row, row, row],
        out_specs=[pl.BlockSpec((4, tm, D), lambda i: (1, i, 0)), pl.BlockSpec((8, D), lambda i: (0, 0))],
        out_shape=[SDS(dproj.shape, BF16), SDS((8, D), F32)],
        input_output_aliases={0: 0},
        compiler_params=_params(1))(dproj, proj, proj, proj, conv, dza, dpooled, proj, dza, dpooled, proj, proj, cw0, cw1, cw2)


def _bwd_proj(dproj, w_in_g, x, dx1, gain, tm):
    T = x.shape[0]

    def body(dp_ref, w_ref, x_ref, dx1_ref, g_ref, dx_ref, dgain_ref, acc_ref):
        i, s = pl.program_id(0), pl.program_id(1)

        @pl.when((i == 0) & (s == 0))
        def _():
            dgain_ref[...] = jnp.zeros_like(dgain_ref)

        @pl.when(s == 0)
        def _():
            acc_ref[...] = jnp.zeros_like(acc_ref)
        acc_ref[...] += _mm_nt(dp_ref[...], w_ref[...])

        @pl.when(s == NSPLIT - 1)
        def _():
            dx, dgain = _norm_bwd(acc_ref[...], x_ref[...], g_ref[...])
            dx_ref[...] = dx1_ref[...] + dx
            dgain_ref[...] += dgain

    act = pl.BlockSpec((tm, D), lambda i, s: (i, 0))
    row = pl.BlockSpec((1, D), lambda i, s: (0, 0))
    return pl.pallas_call(
        body, name="bwd_proj", grid=(T // tm, NSPLIT),
        in_specs=[pl.BlockSpec((None, tm, D), lambda i, s: (s, i, 0)),
                  pl.BlockSpec((None, D, D), lambda i, s: (_slot_group(s), 0, 0)), act, act, row],
        out_specs=[act, row], out_shape=[SDS((T, D), F32), SDS((1, D), F32)],
        scratch_shapes=[pltpu.VMEM((tm, D), F32)],
        compiler_params=_params(2))(dproj, w_in_g, x, dx1, gain)


def _bwd_kv(dkv, memn, w_kv_g, mem, gain):
    M = mem.shape[0]

    def body(dkv_ref, memn_ref, w_ref, mem_ref, g_ref, dw_ref, dgain_ref, acc_ref):
        j = pl.program_id(0)

        @pl.when(j == 0)
        def _():
            acc_ref[...] = jnp.zeros_like(acc_ref)
        dkv_j = dkv_ref[...].astype(BF16)
        dw_ref[...] = _mm_tn(memn_ref[...], dkv_j).astype(BF16)
        acc_ref[...] += _mm_nt(dkv_j, w_ref[...])

        @pl.when(j == 2 * NH - 1)
        def _():
            dgain_ref[...] = _norm_bwd(acc_ref[...], mem_ref[...], g_ref[...])[1]

    row = pl.BlockSpec((1, D), lambda j: (0, 0))
    return pl.pallas_call(
        body, name="bwd_kv", grid=(2 * NH,),
        in_specs=[pl.BlockSpec((None, M, HD), lambda j: (j, 0, 0)), pl.BlockSpec((M, D), lambda j: (0, 0)),
                  pl.BlockSpec((None, D, HD), lambda j: (j, 0, 0)), pl.BlockSpec((M, D), lambda j: (0, 0)), row],
        out_specs=[pl.BlockSpec((None, D, HD), lambda j: (j, 0, 0)), row],
        out_shape=[SDS((2 * NH, D, HD), BF16), SDS((1, D), F32)],
        scratch_shapes=[pltpu.VMEM((M, D), F32)],
        compiler_params=_params(1))(dkv, memn, w_kv_g, mem, gain)


def _adamw_math(w, g, m, v):
    m = ADAM_B1 * m + (1.0 - ADAM_B1) * g
    v = ADAM_B2 * v + (1.0 - ADAM_B2) * (g * g)
    m_hat = m / (1.0 - ADAM_B1 ** ADAM_STEP)
    v_hat = v / (1.0 - ADAM_B2 ** ADAM_STEP)
    delta = -ADAM_LR * (m_hat / (jnp.sqrt(v_hat) + ADAM_EPS) + ADAM_WD * w)
    return delta, m, v


def _row_tile(rows):
    return 256 if rows % 256 == 0 else rows


def _sum_parts(parts, name):
    n_parts, rows, cols = parts.shape
    tr = _row_tile(rows)

    def body(p_ref, g_ref):
        g = p_ref[0].astype(F32)
        for k in range(1, n_parts):
            g = g + p_ref[k].astype(F32)
        g_ref[...] = g

    blk = pl.BlockSpec((tr, cols), lambda i: (i, 0))
    return pl.pallas_call(
        body, name=name, grid=(rows // tr,),
        in_specs=[pl.BlockSpec((n_parts, tr, cols), lambda i: (0, i, 0))], out_specs=blk,
        out_shape=SDS((rows, cols), F32), compiler_params=_params(1))(parts)


def _adamw(w, g, m, v, name, from_parts):
    rows, cols = w.shape
    tr = _row_tile(rows)

    def body(w_ref, g_ref, m_ref, v_ref, go_ref, d_ref, mo_ref, vo_ref):
        if from_parts:
            g = g_ref[0].astype(F32)
            for k in range(1, g_ref.shape[0]):
                g = g + g_ref[k].astype(F32)
        else:
            g = g_ref[...]
        go_ref[...] = g
        d_ref[...], mo_ref[...], vo_ref[...] = _adamw_math(w_ref[...], g, m_ref[...], v_ref[...])

    blk = pl.BlockSpec((tr, cols), lambda i: (i, 0))
    g_spec = pl.BlockSpec((g.shape[0], tr, cols), lambda i: (0, i, 0)) if from_parts else blk
    return pl.pallas_call(
        body, name=name, grid=(rows // tr,),
        in_specs=[blk, g_spec, blk, blk], out_specs=[blk] * 4,
        out_shape=[SDS((rows, cols), F32)] * 4, compiler_params=_params(1))(w, g, m, v)


def _peer(k, x, y, c):
    return ((1 - x) if k & 4 else x, (1 - y) if k & 2 else y, (1 - c) if k & 1 else c)


def _exchange(arrays, name, scatter):
    n = len(arrays)

    def body(*refs):
        ins, outs = refs[:n], refs[n:2 * n]
        send_sems, recv_sems, local_sems = refs[2 * n:]
        x, y, c = (lax.axis_index(a) for a in AXES)
        me = 4 * x + 2 * y + c

        def remote(a, k):
            px, py, pc = _peer(k, x, y, c)
            there = 4 * px + 2 * py + pc
            return pltpu.make_async_remote_copy(
                src_ref=ins[a].at[there] if scatter else ins[a], dst_ref=outs[a].at[me],
                send_sem=send_sems.at[a, k - 1], recv_sem=recv_sems.at[a, k - 1],
                device_id=(px, py, pc), device_id_type=pl.DeviceIdType.MESH)

        def arrival(a, k):
            px, py, pc = _peer(k, x, y, c)
            there = 4 * px + 2 * py + pc
            return pltpu.make_async_remote_copy(
                src_ref=ins[a].at[there] if scatter else ins[a], dst_ref=outs[a].at[there],
                send_sem=send_sems.at[a, k - 1], recv_sem=recv_sems.at[a, k - 1],
                device_id=(px, py, pc), device_id_type=pl.DeviceIdType.MESH)

        own = [pltpu.make_async_copy(ins[a].at[me] if scatter else ins[a], outs[a].at[me], local_sems.at[a]) for a in range(n)]
        for a in range(n):
            own[a].start()
            for k in range(1, NDEV):
                remote(a, k).start()
        for a in range(n):
            for k in range(1, NDEV):
                arrival(a, k).wait_recv()
        for a in range(n):
            for k in range(1, NDEV):
                remote(a, k).wait_send()
            own[a].wait()

    out_shape = [SDS(a.shape if scatter else (NDEV,) + a.shape, a.dtype) for a in arrays]
    return pl.pallas_call(
        body, name=name, in_specs=[HBM] * n, out_specs=[HBM] * n, out_shape=out_shape,
        scratch_shapes=[pltpu.SemaphoreType.DMA((n, NDEV - 1)), pltpu.SemaphoreType.DMA((n, NDEV - 1)),
                        pltpu.SemaphoreType.DMA((n,))],
        compiler_params=pltpu.CompilerParams(has_side_effects=True))(*arrays)


SEM = pl.BlockSpec(memory_space=pltpu.SEMAPHORE)
IN_HBM = pl.BlockSpec(memory_space=pltpu.HBM)
DATAFLOW = pltpu.SideEffectType.DATAFLOW_SIDE_EFFECTING
TOKEN_SHAPE = (8, 128)


OTHER_CHIPS = (2, 4, 6)


def _place(x, y, c):
    return 4 * x + 2 * y + c


def _plan_gather_chips(n):
    def plan(refs, x, y, c, arriving):
        out = []
        for a in range(n):
            for k in (1,) + OTHER_CHIPS:
                there = _place(*_peer(k, x, y, c))
                out.append((refs[a], refs[n + a].at[there if arriving else _place(x, y, c)], k))
        return out
    return plan, n * 4


def _plan_gather_sibling(n):
    def plan(refs, x, y, c, arriving):
        out = []
        for a in range(n):
            for k in OTHER_CHIPS:
                px, py, pc = _peer(k, x, y, c)
                mine, theirs = _place(px, py, pc), _place(px, py, 1 - pc)
                out.append((refs[a].at[mine], refs[a].at[theirs if arriving else mine], 1))
        return out
    return plan, n * 3


def _plan_scatter_sibling(n):
    def plan(refs, x, y, c, arriving):
        out = []
        for a in range(n):
            for q in range(4):
                out.append((refs[a].at[2 * q + (1 - c)], refs[n + a].at[q], 1))
        return out
    return plan, n * 4


def _plan_scatter_chips(n):
    def plan(refs, x, y, c, arriving):
        out = []
        for a in range(n):
            for k in OTHER_CHIPS:
                px, py, _ = _peer(k, x, y, c)
                out.append((refs[a].at[2 * px + py], refs[n + a].at[(2 * px + py) if arriving else (2 * x + y)], k))
        return out
    return plan, n * 3


def _remote(src, dst, send_sems, recv_sems, i, k):
    x, y, c = (lax.axis_index(n) for n in AXES)
    return pltpu.make_async_remote_copy(src_ref=src, dst_ref=dst, send_sem=send_sems.at[i], recv_sem=recv_sems.at[i],
                                        device_id=_peer(k, x, y, c), device_id_type=pl.DeviceIdType.MESH)


def _copies_start(groups, name, after):
    ng = len(groups)
    total = sum(len(bufs) for bufs, _ in groups)

    def body(*refs):
        sems = refs[1 + total:1 + total + 2 * ng]
        x, y, c = (lax.axis_index(n) for n in AXES)
        off = 1
        for gi, (bufs, (plan, _)) in enumerate(groups):
            for i, (src, dst, k) in enumerate(plan(refs[off:off + len(bufs)], x, y, c, False)):
                _remote(src, dst, sems[2 * gi], sems[2 * gi + 1], i, k).start()
            off += len(bufs)
        refs[-1][...] = jnp.zeros(TOKEN_SHAPE, F32)

    sem_shapes = [pltpu.SemaphoreType.DMA((count,)) for _, (_, count) in groups for _ in range(2)]
    flat = [b for bufs, _ in groups for b in bufs]
    outs = pl.pallas_call(
        body, name=name,
        in_specs=[HBM] + [IN_HBM] * total,
        out_specs=[SEM] * (2 * ng) + [IN_HBM] * total + [pl.BlockSpec(memory_space=pltpu.VMEM)],
        out_shape=sem_shapes + [pltpu.HBM(b.shape, b.dtype) for b in flat] + [SDS(TOKEN_SHAPE, F32)],
        input_output_aliases={1 + i: 2 * ng + i for i in range(total)},
        compiler_params=pltpu.CompilerParams(has_side_effects=DATAFLOW),
    )(after, *[pltpu.with_memory_space_constraint(b, pltpu.HBM) for b in flat])
    handles, off = [], 2 * ng
    for gi, (bufs, _) in enumerate(groups):
        handles.append((outs[2 * gi], outs[2 * gi + 1], list(outs[off:off + len(bufs)])))
        off += len(bufs)
    return handles, outs[-1]


def _copies_wait(handle, plan, name, after):
    send_sems, recv_sems, bufs = handle
    n = len(bufs)

    def body(*refs):
        x, y, c = (lax.axis_index(a) for a in AXES)
        for i, (src, dst, k) in enumerate(plan[0](refs[:n], x, y, c, True)):
            copy = _remote(src, dst, refs[n], refs[n + 1], i, k)
            copy.wait_send()
            copy.wait_recv()

    return pl.pallas_call(
        body, name=name,
        in_specs=[IN_HBM] * n + [SEM, SEM, HBM], out_specs=[IN_HBM] * n,
        out_shape=[pltpu.HBM(b.shape, b.dtype) for b in bufs],
        input_output_aliases={i: i for i in range(n)},
        compiler_params=pltpu.CompilerParams(has_side_effects=DATAFLOW),
    )(*bufs, send_sems, recv_sems, after)


def _pair_sum(mine, theirs, c, name):
    _, rows, cols = theirs.shape
    tr = _row_tile(rows)

    def body(c_ref, m_ref, t_ref, o_ref):
        del c_ref
        o_ref[...] = (m_ref[...].astype(F32) + t_ref[...].astype(F32)).astype(BF16)

    blk = pl.BlockSpec((None, tr, cols), lambda q, i, c_ref: (q, i, 0))
    return pl.pallas_call(
        body, name=name,
        grid_spec=pltpu.PrefetchScalarGridSpec(
            num_scalar_prefetch=1, grid=(4, rows // tr),
            in_specs=[pl.BlockSpec((None, tr, cols), lambda q, i, c_ref: (2 * q + c_ref[0], i, 0)), blk], out_specs=blk),
        out_shape=SDS(theirs.shape, BF16), compiler_params=_params(2))(c.reshape(1), mine, theirs)


def _local_step(x, mem, target, gains, get, put, tm_big=1024, tm_mid=512, tm_small=256):
    g_mix, pscale, g_mem, g_ffn, g_fin = gains
    T = x.shape[0]
    tm_big, tm_mid, tm_small = min(tm_big, T), min(tm_mid, T), min(tm_small, T)
    tn = DFF // 2

    w_in = get("in", x)
    proj, h = _fwd_proj(x, g_mix, w_in, tm_big)
    cw0, cw1, cw2, w_co, w_pool, w_kv = get("mix", proj)
    kv, memn = _fwd_kv(mem, g_mem, w_kv)
    za, conv, pooled, ya, yp = _fwd_mix(proj, cw0, cw1, cw2, w_co, w_pool, tm_small)
    w_xo, w_o = get("merge", ya)
    o, yx, merged, x1, h2 = _fwd_merge(proj, ya, yp, x, kv, w_xo, w_o, pscale, g_ffn, tm_small)
    wg_t, wu_t, w_d = get("ffn", x1)
    gate, up, act = _fwd_ffn_up(h2, wg_t, wu_t, tm_mid, tn)
    dx2, loss, dg_fin = _fwd_ffn_down_loss(act, w_d, x1, target, g_fin, tm_small)

    dgate, dup = _bwd_ffn_down(dx2, w_d, gate, up, tm_mid, tn)
    dx1, dg_ffn = _bwd_ffn_up(dgate, dup, wg_t, wu_t, x1, dx2, g_ffn, tm_small)
    dw_d = _wgrad_dense(act, dx2, "wgrad_down", tm_mid, a_cols=tn)
    dwg_t = _wgrad_dense(dgate, h2, "wgrad_gate", tm_mid, a_cols=tn)
    dwu_t = _wgrad_dense(dup, h2, "wgrad_up", tm_mid, a_cols=tn)
    zero = put("ffn", (dwg_t, dwu_t, dw_d))

    dproj, dya, dyx, dyps, dza, do, dpooled, dpscale = _bwd_merge(
        dx1, proj, ya, yp, yx, pscale + zero, w_o, w_co, w_xo, w_pool, tm_small)
    dw_o = _wgrad_dense(merged, dx1, "wgrad_out", tm_mid)
    dw_co = _wgrad_dense(za, dya, "wgrad_conv_out", tm_mid)
    dw_xo = _wgrad_dense(o, dyx, "wgrad_xattn_out", tm_mid)
    dw_pool = _wgrad(pooled, dyps[None], name="wgrad_pool", groups=NPOOL, a_cols=HD, b_cols=HD, tt=tm_mid,
                     a_index=lambda g, k, t: (t, g), b_index=lambda g, k, t: (0, t, g),
                     o_index=lambda g, k, t: (g, 0, 0), out_shape=(NPOOL, HD, HD))
    dproj, dkv = _bwd_attn(dproj, proj, do, kv, tm_small)
    dw_kv, dg_mem = _bwd_kv(dkv, memn, w_kv, mem, g_mem)
    zero = put("mix", (dw_co, dw_xo, dw_o, dw_pool, dw_kv))

    dproj, dcw = _bwd_mix(dproj, proj, conv, dza, dpooled, cw0 + zero, cw1, cw2, tm_small)
    dw_in = _wgrad(h, dproj, name="wgrad_in", groups=NSPLIT, a_cols=D, b_cols=D, tt=tm_mid,
                   a_index=lambda g, k, t: (t, 0), b_index=lambda g, k, t: (g, t, 0),
                   o_index=lambda g, k, t: (_slot_group(g), 0, 0), out_shape=(NSPLIT, D, D))
    zero = put("in", (dw_in,))
    grad_x, dg_mix = _bwd_proj(dproj, w_in, x, dx1, g_mix + zero, tm_big)

    small = jnp.concatenate([dg_mix, dpscale, dg_mem, dg_ffn, dg_fin, dcw[0:3]], axis=0)
    return loss[0, 0], grad_x, small


def kernel(x, mem, norm_mix, w_in, conv_w, w_conv_out, w_pool, pool_scale, norm_mem, w_kv, w_xattn_out, w_out, norm_ffn, w_gate, w_up, w_down, norm_final, loss_target, m_norm_mix, m_w_in, m_conv_w, m_w_conv_out, m_w_pool, m_pool_scale, m_norm_mem, m_w_kv, m_w_xattn_out, m_w_out, m_norm_ffn, m_w_gate, m_w_up, m_w_down, m_norm_final, v_norm_mix, v_w_in, v_conv_w, v_w_conv_out, v_w_pool, v_pool_scale, v_norm_mem, v_w_kv, v_w_xattn_out, v_w_out, v_norm_ffn, v_w_gate, v_w_up, v_w_down, v_norm_final):
    T = x.shape[1]
    rows = D // NDEV
    ffb = DFF // NDEV
    prow = HD // NDEV
    me = 4 * lax.axis_index("x") + 2 * lax.axis_index("y") + lax.axis_index("c")

    shards = [w_in[0].astype(BF16), w_conv_out[0].astype(BF16), w_xattn_out[0].astype(BF16), w_out[0].astype(BF16),
              w_pool[0].astype(BF16).reshape(NPOOL * prow, HD), w_kv[0].astype(BF16),
              w_gate[0].T.astype(BF16), w_up[0].T.astype(BF16), w_down[0].astype(BF16),
              jnp.pad(conv_w[0], ((0, 5), (0, 0)))]

    cx, cy, cc = (lax.axis_index(n) for n in AXES)
    chip = 2 * cx + cy

    def land(own, index, slots):
        return lax.dynamic_update_index_in_dim(lax.empty((slots,) + own.shape, own.dtype), own, index, 0)

    needed = ["in", "mix", "merge", "ffn"]
    members = {"in": [0], "mix": [9, 1, 4, 5], "merge": [2, 3], "ffn": [6, 7, 8]}
    g_handles, _ = _copies_start(
        [([shards[i] for i in members[n]] + [land(shards[i], me, NDEV) for i in members[n]], _plan_gather_chips(len(members[n])))
         for n in needed], "gather_start", x)

    def get(group, after):
        n = len(members[group])
        bufs = _copies_wait(g_handles[needed.index(group)], _plan_gather_chips(n), "gather_wait_" + group, after)
        (handle,), token = _copies_start([(list(bufs[n:]), _plan_gather_sibling(n))], "gather_pass_" + group, norm_mix)
        got = _copies_wait(handle, _plan_gather_sibling(n), "gather_passed_" + group, token)
        if group == "in":
            return got[0]
        if group == "mix":
            cw_g, w_co_g, w_pool_g, w_kv_g = got
            cw_full = cw_g.transpose(1, 0, 2).reshape(8, D)
            w_pool_full = w_pool_g.reshape(NDEV, NPOOL, prow, HD).transpose(1, 0, 2, 3).reshape(NPOOL, HD, HD)
            return cw_full[0:1], cw_full[1:2], cw_full[2:3], w_co_g.reshape(D, D), w_pool_full, w_kv_g
        if group == "merge":
            return got[0].reshape(D, D), got[1].reshape(D, D)
        return got[0].reshape(DFF, D), got[1].reshape(DFF, D), got[2].reshape(DFF, D)

    started = {}

    def put(group, grads):
        if group == "ffn":
            sends = [g.reshape(NDEV, ffb, D) for g in grads]
        elif group == "mix":
            dw_co, dw_xo, dw_o, dw_pool, dw_kv = grads
            sends = [dw_co.reshape(NDEV, rows, D), dw_xo.reshape(NDEV, rows, D), dw_o.reshape(NDEV, rows, D),
                     dw_pool.reshape(NPOOL, NDEV, prow, HD).transpose(1, 0, 2, 3).reshape(NDEV, NPOOL * prow, HD), dw_kv]
        else:
            sends = list(grads)
        zero = jnp.zeros((1, 1), F32)
        if swapping:
            zero = zero + add_and_send(sends[0])
        n = len(sends)
        halves = [lax.empty((4,) + s.shape[1:], s.dtype) for s in sends]
        (handle,), token = _copies_start([(sends + halves, _plan_scatter_sibling(n))], "scatter_swap_" + group, norm_mix)
        swapping.append((group, handle, n))
        zero = zero + token[0:1, 0:1]
        if group == "in":
            zero = zero + add_and_send(token)
        return zero

    swapping = []

    def add_and_send(after):
        group, handle, n = swapping.pop()
        bufs = _copies_wait(handle, _plan_scatter_sibling(n), "scatter_swapped_" + group, after)
        sums = [_pair_sum(bufs[a], bufs[n + a], cc, "pair_sum_%s_%d" % (group, a)) for a in range(n)]
        lands = [land(lax.dynamic_index_in_dim(s, chip, 0, keepdims=False), chip, 4) for s in sums]
        (handle,), token = _copies_start([(sums + lands, _plan_scatter_chips(n))], "scatter_start_" + group, norm_mix)
        started[group] = (handle, _plan_scatter_chips(n))
        return token[0:1, 0:1]

    def take(group, after):
        handle, plan = started[group]
        return _copies_wait(handle, plan, "scatter_wait_" + group, after)[len(handle[2]) // 2:]

    gains = (norm_mix, pool_scale, norm_mem, norm_ffn, norm_final.reshape(1, D))
    loss_part, grad_x, small = _local_step(x[0], mem[0], loss_target[0], gains, get, put)

    (small_all,) = _exchange([small], "gather_small", scatter=False)
    small_sum = _sum_parts(small_all, "sum_small")

    loss = lax.psum(loss_part, AXES)

    def sharded(name, w, parts, m, v):
        shape = w.shape
        flat = lambda a: a.reshape(parts.shape[1], parts.shape[2])
        outs = _adamw(flat(w), parts, flat(m), flat(v), "adamw_" + name, from_parts=True)
        return [o.reshape(shape) for o in outs]

    def transposed(name, w, parts, m, v):
        outs = _adamw(w[0].T, parts, m[0].T, v[0].T, "adamw_" + name, from_parts=True)
        return [o.T[None] for o in outs]

    def replicated(name, w, g, m, v):
        shape = w.shape
        flat = lambda a: a.reshape(g.shape)
        outs = _adamw(flat(w), g, flat(m), flat(v), "adamw_" + name, from_parts=False)
        return [o.reshape(shape) for o in outs]

    g_cw = lax.dynamic_slice_in_dim(small_sum[5:8], me * rows, rows, axis=1)
    res = {
        "norm_mix": replicated("norm_mix", norm_mix, small_sum[0:1], m_norm_mix, v_norm_mix),
        "conv_w": replicated("conv_w", conv_w, g_cw, m_conv_w, v_conv_w),
        "pool_scale": replicated("pool_scale", pool_scale, small_sum[1:2], m_pool_scale, v_pool_scale),
        "norm_mem": replicated("norm_mem", norm_mem, small_sum[2:3], m_norm_mem, v_norm_mem),
        "norm_ffn": replicated("norm_ffn", norm_ffn, small_sum[3:4], m_norm_ffn, v_norm_ffn),
        "norm_final": replicated("norm_final", norm_final, small_sum[4:5], m_norm_final, v_norm_final),
    }
    p_g, p_u, p_d = take("ffn", res["norm_final"][1])
    res["w_gate"] = transposed("w_gate", w_gate, p_g, m_w_gate, v_w_gate)
    res["w_up"] = transposed("w_up", w_up, p_u, m_w_up, v_w_up)
    res["w_down"] = sharded("w_down", w_down, p_d, m_w_down, v_w_down)
    p_co, p_xo, p_o, p_pool, p_kv = take("mix", res["w_down"][1])
    res["w_conv_out"] = sharded("w_conv_out", w_conv_out, p_co, m_w_conv_out, v_w_conv_out)
    res["w_pool"] = sharded("w_pool", w_pool, p_pool, m_w_pool, v_w_pool)
    res["w_kv"] = sharded("w_kv", w_kv, p_kv, m_w_kv, v_w_kv)
    res["w_xattn_out"] = sharded("w_xattn_out", w_xattn_out, p_xo, m_w_xattn_out, v_w_xattn_out)
    res["w_out"] = sharded("w_out", w_out, p_o, m_w_out, v_w_out)
    (p_in,) = take("in", res["w_out"][1])
    res["w_in"] = sharded("w_in", w_in, p_in, m_w_in, v_w_in)
    order = ["norm_mix", "w_in", "conv_w", "w_conv_out", "w_pool", "pool_scale", "norm_mem", "w_kv", "w_xattn_out", "w_out",
             "norm_ffn", "w_gate", "w_up", "w_down", "norm_final"]
    return (loss, grad_x[None], *[res[n][0] for n in order], *[res[n][1] for n in order],
            *[res[n][2] for n in order], *[res[n][3] for n in order])
```

```python
import jax
import jax.numpy as jnp
from jax import lax
from jax.experimental import pallas as pl
from jax.experimental.pallas import tpu as pltpu

F32 = jnp.float32
BF16 = jnp.bfloat16
SDS = jax.ShapeDtypeStruct

AXES = ("x", "y", "c")
NDEV = 8
D = 1024
NSPLIT = 8
NH = 4
HD = D // NH
NPOOL = 4
DFF = 2816
EPS = 1e-6
ATT_SCALE = HD ** -0.5
HALO = 16


def _slot_group(s):
    return jnp.where(s < 3, s + 5, jnp.where(s == 3, 4, s - 4))


ADAM_LR = 0.001
ADAM_B1 = 0.9
ADAM_B2 = 0.999
ADAM_EPS = 1e-08
ADAM_WD = 0.01
ADAM_STEP = 10

V7X_VMEM_BYTES = 64 * 1024 * 1024
VMEM_LIMIT = V7X_VMEM_BYTES - 8 * 1024 * 1024
HBM = pl.BlockSpec(memory_space=pl.ANY)


def _params(n_grid):
    return pltpu.CompilerParams(dimension_semantics=("arbitrary",) * n_grid, vmem_limit_bytes=VMEM_LIMIT)


def _mm(a, b):
    return jnp.dot(a, b, preferred_element_type=F32)


def _mm_nt(a, b):
    return lax.dot_general(a, b, (((1,), (1,)), ((), ())), preferred_element_type=F32)


def _mm_tn(a, b):
    return lax.dot_general(a, b, (((0,), (0,)), ((), ())), preferred_element_type=F32)


def _sigmoid(x):
    return 1.0 / (1.0 + jnp.exp(-x))


def _rms(x):
    return lax.rsqrt(jnp.mean(x * x, axis=-1, keepdims=True) + EPS)


def _norm_bwd(dh, x, gain):
    r = _rms(x)
    xh = x * r
    dxh = dh * gain
    dx = r * (dxh - xh * jnp.mean(dxh * xh, axis=-1, keepdims=True))
    return dx, jnp.sum(dh * xh, axis=0, keepdims=True)


def _col_chunks(n, width=512):
    return [slice(c, min(c + width, n)) for c in range(0, n, width)]


def _shift_down(v, k):
    return pltpu.roll(v, k, 0)


def _shift_up(v, k):
    return pltpu.roll(v, v.shape[0] - k, 0)


def _fwd_proj(x, gain, w_in_g, tm):
    T = x.shape[0]

    def body(x_ref, g_ref, w_ref, proj_ref, h_ref):
        @pl.when(pl.program_id(1) == 0)
        def _():
            xf = x_ref[...]
            h_ref[...] = (xf * _rms(xf) * g_ref[...]).astype(BF16)
        proj_ref[...] = _mm(h_ref[...], w_ref[...]).astype(BF16)

    return pl.pallas_call(
        body, name="fwd_proj", grid=(T // tm, NSPLIT),
        in_specs=[pl.BlockSpec((tm, D), lambda i, j: (i, 0)), pl.BlockSpec((1, D), lambda i, j: (0, 0)),
                  pl.BlockSpec((None, D, D), lambda i, j: (j, 0, 0))],
        out_specs=[pl.BlockSpec((None, tm, D), lambda i, j: (j, i, 0)), pl.BlockSpec((tm, D), lambda i, j: (i, 0))],
        out_shape=[SDS((NSPLIT, T, D), BF16), SDS((T, D), BF16)],
        compiler_params=_params(2))(x, gain, w_in_g)


def _fwd_kv(mem, gain, w_kv_g):
    M = mem.shape[0]

    def body(mem_ref, g_ref, w_ref, kv_ref, memn_ref):
        @pl.when(pl.program_id(0) == 0)
        def _():
            m = mem_ref[...]
            memn_ref[...] = (m * _rms(m) * g_ref[...]).astype(BF16)
        kv_ref[...] = _mm(memn_ref[...], w_ref[...]).astype(BF16)

    return pl.pallas_call(
        body, name="fwd_kv", grid=(2 * NH,),
        in_specs=[pl.BlockSpec((M, D), lambda j: (0, 0)), pl.BlockSpec((1, D), lambda j: (0, 0)),
                  pl.BlockSpec((None, D, HD), lambda j: (j, 0, 0))],
        out_specs=[pl.BlockSpec((None, M, HD), lambda j: (j, 0, 0)), pl.BlockSpec((M, D), lambda j: (0, 0))],
        out_shape=[SDS((2 * NH, M, HD), BF16), SDS((M, D), BF16)],
        compiler_params=_params(1))(mem, gain, w_kv_g)


def _halo_before(split, tm):
    return pl.BlockSpec((None, HALO, D), lambda i: (split, jnp.maximum(i * (tm // HALO) - 1, 0), 0))


def _fwd_mix(proj, cw0, cw1, cw2, w_co, w_pool, tm):
    T = proj.shape[1]

    def body(b_ref, c_ref, ua_ref, up_ref, ch_ref, uah_ref, uph_ref, cw0_ref, cw1_ref, cw2_ref, wco_ref, wp_ref,
             za_ref, conv_ref, pooled_ref, ya_ref, yp_ref):
        i = pl.program_id(0)
        keep = jnp.where(i > 0, 1.0, 0.0).astype(F32)
        cu = c_ref[...].astype(F32) * ua_ref[...].astype(F32)
        cu_h = ch_ref[...].astype(F32) * uah_ref[...].astype(F32) * keep
        ext = jnp.concatenate([cu_h, cu], axis=0)
        conv = (cw2_ref[...] * ext + cw1_ref[...] * _shift_down(ext, 1) + cw0_ref[...] * _shift_down(ext, 2))[HALO:]
        za = (b_ref[...].astype(F32) * conv).astype(BF16)
        conv_ref[...] = conv.astype(BF16)
        za_ref[...] = za
        ya_ref[...] = _mm(za, wco_ref[...]).astype(BF16)

        up = up_ref[...].astype(F32)
        ext_u = jnp.concatenate([uph_ref[...].astype(F32) * keep, up], axis=0)
        pos = i * tm + lax.broadcasted_iota(jnp.int32, (tm, HD), 0)
        for g in range(NPOOL):
            cols = slice(g * HD, (g + 1) * HD)
            s = ext_u[:, cols]
            for k in range(g + 1):
                s = s + _shift_down(s, 1 << k)
            cnt = jnp.minimum(pos + 1, 2 << g).astype(F32)
            pooled = (s[HALO:] / cnt - up[:, cols]).astype(BF16)
            pooled_ref[:, cols] = pooled
            yp_ref[:, cols] = _mm(pooled, wp_ref[g]).astype(BF16)

    tile = lambda s: pl.BlockSpec((None, tm, D), lambda i: (s, i, 0))
    row = pl.BlockSpec((1, D), lambda i: (0, 0))
    out = pl.BlockSpec((tm, D), lambda i: (i, 0))
    return pl.pallas_call(
        body, name="fwd_mix", grid=(T // tm,),
        in_specs=[tile(0), tile(1), tile(2), tile(3), _halo_before(1, tm), _halo_before(2, tm), _halo_before(3, tm),
                  row, row, row, pl.BlockSpec((D, D), lambda i: (0, 0)), pl.BlockSpec((NPOOL, HD, HD), lambda i: (0, 0, 0))],
        out_specs=[out] * 5,
        out_shape=[SDS((T, D), BF16)] * 5,
        compiler_params=_params(1))(proj, proj, proj, proj, proj, proj, proj, cw0, cw1, cw2, w_co, w_pool)


def _softmax_rows(s):
    e = jnp.exp(s - jnp.max(s, axis=-1, keepdims=True))
    return e / jnp.sum(e, axis=-1, keepdims=True)


def _fwd_merge(proj, ya, yp, x, kv, w_xo, w_o, pscale, gain_ffn, tm):
    T = x.shape[0]

    def body(q_ref, ga_ref, gp_ref, gx_ref, ya_ref, yp_ref, x_ref, kv_ref, wxo_ref, wo_ref, ps_ref, gf_ref,
             o_ref, yx_ref, merged_ref, x1_ref, h2_ref):
        for h in range(NH):
            cols = slice(h * HD, (h + 1) * HD)
            p = _softmax_rows(_mm_nt(q_ref[:, cols], kv_ref[h]) * ATT_SCALE)
            o_ref[:, cols] = _mm(p.astype(BF16), kv_ref[NH + h]).astype(BF16)
        yx = _mm(o_ref[...], wxo_ref[...])
        yx_ref[...] = yx.astype(BF16)
        merged = (_sigmoid(ga_ref[...].astype(F32)) * ya_ref[...].astype(F32)
                  + _sigmoid(gp_ref[...].astype(F32)) * (yp_ref[...].astype(F32) * ps_ref[...])
                  + _sigmoid(gx_ref[...].astype(F32)) * yx).astype(BF16)
        merged_ref[...] = merged
        x1 = x_ref[...] + _mm(merged, wo_ref[...])
        x1_ref[...] = x1
        h2_ref[...] = (x1 * _rms(x1) * gf_ref[...]).astype(BF16)

    tile = lambda s: pl.BlockSpec((None, tm, D), lambda i: (s, i, 0))
    row = pl.BlockSpec((1, D), lambda i: (0, 0))
    act = pl.BlockSpec((tm, D), lambda i: (i, 0))
    full = pl.BlockSpec((D, D), lambda i: (0, 0))
    return pl.pallas_call(
        body, name="fwd_merge", grid=(T // tm,),
        in_specs=[tile(4), tile(5), tile(6), tile(7), act, act, act,
                  pl.BlockSpec((2 * NH, kv.shape[1], HD), lambda i: (0, 0, 0)), full, full, row, row],
        out_specs=[act] * 5,
        out_shape=[SDS((T, D), BF16), SDS((T, D), BF16), SDS((T, D), BF16), SDS((T, D), F32), SDS((T, D), BF16)],
        compiler_params=_params(1))(proj, proj, proj, proj, ya, yp, x, kv, w_xo, w_o, pscale, gain_ffn)


def _fwd_ffn_up(h2, wg_t, wu_t, tm, tn):
    T = h2.shape[0]

    def body(h_ref, wg_ref, wu_ref, gate_ref, up_ref, act_ref):
        for cols in _col_chunks(tn):
            gate = _mm_nt(h_ref[...], wg_ref[cols, :])
            up = _mm_nt(h_ref[...], wu_ref[cols, :])
            gate_ref[:, cols] = gate.astype(BF16)
            up_ref[:, cols] = up.astype(BF16)
            act_ref[:, cols] = (gate * _sigmoid(gate) * up).astype(BF16)

    w = pl.BlockSpec((tn, D), lambda n, i: (n, 0))
    o = pl.BlockSpec((tm, tn), lambda n, i: (i, n))
    return pl.pallas_call(
        body, name="fwd_ffn_up", grid=(DFF // tn, T // tm),
        in_specs=[pl.BlockSpec((tm, D), lambda n, i: (i, 0)), w, w],
        out_specs=[o] * 3, out_shape=[SDS((T, DFF), BF16)] * 3,
        compiler_params=_params(2))(h2, wg_t, wu_t)


def _fwd_ffn_down_loss(act, w_d, x1, target, gain_final, tm):
    T = x1.shape[0]

    def body(act_ref, wd_ref, x1_ref, tgt_ref, g_ref, dx2_ref, loss_ref, dgain_ref):
        @pl.when(pl.program_id(0) == 0)
        def _():
            loss_ref[...] = jnp.zeros_like(loss_ref)
            dgain_ref[...] = jnp.zeros_like(dgain_ref)
        x2 = x1_ref[...] + _mm(act_ref[...], wd_ref[...])
        gain = g_ref[...]
        y = x2 * _rms(x2) * gain
        err = y - tgt_ref[...]
        loss_ref[...] += 0.5 * jnp.sum(jnp.mean(err * err, axis=-1, keepdims=True))
        dx2, dgain = _norm_bwd(err * (1.0 / D), x2, gain)
        dx2_ref[...] = dx2
        dgain_ref[...] += dgain

    act_spec = pl.BlockSpec((tm, D), lambda i: (i, 0))
    row = pl.BlockSpec((1, D), lambda i: (0, 0))
    return pl.pallas_call(
        body, name="fwd_ffn_down_loss", grid=(T // tm,),
        in_specs=[pl.BlockSpec((tm, DFF), lambda i: (i, 0)), pl.BlockSpec((DFF, D), lambda i: (0, 0)), act_spec, act_spec, row],
        out_specs=[act_spec, pl.BlockSpec((8, D), lambda i: (0, 0)), row],
        out_shape=[SDS((T, D), F32), SDS((8, D), F32), SDS((1, D), F32)],
        compiler_params=_params(1))(act, w_d, x1, target, gain_final)


def _bwd_ffn_down(dx2, w_d, gate, up, tm, tn):
    T = dx2.shape[0]

    def body(dx_ref, wd_ref, gate_ref, up_ref, dgate_ref, dup_ref):
        dx = dx_ref[...].astype(BF16)
        for cols in _col_chunks(tn):
            dact = _mm_nt(dx, wd_ref[cols, :])
            gate = gate_ref[:, cols].astype(F32)
            sg = _sigmoid(gate)
            dgate_ref[:, cols] = (dact * up_ref[:, cols].astype(F32) * (sg * (1.0 + gate * (1.0 - sg)))).astype(BF16)
            dup_ref[:, cols] = (dact * gate * sg).astype(BF16)

    o = pl.BlockSpec((tm, tn), lambda n, i: (i, n))
    return pl.pallas_call(
        body, name="bwd_ffn_down", grid=(DFF // tn, T // tm),
        in_specs=[pl.BlockSpec((tm, D), lambda n, i: (i, 0)), pl.BlockSpec((tn, D), lambda n, i: (n, 0)), o, o],
        out_specs=[o] * 2, out_shape=[SDS((T, DFF), BF16)] * 2,
        compiler_params=_params(2))(dx2, w_d, gate, up)


def _bwd_ffn_up(dgate, dup, wg_t, wu_t, x1, dx2, gain_ffn, tm):
    T = x1.shape[0]

    def body(dg_ref, du_ref, wg_ref, wu_ref, x1_ref, dx2_ref, g_ref, dx1_ref, dgain_ref):
        @pl.when(pl.program_id(0) == 0)
        def _():
            dgain_ref[...] = jnp.zeros_like(dgain_ref)
        dh2 = _mm(dg_ref[...], wg_ref[...]) + _mm(du_ref[...], wu_ref[...])
        dx, dgain = _norm_bwd(dh2, x1_ref[...], g_ref[...])
        dx1_ref[...] = dx2_ref[...] + dx
        dgain_ref[...] += dgain

    wide = pl.BlockSpec((tm, DFF), lambda i: (i, 0))
    w = pl.BlockSpec((DFF, D), lambda i: (0, 0))
    act = pl.BlockSpec((tm, D), lambda i: (i, 0))
    row = pl.BlockSpec((1, D), lambda i: (0, 0))
    return pl.pallas_call(
        body, name="bwd_ffn_up", grid=(T // tm,),
        in_specs=[wide, wide, w, w, act, act, row], out_specs=[act, row],
        out_shape=[SDS((T, D), F32), SDS((1, D), F32)],
        compiler_params=_params(1))(dgate, dup, wg_t, wu_t, x1, dx2, gain_ffn)


def _wgrad(a, b, *, name, groups, a_cols, b_cols, tt, a_index, b_index, o_index, out_shape, after):
    T = a.shape[0]
    nt = T // tt
    n_a = a.shape[1] // a_cols if groups == 1 else 1

    def body(a_ref, b_ref, after_ref, o_ref, acc_ref):
        del after_ref
        t = pl.program_id(2)

        @pl.when(t == 0)
        def _():
            acc_ref[...] = jnp.zeros_like(acc_ref)
        acc_ref[...] += _mm_tn(a_ref[...].astype(BF16), b_ref[...].astype(BF16))

        @pl.when(t == nt - 1)
        def _():
            o_ref[...] = acc_ref[...].astype(o_ref.dtype)

    return pl.pallas_call(
        body, name=name, grid=(groups, n_a, nt),
        in_specs=[pl.BlockSpec((tt, a_cols), a_index), pl.BlockSpec((None, tt, b_cols), b_index), HBM],
        out_specs=pl.BlockSpec((None, a_cols, b_cols), o_index),
        out_shape=SDS(out_shape, BF16),
        scratch_shapes=[pltpu.VMEM((a_cols, b_cols), F32)],
        compiler_params=_params(3))(a, b, after)


def _wgrad_dense(a, b, name, tt, after, a_cols=None):
    ka, nb = a.shape[1], b.shape[1]
    a_cols = ka if a_cols is None else a_cols
    out = _wgrad(a, b[None], name=name, groups=1, a_cols=a_cols, b_cols=nb, tt=tt,
                 a_index=lambda g, k, t: (t, k), b_index=lambda g, k, t: (0, t, 0),
                 o_index=lambda g, k, t: (k, 0, 0), out_shape=(ka // a_cols, a_cols, nb), after=after)
    return out.reshape(ka, nb)


def _bwd_merge(dx1, proj, ya, yp, yx, pscale, w_o, w_co, w_xo, w_pool, tm):
    T = dx1.shape[0]

    def body(dx1_ref, ga_ref, gp_ref, gx_ref, ya_ref, yp_ref, yx_ref, ps_ref, wo_ref, wco_ref, wxo_ref, wp_ref,
             dgates_ref, dya_ref, dyx_ref, dyps_ref, dza_ref, do_ref, dpooled_ref, dps_ref):
        @pl.when(pl.program_id(0) == 0)
        def _():
            dps_ref[...] = jnp.zeros_like(dps_ref)
        dmerged = _mm_nt(dx1_ref[...].astype(BF16), wo_ref[...])
        scale = ps_ref[...]
        sa, sp, sx = (_sigmoid(r[...].astype(F32)) for r in (ga_ref, gp_ref, gx_ref))
        ya, yp_pre, yx = (r[...].astype(F32) for r in (ya_ref, yp_ref, yx_ref))
        dgates_ref[0] = (dmerged * ya * sa * (1.0 - sa)).astype(BF16)
        dgates_ref[1] = (dmerged * (yp_pre * scale) * sp * (1.0 - sp)).astype(BF16)
        dgates_ref[2] = (dmerged * yx * sx * (1.0 - sx)).astype(BF16)
        dya = (dmerged * sa).astype(BF16)
        dyx = (dmerged * sx).astype(BF16)
        dyp = dmerged * sp
        dyps = (dyp * scale).astype(BF16)
        dps_ref[...] += jnp.sum(dyp * yp_pre, axis=0, keepdims=True)
        dya_ref[...] = dya
        dyx_ref[...] = dyx
        dyps_ref[...] = dyps
        dza_ref[...] = _mm_nt(dya, wco_ref[...]).astype(BF16)
        do_ref[...] = _mm_nt(dyx, wxo_ref[...]).astype(BF16)
        for g in range(NPOOL):
            cols = slice(g * HD, (g + 1) * HD)
            dpooled_ref[:, cols] = _mm_nt(dyps[:, cols], wp_ref[g]).astype(BF16)

    tile = lambda s: pl.BlockSpec((None, tm, D), lambda i: (s, i, 0))
    row = pl.BlockSpec((1, D), lambda i: (0, 0))
    act = pl.BlockSpec((tm, D), lambda i: (i, 0))
    full = pl.BlockSpec((D, D), lambda i: (0, 0))
    return pl.pallas_call(
        body, name="bwd_merge", grid=(T // tm,),
        in_specs=[act, tile(5), tile(6), tile(7), act, act, act, row, full, full, full,
                  pl.BlockSpec((NPOOL, HD, HD), lambda i: (0, 0, 0))],
        out_specs=[pl.BlockSpec((3, tm, D), lambda i: (0, i, 0))] + [act] * 6 + [row],
        out_shape=[SDS((NSPLIT, T, D), BF16)] + [SDS((T, D), BF16)] * 6 + [SDS((1, D), F32)],
        compiler_params=_params(1))(dx1, proj, proj, proj, ya, yp, yx, pscale, w_o, w_co, w_xo, w_pool)


def _bwd_attn(dproj, proj, do, kv, tm):
    T = do.shape[0]
    M = kv.shape[1]

    def body(dproj_hbm, q_ref, do_ref, kv_ref, dq_ref, dkv_ref):
        del dproj_hbm

        @pl.when(pl.program_id(0) == 0)
        def _():
            dkv_ref[...] = jnp.zeros_like(dkv_ref)
        for h in range(NH):
            cols = slice(h * HD, (h + 1) * HD)
            q = q_ref[:, cols]
            do_h = do_ref[:, cols]
            p = _softmax_rows(_mm_nt(q, kv_ref[h]) * ATT_SCALE)
            dp = _mm_nt(do_h, kv_ref[NH + h])
            ds = (p * (dp - jnp.sum(dp * p, axis=-1, keepdims=True)) * ATT_SCALE).astype(BF16)
            dq_ref[:, cols] = _mm(ds, kv_ref[h]).astype(BF16)
            dkv_ref[h] += _mm_tn(ds, q)
            dkv_ref[NH + h] += _mm_tn(p.astype(BF16), do_h)

    kv_spec = pl.BlockSpec((2 * NH, M, HD), lambda i: (0, 0, 0))
    return pl.pallas_call(
        body, name="bwd_attn", grid=(T // tm,),
        in_specs=[HBM, pl.BlockSpec((None, tm, D), lambda i: (4, i, 0)), pl.BlockSpec((tm, D), lambda i: (i, 0)), kv_spec],
        out_specs=[pl.BlockSpec((None, tm, D), lambda i: (3, i, 0)), kv_spec],
        out_shape=[SDS(dproj.shape, BF16), SDS((2 * NH, M, HD), F32)],
        input_output_aliases={0: 0},
        compiler_params=_params(1))(dproj, proj, do, kv)


def _bwd_mix(dproj, proj, conv, dza, dpooled, cw0, cw1, cw2, tm):
    T = dza.shape[0]
    nt = T // tm

    def halo_after(split_or_none):
        idx = lambda i: jnp.minimum((i + 1) * (tm // HALO), T // HALO - 1)
        if split_or_none is None:
            return pl.BlockSpec((HALO, D), lambda i: (idx(i), 0))
        return pl.BlockSpec((None, HALO, D), lambda i: (split_or_none, idx(i), 0))

    def body(dproj_hbm, b_ref, c_ref, ua_ref, conv_ref, dza_ref, dpo_ref, bn_ref, dzan_ref, dpon_ref, ch_ref, uah_ref,
             cw0_ref, cw1_ref, cw2_ref, dabcu_ref, dcw_ref):
        del dproj_hbm
        i = pl.program_id(0)

        @pl.when(i == 0)
        def _():
            dcw_ref[...] = jnp.zeros_like(dcw_ref)
        keep_prev = jnp.where(i > 0, 1.0, 0.0).astype(F32)
        keep_next = jnp.where(i < nt - 1, 1.0, 0.0).astype(F32)
        dza = dza_ref[...].astype(F32)
        c = c_ref[...].astype(F32)
        ua = ua_ref[...].astype(F32)
        dconv = dza * b_ref[...].astype(F32)
        dconv_n = dzan_ref[...].astype(F32) * bn_ref[...].astype(F32) * keep_next
        ext = jnp.concatenate([dconv, dconv_n], axis=0)
        dcu = (cw2_ref[...] * ext + cw1_ref[...] * _shift_up(ext, 1) + cw0_ref[...] * _shift_up(ext, 2))[:tm]
        dabcu_ref[0] = (dza * conv_ref[...].astype(F32)).astype(BF16)
        dabcu_ref[1] = (dcu * ua).astype(BF16)
        dabcu_ref[2] = (dcu * c).astype(BF16)

        cu = c * ua
        ext_cu = jnp.concatenate([ch_ref[...].astype(F32) * uah_ref[...].astype(F32) * keep_prev, cu], axis=0)
        dcw_ref[2:3, :] += jnp.sum(dconv * cu, axis=0, keepdims=True)
        dcw_ref[1:2, :] += jnp.sum(dconv * _shift_down(ext_cu, 1)[HALO:], axis=0, keepdims=True)
        dcw_ref[0:1, :] += jnp.sum(dconv * _shift_down(ext_cu, 2)[HALO:], axis=0, keepdims=True)

        dpo = dpo_ref[...].astype(F32)
        ext_dpo = jnp.concatenate([dpo, dpon_ref[...].astype(F32) * keep_next], axis=0)
        pos = i * tm + lax.broadcasted_iota(jnp.int32, (tm + HALO, HD), 0)
        for g in range(NPOOL):
            cols = slice(g * HD, (g + 1) * HD)
            s = ext_dpo[:, cols] / jnp.minimum(pos + 1, 2 << g).astype(F32)
            for k in range(g + 1):
                s = s + _shift_up(s, 1 << k)
            dabcu_ref[3, :, cols] = (s[:tm] - dpo[:, cols]).astype(BF16)

    tile = lambda s: pl.BlockSpec((None, tm, D), lambda i: (s, i, 0))
    act = pl.BlockSpec((tm, D), lambda i: (i, 0))
    row = pl.BlockSpec((1, D), lambda i: (0, 0))
    return pl.pallas_call(
        body, name="bwd_mix", grid=(nt,),
        in_specs=[HBM, tile(0), tile(1), tile(2), act, act, act, halo_after(0), halo_after(None), halo_after(None),
                  _halo_before(1, tm), _halo_before(2, tm), row, row, row],
        out_specs=[pl.BlockSpec((4, tm, D), lambda i: (1, i, 0)), pl.BlockSpec((8, D), lambda i: (0, 0))],
        out_shape=[SDS(dproj.shape, BF16), SDS((8, D), F32)],
        input_output_aliases={0: 0},
        compiler_params=_params(1))(dproj, proj, proj, proj, conv, dza, dpooled, proj, dza, dpooled, proj, proj, cw0, cw1, cw2)


def _bwd_proj(dproj, w_in_g, x, dx1, gain, tm):
    T = x.shape[0]

    def body(dp_ref, w_ref, x_ref, dx1_ref, g_ref, dx_ref, dgain_ref, acc_ref):
        i, s = pl.program_id(0), pl.program_id(1)

        @pl.when((i == 0) & (s == 0))
        def _():
            dgain_ref[...] = jnp.zeros_like(dgain_ref)

        @pl.when(s == 0)
        def _():
            acc_ref[...] = jnp.zeros_like(acc_ref)
        acc_ref[...] += _mm_nt(dp_ref[...], w_ref[...])

        @pl.when(s == NSPLIT - 1)
        def _():
            dx, dgain = _norm_bwd(acc_ref[...], x_ref[...], g_ref[...])
            dx_ref[...] = dx1_ref[...] + dx
            dgain_ref[...] += dgain

    act = pl.BlockSpec((tm, D), lambda i, s: (i, 0))
    row = pl.BlockSpec((1, D), lambda i, s: (0, 0))
    return pl.pallas_call(
        body, name="bwd_proj", grid=(T // tm, NSPLIT),
        in_specs=[pl.BlockSpec((None, tm, D), lambda i, s: (s, i, 0)),
                  pl.BlockSpec((None, D, D), lambda i, s: (_slot_group(s), 0, 0)), act, act, row],
        out_specs=[act, row], out_shape=[SDS((T, D), F32), SDS((1, D), F32)],
        scratch_shapes=[pltpu.VMEM((tm, D), F32)],
        compiler_params=_params(2))(dproj, w_in_g, x, dx1, gain)


def _bwd_kv(dkv, memn, w_kv_g, mem, gain):
    M = mem.shape[0]

    def body(dkv_ref, memn_ref, w_ref, mem_ref, g_ref, dw_ref, dgain_ref, acc_ref):
        j = pl.program_id(0)

        @pl.when(j == 0)
        def _():
            acc_ref[...] = jnp.zeros_like(acc_ref)
        dkv_j = dkv_ref[...].astype(BF16)
        dw_ref[...] = _mm_tn(memn_ref[...], dkv_j).astype(BF16)
        acc_ref[...] += _mm_nt(dkv_j, w_ref[...])

        @pl.when(j == 2 * NH - 1)
        def _():
            dgain_ref[...] = _norm_bwd(acc_ref[...], mem_ref[...], g_ref[...])[1]

    row = pl.BlockSpec((1, D), lambda j: (0, 0))
    return pl.pallas_call(
        body, name="bwd_kv", grid=(2 * NH,),
        in_specs=[pl.BlockSpec((None, M, HD), lambda j: (j, 0, 0)), pl.BlockSpec((M, D), lambda j: (0, 0)),
                  pl.BlockSpec((None, D, HD), lambda j: (j, 0, 0)), pl.BlockSpec((M, D), lambda j: (0, 0)), row],
        out_specs=[pl.BlockSpec((None, D, HD), lambda j: (j, 0, 0)), row],
        out_shape=[SDS((2 * NH, D, HD), BF16), SDS((1, D), F32)],
        scratch_shapes=[pltpu.VMEM((M, D), F32)],
        compiler_params=_params(1))(dkv, memn, w_kv_g, mem, gain)


def _adamw_math(w, g, m, v):
    m = ADAM_B1 * m + (1.0 - ADAM_B1) * g
    v = ADAM_B2 * v + (1.0 - ADAM_B2) * (g * g)
    m_hat = m / (1.0 - ADAM_B1 ** ADAM_STEP)
    v_hat = v / (1.0 - ADAM_B2 ** ADAM_STEP)
    delta = -ADAM_LR * (m_hat / (jnp.sqrt(v_hat) + ADAM_EPS) + ADAM_WD * w)
    return delta, m, v


def _row_tile(rows):
    return 256 if rows % 256 == 0 else rows


def _sum_parts(parts, name):
    n_parts, rows, cols = parts.shape
    tr = _row_tile(rows)

    def body(p_ref, g_ref):
        g = p_ref[0].astype(F32)
        for k in range(1, n_parts):
            g = g + p_ref[k].astype(F32)
        g_ref[...] = g

    blk = pl.BlockSpec((tr, cols), lambda i: (i, 0))
    return pl.pallas_call(
        body, name=name, grid=(rows // tr,),
        in_specs=[pl.BlockSpec((n_parts, tr, cols), lambda i: (0, i, 0))], out_specs=blk,
        out_shape=SDS((rows, cols), F32), compiler_params=_params(1))(parts)


def _adamw(w, g, m, v, name, from_parts):
    rows, cols = w.shape
    tr = _row_tile(rows)

    def body(w_ref, g_ref, m_ref, v_ref, go_ref, d_ref, mo_ref, vo_ref):
        if from_parts:
            g = g_ref[0].astype(F32)
            for k in range(1, g_ref.shape[0]):
                g = g + g_ref[k].astype(F32)
        else:
            g = g_ref[...]
        go_ref[...] = g
        d_ref[...], mo_ref[...], vo_ref[...] = _adamw_math(w_ref[...], g, m_ref[...], v_ref[...])

    blk = pl.BlockSpec((tr, cols), lambda i: (i, 0))
    g_spec = pl.BlockSpec((g.shape[0], tr, cols), lambda i: (0, i, 0)) if from_parts else blk
    return pl.pallas_call(
        body, name=name, grid=(rows // tr,),
        in_specs=[blk, g_spec, blk, blk], out_specs=[blk] * 4,
        out_shape=[SDS((rows, cols), F32)] * 4, compiler_params=_params(1))(w, g, m, v)


def _peer(k, x, y, c):
    return ((1 - x) if k & 4 else x, (1 - y) if k & 2 else y, (1 - c) if k & 1 else c)


def _exchange(arrays, name, scatter):
    n = len(arrays)

    def body(*refs):
        ins, outs = refs[:n], refs[n:2 * n]
        send_sems, recv_sems, local_sems = refs[2 * n:]
        x, y, c = (lax.axis_index(a) for a in AXES)
        me = 4 * x + 2 * y + c

        def remote(a, k):
            px, py, pc = _peer(k, x, y, c)
            there = 4 * px + 2 * py + pc
            return pltpu.make_async_remote_copy(
                src_ref=ins[a].at[there] if scatter else ins[a], dst_ref=outs[a].at[me],
                send_sem=send_sems.at[a, k - 1], recv_sem=recv_sems.at[a, k - 1],
                device_id=(px, py, pc), device_id_type=pl.DeviceIdType.MESH)

        def arrival(a, k):
            px, py, pc = _peer(k, x, y, c)
            there = 4 * px + 2 * py + pc
            return pltpu.make_async_remote_copy(
                src_ref=ins[a].at[there] if scatter else ins[a], dst_ref=outs[a].at[there],
                send_sem=send_sems.at[a, k - 1], recv_sem=recv_sems.at[a, k - 1],
                device_id=(px, py, pc), device_id_type=pl.DeviceIdType.MESH)

        own = [pltpu.make_async_copy(ins[a].at[me] if scatter else ins[a], outs[a].at[me], local_sems.at[a]) for a in range(n)]
        for a in range(n):
            own[a].start()
            for k in range(1, NDEV):
                remote(a, k).start()
        for a in range(n):
            for k in range(1, NDEV):
                arrival(a, k).wait_recv()
        for a in range(n):
            for k in range(1, NDEV):
                remote(a, k).wait_send()
            own[a].wait()

    out_shape = [SDS(a.shape if scatter else (NDEV,) + a.shape, a.dtype) for a in arrays]
    return pl.pallas_call(
        body, name=name, in_specs=[HBM] * n, out_specs=[HBM] * n, out_shape=out_shape,
        scratch_shapes=[pltpu.SemaphoreType.DMA((n, NDEV - 1)), pltpu.SemaphoreType.DMA((n, NDEV - 1)),
                        pltpu.SemaphoreType.DMA((n,))],
        compiler_params=pltpu.CompilerParams(has_side_effects=True))(*arrays)


SEM = pl.BlockSpec(memory_space=pltpu.SEMAPHORE)
IN_HBM = pl.BlockSpec(memory_space=pltpu.HBM)
DATAFLOW = pltpu.SideEffectType.DATAFLOW_SIDE_EFFECTING
TOKEN_SHAPE = (8, 128)


OTHER_CHIPS = (2, 4, 6)


def _place(x, y, c):
    return 4 * x + 2 * y + c


def _plan_gather_chips(n):
    def plan(refs, x, y, c, arriving):
        out = []
        for a in range(n):
            for k in (1,) + OTHER_CHIPS:
                there = _place(*_peer(k, x, y, c))
                out.append((refs[a], refs[n + a].at[there if arriving else _place(x, y, c)], k))
        return out
    return plan, n * 4


def _plan_gather_sibling(n):
    def plan(refs, x, y, c, arriving):
        out = []
        for a in range(n):
            for k in OTHER_CHIPS:
                px, py, pc = _peer(k, x, y, c)
                mine, theirs = _place(px, py, pc), _place(px, py, 1 - pc)
                out.append((refs[a].at[mine], refs[a].at[theirs if arriving else mine], 1))
        return out
    return plan, n * 3


def _plan_scatter_sibling(n):
    def plan(refs, x, y, c, arriving):
        out = []
        for a in range(n):
            for q in range(4):
                out.append((refs[a].at[2 * q + (1 - c)], refs[n + a].at[q], 1))
        return out
    return plan, n * 4


def _plan_scatter_chips(n):
    def plan(refs, x, y, c, arriving):
        out = []
        for a in range(n):
            for k in OTHER_CHIPS:
                px, py, _ = _peer(k, x, y, c)
                out.append((refs[a].at[2 * px + py], refs[n + a].at[(2 * px + py) if arriving else (2 * x + y)], k))
        return out
    return plan, n * 3


def _remote(src, dst, send_sems, recv_sems, i, k):
    x, y, c = (lax.axis_index(n) for n in AXES)
    return pltpu.make_async_remote_copy(src_ref=src, dst_ref=dst, send_sem=send_sems.at[i], recv_sem=recv_sems.at[i],
                                        device_id=_peer(k, x, y, c), device_id_type=pl.DeviceIdType.MESH)


def _copies_start(groups, name, after):
    ng = len(groups)
    total = sum(len(bufs) for bufs, _ in groups)

    def body(*refs):
        sems = refs[1 + total:1 + total + 2 * ng]
        x, y, c = (lax.axis_index(n) for n in AXES)
        off = 1
        for gi, (bufs, (plan, _)) in enumerate(groups):
            for i, (src, dst, k) in enumerate(plan(refs[off:off + len(bufs)], x, y, c, False)):
                _remote(src, dst, sems[2 * gi], sems[2 * gi + 1], i, k).start()
            off += len(bufs)
        refs[-1][...] = jnp.zeros(TOKEN_SHAPE, F32)

    sem_shapes = [pltpu.SemaphoreType.DMA((count,)) for _, (_, count) in groups for _ in range(2)]
    flat = [b for bufs, _ in groups for b in bufs]
    outs = pl.pallas_call(
        body, name=name,
        in_specs=[HBM] + [IN_HBM] * total,
        out_specs=[SEM] * (2 * ng) + [IN_HBM] * total + [pl.BlockSpec(memory_space=pltpu.VMEM)],
        out_shape=sem_shapes + [pltpu.HBM(b.shape, b.dtype) for b in flat] + [SDS(TOKEN_SHAPE, F32)],
        input_output_aliases={1 + i: 2 * ng + i for i in range(total)},
        compiler_params=pltpu.CompilerParams(has_side_effects=DATAFLOW),
    )(after, *[pltpu.with_memory_space_constraint(b, pltpu.HBM) for b in flat])
    handles, off = [], 2 * ng
    for gi, (bufs, _) in enumerate(groups):
        handles.append((outs[2 * gi], outs[2 * gi + 1], list(outs[off:off + len(bufs)])))
        off += len(bufs)
    return handles, outs[-1]


def _copies_wait(handle, plan, name, after):
    send_sems, recv_sems, bufs = handle
    n = len(bufs)

    def body(*refs):
        x, y, c = (lax.axis_index(a) for a in AXES)
        for i, (src, dst, k) in enumerate(plan[0](refs[:n], x, y, c, True)):
            copy = _remote(src, dst, refs[n], refs[n + 1], i, k)
            copy.wait_send()
            copy.wait_recv()

    return pl.pallas_call(
        body, name=name,
        in_specs=[IN_HBM] * n + [SEM, SEM, HBM], out_specs=[IN_HBM] * n,
        out_shape=[pltpu.HBM(b.shape, b.dtype) for b in bufs],
        input_output_aliases={i: i for i in range(n)},
        compiler_params=pltpu.CompilerParams(has_side_effects=DATAFLOW),
    )(*bufs, send_sems, recv_sems, after)


def _pair_sums(mine, theirs, c, name):
    n = len(mine)

    def body(c_ref, *refs):
        del c_ref
        for a in range(n):
            refs[2 * n + a][...] = (refs[a][...].astype(F32) + refs[n + a][...].astype(F32)).astype(BF16)

    block = lambda t: (None,) + t.shape[1:]
    zeros = lambda t: (0,) * (t.ndim - 1)
    return pl.pallas_call(
        body, name=name,
        grid_spec=pltpu.PrefetchScalarGridSpec(
            num_scalar_prefetch=1, grid=(4,),
            in_specs=[pl.BlockSpec(block(t), lambda q, c_ref, z=zeros(t): (2 * q + c_ref[0],) + z) for t in theirs]
            + [pl.BlockSpec(block(t), lambda q, c_ref, z=zeros(t): (q,) + z) for t in theirs],
            out_specs=[pl.BlockSpec(block(t), lambda q, c_ref, z=zeros(t): (q,) + z) for t in theirs]),
        out_shape=[SDS(t.shape, BF16) for t in theirs], compiler_params=_params(1))(c.reshape(1), *mine, *theirs)


def _local_step(x, mem, target, gains, get, put, flush, share, tm_huge=2048, tm_big=1024, tm_mid=512, tm_small=256):
    g_mix, pscale, g_mem, g_ffn, g_fin = gains
    T = x.shape[0]
    tm_huge, tm_big, tm_mid, tm_small = min(tm_huge, T), min(tm_big, T), min(tm_mid, T), min(tm_small, T)
    tn = DFF // 2

    w_in = get("in", x)
    proj, h = _fwd_proj(x, g_mix, w_in, tm_huge)
    cw0, cw1, cw2, w_co, w_pool, w_kv = get("mix", proj)
    kv, memn = _fwd_kv(mem, g_mem, w_kv)
    za, conv, pooled, ya, yp = _fwd_mix(proj, cw0, cw1, cw2, w_co, w_pool, tm_mid)
    w_xo, w_o = get("merge", ya)
    o, yx, merged, x1, h2 = _fwd_merge(proj, ya, yp, x, kv, w_xo, w_o, pscale, g_ffn, tm_mid)
    wg_t, wu_t, w_d = get("ffn", x1)
    gate, up, act = _fwd_ffn_up(h2, wg_t, wu_t, tm_mid, tn)
    dx2, loss, dg_fin = _fwd_ffn_down_loss(act, w_d, x1, target, g_fin, tm_mid)

    dgate, dup = _bwd_ffn_down(dx2, w_d, gate, up, tm_mid, tn)
    dx1, dg_ffn = _bwd_ffn_up(dgate, dup, wg_t, wu_t, x1, dx2, g_ffn, tm_small)
    dw_d = _wgrad_dense(act, dx2, "wgrad_down", tm_huge, g_mix, a_cols=tn)
    dwg_t = _wgrad_dense(dgate, h2, "wgrad_gate", tm_huge, g_mix, a_cols=tn)
    dwu_t = _wgrad_dense(dup, h2, "wgrad_up", tm_huge, g_mix, a_cols=tn)
    token = put("ffn", (dwg_t, dwu_t, dw_d))

    dproj, dya, dyx, dyps, dza, do, dpooled, dpscale = _bwd_merge(
        dx1, proj, ya, yp, yx, pscale + token[0:1, 0:1], w_o, w_co, w_xo, w_pool, tm_small)
    token = flush(dyps)
    dw_o = _wgrad_dense(merged, dx1, "wgrad_out", tm_big, token)
    dw_co = _wgrad_dense(za, dya, "wgrad_conv_out", tm_big, token)
    dw_xo = _wgrad_dense(o, dyx, "wgrad_xattn_out", tm_big, token)
    dw_pool = _wgrad(pooled, dyps[None], name="wgrad_pool", groups=NPOOL, a_cols=HD, b_cols=HD, tt=T,
                     a_index=lambda g, k, t: (t, g), b_index=lambda g, k, t: (0, t, g),
                     o_index=lambda g, k, t: (g, 0, 0), out_shape=(NPOOL, HD, HD), after=token)
    dproj, dkv = _bwd_attn(dproj, proj, do, kv, tm_big)
    dw_kv, dg_mem = _bwd_kv(dkv, memn, w_kv, mem, g_mem)
    token = put("mix", (dw_co, dw_xo, dw_o, dw_pool, dw_kv))

    dproj, dcw = _bwd_mix(dproj, proj, conv, dza, dpooled, cw0 + token[0:1, 0:1], cw1, cw2, tm_mid)
    token = flush(dcw)
    dw_in = _wgrad(h, dproj, name="wgrad_in", groups=NSPLIT, a_cols=D, b_cols=D, tt=tm_huge,
                   a_index=lambda g, k, t: (t, 0), b_index=lambda g, k, t: (g, t, 0),
                   o_index=lambda g, k, t: (_slot_group(g), 0, 0), out_shape=(NSPLIT, D, D), after=token)
    token = flush(put("in", (dw_in,)))
    grad_x, dg_mix = _bwd_proj(dproj, w_in, x, dx1, g_mix + token[0:1, 0:1], tm_big)

    small = share(jnp.concatenate([dg_mix, dpscale, dg_mem, dg_ffn, dg_fin, dcw[0:3], loss], axis=0))
    return grad_x, small


def kernel(x, mem, norm_mix, w_in, conv_w, w_conv_out, w_pool, pool_scale, norm_mem, w_kv, w_xattn_out, w_out, norm_ffn, w_gate, w_up, w_down, norm_final, loss_target, m_norm_mix, m_w_in, m_conv_w, m_w_conv_out, m_w_pool, m_pool_scale, m_norm_mem, m_w_kv, m_w_xattn_out, m_w_out, m_norm_ffn, m_w_gate, m_w_up, m_w_down, m_norm_final, v_norm_mix, v_w_in, v_conv_w, v_w_conv_out, v_w_pool, v_pool_scale, v_norm_mem, v_w_kv, v_w_xattn_out, v_w_out, v_norm_ffn, v_w_gate, v_w_up, v_w_down, v_norm_final):
    T = x.shape[1]
    rows = D // NDEV
    ffb = DFF // NDEV
    prow = HD // NDEV
    me = 4 * lax.axis_index("x") + 2 * lax.axis_index("y") + lax.axis_index("c")

    shards = [w_in[0].astype(BF16), w_conv_out[0].astype(BF16), w_xattn_out[0].astype(BF16), w_out[0].astype(BF16),
              w_pool[0].astype(BF16).reshape(NPOOL * prow, HD), w_kv[0].astype(BF16),
              w_gate[0].T.astype(BF16), w_up[0].T.astype(BF16), w_down[0].astype(BF16),
              jnp.pad(conv_w[0], ((0, 5), (0, 0)))]

    cx, cy, cc = (lax.axis_index(n) for n in AXES)
    chip = 2 * cx + cy

    def land(own, index, slots):
        return lax.dynamic_update_index_in_dim(lax.empty((slots,) + own.shape, own.dtype), own, index, 0)

    needed = ["in", "mix", "merge", "ffn"]
    members = {"in": [0], "mix": [9, 1, 4, 5], "merge": [2, 3], "ffn": [6, 7, 8]}
    g_handles, _ = _copies_start(
        [([shards[i] for i in members[n]] + [land(shards[i], me, NDEV) for i in members[n]], _plan_gather_chips(len(members[n])))
         for n in needed], "gather_start", x)

    def get(group, after):
        n = len(members[group])
        bufs = _copies_wait(g_handles[needed.index(group)], _plan_gather_chips(n), "gather_wait_" + group, after)
        (handle,), token = _copies_start([(list(bufs[n:]), _plan_gather_sibling(n))], "gather_pass_" + group, norm_mix)
        got = _copies_wait(handle, _plan_gather_sibling(n), "gather_passed_" + group, token)
        if group == "in":
            return got[0]
        if group == "mix":
            cw_g, w_co_g, w_pool_g, w_kv_g = got
            cw_full = cw_g.transpose(1, 0, 2).reshape(8, D)
            w_pool_full = w_pool_g.reshape(NDEV, NPOOL, prow, HD).transpose(1, 0, 2, 3).reshape(NPOOL, HD, HD)
            return cw_full[0:1], cw_full[1:2], cw_full[2:3], w_co_g.reshape(D, D), w_pool_full, w_kv_g
        if group == "merge":
            return got[0].reshape(D, D), got[1].reshape(D, D)
        return got[0].reshape(DFF, D), got[1].reshape(DFF, D), got[2].reshape(DFF, D)

    started = {}

    def put(group, grads):
        if group == "ffn":
            sends = [g.reshape(NDEV, ffb, D) for g in grads]
        elif group == "mix":
            dw_co, dw_xo, dw_o, dw_pool, dw_kv = grads
            sends = [dw_co.reshape(NDEV, rows, D), dw_xo.reshape(NDEV, rows, D), dw_o.reshape(NDEV, rows, D),
                     dw_pool.reshape(NPOOL, NDEV, prow, HD).transpose(1, 0, 2, 3).reshape(NDEV, NPOOL * prow, HD), dw_kv]
        else:
            sends = list(grads)
        n = len(sends)
        halves = [lax.empty((4,) + s.shape[1:], s.dtype) for s in sends]
        (handle,), token = _copies_start([(sends + halves, _plan_scatter_sibling(n))], "scatter_swap_" + group, norm_mix)
        swapping.append((group, handle, n))
        return token

    swapping = []

    def flush(after):
        group, handle, n = swapping.pop()
        bufs = _copies_wait(handle, _plan_scatter_sibling(n), "scatter_swapped_" + group, after)
        sums = _pair_sums(bufs[:n], bufs[n:], cc, "pair_sums_" + group)
        lands = [land(lax.dynamic_index_in_dim(s, chip, 0, keepdims=False), chip, 4) for s in sums]
        (handle,), token = _copies_start([(list(sums) + lands, _plan_scatter_chips(n))], "scatter_start_" + group, norm_mix)
        started[group] = (handle, _plan_scatter_chips(n))
        return token

    def take(group, after):
        handle, plan = started[group]
        return _copies_wait(handle, plan, "scatter_wait_" + group, after)[len(handle[2]) // 2:]

    def share(rows):
        (everyone,) = _exchange([rows], "gather_small", scatter=False)
        return _sum_parts(everyone, "sum_small")

    gains = (norm_mix, pool_scale, norm_mem, norm_ffn, norm_final.reshape(1, D))
    grad_x, small_sum = _local_step(x[0], mem[0], loss_target[0], gains, get, put, flush, share)
    loss = small_sum[8, 0]

    def sharded(name, w, parts, m, v):
        shape = w.shape
        flat = lambda a: a.reshape(parts.shape[1], parts.shape[2])
        outs = _adamw(flat(w), parts, flat(m), flat(v), "adamw_" + name, from_parts=True)
        return [o.reshape(shape) for o in outs]

    def transposed(name, w, parts, m, v):
        outs = _adamw(w[0].T, parts, m[0].T, v[0].T, "adamw_" + name, from_parts=True)
        return [o.T[None] for o in outs]

    def replicated(name, w, g, m, v):
        shape = w.shape
        flat = lambda a: a.reshape(g.shape)
        outs = _adamw(flat(w), g, flat(m), flat(v), "adamw_" + name, from_parts=False)
        return [o.reshape(shape) for o in outs]

    g_cw = lax.dynamic_slice_in_dim(small_sum[5:8], me * rows, rows, axis=1)
    res = {
        "norm_mix": replicated("norm_mix", norm_mix, small_sum[0:1], m_norm_mix, v_norm_mix),
        "conv_w": replicated("conv_w", conv_w, g_cw, m_conv_w, v_conv_w),
        "pool_scale": replicated("pool_scale", pool_scale, small_sum[1:2], m_pool_scale, v_pool_scale),
        "norm_mem": replicated("norm_mem", norm_mem, small_sum[2:3], m_norm_mem, v_norm_mem),
        "norm_ffn": replicated("norm_ffn", norm_ffn, small_sum[3:4], m_norm_ffn, v_norm_ffn),
        "norm_final": replicated("norm_final", norm_final, small_sum[4:5], m_norm_final, v_norm_final),
    }
    p_g, p_u, p_d = take("ffn", res["norm_final"][1])
    res["w_gate"] = transposed("w_gate", w_gate, p_g, m_w_gate, v_w_gate)
    res["w_up"] = transposed("w_up", w_up, p_u, m_w_up, v_w_up)
    res["w_down"] = sharded("w_down", w_down, p_d, m_w_down, v_w_down)
    p_co, p_xo, p_o, p_pool, p_kv = take("mix", res["w_down"][1])
    res["w_conv_out"] = sharded("w_conv_out", w_conv_out, p_co, m_w_conv_out, v_w_conv_out)
    res["w_pool"] = sharded("w_pool", w_pool, p_pool, m_w_pool, v_w_pool)
    res["w_kv"] = sharded("w_kv", w_kv, p_kv, m_w_kv, v_w_kv)
    res["w_xattn_out"] = sharded("w_xattn_out", w_xattn_out, p_xo, m_w_xattn_out, v_w_xattn_out)
    res["w_out"] = sharded("w_out", w_out, p_o, m_w_out, v_w_out)
    (p_in,) = take("in", res["w_out"][1])
    res["w_in"] = sharded("w_in", w_in, p_in, m_w_in, v_w_in)
    order = ["norm_mix", "w_in", "conv_w", "w_conv_out", "w_pool", "pool_scale", "norm_mem", "w_kv", "w_xattn_out", "w_out",
             "norm_ffn", "w_gate", "w_up", "w_down", "norm_final"]
    return (loss, grad_x[None], *[res[n][0] for n in order], *[res[n][1] for n in order],
            *[res[n][2] for n in order], *[res[n][3] for n in order])
```

```python
import jax
import jax.numpy as jnp
from jax import lax
from jax.experimental import pallas as pl
from jax.experimental.pallas import tpu as pltpu

F32 = jnp.float32
BF16 = jnp.bfloat16
SDS = jax.ShapeDtypeStruct

AXES = ("x", "y", "c")
NDEV = 8
D = 1024
NSPLIT = 8
NH = 4
HD = D // NH
NPOOL = 4
DFF = 2816
EPS = 1e-6
ATT_SCALE = HD ** -0.5
HALO = 16


def _slot_group(s):
    return jnp.where(s < 3, s + 5, jnp.where(s == 3, 4, s - 4))


ADAM_LR = 0.001
ADAM_B1 = 0.9
ADAM_B2 = 0.999
ADAM_EPS = 1e-08
ADAM_WD = 0.01
ADAM_STEP = 10

V7X_VMEM_BYTES = 64 * 1024 * 1024
VMEM_LIMIT = V7X_VMEM_BYTES - 8 * 1024 * 1024
HBM = pl.BlockSpec(memory_space=pl.ANY)


def _whole(shape):
    return pl.BlockSpec(shape, lambda *_: (0,) * len(shape), pipeline_mode=pl.Buffered(1))


def _params(n_grid):
    return pltpu.CompilerParams(dimension_semantics=("arbitrary",) * n_grid, vmem_limit_bytes=VMEM_LIMIT)


def _mm(a, b):
    return jnp.dot(a, b, preferred_element_type=F32)


def _mm_nt(a, b):
    return lax.dot_general(a, b, (((1,), (1,)), ((), ())), preferred_element_type=F32)


def _mm_tn(a, b):
    return lax.dot_general(a, b, (((0,), (0,)), ((), ())), preferred_element_type=F32)


def _sigmoid(x):
    return 1.0 / (1.0 + jnp.exp(-x))


def _rms(x):
    return lax.rsqrt(jnp.mean(x * x, axis=-1, keepdims=True) + EPS)


def _norm_bwd(dh, x, gain):
    r = _rms(x)
    xh = x * r
    dxh = dh * gain
    dx = r * (dxh - xh * jnp.mean(dxh * xh, axis=-1, keepdims=True))
    return dx, jnp.sum(dh * xh, axis=0, keepdims=True)


def _col_chunks(n, width=512):
    return [slice(c, min(c + width, n)) for c in range(0, n, width)]


def _shift_down(v, k):
    return pltpu.roll(v, k, 0)


def _shift_up(v, k):
    return pltpu.roll(v, v.shape[0] - k, 0)


def _fwd_proj(x, gain, w_blocks, groups, tm):
    T = x.shape[0]
    n = groups.shape[0]

    def body(g_ids, x_ref, g_ref, w_ref, proj_ref, h_ref):
        del g_ids

        @pl.when(pl.program_id(1) == 0)
        def _():
            xf = x_ref[...]
            h_ref[...] = (xf * _rms(xf) * g_ref[...]).astype(BF16)
        proj_ref[...] = _mm(h_ref[...], w_ref[...]).astype(BF16)

    return pl.pallas_call(
        body, name="fwd_proj",
        grid_spec=pltpu.PrefetchScalarGridSpec(
            num_scalar_prefetch=1, grid=(T // tm, n),
            in_specs=[pl.BlockSpec((tm, D), lambda i, j, ids: (i, 0)), pl.BlockSpec((1, D), lambda i, j, ids: (0, 0)),
                      pl.BlockSpec((None, D, D), lambda i, j, ids: (ids[j], 0, 0))],
            out_specs=[pl.BlockSpec((None, tm, D), lambda i, j, ids: (ids[j], i, 0)),
                       pl.BlockSpec((tm, D), lambda i, j, ids: (i, 0))]),
        out_shape=[SDS((NSPLIT, T, D), BF16), SDS((T, D), BF16)],
        compiler_params=_params(2))(groups, x, gain, w_blocks)


def _fwd_proj_more(h, w_blocks, proj, first_group, tm):
    T = h.shape[0]
    n = w_blocks.shape[0]

    def body(first, h_ref, w_ref, proj_hbm, proj_ref):
        del first, proj_hbm
        proj_ref[...] = _mm(h_ref[...], w_ref[...]).astype(BF16)

    return pl.pallas_call(
        body, name="fwd_proj_more",
        grid_spec=pltpu.PrefetchScalarGridSpec(
            num_scalar_prefetch=1, grid=(T // tm, n),
            in_specs=[pl.BlockSpec((tm, D), lambda i, j, first: (i, 0)),
                      pl.BlockSpec((None, D, D), lambda i, j, first: (j, 0, 0)), HBM],
            out_specs=pl.BlockSpec((None, tm, D), lambda i, j, first: (first[0] + j, i, 0))),
        out_shape=SDS(proj.shape, BF16), input_output_aliases={3: 0},
        compiler_params=_params(2))(first_group.reshape(1), h, w_blocks, proj)


def _fwd_kv(mem, gain, w_kv_g):
    M = mem.shape[0]

    def body(mem_ref, g_ref, w_ref, kv_ref, memn_ref):
        @pl.when(pl.program_id(0) == 0)
        def _():
            m = mem_ref[...]
            memn_ref[...] = (m * _rms(m) * g_ref[...]).astype(BF16)
        kv_ref[...] = _mm(memn_ref[...], w_ref[...]).astype(BF16)

    return pl.pallas_call(
        body, name="fwd_kv", grid=(2 * NH,),
        in_specs=[pl.BlockSpec((M, D), lambda j: (0, 0)), pl.BlockSpec((1, D), lambda j: (0, 0)),
                  pl.BlockSpec((None, D, HD), lambda j: (j, 0, 0))],
        out_specs=[pl.BlockSpec((None, M, HD), lambda j: (j, 0, 0)), pl.BlockSpec((M, D), lambda j: (0, 0))],
        out_shape=[SDS((2 * NH, M, HD), BF16), SDS((M, D), BF16)],
        compiler_params=_params(1))(mem, gain, w_kv_g)


def _halo_before(split, tm):
    return pl.BlockSpec((None, HALO, D), lambda i: (split, jnp.maximum(i * (tm // HALO) - 1, 0), 0))


def _fwd_mix(proj, cw0, cw1, cw2, w_co, w_pool, tm):
    T = proj.shape[1]

    def body(b_ref, c_ref, ua_ref, up_ref, ch_ref, uah_ref, uph_ref, cw0_ref, cw1_ref, cw2_ref, wco_ref, wp_ref,
             za_ref, conv_ref, pooled_ref, ya_ref, yp_ref):
        i = pl.program_id(0)
        keep = jnp.where(i > 0, 1.0, 0.0).astype(F32)
        cu = c_ref[...].astype(F32) * ua_ref[...].astype(F32)
        cu_h = ch_ref[...].astype(F32) * uah_ref[...].astype(F32) * keep
        ext = jnp.concatenate([cu_h, cu], axis=0)
        conv = (cw2_ref[...] * ext + cw1_ref[...] * _shift_down(ext, 1) + cw0_ref[...] * _shift_down(ext, 2))[HALO:]
        za = (b_ref[...].astype(F32) * conv).astype(BF16)
        conv_ref[...] = conv.astype(BF16)
        za_ref[...] = za
        ya_ref[...] = _mm(za, wco_ref[...]).astype(BF16)

        up = up_ref[...].astype(F32)
        ext_u = jnp.concatenate([uph_ref[...].astype(F32) * keep, up], axis=0)
        pos = i * tm + lax.broadcasted_iota(jnp.int32, (tm, HD), 0)
        for g in range(NPOOL):
            cols = slice(g * HD, (g + 1) * HD)
            s = ext_u[:, cols]
            for k in range(g + 1):
                s = s + _shift_down(s, 1 << k)
            cnt = jnp.minimum(pos + 1, 2 << g).astype(F32)
            pooled = (s[HALO:] / cnt - up[:, cols]).astype(BF16)
            pooled_ref[:, cols] = pooled
            yp_ref[:, cols] = _mm(pooled, wp_ref[g]).astype(BF16)

    tile = lambda s: pl.BlockSpec((None, tm, D), lambda i: (s, i, 0))
    row = pl.BlockSpec((1, D), lambda i: (0, 0))
    out = pl.BlockSpec((tm, D), lambda i: (i, 0))
    return pl.pallas_call(
        body, name="fwd_mix", grid=(T // tm,),
        in_specs=[tile(0), tile(1), tile(2), tile(3), _halo_before(1, tm), _halo_before(2, tm), _halo_before(3, tm),
                  row, row, row, _whole((D, D)), _whole((NPOOL, HD, HD))],
        out_specs=[out] * 5,
        out_shape=[SDS((T, D), BF16)] * 5,
        compiler_params=_params(1))(proj, proj, proj, proj, proj, proj, proj, cw0, cw1, cw2, w_co, w_pool)


def _softmax_rows(s):
    e = jnp.exp(s - jnp.max(s, axis=-1, keepdims=True))
    return e / jnp.sum(e, axis=-1, keepdims=True)


def _fwd_merge(proj, ya, yp, x, kv, w_xo, w_o, pscale, gain_ffn, tm):
    T = x.shape[0]

    def body(q_ref, ga_ref, gp_ref, gx_ref, ya_ref, yp_ref, x_ref, kv_ref, wxo_ref, wo_ref, ps_ref, gf_ref,
             o_ref, yx_ref, merged_ref, x1_ref, h2_ref):
        for h in range(NH):
            cols = slice(h * HD, (h + 1) * HD)
            p = _softmax_rows(_mm_nt(q_ref[:, cols], kv_ref[h]) * ATT_SCALE)
            o_ref[:, cols] = _mm(p.astype(BF16), kv_ref[NH + h]).astype(BF16)
        yx = _mm(o_ref[...], wxo_ref[...])
        yx_ref[...] = yx.astype(BF16)
        merged = (_sigmoid(ga_ref[...].astype(F32)) * ya_ref[...].astype(F32)
                  + _sigmoid(gp_ref[...].astype(F32)) * (yp_ref[...].astype(F32) * ps_ref[...])
                  + _sigmoid(gx_ref[...].astype(F32)) * yx).astype(BF16)
        merged_ref[...] = merged
        x1 = x_ref[...] + _mm(merged, wo_ref[...])
        x1_ref[...] = x1
        h2_ref[...] = (x1 * _rms(x1) * gf_ref[...]).astype(BF16)

    tile = lambda s: pl.BlockSpec((None, tm, D), lambda i: (s, i, 0))
    row = pl.BlockSpec((1, D), lambda i: (0, 0))
    act = pl.BlockSpec((tm, D), lambda i: (i, 0))
    full = _whole((D, D))
    return pl.pallas_call(
        body, name="fwd_merge", grid=(T // tm,),
        in_specs=[tile(4), tile(5), tile(6), tile(7), act, act, act,
                  _whole((2 * NH, kv.shape[1], HD)), full, full, row, row],
        out_specs=[act] * 5,
        out_shape=[SDS((T, D), BF16), SDS((T, D), BF16), SDS((T, D), BF16), SDS((T, D), F32), SDS((T, D), BF16)],
        compiler_params=_params(1))(proj, proj, proj, proj, ya, yp, x, kv, w_xo, w_o, pscale, gain_ffn)


def _fwd_ffn_up(h2, wg_t, wu_t, tm, tn):
    T = h2.shape[0]

    def body(h_ref, wg_ref, wu_ref, gate_ref, up_ref, act_ref):
        for cols in _col_chunks(tn):
            gate = _mm_nt(h_ref[...], wg_ref[cols, :])
            up = _mm_nt(h_ref[...], wu_ref[cols, :])
            gate_ref[:, cols] = gate.astype(BF16)
            up_ref[:, cols] = up.astype(BF16)
            act_ref[:, cols] = (gate * _sigmoid(gate) * up).astype(BF16)

    w = pl.BlockSpec((tn, D), lambda n, i: (n, 0))
    o = pl.BlockSpec((tm, tn), lambda n, i: (i, n))
    return pl.pallas_call(
        body, name="fwd_ffn_up", grid=(DFF // tn, T // tm),
        in_specs=[pl.BlockSpec((tm, D), lambda n, i: (i, 0)), w, w],
        out_specs=[o] * 3, out_shape=[SDS((T, DFF), BF16)] * 3,
        compiler_params=_params(2))(h2, wg_t, wu_t)


def _fwd_ffn_down_loss(act, w_d, x1, target, gain_final, tm):
    T = x1.shape[0]

    def body(act_ref, wd_ref, x1_ref, tgt_ref, g_ref, dx2_ref, loss_ref, dgain_ref):
        @pl.when(pl.program_id(0) == 0)
        def _():
            loss_ref[...] = jnp.zeros_like(loss_ref)
            dgain_ref[...] = jnp.zeros_like(dgain_ref)
        x2 = x1_ref[...] + _mm(act_ref[...], wd_ref[...])
        gain = g_ref[...]
        y = x2 * _rms(x2) * gain
        err = y - tgt_ref[...]
        loss_ref[...] += 0.5 * jnp.sum(jnp.mean(err * err, axis=-1, keepdims=True))
        dx2, dgain = _norm_bwd(err * (1.0 / D), x2, gain)
        dx2_ref[...] = dx2
        dgain_ref[...] += dgain

    act_spec = pl.BlockSpec((tm, D), lambda i: (i, 0))
    row = pl.BlockSpec((1, D), lambda i: (0, 0))
    return pl.pallas_call(
        body, name="fwd_ffn_down_loss", grid=(T // tm,),
        in_specs=[pl.BlockSpec((tm, DFF), lambda i: (i, 0)), _whole((DFF, D)), act_spec, act_spec, row],
        out_specs=[act_spec, pl.BlockSpec((8, D), lambda i: (0, 0)), row],
        out_shape=[SDS((T, D), F32), SDS((8, D), F32), SDS((1, D), F32)],
        compiler_params=_params(1))(act, w_d, x1, target, gain_final)


def _bwd_ffn_down(dx2, w_d, gate, up, tm, tn):
    T = dx2.shape[0]

    def body(dx_ref, wd_ref, gate_ref, up_ref, dgate_ref, dup_ref):
        dx = dx_ref[...].astype(BF16)
        for cols in _col_chunks(tn):
            dact = _mm_nt(dx, wd_ref[cols, :])
            gate = gate_ref[:, cols].astype(F32)
            sg = _sigmoid(gate)
            dgate_ref[:, cols] = (dact * up_ref[:, cols].astype(F32) * (sg * (1.0 + gate * (1.0 - sg)))).astype(BF16)
            dup_ref[:, cols] = (dact * gate * sg).astype(BF16)

    o = pl.BlockSpec((tm, tn), lambda n, i: (i, n))
    return pl.pallas_call(
        body, name="bwd_ffn_down", grid=(DFF // tn, T // tm),
        in_specs=[pl.BlockSpec((tm, D), lambda n, i: (i, 0)), pl.BlockSpec((tn, D), lambda n, i: (n, 0)), o, o],
        out_specs=[o] * 2, out_shape=[SDS((T, DFF), BF16)] * 2,
        compiler_params=_params(2))(dx2, w_d, gate, up)


def _bwd_ffn_up(dgate, dup, wg_t, wu_t, x1, dx2, gain_ffn, tm):
    T = x1.shape[0]

    def body(dg_ref, du_ref, wg_ref, wu_ref, x1_ref, dx2_ref, g_ref, dx1_ref, dgain_ref):
        @pl.when(pl.program_id(0) == 0)
        def _():
            dgain_ref[...] = jnp.zeros_like(dgain_ref)
        dh2 = _mm(dg_ref[...], wg_ref[...]) + _mm(du_ref[...], wu_ref[...])
        dx, dgain = _norm_bwd(dh2, x1_ref[...], g_ref[...])
        dx1_ref[...] = dx2_ref[...] + dx
        dgain_ref[...] += dgain

    wide = pl.BlockSpec((tm, DFF), lambda i: (i, 0))
    w = _whole((DFF, D))
    act = pl.BlockSpec((tm, D), lambda i: (i, 0))
    row = pl.BlockSpec((1, D), lambda i: (0, 0))
    return pl.pallas_call(
        body, name="bwd_ffn_up", grid=(T // tm,),
        in_specs=[wide, wide, w, w, act, act, row], out_specs=[act, row],
        out_shape=[SDS((T, D), F32), SDS((1, D), F32)],
        compiler_params=_params(1))(dgate, dup, wg_t, wu_t, x1, dx2, gain_ffn)


def _wgrad(a, b, *, name, groups, a_cols, b_cols, tt, a_index, b_index, o_index, out_shape, after):
    T = a.shape[0]
    nt = T // tt
    n_a = a.shape[1] // a_cols if groups == 1 else 1

    def body(a_ref, b_ref, after_ref, o_ref, acc_ref):
        del after_ref
        t = pl.program_id(2)

        @pl.when(t == 0)
        def _():
            acc_ref[...] = jnp.zeros_like(acc_ref)
        acc_ref[...] += _mm_tn(a_ref[...].astype(BF16), b_ref[...].astype(BF16))

        @pl.when(t == nt - 1)
        def _():
            o_ref[...] = acc_ref[...].astype(o_ref.dtype)

    return pl.pallas_call(
        body, name=name, grid=(groups, n_a, nt),
        in_specs=[pl.BlockSpec((tt, a_cols), a_index), pl.BlockSpec((None, tt, b_cols), b_index), HBM],
        out_specs=pl.BlockSpec((None, a_cols, b_cols), o_index),
        out_shape=SDS(out_shape, BF16),
        scratch_shapes=[pltpu.VMEM((a_cols, b_cols), F32)],
        compiler_params=_params(3))(a, b, after)


def _wgrad_dense(a, b, name, tt, after, a_cols=None):
    ka, nb = a.shape[1], b.shape[1]
    a_cols = ka if a_cols is None else a_cols
    out = _wgrad(a, b[None], name=name, groups=1, a_cols=a_cols, b_cols=nb, tt=tt,
                 a_index=lambda g, k, t: (t, k), b_index=lambda g, k, t: (0, t, 0),
                 o_index=lambda g, k, t: (k, 0, 0), out_shape=(ka // a_cols, a_cols, nb), after=after)
    return out.reshape(ka, nb)


def _bwd_merge(dx1, proj, ya, yp, yx, pscale, w_o, w_co, w_xo, w_pool, tm):
    T = dx1.shape[0]

    def body(dx1_ref, ga_ref, gp_ref, gx_ref, ya_ref, yp_ref, yx_ref, ps_ref, wo_ref, wco_ref, wxo_ref, wp_ref,
             dgates_ref, dya_ref, dyx_ref, dyps_ref, dza_ref, do_ref, dpooled_ref, dps_ref):
        @pl.when(pl.program_id(0) == 0)
        def _():
            dps_ref[...] = jnp.zeros_like(dps_ref)
        dmerged = _mm_nt(dx1_ref[...].astype(BF16), wo_ref[...])
        scale = ps_ref[...]
        sa, sp, sx = (_sigmoid(r[...].astype(F32)) for r in (ga_ref, gp_ref, gx_ref))
        ya, yp_pre, yx = (r[...].astype(F32) for r in (ya_ref, yp_ref, yx_ref))
        dgates_ref[0] = (dmerged * ya * sa * (1.0 - sa)).astype(BF16)
        dgates_ref[1] = (dmerged * (yp_pre * scale) * sp * (1.0 - sp)).astype(BF16)
        dgates_ref[2] = (dmerged * yx * sx * (1.0 - sx)).astype(BF16)
        dya = (dmerged * sa).astype(BF16)
        dyx = (dmerged * sx).astype(BF16)
        dyp = dmerged * sp
        dyps = (dyp * scale).astype(BF16)
        dps_ref[...] += jnp.sum(dyp * yp_pre, axis=0, keepdims=True)
        dya_ref[...] = dya
        dyx_ref[...] = dyx
        dyps_ref[...] = dyps
        dza_ref[...] = _mm_nt(dya, wco_ref[...]).astype(BF16)
        do_ref[...] = _mm_nt(dyx, wxo_ref[...]).astype(BF16)
        for g in range(NPOOL):
            cols = slice(g * HD, (g + 1) * HD)
            dpooled_ref[:, cols] = _mm_nt(dyps[:, cols], wp_ref[g]).astype(BF16)

    tile = lambda s: pl.BlockSpec((None, tm, D), lambda i: (s, i, 0))
    row = pl.BlockSpec((1, D), lambda i: (0, 0))
    act = pl.BlockSpec((tm, D), lambda i: (i, 0))
    full = _whole((D, D))
    return pl.pallas_call(
        body, name="bwd_merge", grid=(T // tm,),
        in_specs=[act, tile(5), tile(6), tile(7), act, act, act, row, full, full, full,
                  _whole((NPOOL, HD, HD))],
        out_specs=[pl.BlockSpec((3, tm, D), lambda i: (0, i, 0))] + [act] * 6 + [row],
        out_shape=[SDS((NSPLIT, T, D), BF16)] + [SDS((T, D), BF16)] * 6 + [SDS((1, D), F32)],
        compiler_params=_params(1))(dx1, proj, proj, proj, ya, yp, yx, pscale, w_o, w_co, w_xo, w_pool)


def _bwd_attn(dproj, proj, do, kv, tm):
    T = do.shape[0]
    M = kv.shape[1]

    def body(dproj_hbm, q_ref, do_ref, kv_ref, dq_ref, dkv_ref):
        del dproj_hbm

        @pl.when(pl.program_id(0) == 0)
        def _():
            dkv_ref[...] = jnp.zeros_like(dkv_ref)
        for h in range(NH):
            cols = slice(h * HD, (h + 1) * HD)
            q = q_ref[:, cols]
            do_h = do_ref[:, cols]
            p = _softmax_rows(_mm_nt(q, kv_ref[h]) * ATT_SCALE)
            dp = _mm_nt(do_h, kv_ref[NH + h])
            ds = (p * (dp - jnp.sum(dp * p, axis=-1, keepdims=True)) * ATT_SCALE).astype(BF16)
            dq_ref[:, cols] = _mm(ds, kv_ref[h]).astype(BF16)
            dkv_ref[h] += _mm_tn(ds, q)
            dkv_ref[NH + h] += _mm_tn(p.astype(BF16), do_h)

    kv_spec = pl.BlockSpec((2 * NH, M, HD), lambda i: (0, 0, 0))
    return pl.pallas_call(
        body, name="bwd_attn", grid=(T // tm,),
        in_specs=[HBM, pl.BlockSpec((None, tm, D), lambda i: (4, i, 0)), pl.BlockSpec((tm, D), lambda i: (i, 0)), kv_spec],
        out_specs=[pl.BlockSpec((None, tm, D), lambda i: (3, i, 0)), kv_spec],
        out_shape=[SDS(dproj.shape, BF16), SDS((2 * NH, M, HD), F32)],
        input_output_aliases={0: 0},
        compiler_params=_params(1))(dproj, proj, do, kv)


def _bwd_mix(dproj, proj, conv, dza, dpooled, cw0, cw1, cw2, tm):
    T = dza.shape[0]
    nt = T // tm

    def halo_after(split_or_none):
        idx = lambda i: jnp.minimum((i + 1) * (tm // HALO), T // HALO - 1)
        if split_or_none is None:
            return pl.BlockSpec((HALO, D), lambda i: (idx(i), 0))
        return pl.BlockSpec((None, HALO, D), lambda i: (split_or_none, idx(i), 0))

    def body(dproj_hbm, b_ref, c_ref, ua_ref, conv_ref, dza_ref, dpo_ref, bn_ref, dzan_ref, dpon_ref, ch_ref, uah_ref,
             cw0_ref, cw1_ref, cw2_ref, dabcu_ref, dcw_ref):
        del dproj_hbm
        i = pl.program_id(0)

        @pl.when(i == 0)
        def _():
            dcw_ref[...] = jnp.zeros_like(dcw_ref)
        keep_prev = jnp.where(i > 0, 1.0, 0.0).astype(F32)
        keep_next = jnp.where(i < nt - 1, 1.0, 0.0).astype(F32)
        dza = dza_ref[...].astype(F32)
        c = c_ref[...].astype(F32)
        ua = ua_ref[...].astype(F32)
        dconv = dza * b_ref[...].astype(F32)
        dconv_n = dzan_ref[...].astype(F32) * bn_ref[...].astype(F32) * keep_next
        ext = jnp.concatenate([dconv, dconv_n], axis=0)
        dcu = (cw2_ref[...] * ext + cw1_ref[...] * _shift_up(ext, 1) + cw0_ref[...] * _shift_up(ext, 2))[:tm]
        dabcu_ref[0] = (dza * conv_ref[...].astype(F32)).astype(BF16)
        dabcu_ref[1] = (dcu * ua).astype(BF16)
        dabcu_ref[2] = (dcu * c).astype(BF16)

        cu = c * ua
        ext_cu = jnp.concatenate([ch_ref[...].astype(F32) * uah_ref[...].astype(F32) * keep_prev, cu], axis=0)
        dcw_ref[2:3, :] += jnp.sum(dconv * cu, axis=0, keepdims=True)
        dcw_ref[1:2, :] += jnp.sum(dconv * _shift_down(ext_cu, 1)[HALO:], axis=0, keepdims=True)
        dcw_ref[0:1, :] += jnp.sum(dconv * _shift_down(ext_cu, 2)[HALO:], axis=0, keepdims=True)

        dpo = dpo_ref[...].astype(F32)
        ext_dpo = jnp.concatenate([dpo, dpon_ref[...].astype(F32) * keep_next], axis=0)
        pos = i * tm + lax.broadcasted_iota(jnp.int32, (tm + HALO, HD), 0)
        for g in range(NPOOL):
            cols = slice(g * HD, (g + 1) * HD)
            s = ext_dpo[:, cols] / jnp.minimum(pos + 1, 2 << g).astype(F32)
            for k in range(g + 1):
                s = s + _shift_up(s, 1 << k)
            dabcu_ref[3, :, cols] = (s[:tm] - dpo[:, cols]).astype(BF16)

    tile = lambda s: pl.BlockSpec((None, tm, D), lambda i: (s, i, 0))
    act = pl.BlockSpec((tm, D), lambda i: (i, 0))
    row = pl.BlockSpec((1, D), lambda i: (0, 0))
    return pl.pallas_call(
        body, name="bwd_mix", grid=(nt,),
        in_specs=[HBM, tile(0), tile(1), tile(2), act, act, act, halo_after(0), halo_after(None), halo_after(None),
                  _halo_before(1, tm), _halo_before(2, tm), row, row, row],
        out_specs=[pl.BlockSpec((4, tm, D), lambda i: (1, i, 0)), pl.BlockSpec((8, D), lambda i: (0, 0))],
        out_shape=[SDS(dproj.shape, BF16), SDS((8, D), F32)],
        input_output_aliases={0: 0},
        compiler_params=_params(1))(dproj, proj, proj, proj, conv, dza, dpooled, proj, dza, dpooled, proj, proj, cw0, cw1, cw2)


def _bwd_proj(dproj, w_in_g, x, dx1, gain, tm):
    T = x.shape[0]

    def body(dp_ref, w_ref, x_ref, dx1_ref, g_ref, dx_ref, dgain_ref, acc_ref):
        i, s = pl.program_id(0), pl.program_id(1)

        @pl.when((i == 0) & (s == 0))
        def _():
            dgain_ref[...] = jnp.zeros_like(dgain_ref)

        @pl.when(s == 0)
        def _():
            acc_ref[...] = jnp.zeros_like(acc_ref)
        acc_ref[...] += _mm_nt(dp_ref[...], w_ref[...])

        @pl.when(s == NSPLIT - 1)
        def _():
            dx, dgain = _norm_bwd(acc_ref[...], x_ref[...], g_ref[...])
            dx_ref[...] = dx1_ref[...] + dx
            dgain_ref[...] += dgain

    act = pl.BlockSpec((tm, D), lambda i, s: (i, 0))
    row = pl.BlockSpec((1, D), lambda i, s: (0, 0))
    return pl.pallas_call(
        body, name="bwd_proj", grid=(T // tm, NSPLIT),
        in_specs=[pl.BlockSpec((None, tm, D), lambda i, s: (s, i, 0)),
                  pl.BlockSpec((None, D, D), lambda i, s: (_slot_group(s), 0, 0)), act, act, row],
        out_specs=[act, row], out_shape=[SDS((T, D), F32), SDS((1, D), F32)],
        scratch_shapes=[pltpu.VMEM((tm, D), F32)],
        compiler_params=_params(2))(dproj, w_in_g, x, dx1, gain)


def _bwd_kv(dkv, memn, w_kv_g, mem, gain):
    M = mem.shape[0]

    def body(dkv_ref, memn_ref, w_ref, mem_ref, g_ref, dw_ref, dgain_ref, acc_ref):
        j = pl.program_id(0)

        @pl.when(j == 0)
        def _():
            acc_ref[...] = jnp.zeros_like(acc_ref)
        dkv_j = dkv_ref[...].astype(BF16)
        dw_ref[...] = _mm_tn(memn_ref[...], dkv_j).astype(BF16)
        acc_ref[...] += _mm_nt(dkv_j, w_ref[...])

        @pl.when(j == 2 * NH - 1)
        def _():
            dgain_ref[...] = _norm_bwd(acc_ref[...], mem_ref[...], g_ref[...])[1]

    row = pl.BlockSpec((1, D), lambda j: (0, 0))
    return pl.pallas_call(
        body, name="bwd_kv", grid=(2 * NH,),
        in_specs=[pl.BlockSpec((None, M, HD), lambda j: (j, 0, 0)), pl.BlockSpec((M, D), lambda j: (0, 0)),
                  pl.BlockSpec((None, D, HD), lambda j: (j, 0, 0)), pl.BlockSpec((M, D), lambda j: (0, 0)), row],
        out_specs=[pl.BlockSpec((None, D, HD), lambda j: (j, 0, 0)), row],
        out_shape=[SDS((2 * NH, D, HD), BF16), SDS((1, D), F32)],
        scratch_shapes=[pltpu.VMEM((M, D), F32)],
        compiler_params=_params(1))(dkv, memn, w_kv_g, mem, gain)


def _adamw_math(w, g, m, v):
    m = ADAM_B1 * m + (1.0 - ADAM_B1) * g
    v = ADAM_B2 * v + (1.0 - ADAM_B2) * (g * g)
    m_hat = m / (1.0 - ADAM_B1 ** ADAM_STEP)
    v_hat = v / (1.0 - ADAM_B2 ** ADAM_STEP)
    delta = -ADAM_LR * (m_hat / (jnp.sqrt(v_hat) + ADAM_EPS) + ADAM_WD * w)
    return delta, m, v


def _row_tile(rows):
    return 256 if rows % 256 == 0 else rows


def _sum_parts(parts, name):
    n_parts, rows, cols = parts.shape
    tr = _row_tile(rows)

    def body(p_ref, g_ref):
        g = p_ref[0].astype(F32)
        for k in range(1, n_parts):
            g = g + p_ref[k].astype(F32)
        g_ref[...] = g

    blk = pl.BlockSpec((tr, cols), lambda i: (i, 0))
    return pl.pallas_call(
        body, name=name, grid=(rows // tr,),
        in_specs=[pl.BlockSpec((n_parts, tr, cols), lambda i: (0, i, 0))], out_specs=blk,
        out_shape=SDS((rows, cols), F32), compiler_params=_params(1))(parts)


def _adamw(w, g, m, v, name, from_parts):
    rows, cols = w.shape
    tr = _row_tile(rows)

    def body(w_ref, g_ref, m_ref, v_ref, go_ref, d_ref, mo_ref, vo_ref):
        if from_parts:
            g = g_ref[0].astype(F32)
            for k in range(1, g_ref.shape[0]):
                g = g + g_ref[k].astype(F32)
        else:
            g = g_ref[...]
        go_ref[...] = g
        d_ref[...], mo_ref[...], vo_ref[...] = _adamw_math(w_ref[...], g, m_ref[...], v_ref[...])

    blk = pl.BlockSpec((tr, cols), lambda i: (i, 0))
    g_spec = pl.BlockSpec((g.shape[0], tr, cols), lambda i: (0, i, 0)) if from_parts else blk
    return pl.pallas_call(
        body, name=name, grid=(rows // tr,),
        in_specs=[blk, g_spec, blk, blk], out_specs=[blk] * 4,
        out_shape=[SDS((rows, cols), F32)] * 4, compiler_params=_params(1))(w, g, m, v)


def _peer(k, x, y, c):
    return ((1 - x) if k & 4 else x, (1 - y) if k & 2 else y, (1 - c) if k & 1 else c)


def _exchange(arrays, name, scatter):
    n = len(arrays)

    def body(*refs):
        ins, outs = refs[:n], refs[n:2 * n]
        send_sems, recv_sems, local_sems = refs[2 * n:]
        x, y, c = (lax.axis_index(a) for a in AXES)
        me = 4 * x + 2 * y + c

        def remote(a, k):
            px, py, pc = _peer(k, x, y, c)
            there = 4 * px + 2 * py + pc
            return pltpu.make_async_remote_copy(
                src_ref=ins[a].at[there] if scatter else ins[a], dst_ref=outs[a].at[me],
                send_sem=send_sems.at[a, k - 1], recv_sem=recv_sems.at[a, k - 1],
                device_id=(px, py, pc), device_id_type=pl.DeviceIdType.MESH)

        def arrival(a, k):
            px, py, pc = _peer(k, x, y, c)
            there = 4 * px + 2 * py + pc
            return pltpu.make_async_remote_copy(
                src_ref=ins[a].at[there] if scatter else ins[a], dst_ref=outs[a].at[there],
                send_sem=send_sems.at[a, k - 1], recv_sem=recv_sems.at[a, k - 1],
                device_id=(px, py, pc), device_id_type=pl.DeviceIdType.MESH)

        own = [pltpu.make_async_copy(ins[a].at[me] if scatter else ins[a], outs[a].at[me], local_sems.at[a]) for a in range(n)]
        for a in range(n):
            own[a].start()
            for k in range(1, NDEV):
                remote(a, k).start()
        for a in range(n):
            for k in range(1, NDEV):
                arrival(a, k).wait_recv()
        for a in range(n):
            for k in range(1, NDEV):
                remote(a, k).wait_send()
            own[a].wait()

    out_shape = [SDS(a.shape if scatter else (NDEV,) + a.shape, a.dtype) for a in arrays]
    return pl.pallas_call(
        body, name=name, in_specs=[HBM] * n, out_specs=[HBM] * n, out_shape=out_shape,
        scratch_shapes=[pltpu.SemaphoreType.DMA((n, NDEV - 1)), pltpu.SemaphoreType.DMA((n, NDEV - 1)),
                        pltpu.SemaphoreType.DMA((n,))],
        compiler_params=pltpu.CompilerParams(has_side_effects=True))(*arrays)


SEM = pl.BlockSpec(memory_space=pltpu.SEMAPHORE)
IN_HBM = pl.BlockSpec(memory_space=pltpu.HBM)
DATAFLOW = pltpu.SideEffectType.DATAFLOW_SIDE_EFFECTING
TOKEN_SHAPE = (8, 128)


OTHER_CHIPS = (2, 4, 6)


def _place(x, y, c):
    return 4 * x + 2 * y + c


def _plan_gather_chips(n, ks=(1,) + OTHER_CHIPS):
    def plan(refs, x, y, c, arriving):
        out = []
        for a in range(n):
            for k in ks:
                there = _place(*_peer(k, x, y, c))
                out.append((refs[a], refs[n + a].at[there if arriving else _place(x, y, c)], k))
        return out
    return plan, n * len(ks)


def _plan_gather_sibling(n, ks=OTHER_CHIPS):
    def plan(refs, x, y, c, arriving):
        out = []
        for a in range(n):
            for k in ks:
                px, py, pc = _peer(k, x, y, c)
                mine, theirs = _place(px, py, pc), _place(px, py, 1 - pc)
                out.append((refs[a].at[mine], refs[a].at[theirs if arriving else mine], 1))
        return out
    return plan, n * len(ks)


def _plan_far_chip():
    def plan(refs, x, y, c, arriving):
        return [(refs[0], refs[1].at[c], 6)]
    return plan, 1


def _plan_far_sibling():
    def plan(refs, x, y, c, arriving):
        return [(refs[0].at[c], refs[0].at[(1 - c) if arriving else c], 1)]
    return plan, 1


def _plan_scatter_sibling(n):
    def plan(refs, x, y, c, arriving):
        out = []
        for a in range(n):
            for q in range(4):
                out.append((refs[a].at[2 * q + (1 - c)], refs[n + a].at[q], 1))
        return out
    return plan, n * 4


def _plan_scatter_chips(n):
    def plan(refs, x, y, c, arriving):
        out = []
        for a in range(n):
            for k in OTHER_CHIPS:
                px, py, _ = _peer(k, x, y, c)
                out.append((refs[a].at[2 * px + py], refs[n + a].at[(2 * px + py) if arriving else (2 * x + y)], k))
        return out
    return plan, n * 3


def _remote(src, dst, send_sems, recv_sems, i, k):
    x, y, c = (lax.axis_index(n) for n in AXES)
    return pltpu.make_async_remote_copy(src_ref=src, dst_ref=dst, send_sem=send_sems.at[i], recv_sem=recv_sems.at[i],
                                        device_id=_peer(k, x, y, c), device_id_type=pl.DeviceIdType.MESH)


def _copies_start(groups, name, after):
    ng = len(groups)
    total = sum(len(bufs) for bufs, _ in groups)

    def body(*refs):
        sems = refs[1 + total:1 + total + 2 * ng]
        x, y, c = (lax.axis_index(n) for n in AXES)
        off = 1
        for gi, (bufs, (plan, _)) in enumerate(groups):
            for i, (src, dst, k) in enumerate(plan(refs[off:off + len(bufs)], x, y, c, False)):
                _remote(src, dst, sems[2 * gi], sems[2 * gi + 1], i, k).start()
            off += len(bufs)
        refs[-1][...] = jnp.zeros(TOKEN_SHAPE, F32)

    sem_shapes = [pltpu.SemaphoreType.DMA((count,)) for _, (_, count) in groups for _ in range(2)]
    flat = [b for bufs, _ in groups for b in bufs]
    outs = pl.pallas_call(
        body, name=name,
        in_specs=[HBM] + [IN_HBM] * total,
        out_specs=[SEM] * (2 * ng) + [IN_HBM] * total + [pl.BlockSpec(memory_space=pltpu.VMEM)],
        out_shape=sem_shapes + [pltpu.HBM(b.shape, b.dtype) for b in flat] + [SDS(TOKEN_SHAPE, F32)],
        input_output_aliases={1 + i: 2 * ng + i for i in range(total)},
        compiler_params=pltpu.CompilerParams(has_side_effects=DATAFLOW),
    )(after, *[pltpu.with_memory_space_constraint(b, pltpu.HBM) for b in flat])
    handles, off = [], 2 * ng
    for gi, (bufs, _) in enumerate(groups):
        handles.append((outs[2 * gi], outs[2 * gi + 1], list(outs[off:off + len(bufs)])))
        off += len(bufs)
    return handles, outs[-1]


def _copies_wait(handle, plan, name, after):
    send_sems, recv_sems, bufs = handle
    n = len(bufs)

    def body(*refs):
        x, y, c = (lax.axis_index(a) for a in AXES)
        for i, (src, dst, k) in enumerate(plan[0](refs[:n], x, y, c, True)):
            copy = _remote(src, dst, refs[n], refs[n + 1], i, k)
            copy.wait_send()
            copy.wait_recv()

    return pl.pallas_call(
        body, name=name,
        in_specs=[IN_HBM] * n + [SEM, SEM, HBM], out_specs=[IN_HBM] * n,
        out_shape=[pltpu.HBM(b.shape, b.dtype) for b in bufs],
        input_output_aliases={i: i for i in range(n)},
        compiler_params=pltpu.CompilerParams(has_side_effects=DATAFLOW),
    )(*bufs, send_sems, recv_sems, after)


def _pair_sums(mine, theirs, c, name):
    n = len(mine)

    def body(c_ref, *refs):
        del c_ref
        for a in range(n):
            refs[2 * n + a][...] = (refs[a][...].astype(F32) + refs[n + a][...].astype(F32)).astype(BF16)

    block = lambda t: (None,) + t.shape[1:]
    zeros = lambda t: (0,) * (t.ndim - 1)
    return pl.pallas_call(
        body, name=name,
        grid_spec=pltpu.PrefetchScalarGridSpec(
            num_scalar_prefetch=1, grid=(4,),
            in_specs=[pl.BlockSpec(block(t), lambda q, c_ref, z=zeros(t): (2 * q + c_ref[0],) + z) for t in theirs]
            + [pl.BlockSpec(block(t), lambda q, c_ref, z=zeros(t): (q,) + z) for t in theirs],
            out_specs=[pl.BlockSpec(block(t), lambda q, c_ref, z=zeros(t): (q,) + z) for t in theirs]),
        out_shape=[SDS(t.shape, BF16) for t in theirs], compiler_params=_params(1))(c.reshape(1), *mine, *theirs)


def _local_step(x, mem, target, gains, get, put, flush, share, tm_huge=2048, tm_big=1024, tm_mid=512, tm_small=256):
    g_mix, pscale, g_mem, g_ffn, g_fin = gains
    T = x.shape[0]
    tm_huge, tm_big, tm_mid, tm_small = min(tm_huge, T), min(tm_big, T), min(tm_mid, T), min(tm_small, T)
    tn = DFF // 2

    w_near, near_groups = get("in_near", x)
    proj, h = _fwd_proj(x, g_mix, w_near, near_groups, tm_huge)
    w_far, far_first = get("in_far", h)
    proj = _fwd_proj_more(h, w_far, proj, far_first, tm_huge)
    w_in = get("in_whole", (w_near, w_far, far_first))
    cw0, cw1, cw2, w_co, w_pool, w_kv = get("mix", proj)
    kv, memn = _fwd_kv(mem, g_mem, w_kv)
    za, conv, pooled, ya, yp = _fwd_mix(proj, cw0, cw1, cw2, w_co, w_pool, tm_mid)
    w_xo, w_o = get("merge", ya)
    o, yx, merged, x1, h2 = _fwd_merge(proj, ya, yp, x, kv, w_xo, w_o, pscale, g_ffn, tm_mid)
    wg_t, wu_t, w_d = get("ffn", x1)
    gate, up, act = _fwd_ffn_up(h2, wg_t, wu_t, tm_mid, tn)
    dx2, loss, dg_fin = _fwd_ffn_down_loss(act, w_d, x1, target, g_fin, tm_mid)

    dgate, dup = _bwd_ffn_down(dx2, w_d, gate, up, tm_mid, tn)
    dx1, dg_ffn = _bwd_ffn_up(dgate, dup, wg_t, wu_t, x1, dx2, g_ffn, tm_mid)
    dw_d = _wgrad_dense(act, dx2, "wgrad_down", tm_huge, g_mix, a_cols=tn)
    dwg_t = _wgrad_dense(dgate, h2, "wgrad_gate", tm_huge, g_mix, a_cols=tn)
    dwu_t = _wgrad_dense(dup, h2, "wgrad_up", tm_huge, g_mix, a_cols=tn)
    token = put("ffn", (dwg_t, dwu_t, dw_d))

    dproj, dya, dyx, dyps, dza, do, dpooled, dpscale = _bwd_merge(
        dx1, proj, ya, yp, yx, pscale + token[0:1, 0:1], w_o, w_co, w_xo, w_pool, tm_mid)
    token = flush(dyps)
    dw_o = _wgrad_dense(merged, dx1, "wgrad_out", tm_big, token)
    dw_co = _wgrad_dense(za, dya, "wgrad_conv_out", tm_big, token)
    dw_xo = _wgrad_dense(o, dyx, "wgrad_xattn_out", tm_big, token)
    dw_pool = _wgrad(pooled, dyps[None], name="wgrad_pool", groups=NPOOL, a_cols=HD, b_cols=HD, tt=T,
                     a_index=lambda g, k, t: (t, g), b_index=lambda g, k, t: (0, t, g),
                     o_index=lambda g, k, t: (g, 0, 0), out_shape=(NPOOL, HD, HD), after=token)
    dproj, dkv = _bwd_attn(dproj, proj, do, kv, tm_big)
    dw_kv, dg_mem = _bwd_kv(dkv, memn, w_kv, mem, g_mem)
    token = put("mix", (dw_co, dw_xo, dw_o, dw_pool, dw_kv))

    dproj, dcw = _bwd_mix(dproj, proj, conv, dza, dpooled, cw0 + token[0:1, 0:1], cw1, cw2, tm_mid)
    token = flush(dcw)
    dw_in = _wgrad(h, dproj, name="wgrad_in", groups=NSPLIT, a_cols=D, b_cols=D, tt=tm_huge,
                   a_index=lambda g, k, t: (t, 0), b_index=lambda g, k, t: (g, t, 0),
                   o_index=lambda g, k, t: (_slot_group(g), 0, 0), out_shape=(NSPLIT, D, D), after=token)
    token = flush(put("in", (dw_in,)))
    grad_x, dg_mix = _bwd_proj(dproj, w_in, x, dx1, g_mix + token[0:1, 0:1], tm_big)

    small = share(jnp.concatenate([dg_mix, dpscale, dg_mem, dg_ffn, dg_fin, dcw[0:3], loss], axis=0))
    return grad_x, small


def kernel(x, mem, norm_mix, w_in, conv_w, w_conv_out, w_pool, pool_scale, norm_mem, w_kv, w_xattn_out, w_out, norm_ffn, w_gate, w_up, w_down, norm_final, loss_target, m_norm_mix, m_w_in, m_conv_w, m_w_conv_out, m_w_pool, m_pool_scale, m_norm_mem, m_w_kv, m_w_xattn_out, m_w_out, m_norm_ffn, m_w_gate, m_w_up, m_w_down, m_norm_final, v_norm_mix, v_w_in, v_conv_w, v_w_conv_out, v_w_pool, v_pool_scale, v_norm_mem, v_w_kv, v_w_xattn_out, v_w_out, v_norm_ffn, v_w_gate, v_w_up, v_w_down, v_norm_final):
    T = x.shape[1]
    rows = D // NDEV
    ffb = DFF // NDEV
    prow = HD // NDEV
    me = 4 * lax.axis_index("x") + 2 * lax.axis_index("y") + lax.axis_index("c")

    shards = [w_in[0].astype(BF16), w_conv_out[0].astype(BF16), w_xattn_out[0].astype(BF16), w_out[0].astype(BF16),
              w_pool[0].astype(BF16).reshape(NPOOL * prow, HD), w_kv[0].astype(BF16),
              w_gate[0].T.astype(BF16), w_up[0].T.astype(BF16), w_down[0].astype(BF16),
              jnp.pad(conv_w[0], ((0, 5), (0, 0)))]

    cx, cy, cc = (lax.axis_index(n) for n in AXES)
    chip = 2 * cx + cy

    def land(own, index, slots):
        return lax.dynamic_update_index_in_dim(lax.empty((slots,) + own.shape, own.dtype), own, index, 0)

    needed = ["in_near", "in_far", "mix", "merge", "ffn"]
    members = {"mix": [9, 1, 4, 5], "merge": [2, 3], "ffn": [6, 7, 8]}
    near, far = (1, 2, 4), 6
    plans = {"in_near": _plan_gather_chips(1, near), "in_far": _plan_far_chip()}
    plans.update({n: _plan_gather_chips(len(members[n])) for n in members})
    g_bufs = {"in_near": [shards[0], land(shards[0], me, NDEV)],
              "in_far": [w_in[0].astype(BF16), lax.empty((2, D, D), BF16)]}
    g_bufs.update({n: [shards[i] for i in members[n]] + [land(shards[i], me, NDEV) for i in members[n]] for n in members})
    (near_handle,), _ = _copies_start([(g_bufs["in_near"], plans["in_near"])], "gather_start", x)
    g_handles = {}

    def get(group, after):
        if group == "in_whole":
            w_near, w_far, far_first = after
            return lax.dynamic_update_slice_in_dim(w_near, w_far, far_first, 0)
        if group == "in_near":
            bufs = _copies_wait(near_handle, plans[group], "gather_wait_in_near", after)
            plan = _plan_gather_sibling(1, near[1:])
            handles, token = _copies_start([([bufs[1]], plan)] + [(g_bufs[n], plans[n]) for n in needed[1:]],
                                           "gather_rest", norm_mix)
            g_handles.update(zip(needed[1:], handles[1:]))
            handle = handles[0]
        else:
            bufs = _copies_wait(g_handles[group], plans[group], "gather_wait_" + group, after)
            if group == "in_far":
                passing, plan = [bufs[1]], _plan_far_sibling()
            else:
                passing, plan = list(bufs[len(bufs) // 2:]), _plan_gather_sibling(len(bufs) // 2)
            (handle,), token = _copies_start([(passing, plan)], "gather_pass_" + group, norm_mix)
        got = _copies_wait(handle, plan, "gather_passed_" + group, token)
        if group == "in_near":
            return got[0], jnp.stack([me ^ k for k in range(6)]).astype(jnp.int32)
        if group == "in_far":
            return got[0], (2 * (3 - chip)).astype(jnp.int32)
        if group == "mix":
            cw_g, w_co_g, w_pool_g, w_kv_g = got
            cw_full = cw_g.transpose(1, 0, 2).reshape(8, D)
            w_pool_full = w_pool_g.reshape(NDEV, NPOOL, prow, HD).transpose(1, 0, 2, 3).reshape(NPOOL, HD, HD)
            return cw_full[0:1], cw_full[1:2], cw_full[2:3], w_co_g.reshape(D, D), w_pool_full, w_kv_g
        if group == "merge":
            return got[0].reshape(D, D), got[1].reshape(D, D)
        return got[0].reshape(DFF, D), got[1].reshape(DFF, D), got[2].reshape(DFF, D)

    started = {}

    def put(group, grads):
        if group == "ffn":
            sends = [g.reshape(NDEV, ffb, D) for g in grads]
        elif group == "mix":
            dw_co, dw_xo, dw_o, dw_pool, dw_kv = grads
            sends = [dw_co.reshape(NDEV, rows, D), dw_xo.reshape(NDEV, rows, D), dw_o.reshape(NDEV, rows, D),
                     dw_pool.reshape(NPOOL, NDEV, prow, HD).transpose(1, 0, 2, 3).reshape(NDEV, NPOOL * prow, HD), dw_kv]
        else:
            sends = list(grads)
        n = len(sends)
        halves = [lax.empty((4,) + s.shape[1:], s.dtype) for s in sends]
        (handle,), token = _copies_start([(sends + halves, _plan_scatter_sibling(n))], "scatter_swap_" + group, norm_mix)
        swapping.append((group, handle, n))
        return token

    swapping = []

    def flush(after):
        group, handle, n = swapping.pop()
        bufs = _copies_wait(handle, _plan_scatter_sibling(n), "scatter_swapped_" + group, after)
        sums = _pair_sums(bufs[:n], bufs[n:], cc, "pair_sums_" + group)
        lands = [land(lax.dynamic_index_in_dim(s, chip, 0, keepdims=False), chip, 4) for s in sums]
        (handle,), token = _copies_start([(list(sums) + lands, _plan_scatter_chips(n))], "scatter_start_" + group, norm_mix)
        started[group] = (handle, _plan_scatter_chips(n))
        return token

    def take(group, after):
        handle, plan = started[group]
        return _copies_wait(handle, plan, "scatter_wait_" + group, after)[len(handle[2]) // 2:]

    def share(rows):
        (everyone,) = _exchange([rows], "gather_small", scatter=False)
        return _sum_parts(everyone, "sum_small")

    gains = (norm_mix, pool_scale, norm_mem, norm_ffn, norm_final.reshape(1, D))
    grad_x, small_sum = _local_step(x[0], mem[0], loss_target[0], gains, get, put, flush, share)
    loss = small_sum[8, 0]

    def sharded(name, w, parts, m, v):
        shape = w.shape
        flat = lambda a: a.reshape(parts.shape[1], parts.shape[2])
        outs = _adamw(flat(w), parts, flat(m), flat(v), "adamw_" + name, from_parts=True)
        return [o.reshape(shape) for o in outs]

    def transposed(name, w, parts, m, v):
        outs = _adamw(w[0].T, parts, m[0].T, v[0].T, "adamw_" + name, from_parts=True)
        return [o.T[None] for o in outs]

    def replicated(name, w, g, m, v):
        shape = w.shape
        flat = lambda a: a.reshape(g.shape)
        outs = _adamw(flat(w), g, flat(m), flat(v), "adamw_" + name, from_parts=False)
        return [o.reshape(shape) for o in outs]

    g_cw = lax.dynamic_slice_in_dim(small_sum[5:8], me * rows, rows, axis=1)
    res = {
        "norm_mix": replicated("norm_mix", norm_mix, small_sum[0:1], m_norm_mix, v_norm_mix),
        "conv_w": replicated("conv_w", conv_w, g_cw, m_conv_w, v_conv_w),
        "pool_scale": replicated("pool_scale", pool_scale, small_sum[1:2], m_pool_scale, v_pool_scale),
        "norm_mem": replicated("norm_mem", norm_mem, small_sum[2:3], m_norm_mem, v_norm_mem),
        "norm_ffn": replicated("norm_ffn", norm_ffn, small_sum[3:4], m_norm_ffn, v_norm_ffn),
        "norm_final": replicated("norm_final", norm_final, small_sum[4:5], m_norm_final, v_norm_final),
    }
    p_g, p_u, p_d = take("ffn", res["norm_final"][1])
    res["w_gate"] = transposed("w_gate", w_gate, p_g, m_w_gate, v_w_gate)
    res["w_up"] = transposed("w_up", w_up, p_u, m_w_up, v_w_up)
    res["w_down"] = sharded("w_down", w_down, p_d, m_w_down, v_w_down)
    p_co, p_xo, p_o, p_pool, p_kv = take("mix", res["w_down"][1])
    res["w_conv_out"] = sharded("w_conv_out", w_conv_out, p_co, m_w_conv_out, v_w_conv_out)
    res["w_pool"] = sharded("w_pool", w_pool, p_pool, m_w_pool, v_w_pool)
    res["w_kv"] = sharded("w_kv", w_kv, p_kv, m_w_kv, v_w_kv)
    res["w_xattn_out"] = sharded("w_xattn_out", w_xattn_out, p_xo, m_w_xattn_out, v_w_xattn_out)
    res["w_out"] = sharded("w_out", w_out, p_o, m_w_out, v_w_out)
    (p_in,) = take("in", res["w_out"][1])
    res["w_in"] = sharded("w_in", w_in, p_in, m_w_in, v_w_in)
    order = ["norm_mix", "w_in", "conv_w", "w_conv_out", "w_pool", "pool_scale", "norm_mem", "w_kv", "w_xattn_out", "w_out",
             "norm_ffn", "w_gate", "w_up", "w_down", "norm_final"]
    return (loss, grad_x[None], *[res[n][0] for n in order], *[res[n][1] for n in order],
            *[res[n][2] for n in order], *[res[n][3] for n in order])
```

```python
import jax
import jax.numpy as jnp
from jax import lax
from jax.experimental import pallas as pl
from jax.experimental.pallas import tpu as pltpu

F32 = jnp.float32
BF16 = jnp.bfloat16
SDS = jax.ShapeDtypeStruct

AXES = ("x", "y", "c")
NDEV = 8
D = 1024
NSPLIT = 8
NH = 4
HD = D // NH
NPOOL = 4
DFF = 2816
EPS = 1e-6
ATT_SCALE = HD ** -0.5
HALO = 16


def _slot_group(s):
    return jnp.where(s < 3, s + 5, jnp.where(s == 3, 4, s - 4))


ADAM_LR = 0.001
ADAM_B1 = 0.9
ADAM_B2 = 0.999
ADAM_EPS = 1e-08
ADAM_WD = 0.01
ADAM_STEP = 10

V7X_VMEM_BYTES = 64 * 1024 * 1024
VMEM_LIMIT = V7X_VMEM_BYTES - 8 * 1024 * 1024
HBM = pl.BlockSpec(memory_space=pl.ANY)


def _whole(shape):
    return pl.BlockSpec(shape, lambda *_: (0,) * len(shape), pipeline_mode=pl.Buffered(1))


def _params(n_grid):
    return pltpu.CompilerParams(dimension_semantics=("arbitrary",) * n_grid, vmem_limit_bytes=VMEM_LIMIT)


def _mm(a, b):
    return jnp.dot(a, b, preferred_element_type=F32)


def _mm_nt(a, b):
    return lax.dot_general(a, b, (((1,), (1,)), ((), ())), preferred_element_type=F32)


def _mm_tn(a, b):
    return lax.dot_general(a, b, (((0,), (0,)), ((), ())), preferred_element_type=F32)


def _sigmoid(x):
    return 1.0 / (1.0 + jnp.exp(-x))


def _rms(x):
    return lax.rsqrt(jnp.mean(x * x, axis=-1, keepdims=True) + EPS)


def _norm_bwd(dh, x, gain):
    r = _rms(x)
    xh = x * r
    dxh = dh * gain
    dx = r * (dxh - xh * jnp.mean(dxh * xh, axis=-1, keepdims=True))
    return dx, jnp.sum(dh * xh, axis=0, keepdims=True)


def _col_chunks(n, width=512):
    return [slice(c, min(c + width, n)) for c in range(0, n, width)]


def _shift_down(v, k):
    return pltpu.roll(v, k, 0)


def _shift_up(v, k):
    return pltpu.roll(v, v.shape[0] - k, 0)


def _fwd_proj(x, gain, w_blocks, w_ids, p_ids, tm):
    T = x.shape[0]

    def body(w_ids_ref, p_ids_ref, x_ref, g_ref, w_ref, proj_ref, h_ref):
        del w_ids_ref, p_ids_ref

        @pl.when(pl.program_id(1) == 0)
        def _():
            xf = x_ref[...]
            h_ref[...] = (xf * _rms(xf) * g_ref[...]).astype(BF16)
        proj_ref[...] = _mm(h_ref[...], w_ref[...]).astype(BF16)

    return pl.pallas_call(
        body, name="fwd_proj",
        grid_spec=pltpu.PrefetchScalarGridSpec(
            num_scalar_prefetch=2, grid=(T // tm, w_ids.shape[0]),
            in_specs=[pl.BlockSpec((tm, D), lambda i, j, w, p: (i, 0)), pl.BlockSpec((1, D), lambda i, j, w, p: (0, 0)),
                      pl.BlockSpec((None, D, D), lambda i, j, w, p: (w[j], 0, 0))],
            out_specs=[pl.BlockSpec((None, tm, D), lambda i, j, w, p: (p[j], i, 0)),
                       pl.BlockSpec((tm, D), lambda i, j, w, p: (i, 0))]),
        out_shape=[SDS((NSPLIT, T, D), BF16), SDS((T, D), BF16)],
        compiler_params=_params(2))(w_ids, p_ids, x, gain, w_blocks)


def _fwd_proj_more(h, w_blocks, proj, w_ids, p_ids, tm, name):
    T = h.shape[0]

    def body(w_ids_ref, p_ids_ref, h_ref, w_ref, proj_hbm, proj_ref):
        del w_ids_ref, p_ids_ref, proj_hbm
        proj_ref[...] = _mm(h_ref[...], w_ref[...]).astype(BF16)

    return pl.pallas_call(
        body, name=name,
        grid_spec=pltpu.PrefetchScalarGridSpec(
            num_scalar_prefetch=2, grid=(T // tm, w_ids.shape[0]),
            in_specs=[pl.BlockSpec((tm, D), lambda i, j, w, p: (i, 0)),
                      pl.BlockSpec((None, D, D), lambda i, j, w, p: (w[j], 0, 0)), HBM],
            out_specs=pl.BlockSpec((None, tm, D), lambda i, j, w, p: (p[j], i, 0))),
        out_shape=SDS(proj.shape, BF16), input_output_aliases={4: 0},
        compiler_params=_params(2))(w_ids, p_ids, h, w_blocks, proj)


def _fwd_kv(mem, gain, w_kv_g):
    M = mem.shape[0]

    def body(mem_ref, g_ref, w_ref, kv_ref, memn_ref):
        @pl.when(pl.program_id(0) == 0)
        def _():
            m = mem_ref[...]
            memn_ref[...] = (m * _rms(m) * g_ref[...]).astype(BF16)
        kv_ref[...] = _mm(memn_ref[...], w_ref[...]).astype(BF16)

    return pl.pallas_call(
        body, name="fwd_kv", grid=(2 * NH,),
        in_specs=[pl.BlockSpec((M, D), lambda j: (0, 0)), pl.BlockSpec((1, D), lambda j: (0, 0)),
                  pl.BlockSpec((None, D, HD), lambda j: (j, 0, 0))],
        out_specs=[pl.BlockSpec((None, M, HD), lambda j: (j, 0, 0)), pl.BlockSpec((M, D), lambda j: (0, 0))],
        out_shape=[SDS((2 * NH, M, HD), BF16), SDS((M, D), BF16)],
        compiler_params=_params(1))(mem, gain, w_kv_g)


def _halo_before(split, tm):
    return pl.BlockSpec((None, HALO, D), lambda i: (split, jnp.maximum(i * (tm // HALO) - 1, 0), 0))


def _fwd_mix(proj, cw0, cw1, cw2, w_co, w_pool, tm):
    T = proj.shape[1]

    def body(b_ref, c_ref, ua_ref, up_ref, ch_ref, uah_ref, uph_ref, cw0_ref, cw1_ref, cw2_ref, wco_ref, wp_ref,
             za_ref, conv_ref, pooled_ref, ya_ref, yp_ref):
        i = pl.program_id(0)
        keep = jnp.where(i > 0, 1.0, 0.0).astype(F32)
        cu = c_ref[...].astype(F32) * ua_ref[...].astype(F32)
        cu_h = ch_ref[...].astype(F32) * uah_ref[...].astype(F32) * keep
        ext = jnp.concatenate([cu_h, cu], axis=0)
        conv = (cw2_ref[...] * ext + cw1_ref[...] * _shift_down(ext, 1) + cw0_ref[...] * _shift_down(ext, 2))[HALO:]
        za = (b_ref[...].astype(F32) * conv).astype(BF16)
        conv_ref[...] = conv.astype(BF16)
        za_ref[...] = za
        ya_ref[...] = _mm(za, wco_ref[...]).astype(BF16)

        up = up_ref[...].astype(F32)
        ext_u = jnp.concatenate([uph_ref[...].astype(F32) * keep, up], axis=0)
        pos = i * tm + lax.broadcasted_iota(jnp.int32, (tm, HD), 0)
        for g in range(NPOOL):
            cols = slice(g * HD, (g + 1) * HD)
            s = ext_u[:, cols]
            for k in range(g + 1):
                s = s + _shift_down(s, 1 << k)
            cnt = jnp.minimum(pos + 1, 2 << g).astype(F32)
            pooled = (s[HALO:] / cnt - up[:, cols]).astype(BF16)
            pooled_ref[:, cols] = pooled
            yp_ref[:, cols] = _mm(pooled, wp_ref[g]).astype(BF16)

    tile = lambda s: pl.BlockSpec((None, tm, D), lambda i: (s, i, 0))
    row = pl.BlockSpec((1, D), lambda i: (0, 0))
    out = pl.BlockSpec((tm, D), lambda i: (i, 0))
    return pl.pallas_call(
        body, name="fwd_mix", grid=(T // tm,),
        in_specs=[tile(0), tile(1), tile(2), tile(3), _halo_before(1, tm), _halo_before(2, tm), _halo_before(3, tm),
                  row, row, row, _whole((D, D)), _whole((NPOOL, HD, HD))],
        out_specs=[out] * 5,
        out_shape=[SDS((T, D), BF16)] * 5,
        compiler_params=_params(1))(proj, proj, proj, proj, proj, proj, proj, cw0, cw1, cw2, w_co, w_pool)


def _softmax_rows(s):
    e = jnp.exp(s - jnp.max(s, axis=-1, keepdims=True))
    return e / jnp.sum(e, axis=-1, keepdims=True)


def _fwd_merge(proj, ya, yp, x, kv, w_xo, w_o, pscale, gain_ffn, tm):
    T = x.shape[0]

    def body(q_ref, ga_ref, gp_ref, gx_ref, ya_ref, yp_ref, x_ref, kv_ref, wxo_ref, wo_ref, ps_ref, gf_ref,
             o_ref, yx_ref, merged_ref, x1_ref, h2_ref):
        for h in range(NH):
            cols = slice(h * HD, (h + 1) * HD)
            p = _softmax_rows(_mm_nt(q_ref[:, cols], kv_ref[h]) * ATT_SCALE)
            o_ref[:, cols] = _mm(p.astype(BF16), kv_ref[NH + h]).astype(BF16)
        yx = _mm(o_ref[...], wxo_ref[...])
        yx_ref[...] = yx.astype(BF16)
        merged = (_sigmoid(ga_ref[...].astype(F32)) * ya_ref[...].astype(F32)
                  + _sigmoid(gp_ref[...].astype(F32)) * (yp_ref[...].astype(F32) * ps_ref[...])
                  + _sigmoid(gx_ref[...].astype(F32)) * yx).astype(BF16)
        merged_ref[...] = merged
        x1 = x_ref[...] + _mm(merged, wo_ref[...])
        x1_ref[...] = x1
        h2_ref[...] = (x1 * _rms(x1) * gf_ref[...]).astype(BF16)

    tile = lambda s: pl.BlockSpec((None, tm, D), lambda i: (s, i, 0))
    row = pl.BlockSpec((1, D), lambda i: (0, 0))
    act = pl.BlockSpec((tm, D), lambda i: (i, 0))
    full = _whole((D, D))
    return pl.pallas_call(
        body, name="fwd_merge", grid=(T // tm,),
        in_specs=[tile(4), tile(5), tile(6), tile(7), act, act, act,
                  _whole((2 * NH, kv.shape[1], HD)), full, full, row, row],
        out_specs=[act] * 5,
        out_shape=[SDS((T, D), BF16), SDS((T, D), BF16), SDS((T, D), BF16), SDS((T, D), F32), SDS((T, D), BF16)],
        compiler_params=_params(1))(proj, proj, proj, proj, ya, yp, x, kv, w_xo, w_o, pscale, gain_ffn)


def _fwd_ffn_up(h2, wg_t, wu_t, tm, tn):
    T = h2.shape[0]

    def body(h_ref, wg_ref, wu_ref, gate_ref, up_ref, act_ref):
        for cols in _col_chunks(tn):
            gate = _mm_nt(h_ref[...], wg_ref[cols, :])
            up = _mm_nt(h_ref[...], wu_ref[cols, :])
            gate_ref[:, cols] = gate.astype(BF16)
            up_ref[:, cols] = up.astype(BF16)
            act_ref[:, cols] = (gate * _sigmoid(gate) * up).astype(BF16)

    w = pl.BlockSpec((tn, D), lambda n, i: (n, 0))
    o = pl.BlockSpec((tm, tn), lambda n, i: (i, n))
    return pl.pallas_call(
        body, name="fwd_ffn_up", grid=(DFF // tn, T // tm),
        in_specs=[pl.BlockSpec((tm, D), lambda n, i: (i, 0)), w, w],
        out_specs=[o] * 3, out_shape=[SDS((T, DFF), BF16)] * 3,
        compiler_params=_params(2))(h2, wg_t, wu_t)


def _fwd_ffn_down_loss(act, w_d, x1, target, gain_final, tm):
    T = x1.shape[0]

    def body(act_ref, wd_ref, x1_ref, tgt_ref, g_ref, dx2_ref, loss_ref, dgain_ref):
        @pl.when(pl.program_id(0) == 0)
        def _():
            loss_ref[...] = jnp.zeros_like(loss_ref)
            dgain_ref[...] = jnp.zeros_like(dgain_ref)
        x2 = x1_ref[...] + _mm(act_ref[...], wd_ref[...])
        gain = g_ref[...]
        y = x2 * _rms(x2) * gain
        err = y - tgt_ref[...]
        loss_ref[...] += 0.5 * jnp.sum(jnp.mean(err * err, axis=-1, keepdims=True))
        dx2, dgain = _norm_bwd(err * (1.0 / D), x2, gain)
        dx2_ref[...] = dx2
        dgain_ref[...] += dgain

    act_spec = pl.BlockSpec((tm, D), lambda i: (i, 0))
    row = pl.BlockSpec((1, D), lambda i: (0, 0))
    return pl.pallas_call(
        body, name="fwd_ffn_down_loss", grid=(T // tm,),
        in_specs=[pl.BlockSpec((tm, DFF), lambda i: (i, 0)), _whole((DFF, D)), act_spec, act_spec, row],
        out_specs=[act_spec, pl.BlockSpec((8, D), lambda i: (0, 0)), row],
        out_shape=[SDS((T, D), F32), SDS((8, D), F32), SDS((1, D), F32)],
        compiler_params=_params(1))(act, w_d, x1, target, gain_final)


def _bwd_ffn_down(dx2, w_d, gate, up, tm, tn):
    T = dx2.shape[0]

    def body(dx_ref, wd_ref, gate_ref, up_ref, dgate_ref, dup_ref):
        dx = dx_ref[...].astype(BF16)
        for cols in _col_chunks(tn):
            dact = _mm_nt(dx, wd_ref[cols, :])
            gate = gate_ref[:, cols].astype(F32)
            sg = _sigmoid(gate)
            dgate_ref[:, cols] = (dact * up_ref[:, cols].astype(F32) * (sg * (1.0 + gate * (1.0 - sg)))).astype(BF16)
            dup_ref[:, cols] = (dact * gate * sg).astype(BF16)

    o = pl.BlockSpec((tm, tn), lambda n, i: (i, n))
    return pl.pallas_call(
        body, name="bwd_ffn_down", grid=(DFF // tn, T // tm),
        in_specs=[pl.BlockSpec((tm, D), lambda n, i: (i, 0)), pl.BlockSpec((tn, D), lambda n, i: (n, 0)), o, o],
        out_specs=[o] * 2, out_shape=[SDS((T, DFF), BF16)] * 2,
        compiler_params=_params(2))(dx2, w_d, gate, up)


def _bwd_ffn_up(dgate, dup, wg_t, wu_t, x1, dx2, gain_ffn, tm):
    T = x1.shape[0]

    def body(dg_ref, du_ref, wg_ref, wu_ref, x1_ref, dx2_ref, g_ref, dx1_ref, dgain_ref):
        @pl.when(pl.program_id(0) == 0)
        def _():
            dgain_ref[...] = jnp.zeros_like(dgain_ref)
        dh2 = _mm(dg_ref[...], wg_ref[...]) + _mm(du_ref[...], wu_ref[...])
        dx, dgain = _norm_bwd(dh2, x1_ref[...], g_ref[...])
        dx1_ref[...] = dx2_ref[...] + dx
        dgain_ref[...] += dgain

    wide = pl.BlockSpec((tm, DFF), lambda i: (i, 0))
    w = _whole((DFF, D))
    act = pl.BlockSpec((tm, D), lambda i: (i, 0))
    row = pl.BlockSpec((1, D), lambda i: (0, 0))
    return pl.pallas_call(
        body, name="bwd_ffn_up", grid=(T // tm,),
        in_specs=[wide, wide, w, w, act, act, row], out_specs=[act, row],
        out_shape=[SDS((T, D), F32), SDS((1, D), F32)],
        compiler_params=_params(1))(dgate, dup, wg_t, wu_t, x1, dx2, gain_ffn)


def _wgrad(a, b, *, name, groups, a_cols, b_cols, tt, a_index, b_index, o_index, out_shape, after):
    T = a.shape[0]
    nt = T // tt
    n_a = a.shape[1] // a_cols if groups == 1 else 1

    def body(a_ref, b_ref, after_ref, o_ref, acc_ref):
        del after_ref
        t = pl.program_id(2)

        @pl.when(t == 0)
        def _():
            acc_ref[...] = jnp.zeros_like(acc_ref)
        acc_ref[...] += _mm_tn(a_ref[...].astype(BF16), b_ref[...].astype(BF16))

        @pl.when(t == nt - 1)
        def _():
            o_ref[...] = acc_ref[...].astype(o_ref.dtype)

    return pl.pallas_call(
        body, name=name, grid=(groups, n_a, nt),
        in_specs=[pl.BlockSpec((tt, a_cols), a_index), pl.BlockSpec((None, tt, b_cols), b_index), HBM],
        out_specs=pl.BlockSpec((None, a_cols, b_cols), o_index),
        out_shape=SDS(out_shape, BF16),
        scratch_shapes=[pltpu.VMEM((a_cols, b_cols), F32)],
        compiler_params=_params(3))(a, b, after)


def _wgrad_dense(a, b, name, tt, after, a_cols=None):
    ka, nb = a.shape[1], b.shape[1]
    a_cols = ka if a_cols is None else a_cols
    out = _wgrad(a, b[None], name=name, groups=1, a_cols=a_cols, b_cols=nb, tt=tt,
                 a_index=lambda g, k, t: (t, k), b_index=lambda g, k, t: (0, t, 0),
                 o_index=lambda g, k, t: (k, 0, 0), out_shape=(ka // a_cols, a_cols, nb), after=after)
    return out.reshape(ka, nb)


def _bwd_merge(dx1, proj, ya, yp, yx, pscale, w_o, w_co, w_xo, w_pool, tm):
    T = dx1.shape[0]

    def body(dx1_ref, ga_ref, gp_ref, gx_ref, ya_ref, yp_ref, yx_ref, ps_ref, wo_ref, wco_ref, wxo_ref, wp_ref,
             dgates_ref, dya_ref, dyx_ref, dyps_ref, dza_ref, do_ref, dpooled_ref, dps_ref):
        @pl.when(pl.program_id(0) == 0)
        def _():
            dps_ref[...] = jnp.zeros_like(dps_ref)
        dmerged = _mm_nt(dx1_ref[...].astype(BF16), wo_ref[...])
        scale = ps_ref[...]
        sa, sp, sx = (_sigmoid(r[...].astype(F32)) for r in (ga_ref, gp_ref, gx_ref))
        ya, yp_pre, yx = (r[...].astype(F32) for r in (ya_ref, yp_ref, yx_ref))
        dgates_ref[0] = (dmerged * ya * sa * (1.0 - sa)).astype(BF16)
        dgates_ref[1] = (dmerged * (yp_pre * scale) * sp * (1.0 - sp)).astype(BF16)
        dgates_ref[2] = (dmerged * yx * sx * (1.0 - sx)).astype(BF16)
        dya = (dmerged * sa).astype(BF16)
        dyx = (dmerged * sx).astype(BF16)
        dyp = dmerged * sp
        dyps = (dyp * scale).astype(BF16)
        dps_ref[...] += jnp.sum(dyp * yp_pre, axis=0, keepdims=True)
        dya_ref[...] = dya
        dyx_ref[...] = dyx
        dyps_ref[...] = dyps
        dza_ref[...] = _mm_nt(dya, wco_ref[...]).astype(BF16)
        do_ref[...] = _mm_nt(dyx, wxo_ref[...]).astype(BF16)
        for g in range(NPOOL):
            cols = slice(g * HD, (g + 1) * HD)
            dpooled_ref[:, cols] = _mm_nt(dyps[:, cols], wp_ref[g]).astype(BF16)

    tile = lambda s: pl.BlockSpec((None, tm, D), lambda i: (s, i, 0))
    row = pl.BlockSpec((1, D), lambda i: (0, 0))
    act = pl.BlockSpec((tm, D), lambda i: (i, 0))
    full = _whole((D, D))
    return pl.pallas_call(
        body, name="bwd_merge", grid=(T // tm,),
        in_specs=[act, tile(5), tile(6), tile(7), act, act, act, row, full, full, full,
                  _whole((NPOOL, HD, HD))],
        out_specs=[pl.BlockSpec((3, tm, D), lambda i: (0, i, 0))] + [act] * 6 + [row],
        out_shape=[SDS((NSPLIT, T, D), BF16)] + [SDS((T, D), BF16)] * 6 + [SDS((1, D), F32)],
        compiler_params=_params(1))(dx1, proj, proj, proj, ya, yp, yx, pscale, w_o, w_co, w_xo, w_pool)


def _bwd_attn(dproj, proj, do, kv, tm):
    T = do.shape[0]
    M = kv.shape[1]

    def body(dproj_hbm, q_ref, do_ref, kv_ref, dq_ref, dkv_ref):
        del dproj_hbm

        @pl.when(pl.program_id(0) == 0)
        def _():
            dkv_ref[...] = jnp.zeros_like(dkv_ref)
        for h in range(NH):
            cols = slice(h * HD, (h + 1) * HD)
            q = q_ref[:, cols]
            do_h = do_ref[:, cols]
            p = _softmax_rows(_mm_nt(q, kv_ref[h]) * ATT_SCALE)
            dp = _mm_nt(do_h, kv_ref[NH + h])
            ds = (p * (dp - jnp.sum(dp * p, axis=-1, keepdims=True)) * ATT_SCALE).astype(BF16)
            dq_ref[:, cols] = _mm(ds, kv_ref[h]).astype(BF16)
            dkv_ref[h] += _mm_tn(ds, q)
            dkv_ref[NH + h] += _mm_tn(p.astype(BF16), do_h)

    kv_spec = pl.BlockSpec((2 * NH, M, HD), lambda i: (0, 0, 0))
    return pl.pallas_call(
        body, name="bwd_attn", grid=(T // tm,),
        in_specs=[HBM, pl.BlockSpec((None, tm, D), lambda i: (4, i, 0)), pl.BlockSpec((tm, D), lambda i: (i, 0)), kv_spec],
        out_specs=[pl.BlockSpec((None, tm, D), lambda i: (3, i, 0)), kv_spec],
        out_shape=[SDS(dproj.shape, BF16), SDS((2 * NH, M, HD), F32)],
        input_output_aliases={0: 0},
        compiler_params=_params(1))(dproj, proj, do, kv)


def _bwd_mix(dproj, proj, conv, dza, dpooled, cw0, cw1, cw2, tm):
    T = dza.shape[0]
    nt = T // tm

    def halo_after(split_or_none):
        idx = lambda i: jnp.minimum((i + 1) * (tm // HALO), T // HALO - 1)
        if split_or_none is None:
            return pl.BlockSpec((HALO, D), lambda i: (idx(i), 0))
        return pl.BlockSpec((None, HALO, D), lambda i: (split_or_none, idx(i), 0))

    def body(dproj_hbm, b_ref, c_ref, ua_ref, conv_ref, dza_ref, dpo_ref, bn_ref, dzan_ref, dpon_ref, ch_ref, uah_ref,
             cw0_ref, cw1_ref, cw2_ref, dabcu_ref, dcw_ref):
        del dproj_hbm
        i = pl.program_id(0)

        @pl.when(i == 0)
        def _():
            dcw_ref[...] = jnp.zeros_like(dcw_ref)
        keep_prev = jnp.where(i > 0, 1.0, 0.0).astype(F32)
        keep_next = jnp.where(i < nt - 1, 1.0, 0.0).astype(F32)
        dza = dza_ref[...].astype(F32)
        c = c_ref[...].astype(F32)
        ua = ua_ref[...].astype(F32)
        dconv = dza * b_ref[...].astype(F32)
        dconv_n = dzan_ref[...].astype(F32) * bn_ref[...].astype(F32) * keep_next
        ext = jnp.concatenate([dconv, dconv_n], axis=0)
        dcu = (cw2_ref[...] * ext + cw1_ref[...] * _shift_up(ext, 1) + cw0_ref[...] * _shift_up(ext, 2))[:tm]
        dabcu_ref[0] = (dza * conv_ref[...].astype(F32)).astype(BF16)
        dabcu_ref[1] = (dcu * ua).astype(BF16)
        dabcu_ref[2] = (dcu * c).astype(BF16)

        cu = c * ua
        ext_cu = jnp.concatenate([ch_ref[...].astype(F32) * uah_ref[...].astype(F32) * keep_prev, cu], axis=0)
        dcw_ref[2:3, :] += jnp.sum(dconv * cu, axis=0, keepdims=True)
        dcw_ref[1:2, :] += jnp.sum(dconv * _shift_down(ext_cu, 1)[HALO:], axis=0, keepdims=True)
        dcw_ref[0:1, :] += jnp.sum(dconv * _shift_down(ext_cu, 2)[HALO:], axis=0, keepdims=True)

        dpo = dpo_ref[...].astype(F32)
        ext_dpo = jnp.concatenate([dpo, dpon_ref[...].astype(F32) * keep_next], axis=0)
        pos = i * tm + lax.broadcasted_iota(jnp.int32, (tm + HALO, HD), 0)
        for g in range(NPOOL):
            cols = slice(g * HD, (g + 1) * HD)
            s = ext_dpo[:, cols] / jnp.minimum(pos + 1, 2 << g).astype(F32)
            for k in range(g + 1):
                s = s + _shift_up(s, 1 << k)
            dabcu_ref[3, :, cols] = (s[:tm] - dpo[:, cols]).astype(BF16)

    tile = lambda s: pl.BlockSpec((None, tm, D), lambda i: (s, i, 0))
    act = pl.BlockSpec((tm, D), lambda i: (i, 0))
    row = pl.BlockSpec((1, D), lambda i: (0, 0))
    return pl.pallas_call(
        body, name="bwd_mix", grid=(nt,),
        in_specs=[HBM, tile(0), tile(1), tile(2), act, act, act, halo_after(0), halo_after(None), halo_after(None),
                  _halo_before(1, tm), _halo_before(2, tm), row, row, row],
        out_specs=[pl.BlockSpec((4, tm, D), lambda i: (1, i, 0)), pl.BlockSpec((8, D), lambda i: (0, 0))],
        out_shape=[SDS(dproj.shape, BF16), SDS((8, D), F32)],
        input_output_aliases={0: 0},
        compiler_params=_params(1))(dproj, proj, proj, proj, conv, dza, dpooled, proj, dza, dpooled, proj, proj, cw0, cw1, cw2)


def _bwd_proj(dproj, w_in_g, x, dx1, gain, tm):
    T = x.shape[0]

    def body(dp_ref, w_ref, x_ref, dx1_ref, g_ref, dx_ref, dgain_ref, acc_ref):
        i, s = pl.program_id(0), pl.program_id(1)

        @pl.when((i == 0) & (s == 0))
        def _():
            dgain_ref[...] = jnp.zeros_like(dgain_ref)

        @pl.when(s == 0)
        def _():
            acc_ref[...] = jnp.zeros_like(acc_ref)
        acc_ref[...] += _mm_nt(dp_ref[...], w_ref[...])

        @pl.when(s == NSPLIT - 1)
        def _():
            dx, dgain = _norm_bwd(acc_ref[...], x_ref[...], g_ref[...])
            dx_ref[...] = dx1_ref[...] + dx
            dgain_ref[...] += dgain

    act = pl.BlockSpec((tm, D), lambda i, s: (i, 0))
    row = pl.BlockSpec((1, D), lambda i, s: (0, 0))
    return pl.pallas_call(
        body, name="bwd_proj", grid=(T // tm, NSPLIT),
        in_specs=[pl.BlockSpec((None, tm, D), lambda i, s: (s, i, 0)),
                  pl.BlockSpec((None, D, D), lambda i, s: (_slot_group(s), 0, 0)), act, act, row],
        out_specs=[act, row], out_shape=[SDS((T, D), F32), SDS((1, D), F32)],
        scratch_shapes=[pltpu.VMEM((tm, D), F32)],
        compiler_params=_params(2))(dproj, w_in_g, x, dx1, gain)


def _bwd_kv(dkv, memn, w_kv_g, mem, gain):
    M = mem.shape[0]

    def body(dkv_ref, memn_ref, w_ref, mem_ref, g_ref, dw_ref, dgain_ref, acc_ref):
        j = pl.program_id(0)

        @pl.when(j == 0)
        def _():
            acc_ref[...] = jnp.zeros_like(acc_ref)
        dkv_j = dkv_ref[...].astype(BF16)
        dw_ref[...] = _mm_tn(memn_ref[...], dkv_j).astype(BF16)
        acc_ref[...] += _mm_nt(dkv_j, w_ref[...])

        @pl.when(j == 2 * NH - 1)
        def _():
            dgain_ref[...] = _norm_bwd(acc_ref[...], mem_ref[...], g_ref[...])[1]

    row = pl.BlockSpec((1, D), lambda j: (0, 0))
    return pl.pallas_call(
        body, name="bwd_kv", grid=(2 * NH,),
        in_specs=[pl.BlockSpec((None, M, HD), lambda j: (j, 0, 0)), pl.BlockSpec((M, D), lambda j: (0, 0)),
                  pl.BlockSpec((None, D, HD), lambda j: (j, 0, 0)), pl.BlockSpec((M, D), lambda j: (0, 0)), row],
        out_specs=[pl.BlockSpec((None, D, HD), lambda j: (j, 0, 0)), row],
        out_shape=[SDS((2 * NH, D, HD), BF16), SDS((1, D), F32)],
        scratch_shapes=[pltpu.VMEM((M, D), F32)],
        compiler_params=_params(1))(dkv, memn, w_kv_g, mem, gain)


def _adamw_math(w, g, m, v):
    m = ADAM_B1 * m + (1.0 - ADAM_B1) * g
    v = ADAM_B2 * v + (1.0 - ADAM_B2) * (g * g)
    m_hat = m / (1.0 - ADAM_B1 ** ADAM_STEP)
    v_hat = v / (1.0 - ADAM_B2 ** ADAM_STEP)
    delta = -ADAM_LR * (m_hat / (jnp.sqrt(v_hat) + ADAM_EPS) + ADAM_WD * w)
    return delta, m, v


def _row_tile(rows):
    return 256 if rows % 256 == 0 else rows


def _sum_parts(parts, name):
    n_parts, rows, cols = parts.shape
    tr = _row_tile(rows)

    def body(p_ref, g_ref):
        g = p_ref[0].astype(F32)
        for k in range(1, n_parts):
            g = g + p_ref[k].astype(F32)
        g_ref[...] = g

    blk = pl.BlockSpec((tr, cols), lambda i: (i, 0))
    return pl.pallas_call(
        body, name=name, grid=(rows // tr,),
        in_specs=[pl.BlockSpec((n_parts, tr, cols), lambda i: (0, i, 0))], out_specs=blk,
        out_shape=SDS((rows, cols), F32), compiler_params=_params(1))(parts)


def _adamw(w, g, m, v, name, from_parts):
    rows, cols = w.shape
    tr = _row_tile(rows)

    def body(w_ref, g_ref, m_ref, v_ref, go_ref, d_ref, mo_ref, vo_ref):
        if from_parts:
            g = g_ref[0].astype(F32)
            for k in range(1, g_ref.shape[0]):
                g = g + g_ref[k].astype(F32)
        else:
            g = g_ref[...]
        go_ref[...] = g
        d_ref[...], mo_ref[...], vo_ref[...] = _adamw_math(w_ref[...], g, m_ref[...], v_ref[...])

    blk = pl.BlockSpec((tr, cols), lambda i: (i, 0))
    g_spec = pl.BlockSpec((g.shape[0], tr, cols), lambda i: (0, i, 0)) if from_parts else blk
    return pl.pallas_call(
        body, name=name, grid=(rows // tr,),
        in_specs=[blk, g_spec, blk, blk], out_specs=[blk] * 4,
        out_shape=[SDS((rows, cols), F32)] * 4, compiler_params=_params(1))(w, g, m, v)


def _peer(k, x, y, c):
    return ((1 - x) if k & 4 else x, (1 - y) if k & 2 else y, (1 - c) if k & 1 else c)


def _exchange(arrays, name, scatter):
    n = len(arrays)

    def body(*refs):
        ins, outs = refs[:n], refs[n:2 * n]
        send_sems, recv_sems, local_sems = refs[2 * n:]
        x, y, c = (lax.axis_index(a) for a in AXES)
        me = 4 * x + 2 * y + c

        def remote(a, k):
            px, py, pc = _peer(k, x, y, c)
            there = 4 * px + 2 * py + pc
            return pltpu.make_async_remote_copy(
                src_ref=ins[a].at[there] if scatter else ins[a], dst_ref=outs[a].at[me],
                send_sem=send_sems.at[a, k - 1], recv_sem=recv_sems.at[a, k - 1],
                device_id=(px, py, pc), device_id_type=pl.DeviceIdType.MESH)

        def arrival(a, k):
            px, py, pc = _peer(k, x, y, c)
            there = 4 * px + 2 * py + pc
            return pltpu.make_async_remote_copy(
                src_ref=ins[a].at[there] if scatter else ins[a], dst_ref=outs[a].at[there],
                send_sem=send_sems.at[a, k - 1], recv_sem=recv_sems.at[a, k - 1],
                device_id=(px, py, pc), device_id_type=pl.DeviceIdType.MESH)

        own = [pltpu.make_async_copy(ins[a].at[me] if scatter else ins[a], outs[a].at[me], local_sems.at[a]) for a in range(n)]
        for a in range(n):
            own[a].start()
            for k in range(1, NDEV):
                remote(a, k).start()
        for a in range(n):
            for k in range(1, NDEV):
                arrival(a, k).wait_recv()
        for a in range(n):
            for k in range(1, NDEV):
                remote(a, k).wait_send()
            own[a].wait()

    out_shape = [SDS(a.shape if scatter else (NDEV,) + a.shape, a.dtype) for a in arrays]
    return pl.pallas_call(
        body, name=name, in_specs=[HBM] * n, out_specs=[HBM] * n, out_shape=out_shape,
        scratch_shapes=[pltpu.SemaphoreType.DMA((n, NDEV - 1)), pltpu.SemaphoreType.DMA((n, NDEV - 1)),
                        pltpu.SemaphoreType.DMA((n,))],
        compiler_params=pltpu.CompilerParams(has_side_effects=True))(*arrays)


SEM = pl.BlockSpec(memory_space=pltpu.SEMAPHORE)
IN_HBM = pl.BlockSpec(memory_space=pltpu.HBM)
DATAFLOW = pltpu.SideEffectType.DATAFLOW_SIDE_EFFECTING
TOKEN_SHAPE = (8, 128)


OTHER_CHIPS = (2, 4, 6)


def _place(x, y, c):
    return 4 * x + 2 * y + c


def _plan_gather_chips(n, ks=(1,) + OTHER_CHIPS):
    def plan(refs, x, y, c, arriving):
        out = []
        for a in range(n):
            for k in ks:
                there = _place(*_peer(k, x, y, c))
                out.append((refs[a], refs[n + a].at[there if arriving else _place(x, y, c)], k))
        return out
    return plan, n * len(ks)


def _plan_gather_sibling(n, ks=OTHER_CHIPS):
    def plan(refs, x, y, c, arriving):
        out = []
        for a in range(n):
            for k in ks:
                px, py, pc = _peer(k, x, y, c)
                mine, theirs = _place(px, py, pc), _place(px, py, 1 - pc)
                out.append((refs[a].at[mine], refs[a].at[theirs if arriving else mine], 1))
        return out
    return plan, n * len(ks)


def _plan_pair():
    def plan(refs, x, y, c, arriving):
        return [(refs[0], refs[1].at[(1 - c) if arriving else c], 1)]
    return plan, 1


def _plan_far_chip():
    def plan(refs, x, y, c, arriving):
        return [(refs[0], refs[1].at[c], 6)]
    return plan, 1


def _plan_far_sibling():
    def plan(refs, x, y, c, arriving):
        return [(refs[0].at[c], refs[0].at[(1 - c) if arriving else c], 1)]
    return plan, 1


def _plan_scatter_sibling(n):
    def plan(refs, x, y, c, arriving):
        out = []
        for a in range(n):
            for q in range(4):
                out.append((refs[a].at[2 * q + (1 - c)], refs[n + a].at[q], 1))
        return out
    return plan, n * 4


def _plan_scatter_chips(n):
    def plan(refs, x, y, c, arriving):
        out = []
        for a in range(n):
            for k in OTHER_CHIPS:
                px, py, _ = _peer(k, x, y, c)
                out.append((refs[a].at[2 * px + py], refs[n + a].at[(2 * px + py) if arriving else (2 * x + y)], k))
        return out
    return plan, n * 3


def _remote(src, dst, send_sems, recv_sems, i, k):
    x, y, c = (lax.axis_index(n) for n in AXES)
    return pltpu.make_async_remote_copy(src_ref=src, dst_ref=dst, send_sem=send_sems.at[i], recv_sem=recv_sems.at[i],
                                        device_id=_peer(k, x, y, c), device_id_type=pl.DeviceIdType.MESH)


def _copies_start(groups, name, after):
    ng = len(groups)
    total = sum(len(bufs) for bufs, _ in groups)

    def body(*refs):
        sems = refs[1 + total:1 + total + 2 * ng]
        x, y, c = (lax.axis_index(n) for n in AXES)
        off = 1
        for gi, (bufs, (plan, _)) in enumerate(groups):
            for i, (src, dst, k) in enumerate(plan(refs[off:off + len(bufs)], x, y, c, False)):
                _remote(src, dst, sems[2 * gi], sems[2 * gi + 1], i, k).start()
            off += len(bufs)
        refs[-1][...] = jnp.zeros(TOKEN_SHAPE, F32)

    sem_shapes = [pltpu.SemaphoreType.DMA((count,)) for _, (_, count) in groups for _ in range(2)]
    flat = [b for bufs, _ in groups for b in bufs]
    outs = pl.pallas_call(
        body, name=name,
        in_specs=[HBM] + [IN_HBM] * total,
        out_specs=[SEM] * (2 * ng) + [IN_HBM] * total + [pl.BlockSpec(memory_space=pltpu.VMEM)],
        out_shape=sem_shapes + [pltpu.HBM(b.shape, b.dtype) for b in flat] + [SDS(TOKEN_SHAPE, F32)],
        input_output_aliases={1 + i: 2 * ng + i for i in range(total)},
        compiler_params=pltpu.CompilerParams(has_side_effects=DATAFLOW),
    )(after, *[pltpu.with_memory_space_constraint(b, pltpu.HBM) for b in flat])
    handles, off = [], 2 * ng
    for gi, (bufs, _) in enumerate(groups):
        handles.append((outs[2 * gi], outs[2 * gi + 1], list(outs[off:off + len(bufs)])))
        off += len(bufs)
    return handles, outs[-1]


def _copies_wait(handle, plan, name, after):
    send_sems, recv_sems, bufs = handle
    n = len(bufs)

    def body(*refs):
        x, y, c = (lax.axis_index(a) for a in AXES)
        for i, (src, dst, k) in enumerate(plan[0](refs[:n], x, y, c, True)):
            copy = _remote(src, dst, refs[n], refs[n + 1], i, k)
            copy.wait_send()
            copy.wait_recv()

    return pl.pallas_call(
        body, name=name,
        in_specs=[IN_HBM] * n + [SEM, SEM, HBM], out_specs=[IN_HBM] * n,
        out_shape=[pltpu.HBM(b.shape, b.dtype) for b in bufs],
        input_output_aliases={i: i for i in range(n)},
        compiler_params=pltpu.CompilerParams(has_side_effects=DATAFLOW),
    )(*bufs, send_sems, recv_sems, after)


def _pair_sums(mine, theirs, c, chip, name):
    n = len(mine)

    def body(where_ref, *refs):
        q = pl.program_id(0)
        for a in range(n):
            total = (refs[a][...].astype(F32) + refs[n + a][...].astype(F32)).astype(BF16)
            refs[2 * n + a][...] = total

            @pl.when(q == where_ref[1])
            def _():
                refs[3 * n + a][...] = total

    block = lambda t: (None,) + t.shape[1:]
    zeros = lambda t: (0,) * (t.ndim - 1)
    outs = pl.pallas_call(
        body, name=name,
        grid_spec=pltpu.PrefetchScalarGridSpec(
            num_scalar_prefetch=1, grid=(4,),
            in_specs=[pl.BlockSpec(block(t), lambda q, w, z=zeros(t): (2 * q + w[0],) + z) for t in theirs]
            + [pl.BlockSpec(block(t), lambda q, w, z=zeros(t): (q,) + z) for t in theirs],
            out_specs=[pl.BlockSpec(block(t), lambda q, w, z=zeros(t): (q,) + z) for t in theirs]
            + [pl.BlockSpec(block(t), lambda q, w, z=zeros(t): (w[1],) + z) for t in theirs]),
        out_shape=[SDS(t.shape, BF16) for t in theirs] * 2,
        compiler_params=_params(1))(jnp.stack([c, chip]).astype(jnp.int32), *mine, *theirs)
    return list(outs[:n]), list(outs[n:])


def _local_step(x, mem, target, gains, get, put, flush, share, tm_huge=2048, tm_big=1024, tm_mid=512, tm_small=256):
    g_mix, pscale, g_mem, g_ffn, g_fin = gains
    T = x.shape[0]
    tm_huge, tm_big, tm_mid, tm_small = min(tm_huge, T), min(tm_big, T), min(tm_mid, T), min(tm_small, T)
    tn = DFF // 2

    w_pair, w_ids, p_ids = get("in_pair", x)
    proj, h = _fwd_proj(x, g_mix, w_pair, w_ids, p_ids, tm_huge)
    w_near, w_ids, p_ids = get("in_near", h)
    proj = _fwd_proj_more(h, w_near, proj, w_ids, p_ids, tm_huge, "fwd_proj_near")
    w_far, w_ids, p_ids = get("in_far", proj)
    proj = _fwd_proj_more(h, w_far, proj, w_ids, p_ids, tm_huge, "fwd_proj_far")
    w_in = get("in_whole", (w_pair, w_near, w_far))
    cw0, cw1, cw2, w_co, w_pool, w_kv = get("mix", proj)
    kv, memn = _fwd_kv(mem, g_mem, w_kv)
    za, conv, pooled, ya, yp = _fwd_mix(proj, cw0, cw1, cw2, w_co, w_pool, tm_mid)
    w_xo, w_o = get("merge", ya)
    o, yx, merged, x1, h2 = _fwd_merge(proj, ya, yp, x, kv, w_xo, w_o, pscale, g_ffn, tm_mid)
    wg_t, wu_t = get("gate_up", x1)
    get("down", x1, early=True)
    gate, up, act = _fwd_ffn_up(h2, wg_t, wu_t, tm_mid, tn)
    (w_d,) = get("down", gate)
    dx2, loss, dg_fin = _fwd_ffn_down_loss(act, w_d, x1, target, g_fin, tm_mid)

    dgate, dup = _bwd_ffn_down(dx2, w_d, gate, up, tm_mid, tn)
    dx1, dg_ffn = _bwd_ffn_up(dgate, dup, wg_t, wu_t, x1, dx2, g_ffn, tm_mid)
    dw_d = _wgrad_dense(act, dx2, "wgrad_down", tm_huge, g_mix, a_cols=tn)
    dwg_t = _wgrad_dense(dgate, h2, "wgrad_gate", tm_huge, g_mix, a_cols=tn)
    dwu_t = _wgrad_dense(dup, h2, "wgrad_up", tm_huge, g_mix, a_cols=tn)
    token = put("ffn", (dwg_t, dwu_t, dw_d))

    dproj, dya, dyx, dyps, dza, do, dpooled, dpscale = _bwd_merge(
        dx1, proj, ya, yp, yx, pscale + token[0:1, 0:1], w_o, w_co, w_xo, w_pool, tm_mid)
    token = flush(dyps)
    dw_o = _wgrad_dense(merged, dx1, "wgrad_out", tm_big, token)
    dw_co = _wgrad_dense(za, dya, "wgrad_conv_out", tm_big, token)
    dw_xo = _wgrad_dense(o, dyx, "wgrad_xattn_out", tm_big, token)
    dw_pool = _wgrad(pooled, dyps[None], name="wgrad_pool", groups=NPOOL, a_cols=HD, b_cols=HD, tt=T,
                     a_index=lambda g, k, t: (t, g), b_index=lambda g, k, t: (0, t, g),
                     o_index=lambda g, k, t: (g, 0, 0), out_shape=(NPOOL, HD, HD), after=token)
    dproj, dkv = _bwd_attn(dproj, proj, do, kv, tm_big)
    dw_kv, dg_mem = _bwd_kv(dkv, memn, w_kv, mem, g_mem)
    token = put("mix", (dw_co, dw_xo, dw_o, dw_pool, dw_kv))

    dproj, dcw = _bwd_mix(dproj, proj, conv, dza, dpooled, cw0 + token[0:1, 0:1], cw1, cw2, tm_mid)
    token = flush(dcw)
    dw_in = _wgrad(h, dproj, name="wgrad_in", groups=NSPLIT, a_cols=D, b_cols=D, tt=tm_huge,
                   a_index=lambda g, k, t: (t, 0), b_index=lambda g, k, t: (g, t, 0),
                   o_index=lambda g, k, t: (_slot_group(g), 0, 0), out_shape=(NSPLIT, D, D), after=token)
    token = flush(put("in", (dw_in,)))
    grad_x, dg_mix = _bwd_proj(dproj, w_in, x, dx1, g_mix + token[0:1, 0:1], tm_big)

    small = share(jnp.concatenate([dg_mix, dpscale, dg_mem, dg_ffn, dg_fin, dcw[0:3], loss], axis=0))
    return grad_x, small


def kernel(x, mem, norm_mix, w_in, conv_w, w_conv_out, w_pool, pool_scale, norm_mem, w_kv, w_xattn_out, w_out, norm_ffn, w_gate, w_up, w_down, norm_final, loss_target, m_norm_mix, m_w_in, m_conv_w, m_w_conv_out, m_w_pool, m_pool_scale, m_norm_mem, m_w_kv, m_w_xattn_out, m_w_out, m_norm_ffn, m_w_gate, m_w_up, m_w_down, m_norm_final, v_norm_mix, v_w_in, v_conv_w, v_w_conv_out, v_w_pool, v_pool_scale, v_norm_mem, v_w_kv, v_w_xattn_out, v_w_out, v_norm_ffn, v_w_gate, v_w_up, v_w_down, v_norm_final):
    T = x.shape[1]
    rows = D // NDEV
    ffb = DFF // NDEV
    prow = HD // NDEV
    me = 4 * lax.axis_index("x") + 2 * lax.axis_index("y") + lax.axis_index("c")

    shards = [w_in[0].astype(BF16), w_conv_out[0].astype(BF16), w_xattn_out[0].astype(BF16), w_out[0].astype(BF16),
              w_pool[0].astype(BF16).reshape(NPOOL * prow, HD), w_kv[0].astype(BF16),
              w_gate[0].T.astype(BF16), w_up[0].T.astype(BF16), w_down[0].astype(BF16),
              jnp.pad(conv_w[0], ((0, 5), (0, 0)))]

    cx, cy, cc = (lax.axis_index(n) for n in AXES)
    chip = 2 * cx + cy

    def land(own, index, slots):
        return lax.dynamic_update_index_in_dim(lax.empty((slots,) + own.shape, own.dtype), own, index, 0)

    needed = ["in_pair", "in_near", "in_far", "mix", "merge", "gate_up", "down"]
    members = {"mix": [9, 1, 4, 5], "gate_up": [6, 7], "merge": [2, 3], "down": [8]}
    near = (2, 4)
    plans = {"in_pair": _plan_pair(), "in_near": _plan_gather_chips(1, near), "in_far": _plan_far_chip()}
    plans.update({n: _plan_gather_chips(len(members[n])) for n in members})
    g_bufs = {"in_pair": [shards[0], land(shards[0], cc, 2)],
              "in_near": [w_in[0].astype(BF16), lax.empty((NDEV, D, D), BF16)],
              "in_far": [w_in[0].astype(BF16), lax.empty((2, D, D), BF16)]}
    g_bufs.update({n: [shards[i] for i in members[n]] + [land(shards[i], me, NDEV) for i in members[n]] for n in members})
    first_handles, _ = _copies_start([(g_bufs[n], plans[n]) for n in needed[:2]], "gather_start", x)
    g_handles = dict(zip(needed[:2], first_handles))
    pair_ids = jnp.array([0, 1], jnp.int32)

    on_last_leg = {}

    def get(group, after, early=False):
        if group == "in_whole":
            w_pair, w_near, w_far = after
            w_whole = lax.dynamic_update_slice_in_dim(w_near, w_pair, 2 * chip, 0)
            return lax.dynamic_update_slice_in_dim(w_whole, w_far, 2 * (3 - chip), 0)
        if group not in on_last_leg:
            bufs = _copies_wait(g_handles[group], plans[group], "gather_wait_" + group, after)
            if group == "in_pair":
                return bufs[1], pair_ids, (2 * chip + pair_ids).astype(jnp.int32)
            if group == "in_near":
                plan = _plan_gather_sibling(1, near)
                handles, token = _copies_start([([bufs[1]], plan)] + [(g_bufs[n], plans[n]) for n in needed[2:]],
                                               "gather_rest", norm_mix)
                g_handles.update(zip(needed[2:], handles[1:]))
                handle = handles[0]
            else:
                if group == "in_far":
                    passing, plan = [bufs[1]], _plan_far_sibling()
                else:
                    passing, plan = list(bufs[len(bufs) // 2:]), _plan_gather_sibling(len(bufs) // 2)
                (handle,), token = _copies_start([(passing, plan)], "gather_pass_" + group, norm_mix)
            on_last_leg[group] = (handle, plan, token)
        if early:
            return None
        handle, plan, token = on_last_leg[group]
        got = _copies_wait(handle, plan, "gather_passed_" + group, after if group in ("gate_up", "down") else token)
        if group == "in_near":
            groups = jnp.stack([me ^ k for k in (2, 3, 4, 5)]).astype(jnp.int32)
            return got[0], groups, groups
        if group == "in_far":
            return got[0], pair_ids, (2 * (3 - chip) + pair_ids).astype(jnp.int32)
        if group == "mix":
            cw_g, w_co_g, w_pool_g, w_kv_g = got
            cw_full = cw_g.transpose(1, 0, 2).reshape(8, D)
            w_pool_full = w_pool_g.reshape(NDEV, NPOOL, prow, HD).transpose(1, 0, 2, 3).reshape(NPOOL, HD, HD)
            return cw_full[0:1], cw_full[1:2], cw_full[2:3], w_co_g.reshape(D, D), w_pool_full, w_kv_g
        if group == "merge":
            return got[0].reshape(D, D), got[1].reshape(D, D)
        return [g.reshape(DFF, D) for g in got]

    started = {}

    def put(group, grads):
        if group == "ffn":
            sends = [g.reshape(NDEV, ffb, D) for g in grads]
        elif group == "mix":
            dw_co, dw_xo, dw_o, dw_pool, dw_kv = grads
            sends = [dw_co.reshape(NDEV, rows, D), dw_xo.reshape(NDEV, rows, D), dw_o.reshape(NDEV, rows, D),
                     dw_pool.reshape(NPOOL, NDEV, prow, HD).transpose(1, 0, 2, 3).reshape(NDEV, NPOOL * prow, HD), dw_kv]
        else:
            sends = list(grads)
        n = len(sends)
        halves = [lax.empty((4,) + s.shape[1:], s.dtype) for s in sends]
        (handle,), token = _copies_start([(sends + halves, _plan_scatter_sibling(n))], "scatter_swap_" + group, norm_mix)
        swapping.append((group, handle, n))
        return token

    swapping = []

    def flush(after):
        group, handle, n = swapping.pop()
        bufs = _copies_wait(handle, _plan_scatter_sibling(n), "scatter_swapped_" + group, after)
        sums, lands = _pair_sums(bufs[:n], bufs[n:], cc, chip, "pair_sums_" + group)
        (handle,), token = _copies_start([(sums + lands, _plan_scatter_chips(n))], "scatter_start_" + group, norm_mix)
        started[group] = (handle, _plan_scatter_chips(n))
        return token

    def take(group, after):
        handle, plan = started[group]
        return _copies_wait(handle, plan, "scatter_wait_" + group, after)[len(handle[2]) // 2:]

    def share(rows):
        (everyone,) = _exchange([rows], "gather_small", scatter=False)
        return _sum_parts(everyone, "sum_small")

    gains = (norm_mix, pool_scale, norm_mem, norm_ffn, norm_final.reshape(1, D))
    grad_x, small_sum = _local_step(x[0], mem[0], loss_target[0], gains, get, put, flush, share)
    loss = small_sum[8, 0]

    def sharded(name, w, parts, m, v):
        shape = w.shape
        flat = lambda a: a.reshape(parts.shape[1], parts.shape[2])
        outs = _adamw(flat(w), parts, flat(m), flat(v), "adamw_" + name, from_parts=True)
        return [o.reshape(shape) for o in outs]

    def transposed(name, w, parts, m, v):
        outs = _adamw(w[0].T, parts, m[0].T, v[0].T, "adamw_" + name, from_parts=True)
        return [o.T[None] for o in outs]

    def replicated(name, w, g, m, v):
        shape = w.shape
        flat = lambda a: a.reshape(g.shape)
        outs = _adamw(flat(w), g, flat(m), flat(v), "adamw_" + name, from_parts=False)
        return [o.reshape(shape) for o in outs]

    g_cw = lax.dynamic_slice_in_dim(small_sum[5:8], me * rows, rows, axis=1)
    res = {
        "norm_mix": replicated("norm_mix", norm_mix, small_sum[0:1], m_norm_mix, v_norm_mix),
        "conv_w": replicated("conv_w", conv_w, g_cw, m_conv_w, v_conv_w),
        "pool_scale": replicated("pool_scale", pool_scale, small_sum[1:2], m_pool_scale, v_pool_scale),
        "norm_mem": replicated("norm_mem", norm_mem, small_sum[2:3], m_norm_mem, v_norm_mem),
        "norm_ffn": replicated("norm_ffn", norm_ffn, small_sum[3:4], m_norm_ffn, v_norm_ffn),
        "norm_final": replicated("norm_final", norm_final, small_sum[4:5], m_norm_final, v_norm_final),
    }
    p_g, p_u, p_d = take("ffn", res["norm_final"][1])
    res["w_gate"] = transposed("w_gate", w_gate, p_g, m_w_gate, v_w_gate)
    res["w_up"] = transposed("w_up", w_up, p_u, m_w_up, v_w_up)
    res["w_down"] = sharded("w_down", w_down, p_d, m_w_down, v_w_down)
    p_co, p_xo, p_o, p_pool, p_kv = take("mix", res["w_down"][1])
    res["w_conv_out"] = sharded("w_conv_out", w_conv_out, p_co, m_w_conv_out, v_w_conv_out)
    res["w_pool"] = sharded("w_pool", w_pool, p_pool, m_w_pool, v_w_pool)
    res["w_kv"] = sharded("w_kv", w_kv, p_kv, m_w_kv, v_w_kv)
    res["w_xattn_out"] = sharded("w_xattn_out", w_xattn_out, p_xo, m_w_xattn_out, v_w_xattn_out)
    res["w_out"] = sharded("w_out", w_out, p_o, m_w_out, v_w_out)
    (p_in,) = take("in", res["w_out"][1])
    res["w_in"] = sharded("w_in", w_in, p_in, m_w_in, v_w_in)
    order = ["norm_mix", "w_in", "conv_w", "w_conv_out", "w_pool", "pool_scale", "norm_mem", "w_kv", "w_xattn_out", "w_out",
             "norm_ffn", "w_gate", "w_up", "w_down", "norm_final"]
    return (loss, grad_x[None], *[res[n][0] for n in order], *[res[n][1] for n in order],
            *[res[n][2] for n in order], *[res[n][3] for n in order])
```

```python
import jax
import jax.numpy as jnp
from jax import lax
from jax.experimental import pallas as pl
from jax.experimental.pallas import tpu as pltpu

F32 = jnp.float32
BF16 = jnp.bfloat16
SDS = jax.ShapeDtypeStruct

AXES = ("x", "y", "c")
NDEV = 8
D = 1024
NSPLIT = 8
NH = 4
HD = D // NH
NPOOL = 4
DFF = 2816
EPS = 1e-6
ATT_SCALE = HD ** -0.5
HALO = 16


def _slot_group(s):
    return jnp.where(s < 3, s + 5, jnp.where(s == 3, 4, s - 4))


ADAM_LR = 0.001
ADAM_B1 = 0.9
ADAM_B2 = 0.999
ADAM_EPS = 1e-08
ADAM_WD = 0.01
ADAM_STEP = 10

V7X_VMEM_BYTES = 64 * 1024 * 1024
VMEM_LIMIT = V7X_VMEM_BYTES - 8 * 1024 * 1024
HBM = pl.BlockSpec(memory_space=pl.ANY)


def _whole(shape):
    return pl.BlockSpec(shape, lambda *_: (0,) * len(shape), pipeline_mode=pl.Buffered(1))


def _params(n_grid):
    return pltpu.CompilerParams(dimension_semantics=("arbitrary",) * n_grid, vmem_limit_bytes=VMEM_LIMIT)


def _mm(a, b):
    return jnp.dot(a, b, preferred_element_type=F32)


def _mm_nt(a, b):
    return lax.dot_general(a, b, (((1,), (1,)), ((), ())), preferred_element_type=F32)


def _mm_tn(a, b):
    return lax.dot_general(a, b, (((0,), (0,)), ((), ())), preferred_element_type=F32)


def _sigmoid(x):
    return 1.0 / (1.0 + jnp.exp(-x))


def _rms(x):
    return lax.rsqrt(jnp.mean(x * x, axis=-1, keepdims=True) + EPS)


def _norm_bwd(dh, x, gain):
    r = _rms(x)
    xh = x * r
    dxh = dh * gain
    dx = r * (dxh - xh * jnp.mean(dxh * xh, axis=-1, keepdims=True))
    return dx, jnp.sum(dh * xh, axis=0, keepdims=True)


def _col_chunks(n, width=512):
    return [slice(c, min(c + width, n)) for c in range(0, n, width)]


def _shift_down(v, k):
    return pltpu.roll(v, k, 0)


def _shift_up(v, k):
    return pltpu.roll(v, v.shape[0] - k, 0)


def _fwd_proj(x, gain, w_blocks, w_ids, p_ids, tm):
    T = x.shape[0]

    def body(w_ids_ref, p_ids_ref, x_ref, g_ref, w_ref, proj_ref, h_ref):
        del w_ids_ref, p_ids_ref

        @pl.when(pl.program_id(1) == 0)
        def _():
            xf = x_ref[...]
            h_ref[...] = (xf * _rms(xf) * g_ref[...]).astype(BF16)
        proj_ref[...] = _mm(h_ref[...], w_ref[...]).astype(BF16)

    return pl.pallas_call(
        body, name="fwd_proj",
        grid_spec=pltpu.PrefetchScalarGridSpec(
            num_scalar_prefetch=2, grid=(T // tm, w_ids.shape[0]),
            in_specs=[pl.BlockSpec((tm, D), lambda i, j, w, p: (i, 0)), pl.BlockSpec((1, D), lambda i, j, w, p: (0, 0)),
                      pl.BlockSpec((None, D, D), lambda i, j, w, p: (w[j], 0, 0))],
            out_specs=[pl.BlockSpec((None, tm, D), lambda i, j, w, p: (p[j], i, 0)),
                       pl.BlockSpec((tm, D), lambda i, j, w, p: (i, 0))]),
        out_shape=[SDS((NSPLIT, T, D), BF16), SDS((T, D), BF16)],
        compiler_params=_params(2))(w_ids, p_ids, x, gain, w_blocks)


def _fwd_proj_more(h, w_blocks, proj, w_ids, p_ids, tm, name):
    T = h.shape[0]

    def body(w_ids_ref, p_ids_ref, h_ref, w_ref, proj_hbm, proj_ref):
        del w_ids_ref, p_ids_ref, proj_hbm
        proj_ref[...] = _mm(h_ref[...], w_ref[...]).astype(BF16)

    return pl.pallas_call(
        body, name=name,
        grid_spec=pltpu.PrefetchScalarGridSpec(
            num_scalar_prefetch=2, grid=(T // tm, w_ids.shape[0]),
            in_specs=[pl.BlockSpec((tm, D), lambda i, j, w, p: (i, 0)),
                      pl.BlockSpec((None, D, D), lambda i, j, w, p: (w[j], 0, 0)), HBM],
            out_specs=pl.BlockSpec((None, tm, D), lambda i, j, w, p: (p[j], i, 0))),
        out_shape=SDS(proj.shape, BF16), input_output_aliases={4: 0},
        compiler_params=_params(2))(w_ids, p_ids, h, w_blocks, proj)


def _halo_before(split, tm):
    return pl.BlockSpec((None, HALO, D), lambda i: (split, jnp.maximum(i * (tm // HALO) - 1, 0), 0))


def _fwd_mix(proj, cw0, cw1, cw2, w_co, w_pool, mem, gain_mem, w_kv, tm):
    T = proj.shape[1]
    M = mem.shape[0]

    def body(b_ref, c_ref, ua_ref, up_ref, ch_ref, uah_ref, uph_ref, cw0_ref, cw1_ref, cw2_ref, wco_ref, wp_ref,
             mem_ref, gm_ref, wkv_ref, za_ref, conv_ref, pooled_ref, ya_ref, yp_ref, kv_ref, memn_ref):
        i = pl.program_id(0)

        @pl.when(i == 0)
        def _():
            m = mem_ref[...]
            memn = (m * _rms(m) * gm_ref[...]).astype(BF16)
            memn_ref[...] = memn
            for j in range(2 * NH):
                kv_ref[j] = _mm(memn, wkv_ref[j]).astype(BF16)
        keep = jnp.where(i > 0, 1.0, 0.0).astype(F32)
        cu = c_ref[...].astype(F32) * ua_ref[...].astype(F32)
        cu_h = ch_ref[...].astype(F32) * uah_ref[...].astype(F32) * keep
        ext = jnp.concatenate([cu_h, cu], axis=0)
        conv = (cw2_ref[...] * ext + cw1_ref[...] * _shift_down(ext, 1) + cw0_ref[...] * _shift_down(ext, 2))[HALO:]
        za = (b_ref[...].astype(F32) * conv).astype(BF16)
        conv_ref[...] = conv.astype(BF16)
        za_ref[...] = za
        ya_ref[...] = _mm(za, wco_ref[...]).astype(BF16)

        up = up_ref[...].astype(F32)
        ext_u = jnp.concatenate([uph_ref[...].astype(F32) * keep, up], axis=0)
        pos = i * tm + lax.broadcasted_iota(jnp.int32, (tm, HD), 0)
        for g in range(NPOOL):
            cols = slice(g * HD, (g + 1) * HD)
            s = ext_u[:, cols]
            for k in range(g + 1):
                s = s + _shift_down(s, 1 << k)
            cnt = jnp.minimum(pos + 1, 2 << g).astype(F32)
            pooled = (s[HALO:] / cnt - up[:, cols]).astype(BF16)
            pooled_ref[:, cols] = pooled
            yp_ref[:, cols] = _mm(pooled, wp_ref[g]).astype(BF16)

    tile = lambda s: pl.BlockSpec((None, tm, D), lambda i: (s, i, 0))
    row = pl.BlockSpec((1, D), lambda i: (0, 0))
    out = pl.BlockSpec((tm, D), lambda i: (i, 0))
    return pl.pallas_call(
        body, name="fwd_mix", grid=(T // tm,),
        in_specs=[tile(0), tile(1), tile(2), tile(3), _halo_before(1, tm), _halo_before(2, tm), _halo_before(3, tm),
                  row, row, row, _whole((D, D)), _whole((NPOOL, HD, HD)), _whole((M, D)), row, _whole((2 * NH, D, HD))],
        out_specs=[out] * 5 + [pl.BlockSpec((2 * NH, M, HD), lambda i: (0, 0, 0)), pl.BlockSpec((M, D), lambda i: (0, 0))],
        out_shape=[SDS((T, D), BF16)] * 5 + [SDS((2 * NH, M, HD), BF16), SDS((M, D), BF16)],
        compiler_params=_params(1))(proj, proj, proj, proj, proj, proj, proj, cw0, cw1, cw2, w_co, w_pool, mem, gain_mem, w_kv)


def _softmax_rows(s):
    e = jnp.exp(s - jnp.max(s, axis=-1, keepdims=True))
    return e / jnp.sum(e, axis=-1, keepdims=True)


def _fwd_merge(proj, ya, yp, x, kv, w_xo, w_o, pscale, gain_ffn, tm):
    T = x.shape[0]

    def body(q_ref, ga_ref, gp_ref, gx_ref, ya_ref, yp_ref, x_ref, kv_ref, wxo_ref, wo_ref, ps_ref, gf_ref,
             o_ref, yx_ref, merged_ref, x1_ref, h2_ref):
        for h in range(NH):
            cols = slice(h * HD, (h + 1) * HD)
            p = _softmax_rows(_mm_nt(q_ref[:, cols], kv_ref[h]) * ATT_SCALE)
            o_ref[:, cols] = _mm(p.astype(BF16), kv_ref[NH + h]).astype(BF16)
        yx = _mm(o_ref[...], wxo_ref[...])
        yx_ref[...] = yx.astype(BF16)
        merged = (_sigmoid(ga_ref[...].astype(F32)) * ya_ref[...].astype(F32)
                  + _sigmoid(gp_ref[...].astype(F32)) * (yp_ref[...].astype(F32) * ps_ref[...])
                  + _sigmoid(gx_ref[...].astype(F32)) * yx).astype(BF16)
        merged_ref[...] = merged
        x1 = x_ref[...] + _mm(merged, wo_ref[...])
        x1_ref[...] = x1
        h2_ref[...] = (x1 * _rms(x1) * gf_ref[...]).astype(BF16)

    tile = lambda s: pl.BlockSpec((None, tm, D), lambda i: (s, i, 0))
    row = pl.BlockSpec((1, D), lambda i: (0, 0))
    act = pl.BlockSpec((tm, D), lambda i: (i, 0))
    full = _whole((D, D))
    return pl.pallas_call(
        body, name="fwd_merge", grid=(T // tm,),
        in_specs=[tile(4), tile(5), tile(6), tile(7), act, act, act,
                  _whole((2 * NH, kv.shape[1], HD)), full, full, row, row],
        out_specs=[act] * 5,
        out_shape=[SDS((T, D), BF16), SDS((T, D), BF16), SDS((T, D), BF16), SDS((T, D), F32), SDS((T, D), BF16)],
        compiler_params=_params(1))(proj, proj, proj, proj, ya, yp, x, kv, w_xo, w_o, pscale, gain_ffn)


def _fwd_ffn_up(h2, wg_t, wu_t, tm, tn):
    T = h2.shape[0]

    def body(h_ref, wg_ref, wu_ref, gate_ref, up_ref, act_ref):
        for cols in _col_chunks(tn):
            gate = _mm_nt(h_ref[...], wg_ref[cols, :])
            up = _mm_nt(h_ref[...], wu_ref[cols, :])
            gate_ref[:, cols] = gate.astype(BF16)
            up_ref[:, cols] = up.astype(BF16)
            act_ref[:, cols] = (gate * _sigmoid(gate) * up).astype(BF16)

    w = pl.BlockSpec((tn, D), lambda n, i: (n, 0))
    o = pl.BlockSpec((tm, tn), lambda n, i: (i, n))
    return pl.pallas_call(
        body, name="fwd_ffn_up", grid=(DFF // tn, T // tm),
        in_specs=[pl.BlockSpec((tm, D), lambda n, i: (i, 0)), w, w],
        out_specs=[o] * 3, out_shape=[SDS((T, DFF), BF16)] * 3,
        compiler_params=_params(2))(h2, wg_t, wu_t)


def _fwd_ffn_down_loss(act, w_d, x1, target, gain_final, tm):
    T = x1.shape[0]

    def body(act_ref, wd_ref, x1_ref, tgt_ref, g_ref, dx2_ref, loss_ref, dgain_ref):
        @pl.when(pl.program_id(0) == 0)
        def _():
            loss_ref[...] = jnp.zeros_like(loss_ref)
            dgain_ref[...] = jnp.zeros_like(dgain_ref)
        x2 = x1_ref[...] + _mm(act_ref[...], wd_ref[...])
        gain = g_ref[...]
        y = x2 * _rms(x2) * gain
        err = y - tgt_ref[...]
        loss_ref[...] += 0.5 * jnp.sum(jnp.mean(err * err, axis=-1, keepdims=True))
        dx2, dgain = _norm_bwd(err * (1.0 / D), x2, gain)
        dx2_ref[...] = dx2
        dgain_ref[...] += dgain

    act_spec = pl.BlockSpec((tm, D), lambda i: (i, 0))
    row = pl.BlockSpec((1, D), lambda i: (0, 0))
    return pl.pallas_call(
        body, name="fwd_ffn_down_loss", grid=(T // tm,),
        in_specs=[pl.BlockSpec((tm, DFF), lambda i: (i, 0)), _whole((DFF, D)), act_spec, act_spec, row],
        out_specs=[act_spec, pl.BlockSpec((8, D), lambda i: (0, 0)), row],
        out_shape=[SDS((T, D), F32), SDS((8, D), F32), SDS((1, D), F32)],
        compiler_params=_params(1))(act, w_d, x1, target, gain_final)


def _bwd_ffn_down(dx2, w_d, gate, up, tm, tn):
    T = dx2.shape[0]

    def body(dx_ref, wd_ref, gate_ref, up_ref, dgate_ref, dup_ref):
        dx = dx_ref[...].astype(BF16)
        for cols in _col_chunks(tn):
            dact = _mm_nt(dx, wd_ref[cols, :])
            gate = gate_ref[:, cols].astype(F32)
            sg = _sigmoid(gate)
            dgate_ref[:, cols] = (dact * up_ref[:, cols].astype(F32) * (sg * (1.0 + gate * (1.0 - sg)))).astype(BF16)
            dup_ref[:, cols] = (dact * gate * sg).astype(BF16)

    o = pl.BlockSpec((tm, tn), lambda n, i: (i, n))
    return pl.pallas_call(
        body, name="bwd_ffn_down", grid=(DFF // tn, T // tm),
        in_specs=[pl.BlockSpec((tm, D), lambda n, i: (i, 0)), pl.BlockSpec((tn, D), lambda n, i: (n, 0)), o, o],
        out_specs=[o] * 2, out_shape=[SDS((T, DFF), BF16)] * 2,
        compiler_params=_params(2))(dx2, w_d, gate, up)


def _bwd_ffn_up(dgate, dup, wg_t, wu_t, x1, dx2, gain_ffn, tm):
    T = x1.shape[0]

    def body(dg_ref, du_ref, wg_ref, wu_ref, x1_ref, dx2_ref, g_ref, dx1_ref, dgain_ref):
        @pl.when(pl.program_id(0) == 0)
        def _():
            dgain_ref[...] = jnp.zeros_like(dgain_ref)
        dh2 = _mm(dg_ref[...], wg_ref[...]) + _mm(du_ref[...], wu_ref[...])
        dx, dgain = _norm_bwd(dh2, x1_ref[...], g_ref[...])
        dx1_ref[...] = dx2_ref[...] + dx
        dgain_ref[...] += dgain

    wide = pl.BlockSpec((tm, DFF), lambda i: (i, 0))
    w = _whole((DFF, D))
    act = pl.BlockSpec((tm, D), lambda i: (i, 0))
    row = pl.BlockSpec((1, D), lambda i: (0, 0))
    return pl.pallas_call(
        body, name="bwd_ffn_up", grid=(T // tm,),
        in_specs=[wide, wide, w, w, act, act, row], out_specs=[act, row],
        out_shape=[SDS((T, D), F32), SDS((1, D), F32)],
        compiler_params=_params(1))(dgate, dup, wg_t, wu_t, x1, dx2, gain_ffn)


def _wgrad(a, b, *, name, groups, a_cols, b_cols, tt, a_index, b_index, o_index, out_shape, after):
    T = a.shape[0]
    nt = T // tt
    n_a = a.shape[1] // a_cols if groups == 1 else 1

    def body(a_ref, b_ref, after_ref, o_ref, acc_ref):
        del after_ref
        t = pl.program_id(2)

        @pl.when(t == 0)
        def _():
            acc_ref[...] = jnp.zeros_like(acc_ref)
        acc_ref[...] += _mm_tn(a_ref[...].astype(BF16), b_ref[...].astype(BF16))

        @pl.when(t == nt - 1)
        def _():
            o_ref[...] = acc_ref[...].astype(o_ref.dtype)

    return pl.pallas_call(
        body, name=name, grid=(groups, n_a, nt),
        in_specs=[pl.BlockSpec((tt, a_cols), a_index), pl.BlockSpec((None, tt, b_cols), b_index), HBM],
        out_specs=pl.BlockSpec((None, a_cols, b_cols), o_index),
        out_shape=SDS(out_shape, BF16),
        scratch_shapes=[pltpu.VMEM((a_cols, b_cols), F32)],
        compiler_params=_params(3))(a, b, after)


def _wgrad_dense(a, b, name, tt, after, a_cols=None):
    ka, nb = a.shape[1], b.shape[1]
    a_cols = ka if a_cols is None else a_cols
    out = _wgrad(a, b[None], name=name, groups=1, a_cols=a_cols, b_cols=nb, tt=tt,
                 a_index=lambda g, k, t: (t, k), b_index=lambda g, k, t: (0, t, 0),
                 o_index=lambda g, k, t: (k, 0, 0), out_shape=(ka // a_cols, a_cols, nb), after=after)
    return out.reshape(ka, nb)


def _bwd_merge(dx1, proj, ya, yp, yx, pooled, pscale, w_o, w_co, w_xo, w_pool, tm):
    T = dx1.shape[0]
    nt = T // tm

    def body(dx1_ref, ga_ref, gp_ref, gx_ref, ya_ref, yp_ref, yx_ref, pooled_ref, ps_ref, wo_ref, wco_ref, wxo_ref, wp_ref,
             dgates_ref, dya_ref, dyx_ref, dza_ref, do_ref, dpooled_ref, dps_ref, dwp_ref, acc_ref):
        @pl.when(pl.program_id(0) == 0)
        def _():
            dps_ref[...] = jnp.zeros_like(dps_ref)
            acc_ref[...] = jnp.zeros_like(acc_ref)
        dmerged = _mm_nt(dx1_ref[...].astype(BF16), wo_ref[...])
        scale = ps_ref[...]
        sa, sp, sx = (_sigmoid(r[...].astype(F32)) for r in (ga_ref, gp_ref, gx_ref))
        ya, yp_pre, yx = (r[...].astype(F32) for r in (ya_ref, yp_ref, yx_ref))
        dgates_ref[0] = (dmerged * ya * sa * (1.0 - sa)).astype(BF16)
        dgates_ref[1] = (dmerged * (yp_pre * scale) * sp * (1.0 - sp)).astype(BF16)
        dgates_ref[2] = (dmerged * yx * sx * (1.0 - sx)).astype(BF16)
        dya = (dmerged * sa).astype(BF16)
        dyx = (dmerged * sx).astype(BF16)
        dyp = dmerged * sp
        dyps = (dyp * scale).astype(BF16)
        dps_ref[...] += jnp.sum(dyp * yp_pre, axis=0, keepdims=True)
        dya_ref[...] = dya
        dyx_ref[...] = dyx
        dza_ref[...] = _mm_nt(dya, wco_ref[...]).astype(BF16)
        do_ref[...] = _mm_nt(dyx, wxo_ref[...]).astype(BF16)
        for g in range(NPOOL):
            cols = slice(g * HD, (g + 1) * HD)
            dpooled_ref[:, cols] = _mm_nt(dyps[:, cols], wp_ref[g]).astype(BF16)
            acc_ref[g] += _mm_tn(pooled_ref[:, cols], dyps[:, cols])

        @pl.when(pl.program_id(0) == nt - 1)
        def _():
            dwp_ref[...] = acc_ref[...].astype(BF16)

    tile = lambda s: pl.BlockSpec((None, tm, D), lambda i: (s, i, 0))
    row = pl.BlockSpec((1, D), lambda i: (0, 0))
    act = pl.BlockSpec((tm, D), lambda i: (i, 0))
    full = _whole((D, D))
    return pl.pallas_call(
        body, name="bwd_merge", grid=(T // tm,),
        in_specs=[act, tile(5), tile(6), tile(7), act, act, act, act, row, full, full, full,
                  _whole((NPOOL, HD, HD))],
        out_specs=[pl.BlockSpec((3, tm, D), lambda i: (0, i, 0))] + [act] * 5
        + [row, pl.BlockSpec((NPOOL, HD, HD), lambda i: (0, 0, 0))],
        out_shape=[SDS((NSPLIT, T, D), BF16)] + [SDS((T, D), BF16)] * 5 + [SDS((1, D), F32), SDS((NPOOL, HD, HD), BF16)],
        scratch_shapes=[pltpu.VMEM((NPOOL, HD, HD), F32)],
        compiler_params=_params(1))(dx1, proj, proj, proj, ya, yp, yx, pooled, pscale, w_o, w_co, w_xo, w_pool)


def _bwd_attn(dproj, proj, do, kv, memn, w_kv, mem, gain_mem, tm):
    T = do.shape[0]
    M = kv.shape[1]
    nt = T // tm

    def body(dproj_hbm, q_ref, do_ref, kv_ref, memn_ref, wkv_ref, mem_ref, gm_ref, dq_ref, dw_ref, dgain_ref, dkv_ref):
        del dproj_hbm

        @pl.when(pl.program_id(0) == 0)
        def _():
            dkv_ref[...] = jnp.zeros_like(dkv_ref)
        for h in range(NH):
            cols = slice(h * HD, (h + 1) * HD)
            q = q_ref[:, cols]
            do_h = do_ref[:, cols]
            p = _softmax_rows(_mm_nt(q, kv_ref[h]) * ATT_SCALE)
            dp = _mm_nt(do_h, kv_ref[NH + h])
            ds = (p * (dp - jnp.sum(dp * p, axis=-1, keepdims=True)) * ATT_SCALE).astype(BF16)
            dq_ref[:, cols] = _mm(ds, kv_ref[h]).astype(BF16)
            dkv_ref[h] += _mm_tn(ds, q)
            dkv_ref[NH + h] += _mm_tn(p.astype(BF16), do_h)

        @pl.when(pl.program_id(0) == nt - 1)
        def _():
            dmemn = jnp.zeros((M, D), F32)
            for j in range(2 * NH):
                dkv_j = dkv_ref[j].astype(BF16)
                dw_ref[j] = _mm_tn(memn_ref[...], dkv_j).astype(BF16)
                dmemn = dmemn + _mm_nt(dkv_j, wkv_ref[j])
            dgain_ref[...] = _norm_bwd(dmemn, mem_ref[...], gm_ref[...])[1]

    row = pl.BlockSpec((1, D), lambda i: (0, 0))
    return pl.pallas_call(
        body, name="bwd_attn", grid=(nt,),
        in_specs=[HBM, pl.BlockSpec((None, tm, D), lambda i: (4, i, 0)), pl.BlockSpec((tm, D), lambda i: (i, 0)),
                  _whole((2 * NH, M, HD)), _whole((M, D)), _whole((2 * NH, D, HD)), _whole((M, D)), row],
        out_specs=[pl.BlockSpec((None, tm, D), lambda i: (3, i, 0)),
                   pl.BlockSpec((2 * NH, D, HD), lambda i: (0, 0, 0)), row],
        out_shape=[SDS(dproj.shape, BF16), SDS((2 * NH, D, HD), BF16), SDS((1, D), F32)],
        scratch_shapes=[pltpu.VMEM((2 * NH, M, HD), F32)],
        input_output_aliases={0: 0},
        compiler_params=_params(1))(dproj, proj, do, kv, memn, w_kv, mem, gain_mem)


def _bwd_mix(dproj, proj, conv, dza, dpooled, cw0, cw1, cw2, tm):
    T = dza.shape[0]
    nt = T // tm

    def halo_after(split_or_none):
        idx = lambda i: jnp.minimum((i + 1) * (tm // HALO), T // HALO - 1)
        if split_or_none is None:
            return pl.BlockSpec((HALO, D), lambda i: (idx(i), 0))
        return pl.BlockSpec((None, HALO, D), lambda i: (split_or_none, idx(i), 0))

    def body(dproj_hbm, b_ref, c_ref, ua_ref, conv_ref, dza_ref, dpo_ref, bn_ref, dzan_ref, dpon_ref, ch_ref, uah_ref,
             cw0_ref, cw1_ref, cw2_ref, dabcu_ref, dcw_ref):
        del dproj_hbm
        i = pl.program_id(0)

        @pl.when(i == 0)
        def _():
            dcw_ref[...] = jnp.zeros_like(dcw_ref)
        keep_prev = jnp.where(i > 0, 1.0, 0.0).astype(F32)
        keep_next = jnp.where(i < nt - 1, 1.0, 0.0).astype(F32)
        dza = dza_ref[...].astype(F32)
        c = c_ref[...].astype(F32)
        ua = ua_ref[...].astype(F32)
        dconv = dza * b_ref[...].astype(F32)
        dconv_n = dzan_ref[...].astype(F32) * bn_ref[...].astype(F32) * keep_next
        ext = jnp.concatenate([dconv, dconv_n], axis=0)
        dcu = (cw2_ref[...] * ext + cw1_ref[...] * _shift_up(ext, 1) + cw0_ref[...] * _shift_up(ext, 2))[:tm]
        dabcu_ref[0] = (dza * conv_ref[...].astype(F32)).astype(BF16)
        dabcu_ref[1] = (dcu * ua).astype(BF16)
        dabcu_ref[2] = (dcu * c).astype(BF16)

        cu = c * ua
        ext_cu = jnp.concatenate([ch_ref[...].astype(F32) * uah_ref[...].astype(F32) * keep_prev, cu], axis=0)
        dcw_ref[2:3, :] += jnp.sum(dconv * cu, axis=0, keepdims=True)
        dcw_ref[1:2, :] += jnp.sum(dconv * _shift_down(ext_cu, 1)[HALO:], axis=0, keepdims=True)
        dcw_ref[0:1, :] += jnp.sum(dconv * _shift_down(ext_cu, 2)[HALO:], axis=0, keepdims=True)

        dpo = dpo_ref[...].astype(F32)
        ext_dpo = jnp.concatenate([dpo, dpon_ref[...].astype(F32) * keep_next], axis=0)
        pos = i * tm + lax.broadcasted_iota(jnp.int32, (tm + HALO, HD), 0)
        for g in range(NPOOL):
            cols = slice(g * HD, (g + 1) * HD)
            s = ext_dpo[:, cols] / jnp.minimum(pos + 1, 2 << g).astype(F32)
            for k in range(g + 1):
                s = s + _shift_up(s, 1 << k)
            dabcu_ref[3, :, cols] = (s[:tm] - dpo[:, cols]).astype(BF16)

    tile = lambda s: pl.BlockSpec((None, tm, D), lambda i: (s, i, 0))
    act = pl.BlockSpec((tm, D), lambda i: (i, 0))
    row = pl.BlockSpec((1, D), lambda i: (0, 0))
    return pl.pallas_call(
        body, name="bwd_mix", grid=(nt,),
        in_specs=[HBM, tile(0), tile(1), tile(2), act, act, act, halo_after(0), halo_after(None), halo_after(None),
                  _halo_before(1, tm), _halo_before(2, tm), row, row, row],
        out_specs=[pl.BlockSpec((4, tm, D), lambda i: (1, i, 0)), pl.BlockSpec((8, D), lambda i: (0, 0))],
        out_shape=[SDS(dproj.shape, BF16), SDS((8, D), F32)],
        input_output_aliases={0: 0},
        compiler_params=_params(1))(dproj, proj, proj, proj, conv, dza, dpooled, proj, dza, dpooled, proj, proj, cw0, cw1, cw2)


def _bwd_proj(dproj, w_in_g, x, dx1, gain, tm):
    T = x.shape[0]

    def body(dp_ref, w_ref, x_ref, dx1_ref, g_ref, dx_ref, dgain_ref, acc_ref):
        i, s = pl.program_id(0), pl.program_id(1)

        @pl.when((i == 0) & (s == 0))
        def _():
            dgain_ref[...] = jnp.zeros_like(dgain_ref)

        @pl.when(s == 0)
        def _():
            acc_ref[...] = jnp.zeros_like(acc_ref)
        acc_ref[...] += _mm_nt(dp_ref[...], w_ref[...])

        @pl.when(s == NSPLIT - 1)
        def _():
            dx, dgain = _norm_bwd(acc_ref[...], x_ref[...], g_ref[...])
            dx_ref[...] = dx1_ref[...] + dx
            dgain_ref[...] += dgain

    act = pl.BlockSpec((tm, D), lambda i, s: (i, 0))
    row = pl.BlockSpec((1, D), lambda i, s: (0, 0))
    return pl.pallas_call(
        body, name="bwd_proj", grid=(T // tm, NSPLIT),
        in_specs=[pl.BlockSpec((None, tm, D), lambda i, s: (s, i, 0)),
                  pl.BlockSpec((None, D, D), lambda i, s: (_slot_group(s), 0, 0)), act, act, row],
        out_specs=[act, row], out_shape=[SDS((T, D), F32), SDS((1, D), F32)],
        scratch_shapes=[pltpu.VMEM((tm, D), F32)],
        compiler_params=_params(2))(dproj, w_in_g, x, dx1, gain)


def _adamw_math(w, g, m, v):
    m = ADAM_B1 * m + (1.0 - ADAM_B1) * g
    v = ADAM_B2 * v + (1.0 - ADAM_B2) * (g * g)
    m_hat = m / (1.0 - ADAM_B1 ** ADAM_STEP)
    v_hat = v / (1.0 - ADAM_B2 ** ADAM_STEP)
    delta = -ADAM_LR * (m_hat / (jnp.sqrt(v_hat) + ADAM_EPS) + ADAM_WD * w)
    return delta, m, v


def _row_tile(rows):
    return 256 if rows % 256 == 0 else rows


def _sum_parts(parts, name):
    n_parts, rows, cols = parts.shape
    tr = _row_tile(rows)

    def body(p_ref, g_ref):
        g = p_ref[0].astype(F32)
        for k in range(1, n_parts):
            g = g + p_ref[k].astype(F32)
        g_ref[...] = g

    blk = pl.BlockSpec((tr, cols), lambda i: (i, 0))
    return pl.pallas_call(
        body, name=name, grid=(rows // tr,),
        in_specs=[pl.BlockSpec((n_parts, tr, cols), lambda i: (0, i, 0))], out_specs=blk,
        out_shape=SDS((rows, cols), F32), compiler_params=_params(1))(parts)


def _adamw(w, g, m, v, name, from_parts):
    rows, cols = w.shape
    tr = _row_tile(rows)

    def body(w_ref, g_ref, m_ref, v_ref, go_ref, d_ref, mo_ref, vo_ref):
        if from_parts:
            g = g_ref[0].astype(F32)
            for k in range(1, g_ref.shape[0]):
                g = g + g_ref[k].astype(F32)
        else:
            g = g_ref[...]
        go_ref[...] = g
        d_ref[...], mo_ref[...], vo_ref[...] = _adamw_math(w_ref[...], g, m_ref[...], v_ref[...])

    blk = pl.BlockSpec((tr, cols), lambda i: (i, 0))
    g_spec = pl.BlockSpec((g.shape[0], tr, cols), lambda i: (0, i, 0)) if from_parts else blk
    return pl.pallas_call(
        body, name=name, grid=(rows // tr,),
        in_specs=[blk, g_spec, blk, blk], out_specs=[blk] * 4,
        out_shape=[SDS((rows, cols), F32)] * 4, compiler_params=_params(1))(w, g, m, v)


def _peer(k, x, y, c):
    return ((1 - x) if k & 4 else x, (1 - y) if k & 2 else y, (1 - c) if k & 1 else c)


def _exchange(arrays, name, scatter):
    n = len(arrays)

    def body(*refs):
        ins, outs = refs[:n], refs[n:2 * n]
        send_sems, recv_sems, local_sems = refs[2 * n:]
        x, y, c = (lax.axis_index(a) for a in AXES)
        me = 4 * x + 2 * y + c

        def remote(a, k):
            px, py, pc = _peer(k, x, y, c)
            there = 4 * px + 2 * py + pc
            return pltpu.make_async_remote_copy(
                src_ref=ins[a].at[there] if scatter else ins[a], dst_ref=outs[a].at[me],
                send_sem=send_sems.at[a, k - 1], recv_sem=recv_sems.at[a, k - 1],
                device_id=(px, py, pc), device_id_type=pl.DeviceIdType.MESH)

        def arrival(a, k):
            px, py, pc = _peer(k, x, y, c)
            there = 4 * px + 2 * py + pc
            return pltpu.make_async_remote_copy(
                src_ref=ins[a].at[there] if scatter else ins[a], dst_ref=outs[a].at[there],
                send_sem=send_sems.at[a, k - 1], recv_sem=recv_sems.at[a, k - 1],
                device_id=(px, py, pc), device_id_type=pl.DeviceIdType.MESH)

        own = [pltpu.make_async_copy(ins[a].at[me] if scatter else ins[a], outs[a].at[me], local_sems.at[a]) for a in range(n)]
        for a in range(n):
            own[a].start()
            for k in range(1, NDEV):
                remote(a, k).start()
        for a in range(n):
            for k in range(1, NDEV):
                arrival(a, k).wait_recv()
        for a in range(n):
            for k in range(1, NDEV):
                remote(a, k).wait_send()
            own[a].wait()

    out_shape = [SDS(a.shape if scatter else (NDEV,) + a.shape, a.dtype) for a in arrays]
    return pl.pallas_call(
        body, name=name, in_specs=[HBM] * n, out_specs=[HBM] * n, out_shape=out_shape,
        scratch_shapes=[pltpu.SemaphoreType.DMA((n, NDEV - 1)), pltpu.SemaphoreType.DMA((n, NDEV - 1)),
                        pltpu.SemaphoreType.DMA((n,))],
        compiler_params=pltpu.CompilerParams(has_side_effects=True))(*arrays)


SEM = pl.BlockSpec(memory_space=pltpu.SEMAPHORE)
IN_HBM = pl.BlockSpec(memory_space=pltpu.HBM)
DATAFLOW = pltpu.SideEffectType.DATAFLOW_SIDE_EFFECTING
TOKEN_SHAPE = (8, 128)


OTHER_CHIPS = (2, 4, 6)


def _place(x, y, c):
    return 4 * x + 2 * y + c


def _plan_gather_chips(n, ks=(1,) + OTHER_CHIPS):
    def plan(refs, x, y, c, arriving):
        out = []
        for a in range(n):
            for k in ks:
                there = _place(*_peer(k, x, y, c))
                out.append((refs[a], refs[n + a].at[there if arriving else _place(x, y, c)], k))
        return out
    return plan, n * len(ks)


def _plan_gather_sibling(n, ks=OTHER_CHIPS):
    def plan(refs, x, y, c, arriving):
        out = []
        for a in range(n):
            for k in ks:
                px, py, pc = _peer(k, x, y, c)
                mine, theirs = _place(px, py, pc), _place(px, py, 1 - pc)
                out.append((refs[a].at[mine], refs[a].at[theirs if arriving else mine], 1))
        return out
    return plan, n * len(ks)


def _plan_pair():
    def plan(refs, x, y, c, arriving):
        return [(refs[0], refs[1].at[(1 - c) if arriving else c], 1)]
    return plan, 1


def _plan_far_chip():
    def plan(refs, x, y, c, arriving):
        return [(refs[0], refs[1].at[c], 6)]
    return plan, 1


def _plan_far_sibling():
    def plan(refs, x, y, c, arriving):
        return [(refs[0].at[c], refs[0].at[(1 - c) if arriving else c], 1)]
    return plan, 1


def _plan_scatter_sibling(n):
    def plan(refs, x, y, c, arriving):
        out = []
        for a in range(n):
            for q in range(4):
                out.append((refs[a].at[2 * q + (1 - c)], refs[n + a].at[q], 1))
        return out
    return plan, n * 4


def _plan_scatter_chips(n):
    def plan(refs, x, y, c, arriving):
        out = []
        for a in range(n):
            for k in OTHER_CHIPS:
                px, py, _ = _peer(k, x, y, c)
                out.append((refs[a].at[2 * px + py], refs[n + a].at[(2 * px + py) if arriving else (2 * x + y)], k))
        return out
    return plan, n * 3


def _remote(src, dst, send_sems, recv_sems, i, k):
    x, y, c = (lax.axis_index(n) for n in AXES)
    return pltpu.make_async_remote_copy(src_ref=src, dst_ref=dst, send_sem=send_sems.at[i], recv_sem=recv_sems.at[i],
                                        device_id=_peer(k, x, y, c), device_id_type=pl.DeviceIdType.MESH)


def _copies_start(groups, name, after):
    ng = len(groups)
    total = sum(len(bufs) for bufs, _ in groups)

    def body(*refs):
        sems = refs[1 + total:1 + total + 2 * ng]
        x, y, c = (lax.axis_index(n) for n in AXES)
        off = 1
        for gi, (bufs, (plan, _)) in enumerate(groups):
            for i, (src, dst, k) in enumerate(plan(refs[off:off + len(bufs)], x, y, c, False)):
                _remote(src, dst, sems[2 * gi], sems[2 * gi + 1], i, k).start()
            off += len(bufs)
        refs[-1][...] = jnp.zeros(TOKEN_SHAPE, F32)

    sem_shapes = [pltpu.SemaphoreType.DMA((count,)) for _, (_, count) in groups for _ in range(2)]
    flat = [b for bufs, _ in groups for b in bufs]
    outs = pl.pallas_call(
        body, name=name,
        in_specs=[HBM] + [IN_HBM] * total,
        out_specs=[SEM] * (2 * ng) + [IN_HBM] * total + [pl.BlockSpec(memory_space=pltpu.VMEM)],
        out_shape=sem_shapes + [pltpu.HBM(b.shape, b.dtype) for b in flat] + [SDS(TOKEN_SHAPE, F32)],
        input_output_aliases={1 + i: 2 * ng + i for i in range(total)},
        compiler_params=pltpu.CompilerParams(has_side_effects=DATAFLOW),
    )(after, *[pltpu.with_memory_space_constraint(b, pltpu.HBM) for b in flat])
    handles, off = [], 2 * ng
    for gi, (bufs, _) in enumerate(groups):
        handles.append((outs[2 * gi], outs[2 * gi + 1], list(outs[off:off + len(bufs)])))
        off += len(bufs)
    return handles, outs[-1]


def _copies_wait_start(handle, plan, pass_on, more, name, after):
    send_sems, recv_sems, bufs = handle
    n = len(bufs)
    idx, (pass_plan, pass_count) = pass_on
    total = sum(len(b) for b, _ in more)
    ng = 1 + len(more)

    def body(*refs):
        x, y, c = (lax.axis_index(a) for a in AXES)
        waited = refs[1:1 + n]
        outs = refs[3 + n + total:]
        new_sems = outs[n + total:n + total + 2 * ng]
        for i, (src, dst, k) in enumerate(plan[0](waited, x, y, c, True)):
            copy = _remote(src, dst, refs[1 + n + total], refs[2 + n + total], i, k)
            copy.wait_send()
            copy.wait_recv()
        for i, (src, dst, k) in enumerate(pass_plan([waited[j] for j in idx], x, y, c, False)):
            _remote(src, dst, new_sems[0], new_sems[1], i, k).start()
        off = 1 + n
        for gi, (b, (p, _)) in enumerate(more):
            for i, (src, dst, k) in enumerate(p(refs[off:off + len(b)], x, y, c, False)):
                _remote(src, dst, new_sems[2 + 2 * gi], new_sems[3 + 2 * gi], i, k).start()
            off += len(b)
        outs[-1][...] = jnp.zeros(TOKEN_SHAPE, F32)

    flat = list(bufs) + [a for b, _ in more for a in b]
    sem_shapes = [pltpu.SemaphoreType.DMA((count,)) for count in [pass_count] + [cnt for _, (_, cnt) in more] for _ in range(2)]
    outs = pl.pallas_call(
        body, name=name,
        in_specs=[HBM] + [IN_HBM] * (n + total) + [SEM, SEM],
        out_specs=[IN_HBM] * (n + total) + [SEM] * (2 * ng) + [pl.BlockSpec(memory_space=pltpu.VMEM)],
        out_shape=[pltpu.HBM(b.shape, b.dtype) for b in flat] + sem_shapes + [SDS(TOKEN_SHAPE, F32)],
        input_output_aliases={1 + i: i for i in range(n + total)},
        compiler_params=pltpu.CompilerParams(has_side_effects=DATAFLOW),
    )(after, *[pltpu.with_memory_space_constraint(b, pltpu.HBM) for b in flat], send_sems, recv_sems)
    thru = list(outs[:n])
    sems_out = outs[n + total:n + total + 2 * ng]
    handles = [(sems_out[0], sems_out[1], [thru[j] for j in idx])]
    off = n
    for gi, (b, _) in enumerate(more):
        handles.append((sems_out[2 + 2 * gi], sems_out[3 + 2 * gi], list(outs[off:off + len(b)])))
        off += len(b)
    return thru, handles, outs[-1]


def _copies_wait(handle, plan, name, after):
    send_sems, recv_sems, bufs = handle
    n = len(bufs)

    def body(*refs):
        x, y, c = (lax.axis_index(a) for a in AXES)
        for i, (src, dst, k) in enumerate(plan[0](refs[:n], x, y, c, True)):
            copy = _remote(src, dst, refs[n], refs[n + 1], i, k)
            copy.wait_send()
            copy.wait_recv()

    return pl.pallas_call(
        body, name=name,
        in_specs=[IN_HBM] * n + [SEM, SEM, HBM], out_specs=[IN_HBM] * n,
        out_shape=[pltpu.HBM(b.shape, b.dtype) for b in bufs],
        input_output_aliases={i: i for i in range(n)},
        compiler_params=pltpu.CompilerParams(has_side_effects=DATAFLOW),
    )(*bufs, send_sems, recv_sems, after)


def _pair_sums(mine, theirs, c, chip, name):
    n = len(mine)

    def body(where_ref, *refs):
        q = pl.program_id(0)
        for a in range(n):
            total = (refs[a][...].astype(F32) + refs[n + a][...].astype(F32)).astype(BF16)
            refs[2 * n + a][...] = total

            @pl.when(q == where_ref[1])
            def _():
                refs[3 * n + a][...] = total

    block = lambda t: (None,) + t.shape[1:]
    zeros = lambda t: (0,) * (t.ndim - 1)
    outs = pl.pallas_call(
        body, name=name,
        grid_spec=pltpu.PrefetchScalarGridSpec(
            num_scalar_prefetch=1, grid=(4,),
            in_specs=[pl.BlockSpec(block(t), lambda q, w, z=zeros(t): (2 * q + w[0],) + z) for t in theirs]
            + [pl.BlockSpec(block(t), lambda q, w, z=zeros(t): (q,) + z) for t in theirs],
            out_specs=[pl.BlockSpec(block(t), lambda q, w, z=zeros(t): (q,) + z) for t in theirs]
            + [pl.BlockSpec(block(t), lambda q, w, z=zeros(t): (w[1],) + z) for t in theirs]),
        out_shape=[SDS(t.shape, BF16) for t in theirs] * 2,
        compiler_params=_params(1))(jnp.stack([c, chip]).astype(jnp.int32), *mine, *theirs)
    return list(outs[:n]), list(outs[n:])


def _local_step(x, mem, target, gains, get, put, flush, share, tm_huge=2048, tm_big=1024, tm_mid=512, tm_small=256):
    g_mix, pscale, g_mem, g_ffn, g_fin = gains
    T = x.shape[0]
    tm_huge, tm_big, tm_mid, tm_small = min(tm_huge, T), min(tm_big, T), min(tm_mid, T), min(tm_small, T)
    tn = DFF // 2

    w_pair, w_ids, p_ids = get("in_pair", x)
    proj, h = _fwd_proj(x, g_mix, w_pair, w_ids, p_ids, tm_huge)
    w_near, w_ids, p_ids = get("in_near", h)
    proj = _fwd_proj_more(h, w_near, proj, w_ids, p_ids, tm_huge, "fwd_proj_near")
    w_far, w_ids, p_ids = get("in_far", proj)
    proj = _fwd_proj_more(h, w_far, proj, w_ids, p_ids, tm_huge, "fwd_proj_far")
    w_in = get("in_whole", (w_pair, w_near, w_far))
    cw0, cw1, cw2, w_co, w_pool, w_kv = get("mix", proj)
    za, conv, pooled, ya, yp, kv, memn = _fwd_mix(proj, cw0, cw1, cw2, w_co, w_pool, mem, g_mem, w_kv, tm_mid)
    w_xo, w_o = get("merge", ya)
    o, yx, merged, x1, h2 = _fwd_merge(proj, ya, yp, x, kv, w_xo, w_o, pscale, g_ffn, tm_mid)
    wg_t, wu_t = get("gate_up", x1)
    get("down", x1, early=True)
    gate, up, act = _fwd_ffn_up(h2, wg_t, wu_t, tm_mid, tn)
    (w_d,) = get("down", gate)
    dx2, loss, dg_fin = _fwd_ffn_down_loss(act, w_d, x1, target, g_fin, tm_mid)

    dgate, dup = _bwd_ffn_down(dx2, w_d, gate, up, tm_mid, tn)
    dx1, dg_ffn = _bwd_ffn_up(dgate, dup, wg_t, wu_t, x1, dx2, g_ffn, tm_mid)
    dw_d = _wgrad_dense(act, dx2, "wgrad_down", tm_huge, g_mix, a_cols=tn)
    dwg_t = _wgrad_dense(dgate, h2, "wgrad_gate", tm_huge, g_mix, a_cols=tn)
    dwu_t = _wgrad_dense(dup, h2, "wgrad_up", tm_huge, g_mix, a_cols=tn)
    token = put("ffn", (dwg_t, dwu_t, dw_d))

    dproj, dya, dyx, dza, do, dpooled, dpscale, dw_pool = _bwd_merge(
        dx1, proj, ya, yp, yx, pooled, pscale + token[0:1, 0:1], w_o, w_co, w_xo, w_pool, tm_mid)
    token = flush(dya)
    dw_o = _wgrad_dense(merged, dx1, "wgrad_out", tm_big, token)
    dw_co = _wgrad_dense(za, dya, "wgrad_conv_out", tm_big, token)
    dw_xo = _wgrad_dense(o, dyx, "wgrad_xattn_out", tm_big, token)
    dproj, dw_kv, dg_mem = _bwd_attn(dproj, proj, do, kv, memn, w_kv, mem, g_mem, tm_big)
    token = put("mix", (dw_co, dw_xo, dw_o, dw_pool, dw_kv))

    dproj, dcw = _bwd_mix(dproj, proj, conv, dza, dpooled, cw0 + token[0:1, 0:1], cw1, cw2, tm_mid)
    token = flush(dcw)
    dw_in = _wgrad(h, dproj, name="wgrad_in", groups=NSPLIT, a_cols=D, b_cols=D, tt=tm_huge,
                   a_index=lambda g, k, t: (t, 0), b_index=lambda g, k, t: (g, t, 0),
                   o_index=lambda g, k, t: (_slot_group(g), 0, 0), out_shape=(NSPLIT, D, D), after=token)
    token = flush(put("in", (dw_in,)))
    grad_x, dg_mix = _bwd_proj(dproj, w_in, x, dx1, g_mix + token[0:1, 0:1], tm_big)

    small = share(jnp.concatenate([dg_mix, dpscale, dg_mem, dg_ffn, dg_fin, dcw[0:3], loss], axis=0))
    return grad_x, small


def kernel(x, mem, norm_mix, w_in, conv_w, w_conv_out, w_pool, pool_scale, norm_mem, w_kv, w_xattn_out, w_out, norm_ffn, w_gate, w_up, w_down, norm_final, loss_target, m_norm_mix, m_w_in, m_conv_w, m_w_conv_out, m_w_pool, m_pool_scale, m_norm_mem, m_w_kv, m_w_xattn_out, m_w_out, m_norm_ffn, m_w_gate, m_w_up, m_w_down, m_norm_final, v_norm_mix, v_w_in, v_conv_w, v_w_conv_out, v_w_pool, v_pool_scale, v_norm_mem, v_w_kv, v_w_xattn_out, v_w_out, v_norm_ffn, v_w_gate, v_w_up, v_w_down, v_norm_final):
    T = x.shape[1]
    rows = D // NDEV
    ffb = DFF // NDEV
    prow = HD // NDEV
    me = 4 * lax.axis_index("x") + 2 * lax.axis_index("y") + lax.axis_index("c")

    shards = [w_in[0].astype(BF16), w_conv_out[0].astype(BF16), w_xattn_out[0].astype(BF16), w_out[0].astype(BF16),
              w_pool[0].astype(BF16).reshape(NPOOL * prow, HD), w_kv[0].astype(BF16),
              w_gate[0].T.astype(BF16), w_up[0].T.astype(BF16), w_down[0].astype(BF16),
              jnp.pad(conv_w[0], ((0, 5), (0, 0)))]

    cx, cy, cc = (lax.axis_index(n) for n in AXES)
    chip = 2 * cx + cy

    def land(own, index, slots):
        return lax.dynamic_update_index_in_dim(lax.empty((slots,) + own.shape, own.dtype), own, index, 0)

    needed = ["in_pair", "in_near", "in_far", "mix", "merge", "gate_up", "down"]
    members = {"mix": [9, 1, 4, 5], "gate_up": [6, 7], "merge": [2, 3], "down": [8]}
    near = (2, 4)
    plans = {"in_pair": _plan_pair(), "in_near": _plan_gather_chips(1, near), "in_far": _plan_far_chip()}
    plans.update({n: _plan_gather_chips(len(members[n])) for n in members})
    g_bufs = {"in_pair": [shards[0], land(shards[0], cc, 2)],
              "in_near": [w_in[0].astype(BF16), lax.empty((NDEV, D, D), BF16)],
              "in_far": [w_in[0].astype(BF16), lax.empty((2, D, D), BF16)]}
    g_bufs.update({n: [shards[i] for i in members[n]] + [land(shards[i], me, NDEV) for i in members[n]] for n in members})
    first_handles, _ = _copies_start([(g_bufs[n], plans[n]) for n in needed[:2]], "gather_start", x)
    g_handles = dict(zip(needed[:2], first_handles))
    pair_ids = jnp.array([0, 1], jnp.int32)

    on_last_leg = {}

    def get(group, after, early=False):
        if group == "in_whole":
            w_pair, w_near, w_far = after
            w_whole = lax.dynamic_update_slice_in_dim(w_near, w_pair, 2 * chip, 0)
            return lax.dynamic_update_slice_in_dim(w_whole, w_far, 2 * (3 - chip), 0)
        if group == "in_pair":
            bufs = _copies_wait(g_handles[group], plans[group], "gather_wait_" + group, after)
            return bufs[1], pair_ids, (2 * chip + pair_ids).astype(jnp.int32)
        if group not in on_last_leg:
            n_bufs = len(g_bufs[group])
            landed = list(range(n_bufs // 2, n_bufs))
            if group == "in_near":
                plan, more = _plan_gather_sibling(1, near), [(g_bufs[n], plans[n]) for n in needed[2:]]
            elif group == "in_far":
                plan, more = _plan_far_sibling(), []
            else:
                plan, more = _plan_gather_sibling(n_bufs // 2), []
            _, handles, token = _copies_wait_start(g_handles[group], plans[group], (landed, plan), more,
                                                   "gather_pass_" + group, after)
            g_handles.update(zip(needed[2:], handles[1:]))
            on_last_leg[group] = (handles[0], plan, token)
        if early:
            return None
        handle, plan, token = on_last_leg[group]
        got = _copies_wait(handle, plan, "gather_passed_" + group, after if group in ("gate_up", "down") else token)
        if group == "in_near":
            groups = jnp.stack([me ^ k for k in (2, 3, 4, 5)]).astype(jnp.int32)
            return got[0], groups, groups
        if group == "in_far":
            return got[0], pair_ids, (2 * (3 - chip) + pair_ids).astype(jnp.int32)
        if group == "mix":
            cw_g, w_co_g, w_pool_g, w_kv_g = got
            cw_full = cw_g.transpose(1, 0, 2).reshape(8, D)
            w_pool_full = w_pool_g.reshape(NDEV, NPOOL, prow, HD).transpose(1, 0, 2, 3).reshape(NPOOL, HD, HD)
            return cw_full[0:1], cw_full[1:2], cw_full[2:3], w_co_g.reshape(D, D), w_pool_full, w_kv_g
        if group == "merge":
            return got[0].reshape(D, D), got[1].reshape(D, D)
        return [g.reshape(DFF, D) for g in got]

    started = {}

    def put(group, grads):
        if group == "ffn":
            sends = [g.reshape(NDEV, ffb, D) for g in grads]
        elif group == "mix":
            dw_co, dw_xo, dw_o, dw_pool, dw_kv = grads
            sends = [dw_co.reshape(NDEV, rows, D), dw_xo.reshape(NDEV, rows, D), dw_o.reshape(NDEV, rows, D),
                     dw_pool.reshape(NPOOL, NDEV, prow, HD).transpose(1, 0, 2, 3).reshape(NDEV, NPOOL * prow, HD), dw_kv]
        else:
            sends = list(grads)
        n = len(sends)
        halves = [lax.empty((4,) + s.shape[1:], s.dtype) for s in sends]
        (handle,), token = _copies_start([(sends + halves, _plan_scatter_sibling(n))], "scatter_swap_" + group, norm_mix)
        swapping.append((group, handle, n))
        return token

    swapping = []

    def flush(after):
        group, handle, n = swapping.pop()
        bufs = _copies_wait(handle, _plan_scatter_sibling(n), "scatter_swapped_" + group, after)
        sums, lands = _pair_sums(bufs[:n], bufs[n:], cc, chip, "pair_sums_" + group)
        (handle,), token = _copies_start([(sums + lands, _plan_scatter_chips(n))], "scatter_start_" + group, norm_mix)
        started[group] = (handle, _plan_scatter_chips(n))
        return token

    def take(group, after):
        handle, plan = started[group]
        return _copies_wait(handle, plan, "scatter_wait_" + group, after)[len(handle[2]) // 2:]

    def share(rows):
        (everyone,) = _exchange([rows], "gather_small", scatter=False)
        return _sum_parts(everyone, "sum_small")

    gains = (norm_mix, pool_scale, norm_mem, norm_ffn, norm_final.reshape(1, D))
    grad_x, small_sum = _local_step(x[0], mem[0], loss_target[0], gains, get, put, flush, share)
    loss = small_sum[8, 0]

    def sharded(name, w, parts, m, v):
        shape = w.shape
        flat = lambda a: a.reshape(parts.shape[1], parts.shape[2])
        outs = _adamw(flat(w), parts, flat(m), flat(v), "adamw_" + name, from_parts=True)
        return [o.reshape(shape) for o in outs]

    def transposed(name, w, parts, m, v):
        outs = _adamw(w[0].T, parts, m[0].T, v[0].T, "adamw_" + name, from_parts=True)
        return [o.T[None] for o in outs]

    def replicated(name, w, g, m, v):
        shape = w.shape
        flat = lambda a: a.reshape(g.shape)
        outs = _adamw(flat(w), g, flat(m), flat(v), "adamw_" + name, from_parts=False)
        return [o.reshape(shape) for o in outs]

    g_cw = lax.dynamic_slice_in_dim(small_sum[5:8], me * rows, rows, axis=1)
    res = {
        "norm_mix": replicated("norm_mix", norm_mix, small_sum[0:1], m_norm_mix, v_norm_mix),
        "conv_w": replicated("conv_w", conv_w, g_cw, m_conv_w, v_conv_w),
        "pool_scale": replicated("pool_scale", pool_scale, small_sum[1:2], m_pool_scale, v_pool_scale),
        "norm_mem": replicated("norm_mem", norm_mem, small_sum[2:3], m_norm_mem, v_norm_mem),
        "norm_ffn": replicated("norm_ffn", norm_ffn, small_sum[3:4], m_norm_ffn, v_norm_ffn),
        "norm_final": replicated("norm_final", norm_final, small_sum[4:5], m_norm_final, v_norm_final),
    }
    p_g, p_u, p_d = take("ffn", res["norm_final"][1])
    res["w_gate"] = transposed("w_gate", w_gate, p_g, m_w_gate, v_w_gate)
    res["w_up"] = transposed("w_up", w_up, p_u, m_w_up, v_w_up)
    res["w_down"] = sharded("w_down", w_down, p_d, m_w_down, v_w_down)
    p_co, p_xo, p_o, p_pool, p_kv = take("mix", res["w_down"][1])
    res["w_conv_out"] = sharded("w_conv_out", w_conv_out, p_co, m_w_conv_out, v_w_conv_out)
    res["w_pool"] = sharded("w_pool", w_pool, p_pool, m_w_pool, v_w_pool)
    res["w_kv"] = sharded("w_kv", w_kv, p_kv, m_w_kv, v_w_kv)
    res["w_xattn_out"] = sharded("w_xattn_out", w_xattn_out, p_xo, m_w_xattn_out, v_w_xattn_out)
    res["w_out"] = sharded("w_out", w_out, p_o, m_w_out, v_w_out)
    (p_in,) = take("in", res["w_out"][1])
    res["w_in"] = sharded("w_in", w_in, p_in, m_w_in, v_w_in)
    order = ["norm_mix", "w_in", "conv_w", "w_conv_out", "w_pool", "pool_scale", "norm_mem", "w_kv", "w_xattn_out", "w_out",
             "norm_ffn", "w_gate", "w_up", "w_down", "norm_final"]
    return (loss, grad_x[None], *[res[n][0] for n in order], *[res[n][1] for n in order],
            *[res[n][2] for n in order], *[res[n][3] for n in order])
```

```python
import jax
import jax.numpy as jnp
from jax import lax
from jax.experimental import pallas as pl
from jax.experimental.pallas import tpu as pltpu

F32 = jnp.float32
BF16 = jnp.bfloat16
SDS = jax.ShapeDtypeStruct

AXES = ("x", "y", "c")
NDEV = 8
D = 1024
NSPLIT = 8
NH = 4
HD = D // NH
NPOOL = 4
DFF = 2816
EPS = 1e-6
ATT_SCALE = HD ** -0.5
HALO = 16


def _slot_group(s):
    return jnp.where(s < 3, s + 5, jnp.where(s == 3, 4, s - 4))


ADAM_LR = 0.001
ADAM_B1 = 0.9
ADAM_B2 = 0.999
ADAM_EPS = 1e-08
ADAM_WD = 0.01
ADAM_STEP = 10

V7X_VMEM_BYTES = 64 * 1024 * 1024
VMEM_LIMIT = V7X_VMEM_BYTES - 8 * 1024 * 1024
HBM = pl.BlockSpec(memory_space=pl.ANY)


def _whole(shape):
    return pl.BlockSpec(shape, lambda *_: (0,) * len(shape), pipeline_mode=pl.Buffered(1))


def _params(n_grid):
    return pltpu.CompilerParams(dimension_semantics=("arbitrary",) * n_grid, vmem_limit_bytes=VMEM_LIMIT)


def _mm(a, b):
    return jnp.dot(a, b, preferred_element_type=F32)


def _mm_nt(a, b):
    return lax.dot_general(a, b, (((1,), (1,)), ((), ())), preferred_element_type=F32)


def _mm_tn(a, b):
    return lax.dot_general(a, b, (((0,), (0,)), ((), ())), preferred_element_type=F32)


def _sigmoid(x):
    return 1.0 / (1.0 + jnp.exp(-x))


def _rms(x):
    return lax.rsqrt(jnp.mean(x * x, axis=-1, keepdims=True) + EPS)


def _norm_bwd(dh, x, gain):
    r = _rms(x)
    xh = x * r
    dxh = dh * gain
    dx = r * (dxh - xh * jnp.mean(dxh * xh, axis=-1, keepdims=True))
    return dx, jnp.sum(dh * xh, axis=0, keepdims=True)


def _col_chunks(n, width=512):
    return [slice(c, min(c + width, n)) for c in range(0, n, width)]


def _shift_down(v, k):
    return pltpu.roll(v, k, 0)


def _shift_up(v, k):
    return pltpu.roll(v, v.shape[0] - k, 0)


def _fwd_proj(x, gain, w_blocks, w_ids, p_ids, tm):
    T = x.shape[0]

    def body(w_ids_ref, p_ids_ref, x_ref, g_ref, w_ref, proj_ref, h_ref):
        del w_ids_ref, p_ids_ref

        @pl.when(pl.program_id(1) == 0)
        def _():
            xf = x_ref[...]
            h_ref[...] = (xf * _rms(xf) * g_ref[...]).astype(BF16)
        proj_ref[...] = _mm(h_ref[...], w_ref[...]).astype(BF16)

    return pl.pallas_call(
        body, name="fwd_proj",
        grid_spec=pltpu.PrefetchScalarGridSpec(
            num_scalar_prefetch=2, grid=(T // tm, w_ids.shape[0]),
            in_specs=[pl.BlockSpec((tm, D), lambda i, j, w, p: (i, 0)), pl.BlockSpec((1, D), lambda i, j, w, p: (0, 0)),
                      pl.BlockSpec((None, D, D), lambda i, j, w, p: (w[j], 0, 0))],
            out_specs=[pl.BlockSpec((None, tm, D), lambda i, j, w, p: (p[j], i, 0)),
                       pl.BlockSpec((tm, D), lambda i, j, w, p: (i, 0))]),
        out_shape=[SDS((NSPLIT, T, D), BF16), SDS((T, D), BF16)],
        compiler_params=_params(2))(w_ids, p_ids, x, gain, w_blocks)


def _fwd_proj_more(h, w_blocks, proj, w_ids, p_ids, tm, name):
    T = h.shape[0]

    def body(w_ids_ref, p_ids_ref, h_ref, w_ref, proj_hbm, proj_ref):
        del w_ids_ref, p_ids_ref, proj_hbm
        proj_ref[...] = _mm(h_ref[...], w_ref[...]).astype(BF16)

    return pl.pallas_call(
        body, name=name,
        grid_spec=pltpu.PrefetchScalarGridSpec(
            num_scalar_prefetch=2, grid=(T // tm, w_ids.shape[0]),
            in_specs=[pl.BlockSpec((tm, D), lambda i, j, w, p: (i, 0)),
                      pl.BlockSpec((None, D, D), lambda i, j, w, p: (w[j], 0, 0)), HBM],
            out_specs=pl.BlockSpec((None, tm, D), lambda i, j, w, p: (p[j], i, 0))),
        out_shape=SDS(proj.shape, BF16), input_output_aliases={4: 0},
        compiler_params=_params(2))(w_ids, p_ids, h, w_blocks, proj)


def _halo_before(split, tm):
    return pl.BlockSpec((None, HALO, D), lambda i: (split, jnp.maximum(i * (tm // HALO) - 1, 0), 0))


def _fwd_mix(proj, cw0, cw1, cw2, w_co, w_pool, mem, gain_mem, w_kv, tm):
    T = proj.shape[1]
    M = mem.shape[0]

    def body(b_ref, c_ref, ua_ref, up_ref, ch_ref, uah_ref, uph_ref, cw0_ref, cw1_ref, cw2_ref, wco_ref, wp_ref,
             mem_ref, gm_ref, wkv_ref, za_ref, conv_ref, pooled_ref, ya_ref, yp_ref, kv_ref, memn_ref):
        i = pl.program_id(0)

        @pl.when(i == 0)
        def _():
            m = mem_ref[...]
            memn = (m * _rms(m) * gm_ref[...]).astype(BF16)
            memn_ref[...] = memn
            for j in range(2 * NH):
                kv_ref[j] = _mm(memn, wkv_ref[j]).astype(BF16)
        keep = jnp.where(i > 0, 1.0, 0.0).astype(F32)
        cu = c_ref[...].astype(F32) * ua_ref[...].astype(F32)
        cu_h = ch_ref[...].astype(F32) * uah_ref[...].astype(F32) * keep
        ext = jnp.concatenate([cu_h, cu], axis=0)
        conv = (cw2_ref[...] * ext + cw1_ref[...] * _shift_down(ext, 1) + cw0_ref[...] * _shift_down(ext, 2))[HALO:]
        za = (b_ref[...].astype(F32) * conv).astype(BF16)
        conv_ref[...] = conv.astype(BF16)
        za_ref[...] = za
        ya_ref[...] = _mm(za, wco_ref[...]).astype(BF16)

        up = up_ref[...].astype(F32)
        ext_u = jnp.concatenate([uph_ref[...].astype(F32) * keep, up], axis=0)
        pos = i * tm + lax.broadcasted_iota(jnp.int32, (tm, HD), 0)
        for g in range(NPOOL):
            cols = slice(g * HD, (g + 1) * HD)
            s = ext_u[:, cols]
            for k in range(g + 1):
                s = s + _shift_down(s, 1 << k)
            cnt = jnp.minimum(pos + 1, 2 << g).astype(F32)
            pooled = (s[HALO:] / cnt - up[:, cols]).astype(BF16)
            pooled_ref[:, cols] = pooled
            yp_ref[:, cols] = _mm(pooled, wp_ref[g]).astype(BF16)

    tile = lambda s: pl.BlockSpec((None, tm, D), lambda i: (s, i, 0))
    row = pl.BlockSpec((1, D), lambda i: (0, 0))
    out = pl.BlockSpec((tm, D), lambda i: (i, 0))
    return pl.pallas_call(
        body, name="fwd_mix", grid=(T // tm,),
        in_specs=[tile(0), tile(1), tile(2), tile(3), _halo_before(1, tm), _halo_before(2, tm), _halo_before(3, tm),
                  row, row, row, _whole((D, D)), _whole((NPOOL, HD, HD)), _whole((M, D)), row, _whole((2 * NH, D, HD))],
        out_specs=[out] * 5 + [pl.BlockSpec((2 * NH, M, HD), lambda i: (0, 0, 0)), pl.BlockSpec((M, D), lambda i: (0, 0))],
        out_shape=[SDS((T, D), BF16)] * 5 + [SDS((2 * NH, M, HD), BF16), SDS((M, D), BF16)],
        compiler_params=_params(1))(proj, proj, proj, proj, proj, proj, proj, cw0, cw1, cw2, w_co, w_pool, mem, gain_mem, w_kv)


def _softmax_rows(s):
    e = jnp.exp(s - jnp.max(s, axis=-1, keepdims=True))
    return e / jnp.sum(e, axis=-1, keepdims=True)


def _fwd_merge(proj, ya, yp, x, kv, w_xo, w_o, pscale, gain_ffn, tm):
    T = x.shape[0]

    def body(q_ref, ga_ref, gp_ref, gx_ref, ya_ref, yp_ref, x_ref, kv_ref, wxo_ref, wo_ref, ps_ref, gf_ref,
             o_ref, yx_ref, merged_ref, x1_ref, h2_ref):
        for h in range(NH):
            cols = slice(h * HD, (h + 1) * HD)
            p = _softmax_rows(_mm_nt(q_ref[:, cols], kv_ref[h]) * ATT_SCALE)
            o_ref[:, cols] = _mm(p.astype(BF16), kv_ref[NH + h]).astype(BF16)
        yx = _mm(o_ref[...], wxo_ref[...])
        yx_ref[...] = yx.astype(BF16)
        merged = (_sigmoid(ga_ref[...].astype(F32)) * ya_ref[...].astype(F32)
                  + _sigmoid(gp_ref[...].astype(F32)) * (yp_ref[...].astype(F32) * ps_ref[...])
                  + _sigmoid(gx_ref[...].astype(F32)) * yx).astype(BF16)
        merged_ref[...] = merged
        x1 = x_ref[...] + _mm(merged, wo_ref[...])
        x1_ref[...] = x1
        h2_ref[...] = (x1 * _rms(x1) * gf_ref[...]).astype(BF16)

    tile = lambda s: pl.BlockSpec((None, tm, D), lambda i: (s, i, 0))
    row = pl.BlockSpec((1, D), lambda i: (0, 0))
    act = pl.BlockSpec((tm, D), lambda i: (i, 0))
    full = _whole((D, D))
    return pl.pallas_call(
        body, name="fwd_merge", grid=(T // tm,),
        in_specs=[tile(4), tile(5), tile(6), tile(7), act, act, act,
                  _whole((2 * NH, kv.shape[1], HD)), full, full, row, row],
        out_specs=[act] * 5,
        out_shape=[SDS((T, D), BF16), SDS((T, D), BF16), SDS((T, D), BF16), SDS((T, D), F32), SDS((T, D), BF16)],
        compiler_params=_params(1))(proj, proj, proj, proj, ya, yp, x, kv, w_xo, w_o, pscale, gain_ffn)


def _fwd_ffn_up(h2, wg_t, wu_t, tm, tn):
    T = h2.shape[0]

    def body(h_ref, wg_ref, wu_ref, gate_ref, up_ref, act_ref):
        for cols in _col_chunks(tn):
            gate = _mm_nt(h_ref[...], wg_ref[cols, :])
            up = _mm_nt(h_ref[...], wu_ref[cols, :])
            gate_ref[:, cols] = gate.astype(BF16)
            up_ref[:, cols] = up.astype(BF16)
            act_ref[:, cols] = (gate * _sigmoid(gate) * up).astype(BF16)

    w = pl.BlockSpec((tn, D), lambda n, i: (n, 0))
    o = pl.BlockSpec((tm, tn), lambda n, i: (i, n))
    return pl.pallas_call(
        body, name="fwd_ffn_up", grid=(DFF // tn, T // tm),
        in_specs=[pl.BlockSpec((tm, D), lambda n, i: (i, 0)), w, w],
        out_specs=[o] * 3, out_shape=[SDS((T, DFF), BF16)] * 3,
        compiler_params=_params(2))(h2, wg_t, wu_t)


def _fwd_ffn_down_loss(act, w_d, x1, target, gain_final, tm):
    T = x1.shape[0]

    def body(act_ref, wd_ref, x1_ref, tgt_ref, g_ref, dx2_ref, loss_ref, dgain_ref):
        @pl.when(pl.program_id(0) == 0)
        def _():
            loss_ref[...] = jnp.zeros_like(loss_ref)
            dgain_ref[...] = jnp.zeros_like(dgain_ref)
        x2 = x1_ref[...] + _mm(act_ref[...], wd_ref[...])
        gain = g_ref[...]
        y = x2 * _rms(x2) * gain
        err = y - tgt_ref[...]
        loss_ref[...] += 0.5 * jnp.sum(jnp.mean(err * err, axis=-1, keepdims=True))
        dx2, dgain = _norm_bwd(err * (1.0 / D), x2, gain)
        dx2_ref[...] = dx2
        dgain_ref[...] += dgain

    act_spec = pl.BlockSpec((tm, D), lambda i: (i, 0))
    row = pl.BlockSpec((1, D), lambda i: (0, 0))
    return pl.pallas_call(
        body, name="fwd_ffn_down_loss", grid=(T // tm,),
        in_specs=[pl.BlockSpec((tm, DFF), lambda i: (i, 0)), _whole((DFF, D)), act_spec, act_spec, row],
        out_specs=[act_spec, pl.BlockSpec((8, D), lambda i: (0, 0)), row],
        out_shape=[SDS((T, D), F32), SDS((8, D), F32), SDS((1, D), F32)],
        compiler_params=_params(1))(act, w_d, x1, target, gain_final)


def _bwd_ffn_down(dx2, w_d, gate, up, tm, tn):
    T = dx2.shape[0]

    def body(dx_ref, wd_ref, gate_ref, up_ref, dgate_ref, dup_ref):
        dx = dx_ref[...].astype(BF16)
        for cols in _col_chunks(tn):
            dact = _mm_nt(dx, wd_ref[cols, :])
            gate = gate_ref[:, cols].astype(F32)
            sg = _sigmoid(gate)
            dgate_ref[:, cols] = (dact * up_ref[:, cols].astype(F32) * (sg * (1.0 + gate * (1.0 - sg)))).astype(BF16)
            dup_ref[:, cols] = (dact * gate * sg).astype(BF16)

    o = pl.BlockSpec((tm, tn), lambda n, i: (i, n))
    return pl.pallas_call(
        body, name="bwd_ffn_down", grid=(DFF // tn, T // tm),
        in_specs=[pl.BlockSpec((tm, D), lambda n, i: (i, 0)), pl.BlockSpec((tn, D), lambda n, i: (n, 0)), o, o],
        out_specs=[o] * 2, out_shape=[SDS((T, DFF), BF16)] * 2,
        compiler_params=_params(2))(dx2, w_d, gate, up)


def _bwd_ffn_up(dgate, dup, wg_t, wu_t, x1, dx2, gain_ffn, tm):
    T = x1.shape[0]

    def body(dg_ref, du_ref, wg_ref, wu_ref, x1_ref, dx2_ref, g_ref, dx1_ref, dgain_ref):
        @pl.when(pl.program_id(0) == 0)
        def _():
            dgain_ref[...] = jnp.zeros_like(dgain_ref)
        dh2 = _mm(dg_ref[...], wg_ref[...]) + _mm(du_ref[...], wu_ref[...])
        dx, dgain = _norm_bwd(dh2, x1_ref[...], g_ref[...])
        dx1_ref[...] = dx2_ref[...] + dx
        dgain_ref[...] += dgain

    wide = pl.BlockSpec((tm, DFF), lambda i: (i, 0))
    w = _whole((DFF, D))
    act = pl.BlockSpec((tm, D), lambda i: (i, 0))
    row = pl.BlockSpec((1, D), lambda i: (0, 0))
    return pl.pallas_call(
        body, name="bwd_ffn_up", grid=(T // tm,),
        in_specs=[wide, wide, w, w, act, act, row], out_specs=[act, row],
        out_shape=[SDS((T, D), F32), SDS((1, D), F32)],
        compiler_params=_params(1))(dgate, dup, wg_t, wu_t, x1, dx2, gain_ffn)


def _wgrad(a, b, *, name, groups, a_cols, b_cols, tt, a_index, b_index, o_index, out_shape, after):
    T = a.shape[0]
    nt = T // tt
    n_a = a.shape[1] // a_cols if groups == 1 else 1

    def body(a_ref, b_ref, after_ref, o_ref, acc_ref):
        del after_ref
        t = pl.program_id(2)

        @pl.when(t == 0)
        def _():
            acc_ref[...] = jnp.zeros_like(acc_ref)
        acc_ref[...] += _mm_tn(a_ref[...].astype(BF16), b_ref[...].astype(BF16))

        @pl.when(t == nt - 1)
        def _():
            o_ref[...] = acc_ref[...].astype(o_ref.dtype)

    return pl.pallas_call(
        body, name=name, grid=(groups, n_a, nt),
        in_specs=[pl.BlockSpec((tt, a_cols), a_index), pl.BlockSpec((None, tt, b_cols), b_index), HBM],
        out_specs=pl.BlockSpec((None, a_cols, b_cols), o_index),
        out_shape=SDS(out_shape, BF16),
        scratch_shapes=[pltpu.VMEM((a_cols, b_cols), F32)],
        compiler_params=_params(3))(a, b, after)


def _wgrad_dense(a, b, name, tt, after, a_cols=None):
    ka, nb = a.shape[1], b.shape[1]
    a_cols = ka if a_cols is None else a_cols
    out = _wgrad(a, b[None], name=name, groups=1, a_cols=a_cols, b_cols=nb, tt=tt,
                 a_index=lambda g, k, t: (t, k), b_index=lambda g, k, t: (0, t, 0),
                 o_index=lambda g, k, t: (k, 0, 0), out_shape=(ka // a_cols, a_cols, nb), after=after)
    return out.reshape(ka, nb)


def _bwd_merge(dx1, proj, ya, yp, yx, pooled, pscale, w_o, w_co, w_xo, w_pool, tm):
    T = dx1.shape[0]
    nt = T // tm

    def body(dx1_ref, ga_ref, gp_ref, gx_ref, ya_ref, yp_ref, yx_ref, pooled_ref, ps_ref, wo_ref, wco_ref, wxo_ref, wp_ref,
             dgates_ref, dya_ref, dyx_ref, dza_ref, do_ref, dpooled_ref, dps_ref, dwp_ref, acc_ref):
        @pl.when(pl.program_id(0) == 0)
        def _():
            dps_ref[...] = jnp.zeros_like(dps_ref)
            acc_ref[...] = jnp.zeros_like(acc_ref)
        dmerged = _mm_nt(dx1_ref[...].astype(BF16), wo_ref[...])
        scale = ps_ref[...]
        sa, sp, sx = (_sigmoid(r[...].astype(F32)) for r in (ga_ref, gp_ref, gx_ref))
        ya, yp_pre, yx = (r[...].astype(F32) for r in (ya_ref, yp_ref, yx_ref))
        dgates_ref[0] = (dmerged * ya * sa * (1.0 - sa)).astype(BF16)
        dgates_ref[1] = (dmerged * (yp_pre * scale) * sp * (1.0 - sp)).astype(BF16)
        dgates_ref[2] = (dmerged * yx * sx * (1.0 - sx)).astype(BF16)
        dya = (dmerged * sa).astype(BF16)
        dyx = (dmerged * sx).astype(BF16)
        dyp = dmerged * sp
        dyps = (dyp * scale).astype(BF16)
        dps_ref[...] += jnp.sum(dyp * yp_pre, axis=0, keepdims=True)
        dya_ref[...] = dya
        dyx_ref[...] = dyx
        dza_ref[...] = _mm_nt(dya, wco_ref[...]).astype(BF16)
        do_ref[...] = _mm_nt(dyx, wxo_ref[...]).astype(BF16)
        for g in range(NPOOL):
            cols = slice(g * HD, (g + 1) * HD)
            dpooled_ref[:, cols] = _mm_nt(dyps[:, cols], wp_ref[g]).astype(BF16)
            acc_ref[g] += _mm_tn(pooled_ref[:, cols], dyps[:, cols])

        @pl.when(pl.program_id(0) == nt - 1)
        def _():
            dwp_ref[...] = acc_ref[...].astype(BF16)

    tile = lambda s: pl.BlockSpec((None, tm, D), lambda i: (s, i, 0))
    row = pl.BlockSpec((1, D), lambda i: (0, 0))
    act = pl.BlockSpec((tm, D), lambda i: (i, 0))
    full = _whole((D, D))
    return pl.pallas_call(
        body, name="bwd_merge", grid=(T // tm,),
        in_specs=[act, tile(5), tile(6), tile(7), act, act, act, act, row, full, full, full,
                  _whole((NPOOL, HD, HD))],
        out_specs=[pl.BlockSpec((3, tm, D), lambda i: (0, i, 0))] + [act] * 5
        + [row, pl.BlockSpec((NPOOL, HD, HD), lambda i: (0, 0, 0))],
        out_shape=[SDS((NSPLIT, T, D), BF16)] + [SDS((T, D), BF16)] * 5 + [SDS((1, D), F32), SDS((NPOOL, HD, HD), BF16)],
        scratch_shapes=[pltpu.VMEM((NPOOL, HD, HD), F32)],
        compiler_params=_params(1))(dx1, proj, proj, proj, ya, yp, yx, pooled, pscale, w_o, w_co, w_xo, w_pool)


def _bwd_attn(dproj, proj, do, kv, memn, w_kv, mem, gain_mem, tm):
    T = do.shape[0]
    M = kv.shape[1]
    nt = T // tm

    def body(dproj_hbm, q_ref, do_ref, kv_ref, memn_ref, wkv_ref, mem_ref, gm_ref, dq_ref, dw_ref, dgain_ref, dkv_ref):
        del dproj_hbm

        @pl.when(pl.program_id(0) == 0)
        def _():
            dkv_ref[...] = jnp.zeros_like(dkv_ref)
        for h in range(NH):
            cols = slice(h * HD, (h + 1) * HD)
            q = q_ref[:, cols]
            do_h = do_ref[:, cols]
            p = _softmax_rows(_mm_nt(q, kv_ref[h]) * ATT_SCALE)
            dp = _mm_nt(do_h, kv_ref[NH + h])
            ds = (p * (dp - jnp.sum(dp * p, axis=-1, keepdims=True)) * ATT_SCALE).astype(BF16)
            dq_ref[:, cols] = _mm(ds, kv_ref[h]).astype(BF16)
            dkv_ref[h] += _mm_tn(ds, q)
            dkv_ref[NH + h] += _mm_tn(p.astype(BF16), do_h)

        @pl.when(pl.program_id(0) == nt - 1)
        def _():
            dmemn = jnp.zeros((M, D), F32)
            for j in range(2 * NH):
                dkv_j = dkv_ref[j].astype(BF16)
                dw_ref[j] = _mm_tn(memn_ref[...], dkv_j).astype(BF16)
                dmemn = dmemn + _mm_nt(dkv_j, wkv_ref[j])
            dgain_ref[...] = _norm_bwd(dmemn, mem_ref[...], gm_ref[...])[1]

    row = pl.BlockSpec((1, D), lambda i: (0, 0))
    return pl.pallas_call(
        body, name="bwd_attn", grid=(nt,),
        in_specs=[HBM, pl.BlockSpec((None, tm, D), lambda i: (4, i, 0)), pl.BlockSpec((tm, D), lambda i: (i, 0)),
                  _whole((2 * NH, M, HD)), _whole((M, D)), _whole((2 * NH, D, HD)), _whole((M, D)), row],
        out_specs=[pl.BlockSpec((None, tm, D), lambda i: (3, i, 0)),
                   pl.BlockSpec((2 * NH, D, HD), lambda i: (0, 0, 0)), row],
        out_shape=[SDS(dproj.shape, BF16), SDS((2 * NH, D, HD), BF16), SDS((1, D), F32)],
        scratch_shapes=[pltpu.VMEM((2 * NH, M, HD), F32)],
        input_output_aliases={0: 0},
        compiler_params=_params(1))(dproj, proj, do, kv, memn, w_kv, mem, gain_mem)


def _bwd_mix(dproj, proj, conv, dza, dpooled, cw0, cw1, cw2, tm):
    T = dza.shape[0]
    nt = T // tm

    def halo_after(split_or_none):
        idx = lambda i: jnp.minimum((i + 1) * (tm // HALO), T // HALO - 1)
        if split_or_none is None:
            return pl.BlockSpec((HALO, D), lambda i: (idx(i), 0))
        return pl.BlockSpec((None, HALO, D), lambda i: (split_or_none, idx(i), 0))

    def body(dproj_hbm, b_ref, c_ref, ua_ref, conv_ref, dza_ref, dpo_ref, bn_ref, dzan_ref, dpon_ref, ch_ref, uah_ref,
             cw0_ref, cw1_ref, cw2_ref, dabcu_ref, dcw_ref):
        del dproj_hbm
        i = pl.program_id(0)

        @pl.when(i == 0)
        def _():
            dcw_ref[...] = jnp.zeros_like(dcw_ref)
        keep_prev = jnp.where(i > 0, 1.0, 0.0).astype(F32)
        keep_next = jnp.where(i < nt - 1, 1.0, 0.0).astype(F32)
        dza = dza_ref[...].astype(F32)
        c = c_ref[...].astype(F32)
        ua = ua_ref[...].astype(F32)
        dconv = dza * b_ref[...].astype(F32)
        dconv_n = dzan_ref[...].astype(F32) * bn_ref[...].astype(F32) * keep_next
        ext = jnp.concatenate([dconv, dconv_n], axis=0)
        dcu = (cw2_ref[...] * ext + cw1_ref[...] * _shift_up(ext, 1) + cw0_ref[...] * _shift_up(ext, 2))[:tm]
        dabcu_ref[0] = (dza * conv_ref[...].astype(F32)).astype(BF16)
        dabcu_ref[1] = (dcu * ua).astype(BF16)
        dabcu_ref[2] = (dcu * c).astype(BF16)

        cu = c * ua
        ext_cu = jnp.concatenate([ch_ref[...].astype(F32) * uah_ref[...].astype(F32) * keep_prev, cu], axis=0)
        dcw_ref[2:3, :] += jnp.sum(dconv * cu, axis=0, keepdims=True)
        dcw_ref[1:2, :] += jnp.sum(dconv * _shift_down(ext_cu, 1)[HALO:], axis=0, keepdims=True)
        dcw_ref[0:1, :] += jnp.sum(dconv * _shift_down(ext_cu, 2)[HALO:], axis=0, keepdims=True)

        dpo = dpo_ref[...].astype(F32)
        ext_dpo = jnp.concatenate([dpo, dpon_ref[...].astype(F32) * keep_next], axis=0)
        pos = i * tm + lax.broadcasted_iota(jnp.int32, (tm + HALO, HD), 0)
        for g in range(NPOOL):
            cols = slice(g * HD, (g + 1) * HD)
            s = ext_dpo[:, cols] / jnp.minimum(pos + 1, 2 << g).astype(F32)
            for k in range(g + 1):
                s = s + _shift_up(s, 1 << k)
            dabcu_ref[3, :, cols] = (s[:tm] - dpo[:, cols]).astype(BF16)

    tile = lambda s: pl.BlockSpec((None, tm, D), lambda i: (s, i, 0))
    act = pl.BlockSpec((tm, D), lambda i: (i, 0))
    row = pl.BlockSpec((1, D), lambda i: (0, 0))
    return pl.pallas_call(
        body, name="bwd_mix", grid=(nt,),
        in_specs=[HBM, tile(0), tile(1), tile(2), act, act, act, halo_after(0), halo_after(None), halo_after(None),
                  _halo_before(1, tm), _halo_before(2, tm), row, row, row],
        out_specs=[pl.BlockSpec((4, tm, D), lambda i: (1, i, 0)), pl.BlockSpec((8, D), lambda i: (0, 0))],
        out_shape=[SDS(dproj.shape, BF16), SDS((8, D), F32)],
        input_output_aliases={0: 0},
        compiler_params=_params(1))(dproj, proj, proj, proj, conv, dza, dpooled, proj, dza, dpooled, proj, proj, cw0, cw1, cw2)


def _bwd_proj(dproj, w_in_g, x, dx1, gain, tm):
    T = x.shape[0]

    def body(dp_ref, w_ref, x_ref, dx1_ref, g_ref, dx_ref, dgain_ref, acc_ref):
        i, s = pl.program_id(0), pl.program_id(1)

        @pl.when((i == 0) & (s == 0))
        def _():
            dgain_ref[...] = jnp.zeros_like(dgain_ref)

        @pl.when(s == 0)
        def _():
            acc_ref[...] = jnp.zeros_like(acc_ref)
        acc_ref[...] += _mm_nt(dp_ref[...], w_ref[...])

        @pl.when(s == NSPLIT - 1)
        def _():
            dx, dgain = _norm_bwd(acc_ref[...], x_ref[...], g_ref[...])
            dx_ref[...] = dx1_ref[...] + dx
            dgain_ref[...] += dgain

    act = pl.BlockSpec((tm, D), lambda i, s: (i, 0))
    row = pl.BlockSpec((1, D), lambda i, s: (0, 0))
    return pl.pallas_call(
        body, name="bwd_proj", grid=(T // tm, NSPLIT),
        in_specs=[pl.BlockSpec((None, tm, D), lambda i, s: (s, i, 0)),
                  pl.BlockSpec((None, D, D), lambda i, s: (_slot_group(s), 0, 0)), act, act, row],
        out_specs=[act, row], out_shape=[SDS((T, D), F32), SDS((1, D), F32)],
        scratch_shapes=[pltpu.VMEM((tm, D), F32)],
        compiler_params=_params(2))(dproj, w_in_g, x, dx1, gain)


def _adamw_math(w, g, m, v):
    m = ADAM_B1 * m + (1.0 - ADAM_B1) * g
    v = ADAM_B2 * v + (1.0 - ADAM_B2) * (g * g)
    m_hat = m / (1.0 - ADAM_B1 ** ADAM_STEP)
    v_hat = v / (1.0 - ADAM_B2 ** ADAM_STEP)
    delta = -ADAM_LR * (m_hat / (jnp.sqrt(v_hat) + ADAM_EPS) + ADAM_WD * w)
    return delta, m, v


def _row_tile(rows):
    return 256 if rows % 256 == 0 else rows


def _sum_parts(parts, name):
    n_parts, rows, cols = parts.shape
    tr = _row_tile(rows)

    def body(p_ref, g_ref):
        g = p_ref[0].astype(F32)
        for k in range(1, n_parts):
            g = g + p_ref[k].astype(F32)
        g_ref[...] = g

    blk = pl.BlockSpec((tr, cols), lambda i: (i, 0))
    return pl.pallas_call(
        body, name=name, grid=(rows // tr,),
        in_specs=[pl.BlockSpec((n_parts, tr, cols), lambda i: (0, i, 0))], out_specs=blk,
        out_shape=SDS((rows, cols), F32), compiler_params=_params(1))(parts)


def _adamw(ws, gs, ms, vs, name, from_parts, steps):
    n = len(ws)

    def body(*refs):
        for a in range(n):
            w_ref, g_ref, m_ref, v_ref = refs[4 * a:4 * a + 4]
            go_ref, d_ref, mo_ref, vo_ref = refs[4 * n + 4 * a:4 * n + 4 * a + 4]
            if from_parts:
                g = g_ref[0].astype(F32)
                for k in range(1, g_ref.shape[0]):
                    g = g + g_ref[k].astype(F32)
            else:
                g = g_ref[...]
            go_ref[...] = g
            d_ref[...], mo_ref[...], vo_ref[...] = _adamw_math(w_ref[...], g, m_ref[...], v_ref[...])

    in_specs, out_specs, out_shape, operands = [], [], [], []
    for w, g, m, v in zip(ws, gs, ms, vs):
        rows, cols = w.shape
        blk = pl.BlockSpec((rows // steps, cols), lambda i: (i, 0))
        g_spec = pl.BlockSpec((g.shape[0], rows // steps, cols), lambda i: (0, i, 0)) if from_parts else blk
        in_specs += [blk, g_spec, blk, blk]
        out_specs += [blk] * 4
        out_shape += [SDS((rows, cols), F32)] * 4
        operands += [w, g, m, v]
    outs = pl.pallas_call(body, name=name, grid=(steps,), in_specs=in_specs, out_specs=out_specs, out_shape=out_shape,
                          compiler_params=_params(1))(*operands)
    return [outs[4 * a:4 * a + 4] for a in range(n)]


def _peer(k, x, y, c):
    return ((1 - x) if k & 4 else x, (1 - y) if k & 2 else y, (1 - c) if k & 1 else c)


SEM = pl.BlockSpec(memory_space=pltpu.SEMAPHORE)
IN_HBM = pl.BlockSpec(memory_space=pltpu.HBM)
DATAFLOW = pltpu.SideEffectType.DATAFLOW_SIDE_EFFECTING
TOKEN_SHAPE = (8, 128)


OTHER_CHIPS = (2, 4, 6)


def _place(x, y, c):
    return 4 * x + 2 * y + c


def _plan_gather_chips(n, ks=(1,) + OTHER_CHIPS):
    def plan(refs, x, y, c, arriving):
        out = []
        for a in range(n):
            for k in ks:
                there = _place(*_peer(k, x, y, c))
                out.append((refs[a], refs[n + a].at[there if arriving else _place(x, y, c)], k))
        return out
    return plan, n * len(ks)


def _plan_gather_sibling(n, ks=OTHER_CHIPS):
    def plan(refs, x, y, c, arriving):
        out = []
        for a in range(n):
            for k in ks:
                px, py, pc = _peer(k, x, y, c)
                mine, theirs = _place(px, py, pc), _place(px, py, 1 - pc)
                out.append((refs[a].at[mine], refs[a].at[theirs if arriving else mine], 1))
        return out
    return plan, n * len(ks)


def _plan_pair():
    def plan(refs, x, y, c, arriving):
        return [(refs[0], refs[1].at[(1 - c) if arriving else c], 1)]
    return plan, 1


def _plan_far_chip():
    def plan(refs, x, y, c, arriving):
        return [(refs[0], refs[1].at[c], 6)]
    return plan, 1


def _plan_far_sibling():
    def plan(refs, x, y, c, arriving):
        return [(refs[0].at[c], refs[0].at[(1 - c) if arriving else c], 1)]
    return plan, 1


def _plan_scatter_sibling(n):
    def plan(refs, x, y, c, arriving):
        out = []
        for a in range(n):
            for q in range(4):
                out.append((refs[a].at[2 * q + (1 - c)], refs[n + a].at[q], 1))
        return out
    return plan, n * 4


def _plan_scatter_chips(n):
    def plan(refs, x, y, c, arriving):
        out = []
        for a in range(n):
            for k in OTHER_CHIPS:
                px, py, _ = _peer(k, x, y, c)
                out.append((refs[a].at[2 * px + py], refs[n + a].at[(2 * px + py) if arriving else (2 * x + y)], k))
        return out
    return plan, n * 3


def _remote(src, dst, send_sems, recv_sems, i, k):
    x, y, c = (lax.axis_index(n) for n in AXES)
    return pltpu.make_async_remote_copy(src_ref=src, dst_ref=dst, send_sem=send_sems.at[i], recv_sem=recv_sems.at[i],
                                        device_id=_peer(k, x, y, c), device_id_type=pl.DeviceIdType.MESH)


def _copies_start(groups, name, after):
    ng = len(groups)
    total = sum(len(bufs) for bufs, _ in groups)

    def body(*refs):
        sems = refs[1 + total:1 + total + 2 * ng]
        x, y, c = (lax.axis_index(n) for n in AXES)
        off = 1
        for gi, (bufs, (plan, _)) in enumerate(groups):
            for i, (src, dst, k) in enumerate(plan(refs[off:off + len(bufs)], x, y, c, False)):
                _remote(src, dst, sems[2 * gi], sems[2 * gi + 1], i, k).start()
            off += len(bufs)
        refs[-1][...] = jnp.zeros(TOKEN_SHAPE, F32)

    sem_shapes = [pltpu.SemaphoreType.DMA((count,)) for _, (_, count) in groups for _ in range(2)]
    flat = [b for bufs, _ in groups for b in bufs]
    outs = pl.pallas_call(
        body, name=name,
        in_specs=[HBM] + [IN_HBM] * total,
        out_specs=[SEM] * (2 * ng) + [IN_HBM] * total + [pl.BlockSpec(memory_space=pltpu.VMEM)],
        out_shape=sem_shapes + [pltpu.HBM(b.shape, b.dtype) for b in flat] + [SDS(TOKEN_SHAPE, F32)],
        input_output_aliases={1 + i: 2 * ng + i for i in range(total)},
        compiler_params=pltpu.CompilerParams(has_side_effects=DATAFLOW),
    )(after, *[pltpu.with_memory_space_constraint(b, pltpu.HBM) for b in flat])
    handles, off = [], 2 * ng
    for gi, (bufs, _) in enumerate(groups):
        handles.append((outs[2 * gi], outs[2 * gi + 1], list(outs[off:off + len(bufs)])))
        off += len(bufs)
    return handles, outs[-1]


def _copies_wait_start(handle, plan, pass_on, more, name, after):
    send_sems, recv_sems, bufs = handle
    n = len(bufs)
    idx, (pass_plan, pass_count) = pass_on
    total = sum(len(b) for b, _ in more)
    ng = 1 + len(more)

    def body(*refs):
        x, y, c = (lax.axis_index(a) for a in AXES)
        waited = refs[1:1 + n]
        outs = refs[3 + n + total:]
        new_sems = outs[n + total:n + total + 2 * ng]
        for i, (src, dst, k) in enumerate(plan[0](waited, x, y, c, True)):
            copy = _remote(src, dst, refs[1 + n + total], refs[2 + n + total], i, k)
            copy.wait_send()
            copy.wait_recv()
        for i, (src, dst, k) in enumerate(pass_plan([waited[j] for j in idx], x, y, c, False)):
            _remote(src, dst, new_sems[0], new_sems[1], i, k).start()
        off = 1 + n
        for gi, (b, (p, _)) in enumerate(more):
            for i, (src, dst, k) in enumerate(p(refs[off:off + len(b)], x, y, c, False)):
                _remote(src, dst, new_sems[2 + 2 * gi], new_sems[3 + 2 * gi], i, k).start()
            off += len(b)
        outs[-1][...] = jnp.zeros(TOKEN_SHAPE, F32)

    flat = list(bufs) + [a for b, _ in more for a in b]
    sem_shapes = [pltpu.SemaphoreType.DMA((count,)) for count in [pass_count] + [cnt for _, (_, cnt) in more] for _ in range(2)]
    outs = pl.pallas_call(
        body, name=name,
        in_specs=[HBM] + [IN_HBM] * (n + total) + [SEM, SEM],
        out_specs=[IN_HBM] * (n + total) + [SEM] * (2 * ng) + [pl.BlockSpec(memory_space=pltpu.VMEM)],
        out_shape=[pltpu.HBM(b.shape, b.dtype) for b in flat] + sem_shapes + [SDS(TOKEN_SHAPE, F32)],
        input_output_aliases={1 + i: i for i in range(n + total)},
        compiler_params=pltpu.CompilerParams(has_side_effects=DATAFLOW),
    )(after, *[pltpu.with_memory_space_constraint(b, pltpu.HBM) for b in flat], send_sems, recv_sems)
    thru = list(outs[:n])
    sems_out = outs[n + total:n + total + 2 * ng]
    handles = [(sems_out[0], sems_out[1], [thru[j] for j in idx])]
    off = n
    for gi, (b, _) in enumerate(more):
        handles.append((sems_out[2 + 2 * gi], sems_out[3 + 2 * gi], list(outs[off:off + len(b)])))
        off += len(b)
    return thru, handles, outs[-1]


def _copies_wait(handle, plan, name, after):
    send_sems, recv_sems, bufs = handle
    n = len(bufs)

    def body(*refs):
        x, y, c = (lax.axis_index(a) for a in AXES)
        for i, (src, dst, k) in enumerate(plan[0](refs[:n], x, y, c, True)):
            copy = _remote(src, dst, refs[n], refs[n + 1], i, k)
            copy.wait_send()
            copy.wait_recv()

    return pl.pallas_call(
        body, name=name,
        in_specs=[IN_HBM] * n + [SEM, SEM, HBM], out_specs=[IN_HBM] * n,
        out_shape=[pltpu.HBM(b.shape, b.dtype) for b in bufs],
        input_output_aliases={i: i for i in range(n)},
        compiler_params=pltpu.CompilerParams(has_side_effects=DATAFLOW),
    )(*bufs, send_sems, recv_sems, after)


def _pair_sums(mine, theirs, c, chip, name):
    n = len(mine)

    def body(where_ref, *refs):
        q = pl.program_id(0)
        for a in range(n):
            total = (refs[a][...].astype(F32) + refs[n + a][...].astype(F32)).astype(BF16)
            refs[2 * n + a][...] = total

            @pl.when(q == where_ref[1])
            def _():
                refs[3 * n + a][...] = total

    block = lambda t: (None,) + t.shape[1:]
    zeros = lambda t: (0,) * (t.ndim - 1)
    outs = pl.pallas_call(
        body, name=name,
        grid_spec=pltpu.PrefetchScalarGridSpec(
            num_scalar_prefetch=1, grid=(4,),
            in_specs=[pl.BlockSpec(block(t), lambda q, w, z=zeros(t): (2 * q + w[0],) + z) for t in theirs]
            + [pl.BlockSpec(block(t), lambda q, w, z=zeros(t): (q,) + z) for t in theirs],
            out_specs=[pl.BlockSpec(block(t), lambda q, w, z=zeros(t): (q,) + z) for t in theirs]
            + [pl.BlockSpec(block(t), lambda q, w, z=zeros(t): (w[1],) + z) for t in theirs]),
        out_shape=[SDS(t.shape, BF16) for t in theirs] * 2,
        compiler_params=_params(1))(jnp.stack([c, chip]).astype(jnp.int32), *mine, *theirs)
    return list(outs[:n]), list(outs[n:])


def _local_step(x, mem, target, gains, get, put, flush, share, tm_huge=2048, tm_big=1024, tm_mid=512, tm_small=256):
    g_mix, pscale, g_mem, g_ffn, g_fin = gains
    T = x.shape[0]
    tm_huge, tm_big, tm_mid, tm_small = min(tm_huge, T), min(tm_big, T), min(tm_mid, T), min(tm_small, T)
    tn = DFF // 2

    w_pair, w_ids, p_ids = get("in_pair", x)
    proj, h = _fwd_proj(x, g_mix, w_pair, w_ids, p_ids, tm_huge)
    w_near, w_ids, p_ids = get("in_near", h)
    proj = _fwd_proj_more(h, w_near, proj, w_ids, p_ids, tm_huge, "fwd_proj_near")
    w_far, w_ids, p_ids = get("in_far", proj)
    proj = _fwd_proj_more(h, w_far, proj, w_ids, p_ids, tm_huge, "fwd_proj_far")
    w_in = get("in_whole", (w_pair, w_near, w_far))
    cw0, cw1, cw2, w_co, w_pool, w_kv = get("mix", proj)
    za, conv, pooled, ya, yp, kv, memn = _fwd_mix(proj, cw0, cw1, cw2, w_co, w_pool, mem, g_mem, w_kv, tm_mid)
    w_xo, w_o = get("merge", ya)
    o, yx, merged, x1, h2 = _fwd_merge(proj, ya, yp, x, kv, w_xo, w_o, pscale, g_ffn, tm_mid)
    wg_t, wu_t = get("gate_up", x1)
    get("down", x1, early=True)
    gate, up, act = _fwd_ffn_up(h2, wg_t, wu_t, tm_mid, tn)
    (w_d,) = get("down", gate)
    dx2, loss, dg_fin = _fwd_ffn_down_loss(act, w_d, x1, target, g_fin, tm_mid)

    dgate, dup = _bwd_ffn_down(dx2, w_d, gate, up, tm_mid, tn)
    dx1, dg_ffn = _bwd_ffn_up(dgate, dup, wg_t, wu_t, x1, dx2, g_ffn, tm_small)
    dw_d = _wgrad_dense(act, dx2, "wgrad_down", tm_huge, g_mix, a_cols=tn)
    dwg_t = _wgrad_dense(dgate, h2, "wgrad_gate", tm_huge, g_mix, a_cols=tn)
    dwu_t = _wgrad_dense(dup, h2, "wgrad_up", tm_huge, g_mix, a_cols=tn)
    token = put("ffn", (dwg_t, dwu_t, dw_d))

    dproj, dya, dyx, dza, do, dpooled, dpscale, dw_pool = _bwd_merge(
        dx1, proj, ya, yp, yx, pooled, pscale + token[0:1, 0:1], w_o, w_co, w_xo, w_pool, tm_mid)
    token = flush(dya)
    dw_o = _wgrad_dense(merged, dx1, "wgrad_out", tm_big, token)
    dw_co = _wgrad_dense(za, dya, "wgrad_conv_out", tm_big, token)
    dw_xo = _wgrad_dense(o, dyx, "wgrad_xattn_out", tm_big, token)
    dproj, dw_kv, dg_mem = _bwd_attn(dproj, proj, do, kv, memn, w_kv, mem, g_mem, tm_big)
    token = put("mix", (dw_co, dw_xo, dw_o, dw_pool, dw_kv))

    dproj, dcw = _bwd_mix(dproj, proj, conv, dza, dpooled, cw0 + token[0:1, 0:1], cw1, cw2, tm_mid)
    token = flush(dcw)
    dw_in = _wgrad(h, dproj, name="wgrad_in", groups=NSPLIT, a_cols=D, b_cols=D, tt=tm_huge,
                   a_index=lambda g, k, t: (t, 0), b_index=lambda g, k, t: (g, t, 0),
                   o_index=lambda g, k, t: (_slot_group(g), 0, 0), out_shape=(NSPLIT, D, D), after=token)
    token = flush(put("in", (dw_in,)))
    grad_x, dg_mix = _bwd_proj(dproj, w_in, x, dx1, g_mix + token[0:1, 0:1], tm_big)

    small = share(jnp.concatenate([dg_mix, dpscale, dg_mem, dg_ffn, dg_fin, dcw[0:3], loss], axis=0))
    return grad_x, small


def kernel(x, mem, norm_mix, w_in, conv_w, w_conv_out, w_pool, pool_scale, norm_mem, w_kv, w_xattn_out, w_out, norm_ffn, w_gate, w_up, w_down, norm_final, loss_target, m_norm_mix, m_w_in, m_conv_w, m_w_conv_out, m_w_pool, m_pool_scale, m_norm_mem, m_w_kv, m_w_xattn_out, m_w_out, m_norm_ffn, m_w_gate, m_w_up, m_w_down, m_norm_final, v_norm_mix, v_w_in, v_conv_w, v_w_conv_out, v_w_pool, v_pool_scale, v_norm_mem, v_w_kv, v_w_xattn_out, v_w_out, v_norm_ffn, v_w_gate, v_w_up, v_w_down, v_norm_final):
    T = x.shape[1]
    rows = D // NDEV
    ffb = DFF // NDEV
    prow = HD // NDEV
    me = 4 * lax.axis_index("x") + 2 * lax.axis_index("y") + lax.axis_index("c")

    shards = [w_in[0].astype(BF16), w_conv_out[0].astype(BF16), w_xattn_out[0].astype(BF16), w_out[0].astype(BF16),
              w_pool[0].astype(BF16).reshape(NPOOL * prow, HD), w_kv[0].astype(BF16),
              w_gate[0].T.astype(BF16), w_up[0].T.astype(BF16), w_down[0].astype(BF16),
              jnp.pad(conv_w[0], ((0, 5), (0, 0)))]

    cx, cy, cc = (lax.axis_index(n) for n in AXES)
    chip = 2 * cx + cy

    def land(own, index, slots):
        return lax.dynamic_update_index_in_dim(lax.empty((slots,) + own.shape, own.dtype), own, index, 0)

    needed = ["in_pair", "in_near", "in_far", "mix", "merge", "gate_up", "down"]
    members = {"mix": [9, 1, 4, 5], "gate_up": [6, 7], "merge": [2, 3], "down": [8]}
    near = (2, 4)
    plans = {"in_pair": _plan_pair(), "in_near": _plan_gather_chips(1, near), "in_far": _plan_far_chip()}
    plans.update({n: _plan_gather_chips(len(members[n])) for n in members})
    g_bufs = {"in_pair": [shards[0], land(shards[0], cc, 2)],
              "in_near": [w_in[0].astype(BF16), lax.empty((NDEV, D, D), BF16)],
              "in_far": [w_in[0].astype(BF16), lax.empty((2, D, D), BF16)]}
    g_bufs.update({n: [shards[i] for i in members[n]] + [land(shards[i], me, NDEV) for i in members[n]] for n in members})
    first_handles, _ = _copies_start([(g_bufs[n], plans[n]) for n in needed[:2]], "gather_start", x)
    g_handles = dict(zip(needed[:2], first_handles))
    pair_ids = jnp.array([0, 1], jnp.int32)

    on_last_leg = {}

    def get(group, after, early=False):
        if group == "in_whole":
            w_pair, w_near, w_far = after
            w_whole = lax.dynamic_update_slice_in_dim(w_near, w_pair, 2 * chip, 0)
            return lax.dynamic_update_slice_in_dim(w_whole, w_far, 2 * (3 - chip), 0)
        if group == "in_pair":
            bufs = _copies_wait(g_handles[group], plans[group], "gather_wait_" + group, after)
            return bufs[1], pair_ids, (2 * chip + pair_ids).astype(jnp.int32)
        if group not in on_last_leg:
            n_bufs = len(g_bufs[group])
            landed = list(range(n_bufs // 2, n_bufs))
            if group == "in_near":
                plan, more = _plan_gather_sibling(1, near), [(g_bufs[n], plans[n]) for n in needed[2:]]
            elif group == "in_far":
                plan, more = _plan_far_sibling(), []
            else:
                plan, more = _plan_gather_sibling(n_bufs // 2), []
            _, handles, token = _copies_wait_start(g_handles[group], plans[group], (landed, plan), more,
                                                   "gather_pass_" + group, after)
            g_handles.update(zip(needed[2:], handles[1:]))
            on_last_leg[group] = (handles[0], plan, token)
        if early:
            return None
        handle, plan, token = on_last_leg[group]
        got = _copies_wait(handle, plan, "gather_passed_" + group, after if group in ("gate_up", "down") else token)
        if group == "in_near":
            groups = jnp.stack([me ^ k for k in (2, 3, 4, 5)]).astype(jnp.int32)
            return got[0], groups, groups
        if group == "in_far":
            return got[0], pair_ids, (2 * (3 - chip) + pair_ids).astype(jnp.int32)
        if group == "mix":
            cw_g, w_co_g, w_pool_g, w_kv_g = got
            cw_full = cw_g.transpose(1, 0, 2).reshape(8, D)
            w_pool_full = w_pool_g.reshape(NDEV, NPOOL, prow, HD).transpose(1, 0, 2, 3).reshape(NPOOL, HD, HD)
            return cw_full[0:1], cw_full[1:2], cw_full[2:3], w_co_g.reshape(D, D), w_pool_full, w_kv_g
        if group == "merge":
            return got[0].reshape(D, D), got[1].reshape(D, D)
        return [g.reshape(DFF, D) for g in got]

    started = {}

    def put(group, grads):
        if group == "ffn":
            sends = [g.reshape(NDEV, ffb, D) for g in grads]
        elif group == "mix":
            dw_co, dw_xo, dw_o, dw_pool, dw_kv = grads
            sends = [dw_co.reshape(NDEV, rows, D), dw_xo.reshape(NDEV, rows, D), dw_o.reshape(NDEV, rows, D),
                     dw_pool.reshape(NPOOL, NDEV, prow, HD).transpose(1, 0, 2, 3).reshape(NDEV, NPOOL * prow, HD), dw_kv]
        else:
            sends = list(grads)
        n = len(sends)
        halves = [lax.empty((4,) + s.shape[1:], s.dtype) for s in sends]
        (handle,), token = _copies_start([(sends + halves, _plan_scatter_sibling(n))], "scatter_swap_" + group, norm_mix)
        swapping.append((group, handle, n))
        return token

    swapping = []

    def flush(after):
        group, handle, n = swapping.pop()
        bufs = _copies_wait(handle, _plan_scatter_sibling(n), "scatter_swapped_" + group, after)
        sums, lands = _pair_sums(bufs[:n], bufs[n:], cc, chip, "pair_sums_" + group)
        (handle,), token = _copies_start([(sums + lands, _plan_scatter_chips(n))], "scatter_start_" + group, norm_mix)
        started[group] = (handle, _plan_scatter_chips(n))
        return token

    def take(group, after):
        handle, plan = started[group]
        return _copies_wait(handle, plan, "scatter_wait_" + group, after)[len(handle[2]) // 2:]

    gains = (norm_mix, pool_scale, norm_mem, norm_ffn, norm_final.reshape(1, D))
    grad_x, small = _local_step(x[0], mem[0], loss_target[0], gains, get, put, flush, lambda rows: rows)

    everyone = _plan_gather_chips(1, tuple(range(1, NDEV)))
    (small_handle,), token = _copies_start([([small, land(small, me, NDEV)], everyone)], "small_start", norm_mix)

    res = {}

    def update(group, names, ws, gs, ms, vs, from_parts, steps, transposed=()):
        view = lambda a, name: a[0].T if name in transposed else a
        flat = [[view(a, name).reshape(g.shape[-2:]) for a in (w, m, v)] for name, w, g, m, v in zip(names, ws, gs, ms, vs)]
        outs = _adamw([f[0] for f in flat], gs, [f[1] for f in flat], [f[2] for f in flat], "adamw_" + group,
                      from_parts, steps)
        for name, w, four in zip(names, ws, outs):
            res[name] = [(o.T if name in transposed else o).reshape(w.shape) for o in four]

    p_g, p_u, p_d = take("ffn", token)
    update("ffn", ["w_gate", "w_up", "w_down"], [w_gate, w_up, w_down], [p_g, p_u, p_d],
           [m_w_gate, m_w_up, m_w_down], [v_w_gate, v_w_up, v_w_down], True, 2, transposed=("w_gate", "w_up"))

    small_all = _copies_wait(small_handle, everyone, "small_wait", res["w_down"][1])[1]
    small_sum = _sum_parts(small_all, "sum_small")
    loss = small_sum[8, 0]
    g_cw = lax.dynamic_slice_in_dim(small_sum[5:8], me * rows, rows, axis=1)
    update("replicated", ["norm_mix", "pool_scale", "norm_mem", "norm_ffn", "norm_final", "conv_w"],
           [norm_mix, pool_scale, norm_mem, norm_ffn, norm_final, conv_w], [small_sum[k:k + 1] for k in range(5)] + [g_cw],
           [m_norm_mix, m_pool_scale, m_norm_mem, m_norm_ffn, m_norm_final, m_conv_w],
           [v_norm_mix, v_pool_scale, v_norm_mem, v_norm_ffn, v_norm_final, v_conv_w], False, 1)

    p_co, p_xo, p_o, p_pool, p_kv = take("mix", res["conv_w"][1])
    update("mix", ["w_conv_out", "w_xattn_out", "w_out", "w_pool", "w_kv"], [w_conv_out, w_xattn_out, w_out, w_pool, w_kv],
           [p_co, p_xo, p_o, p_pool, p_kv], [m_w_conv_out, m_w_xattn_out, m_w_out, m_w_pool, m_w_kv],
           [v_w_conv_out, v_w_xattn_out, v_w_out, v_w_pool, v_w_kv], True, 2)
    (p_in,) = take("in", res["w_out"][1])
    update("in", ["w_in"], [w_in], [p_in], [m_w_in], [v_w_in], True, 4)
    order = ["norm_mix", "w_in", "conv_w", "w_conv_out", "w_pool", "pool_scale", "norm_mem", "w_kv", "w_xattn_out", "w_out",
             "norm_ffn", "w_gate", "w_up", "w_down", "norm_final"]
    return (loss, grad_x[None], *[res[n][0] for n in order], *[res[n][1] for n in order],
            *[res[n][2] for n in order], *[res[n][3] for n in order])
```

```python
import jax
import jax.numpy as jnp
from jax import lax
from jax.experimental import pallas as pl
from jax.experimental.pallas import tpu as pltpu

F32 = jnp.float32
BF16 = jnp.bfloat16
SDS = jax.ShapeDtypeStruct

AXES = ("x", "y", "c")
NDEV = 8
D = 1024
NSPLIT = 8
NH = 4
HD = D // NH
NPOOL = 4
DFF = 2816
EPS = 1e-6
ATT_SCALE = HD ** -0.5
HALO = 16


def _slot_group(s):
    return jnp.where(s < 3, s + 5, jnp.where(s == 3, 4, s - 4))


ADAM_LR = 0.001
ADAM_B1 = 0.9
ADAM_B2 = 0.999
ADAM_EPS = 1e-08
ADAM_WD = 0.01
ADAM_STEP = 10

V7X_VMEM_BYTES = 64 * 1024 * 1024
VMEM_LIMIT = V7X_VMEM_BYTES - 8 * 1024 * 1024
HBM = pl.BlockSpec(memory_space=pl.ANY)


def _whole(shape):
    return pl.BlockSpec(shape, lambda *_: (0,) * len(shape), pipeline_mode=pl.Buffered(1))


def _params(n_grid):
    return pltpu.CompilerParams(dimension_semantics=("arbitrary",) * n_grid, vmem_limit_bytes=VMEM_LIMIT)


def _mm(a, b):
    return jnp.dot(a, b, preferred_element_type=F32)


def _mm_nt(a, b):
    return lax.dot_general(a, b, (((1,), (1,)), ((), ())), preferred_element_type=F32)


def _mm_tn(a, b):
    return lax.dot_general(a, b, (((0,), (0,)), ((), ())), preferred_element_type=F32)


def _sigmoid(x):
    return 1.0 / (1.0 + jnp.exp(-x))


def _rms(x):
    return lax.rsqrt(jnp.mean(x * x, axis=-1, keepdims=True) + EPS)


def _norm_bwd(dh, x, gain):
    r = _rms(x)
    xh = x * r
    dxh = dh * gain
    dx = r * (dxh - xh * jnp.mean(dxh * xh, axis=-1, keepdims=True))
    return dx, jnp.sum(dh * xh, axis=0, keepdims=True)


def _col_chunks(n, width=512):
    return [slice(c, min(c + width, n)) for c in range(0, n, width)]


def _shift_down(v, k):
    return pltpu.roll(v, k, 0)


def _shift_up(v, k):
    return pltpu.roll(v, v.shape[0] - k, 0)


def _fwd_proj(x, gain, w_blocks, w_ids, p_ids, tm):
    T = x.shape[0]

    def body(w_ids_ref, p_ids_ref, x_ref, g_ref, w_ref, proj_ref, h_ref):
        del w_ids_ref, p_ids_ref

        @pl.when(pl.program_id(1) == 0)
        def _():
            xf = x_ref[...]
            h_ref[...] = (xf * _rms(xf) * g_ref[...]).astype(BF16)
        proj_ref[...] = _mm(h_ref[...], w_ref[...]).astype(BF16)

    return pl.pallas_call(
        body, name="fwd_proj",
        grid_spec=pltpu.PrefetchScalarGridSpec(
            num_scalar_prefetch=2, grid=(T // tm, w_ids.shape[0]),
            in_specs=[pl.BlockSpec((tm, D), lambda i, j, w, p: (i, 0)), pl.BlockSpec((1, D), lambda i, j, w, p: (0, 0)),
                      pl.BlockSpec((None, D, D), lambda i, j, w, p: (w[j], 0, 0))],
            out_specs=[pl.BlockSpec((None, tm, D), lambda i, j, w, p: (p[j], i, 0)),
                       pl.BlockSpec((tm, D), lambda i, j, w, p: (i, 0))]),
        out_shape=[SDS((NSPLIT, T, D), BF16), SDS((T, D), BF16)],
        compiler_params=_params(2))(w_ids, p_ids, x, gain, w_blocks)


def _fwd_proj_more(h, w_blocks, proj, w_ids, p_ids, tm, name):
    T = h.shape[0]

    def body(w_ids_ref, p_ids_ref, h_ref, w_ref, proj_hbm, proj_ref):
        del w_ids_ref, p_ids_ref, proj_hbm
        proj_ref[...] = _mm(h_ref[...], w_ref[...]).astype(BF16)

    return pl.pallas_call(
        body, name=name,
        grid_spec=pltpu.PrefetchScalarGridSpec(
            num_scalar_prefetch=2, grid=(T // tm, w_ids.shape[0]),
            in_specs=[pl.BlockSpec((tm, D), lambda i, j, w, p: (i, 0)),
                      pl.BlockSpec((None, D, D), lambda i, j, w, p: (w[j], 0, 0)), HBM],
            out_specs=pl.BlockSpec((None, tm, D), lambda i, j, w, p: (p[j], i, 0))),
        out_shape=SDS(proj.shape, BF16), input_output_aliases={4: 0},
        compiler_params=_params(2))(w_ids, p_ids, h, w_blocks, proj)


def _halo_before(split, tm):
    return pl.BlockSpec((None, HALO, D), lambda i: (split, jnp.maximum(i * (tm // HALO) - 1, 0), 0))


def _fwd_mix(proj, cw0, cw1, cw2, w_co, w_pool, mem, gain_mem, w_kv, tm):
    T = proj.shape[1]
    M = mem.shape[0]

    def body(b_ref, c_ref, ua_ref, up_ref, ch_ref, uah_ref, uph_ref, cw0_ref, cw1_ref, cw2_ref, wco_ref, wp_ref,
             mem_ref, gm_ref, wkv_ref, za_ref, conv_ref, pooled_ref, ya_ref, yp_ref, kv_ref, memn_ref):
        i = pl.program_id(0)

        @pl.when(i == 0)
        def _():
            m = mem_ref[...]
            memn = (m * _rms(m) * gm_ref[...]).astype(BF16)
            memn_ref[...] = memn
            for j in range(2 * NH):
                kv_ref[j] = _mm(memn, wkv_ref[j]).astype(BF16)
        keep = jnp.where(i > 0, 1.0, 0.0).astype(F32)
        cu = c_ref[...].astype(F32) * ua_ref[...].astype(F32)
        cu_h = ch_ref[...].astype(F32) * uah_ref[...].astype(F32) * keep
        ext = jnp.concatenate([cu_h, cu], axis=0)
        conv = (cw2_ref[...] * ext + cw1_ref[...] * _shift_down(ext, 1) + cw0_ref[...] * _shift_down(ext, 2))[HALO:]
        za = (b_ref[...].astype(F32) * conv).astype(BF16)
        conv_ref[...] = conv.astype(BF16)
        za_ref[...] = za
        ya_ref[...] = _mm(za, wco_ref[...]).astype(BF16)

        up = up_ref[...].astype(F32)
        ext_u = jnp.concatenate([uph_ref[...].astype(F32) * keep, up], axis=0)
        pos = i * tm + lax.broadcasted_iota(jnp.int32, (tm, HD), 0)
        for g in range(NPOOL):
            cols = slice(g * HD, (g + 1) * HD)
            s = ext_u[:, cols]
            for k in range(g + 1):
                s = s + _shift_down(s, 1 << k)
            cnt = jnp.minimum(pos + 1, 2 << g).astype(F32)
            pooled = (s[HALO:] / cnt - up[:, cols]).astype(BF16)
            pooled_ref[:, cols] = pooled
            yp_ref[:, cols] = _mm(pooled, wp_ref[g]).astype(BF16)

    tile = lambda s: pl.BlockSpec((None, tm, D), lambda i: (s, i, 0))
    row = pl.BlockSpec((1, D), lambda i: (0, 0))
    out = pl.BlockSpec((tm, D), lambda i: (i, 0))
    return pl.pallas_call(
        body, name="fwd_mix", grid=(T // tm,),
        in_specs=[tile(0), tile(1), tile(2), tile(3), _halo_before(1, tm), _halo_before(2, tm), _halo_before(3, tm),
                  row, row, row, _whole((D, D)), _whole((NPOOL, HD, HD)), _whole((M, D)), row, _whole((2 * NH, D, HD))],
        out_specs=[out] * 5 + [pl.BlockSpec((2 * NH, M, HD), lambda i: (0, 0, 0)), pl.BlockSpec((M, D), lambda i: (0, 0))],
        out_shape=[SDS((T, D), BF16)] * 5 + [SDS((2 * NH, M, HD), BF16), SDS((M, D), BF16)],
        compiler_params=_params(1))(proj, proj, proj, proj, proj, proj, proj, cw0, cw1, cw2, w_co, w_pool, mem, gain_mem, w_kv)


def _softmax_rows(s):
    e = jnp.exp(s - jnp.max(s, axis=-1, keepdims=True))
    return e / jnp.sum(e, axis=-1, keepdims=True)


def _fwd_merge(proj, ya, yp, x, kv, w_xo, w_o, pscale, gain_ffn, tm):
    T = x.shape[0]

    def body(q_ref, ga_ref, gp_ref, gx_ref, ya_ref, yp_ref, x_ref, kv_ref, wxo_ref, wo_ref, ps_ref, gf_ref,
             o_ref, yx_ref, merged_ref, x1_ref, h2_ref):
        for h in range(NH):
            cols = slice(h * HD, (h + 1) * HD)
            p = _softmax_rows(_mm_nt(q_ref[:, cols], kv_ref[h]) * ATT_SCALE)
            o_ref[:, cols] = _mm(p.astype(BF16), kv_ref[NH + h]).astype(BF16)
        yx = _mm(o_ref[...], wxo_ref[...])
        yx_ref[...] = yx.astype(BF16)
        merged = (_sigmoid(ga_ref[...].astype(F32)) * ya_ref[...].astype(F32)
                  + _sigmoid(gp_ref[...].astype(F32)) * (yp_ref[...].astype(F32) * ps_ref[...])
                  + _sigmoid(gx_ref[...].astype(F32)) * yx).astype(BF16)
        merged_ref[...] = merged
        x1 = x_ref[...] + _mm(merged, wo_ref[...])
        x1_ref[...] = x1
        h2_ref[...] = (x1 * _rms(x1) * gf_ref[...]).astype(BF16)

    tile = lambda s: pl.BlockSpec((None, tm, D), lambda i: (s, i, 0))
    row = pl.BlockSpec((1, D), lambda i: (0, 0))
    act = pl.BlockSpec((tm, D), lambda i: (i, 0))
    full = _whole((D, D))
    return pl.pallas_call(
        body, name="fwd_merge", grid=(T // tm,),
        in_specs=[tile(4), tile(5), tile(6), tile(7), act, act, act,
                  _whole((2 * NH, kv.shape[1], HD)), full, full, row, row],
        out_specs=[act] * 5,
        out_shape=[SDS((T, D), BF16), SDS((T, D), BF16), SDS((T, D), BF16), SDS((T, D), F32), SDS((T, D), BF16)],
        compiler_params=_params(1))(proj, proj, proj, proj, ya, yp, x, kv, w_xo, w_o, pscale, gain_ffn)


def _fwd_ffn_up(h2, wg_t, wu_t, tm, tn):
    T = h2.shape[0]

    def body(h_ref, wg_ref, wu_ref, gate_ref, up_ref, act_ref):
        for cols in _col_chunks(tn):
            gate = _mm_nt(h_ref[...], wg_ref[cols, :])
            up = _mm_nt(h_ref[...], wu_ref[cols, :])
            gate_ref[:, cols] = gate.astype(BF16)
            up_ref[:, cols] = up.astype(BF16)
            act_ref[:, cols] = (gate * _sigmoid(gate) * up).astype(BF16)

    w = pl.BlockSpec((tn, D), lambda n, i: (n, 0))
    o = pl.BlockSpec((tm, tn), lambda n, i: (i, n))
    return pl.pallas_call(
        body, name="fwd_ffn_up", grid=(DFF // tn, T // tm),
        in_specs=[pl.BlockSpec((tm, D), lambda n, i: (i, 0)), w, w],
        out_specs=[o] * 3, out_shape=[SDS((T, DFF), BF16)] * 3,
        compiler_params=_params(2))(h2, wg_t, wu_t)


def _fwd_ffn_down_loss(act, w_d, x1, target, gain_final, tm):
    T = x1.shape[0]

    def body(act_ref, wd_ref, x1_ref, tgt_ref, g_ref, dx2_ref, loss_ref, dgain_ref):
        @pl.when(pl.program_id(0) == 0)
        def _():
            loss_ref[...] = jnp.zeros_like(loss_ref)
            dgain_ref[...] = jnp.zeros_like(dgain_ref)
        x2 = x1_ref[...] + _mm(act_ref[...], wd_ref[...])
        gain = g_ref[...]
        y = x2 * _rms(x2) * gain
        err = y - tgt_ref[...]
        loss_ref[...] += 0.5 * jnp.sum(jnp.mean(err * err, axis=-1, keepdims=True))
        dx2, dgain = _norm_bwd(err * (1.0 / D), x2, gain)
        dx2_ref[...] = dx2
        dgain_ref[...] += dgain

    act_spec = pl.BlockSpec((tm, D), lambda i: (i, 0))
    row = pl.BlockSpec((1, D), lambda i: (0, 0))
    return pl.pallas_call(
        body, name="fwd_ffn_down_loss", grid=(T // tm,),
        in_specs=[pl.BlockSpec((tm, DFF), lambda i: (i, 0)), _whole((DFF, D)), act_spec, act_spec, row],
        out_specs=[act_spec, pl.BlockSpec((8, D), lambda i: (0, 0)), row],
        out_shape=[SDS((T, D), F32), SDS((8, D), F32), SDS((1, D), F32)],
        compiler_params=_params(1))(act, w_d, x1, target, gain_final)


def _bwd_ffn_down(dx2, w_d, gate, up, tm, tn):
    T = dx2.shape[0]

    def body(dx_ref, wd_ref, gate_ref, up_ref, dgate_ref, dup_ref):
        dx = dx_ref[...].astype(BF16)
        for cols in _col_chunks(tn):
            dact = _mm_nt(dx, wd_ref[cols, :])
            gate = gate_ref[:, cols].astype(F32)
            sg = _sigmoid(gate)
            dgate_ref[:, cols] = (dact * up_ref[:, cols].astype(F32) * (sg * (1.0 + gate * (1.0 - sg)))).astype(BF16)
            dup_ref[:, cols] = (dact * gate * sg).astype(BF16)

    o = pl.BlockSpec((tm, tn), lambda n, i: (i, n))
    return pl.pallas_call(
        body, name="bwd_ffn_down", grid=(DFF // tn, T // tm),
        in_specs=[pl.BlockSpec((tm, D), lambda n, i: (i, 0)), pl.BlockSpec((tn, D), lambda n, i: (n, 0)), o, o],
        out_specs=[o] * 2, out_shape=[SDS((T, DFF), BF16)] * 2,
        compiler_params=_params(2))(dx2, w_d, gate, up)


def _bwd_ffn_up(dgate, dup, wg_t, wu_t, x1, dx2, gain_ffn, tm):
    T = x1.shape[0]

    def body(dg_ref, du_ref, wg_ref, wu_ref, x1_ref, dx2_ref, g_ref, dx1_ref, dgain_ref):
        @pl.when(pl.program_id(0) == 0)
        def _():
            dgain_ref[...] = jnp.zeros_like(dgain_ref)
        dh2 = _mm(dg_ref[...], wg_ref[...]) + _mm(du_ref[...], wu_ref[...])
        dx, dgain = _norm_bwd(dh2, x1_ref[...], g_ref[...])
        dx1_ref[...] = dx2_ref[...] + dx
        dgain_ref[...] += dgain

    wide = pl.BlockSpec((tm, DFF), lambda i: (i, 0))
    w = _whole((DFF, D))
    act = pl.BlockSpec((tm, D), lambda i: (i, 0))
    row = pl.BlockSpec((1, D), lambda i: (0, 0))
    return pl.pallas_call(
        body, name="bwd_ffn_up", grid=(T // tm,),
        in_specs=[wide, wide, w, w, act, act, row], out_specs=[act, row],
        out_shape=[SDS((T, D), F32), SDS((1, D), F32)],
        compiler_params=_params(1))(dgate, dup, wg_t, wu_t, x1, dx2, gain_ffn)


def _wgrad(a, b, *, name, groups, a_cols, b_cols, tt, a_index, b_index, o_index, out_shape, after):
    T = a.shape[0]
    nt = T // tt
    n_a = a.shape[1] // a_cols if groups == 1 else 1

    def body(a_ref, b_ref, after_ref, o_ref, acc_ref):
        del after_ref
        t = pl.program_id(2)

        @pl.when(t == 0)
        def _():
            acc_ref[...] = jnp.zeros_like(acc_ref)
        acc_ref[...] += _mm_tn(a_ref[...].astype(BF16), b_ref[...].astype(BF16))

        @pl.when(t == nt - 1)
        def _():
            o_ref[...] = acc_ref[...].astype(o_ref.dtype)

    return pl.pallas_call(
        body, name=name, grid=(groups, n_a, nt),
        in_specs=[pl.BlockSpec((tt, a_cols), a_index), pl.BlockSpec((None, tt, b_cols), b_index), HBM],
        out_specs=pl.BlockSpec((None, a_cols, b_cols), o_index),
        out_shape=SDS(out_shape, BF16),
        scratch_shapes=[pltpu.VMEM((a_cols, b_cols), F32)],
        compiler_params=_params(3))(a, b, after)


def _wgrad_dense(a, b, name, tt, after, a_cols=None):
    ka, nb = a.shape[1], b.shape[1]
    a_cols = ka if a_cols is None else a_cols
    out = _wgrad(a, b[None], name=name, groups=1, a_cols=a_cols, b_cols=nb, tt=tt,
                 a_index=lambda g, k, t: (t, k), b_index=lambda g, k, t: (0, t, 0),
                 o_index=lambda g, k, t: (k, 0, 0), out_shape=(ka // a_cols, a_cols, nb), after=after)
    return out.reshape(ka, nb)


def _bwd_merge(dx1, proj, ya, yp, yx, pooled, pscale, w_o, w_co, w_xo, w_pool, tm, after):
    T = dx1.shape[0]
    nt = T // tm

    def body(dx1_ref, ga_ref, gp_ref, gx_ref, ya_ref, yp_ref, yx_ref, pooled_ref, ps_ref, wo_ref, wco_ref, wxo_ref, wp_ref,
             after_ref, dgates_ref, dya_ref, dyx_ref, dza_ref, do_ref, dpooled_ref, dps_ref, dwp_ref, acc_ref):
        del after_ref

        @pl.when(pl.program_id(0) == 0)
        def _():
            dps_ref[...] = jnp.zeros_like(dps_ref)
            acc_ref[...] = jnp.zeros_like(acc_ref)
        dmerged = _mm_nt(dx1_ref[...].astype(BF16), wo_ref[...])
        scale = ps_ref[...]
        sa, sp, sx = (_sigmoid(r[...].astype(F32)) for r in (ga_ref, gp_ref, gx_ref))
        ya, yp_pre, yx = (r[...].astype(F32) for r in (ya_ref, yp_ref, yx_ref))
        dgates_ref[0] = (dmerged * ya * sa * (1.0 - sa)).astype(BF16)
        dgates_ref[1] = (dmerged * (yp_pre * scale) * sp * (1.0 - sp)).astype(BF16)
        dgates_ref[2] = (dmerged * yx * sx * (1.0 - sx)).astype(BF16)
        dya = (dmerged * sa).astype(BF16)
        dyx = (dmerged * sx).astype(BF16)
        dyp = dmerged * sp
        dyps = (dyp * scale).astype(BF16)
        dps_ref[...] += jnp.sum(dyp * yp_pre, axis=0, keepdims=True)
        dya_ref[...] = dya
        dyx_ref[...] = dyx
        dza_ref[...] = _mm_nt(dya, wco_ref[...]).astype(BF16)
        do_ref[...] = _mm_nt(dyx, wxo_ref[...]).astype(BF16)
        for g in range(NPOOL):
            cols = slice(g * HD, (g + 1) * HD)
            dpooled_ref[:, cols] = _mm_nt(dyps[:, cols], wp_ref[g]).astype(BF16)
            acc_ref[g] += _mm_tn(pooled_ref[:, cols], dyps[:, cols])

        @pl.when(pl.program_id(0) == nt - 1)
        def _():
            dwp_ref[...] = acc_ref[...].astype(BF16)

    tile = lambda s: pl.BlockSpec((None, tm, D), lambda i: (s, i, 0))
    row = pl.BlockSpec((1, D), lambda i: (0, 0))
    act = pl.BlockSpec((tm, D), lambda i: (i, 0))
    full = _whole((D, D))
    return pl.pallas_call(
        body, name="bwd_merge", grid=(T // tm,),
        in_specs=[act, tile(5), tile(6), tile(7), act, act, act, act, row, full, full, full,
                  _whole((NPOOL, HD, HD)), HBM],
        out_specs=[pl.BlockSpec((3, tm, D), lambda i: (0, i, 0))] + [act] * 5
        + [row, pl.BlockSpec((NPOOL, HD, HD), lambda i: (0, 0, 0))],
        out_shape=[SDS((NSPLIT, T, D), BF16)] + [SDS((T, D), BF16)] * 5 + [SDS((1, D), F32), SDS((NPOOL, HD, HD), BF16)],
        scratch_shapes=[pltpu.VMEM((NPOOL, HD, HD), F32)],
        compiler_params=_params(1))(dx1, proj, proj, proj, ya, yp, yx, pooled, pscale, w_o, w_co, w_xo, w_pool, after)


def _bwd_attn(dproj, proj, do, kv, memn, w_kv, mem, gain_mem, tm):
    T = do.shape[0]
    M = kv.shape[1]
    nt = T // tm

    def body(dproj_hbm, q_ref, do_ref, kv_ref, memn_ref, wkv_ref, mem_ref, gm_ref, dq_ref, dw_ref, dgain_ref, dkv_ref):
        del dproj_hbm

        @pl.when(pl.program_id(0) == 0)
        def _():
            dkv_ref[...] = jnp.zeros_like(dkv_ref)
        for h in range(NH):
            cols = slice(h * HD, (h + 1) * HD)
            q = q_ref[:, cols]
            do_h = do_ref[:, cols]
            p = _softmax_rows(_mm_nt(q, kv_ref[h]) * ATT_SCALE)
            dp = _mm_nt(do_h, kv_ref[NH + h])
            ds = (p * (dp - jnp.sum(dp * p, axis=-1, keepdims=True)) * ATT_SCALE).astype(BF16)
            dq_ref[:, cols] = _mm(ds, kv_ref[h]).astype(BF16)
            dkv_ref[h] += _mm_tn(ds, q)
            dkv_ref[NH + h] += _mm_tn(p.astype(BF16), do_h)

        @pl.when(pl.program_id(0) == nt - 1)
        def _():
            dmemn = jnp.zeros((M, D), F32)
            for j in range(2 * NH):
                dkv_j = dkv_ref[j].astype(BF16)
                dw_ref[j] = _mm_tn(memn_ref[...], dkv_j).astype(BF16)
                dmemn = dmemn + _mm_nt(dkv_j, wkv_ref[j])
            dgain_ref[...] = _norm_bwd(dmemn, mem_ref[...], gm_ref[...])[1]

    row = pl.BlockSpec((1, D), lambda i: (0, 0))
    return pl.pallas_call(
        body, name="bwd_attn", grid=(nt,),
        in_specs=[HBM, pl.BlockSpec((None, tm, D), lambda i: (4, i, 0)), pl.BlockSpec((tm, D), lambda i: (i, 0)),
                  _whole((2 * NH, M, HD)), _whole((M, D)), _whole((2 * NH, D, HD)), _whole((M, D)), row],
        out_specs=[pl.BlockSpec((None, tm, D), lambda i: (3, i, 0)),
                   pl.BlockSpec((2 * NH, D, HD), lambda i: (0, 0, 0)), row],
        out_shape=[SDS(dproj.shape, BF16), SDS((2 * NH, D, HD), BF16), SDS((1, D), F32)],
        scratch_shapes=[pltpu.VMEM((2 * NH, M, HD), F32)],
        input_output_aliases={0: 0},
        compiler_params=_params(1))(dproj, proj, do, kv, memn, w_kv, mem, gain_mem)


def _bwd_mix(dproj, proj, conv, dza, dpooled, cw0, cw1, cw2, tm, after):
    T = dza.shape[0]
    nt = T // tm

    def halo_after(split_or_none):
        idx = lambda i: jnp.minimum((i + 1) * (tm // HALO), T // HALO - 1)
        if split_or_none is None:
            return pl.BlockSpec((HALO, D), lambda i: (idx(i), 0))
        return pl.BlockSpec((None, HALO, D), lambda i: (split_or_none, idx(i), 0))

    def body(dproj_hbm, b_ref, c_ref, ua_ref, conv_ref, dza_ref, dpo_ref, bn_ref, dzan_ref, dpon_ref, ch_ref, uah_ref,
             cw0_ref, cw1_ref, cw2_ref, after_ref, dabcu_ref, dcw_ref):
        del dproj_hbm, after_ref
        i = pl.program_id(0)

        @pl.when(i == 0)
        def _():
            dcw_ref[...] = jnp.zeros_like(dcw_ref)
        keep_prev = jnp.where(i > 0, 1.0, 0.0).astype(F32)
        keep_next = jnp.where(i < nt - 1, 1.0, 0.0).astype(F32)
        dza = dza_ref[...].astype(F32)
        c = c_ref[...].astype(F32)
        ua = ua_ref[...].astype(F32)
        dconv = dza * b_ref[...].astype(F32)
        dconv_n = dzan_ref[...].astype(F32) * bn_ref[...].astype(F32) * keep_next
        ext = jnp.concatenate([dconv, dconv_n], axis=0)
        dcu = (cw2_ref[...] * ext + cw1_ref[...] * _shift_up(ext, 1) + cw0_ref[...] * _shift_up(ext, 2))[:tm]
        dabcu_ref[0] = (dza * conv_ref[...].astype(F32)).astype(BF16)
        dabcu_ref[1] = (dcu * ua).astype(BF16)
        dabcu_ref[2] = (dcu * c).astype(BF16)

        cu = c * ua
        ext_cu = jnp.concatenate([ch_ref[...].astype(F32) * uah_ref[...].astype(F32) * keep_prev, cu], axis=0)
        dcw_ref[2:3, :] += jnp.sum(dconv * cu, axis=0, keepdims=True)
        dcw_ref[1:2, :] += jnp.sum(dconv * _shift_down(ext_cu, 1)[HALO:], axis=0, keepdims=True)
        dcw_ref[0:1, :] += jnp.sum(dconv * _shift_down(ext_cu, 2)[HALO:], axis=0, keepdims=True)

        dpo = dpo_ref[...].astype(F32)
        ext_dpo = jnp.concatenate([dpo, dpon_ref[...].astype(F32) * keep_next], axis=0)
        pos = i * tm + lax.broadcasted_iota(jnp.int32, (tm + HALO, HD), 0)
        for g in range(NPOOL):
            cols = slice(g * HD, (g + 1) * HD)
            s = ext_dpo[:, cols] / jnp.minimum(pos + 1, 2 << g).astype(F32)
            for k in range(g + 1):
                s = s + _shift_up(s, 1 << k)
            dabcu_ref[3, :, cols] = (s[:tm] - dpo[:, cols]).astype(BF16)

    tile = lambda s: pl.BlockSpec((None, tm, D), lambda i: (s, i, 0))
    act = pl.BlockSpec((tm, D), lambda i: (i, 0))
    row = pl.BlockSpec((1, D), lambda i: (0, 0))
    return pl.pallas_call(
        body, name="bwd_mix", grid=(nt,),
        in_specs=[HBM, tile(0), tile(1), tile(2), act, act, act, halo_after(0), halo_after(None), halo_after(None),
                  _halo_before(1, tm), _halo_before(2, tm), row, row, row, HBM],
        out_specs=[pl.BlockSpec((4, tm, D), lambda i: (1, i, 0)), pl.BlockSpec((8, D), lambda i: (0, 0))],
        out_shape=[SDS(dproj.shape, BF16), SDS((8, D), F32)],
        input_output_aliases={0: 0},
        compiler_params=_params(1))(dproj, proj, proj, proj, conv, dza, dpooled, proj, dza, dpooled, proj, proj, cw0, cw1, cw2, after)


def _bwd_proj(dproj, w_in_g, x, dx1, gain, tm, after):
    T = x.shape[0]

    def body(dp_ref, w_ref, x_ref, dx1_ref, g_ref, after_ref, dx_ref, dgain_ref, acc_ref):
        del after_ref
        i, s = pl.program_id(0), pl.program_id(1)

        @pl.when((i == 0) & (s == 0))
        def _():
            dgain_ref[...] = jnp.zeros_like(dgain_ref)

        @pl.when(s == 0)
        def _():
            acc_ref[...] = jnp.zeros_like(acc_ref)
        acc_ref[...] += _mm_nt(dp_ref[...], w_ref[...])

        @pl.when(s == NSPLIT - 1)
        def _():
            dx, dgain = _norm_bwd(acc_ref[...], x_ref[...], g_ref[...])
            dx_ref[...] = dx1_ref[...] + dx
            dgain_ref[...] += dgain

    act = pl.BlockSpec((tm, D), lambda i, s: (i, 0))
    row = pl.BlockSpec((1, D), lambda i, s: (0, 0))
    return pl.pallas_call(
        body, name="bwd_proj", grid=(T // tm, NSPLIT),
        in_specs=[pl.BlockSpec((None, tm, D), lambda i, s: (s, i, 0)),
                  pl.BlockSpec((None, D, D), lambda i, s: (_slot_group(s), 0, 0)), act, act, row, HBM],
        out_specs=[act, row], out_shape=[SDS((T, D), F32), SDS((1, D), F32)],
        scratch_shapes=[pltpu.VMEM((tm, D), F32)],
        compiler_params=_params(2))(dproj, w_in_g, x, dx1, gain, after)


def _adamw_math(w, g, m, v):
    m = ADAM_B1 * m + (1.0 - ADAM_B1) * g
    v = ADAM_B2 * v + (1.0 - ADAM_B2) * (g * g)
    m_hat = m / (1.0 - ADAM_B1 ** ADAM_STEP)
    v_hat = v / (1.0 - ADAM_B2 ** ADAM_STEP)
    delta = -ADAM_LR * (m_hat / (jnp.sqrt(v_hat) + ADAM_EPS) + ADAM_WD * w)
    return delta, m, v


def _row_tile(rows):
    return 256 if rows % 256 == 0 else rows


def _sum_parts(parts, name):
    n_parts, rows, cols = parts.shape
    tr = _row_tile(rows)

    def body(p_ref, g_ref):
        g = p_ref[0].astype(F32)
        for k in range(1, n_parts):
            g = g + p_ref[k].astype(F32)
        g_ref[...] = g

    blk = pl.BlockSpec((tr, cols), lambda i: (i, 0))
    return pl.pallas_call(
        body, name=name, grid=(rows // tr,),
        in_specs=[pl.BlockSpec((n_parts, tr, cols), lambda i: (0, i, 0))], out_specs=blk,
        out_shape=SDS((rows, cols), F32), compiler_params=_params(1))(parts)


def _adamw(ws, gs, ms, vs, name, from_parts, steps):
    n = len(ws)

    def body(*refs):
        for a in range(n):
            w_ref, g_ref, m_ref, v_ref = refs[4 * a:4 * a + 4]
            go_ref, d_ref, mo_ref, vo_ref = refs[4 * n + 4 * a:4 * n + 4 * a + 4]
            if from_parts:
                g = g_ref[0].astype(F32)
                for k in range(1, g_ref.shape[0]):
                    g = g + g_ref[k].astype(F32)
            else:
                g = g_ref[...]
            go_ref[...] = g
            d_ref[...], mo_ref[...], vo_ref[...] = _adamw_math(w_ref[...], g, m_ref[...], v_ref[...])

    in_specs, out_specs, out_shape, operands = [], [], [], []
    for w, g, m, v in zip(ws, gs, ms, vs):
        rows, cols = w.shape
        blk = pl.BlockSpec((rows // steps, cols), lambda i: (i, 0))
        g_spec = pl.BlockSpec((g.shape[0], rows // steps, cols), lambda i: (0, i, 0)) if from_parts else blk
        in_specs += [blk, g_spec, blk, blk]
        out_specs += [blk] * 4
        out_shape += [SDS((rows, cols), F32)] * 4
        operands += [w, g, m, v]
    outs = pl.pallas_call(body, name=name, grid=(steps,), in_specs=in_specs, out_specs=out_specs, out_shape=out_shape,
                          compiler_params=_params(1))(*operands)
    return [outs[4 * a:4 * a + 4] for a in range(n)]


def _peer(k, x, y, c):
    return ((1 - x) if k & 4 else x, (1 - y) if k & 2 else y, (1 - c) if k & 1 else c)


SEM = pl.BlockSpec(memory_space=pltpu.SEMAPHORE)
IN_HBM = pl.BlockSpec(memory_space=pltpu.HBM)
DATAFLOW = pltpu.SideEffectType.DATAFLOW_SIDE_EFFECTING
TOKEN_SHAPE = (8, 128)


OTHER_CHIPS = (2, 4, 6)


def _place(x, y, c):
    return 4 * x + 2 * y + c


def _plan_gather_chips(n, ks=(1,) + OTHER_CHIPS):
    def plan(refs, x, y, c, arriving):
        out = []
        for a in range(n):
            for k in ks:
                there = _place(*_peer(k, x, y, c))
                out.append((refs[a], refs[n + a].at[there if arriving else _place(x, y, c)], k))
        return out
    return plan, n * len(ks)


def _plan_gather_sibling(n, ks=OTHER_CHIPS):
    def plan(refs, x, y, c, arriving):
        out = []
        for a in range(n):
            for k in ks:
                px, py, pc = _peer(k, x, y, c)
                mine, theirs = _place(px, py, pc), _place(px, py, 1 - pc)
                out.append((refs[a].at[mine], refs[a].at[theirs if arriving else mine], 1))
        return out
    return plan, n * len(ks)


def _plan_pair():
    def plan(refs, x, y, c, arriving):
        return [(refs[0], refs[1].at[(1 - c) if arriving else c], 1)]
    return plan, 1


def _plan_far_chip():
    def plan(refs, x, y, c, arriving):
        return [(refs[0], refs[1].at[c], 6)]
    return plan, 1


def _plan_far_sibling():
    def plan(refs, x, y, c, arriving):
        return [(refs[0].at[c], refs[0].at[(1 - c) if arriving else c], 1)]
    return plan, 1


def _plan_scatter_sibling(n):
    def plan(refs, x, y, c, arriving):
        out = []
        for a in range(n):
            for q in range(4):
                out.append((refs[a].at[2 * q + (1 - c)], refs[n + a].at[q], 1))
        return out
    return plan, n * 4


def _plan_scatter_chips(n):
    def plan(refs, x, y, c, arriving):
        out = []
        for a in range(n):
            for k in OTHER_CHIPS:
                px, py, _ = _peer(k, x, y, c)
                out.append((refs[a].at[2 * px + py], refs[n + a].at[(2 * px + py) if arriving else (2 * x + y)], k))
        return out
    return plan, n * 3


def _remote(src, dst, send_sems, recv_sems, i, k):
    x, y, c = (lax.axis_index(n) for n in AXES)
    return pltpu.make_async_remote_copy(src_ref=src, dst_ref=dst, send_sem=send_sems.at[i], recv_sem=recv_sems.at[i],
                                        device_id=_peer(k, x, y, c), device_id_type=pl.DeviceIdType.MESH)


def _copies_start(groups, name, after):
    ng = len(groups)
    total = sum(len(bufs) for bufs, _ in groups)

    def body(*refs):
        sems = refs[1 + total:1 + total + 2 * ng]
        x, y, c = (lax.axis_index(n) for n in AXES)
        off = 1
        for gi, (bufs, (plan, _)) in enumerate(groups):
            for i, (src, dst, k) in enumerate(plan(refs[off:off + len(bufs)], x, y, c, False)):
                _remote(src, dst, sems[2 * gi], sems[2 * gi + 1], i, k).start()
            off += len(bufs)
        refs[-1][...] = jnp.zeros(TOKEN_SHAPE, F32)

    sem_shapes = [pltpu.SemaphoreType.DMA((count,)) for _, (_, count) in groups for _ in range(2)]
    flat = [b for bufs, _ in groups for b in bufs]
    outs = pl.pallas_call(
        body, name=name,
        in_specs=[HBM] + [IN_HBM] * total,
        out_specs=[SEM] * (2 * ng) + [IN_HBM] * total + [pl.BlockSpec(memory_space=pltpu.VMEM)],
        out_shape=sem_shapes + [pltpu.HBM(b.shape, b.dtype) for b in flat] + [SDS(TOKEN_SHAPE, F32)],
        input_output_aliases={1 + i: 2 * ng + i for i in range(total)},
        compiler_params=pltpu.CompilerParams(has_side_effects=DATAFLOW),
    )(after, *[pltpu.with_memory_space_constraint(b, pltpu.HBM) for b in flat])
    handles, off = [], 2 * ng
    for gi, (bufs, _) in enumerate(groups):
        handles.append((outs[2 * gi], outs[2 * gi + 1], list(outs[off:off + len(bufs)])))
        off += len(bufs)
    return handles, outs[-1]


def _copies_wait_start(handle, plan, pass_on, more, name, after):
    send_sems, recv_sems, bufs = handle
    n = len(bufs)
    idx, (pass_plan, pass_count) = pass_on
    total = sum(len(b) for b, _ in more)
    ng = 1 + len(more)

    def body(*refs):
        x, y, c = (lax.axis_index(a) for a in AXES)
        waited = refs[1:1 + n]
        outs = refs[3 + n + total:]
        new_sems = outs[n + total:n + total + 2 * ng]
        for i, (src, dst, k) in enumerate(plan[0](waited, x, y, c, True)):
            copy = _remote(src, dst, refs[1 + n + total], refs[2 + n + total], i, k)
            copy.wait_send()
            copy.wait_recv()
        for i, (src, dst, k) in enumerate(pass_plan([waited[j] for j in idx], x, y, c, False)):
            _remote(src, dst, new_sems[0], new_sems[1], i, k).start()
        off = 1 + n
        for gi, (b, (p, _)) in enumerate(more):
            for i, (src, dst, k) in enumerate(p(refs[off:off + len(b)], x, y, c, False)):
                _remote(src, dst, new_sems[2 + 2 * gi], new_sems[3 + 2 * gi], i, k).start()
            off += len(b)
        outs[-1][...] = jnp.zeros(TOKEN_SHAPE, F32)

    flat = list(bufs) + [a for b, _ in more for a in b]
    sem_shapes = [pltpu.SemaphoreType.DMA((count,)) for count in [pass_count] + [cnt for _, (_, cnt) in more] for _ in range(2)]
    outs = pl.pallas_call(
        body, name=name,
        in_specs=[HBM] + [IN_HBM] * (n + total) + [SEM, SEM],
        out_specs=[IN_HBM] * (n + total) + [SEM] * (2 * ng) + [pl.BlockSpec(memory_space=pltpu.VMEM)],
        out_shape=[pltpu.HBM(b.shape, b.dtype) for b in flat] + sem_shapes + [SDS(TOKEN_SHAPE, F32)],
        input_output_aliases={1 + i: i for i in range(n + total)},
        compiler_params=pltpu.CompilerParams(has_side_effects=DATAFLOW),
    )(after, *[pltpu.with_memory_space_constraint(b, pltpu.HBM) for b in flat], send_sems, recv_sems)
    thru = list(outs[:n])
    sems_out = outs[n + total:n + total + 2 * ng]
    handles = [(sems_out[0], sems_out[1], [thru[j] for j in idx])]
    off = n
    for gi, (b, _) in enumerate(more):
        handles.append((sems_out[2 + 2 * gi], sems_out[3 + 2 * gi], list(outs[off:off + len(b)])))
        off += len(b)
    return thru, handles, outs[-1]


def _copies_wait(handle, plan, name, after):
    send_sems, recv_sems, bufs = handle
    n = len(bufs)

    def body(*refs):
        x, y, c = (lax.axis_index(a) for a in AXES)
        for i, (src, dst, k) in enumerate(plan[0](refs[:n], x, y, c, True)):
            copy = _remote(src, dst, refs[n], refs[n + 1], i, k)
            copy.wait_send()
            copy.wait_recv()

    return pl.pallas_call(
        body, name=name,
        in_specs=[IN_HBM] * n + [SEM, SEM, HBM], out_specs=[IN_HBM] * n,
        out_shape=[pltpu.HBM(b.shape, b.dtype) for b in bufs],
        input_output_aliases={i: i for i in range(n)},
        compiler_params=pltpu.CompilerParams(has_side_effects=DATAFLOW),
    )(*bufs, send_sems, recv_sems, after)


def _pair_sums(mine, theirs, c, chip, name):
    n = len(mine)

    def body(where_ref, *refs):
        q = pl.program_id(0)
        for a in range(n):
            total = (refs[a][...].astype(F32) + refs[n + a][...].astype(F32)).astype(BF16)
            refs[2 * n + a][...] = total

            @pl.when(q == where_ref[1])
            def _():
                refs[3 * n + a][...] = total

    block = lambda t: (None,) + t.shape[1:]
    zeros = lambda t: (0,) * (t.ndim - 1)
    outs = pl.pallas_call(
        body, name=name,
        grid_spec=pltpu.PrefetchScalarGridSpec(
            num_scalar_prefetch=1, grid=(4,),
            in_specs=[pl.BlockSpec(block(t), lambda q, w, z=zeros(t): (2 * q + w[0],) + z) for t in theirs]
            + [pl.BlockSpec(block(t), lambda q, w, z=zeros(t): (q,) + z) for t in theirs],
            out_specs=[pl.BlockSpec(block(t), lambda q, w, z=zeros(t): (q,) + z) for t in theirs]
            + [pl.BlockSpec(block(t), lambda q, w, z=zeros(t): (w[1],) + z) for t in theirs]),
        out_shape=[SDS(t.shape, BF16) for t in theirs] * 2,
        compiler_params=_params(1))(jnp.stack([c, chip]).astype(jnp.int32), *mine, *theirs)
    return list(outs[:n]), list(outs[n:])


def _local_step(x, mem, target, gains, get, put, flush, tm_huge=2048, tm_big=1024, tm_mid=512, tm_small=256):
    g_mix, pscale, g_mem, g_ffn, g_fin = gains
    T = x.shape[0]
    tm_huge, tm_big, tm_mid, tm_small = min(tm_huge, T), min(tm_big, T), min(tm_mid, T), min(tm_small, T)
    tn = DFF // 2

    w_pair, w_ids, p_ids = get("in_pair", x)
    proj, h = _fwd_proj(x, g_mix, w_pair, w_ids, p_ids, tm_huge)
    w_near, w_ids, p_ids = get("in_near", h)
    proj = _fwd_proj_more(h, w_near, proj, w_ids, p_ids, tm_huge, "fwd_proj_near")
    w_far, w_ids, p_ids = get("in_far", proj)
    proj = _fwd_proj_more(h, w_far, proj, w_ids, p_ids, tm_huge, "fwd_proj_far")
    w_in = get("in_whole", (w_pair, w_near, w_far))
    cw0, cw1, cw2, w_co, w_pool, w_kv, w_xo, w_o = get("mix", proj)
    za, conv, pooled, ya, yp, kv, memn = _fwd_mix(proj, cw0, cw1, cw2, w_co, w_pool, mem, g_mem, w_kv, tm_mid)
    o, yx, merged, x1, h2 = _fwd_merge(proj, ya, yp, x, kv, w_xo, w_o, pscale, g_ffn, tm_mid)
    wg_t, wu_t = get("gate_up", x1)
    get("down", x1, early=True)
    gate, up, act = _fwd_ffn_up(h2, wg_t, wu_t, tm_mid, tn)
    (w_d,) = get("down", gate)
    dx2, loss, dg_fin = _fwd_ffn_down_loss(act, w_d, x1, target, g_fin, tm_mid)

    dgate, dup = _bwd_ffn_down(dx2, w_d, gate, up, tm_mid, tn)
    dx1, dg_ffn = _bwd_ffn_up(dgate, dup, wg_t, wu_t, x1, dx2, g_ffn, tm_small)
    dw_d = _wgrad_dense(act, dx2, "wgrad_down", tm_huge, g_mix, a_cols=tn)
    dwg_t = _wgrad_dense(dgate, h2, "wgrad_gate", tm_huge, g_mix, a_cols=tn)
    dwu_t = _wgrad_dense(dup, h2, "wgrad_up", tm_huge, g_mix, a_cols=tn)
    token = put("ffn", (dwg_t, dwu_t, dw_d))

    dproj, dya, dyx, dza, do, dpooled, dpscale, dw_pool = _bwd_merge(
        dx1, proj, ya, yp, yx, pooled, pscale, w_o, w_co, w_xo, w_pool, tm_mid, token)
    token = flush(dya)
    dw_o = _wgrad_dense(merged, dx1, "wgrad_out", tm_big, token)
    dw_co = _wgrad_dense(za, dya, "wgrad_conv_out", tm_big, token)
    dw_xo = _wgrad_dense(o, dyx, "wgrad_xattn_out", tm_big, token)
    dproj, dw_kv, dg_mem = _bwd_attn(dproj, proj, do, kv, memn, w_kv, mem, g_mem, tm_big)
    token = put("mix", (dw_co, dw_xo, dw_o, dw_pool, dw_kv))

    dproj, dcw = _bwd_mix(dproj, proj, conv, dza, dpooled, cw0, cw1, cw2, tm_mid, token)
    token = flush(dcw)
    dw_in = _wgrad(h, dproj, name="wgrad_in", groups=NSPLIT, a_cols=D, b_cols=D, tt=tm_huge,
                   a_index=lambda g, k, t: (t, 0), b_index=lambda g, k, t: (g, t, 0),
                   o_index=lambda g, k, t: (_slot_group(g), 0, 0), out_shape=(NSPLIT, D, D), after=token)
    token = flush(put("in", (dw_in,)))
    grad_x, dg_mix = _bwd_proj(dproj, w_in, x, dx1, g_mix, tm_big, token)

    small = jnp.concatenate([dg_mix, dpscale, dg_mem, dg_ffn, dg_fin, dcw[0:3], loss], axis=0)
    return grad_x, small


def kernel(x, mem, norm_mix, w_in, conv_w, w_conv_out, w_pool, pool_scale, norm_mem, w_kv, w_xattn_out, w_out, norm_ffn, w_gate, w_up, w_down, norm_final, loss_target, m_norm_mix, m_w_in, m_conv_w, m_w_conv_out, m_w_pool, m_pool_scale, m_norm_mem, m_w_kv, m_w_xattn_out, m_w_out, m_norm_ffn, m_w_gate, m_w_up, m_w_down, m_norm_final, v_norm_mix, v_w_in, v_conv_w, v_w_conv_out, v_w_pool, v_pool_scale, v_norm_mem, v_w_kv, v_w_xattn_out, v_w_out, v_norm_ffn, v_w_gate, v_w_up, v_w_down, v_norm_final):
    T = x.shape[1]
    rows = D // NDEV
    ffb = DFF // NDEV
    prow = HD // NDEV
    me = 4 * lax.axis_index("x") + 2 * lax.axis_index("y") + lax.axis_index("c")

    shards = [w_in[0].astype(BF16), w_conv_out[0].astype(BF16), w_xattn_out[0].astype(BF16), w_out[0].astype(BF16),
              w_pool[0].astype(BF16).reshape(NPOOL * prow, HD), w_kv[0].astype(BF16),
              w_gate[0].T.astype(BF16), w_up[0].T.astype(BF16), w_down[0].astype(BF16),
              jnp.pad(conv_w[0], ((0, 5), (0, 0)))]

    cx, cy, cc = (lax.axis_index(n) for n in AXES)
    chip = 2 * cx + cy

    def land(own, index, slots):
        return lax.dynamic_update_index_in_dim(lax.empty((slots,) + own.shape, own.dtype), own, index, 0)

    needed = ["in_pair", "in_near", "in_far", "mix", "gate_up", "down"]
    members = {"mix": [9, 1, 4, 5, 2, 3], "gate_up": [6, 7], "down": [8]}
    near = (2, 4)
    plans = {"in_pair": _plan_pair(), "in_near": _plan_gather_chips(1, near), "in_far": _plan_far_chip()}
    plans.update({n: _plan_gather_chips(len(members[n])) for n in members})
    g_bufs = {"in_pair": [shards[0], land(shards[0], cc, 2)],
              "in_near": [w_in[0].astype(BF16), lax.empty((NDEV, D, D), BF16)],
              "in_far": [w_in[0].astype(BF16), lax.empty((2, D, D), BF16)]}
    g_bufs.update({n: [shards[i] for i in members[n]] + [land(shards[i], me, NDEV) for i in members[n]] for n in members})
    first_handles, _ = _copies_start([(g_bufs[n], plans[n]) for n in needed[:2]], "gather_start", x)
    g_handles = dict(zip(needed[:2], first_handles))
    pair_ids = jnp.array([0, 1], jnp.int32)

    on_last_leg = {}

    def get(group, after, early=False):
        if group == "in_whole":
            w_pair, w_near, w_far = after
            w_whole = lax.dynamic_update_slice_in_dim(w_near, w_pair, 2 * chip, 0)
            return lax.dynamic_update_slice_in_dim(w_whole, w_far, 2 * (3 - chip), 0)
        if group == "in_pair":
            bufs = _copies_wait(g_handles[group], plans[group], "gather_wait_" + group, after)
            return bufs[1], pair_ids, (2 * chip + pair_ids).astype(jnp.int32)
        if group not in on_last_leg:
            n_bufs = len(g_bufs[group])
            landed = list(range(n_bufs // 2, n_bufs))
            if group == "in_near":
                plan, more = _plan_gather_sibling(1, near), [(g_bufs[n], plans[n]) for n in needed[2:]]
            elif group == "in_far":
                plan, more = _plan_far_sibling(), []
            else:
                plan, more = _plan_gather_sibling(n_bufs // 2), []
            _, handles, token = _copies_wait_start(g_handles[group], plans[group], (landed, plan), more,
                                                   "gather_pass_" + group, after)
            g_handles.update(zip(needed[2:], handles[1:]))
            on_last_leg[group] = (handles[0], plan, token)
        if early:
            return None
        handle, plan, token = on_last_leg[group]
        got = _copies_wait(handle, plan, "gather_passed_" + group, after if group in ("gate_up", "down") else token)
        if group == "in_near":
            groups = jnp.stack([me ^ k for k in (2, 3, 4, 5)]).astype(jnp.int32)
            return got[0], groups, groups
        if group == "in_far":
            return got[0], pair_ids, (2 * (3 - chip) + pair_ids).astype(jnp.int32)
        if group == "mix":
            cw_g, w_co_g, w_pool_g, w_kv_g, w_xo_g, w_o_g = got
            cw_full = cw_g.transpose(1, 0, 2).reshape(8, D)
            w_pool_full = w_pool_g.reshape(NDEV, NPOOL, prow, HD).transpose(1, 0, 2, 3).reshape(NPOOL, HD, HD)
            return (cw_full[0:1], cw_full[1:2], cw_full[2:3], w_co_g.reshape(D, D), w_pool_full, w_kv_g,
                    w_xo_g.reshape(D, D), w_o_g.reshape(D, D))
        return [g.reshape(DFF, D) for g in got]

    started = {}

    def put(group, grads):
        if group == "ffn":
            sends = [g.reshape(NDEV, ffb, D) for g in grads]
        elif group == "mix":
            dw_co, dw_xo, dw_o, dw_pool, dw_kv = grads
            sends = [dw_co.reshape(NDEV, rows, D), dw_xo.reshape(NDEV, rows, D), dw_o.reshape(NDEV, rows, D),
                     dw_pool.reshape(NPOOL, NDEV, prow, HD).transpose(1, 0, 2, 3).reshape(NDEV, NPOOL * prow, HD), dw_kv]
        else:
            sends = list(grads)
        n = len(sends)
        halves = [lax.empty((4,) + s.shape[1:], s.dtype) for s in sends]
        (handle,), token = _copies_start([(sends + halves, _plan_scatter_sibling(n))], "scatter_swap_" + group, norm_mix)
        swapping.append((group, handle, n))
        return token

    swapping = []

    def flush(after):
        group, handle, n = swapping.pop()
        bufs = _copies_wait(handle, _plan_scatter_sibling(n), "scatter_swapped_" + group, after)
        sums, lands = _pair_sums(bufs[:n], bufs[n:], cc, chip, "pair_sums_" + group)
        (handle,), token = _copies_start([(sums + lands, _plan_scatter_chips(n))], "scatter_start_" + group, norm_mix)
        started[group] = (handle, _plan_scatter_chips(n))
        return token

    def take(group, after):
        handle, plan = started[group]
        return _copies_wait(handle, plan, "scatter_wait_" + group, after)[len(handle[2]) // 2:]

    gains = (norm_mix, pool_scale, norm_mem, norm_ffn, norm_final.reshape(1, D))
    grad_x, small = _local_step(x[0], mem[0], loss_target[0], gains, get, put, flush)

    everyone = _plan_gather_chips(1, tuple(range(1, NDEV)))
    (small_handle,), token = _copies_start([([small, land(small, me, NDEV)], everyone)], "small_start", norm_mix)

    res = {}

    def update(group, names, ws, gs, ms, vs, from_parts, steps, transposed=()):
        view = lambda a, name: a[0].T if name in transposed else a
        flat = [[view(a, name).reshape(g.shape[-2:]) for a in (w, m, v)] for name, w, g, m, v in zip(names, ws, gs, ms, vs)]
        outs = _adamw([f[0] for f in flat], gs, [f[1] for f in flat], [f[2] for f in flat], "adamw_" + group,
                      from_parts, steps)
        for name, w, four in zip(names, ws, outs):
            res[name] = [(o.T if name in transposed else o).reshape(w.shape) for o in four]

    p_g, p_u, p_d = take("ffn", token)
    update("ffn", ["w_gate", "w_up", "w_down"], [w_gate, w_up, w_down], [p_g, p_u, p_d],
           [m_w_gate, m_w_up, m_w_down], [v_w_gate, v_w_up, v_w_down], True, 2, transposed=("w_gate", "w_up"))

    small_all = _copies_wait(small_handle, everyone, "small_wait", res["w_down"][1])[1]
    small_sum = _sum_parts(small_all, "sum_small")
    loss = small_sum[8, 0]
    g_cw = lax.dynamic_slice_in_dim(small_sum[5:8], me * rows, rows, axis=1)
    update("replicated", ["norm_mix", "pool_scale", "norm_mem", "norm_ffn", "norm_final", "conv_w"],
           [norm_mix, pool_scale, norm_mem, norm_ffn, norm_final, conv_w], [small_sum[k:k + 1] for k in range(5)] + [g_cw],
           [m_norm_mix, m_pool_scale, m_norm_mem, m_norm_ffn, m_norm_final, m_conv_w],
           [v_norm_mix, v_pool_scale, v_norm_mem, v_norm_ffn, v_norm_final, v_conv_w], False, 1)

    p_co, p_xo, p_o, p_pool, p_kv = take("mix", res["conv_w"][1])
    update("mix", ["w_conv_out", "w_xattn_out", "w_out", "w_pool", "w_kv"], [w_conv_out, w_xattn_out, w_out, w_pool, w_kv],
           [p_co, p_xo, p_o, p_pool, p_kv], [m_w_conv_out, m_w_xattn_out, m_w_out, m_w_pool, m_w_kv],
           [v_w_conv_out, v_w_xattn_out, v_w_out, v_w_pool, v_w_kv], True, 2)
    (p_in,) = take("in", res["w_out"][1])
    update("in", ["w_in"], [w_in], [p_in], [m_w_in], [v_w_in], True, 4)
    order = ["norm_mix", "w_in", "conv_w", "w_conv_out", "w_pool", "pool_scale", "norm_mem", "w_kv", "w_xattn_out", "w_out",
             "norm_ffn", "w_gate", "w_up", "w_down", "norm_final"]
    return (loss, grad_x[None], *[res[n][0] for n in order], *[res[n][1] for n in order],
            *[res[n][2] for n in order], *[res[n][3] for n in order])
```

```python
import jax
import jax.numpy as jnp
from jax import lax
from jax.experimental import pallas as pl
from jax.experimental.pallas import tpu as pltpu

F32 = jnp.float32
BF16 = jnp.bfloat16
SDS = jax.ShapeDtypeStruct

AXES = ("x", "y", "c")
NDEV = 8
D = 1024
NSPLIT = 8
NH = 4
HD = D // NH
NPOOL = 4
DFF = 2816
EPS = 1e-6
ATT_SCALE = HD ** -0.5
HALO = 16


def _slot_group(s):
    return jnp.where(s < 3, s + 5, jnp.where(s == 3, 4, s - 4))


ADAM_LR = 0.001
ADAM_B1 = 0.9
ADAM_B2 = 0.999
ADAM_EPS = 1e-08
ADAM_WD = 0.01
ADAM_STEP = 10

V7X_VMEM_BYTES = 64 * 1024 * 1024
VMEM_LIMIT = V7X_VMEM_BYTES - 8 * 1024 * 1024
HBM = pl.BlockSpec(memory_space=pl.ANY)


def _whole(shape):
    return pl.BlockSpec(shape, lambda *_: (0,) * len(shape), pipeline_mode=pl.Buffered(1))


def _params(n_grid):
    return pltpu.CompilerParams(dimension_semantics=("arbitrary",) * n_grid, vmem_limit_bytes=VMEM_LIMIT)


def _mm(a, b):
    return jnp.dot(a, b, preferred_element_type=F32)


def _mm_nt(a, b):
    return lax.dot_general(a, b, (((1,), (1,)), ((), ())), preferred_element_type=F32)


def _mm_tn(a, b):
    return lax.dot_general(a, b, (((0,), (0,)), ((), ())), preferred_element_type=F32)


def _sigmoid(x):
    return 1.0 / (1.0 + jnp.exp(-x))


def _rms(x):
    return lax.rsqrt(jnp.mean(x * x, axis=-1, keepdims=True) + EPS)


def _norm_bwd(dh, x, gain):
    r = _rms(x)
    xh = x * r
    dxh = dh * gain
    dx = r * (dxh - xh * jnp.mean(dxh * xh, axis=-1, keepdims=True))
    return dx, jnp.sum(dh * xh, axis=0, keepdims=True)


def _col_chunks(n, width=512):
    return [slice(c, min(c + width, n)) for c in range(0, n, width)]


def _shift_down(v, k):
    return pltpu.roll(v, k, 0)


def _shift_up(v, k):
    return pltpu.roll(v, v.shape[0] - k, 0)


def _fwd_proj(x, gain, w_blocks, w_ids, p_ids, tm):
    T = x.shape[0]

    def body(w_ids_ref, p_ids_ref, x_ref, g_ref, w_ref, proj_ref, h_ref):
        del w_ids_ref, p_ids_ref

        @pl.when(pl.program_id(1) == 0)
        def _():
            xf = x_ref[...]
            h_ref[...] = (xf * _rms(xf) * g_ref[...]).astype(BF16)
        proj_ref[...] = _mm(h_ref[...], w_ref[...]).astype(BF16)

    return pl.pallas_call(
        body, name="fwd_proj",
        grid_spec=pltpu.PrefetchScalarGridSpec(
            num_scalar_prefetch=2, grid=(T // tm, w_ids.shape[0]),
            in_specs=[pl.BlockSpec((tm, D), lambda i, j, w, p: (i, 0)), pl.BlockSpec((1, D), lambda i, j, w, p: (0, 0)),
                      pl.BlockSpec((None, D, D), lambda i, j, w, p: (w[j], 0, 0))],
            out_specs=[pl.BlockSpec((None, tm, D), lambda i, j, w, p: (p[j], i, 0)),
                       pl.BlockSpec((tm, D), lambda i, j, w, p: (i, 0))]),
        out_shape=[SDS((NSPLIT, T, D), BF16), SDS((T, D), BF16)],
        compiler_params=_params(2))(w_ids, p_ids, x, gain, w_blocks)


def _fwd_proj_more(h, w_blocks, proj, w_ids, p_ids, tm, name):
    T = h.shape[0]

    def body(w_ids_ref, p_ids_ref, h_ref, w_ref, proj_hbm, proj_ref):
        del w_ids_ref, p_ids_ref, proj_hbm
        proj_ref[...] = _mm(h_ref[...], w_ref[...]).astype(BF16)

    return pl.pallas_call(
        body, name=name,
        grid_spec=pltpu.PrefetchScalarGridSpec(
            num_scalar_prefetch=2, grid=(T // tm, w_ids.shape[0]),
            in_specs=[pl.BlockSpec((tm, D), lambda i, j, w, p: (i, 0)),
                      pl.BlockSpec((None, D, D), lambda i, j, w, p: (w[j], 0, 0)), HBM],
            out_specs=pl.BlockSpec((None, tm, D), lambda i, j, w, p: (p[j], i, 0))),
        out_shape=SDS(proj.shape, BF16), input_output_aliases={4: 0},
        compiler_params=_params(2))(w_ids, p_ids, h, w_blocks, proj)


def _halo_before(split, tm):
    return pl.BlockSpec((None, HALO, D), lambda i: (split, jnp.maximum(i * (tm // HALO) - 1, 0), 0))


def _fwd_mix(proj, cw0, cw1, cw2, w_co, w_pool, mem, gain_mem, w_kv, tm):
    T = proj.shape[1]
    M = mem.shape[0]

    def body(b_ref, c_ref, ua_ref, up_ref, ch_ref, uah_ref, uph_ref, cw0_ref, cw1_ref, cw2_ref, wco_ref, wp_ref,
             mem_ref, gm_ref, wkv_ref, za_ref, conv_ref, pooled_ref, ya_ref, yp_ref, kv_ref, memn_ref):
        i = pl.program_id(0)

        @pl.when(i == 0)
        def _():
            m = mem_ref[...]
            memn = (m * _rms(m) * gm_ref[...]).astype(BF16)
            memn_ref[...] = memn
            for j in range(2 * NH):
                kv_ref[j] = _mm(memn, wkv_ref[j]).astype(BF16)
        keep = jnp.where(i > 0, 1.0, 0.0).astype(F32)
        cu = c_ref[...].astype(F32) * ua_ref[...].astype(F32)
        cu_h = ch_ref[...].astype(F32) * uah_ref[...].astype(F32) * keep
        ext = jnp.concatenate([cu_h, cu], axis=0)
        conv = (cw2_ref[...] * ext + cw1_ref[...] * _shift_down(ext, 1) + cw0_ref[...] * _shift_down(ext, 2))[HALO:]
        za = (b_ref[...].astype(F32) * conv).astype(BF16)
        conv_ref[...] = conv.astype(BF16)
        za_ref[...] = za
        ya_ref[...] = _mm(za, wco_ref[...]).astype(BF16)

        up = up_ref[...].astype(F32)
        ext_u = jnp.concatenate([uph_ref[...].astype(F32) * keep, up], axis=0)
        pos = i * tm + lax.broadcasted_iota(jnp.int32, (tm, HD), 0)
        for g in range(NPOOL):
            cols = slice(g * HD, (g + 1) * HD)
            s = ext_u[:, cols]
            for k in range(g + 1):
                s = s + _shift_down(s, 1 << k)
            cnt = jnp.minimum(pos + 1, 2 << g).astype(F32)
            pooled = (s[HALO:] / cnt - up[:, cols]).astype(BF16)
            pooled_ref[:, cols] = pooled
            yp_ref[:, cols] = _mm(pooled, wp_ref[g]).astype(BF16)

    tile = lambda s: pl.BlockSpec((None, tm, D), lambda i: (s, i, 0))
    row = pl.BlockSpec((1, D), lambda i: (0, 0))
    out = pl.BlockSpec((tm, D), lambda i: (i, 0))
    return pl.pallas_call(
        body, name="fwd_mix", grid=(T // tm,),
        in_specs=[tile(0), tile(1), tile(2), tile(3), _halo_before(1, tm), _halo_before(2, tm), _halo_before(3, tm),
                  row, row, row, _whole((D, D)), _whole((NPOOL, HD, HD)), _whole((M, D)), row, _whole((2 * NH, D, HD))],
        out_specs=[out] * 5 + [pl.BlockSpec((2 * NH, M, HD), lambda i: (0, 0, 0)), pl.BlockSpec((M, D), lambda i: (0, 0))],
        out_shape=[SDS((T, D), BF16)] * 5 + [SDS((2 * NH, M, HD), BF16), SDS((M, D), BF16)],
        compiler_params=_params(1))(proj, proj, proj, proj, proj, proj, proj, cw0, cw1, cw2, w_co, w_pool, mem, gain_mem, w_kv)


def _softmax_rows(s):
    e = jnp.exp(s - jnp.max(s, axis=-1, keepdims=True))
    return e / jnp.sum(e, axis=-1, keepdims=True)


def _fwd_merge(proj, ya, yp, x, kv, w_xo, w_o, pscale, gain_ffn, tm):
    T = x.shape[0]

    def body(q_ref, ga_ref, gp_ref, gx_ref, ya_ref, yp_ref, x_ref, kv_ref, wxo_ref, wo_ref, ps_ref, gf_ref,
             o_ref, yx_ref, merged_ref, x1_ref, h2_ref):
        for h in range(NH):
            cols = slice(h * HD, (h + 1) * HD)
            p = _softmax_rows(_mm_nt(q_ref[:, cols], kv_ref[h]) * ATT_SCALE)
            o_ref[:, cols] = _mm(p.astype(BF16), kv_ref[NH + h]).astype(BF16)
        yx = _mm(o_ref[...], wxo_ref[...])
        yx_ref[...] = yx.astype(BF16)
        merged = (_sigmoid(ga_ref[...].astype(F32)) * ya_ref[...].astype(F32)
                  + _sigmoid(gp_ref[...].astype(F32)) * (yp_ref[...].astype(F32) * ps_ref[...])
                  + _sigmoid(gx_ref[...].astype(F32)) * yx).astype(BF16)
        merged_ref[...] = merged
        x1 = x_ref[...] + _mm(merged, wo_ref[...])
        x1_ref[...] = x1
        h2_ref[...] = (x1 * _rms(x1) * gf_ref[...]).astype(BF16)

    tile = lambda s: pl.BlockSpec((None, tm, D), lambda i: (s, i, 0))
    row = pl.BlockSpec((1, D), lambda i: (0, 0))
    act = pl.BlockSpec((tm, D), lambda i: (i, 0))
    full = _whole((D, D))
    return pl.pallas_call(
        body, name="fwd_merge", grid=(T // tm,),
        in_specs=[tile(4), tile(5), tile(6), tile(7), act, act, act,
                  _whole((2 * NH, kv.shape[1], HD)), full, full, row, row],
        out_specs=[act] * 5,
        out_shape=[SDS((T, D), BF16), SDS((T, D), BF16), SDS((T, D), BF16), SDS((T, D), F32), SDS((T, D), BF16)],
        compiler_params=_params(1))(proj, proj, proj, proj, ya, yp, x, kv, w_xo, w_o, pscale, gain_ffn)


def _fwd_ffn_up(h2, wg_t, wu_t, tm, tn):
    T = h2.shape[0]

    def body(h_ref, wg_ref, wu_ref, gate_ref, up_ref, act_ref):
        for cols in _col_chunks(tn):
            gate = _mm_nt(h_ref[...], wg_ref[cols, :])
            up = _mm_nt(h_ref[...], wu_ref[cols, :])
            gate_ref[:, cols] = gate.astype(BF16)
            up_ref[:, cols] = up.astype(BF16)
            act_ref[:, cols] = (gate * _sigmoid(gate) * up).astype(BF16)

    w = pl.BlockSpec((tn, D), lambda n, i: (n, 0))
    o = pl.BlockSpec((tm, tn), lambda n, i: (i, n))
    return pl.pallas_call(
        body, name="fwd_ffn_up", grid=(DFF // tn, T // tm),
        in_specs=[pl.BlockSpec((tm, D), lambda n, i: (i, 0)), w, w],
        out_specs=[o] * 3, out_shape=[SDS((T, DFF), BF16)] * 3,
        compiler_params=_params(2))(h2, wg_t, wu_t)


def _fwd_ffn_down_loss(act, w_d, x1, target, gain_final, tm):
    T = x1.shape[0]

    def body(act_ref, wd_ref, x1_ref, tgt_ref, g_ref, dx2_ref, loss_ref, dgain_ref):
        @pl.when(pl.program_id(0) == 0)
        def _():
            loss_ref[...] = jnp.zeros_like(loss_ref)
            dgain_ref[...] = jnp.zeros_like(dgain_ref)
        x2 = x1_ref[...] + _mm(act_ref[...], wd_ref[...])
        gain = g_ref[...]
        y = x2 * _rms(x2) * gain
        err = y - tgt_ref[...]
        loss_ref[...] += 0.5 * jnp.sum(jnp.mean(err * err, axis=-1, keepdims=True))
        dx2, dgain = _norm_bwd(err * (1.0 / D), x2, gain)
        dx2_ref[...] = dx2
        dgain_ref[...] += dgain

    act_spec = pl.BlockSpec((tm, D), lambda i: (i, 0))
    row = pl.BlockSpec((1, D), lambda i: (0, 0))
    return pl.pallas_call(
        body, name="fwd_ffn_down_loss", grid=(T // tm,),
        in_specs=[pl.BlockSpec((tm, DFF), lambda i: (i, 0)), _whole((DFF, D)), act_spec, act_spec, row],
        out_specs=[act_spec, pl.BlockSpec((8, D), lambda i: (0, 0)), row],
        out_shape=[SDS((T, D), F32), SDS((8, D), F32), SDS((1, D), F32)],
        compiler_params=_params(1))(act, w_d, x1, target, gain_final)


def _bwd_ffn_down(dx2, w_d, gate, up, tm, tn):
    T = dx2.shape[0]

    def body(dx_ref, wd_ref, gate_ref, up_ref, dgate_ref, dup_ref):
        dx = dx_ref[...].astype(BF16)
        for cols in _col_chunks(tn):
            dact = _mm_nt(dx, wd_ref[cols, :])
            gate = gate_ref[:, cols].astype(F32)
            sg = _sigmoid(gate)
            dgate_ref[:, cols] = (dact * up_ref[:, cols].astype(F32) * (sg * (1.0 + gate * (1.0 - sg)))).astype(BF16)
            dup_ref[:, cols] = (dact * gate * sg).astype(BF16)

    o = pl.BlockSpec((tm, tn), lambda n, i: (i, n))
    return pl.pallas_call(
        body, name="bwd_ffn_down", grid=(DFF // tn, T // tm),
        in_specs=[pl.BlockSpec((tm, D), lambda n, i: (i, 0)), pl.BlockSpec((tn, D), lambda n, i: (n, 0)), o, o],
        out_specs=[o] * 2, out_shape=[SDS((T, DFF), BF16)] * 2,
        compiler_params=_params(2))(dx2, w_d, gate, up)


def _bwd_ffn_up(dgate, dup, wg_t, wu_t, x1, dx2, gain_ffn, tm):
    T = x1.shape[0]

    def body(dg_ref, du_ref, wg_ref, wu_ref, x1_ref, dx2_ref, g_ref, dx1_ref, dgain_ref):
        @pl.when(pl.program_id(0) == 0)
        def _():
            dgain_ref[...] = jnp.zeros_like(dgain_ref)
        dh2 = _mm(dg_ref[...], wg_ref[...]) + _mm(du_ref[...], wu_ref[...])
        dx, dgain = _norm_bwd(dh2, x1_ref[...], g_ref[...])
        dx1_ref[...] = dx2_ref[...] + dx
        dgain_ref[...] += dgain

    wide = pl.BlockSpec((tm, DFF), lambda i: (i, 0))
    w = _whole((DFF, D))
    act = pl.BlockSpec((tm, D), lambda i: (i, 0))
    row = pl.BlockSpec((1, D), lambda i: (0, 0))
    return pl.pallas_call(
        body, name="bwd_ffn_up", grid=(T // tm,),
        in_specs=[wide, wide, w, w, act, act, row], out_specs=[act, row],
        out_shape=[SDS((T, D), F32), SDS((1, D), F32)],
        compiler_params=_params(1))(dgate, dup, wg_t, wu_t, x1, dx2, gain_ffn)


def _wgrad(a, b, *, name, groups, a_cols, b_cols, tt, a_index, b_index, o_index, out_shape, after):
    T = a.shape[0]
    nt = T // tt
    n_a = a.shape[1] // a_cols if groups == 1 else 1

    def body(a_ref, b_ref, after_ref, o_ref, acc_ref):
        del after_ref
        t = pl.program_id(2)

        @pl.when(t == 0)
        def _():
            acc_ref[...] = jnp.zeros_like(acc_ref)
        acc_ref[...] += _mm_tn(a_ref[...].astype(BF16), b_ref[...].astype(BF16))

        @pl.when(t == nt - 1)
        def _():
            o_ref[...] = acc_ref[...].astype(o_ref.dtype)

    return pl.pallas_call(
        body, name=name, grid=(groups, n_a, nt),
        in_specs=[pl.BlockSpec((tt, a_cols), a_index), pl.BlockSpec((None, tt, b_cols), b_index), HBM],
        out_specs=pl.BlockSpec((None, a_cols, b_cols), o_index),
        out_shape=SDS(out_shape, BF16),
        scratch_shapes=[pltpu.VMEM((a_cols, b_cols), F32)],
        compiler_params=_params(3))(a, b, after)


def _wgrad_dense(a, b, name, tt, after, a_cols=None):
    ka, nb = a.shape[1], b.shape[1]
    a_cols = ka if a_cols is None else a_cols
    out = _wgrad(a, b[None], name=name, groups=1, a_cols=a_cols, b_cols=nb, tt=tt,
                 a_index=lambda g, k, t: (t, k), b_index=lambda g, k, t: (0, t, 0),
                 o_index=lambda g, k, t: (k, 0, 0), out_shape=(ka // a_cols, a_cols, nb), after=after)
    return out.reshape(ka, nb)


def _bwd_merge(dx1, proj, ya, yp, yx, pooled, pscale, w_o, w_co, w_xo, w_pool, tm, after):
    T = dx1.shape[0]
    nt = T // tm

    def body(dx1_ref, ga_ref, gp_ref, gx_ref, ya_ref, yp_ref, yx_ref, pooled_ref, ps_ref, wo_ref, wco_ref, wxo_ref, wp_ref,
             after_ref, dgates_ref, dya_ref, dyx_ref, dza_ref, do_ref, dpooled_ref, dps_ref, dwp_ref, acc_ref):
        del after_ref

        @pl.when(pl.program_id(0) == 0)
        def _():
            dps_ref[...] = jnp.zeros_like(dps_ref)
            acc_ref[...] = jnp.zeros_like(acc_ref)
        dmerged = _mm_nt(dx1_ref[...].astype(BF16), wo_ref[...])
        scale = ps_ref[...]
        sa, sp, sx = (_sigmoid(r[...].astype(F32)) for r in (ga_ref, gp_ref, gx_ref))
        ya, yp_pre, yx = (r[...].astype(F32) for r in (ya_ref, yp_ref, yx_ref))
        dgates_ref[0] = (dmerged * ya * sa * (1.0 - sa)).astype(BF16)
        dgates_ref[1] = (dmerged * (yp_pre * scale) * sp * (1.0 - sp)).astype(BF16)
        dgates_ref[2] = (dmerged * yx * sx * (1.0 - sx)).astype(BF16)
        dya = (dmerged * sa).astype(BF16)
        dyx = (dmerged * sx).astype(BF16)
        dyp = dmerged * sp
        dyps = (dyp * scale).astype(BF16)
        dps_ref[...] += jnp.sum(dyp * yp_pre, axis=0, keepdims=True)
        dya_ref[...] = dya
        dyx_ref[...] = dyx
        dza_ref[...] = _mm_nt(dya, wco_ref[...]).astype(BF16)
        do_ref[...] = _mm_nt(dyx, wxo_ref[...]).astype(BF16)
        for g in range(NPOOL):
            cols = slice(g * HD, (g + 1) * HD)
            dpooled_ref[:, cols] = _mm_nt(dyps[:, cols], wp_ref[g]).astype(BF16)
            acc_ref[g] += _mm_tn(pooled_ref[:, cols], dyps[:, cols])

        @pl.when(pl.program_id(0) == nt - 1)
        def _():
            dwp_ref[...] = acc_ref[...].astype(BF16)

    tile = lambda s: pl.BlockSpec((None, tm, D), lambda i: (s, i, 0))
    row = pl.BlockSpec((1, D), lambda i: (0, 0))
    act = pl.BlockSpec((tm, D), lambda i: (i, 0))
    full = _whole((D, D))
    return pl.pallas_call(
        body, name="bwd_merge", grid=(T // tm,),
        in_specs=[act, tile(5), tile(6), tile(7), act, act, act, act, row, full, full, full,
                  _whole((NPOOL, HD, HD)), HBM],
        out_specs=[pl.BlockSpec((3, tm, D), lambda i: (0, i, 0))] + [act] * 5
        + [row, pl.BlockSpec((NPOOL, HD, HD), lambda i: (0, 0, 0))],
        out_shape=[SDS((NSPLIT, T, D), BF16)] + [SDS((T, D), BF16)] * 5 + [SDS((1, D), F32), SDS((NPOOL, HD, HD), BF16)],
        scratch_shapes=[pltpu.VMEM((NPOOL, HD, HD), F32)],
        compiler_params=_params(1))(dx1, proj, proj, proj, ya, yp, yx, pooled, pscale, w_o, w_co, w_xo, w_pool, after)


def _bwd_attn(dproj, proj, do, kv, memn, w_kv, mem, gain_mem, tm):
    T = do.shape[0]
    M = kv.shape[1]
    nt = T // tm

    def body(dproj_hbm, q_ref, do_ref, kv_ref, memn_ref, wkv_ref, mem_ref, gm_ref, dq_ref, dw_ref, dgain_ref, dkv_ref):
        del dproj_hbm

        @pl.when(pl.program_id(0) == 0)
        def _():
            dkv_ref[...] = jnp.zeros_like(dkv_ref)
        for h in range(NH):
            cols = slice(h * HD, (h + 1) * HD)
            q = q_ref[:, cols]
            do_h = do_ref[:, cols]
            p = _softmax_rows(_mm_nt(q, kv_ref[h]) * ATT_SCALE)
            dp = _mm_nt(do_h, kv_ref[NH + h])
            ds = (p * (dp - jnp.sum(dp * p, axis=-1, keepdims=True)) * ATT_SCALE).astype(BF16)
            dq_ref[:, cols] = _mm(ds, kv_ref[h]).astype(BF16)
            dkv_ref[h] += _mm_tn(ds, q)
            dkv_ref[NH + h] += _mm_tn(p.astype(BF16), do_h)

        @pl.when(pl.program_id(0) == nt - 1)
        def _():
            dmemn = jnp.zeros((M, D), F32)
            for j in range(2 * NH):
                dkv_j = dkv_ref[j].astype(BF16)
                dw_ref[j] = _mm_tn(memn_ref[...], dkv_j).astype(BF16)
                dmemn = dmemn + _mm_nt(dkv_j, wkv_ref[j])
            dgain_ref[...] = _norm_bwd(dmemn, mem_ref[...], gm_ref[...])[1]

    row = pl.BlockSpec((1, D), lambda i: (0, 0))
    return pl.pallas_call(
        body, name="bwd_attn", grid=(nt,),
        in_specs=[HBM, pl.BlockSpec((None, tm, D), lambda i: (4, i, 0)), pl.BlockSpec((tm, D), lambda i: (i, 0)),
                  _whole((2 * NH, M, HD)), _whole((M, D)), _whole((2 * NH, D, HD)), _whole((M, D)), row],
        out_specs=[pl.BlockSpec((None, tm, D), lambda i: (3, i, 0)),
                   pl.BlockSpec((2 * NH, D, HD), lambda i: (0, 0, 0)), row],
        out_shape=[SDS(dproj.shape, BF16), SDS((2 * NH, D, HD), BF16), SDS((1, D), F32)],
        scratch_shapes=[pltpu.VMEM((2 * NH, M, HD), F32)],
        input_output_aliases={0: 0},
        compiler_params=_params(1))(dproj, proj, do, kv, memn, w_kv, mem, gain_mem)


def _bwd_mix(dproj, proj, conv, dza, dpooled, cw0, cw1, cw2, tm, after):
    T = dza.shape[0]
    nt = T // tm

    def halo_after(split_or_none):
        idx = lambda i: jnp.minimum((i + 1) * (tm // HALO), T // HALO - 1)
        if split_or_none is None:
            return pl.BlockSpec((HALO, D), lambda i: (idx(i), 0))
        return pl.BlockSpec((None, HALO, D), lambda i: (split_or_none, idx(i), 0))

    def body(dproj_hbm, b_ref, c_ref, ua_ref, conv_ref, dza_ref, dpo_ref, bn_ref, dzan_ref, dpon_ref, ch_ref, uah_ref,
             cw0_ref, cw1_ref, cw2_ref, after_ref, dabcu_ref, dcw_ref):
        del dproj_hbm, after_ref
        i = pl.program_id(0)

        @pl.when(i == 0)
        def _():
            dcw_ref[...] = jnp.zeros_like(dcw_ref)
        keep_prev = jnp.where(i > 0, 1.0, 0.0).astype(F32)
        keep_next = jnp.where(i < nt - 1, 1.0, 0.0).astype(F32)
        dza = dza_ref[...].astype(F32)
        c = c_ref[...].astype(F32)
        ua = ua_ref[...].astype(F32)
        dconv = dza * b_ref[...].astype(F32)
        dconv_n = dzan_ref[...].astype(F32) * bn_ref[...].astype(F32) * keep_next
        ext = jnp.concatenate([dconv, dconv_n], axis=0)
        dcu = (cw2_ref[...] * ext + cw1_ref[...] * _shift_up(ext, 1) + cw0_ref[...] * _shift_up(ext, 2))[:tm]
        dabcu_ref[0] = (dza * conv_ref[...].astype(F32)).astype(BF16)
        dabcu_ref[1] = (dcu * ua).astype(BF16)
        dabcu_ref[2] = (dcu * c).astype(BF16)

        cu = c * ua
        ext_cu = jnp.concatenate([ch_ref[...].astype(F32) * uah_ref[...].astype(F32) * keep_prev, cu], axis=0)
        dcw_ref[2:3, :] += jnp.sum(dconv * cu, axis=0, keepdims=True)
        dcw_ref[1:2, :] += jnp.sum(dconv * _shift_down(ext_cu, 1)[HALO:], axis=0, keepdims=True)
        dcw_ref[0:1, :] += jnp.sum(dconv * _shift_down(ext_cu, 2)[HALO:], axis=0, keepdims=True)

        dpo = dpo_ref[...].astype(F32)
        ext_dpo = jnp.concatenate([dpo, dpon_ref[...].astype(F32) * keep_next], axis=0)
        pos = i * tm + lax.broadcasted_iota(jnp.int32, (tm + HALO, HD), 0)
        for g in range(NPOOL):
            cols = slice(g * HD, (g + 1) * HD)
            s = ext_dpo[:, cols] / jnp.minimum(pos + 1, 2 << g).astype(F32)
            for k in range(g + 1):
                s = s + _shift_up(s, 1 << k)
            dabcu_ref[3, :, cols] = (s[:tm] - dpo[:, cols]).astype(BF16)

    tile = lambda s: pl.BlockSpec((None, tm, D), lambda i: (s, i, 0))
    act = pl.BlockSpec((tm, D), lambda i: (i, 0))
    row = pl.BlockSpec((1, D), lambda i: (0, 0))
    return pl.pallas_call(
        body, name="bwd_mix", grid=(nt,),
        in_specs=[HBM, tile(0), tile(1), tile(2), act, act, act, halo_after(0), halo_after(None), halo_after(None),
                  _halo_before(1, tm), _halo_before(2, tm), row, row, row, HBM],
        out_specs=[pl.BlockSpec((4, tm, D), lambda i: (1, i, 0)), pl.BlockSpec((8, D), lambda i: (0, 0))],
        out_shape=[SDS(dproj.shape, BF16), SDS((8, D), F32)],
        input_output_aliases={0: 0},
        compiler_params=_params(1))(dproj, proj, proj, proj, conv, dza, dpooled, proj, dza, dpooled, proj, proj, cw0, cw1, cw2, after)


def _bwd_proj(dproj, w_in_g, x, dx1, gain, tm, after):
    T = x.shape[0]

    def body(dp_ref, w_ref, x_ref, dx1_ref, g_ref, after_ref, dx_ref, dgain_ref, acc_ref):
        del after_ref
        i, s = pl.program_id(0), pl.program_id(1)

        @pl.when((i == 0) & (s == 0))
        def _():
            dgain_ref[...] = jnp.zeros_like(dgain_ref)

        @pl.when(s == 0)
        def _():
            acc_ref[...] = jnp.zeros_like(acc_ref)
        acc_ref[...] += _mm_nt(dp_ref[...], w_ref[...])

        @pl.when(s == NSPLIT - 1)
        def _():
            dx, dgain = _norm_bwd(acc_ref[...], x_ref[...], g_ref[...])
            dx_ref[...] = dx1_ref[...] + dx
            dgain_ref[...] += dgain

    act = pl.BlockSpec((tm, D), lambda i, s: (i, 0))
    row = pl.BlockSpec((1, D), lambda i, s: (0, 0))
    return pl.pallas_call(
        body, name="bwd_proj", grid=(T // tm, NSPLIT),
        in_specs=[pl.BlockSpec((None, tm, D), lambda i, s: (s, i, 0)),
                  pl.BlockSpec((None, D, D), lambda i, s: (_slot_group(s), 0, 0)), act, act, row, HBM],
        out_specs=[act, row], out_shape=[SDS((T, D), F32), SDS((1, D), F32)],
        scratch_shapes=[pltpu.VMEM((tm, D), F32)],
        compiler_params=_params(2))(dproj, w_in_g, x, dx1, gain, after)


def _adamw_math(w, g, m, v):
    m = ADAM_B1 * m + (1.0 - ADAM_B1) * g
    v = ADAM_B2 * v + (1.0 - ADAM_B2) * (g * g)
    m_hat = m / (1.0 - ADAM_B1 ** ADAM_STEP)
    v_hat = v / (1.0 - ADAM_B2 ** ADAM_STEP)
    delta = -ADAM_LR * (m_hat / (jnp.sqrt(v_hat) + ADAM_EPS) + ADAM_WD * w)
    return delta, m, v


def _row_tile(rows):
    return 256 if rows % 256 == 0 else rows


def _sum_parts(parts, name):
    n_parts, rows, cols = parts.shape
    tr = _row_tile(rows)

    def body(p_ref, g_ref):
        g = p_ref[0].astype(F32)
        for k in range(1, n_parts):
            g = g + p_ref[k].astype(F32)
        g_ref[...] = g

    blk = pl.BlockSpec((tr, cols), lambda i: (i, 0))
    return pl.pallas_call(
        body, name=name, grid=(rows // tr,),
        in_specs=[pl.BlockSpec((n_parts, tr, cols), lambda i: (0, i, 0))], out_specs=blk,
        out_shape=SDS((rows, cols), F32), compiler_params=_params(1))(parts)


def _adamw(ws, gs, ms, vs, name, from_parts, steps):
    n = len(ws)

    def body(*refs):
        for a in range(n):
            w_ref, g_ref, m_ref, v_ref = refs[4 * a:4 * a + 4]
            go_ref, d_ref, mo_ref, vo_ref = refs[4 * n + 4 * a:4 * n + 4 * a + 4]
            if from_parts:
                g = g_ref[0].astype(F32)
                for k in range(1, g_ref.shape[0]):
                    g = g + g_ref[k].astype(F32)
            else:
                g = g_ref[...]
            go_ref[...] = g
            d_ref[...], mo_ref[...], vo_ref[...] = _adamw_math(w_ref[...], g, m_ref[...], v_ref[...])

    in_specs, out_specs, out_shape, operands = [], [], [], []
    for w, g, m, v in zip(ws, gs, ms, vs):
        rows, cols = w.shape
        blk = pl.BlockSpec((rows // steps, cols), lambda i: (i, 0))
        g_spec = pl.BlockSpec((g.shape[0], rows // steps, cols), lambda i: (0, i, 0)) if from_parts else blk
        in_specs += [blk, g_spec, blk, blk]
        out_specs += [blk] * 4
        out_shape += [SDS((rows, cols), F32)] * 4
        operands += [w, g, m, v]
    outs = pl.pallas_call(body, name=name, grid=(steps,), in_specs=in_specs, out_specs=out_specs, out_shape=out_shape,
                          compiler_params=_params(1))(*operands)
    return [outs[4 * a:4 * a + 4] for a in range(n)]


def _peer(k, x, y, c):
    return ((1 - x) if k & 4 else x, (1 - y) if k & 2 else y, (1 - c) if k & 1 else c)


SEM = pl.BlockSpec(memory_space=pltpu.SEMAPHORE)
IN_HBM = pl.BlockSpec(memory_space=pltpu.HBM)
DATAFLOW = pltpu.SideEffectType.DATAFLOW_SIDE_EFFECTING
TOKEN_SHAPE = (8, 128)


OTHER_CHIPS = (2, 4, 6)


def _place(x, y, c):
    return 4 * x + 2 * y + c


def _plan_gather_chips(n, ks=(1,) + OTHER_CHIPS):
    def plan(refs, x, y, c, arriving):
        out = []
        for a in range(n):
            for k in ks:
                there = _place(*_peer(k, x, y, c))
                out.append((refs[a], refs[n + a].at[there if arriving else _place(x, y, c)], k))
        return out
    return plan, n * len(ks)


def _plan_gather_sibling(n, ks=OTHER_CHIPS):
    def plan(refs, x, y, c, arriving):
        out = []
        for a in range(n):
            for k in ks:
                px, py, pc = _peer(k, x, y, c)
                mine, theirs = _place(px, py, pc), _place(px, py, 1 - pc)
                out.append((refs[a].at[mine], refs[a].at[theirs if arriving else mine], 1))
        return out
    return plan, n * len(ks)


def _plan_pair():
    def plan(refs, x, y, c, arriving):
        return [(refs[0], refs[1].at[(1 - c) if arriving else c], 1)]
    return plan, 1


def _plan_far_chip():
    def plan(refs, x, y, c, arriving):
        return [(refs[0], refs[1].at[c], 6)]
    return plan, 1


def _plan_far_sibling():
    def plan(refs, x, y, c, arriving):
        return [(refs[0].at[c], refs[0].at[(1 - c) if arriving else c], 1)]
    return plan, 1


def _plan_scatter_sibling(n):
    def plan(refs, x, y, c, arriving):
        out = []
        for a in range(n):
            for q in range(4):
                out.append((refs[a].at[2 * q + (1 - c)], refs[n + a].at[q], 1))
        return out
    return plan, n * 4


def _plan_scatter_chips(n):
    def plan(refs, x, y, c, arriving):
        out = []
        for a in range(n):
            for k in OTHER_CHIPS:
                px, py, _ = _peer(k, x, y, c)
                out.append((refs[a].at[2 * px + py], refs[n + a].at[(2 * px + py) if arriving else (2 * x + y)], k))
        return out
    return plan, n * 3


def _remote(src, dst, send_sems, recv_sems, i, k):
    x, y, c = (lax.axis_index(n) for n in AXES)
    return pltpu.make_async_remote_copy(src_ref=src, dst_ref=dst, send_sem=send_sems.at[i], recv_sem=recv_sems.at[i],
                                        device_id=_peer(k, x, y, c), device_id_type=pl.DeviceIdType.MESH)


def _copies_start(groups, name, after):
    ng = len(groups)
    total = sum(len(bufs) for bufs, _ in groups)

    def body(*refs):
        sems = refs[1 + total:1 + total + 2 * ng]
        x, y, c = (lax.axis_index(n) for n in AXES)
        off = 1
        for gi, (bufs, (plan, _)) in enumerate(groups):
            for i, (src, dst, k) in enumerate(plan(refs[off:off + len(bufs)], x, y, c, False)):
                _remote(src, dst, sems[2 * gi], sems[2 * gi + 1], i, k).start()
            off += len(bufs)
        refs[-1][...] = jnp.zeros(TOKEN_SHAPE, F32)

    sem_shapes = [pltpu.SemaphoreType.DMA((count,)) for _, (_, count) in groups for _ in range(2)]
    flat = [b for bufs, _ in groups for b in bufs]
    outs = pl.pallas_call(
        body, name=name,
        in_specs=[HBM] + [IN_HBM] * total,
        out_specs=[SEM] * (2 * ng) + [IN_HBM] * total + [pl.BlockSpec(memory_space=pltpu.VMEM)],
        out_shape=sem_shapes + [pltpu.HBM(b.shape, b.dtype) for b in flat] + [SDS(TOKEN_SHAPE, F32)],
        input_output_aliases={1 + i: 2 * ng + i for i in range(total)},
        compiler_params=pltpu.CompilerParams(has_side_effects=DATAFLOW),
    )(after, *[pltpu.with_memory_space_constraint(b, pltpu.HBM) for b in flat])
    handles, off = [], 2 * ng
    for gi, (bufs, _) in enumerate(groups):
        handles.append((outs[2 * gi], outs[2 * gi + 1], list(outs[off:off + len(bufs)])))
        off += len(bufs)
    return handles, outs[-1]


def _copies_wait_start(handle, plan, pass_on, more, name, after):
    send_sems, recv_sems, bufs = handle
    n = len(bufs)
    idx, (pass_plan, pass_count) = pass_on
    total = sum(len(b) for b, _ in more)
    ng = 1 + len(more)

    def body(*refs):
        x, y, c = (lax.axis_index(a) for a in AXES)
        waited = refs[1:1 + n]
        outs = refs[3 + n + total:]
        new_sems = outs[n + total:n + total + 2 * ng]
        for i, (src, dst, k) in enumerate(plan[0](waited, x, y, c, True)):
            copy = _remote(src, dst, refs[1 + n + total], refs[2 + n + total], i, k)
            copy.wait_send()
            copy.wait_recv()
        for i, (src, dst, k) in enumerate(pass_plan([waited[j] for j in idx], x, y, c, False)):
            _remote(src, dst, new_sems[0], new_sems[1], i, k).start()
        off = 1 + n
        for gi, (b, (p, _)) in enumerate(more):
            for i, (src, dst, k) in enumerate(p(refs[off:off + len(b)], x, y, c, False)):
                _remote(src, dst, new_sems[2 + 2 * gi], new_sems[3 + 2 * gi], i, k).start()
            off += len(b)
        outs[-1][...] = jnp.zeros(TOKEN_SHAPE, F32)

    flat = list(bufs) + [a for b, _ in more for a in b]
    sem_shapes = [pltpu.SemaphoreType.DMA((count,)) for count in [pass_count] + [cnt for _, (_, cnt) in more] for _ in range(2)]
    outs = pl.pallas_call(
        body, name=name,
        in_specs=[HBM] + [IN_HBM] * (n + total) + [SEM, SEM],
        out_specs=[IN_HBM] * (n + total) + [SEM] * (2 * ng) + [pl.BlockSpec(memory_space=pltpu.VMEM)],
        out_shape=[pltpu.HBM(b.shape, b.dtype) for b in flat] + sem_shapes + [SDS(TOKEN_SHAPE, F32)],
        input_output_aliases={1 + i: i for i in range(n + total)},
        compiler_params=pltpu.CompilerParams(has_side_effects=DATAFLOW),
    )(after, *[pltpu.with_memory_space_constraint(b, pltpu.HBM) for b in flat], send_sems, recv_sems)
    thru = list(outs[:n])
    sems_out = outs[n + total:n + total + 2 * ng]
    handles = [(sems_out[0], sems_out[1], [thru[j] for j in idx])]
    off = n
    for gi, (b, _) in enumerate(more):
        handles.append((sems_out[2 + 2 * gi], sems_out[3 + 2 * gi], list(outs[off:off + len(b)])))
        off += len(b)
    return thru, handles, outs[-1]


def _copies_wait(handle, plan, name, after):
    send_sems, recv_sems, bufs = handle
    n = len(bufs)

    def body(*refs):
        x, y, c = (lax.axis_index(a) for a in AXES)
        for i, (src, dst, k) in enumerate(plan[0](refs[:n], x, y, c, True)):
            copy = _remote(src, dst, refs[n], refs[n + 1], i, k)
            copy.wait_send()
            copy.wait_recv()

    return pl.pallas_call(
        body, name=name,
        in_specs=[IN_HBM] * n + [SEM, SEM, HBM], out_specs=[IN_HBM] * n,
        out_shape=[pltpu.HBM(b.shape, b.dtype) for b in bufs],
        input_output_aliases={i: i for i in range(n)},
        compiler_params=pltpu.CompilerParams(has_side_effects=DATAFLOW),
    )(*bufs, send_sems, recv_sems, after)


def _pair_sums(mine, theirs, c, chip, name):
    n = len(mine)

    def body(where_ref, *refs):
        q = pl.program_id(0)
        for a in range(n):
            total = (refs[a][...].astype(F32) + refs[n + a][...].astype(F32)).astype(BF16)
            refs[2 * n + a][...] = total

            @pl.when(q == where_ref[1])
            def _():
                refs[3 * n + a][...] = total

    block = lambda t: (None,) + t.shape[1:]
    zeros = lambda t: (0,) * (t.ndim - 1)
    outs = pl.pallas_call(
        body, name=name,
        grid_spec=pltpu.PrefetchScalarGridSpec(
            num_scalar_prefetch=1, grid=(4,),
            in_specs=[pl.BlockSpec(block(t), lambda q, w, z=zeros(t): (2 * q + w[0],) + z) for t in theirs]
            + [pl.BlockSpec(block(t), lambda q, w, z=zeros(t): (q,) + z) for t in theirs],
            out_specs=[pl.BlockSpec(block(t), lambda q, w, z=zeros(t): (q,) + z) for t in theirs]
            + [pl.BlockSpec(block(t), lambda q, w, z=zeros(t): (w[1],) + z) for t in theirs]),
        out_shape=[SDS(t.shape, BF16) for t in theirs] * 2,
        compiler_params=_params(1))(jnp.stack([c, chip]).astype(jnp.int32), *mine, *theirs)
    return list(outs[:n]), list(outs[n:])


def _local_step(x, mem, target, gains, get, put, flush, tm_huge=2048, tm_big=1024, tm_mid=512, tm_small=256):
    g_mix, pscale, g_mem, g_ffn, g_fin = gains
    T = x.shape[0]
    tm_huge, tm_big, tm_mid, tm_small = min(tm_huge, T), min(tm_big, T), min(tm_mid, T), min(tm_small, T)
    tn = DFF // 2

    w_pair, w_ids, p_ids = get("in_pair", x)
    proj, h = _fwd_proj(x, g_mix, w_pair, w_ids, p_ids, tm_huge)
    w_near, w_ids, p_ids = get("in_near", h)
    proj = _fwd_proj_more(h, w_near, proj, w_ids, p_ids, tm_huge, "fwd_proj_near")
    w_far, w_ids, p_ids = get("in_far", proj)
    proj = _fwd_proj_more(h, w_far, proj, w_ids, p_ids, tm_huge, "fwd_proj_far")
    w_in = get("in_whole", (w_pair, w_near, w_far))
    cw0, cw1, cw2, w_co, w_pool, w_kv, w_xo, w_o = get("mix", proj)
    za, conv, pooled, ya, yp, kv, memn = _fwd_mix(proj, cw0, cw1, cw2, w_co, w_pool, mem, g_mem, w_kv, tm_mid)
    o, yx, merged, x1, h2 = _fwd_merge(proj, ya, yp, x, kv, w_xo, w_o, pscale, g_ffn, tm_mid)
    wg_t, wu_t = get("gate_up", x1)
    get("down", x1, early=True)
    gate, up, act = _fwd_ffn_up(h2, wg_t, wu_t, tm_big, tn)
    (w_d,) = get("down", gate)
    dx2, loss, dg_fin = _fwd_ffn_down_loss(act, w_d, x1, target, g_fin, tm_mid)

    dgate, dup = _bwd_ffn_down(dx2, w_d, gate, up, tm_big, tn)
    dx1, dg_ffn = _bwd_ffn_up(dgate, dup, wg_t, wu_t, x1, dx2, g_ffn, tm_small)
    dw_d = _wgrad_dense(act, dx2, "wgrad_down", tm_huge, g_mix, a_cols=tn)
    dwg_t = _wgrad_dense(dgate, h2, "wgrad_gate", tm_huge, g_mix, a_cols=tn)
    dwu_t = _wgrad_dense(dup, h2, "wgrad_up", tm_huge, g_mix, a_cols=tn)
    token = put("ffn", (dwg_t, dwu_t, dw_d))

    dproj, dya, dyx, dza, do, dpooled, dpscale, dw_pool = _bwd_merge(
        dx1, proj, ya, yp, yx, pooled, pscale, w_o, w_co, w_xo, w_pool, tm_mid, token)
    token = flush(dya)
    dw_o = _wgrad_dense(merged, dx1, "wgrad_out", tm_big, token)
    dw_co = _wgrad_dense(za, dya, "wgrad_conv_out", tm_big, token)
    dw_xo = _wgrad_dense(o, dyx, "wgrad_xattn_out", tm_big, token)
    dproj, dw_kv, dg_mem = _bwd_attn(dproj, proj, do, kv, memn, w_kv, mem, g_mem, tm_big)
    token = put("mix", (dw_co, dw_xo, dw_o, dw_pool, dw_kv))

    dproj, dcw = _bwd_mix(dproj, proj, conv, dza, dpooled, cw0, cw1, cw2, tm_mid, token)
    token = flush(dcw)
    dw_in = _wgrad(h, dproj, name="wgrad_in", groups=NSPLIT, a_cols=D, b_cols=D, tt=tm_huge,
                   a_index=lambda g, k, t: (t, 0), b_index=lambda g, k, t: (g, t, 0),
                   o_index=lambda g, k, t: (_slot_group(g), 0, 0), out_shape=(NSPLIT, D, D), after=token)
    token = flush(put("in", (dw_in,)))
    grad_x, dg_mix = _bwd_proj(dproj, w_in, x, dx1, g_mix, tm_big, token)

    small = jnp.concatenate([dg_mix, dpscale, dg_mem, dg_ffn, dg_fin, dcw[0:3], loss], axis=0)
    return grad_x, small


def kernel(x, mem, norm_mix, w_in, conv_w, w_conv_out, w_pool, pool_scale, norm_mem, w_kv, w_xattn_out, w_out, norm_ffn, w_gate, w_up, w_down, norm_final, loss_target, m_norm_mix, m_w_in, m_conv_w, m_w_conv_out, m_w_pool, m_pool_scale, m_norm_mem, m_w_kv, m_w_xattn_out, m_w_out, m_norm_ffn, m_w_gate, m_w_up, m_w_down, m_norm_final, v_norm_mix, v_w_in, v_conv_w, v_w_conv_out, v_w_pool, v_pool_scale, v_norm_mem, v_w_kv, v_w_xattn_out, v_w_out, v_norm_ffn, v_w_gate, v_w_up, v_w_down, v_norm_final):
    T = x.shape[1]
    rows = D // NDEV
    ffb = DFF // NDEV
    prow = HD // NDEV
    me = 4 * lax.axis_index("x") + 2 * lax.axis_index("y") + lax.axis_index("c")

    shards = [w_in[0].astype(BF16), w_conv_out[0].astype(BF16), w_xattn_out[0].astype(BF16), w_out[0].astype(BF16),
              w_pool[0].astype(BF16).reshape(NPOOL * prow, HD), w_kv[0].astype(BF16),
              w_gate[0].T.astype(BF16), w_up[0].T.astype(BF16), w_down[0].astype(BF16),
              jnp.pad(conv_w[0], ((0, 5), (0, 0)))]

    cx, cy, cc = (lax.axis_index(n) for n in AXES)
    chip = 2 * cx + cy

    def land(own, index, slots):
        return lax.dynamic_update_index_in_dim(lax.empty((slots,) + own.shape, own.dtype), own, index, 0)

    needed = ["in_pair", "in_near", "in_far", "mix", "gate_up", "down"]
    members = {"mix": [9, 1, 4, 5, 2, 3], "gate_up": [6, 7], "down": [8]}
    near = (2, 4)
    plans = {"in_pair": _plan_pair(), "in_near": _plan_gather_chips(1, near), "in_far": _plan_far_chip()}
    plans.update({n: _plan_gather_chips(len(members[n])) for n in members})
    g_bufs = {"in_pair": [shards[0], land(shards[0], cc, 2)],
              "in_near": [w_in[0].astype(BF16), lax.empty((NDEV, D, D), BF16)],
              "in_far": [w_in[0].astype(BF16), lax.empty((2, D, D), BF16)]}
    g_bufs.update({n: [shards[i] for i in members[n]] + [land(shards[i], me, NDEV) for i in members[n]] for n in members})
    first_handles, _ = _copies_start([(g_bufs[n], plans[n]) for n in needed[:2]], "gather_start", x)
    g_handles = dict(zip(needed[:2], first_handles))
    pair_ids = jnp.array([0, 1], jnp.int32)

    on_last_leg = {}

    def get(group, after, early=False):
        if group == "in_whole":
            w_pair, w_near, w_far = after
            w_whole = lax.dynamic_update_slice_in_dim(w_near, w_pair, 2 * chip, 0)
            return lax.dynamic_update_slice_in_dim(w_whole, w_far, 2 * (3 - chip), 0)
        if group == "in_pair":
            bufs = _copies_wait(g_handles[group], plans[group], "gather_wait_" + group, after)
            return bufs[1], pair_ids, (2 * chip + pair_ids).astype(jnp.int32)
        if group not in on_last_leg:
            n_bufs = len(g_bufs[group])
            landed = list(range(n_bufs // 2, n_bufs))
            if group == "in_near":
                plan, more = _plan_gather_sibling(1, near), [(g_bufs[n], plans[n]) for n in needed[2:]]
            elif group == "in_far":
                plan, more = _plan_far_sibling(), []
            else:
                plan, more = _plan_gather_sibling(n_bufs // 2), []
            _, handles, token = _copies_wait_start(g_handles[group], plans[group], (landed, plan), more,
                                                   "gather_pass_" + group, after)
            g_handles.update(zip(needed[2:], handles[1:]))
            on_last_leg[group] = (handles[0], plan, token)
        if early:
            return None
        handle, plan, token = on_last_leg[group]
        got = _copies_wait(handle, plan, "gather_passed_" + group, after if group in ("gate_up", "down") else token)
        if group == "in_near":
            groups = jnp.stack([me ^ k for k in (2, 3, 4, 5)]).astype(jnp.int32)
            return got[0], groups, groups
        if group == "in_far":
            return got[0], pair_ids, (2 * (3 - chip) + pair_ids).astype(jnp.int32)
        if group == "mix":
            cw_g, w_co_g, w_pool_g, w_kv_g, w_xo_g, w_o_g = got
            cw_full = cw_g.transpose(1, 0, 2).reshape(8, D)
            w_pool_full = w_pool_g.reshape(NDEV, NPOOL, prow, HD).transpose(1, 0, 2, 3).reshape(NPOOL, HD, HD)
            return (cw_full[0:1], cw_full[1:2], cw_full[2:3], w_co_g.reshape(D, D), w_pool_full, w_kv_g,
                    w_xo_g.reshape(D, D), w_o_g.reshape(D, D))
        return [g.reshape(DFF, D) for g in got]

    started = {}

    def put(group, grads):
        if group == "ffn":
            sends = [g.reshape(NDEV, ffb, D) for g in grads]
        elif group == "mix":
            dw_co, dw_xo, dw_o, dw_pool, dw_kv = grads
            sends = [dw_co.reshape(NDEV, rows, D), dw_xo.reshape(NDEV, rows, D), dw_o.reshape(NDEV, rows, D),
                     dw_pool.reshape(NPOOL, NDEV, prow, HD).transpose(1, 0, 2, 3).reshape(NDEV, NPOOL * prow, HD), dw_kv]
        else:
            sends = list(grads)
        n = len(sends)
        halves = [lax.empty((4,) + s.shape[1:], s.dtype) for s in sends]
        (handle,), token = _copies_start([(sends + halves, _plan_scatter_sibling(n))], "scatter_swap_" + group, norm_mix)
        swapping.append((group, handle, n))
        return token

    swapping = []

    def flush(after):
        group, handle, n = swapping.pop()
        bufs = _copies_wait(handle, _plan_scatter_sibling(n), "scatter_swapped_" + group, after)
        sums, lands = _pair_sums(bufs[:n], bufs[n:], cc, chip, "pair_sums_" + group)
        (handle,), token = _copies_start([(sums + lands, _plan_scatter_chips(n))], "scatter_start_" + group, norm_mix)
        started[group] = (handle, _plan_scatter_chips(n))
        return token

    def take(group, after):
        handle, plan = started[group]
        return _copies_wait(handle, plan, "scatter_wait_" + group, after)[len(handle[2]) // 2:]

    gains = (norm_mix, pool_scale, norm_mem, norm_ffn, norm_final.reshape(1, D))
    grad_x, small = _local_step(x[0], mem[0], loss_target[0], gains, get, put, flush)

    everyone = _plan_gather_chips(1, tuple(range(1, NDEV)))
    (small_handle,), token = _copies_start([([small, land(small, me, NDEV)], everyone)], "small_start", norm_mix)

    res = {}

    def update(group, names, ws, gs, ms, vs, from_parts, steps, transposed=()):
        view = lambda a, name: a[0].T if name in transposed else a
        flat = [[view(a, name).reshape(g.shape[-2:]) for a in (w, m, v)] for name, w, g, m, v in zip(names, ws, gs, ms, vs)]
        outs = _adamw([f[0] for f in flat], gs, [f[1] for f in flat], [f[2] for f in flat], "adamw_" + group,
                      from_parts, steps)
        for name, w, four in zip(names, ws, outs):
            res[name] = [(o.T if name in transposed else o).reshape(w.shape) for o in four]

    p_g, p_u, p_d = take("ffn", token)
    update("ffn", ["w_gate", "w_up", "w_down"], [w_gate, w_up, w_down], [p_g, p_u, p_d],
           [m_w_gate, m_w_up, m_w_down], [v_w_gate, v_w_up, v_w_down], True, 2, transposed=("w_gate", "w_up"))

    small_all = _copies_wait(small_handle, everyone, "small_wait", res["w_down"][1])[1]
    small_sum = _sum_parts(small_all, "sum_small")
    loss = small_sum[8, 0]
    g_cw = lax.dynamic_slice_in_dim(small_sum[5:8], me * rows, rows, axis=1)
    update("replicated", ["norm_mix", "pool_scale", "norm_mem", "norm_ffn", "norm_final", "conv_w"],
           [norm_mix, pool_scale, norm_mem, norm_ffn, norm_final, conv_w], [small_sum[k:k + 1] for k in range(5)] + [g_cw],
           [m_norm_mix, m_pool_scale, m_norm_mem, m_norm_ffn, m_norm_final, m_conv_w],
           [v_norm_mix, v_pool_scale, v_norm_mem, v_norm_ffn, v_norm_final, v_conv_w], False, 1)

    p_co, p_xo, p_o, p_pool, p_kv = take("mix", res["conv_w"][1])
    update("mix", ["w_conv_out", "w_xattn_out", "w_out", "w_pool", "w_kv"], [w_conv_out, w_xattn_out, w_out, w_pool, w_kv],
           [p_co, p_xo, p_o, p_pool, p_kv], [m_w_conv_out, m_w_xattn_out, m_w_out, m_w_pool, m_w_kv],
           [v_w_conv_out, v_w_xattn_out, v_w_out, v_w_pool, v_w_kv], True, 2)
    (p_in,) = take("in", res["w_out"][1])
    update("in", ["w_in"], [w_in], [p_in], [m_w_in], [v_w_in], True, 4)
    order = ["norm_mix", "w_in", "conv_w", "w_conv_out", "w_pool", "pool_scale", "norm_mem", "w_kv", "w_xattn_out", "w_out",
             "norm_ffn", "w_gate", "w_up", "w_down", "norm_final"]
    return (loss, grad_x[None], *[res[n][0] for n in order], *[res[n][1] for n in order],
            *[res[n][2] for n in order], *[res[n][3] for n in order])
```

```python
import jax
import jax.numpy as jnp
from jax import lax
from jax.experimental import pallas as pl
from jax.experimental.pallas import tpu as pltpu

F32 = jnp.float32
BF16 = jnp.bfloat16
SDS = jax.ShapeDtypeStruct

AXES = ("x", "y", "c")
NDEV = 8
D = 1024
NSPLIT = 8
NH = 4
HD = D // NH
NPOOL = 4
DFF = 2816
EPS = 1e-6
ATT_SCALE = HD ** -0.5
HALO = 16


def _slot_group(s):
    return jnp.where(s < 3, s + 5, jnp.where(s == 3, 4, s - 4))


ADAM_LR = 0.001
ADAM_B1 = 0.9
ADAM_B2 = 0.999
ADAM_EPS = 1e-08
ADAM_WD = 0.01
ADAM_STEP = 10

V7X_VMEM_BYTES = 64 * 1024 * 1024
VMEM_LIMIT = V7X_VMEM_BYTES - 8 * 1024 * 1024
HBM = pl.BlockSpec(memory_space=pl.ANY)


def _whole(shape):
    return pl.BlockSpec(shape, lambda *_: (0,) * len(shape), pipeline_mode=pl.Buffered(1))


def _params(n_grid):
    return pltpu.CompilerParams(dimension_semantics=("arbitrary",) * n_grid, vmem_limit_bytes=VMEM_LIMIT)


def _mm(a, b):
    return jnp.dot(a, b, preferred_element_type=F32)


def _mm_nt(a, b):
    return lax.dot_general(a, b, (((1,), (1,)), ((), ())), preferred_element_type=F32)


def _mm_tn(a, b):
    return lax.dot_general(a, b, (((0,), (0,)), ((), ())), preferred_element_type=F32)


def _sigmoid(x):
    return 1.0 / (1.0 + jnp.exp(-x))


def _rms(x):
    return lax.rsqrt(jnp.mean(x * x, axis=-1, keepdims=True) + EPS)


def _norm_bwd(dh, x, gain):
    r = _rms(x)
    xh = x * r
    dxh = dh * gain
    dx = r * (dxh - xh * jnp.mean(dxh * xh, axis=-1, keepdims=True))
    return dx, jnp.sum(dh * xh, axis=0, keepdims=True)


def _col_chunks(n, width=512):
    return [slice(c, min(c + width, n)) for c in range(0, n, width)]


def _shift_down(v, k):
    return pltpu.roll(v, k, 0)


def _shift_up(v, k):
    return pltpu.roll(v, v.shape[0] - k, 0)


def _fwd_proj(x, gain, w_blocks, w_ids, p_ids, tm):
    T = x.shape[0]

    def body(w_ids_ref, p_ids_ref, x_ref, g_ref, w_ref, proj_ref, h_ref):
        del w_ids_ref, p_ids_ref

        @pl.when(pl.program_id(1) == 0)
        def _():
            xf = x_ref[...]
            h_ref[...] = (xf * _rms(xf) * g_ref[...]).astype(BF16)
        proj_ref[...] = _mm(h_ref[...], w_ref[...]).astype(BF16)

    return pl.pallas_call(
        body, name="fwd_proj",
        grid_spec=pltpu.PrefetchScalarGridSpec(
            num_scalar_prefetch=2, grid=(T // tm, w_ids.shape[0]),
            in_specs=[pl.BlockSpec((tm, D), lambda i, j, w, p: (i, 0)), pl.BlockSpec((1, D), lambda i, j, w, p: (0, 0)),
                      pl.BlockSpec((None, D, D), lambda i, j, w, p: (w[j], 0, 0))],
            out_specs=[pl.BlockSpec((None, tm, D), lambda i, j, w, p: (p[j], i, 0)),
                       pl.BlockSpec((tm, D), lambda i, j, w, p: (i, 0))]),
        out_shape=[SDS((NSPLIT, T, D), BF16), SDS((T, D), BF16)],
        compiler_params=_params(2))(w_ids, p_ids, x, gain, w_blocks)


def _fwd_proj_more(h, w_blocks, proj, w_ids, p_ids, tm, name):
    T = h.shape[0]

    def body(w_ids_ref, p_ids_ref, h_ref, w_ref, proj_hbm, proj_ref):
        del w_ids_ref, p_ids_ref, proj_hbm
        proj_ref[...] = _mm(h_ref[...], w_ref[...]).astype(BF16)

    return pl.pallas_call(
        body, name=name,
        grid_spec=pltpu.PrefetchScalarGridSpec(
            num_scalar_prefetch=2, grid=(T // tm, w_ids.shape[0]),
            in_specs=[pl.BlockSpec((tm, D), lambda i, j, w, p: (i, 0)),
                      pl.BlockSpec((None, D, D), lambda i, j, w, p: (w[j], 0, 0)), HBM],
            out_specs=pl.BlockSpec((None, tm, D), lambda i, j, w, p: (p[j], i, 0))),
        out_shape=SDS(proj.shape, BF16), input_output_aliases={4: 0},
        compiler_params=_params(2))(w_ids, p_ids, h, w_blocks, proj)


def _halo_before(split, tm):
    return pl.BlockSpec((None, HALO, D), lambda i: (split, jnp.maximum(i * (tm // HALO) - 1, 0), 0))


def _fwd_mix(proj, cw0, cw1, cw2, w_co, w_pool, mem, gain_mem, w_kv, tm):
    T = proj.shape[1]
    M = mem.shape[0]

    def body(b_ref, c_ref, ua_ref, up_ref, ch_ref, uah_ref, uph_ref, cw0_ref, cw1_ref, cw2_ref, wco_ref, wp_ref,
             mem_ref, gm_ref, wkv_ref, za_ref, conv_ref, pooled_ref, ya_ref, yp_ref, kv_ref, memn_ref):
        i = pl.program_id(0)

        @pl.when(i == 0)
        def _():
            m = mem_ref[...]
            memn = (m * _rms(m) * gm_ref[...]).astype(BF16)
            memn_ref[...] = memn
            for j in range(2 * NH):
                kv_ref[j] = _mm(memn, wkv_ref[j]).astype(BF16)
        keep = jnp.where(i > 0, 1.0, 0.0).astype(F32)
        cu = c_ref[...].astype(F32) * ua_ref[...].astype(F32)
        cu_h = ch_ref[...].astype(F32) * uah_ref[...].astype(F32) * keep
        ext = jnp.concatenate([cu_h, cu], axis=0)
        conv = (cw2_ref[...] * ext + cw1_ref[...] * _shift_down(ext, 1) + cw0_ref[...] * _shift_down(ext, 2))[HALO:]
        za = (b_ref[...].astype(F32) * conv).astype(BF16)
        conv_ref[...] = conv.astype(BF16)
        za_ref[...] = za
        ya_ref[...] = _mm(za, wco_ref[...]).astype(BF16)

        up = up_ref[...].astype(F32)
        ext_u = jnp.concatenate([uph_ref[...].astype(F32) * keep, up], axis=0)
        pos = i * tm + lax.broadcasted_iota(jnp.int32, (tm, HD), 0)
        for g in range(NPOOL):
            cols = slice(g * HD, (g + 1) * HD)
            s = ext_u[:, cols]
            for k in range(g + 1):
                s = s + _shift_down(s, 1 << k)
            cnt = jnp.minimum(pos + 1, 2 << g).astype(F32)
            pooled = (s[HALO:] / cnt - up[:, cols]).astype(BF16)
            pooled_ref[:, cols] = pooled
            yp_ref[:, cols] = _mm(pooled, wp_ref[g]).astype(BF16)

    tile = lambda s: pl.BlockSpec((None, tm, D), lambda i: (s, i, 0))
    row = pl.BlockSpec((1, D), lambda i: (0, 0))
    out = pl.BlockSpec((tm, D), lambda i: (i, 0))
    return pl.pallas_call(
        body, name="fwd_mix", grid=(T // tm,),
        in_specs=[tile(0), tile(1), tile(2), tile(3), _halo_before(1, tm), _halo_before(2, tm), _halo_before(3, tm),
                  row, row, row, _whole((D, D)), _whole((NPOOL, HD, HD)), _whole((M, D)), row, _whole((2 * NH, D, HD))],
        out_specs=[out] * 5 + [pl.BlockSpec((2 * NH, M, HD), lambda i: (0, 0, 0)), pl.BlockSpec((M, D), lambda i: (0, 0))],
        out_shape=[SDS((T, D), BF16)] * 5 + [SDS((2 * NH, M, HD), BF16), SDS((M, D), BF16)],
        compiler_params=_params(1))(proj, proj, proj, proj, proj, proj, proj, cw0, cw1, cw2, w_co, w_pool, mem, gain_mem, w_kv)


def _softmax_rows(s):
    e = jnp.exp(s - jnp.max(s, axis=-1, keepdims=True))
    return e / jnp.sum(e, axis=-1, keepdims=True)


def _fwd_merge(proj, ya, yp, x, kv, w_xo, w_o, pscale, gain_ffn, tm):
    T = x.shape[0]

    def body(q_ref, ga_ref, gp_ref, gx_ref, ya_ref, yp_ref, x_ref, kv_ref, wxo_ref, wo_ref, ps_ref, gf_ref,
             o_ref, yx_ref, merged_ref, x1_ref, h2_ref):
        for h in range(NH):
            cols = slice(h * HD, (h + 1) * HD)
            p = _softmax_rows(_mm_nt(q_ref[:, cols], kv_ref[h]) * ATT_SCALE)
            o_ref[:, cols] = _mm(p.astype(BF16), kv_ref[NH + h]).astype(BF16)
        yx = _mm(o_ref[...], wxo_ref[...])
        yx_ref[...] = yx.astype(BF16)
        merged = (_sigmoid(ga_ref[...].astype(F32)) * ya_ref[...].astype(F32)
                  + _sigmoid(gp_ref[...].astype(F32)) * (yp_ref[...].astype(F32) * ps_ref[...])
                  + _sigmoid(gx_ref[...].astype(F32)) * yx).astype(BF16)
        merged_ref[...] = merged
        x1 = x_ref[...] + _mm(merged, wo_ref[...])
        x1_ref[...] = x1
        h2_ref[...] = (x1 * _rms(x1) * gf_ref[...]).astype(BF16)

    tile = lambda s: pl.BlockSpec((None, tm, D), lambda i: (s, i, 0))
    row = pl.BlockSpec((1, D), lambda i: (0, 0))
    act = pl.BlockSpec((tm, D), lambda i: (i, 0))
    full = _whole((D, D))
    return pl.pallas_call(
        body, name="fwd_merge", grid=(T // tm,),
        in_specs=[tile(4), tile(5), tile(6), tile(7), act, act, act,
                  _whole((2 * NH, kv.shape[1], HD)), full, full, row, row],
        out_specs=[act] * 5,
        out_shape=[SDS((T, D), BF16), SDS((T, D), BF16), SDS((T, D), BF16), SDS((T, D), F32), SDS((T, D), BF16)],
        compiler_params=_params(1))(proj, proj, proj, proj, ya, yp, x, kv, w_xo, w_o, pscale, gain_ffn)


def _fwd_ffn_up(h2, wg_t, wu_t, tm, tn):
    T = h2.shape[0]

    def body(h_ref, wg_ref, wu_ref, gate_ref, up_ref, act_ref):
        for cols in _col_chunks(tn):
            gate = _mm_nt(h_ref[...], wg_ref[cols, :])
            up = _mm_nt(h_ref[...], wu_ref[cols, :])
            gate_ref[:, cols] = gate.astype(BF16)
            up_ref[:, cols] = up.astype(BF16)
            act_ref[:, cols] = (gate * _sigmoid(gate) * up).astype(BF16)

    w = pl.BlockSpec((tn, D), lambda n, i: (n, 0))
    o = pl.BlockSpec((tm, tn), lambda n, i: (i, n))
    return pl.pallas_call(
        body, name="fwd_ffn_up", grid=(DFF // tn, T // tm),
        in_specs=[pl.BlockSpec((tm, D), lambda n, i: (i, 0)), w, w],
        out_specs=[o] * 3, out_shape=[SDS((T, DFF), BF16)] * 3,
        compiler_params=_params(2))(h2, wg_t, wu_t)


def _fwd_ffn_down_loss(act, w_d, x1, target, gain_final, tm):
    T = x1.shape[0]

    def body(act_ref, wd_ref, x1_ref, tgt_ref, g_ref, dx2_ref, loss_ref, dgain_ref):
        @pl.when(pl.program_id(0) == 0)
        def _():
            loss_ref[...] = jnp.zeros_like(loss_ref)
            dgain_ref[...] = jnp.zeros_like(dgain_ref)
        x2 = x1_ref[...] + _mm(act_ref[...], wd_ref[...])
        gain = g_ref[...]
        y = x2 * _rms(x2) * gain
        err = y - tgt_ref[...]
        loss_ref[...] += 0.5 * jnp.sum(jnp.mean(err * err, axis=-1, keepdims=True))
        dx2, dgain = _norm_bwd(err * (1.0 / D), x2, gain)
        dx2_ref[...] = dx2
        dgain_ref[...] += dgain

    act_spec = pl.BlockSpec((tm, D), lambda i: (i, 0))
    row = pl.BlockSpec((1, D), lambda i: (0, 0))
    return pl.pallas_call(
        body, name="fwd_ffn_down_loss", grid=(T // tm,),
        in_specs=[pl.BlockSpec((tm, DFF), lambda i: (i, 0)), _whole((DFF, D)), act_spec, act_spec, row],
        out_specs=[act_spec, pl.BlockSpec((8, D), lambda i: (0, 0)), row],
        out_shape=[SDS((T, D), F32), SDS((8, D), F32), SDS((1, D), F32)],
        compiler_params=_params(1))(act, w_d, x1, target, gain_final)


def _bwd_ffn_down(dx2, w_d, gate, up, tm, tn):
    T = dx2.shape[0]

    def body(dx_ref, wd_ref, gate_ref, up_ref, dgate_ref, dup_ref):
        dx = dx_ref[...].astype(BF16)
        for cols in _col_chunks(tn):
            dact = _mm_nt(dx, wd_ref[cols, :])
            gate = gate_ref[:, cols].astype(F32)
            sg = _sigmoid(gate)
            dgate_ref[:, cols] = (dact * up_ref[:, cols].astype(F32) * (sg * (1.0 + gate * (1.0 - sg)))).astype(BF16)
            dup_ref[:, cols] = (dact * gate * sg).astype(BF16)

    o = pl.BlockSpec((tm, tn), lambda n, i: (i, n))
    return pl.pallas_call(
        body, name="bwd_ffn_down", grid=(DFF // tn, T // tm),
        in_specs=[pl.BlockSpec((tm, D), lambda n, i: (i, 0)), pl.BlockSpec((tn, D), lambda n, i: (n, 0)), o, o],
        out_specs=[o] * 2, out_shape=[SDS((T, DFF), BF16)] * 2,
        compiler_params=_params(2))(dx2, w_d, gate, up)


def _bwd_ffn_up(dgate, dup, wg_t, wu_t, x1, dx2, gain_ffn, tm):
    T = x1.shape[0]

    def body(dg_ref, du_ref, wg_ref, wu_ref, x1_ref, dx2_ref, g_ref, dx1_ref, dgain_ref):
        @pl.when(pl.program_id(0) == 0)
        def _():
            dgain_ref[...] = jnp.zeros_like(dgain_ref)
        dh2 = _mm(dg_ref[...], wg_ref[...]) + _mm(du_ref[...], wu_ref[...])
        dx, dgain = _norm_bwd(dh2, x1_ref[...], g_ref[...])
        dx1_ref[...] = dx2_ref[...] + dx
        dgain_ref[...] += dgain

    wide = pl.BlockSpec((tm, DFF), lambda i: (i, 0))
    w = _whole((DFF, D))
    act = pl.BlockSpec((tm, D), lambda i: (i, 0))
    row = pl.BlockSpec((1, D), lambda i: (0, 0))
    return pl.pallas_call(
        body, name="bwd_ffn_up", grid=(T // tm,),
        in_specs=[wide, wide, w, w, act, act, row], out_specs=[act, row],
        out_shape=[SDS((T, D), F32), SDS((1, D), F32)],
        compiler_params=_params(1))(dgate, dup, wg_t, wu_t, x1, dx2, gain_ffn)


def _wgrad(a, b, *, name, groups, a_cols, b_cols, tt, a_index, b_index, o_index, out_shape, after):
    T = a.shape[0]
    nt = T // tt
    n_a = a.shape[1] // a_cols if groups == 1 else 1

    def body(a_ref, b_ref, after_ref, o_ref, acc_ref):
        del after_ref
        t = pl.program_id(2)

        @pl.when(t == 0)
        def _():
            acc_ref[...] = jnp.zeros_like(acc_ref)
        acc_ref[...] += _mm_tn(a_ref[...].astype(BF16), b_ref[...].astype(BF16))

        @pl.when(t == nt - 1)
        def _():
            o_ref[...] = acc_ref[...].astype(o_ref.dtype)

    return pl.pallas_call(
        body, name=name, grid=(groups, n_a, nt),
        in_specs=[pl.BlockSpec((tt, a_cols), a_index), pl.BlockSpec((None, tt, b_cols), b_index), HBM],
        out_specs=pl.BlockSpec((None, a_cols, b_cols), o_index),
        out_shape=SDS(out_shape, BF16),
        scratch_shapes=[pltpu.VMEM((a_cols, b_cols), F32)],
        compiler_params=_params(3))(a, b, after)


def _wgrad_dense(a, b, name, tt, after, a_cols=None):
    ka, nb = a.shape[1], b.shape[1]
    a_cols = ka if a_cols is None else a_cols
    out = _wgrad(a, b[None], name=name, groups=1, a_cols=a_cols, b_cols=nb, tt=tt,
                 a_index=lambda g, k, t: (t, k), b_index=lambda g, k, t: (0, t, 0),
                 o_index=lambda g, k, t: (k, 0, 0), out_shape=(ka // a_cols, a_cols, nb), after=after)
    return out.reshape(ka, nb)


def _bwd_merge(dx1, proj, ya, yp, yx, pooled, pscale, w_o, w_co, w_xo, w_pool, tm, after):
    T = dx1.shape[0]
    nt = T // tm

    def body(dx1_ref, ga_ref, gp_ref, gx_ref, ya_ref, yp_ref, yx_ref, pooled_ref, ps_ref, wo_ref, wco_ref, wxo_ref, wp_ref,
             after_ref, dgates_ref, dya_ref, dyx_ref, dza_ref, do_ref, dpooled_ref, dps_ref, dwp_ref, acc_ref):
        del after_ref

        @pl.when(pl.program_id(0) == 0)
        def _():
            dps_ref[...] = jnp.zeros_like(dps_ref)
            acc_ref[...] = jnp.zeros_like(acc_ref)
        dmerged = _mm_nt(dx1_ref[...].astype(BF16), wo_ref[...])
        scale = ps_ref[...]
        sa, sp, sx = (_sigmoid(r[...].astype(F32)) for r in (ga_ref, gp_ref, gx_ref))
        ya, yp_pre, yx = (r[...].astype(F32) for r in (ya_ref, yp_ref, yx_ref))
        dgates_ref[0] = (dmerged * ya * sa * (1.0 - sa)).astype(BF16)
        dgates_ref[1] = (dmerged * (yp_pre * scale) * sp * (1.0 - sp)).astype(BF16)
        dgates_ref[2] = (dmerged * yx * sx * (1.0 - sx)).astype(BF16)
        dya = (dmerged * sa).astype(BF16)
        dyx = (dmerged * sx).astype(BF16)
        dyp = dmerged * sp
        dyps = (dyp * scale).astype(BF16)
        dps_ref[...] += jnp.sum(dyp * yp_pre, axis=0, keepdims=True)
        dya_ref[...] = dya
        dyx_ref[...] = dyx
        dza_ref[...] = _mm_nt(dya, wco_ref[...]).astype(BF16)
        do_ref[...] = _mm_nt(dyx, wxo_ref[...]).astype(BF16)
        for g in range(NPOOL):
            cols = slice(g * HD, (g + 1) * HD)
            dpooled_ref[:, cols] = _mm_nt(dyps[:, cols], wp_ref[g]).astype(BF16)
            acc_ref[g] += _mm_tn(pooled_ref[:, cols], dyps[:, cols])

        @pl.when(pl.program_id(0) == nt - 1)
        def _():
            dwp_ref[...] = acc_ref[...].astype(BF16)

    tile = lambda s: pl.BlockSpec((None, tm, D), lambda i: (s, i, 0))
    row = pl.BlockSpec((1, D), lambda i: (0, 0))
    act = pl.BlockSpec((tm, D), lambda i: (i, 0))
    full = _whole((D, D))
    return pl.pallas_call(
        body, name="bwd_merge", grid=(T // tm,),
        in_specs=[act, tile(5), tile(6), tile(7), act, act, act, act, row, full, full, full,
                  _whole((NPOOL, HD, HD)), HBM],
        out_specs=[pl.BlockSpec((3, tm, D), lambda i: (0, i, 0))] + [act] * 5
        + [row, pl.BlockSpec((NPOOL, HD, HD), lambda i: (0, 0, 0))],
        out_shape=[SDS((NSPLIT, T, D), BF16)] + [SDS((T, D), BF16)] * 5 + [SDS((1, D), F32), SDS((NPOOL, HD, HD), BF16)],
        scratch_shapes=[pltpu.VMEM((NPOOL, HD, HD), F32)],
        compiler_params=_params(1))(dx1, proj, proj, proj, ya, yp, yx, pooled, pscale, w_o, w_co, w_xo, w_pool, after)


def _bwd_attn(dproj, proj, do, kv, memn, w_kv, mem, gain_mem, tm):
    T = do.shape[0]
    M = kv.shape[1]
    nt = T // tm

    def body(dproj_hbm, q_ref, do_ref, kv_ref, memn_ref, wkv_ref, mem_ref, gm_ref, dq_ref, dw_ref, dgain_ref, dkv_ref):
        del dproj_hbm

        @pl.when(pl.program_id(0) == 0)
        def _():
            dkv_ref[...] = jnp.zeros_like(dkv_ref)
        for h in range(NH):
            cols = slice(h * HD, (h + 1) * HD)
            q = q_ref[:, cols]
            do_h = do_ref[:, cols]
            p = _softmax_rows(_mm_nt(q, kv_ref[h]) * ATT_SCALE)
            dp = _mm_nt(do_h, kv_ref[NH + h])
            ds = (p * (dp - jnp.sum(dp * p, axis=-1, keepdims=True)) * ATT_SCALE).astype(BF16)
            dq_ref[:, cols] = _mm(ds, kv_ref[h]).astype(BF16)
            dkv_ref[h] += _mm_tn(ds, q)
            dkv_ref[NH + h] += _mm_tn(p.astype(BF16), do_h)

        @pl.when(pl.program_id(0) == nt - 1)
        def _():
            dmemn = jnp.zeros((M, D), F32)
            for j in range(2 * NH):
                dkv_j = dkv_ref[j].astype(BF16)
                dw_ref[j] = _mm_tn(memn_ref[...], dkv_j).astype(BF16)
                dmemn = dmemn + _mm_nt(dkv_j, wkv_ref[j])
            dgain_ref[...] = _norm_bwd(dmemn, mem_ref[...], gm_ref[...])[1]

    row = pl.BlockSpec((1, D), lambda i: (0, 0))
    return pl.pallas_call(
        body, name="bwd_attn", grid=(nt,),
        in_specs=[HBM, pl.BlockSpec((None, tm, D), lambda i: (4, i, 0)), pl.BlockSpec((tm, D), lambda i: (i, 0)),
                  _whole((2 * NH, M, HD)), _whole((M, D)), _whole((2 * NH, D, HD)), _whole((M, D)), row],
        out_specs=[pl.BlockSpec((None, tm, D), lambda i: (3, i, 0)),
                   pl.BlockSpec((2 * NH, D, HD), lambda i: (0, 0, 0)), row],
        out_shape=[SDS(dproj.shape, BF16), SDS((2 * NH, D, HD), BF16), SDS((1, D), F32)],
        scratch_shapes=[pltpu.VMEM((2 * NH, M, HD), F32)],
        input_output_aliases={0: 0},
        compiler_params=_params(1))(dproj, proj, do, kv, memn, w_kv, mem, gain_mem)


def _bwd_mix(dproj, proj, conv, dza, dpooled, cw0, cw1, cw2, tm, after):
    T = dza.shape[0]
    nt = T // tm

    def halo_after(split_or_none):
        idx = lambda i: jnp.minimum((i + 1) * (tm // HALO), T // HALO - 1)
        if split_or_none is None:
            return pl.BlockSpec((HALO, D), lambda i: (idx(i), 0))
        return pl.BlockSpec((None, HALO, D), lambda i: (split_or_none, idx(i), 0))

    def body(dproj_hbm, b_ref, c_ref, ua_ref, conv_ref, dza_ref, dpo_ref, bn_ref, dzan_ref, dpon_ref,
             cw0_ref, cw1_ref, cw2_ref, after_ref, dabcu_ref, dcw_ref):
        del dproj_hbm, after_ref
        i = pl.program_id(0)

        @pl.when(i == 0)
        def _():
            dcw_ref[...] = jnp.zeros_like(dcw_ref)
        keep_next = jnp.where(i < nt - 1, 1.0, 0.0).astype(F32)
        dza = dza_ref[...].astype(F32)
        c = c_ref[...].astype(F32)
        ua = ua_ref[...].astype(F32)
        dconv = dza * b_ref[...].astype(F32)
        dconv_n = dzan_ref[...].astype(F32) * bn_ref[...].astype(F32) * keep_next
        ext = jnp.concatenate([dconv, dconv_n], axis=0)
        dconv_1, dconv_2 = _shift_up(ext, 1)[:tm], _shift_up(ext, 2)[:tm]
        dcu = cw2_ref[...] * dconv + cw1_ref[...] * dconv_1 + cw0_ref[...] * dconv_2
        dabcu_ref[0] = (dza * conv_ref[...].astype(F32)).astype(BF16)
        dabcu_ref[1] = (dcu * ua).astype(BF16)
        dabcu_ref[2] = (dcu * c).astype(BF16)

        cu = c * ua
        dcw_ref[2:3, :] += jnp.sum(dconv * cu, axis=0, keepdims=True)
        dcw_ref[1:2, :] += jnp.sum(dconv_1 * cu, axis=0, keepdims=True)
        dcw_ref[0:1, :] += jnp.sum(dconv_2 * cu, axis=0, keepdims=True)

        dpo = dpo_ref[...].astype(F32)
        ext_dpo = jnp.concatenate([dpo, dpon_ref[...].astype(F32) * keep_next], axis=0)
        pos = i * tm + lax.broadcasted_iota(jnp.int32, (tm + HALO, HD), 0)
        for g in range(NPOOL):
            cols = slice(g * HD, (g + 1) * HD)
            s = ext_dpo[:, cols] / jnp.minimum(pos + 1, 2 << g).astype(F32)
            for k in range(g + 1):
                s = s + _shift_up(s, 1 << k)
            dabcu_ref[3, :, cols] = (s[:tm] - dpo[:, cols]).astype(BF16)

    tile = lambda s: pl.BlockSpec((None, tm, D), lambda i: (s, i, 0))
    act = pl.BlockSpec((tm, D), lambda i: (i, 0))
    row = pl.BlockSpec((1, D), lambda i: (0, 0))
    return pl.pallas_call(
        body, name="bwd_mix", grid=(nt,),
        in_specs=[HBM, tile(0), tile(1), tile(2), act, act, act, halo_after(0), halo_after(None), halo_after(None),
                  row, row, row, HBM],
        out_specs=[pl.BlockSpec((4, tm, D), lambda i: (1, i, 0)), pl.BlockSpec((8, D), lambda i: (0, 0))],
        out_shape=[SDS(dproj.shape, BF16), SDS((8, D), F32)],
        input_output_aliases={0: 0},
        compiler_params=_params(1))(dproj, proj, proj, proj, conv, dza, dpooled, proj, dza, dpooled, cw0, cw1, cw2, after)


def _bwd_proj(dproj, w_in_g, x, dx1, gain, tm, after):
    T = x.shape[0]

    def body(dp_ref, w_ref, x_ref, dx1_ref, g_ref, after_ref, dx_ref, dgain_ref, acc_ref):
        del after_ref
        i, s = pl.program_id(0), pl.program_id(1)

        @pl.when((i == 0) & (s == 0))
        def _():
            dgain_ref[...] = jnp.zeros_like(dgain_ref)

        @pl.when(s == 0)
        def _():
            acc_ref[...] = jnp.zeros_like(acc_ref)
        acc_ref[...] += _mm_nt(dp_ref[...], w_ref[...])

        @pl.when(s == NSPLIT - 1)
        def _():
            dx, dgain = _norm_bwd(acc_ref[...], x_ref[...], g_ref[...])
            dx_ref[...] = dx1_ref[...] + dx
            dgain_ref[...] += dgain

    act = pl.BlockSpec((tm, D), lambda i, s: (i, 0))
    row = pl.BlockSpec((1, D), lambda i, s: (0, 0))
    return pl.pallas_call(
        body, name="bwd_proj", grid=(T // tm, NSPLIT),
        in_specs=[pl.BlockSpec((None, tm, D), lambda i, s: (s, i, 0)),
                  pl.BlockSpec((None, D, D), lambda i, s: (_slot_group(s), 0, 0)), act, act, row, HBM],
        out_specs=[act, row], out_shape=[SDS((T, D), F32), SDS((1, D), F32)],
        scratch_shapes=[pltpu.VMEM((tm, D), F32)],
        compiler_params=_params(2))(dproj, w_in_g, x, dx1, gain, after)


def _adamw_math(w, g, m, v):
    m = ADAM_B1 * m + (1.0 - ADAM_B1) * g
    v = ADAM_B2 * v + (1.0 - ADAM_B2) * (g * g)
    m_hat = m / (1.0 - ADAM_B1 ** ADAM_STEP)
    v_hat = v / (1.0 - ADAM_B2 ** ADAM_STEP)
    delta = -ADAM_LR * (m_hat / (jnp.sqrt(v_hat) + ADAM_EPS) + ADAM_WD * w)
    return delta, m, v


def _row_tile(rows):
    return 256 if rows % 256 == 0 else rows


def _sum_parts(parts, name):
    n_parts, rows, cols = parts.shape
    tr = _row_tile(rows)

    def body(p_ref, g_ref):
        g = p_ref[0].astype(F32)
        for k in range(1, n_parts):
            g = g + p_ref[k].astype(F32)
        g_ref[...] = g

    blk = pl.BlockSpec((tr, cols), lambda i: (i, 0))
    return pl.pallas_call(
        body, name=name, grid=(rows // tr,),
        in_specs=[pl.BlockSpec((n_parts, tr, cols), lambda i: (0, i, 0))], out_specs=blk,
        out_shape=SDS((rows, cols), F32), compiler_params=_params(1))(parts)


def _adamw(ws, gs, ms, vs, name, from_parts, steps):
    n = len(ws)

    def body(*refs):
        for a in range(n):
            w_ref, g_ref, m_ref, v_ref = refs[4 * a:4 * a + 4]
            go_ref, d_ref, mo_ref, vo_ref = refs[4 * n + 4 * a:4 * n + 4 * a + 4]
            if from_parts:
                g = g_ref[0].astype(F32)
                for k in range(1, g_ref.shape[0]):
                    g = g + g_ref[k].astype(F32)
            else:
                g = g_ref[...]
            go_ref[...] = g
            d_ref[...], mo_ref[...], vo_ref[...] = _adamw_math(w_ref[...], g, m_ref[...], v_ref[...])

    in_specs, out_specs, out_shape, operands = [], [], [], []
    for w, g, m, v in zip(ws, gs, ms, vs):
        rows, cols = w.shape
        blk = pl.BlockSpec((rows // steps, cols), lambda i: (i, 0))
        g_spec = pl.BlockSpec((g.shape[0], rows // steps, cols), lambda i: (0, i, 0)) if from_parts else blk
        in_specs += [blk, g_spec, blk, blk]
        out_specs += [blk] * 4
        out_shape += [SDS((rows, cols), F32)] * 4
        operands += [w, g, m, v]
    outs = pl.pallas_call(body, name=name, grid=(steps,), in_specs=in_specs, out_specs=out_specs, out_shape=out_shape,
                          compiler_params=_params(1))(*operands)
    return [outs[4 * a:4 * a + 4] for a in range(n)]


def _peer(k, x, y, c):
    return ((1 - x) if k & 4 else x, (1 - y) if k & 2 else y, (1 - c) if k & 1 else c)


SEM = pl.BlockSpec(memory_space=pltpu.SEMAPHORE)
IN_HBM = pl.BlockSpec(memory_space=pltpu.HBM)
DATAFLOW = pltpu.SideEffectType.DATAFLOW_SIDE_EFFECTING
TOKEN_SHAPE = (8, 128)


OTHER_CHIPS = (2, 4, 6)


def _place(x, y, c):
    return 4 * x + 2 * y + c


def _plan_gather_chips(n, ks=(1,) + OTHER_CHIPS):
    def plan(refs, x, y, c, arriving):
        out = []
        for a in range(n):
            for k in ks:
                there = _place(*_peer(k, x, y, c))
                out.append((refs[a], refs[n + a].at[there if arriving else _place(x, y, c)], k))
        return out
    return plan, n * len(ks)


def _plan_gather_sibling(n, ks=OTHER_CHIPS):
    def plan(refs, x, y, c, arriving):
        out = []
        for a in range(n):
            for k in ks:
                px, py, pc = _peer(k, x, y, c)
                mine, theirs = _place(px, py, pc), _place(px, py, 1 - pc)
                out.append((refs[a].at[mine], refs[a].at[theirs if arriving else mine], 1))
        return out
    return plan, n * len(ks)


def _plan_pair():
    def plan(refs, x, y, c, arriving):
        return [(refs[0], refs[1].at[(1 - c) if arriving else c], 1)]
    return plan, 1


def _plan_far_chip():
    def plan(refs, x, y, c, arriving):
        return [(refs[0], refs[1].at[c], 6)]
    return plan, 1


def _plan_far_sibling():
    def plan(refs, x, y, c, arriving):
        return [(refs[0].at[c], refs[0].at[(1 - c) if arriving else c], 1)]
    return plan, 1


def _plan_scatter_sibling(n):
    def plan(refs, x, y, c, arriving):
        out = []
        for a in range(n):
            for q in range(4):
                out.append((refs[a].at[2 * q + (1 - c)], refs[n + a].at[q], 1))
        return out
    return plan, n * 4


def _plan_scatter_chips(n):
    def plan(refs, x, y, c, arriving):
        out = []
        for a in range(n):
            for k in OTHER_CHIPS:
                px, py, _ = _peer(k, x, y, c)
                out.append((refs[a].at[2 * px + py], refs[n + a].at[(2 * px + py) if arriving else (2 * x + y)], k))
        return out
    return plan, n * 3


def _remote(src, dst, send_sems, recv_sems, i, k):
    x, y, c = (lax.axis_index(n) for n in AXES)
    return pltpu.make_async_remote_copy(src_ref=src, dst_ref=dst, send_sem=send_sems.at[i], recv_sem=recv_sems.at[i],
                                        device_id=_peer(k, x, y, c), device_id_type=pl.DeviceIdType.MESH)


def _copies_start(groups, name, after):
    ng = len(groups)
    total = sum(len(bufs) for bufs, _ in groups)

    def body(*refs):
        sems = refs[1 + total:1 + total + 2 * ng]
        x, y, c = (lax.axis_index(n) for n in AXES)
        off = 1
        for gi, (bufs, (plan, _)) in enumerate(groups):
            for i, (src, dst, k) in enumerate(plan(refs[off:off + len(bufs)], x, y, c, False)):
                _remote(src, dst, sems[2 * gi], sems[2 * gi + 1], i, k).start()
            off += len(bufs)
        refs[-1][...] = jnp.zeros(TOKEN_SHAPE, F32)

    sem_shapes = [pltpu.SemaphoreType.DMA((count,)) for _, (_, count) in groups for _ in range(2)]
    flat = [b for bufs, _ in groups for b in bufs]
    outs = pl.pallas_call(
        body, name=name,
        in_specs=[HBM] + [IN_HBM] * total,
        out_specs=[SEM] * (2 * ng) + [IN_HBM] * total + [pl.BlockSpec(memory_space=pltpu.VMEM)],
        out_shape=sem_shapes + [pltpu.HBM(b.shape, b.dtype) for b in flat] + [SDS(TOKEN_SHAPE, F32)],
        input_output_aliases={1 + i: 2 * ng + i for i in range(total)},
        compiler_params=pltpu.CompilerParams(has_side_effects=DATAFLOW),
    )(after, *[pltpu.with_memory_space_constraint(b, pltpu.HBM) for b in flat])
    handles, off = [], 2 * ng
    for gi, (bufs, _) in enumerate(groups):
        handles.append((outs[2 * gi], outs[2 * gi + 1], list(outs[off:off + len(bufs)])))
        off += len(bufs)
    return handles, outs[-1]


def _copies_wait_start(handle, plan, pass_on, more, name, after):
    send_sems, recv_sems, bufs = handle
    n = len(bufs)
    idx, (pass_plan, pass_count) = pass_on
    total = sum(len(b) for b, _ in more)
    ng = 1 + len(more)

    def body(*refs):
        x, y, c = (lax.axis_index(a) for a in AXES)
        waited = refs[1:1 + n]
        outs = refs[3 + n + total:]
        new_sems = outs[n + total:n + total + 2 * ng]
        for i, (src, dst, k) in enumerate(plan[0](waited, x, y, c, True)):
            copy = _remote(src, dst, refs[1 + n + total], refs[2 + n + total], i, k)
            copy.wait_send()
            copy.wait_recv()
        for i, (src, dst, k) in enumerate(pass_plan([waited[j] for j in idx], x, y, c, False)):
            _remote(src, dst, new_sems[0], new_sems[1], i, k).start()
        off = 1 + n
        for gi, (b, (p, _)) in enumerate(more):
            for i, (src, dst, k) in enumerate(p(refs[off:off + len(b)], x, y, c, False)):
                _remote(src, dst, new_sems[2 + 2 * gi], new_sems[3 + 2 * gi], i, k).start()
            off += len(b)
        outs[-1][...] = jnp.zeros(TOKEN_SHAPE, F32)

    flat = list(bufs) + [a for b, _ in more for a in b]
    sem_shapes = [pltpu.SemaphoreType.DMA((count,)) for count in [pass_count] + [cnt for _, (_, cnt) in more] for _ in range(2)]
    outs = pl.pallas_call(
        body, name=name,
        in_specs=[HBM] + [IN_HBM] * (n + total) + [SEM, SEM],
        out_specs=[IN_HBM] * (n + total) + [SEM] * (2 * ng) + [pl.BlockSpec(memory_space=pltpu.VMEM)],
        out_shape=[pltpu.HBM(b.shape, b.dtype) for b in flat] + sem_shapes + [SDS(TOKEN_SHAPE, F32)],
        input_output_aliases={1 + i: i for i in range(n + total)},
        compiler_params=pltpu.CompilerParams(has_side_effects=DATAFLOW),
    )(after, *[pltpu.with_memory_space_constraint(b, pltpu.HBM) for b in flat], send_sems, recv_sems)
    thru = list(outs[:n])
    sems_out = outs[n + total:n + total + 2 * ng]
    handles = [(sems_out[0], sems_out[1], [thru[j] for j in idx])]
    off = n
    for gi, (b, _) in enumerate(more):
        handles.append((sems_out[2 + 2 * gi], sems_out[3 + 2 * gi], list(outs[off:off + len(b)])))
        off += len(b)
    return thru, handles, outs[-1]


def _copies_wait(handle, plan, name, after):
    send_sems, recv_sems, bufs = handle
    n = len(bufs)

    def body(*refs):
        x, y, c = (lax.axis_index(a) for a in AXES)
        for i, (src, dst, k) in enumerate(plan[0](refs[:n], x, y, c, True)):
            copy = _remote(src, dst, refs[n], refs[n + 1], i, k)
            copy.wait_send()
            copy.wait_recv()

    return pl.pallas_call(
        body, name=name,
        in_specs=[IN_HBM] * n + [SEM, SEM, HBM], out_specs=[IN_HBM] * n,
        out_shape=[pltpu.HBM(b.shape, b.dtype) for b in bufs],
        input_output_aliases={i: i for i in range(n)},
        compiler_params=pltpu.CompilerParams(has_side_effects=DATAFLOW),
    )(*bufs, send_sems, recv_sems, after)


def _pair_sums(mine, theirs, c, chip, name):
    n = len(mine)

    def body(where_ref, *refs):
        q = pl.program_id(0)
        for a in range(n):
            total = (refs[a][...].astype(F32) + refs[n + a][...].astype(F32)).astype(BF16)
            refs[2 * n + a][...] = total

            @pl.when(q == where_ref[1])
            def _():
                refs[3 * n + a][...] = total

    block = lambda t: (None,) + t.shape[1:]
    zeros = lambda t: (0,) * (t.ndim - 1)
    outs = pl.pallas_call(
        body, name=name,
        grid_spec=pltpu.PrefetchScalarGridSpec(
            num_scalar_prefetch=1, grid=(4,),
            in_specs=[pl.BlockSpec(block(t), lambda q, w, z=zeros(t): (2 * q + w[0],) + z) for t in theirs]
            + [pl.BlockSpec(block(t), lambda q, w, z=zeros(t): (q,) + z) for t in theirs],
            out_specs=[pl.BlockSpec(block(t), lambda q, w, z=zeros(t): (q,) + z) for t in theirs]
            + [pl.BlockSpec(block(t), lambda q, w, z=zeros(t): (w[1],) + z) for t in theirs]),
        out_shape=[SDS(t.shape, BF16) for t in theirs] * 2,
        compiler_params=_params(1))(jnp.stack([c, chip]).astype(jnp.int32), *mine, *theirs)
    return list(outs[:n]), list(outs[n:])


def _local_step(x, mem, target, gains, get, put, flush, tm_huge=2048, tm_big=1024, tm_mid=512, tm_small=256):
    g_mix, pscale, g_mem, g_ffn, g_fin = gains
    T = x.shape[0]
    tm_huge, tm_big, tm_mid, tm_small = min(tm_huge, T), min(tm_big, T), min(tm_mid, T), min(tm_small, T)
    tn = DFF // 2

    w_pair, w_ids, p_ids = get("in_pair", x)
    proj, h = _fwd_proj(x, g_mix, w_pair, w_ids, p_ids, tm_huge)
    w_near, w_ids, p_ids = get("in_near", h)
    proj = _fwd_proj_more(h, w_near, proj, w_ids, p_ids, tm_huge, "fwd_proj_near")
    w_far, w_ids, p_ids = get("in_far", proj)
    proj = _fwd_proj_more(h, w_far, proj, w_ids, p_ids, tm_huge, "fwd_proj_far")
    w_in = get("in_whole", (w_pair, w_near, w_far))
    cw0, cw1, cw2, w_co, w_pool, w_kv, w_xo, w_o = get("mix", proj)
    za, conv, pooled, ya, yp, kv, memn = _fwd_mix(proj, cw0, cw1, cw2, w_co, w_pool, mem, g_mem, w_kv, tm_mid)
    o, yx, merged, x1, h2 = _fwd_merge(proj, ya, yp, x, kv, w_xo, w_o, pscale, g_ffn, tm_mid)
    wg_t, wu_t = get("gate_up", x1)
    get("down", x1, early=True)
    gate, up, act = _fwd_ffn_up(h2, wg_t, wu_t, tm_big, tn)
    (w_d,) = get("down", gate)
    dx2, loss, dg_fin = _fwd_ffn_down_loss(act, w_d, x1, target, g_fin, tm_mid)

    dgate, dup = _bwd_ffn_down(dx2, w_d, gate, up, tm_big, tn)
    dx1, dg_ffn = _bwd_ffn_up(dgate, dup, wg_t, wu_t, x1, dx2, g_ffn, tm_small)
    dw_d = _wgrad_dense(act, dx2, "wgrad_down", tm_big, g_mix)
    dwg_t = _wgrad_dense(dgate, h2, "wgrad_gate", tm_big, g_mix)
    dwu_t = _wgrad_dense(dup, h2, "wgrad_up", tm_big, g_mix)
    token = put("ffn", (dwg_t, dwu_t, dw_d))

    dproj, dya, dyx, dza, do, dpooled, dpscale, dw_pool = _bwd_merge(
        dx1, proj, ya, yp, yx, pooled, pscale, w_o, w_co, w_xo, w_pool, tm_mid, token)
    token = flush(dya)
    dw_o = _wgrad_dense(merged, dx1, "wgrad_out", tm_big, token)
    dw_co = _wgrad_dense(za, dya, "wgrad_conv_out", tm_big, token)
    dw_xo = _wgrad_dense(o, dyx, "wgrad_xattn_out", tm_big, token)
    dproj, dw_kv, dg_mem = _bwd_attn(dproj, proj, do, kv, memn, w_kv, mem, g_mem, tm_big)
    token = put("mix", (dw_co, dw_xo, dw_o, dw_pool, dw_kv))

    dproj, dcw = _bwd_mix(dproj, proj, conv, dza, dpooled, cw0, cw1, cw2, tm_mid, token)
    token = flush(dcw)
    dw_in = _wgrad(h, dproj, name="wgrad_in", groups=NSPLIT, a_cols=D, b_cols=D, tt=tm_huge,
                   a_index=lambda g, k, t: (t, 0), b_index=lambda g, k, t: (g, t, 0),
                   o_index=lambda g, k, t: (_slot_group(g), 0, 0), out_shape=(NSPLIT, D, D), after=token)
    token = flush(put("in", (dw_in,)))
    grad_x, dg_mix = _bwd_proj(dproj, w_in, x, dx1, g_mix, tm_big, token)

    small = jnp.concatenate([dg_mix, dpscale, dg_mem, dg_ffn, dg_fin, dcw[0:3], loss], axis=0)
    return grad_x, small


def kernel(x, mem, norm_mix, w_in, conv_w, w_conv_out, w_pool, pool_scale, norm_mem, w_kv, w_xattn_out, w_out, norm_ffn, w_gate, w_up, w_down, norm_final, loss_target, m_norm_mix, m_w_in, m_conv_w, m_w_conv_out, m_w_pool, m_pool_scale, m_norm_mem, m_w_kv, m_w_xattn_out, m_w_out, m_norm_ffn, m_w_gate, m_w_up, m_w_down, m_norm_final, v_norm_mix, v_w_in, v_conv_w, v_w_conv_out, v_w_pool, v_pool_scale, v_norm_mem, v_w_kv, v_w_xattn_out, v_w_out, v_norm_ffn, v_w_gate, v_w_up, v_w_down, v_norm_final):
    T = x.shape[1]
    rows = D // NDEV
    ffb = DFF // NDEV
    prow = HD // NDEV
    me = 4 * lax.axis_index("x") + 2 * lax.axis_index("y") + lax.axis_index("c")

    shards = [w_in[0].astype(BF16), w_conv_out[0].astype(BF16), w_xattn_out[0].astype(BF16), w_out[0].astype(BF16),
              w_pool[0].astype(BF16).reshape(NPOOL * prow, HD), w_kv[0].astype(BF16),
              w_gate[0].T.astype(BF16), w_up[0].T.astype(BF16), w_down[0].astype(BF16),
              jnp.pad(conv_w[0], ((0, 5), (0, 0)))]

    cx, cy, cc = (lax.axis_index(n) for n in AXES)
    chip = 2 * cx + cy

    def land(own, index, slots):
        return lax.dynamic_update_index_in_dim(lax.empty((slots,) + own.shape, own.dtype), own, index, 0)

    needed = ["in_pair", "in_near", "in_far", "mix", "gate_up", "down"]
    members = {"mix": [9, 1, 4, 5, 2, 3], "gate_up": [6, 7], "down": [8]}
    near = (2, 4)
    plans = {"in_pair": _plan_pair(), "in_near": _plan_gather_chips(1, near), "in_far": _plan_far_chip()}
    plans.update({n: _plan_gather_chips(len(members[n])) for n in members})
    g_bufs = {"in_pair": [shards[0], land(shards[0], cc, 2)],
              "in_near": [w_in[0].astype(BF16), lax.empty((NDEV, D, D), BF16)],
              "in_far": [w_in[0].astype(BF16), lax.empty((2, D, D), BF16)]}
    g_bufs.update({n: [shards[i] for i in members[n]] + [land(shards[i], me, NDEV) for i in members[n]] for n in members})
    first_handles, _ = _copies_start([(g_bufs[n], plans[n]) for n in needed[:2]], "gather_start", x)
    g_handles = dict(zip(needed[:2], first_handles))
    pair_ids = jnp.array([0, 1], jnp.int32)

    on_last_leg = {}

    def get(group, after, early=False):
        if group == "in_whole":
            w_pair, w_near, w_far = after
            w_whole = lax.dynamic_update_slice_in_dim(w_near, w_pair, 2 * chip, 0)
            return lax.dynamic_update_slice_in_dim(w_whole, w_far, 2 * (3 - chip), 0)
        if group == "in_pair":
            bufs = _copies_wait(g_handles[group], plans[group], "gather_wait_" + group, after)
            return bufs[1], pair_ids, (2 * chip + pair_ids).astype(jnp.int32)
        if group not in on_last_leg:
            n_bufs = len(g_bufs[group])
            landed = list(range(n_bufs // 2, n_bufs))
            if group == "in_near":
                plan, more = _plan_gather_sibling(1, near), [(g_bufs[n], plans[n]) for n in needed[2:]]
            elif group == "in_far":
                plan, more = _plan_far_sibling(), []
            else:
                plan, more = _plan_gather_sibling(n_bufs // 2), []
            _, handles, token = _copies_wait_start(g_handles[group], plans[group], (landed, plan), more,
                                                   "gather_pass_" + group, after)
            g_handles.update(zip(needed[2:], handles[1:]))
            on_last_leg[group] = (handles[0], plan, token)
        if early:
            return None
        handle, plan, token = on_last_leg[group]
        got = _copies_wait(handle, plan, "gather_passed_" + group, after if group in ("gate_up", "down") else token)
        if group == "in_near":
            groups = jnp.stack([me ^ k for k in (2, 3, 4, 5)]).astype(jnp.int32)
            return got[0], groups, groups
        if group == "in_far":
            return got[0], pair_ids, (2 * (3 - chip) + pair_ids).astype(jnp.int32)
        if group == "mix":
            cw_g, w_co_g, w_pool_g, w_kv_g, w_xo_g, w_o_g = got
            cw_full = cw_g.transpose(1, 0, 2).reshape(8, D)
            w_pool_full = w_pool_g.reshape(NDEV, NPOOL, prow, HD).transpose(1, 0, 2, 3).reshape(NPOOL, HD, HD)
            return (cw_full[0:1], cw_full[1:2], cw_full[2:3], w_co_g.reshape(D, D), w_pool_full, w_kv_g,
                    w_xo_g.reshape(D, D), w_o_g.reshape(D, D))
        return [g.reshape(DFF, D) for g in got]

    started = {}

    def put(group, grads):
        if group == "ffn":
            sends = [g.reshape(NDEV, ffb, D) for g in grads]
        elif group == "mix":
            dw_co, dw_xo, dw_o, dw_pool, dw_kv = grads
            sends = [dw_co.reshape(NDEV, rows, D), dw_xo.reshape(NDEV, rows, D), dw_o.reshape(NDEV, rows, D),
                     dw_pool.reshape(NPOOL, NDEV, prow, HD).transpose(1, 0, 2, 3).reshape(NDEV, NPOOL * prow, HD), dw_kv]
        else:
            sends = list(grads)
        n = len(sends)
        halves = [lax.empty((4,) + s.shape[1:], s.dtype) for s in sends]
        (handle,), token = _copies_start([(sends + halves, _plan_scatter_sibling(n))], "scatter_swap_" + group, norm_mix)
        swapping.append((group, handle, n))
        return token

    swapping = []

    def flush(after):
        group, handle, n = swapping.pop()
        bufs = _copies_wait(handle, _plan_scatter_sibling(n), "scatter_swapped_" + group, after)
        sums, lands = _pair_sums(bufs[:n], bufs[n:], cc, chip, "pair_sums_" + group)
        (handle,), token = _copies_start([(sums + lands, _plan_scatter_chips(n))], "scatter_start_" + group, norm_mix)
        started[group] = (handle, _plan_scatter_chips(n))
        return token

    def take(group, after):
        handle, plan = started[group]
        return _copies_wait(handle, plan, "scatter_wait_" + group, after)[len(handle[2]) // 2:]

    gains = (norm_mix, pool_scale, norm_mem, norm_ffn, norm_final.reshape(1, D))
    grad_x, small = _local_step(x[0], mem[0], loss_target[0], gains, get, put, flush)

    everyone = _plan_gather_chips(1, tuple(range(1, NDEV)))
    (small_handle,), token = _copies_start([([small, land(small, me, NDEV)], everyone)], "small_start", norm_mix)

    res = {}

    def update(group, names, ws, gs, ms, vs, from_parts, steps, transposed=()):
        view = lambda a, name: a[0].T if name in transposed else a
        flat = [[view(a, name).reshape(g.shape[-2:]) for a in (w, m, v)] for name, w, g, m, v in zip(names, ws, gs, ms, vs)]
        outs = _adamw([f[0] for f in flat], gs, [f[1] for f in flat], [f[2] for f in flat], "adamw_" + group,
                      from_parts, steps)
        for name, w, four in zip(names, ws, outs):
            res[name] = [(o.T if name in transposed else o).reshape(w.shape) for o in four]

    p_g, p_u, p_d = take("ffn", token)
    update("ffn", ["w_gate", "w_up", "w_down"], [w_gate, w_up, w_down], [p_g, p_u, p_d],
           [m_w_gate, m_w_up, m_w_down], [v_w_gate, v_w_up, v_w_down], True, 2, transposed=("w_gate", "w_up"))

    small_all = _copies_wait(small_handle, everyone, "small_wait", res["w_down"][1])[1]
    small_sum = _sum_parts(small_all, "sum_small")
    loss = small_sum[8, 0]
    g_cw = lax.dynamic_slice_in_dim(small_sum[5:8], me * rows, rows, axis=1)
    update("replicated", ["norm_mix", "pool_scale", "norm_mem", "norm_ffn", "norm_final", "conv_w"],
           [norm_mix, pool_scale, norm_mem, norm_ffn, norm_final, conv_w], [small_sum[k:k + 1] for k in range(5)] + [g_cw],
           [m_norm_mix, m_pool_scale, m_norm_mem, m_norm_ffn, m_norm_final, m_conv_w],
           [v_norm_mix, v_pool_scale, v_norm_mem, v_norm_ffn, v_norm_final, v_conv_w], False, 1)

    p_co, p_xo, p_o, p_pool, p_kv = take("mix", res["conv_w"][1])
    update("mix", ["w_conv_out", "w_xattn_out", "w_out", "w_pool", "w_kv"], [w_conv_out, w_xattn_out, w_out, w_pool, w_kv],
           [p_co, p_xo, p_o, p_pool, p_kv], [m_w_conv_out, m_w_xattn_out, m_w_out, m_w_pool, m_w_kv],
           [v_w_conv_out, v_w_xattn_out, v_w_out, v_w_pool, v_w_kv], True, 2)
    (p_in,) = take("in", res["w_out"][1])
    update("in", ["w_in"], [w_in], [p_in], [m_w_in], [v_w_in], True, 4)
    order = ["norm_mix", "w_in", "conv_w", "w_conv_out", "w_pool", "pool_scale", "norm_mem", "w_kv", "w_xattn_out", "w_out",
             "norm_ffn", "w_gate", "w_up", "w_down", "norm_final"]
    return (loss, grad_x[None], *[res[n][0] for n in order], *[res[n][1] for n in order],
            *[res[n][2] for n in order], *[res[n][3] for n in order])
```

```python
import jax
import jax.numpy as jnp
from jax import lax
from jax.experimental import pallas as pl
from jax.experimental.pallas import tpu as pltpu

F32 = jnp.float32
BF16 = jnp.bfloat16
SDS = jax.ShapeDtypeStruct

AXES = ("x", "y", "c")
NDEV = 8
D = 1024
NSPLIT = 8
NH = 4
HD = D // NH
NPOOL = 4
DFF = 2816
EPS = 1e-6
ATT_SCALE = HD ** -0.5
HALO = 16


def _slot_group(s):
    return jnp.where(s < 3, s + 5, jnp.where(s == 3, 4, s - 4))


ADAM_LR = 0.001
ADAM_B1 = 0.9
ADAM_B2 = 0.999
ADAM_EPS = 1e-08
ADAM_WD = 0.01
ADAM_STEP = 10

V7X_VMEM_BYTES = 64 * 1024 * 1024
VMEM_LIMIT = V7X_VMEM_BYTES - 8 * 1024 * 1024
HBM = pl.BlockSpec(memory_space=pl.ANY)


def _whole(shape):
    return pl.BlockSpec(shape, lambda *_: (0,) * len(shape), pipeline_mode=pl.Buffered(1))


def _params(n_grid):
    return pltpu.CompilerParams(dimension_semantics=("arbitrary",) * n_grid, vmem_limit_bytes=VMEM_LIMIT)


def _mm(a, b):
    return jnp.dot(a, b, preferred_element_type=F32)


def _mm_nt(a, b):
    return lax.dot_general(a, b, (((1,), (1,)), ((), ())), preferred_element_type=F32)


def _mm_tn(a, b):
    return lax.dot_general(a, b, (((0,), (0,)), ((), ())), preferred_element_type=F32)


def _sigmoid(x):
    return 1.0 / (1.0 + jnp.exp(-x))


def _rms(x):
    return lax.rsqrt(jnp.mean(x * x, axis=-1, keepdims=True) + EPS)


def _norm_bwd(dh, x, gain):
    r = _rms(x)
    xh = x * r
    dxh = dh * gain
    dx = r * (dxh - xh * jnp.mean(dxh * xh, axis=-1, keepdims=True))
    return dx, jnp.sum(dh * xh, axis=0, keepdims=True)


def _col_chunks(n, width=512):
    return [slice(c, min(c + width, n)) for c in range(0, n, width)]


def _shift_down(v, k):
    return pltpu.roll(v, k, 0)


def _shift_up(v, k):
    return pltpu.roll(v, v.shape[0] - k, 0)


def _fwd_proj(x, gain, w_blocks, w_ids, p_ids, tm):
    T = x.shape[0]

    def body(w_ids_ref, p_ids_ref, x_ref, g_ref, w_ref, proj_ref, h_ref):
        del w_ids_ref, p_ids_ref

        @pl.when(pl.program_id(1) == 0)
        def _():
            xf = x_ref[...]
            h_ref[...] = (xf * _rms(xf) * g_ref[...]).astype(BF16)
        proj_ref[...] = _mm(h_ref[...], w_ref[...]).astype(BF16)

    return pl.pallas_call(
        body, name="fwd_proj",
        grid_spec=pltpu.PrefetchScalarGridSpec(
            num_scalar_prefetch=2, grid=(T // tm, w_ids.shape[0]),
            in_specs=[pl.BlockSpec((tm, D), lambda i, j, w, p: (i, 0)), pl.BlockSpec((1, D), lambda i, j, w, p: (0, 0)),
                      pl.BlockSpec((None, D, D), lambda i, j, w, p: (w[j], 0, 0))],
            out_specs=[pl.BlockSpec((None, tm, D), lambda i, j, w, p: (p[j], i, 0)),
                       pl.BlockSpec((tm, D), lambda i, j, w, p: (i, 0))]),
        out_shape=[SDS((NSPLIT, T, D), BF16), SDS((T, D), BF16)],
        compiler_params=_params(2))(w_ids, p_ids, x, gain, w_blocks)


def _fwd_proj_more(h, w_blocks, proj, w_ids, p_ids, tm, name):
    T = h.shape[0]

    def body(w_ids_ref, p_ids_ref, h_ref, w_ref, proj_hbm, proj_ref):
        del w_ids_ref, p_ids_ref, proj_hbm
        proj_ref[...] = _mm(h_ref[...], w_ref[...]).astype(BF16)

    return pl.pallas_call(
        body, name=name,
        grid_spec=pltpu.PrefetchScalarGridSpec(
            num_scalar_prefetch=2, grid=(T // tm, w_ids.shape[0]),
            in_specs=[pl.BlockSpec((tm, D), lambda i, j, w, p: (i, 0)),
                      pl.BlockSpec((None, D, D), lambda i, j, w, p: (w[j], 0, 0)), HBM],
            out_specs=pl.BlockSpec((None, tm, D), lambda i, j, w, p: (p[j], i, 0))),
        out_shape=SDS(proj.shape, BF16), input_output_aliases={4: 0},
        compiler_params=_params(2))(w_ids, p_ids, h, w_blocks, proj)


def _halo_before(split, tm):
    return pl.BlockSpec((None, HALO, D), lambda i: (split, jnp.maximum(i * (tm // HALO) - 1, 0), 0))


def _fwd_mix(proj, cw0, cw1, cw2, w_co, w_pool, mem, gain_mem, w_kv, tm):
    T = proj.shape[1]
    M = mem.shape[0]

    def body(b_ref, c_ref, ua_ref, up_ref, ch_ref, uah_ref, uph_ref, cw0_ref, cw1_ref, cw2_ref, wco_ref, wp_ref,
             mem_ref, gm_ref, wkv_ref, za_ref, conv_ref, pooled_ref, ya_ref, yp_ref, kv_ref, memn_ref):
        i = pl.program_id(0)

        @pl.when(i == 0)
        def _():
            m = mem_ref[...]
            memn = (m * _rms(m) * gm_ref[...]).astype(BF16)
            memn_ref[...] = memn
            for j in range(2 * NH):
                kv_ref[j] = _mm(memn, wkv_ref[j]).astype(BF16)
        keep = jnp.where(i > 0, 1.0, 0.0).astype(F32)
        cu = c_ref[...].astype(F32) * ua_ref[...].astype(F32)
        cu_h = ch_ref[...].astype(F32) * uah_ref[...].astype(F32) * keep
        ext = jnp.concatenate([cu_h, cu], axis=0)
        conv = (cw2_ref[...] * ext + cw1_ref[...] * _shift_down(ext, 1) + cw0_ref[...] * _shift_down(ext, 2))[HALO:]
        za = (b_ref[...].astype(F32) * conv).astype(BF16)
        conv_ref[...] = conv.astype(BF16)
        za_ref[...] = za
        ya_ref[...] = _mm(za, wco_ref[...]).astype(BF16)

        up = up_ref[...].astype(F32)
        ext_u = jnp.concatenate([uph_ref[...].astype(F32) * keep, up], axis=0)
        pos = i * tm + lax.broadcasted_iota(jnp.int32, (tm, HD), 0)
        for g in range(NPOOL):
            cols = slice(g * HD, (g + 1) * HD)
            s = ext_u[:, cols]
            for k in range(g + 1):
                s = s + _shift_down(s, 1 << k)
            cnt = jnp.minimum(pos + 1, 2 << g).astype(F32)
            pooled = (s[HALO:] / cnt - up[:, cols]).astype(BF16)
            pooled_ref[:, cols] = pooled
            yp_ref[:, cols] = _mm(pooled, wp_ref[g]).astype(BF16)

    tile = lambda s: pl.BlockSpec((None, tm, D), lambda i: (s, i, 0))
    row = pl.BlockSpec((1, D), lambda i: (0, 0))
    out = pl.BlockSpec((tm, D), lambda i: (i, 0))
    return pl.pallas_call(
        body, name="fwd_mix", grid=(T // tm,),
        in_specs=[tile(0), tile(1), tile(2), tile(3), _halo_before(1, tm), _halo_before(2, tm), _halo_before(3, tm),
                  row, row, row, _whole((D, D)), _whole((NPOOL, HD, HD)), _whole((M, D)), row, _whole((2 * NH, D, HD))],
        out_specs=[out] * 5 + [pl.BlockSpec((2 * NH, M, HD), lambda i: (0, 0, 0)), pl.BlockSpec((M, D), lambda i: (0, 0))],
        out_shape=[SDS((T, D), BF16)] * 5 + [SDS((2 * NH, M, HD), BF16), SDS((M, D), BF16)],
        compiler_params=_params(1))(proj, proj, proj, proj, proj, proj, proj, cw0, cw1, cw2, w_co, w_pool, mem, gain_mem, w_kv)


def _softmax_rows(s):
    e = jnp.exp(s - jnp.max(s, axis=-1, keepdims=True))
    return e / jnp.sum(e, axis=-1, keepdims=True)


def _fwd_merge(proj, ya, yp, x, kv, w_xo, w_o, pscale, gain_ffn, tm):
    T = x.shape[0]

    def body(q_ref, ga_ref, gp_ref, gx_ref, ya_ref, yp_ref, x_ref, kv_ref, wxo_ref, wo_ref, ps_ref, gf_ref,
             o_ref, yx_ref, merged_ref, x1_ref, h2_ref):
        for h in range(NH):
            cols = slice(h * HD, (h + 1) * HD)
            p = _softmax_rows(_mm_nt(q_ref[:, cols], kv_ref[h]) * ATT_SCALE)
            o_ref[:, cols] = _mm(p.astype(BF16), kv_ref[NH + h]).astype(BF16)
        yx = _mm(o_ref[...], wxo_ref[...])
        yx_ref[...] = yx.astype(BF16)
        merged = (_sigmoid(ga_ref[...].astype(F32)) * ya_ref[...].astype(F32)
                  + _sigmoid(gp_ref[...].astype(F32)) * (yp_ref[...].astype(F32) * ps_ref[...])
                  + _sigmoid(gx_ref[...].astype(F32)) * yx).astype(BF16)
        merged_ref[...] = merged
        x1 = x_ref[...] + _mm(merged, wo_ref[...])
        x1_ref[...] = x1
        h2_ref[...] = (x1 * _rms(x1) * gf_ref[...]).astype(BF16)

    tile = lambda s: pl.BlockSpec((None, tm, D), lambda i: (s, i, 0))
    row = pl.BlockSpec((1, D), lambda i: (0, 0))
    act = pl.BlockSpec((tm, D), lambda i: (i, 0))
    full = _whole((D, D))
    return pl.pallas_call(
        body, name="fwd_merge", grid=(T // tm,),
        in_specs=[tile(4), tile(5), tile(6), tile(7), act, act, act,
                  _whole((2 * NH, kv.shape[1], HD)), full, full, row, row],
        out_specs=[act] * 5,
        out_shape=[SDS((T, D), BF16), SDS((T, D), BF16), SDS((T, D), BF16), SDS((T, D), F32), SDS((T, D), BF16)],
        compiler_params=_params(1))(proj, proj, proj, proj, ya, yp, x, kv, w_xo, w_o, pscale, gain_ffn)


def _fwd_ffn_up(h2, wg_t, wu_t, tm, tn):
    T = h2.shape[0]

    def body(h_ref, wg_ref, wu_ref, gate_ref, up_ref, act_ref):
        for cols in _col_chunks(tn):
            gate = _mm_nt(h_ref[...], wg_ref[cols, :])
            up = _mm_nt(h_ref[...], wu_ref[cols, :])
            gate_ref[:, cols] = gate.astype(BF16)
            up_ref[:, cols] = up.astype(BF16)
            act_ref[:, cols] = (gate * _sigmoid(gate) * up).astype(BF16)

    w = pl.BlockSpec((tn, D), lambda n, i: (n, 0))
    o = pl.BlockSpec((tm, tn), lambda n, i: (i, n))
    return pl.pallas_call(
        body, name="fwd_ffn_up", grid=(DFF // tn, T // tm),
        in_specs=[pl.BlockSpec((tm, D), lambda n, i: (i, 0)), w, w],
        out_specs=[o] * 3, out_shape=[SDS((T, DFF), BF16)] * 3,
        compiler_params=_params(2))(h2, wg_t, wu_t)


def _fwd_ffn_down_loss(act, w_d, x1, target, gain_final, tm):
    T = x1.shape[0]

    def body(act_ref, wd_ref, x1_ref, tgt_ref, g_ref, dx2_ref, loss_ref, dgain_ref):
        @pl.when(pl.program_id(0) == 0)
        def _():
            loss_ref[...] = jnp.zeros_like(loss_ref)
            dgain_ref[...] = jnp.zeros_like(dgain_ref)
        x2 = x1_ref[...] + _mm(act_ref[...], wd_ref[...])
        gain = g_ref[...]
        y = x2 * _rms(x2) * gain
        err = y - tgt_ref[...]
        loss_ref[...] += 0.5 * jnp.sum(jnp.mean(err * err, axis=-1, keepdims=True))
        dx2, dgain = _norm_bwd(err * (1.0 / D), x2, gain)
        dx2_ref[...] = dx2
        dgain_ref[...] += dgain

    act_spec = pl.BlockSpec((tm, D), lambda i: (i, 0))
    row = pl.BlockSpec((1, D), lambda i: (0, 0))
    return pl.pallas_call(
        body, name="fwd_ffn_down_loss", grid=(T // tm,),
        in_specs=[pl.BlockSpec((tm, DFF), lambda i: (i, 0)), _whole((DFF, D)), act_spec, act_spec, row],
        out_specs=[act_spec, pl.BlockSpec((8, D), lambda i: (0, 0)), row],
        out_shape=[SDS((T, D), F32), SDS((8, D), F32), SDS((1, D), F32)],
        compiler_params=_params(1))(act, w_d, x1, target, gain_final)


def _bwd_ffn_down(dx2, w_d, gate, up, tm, tn):
    T = dx2.shape[0]

    def body(dx_ref, wd_ref, gate_ref, up_ref, dgate_ref, dup_ref):
        dx = dx_ref[...].astype(BF16)
        for cols in _col_chunks(tn):
            dact = _mm_nt(dx, wd_ref[cols, :])
            gate = gate_ref[:, cols].astype(F32)
            sg = _sigmoid(gate)
            dgate_ref[:, cols] = (dact * up_ref[:, cols].astype(F32) * (sg * (1.0 + gate * (1.0 - sg)))).astype(BF16)
            dup_ref[:, cols] = (dact * gate * sg).astype(BF16)

    o = pl.BlockSpec((tm, tn), lambda n, i: (i, n))
    return pl.pallas_call(
        body, name="bwd_ffn_down", grid=(DFF // tn, T // tm),
        in_specs=[pl.BlockSpec((tm, D), lambda n, i: (i, 0)), pl.BlockSpec((tn, D), lambda n, i: (n, 0)), o, o],
        out_specs=[o] * 2, out_shape=[SDS((T, DFF), BF16)] * 2,
        compiler_params=_params(2))(dx2, w_d, gate, up)


def _bwd_ffn_up(dgate, dup, wg_t, wu_t, x1, dx2, gain_ffn, tm):
    T = x1.shape[0]

    def body(dg_ref, du_ref, wg_ref, wu_ref, x1_ref, dx2_ref, g_ref, dx1_ref, dgain_ref):
        @pl.when(pl.program_id(0) == 0)
        def _():
            dgain_ref[...] = jnp.zeros_like(dgain_ref)
        dh2 = _mm(dg_ref[...], wg_ref[...]) + _mm(du_ref[...], wu_ref[...])
        dx, dgain = _norm_bwd(dh2, x1_ref[...], g_ref[...])
        dx1_ref[...] = dx2_ref[...] + dx
        dgain_ref[...] += dgain

    wide = pl.BlockSpec((tm, DFF), lambda i: (i, 0))
    w = _whole((DFF, D))
    act = pl.BlockSpec((tm, D), lambda i: (i, 0))
    row = pl.BlockSpec((1, D), lambda i: (0, 0))
    return pl.pallas_call(
        body, name="bwd_ffn_up", grid=(T // tm,),
        in_specs=[wide, wide, w, w, act, act, row], out_specs=[act, row],
        out_shape=[SDS((T, D), F32), SDS((1, D), F32)],
        compiler_params=_params(1))(dgate, dup, wg_t, wu_t, x1, dx2, gain_ffn)


def _wgrad(a, b, *, name, groups, a_cols, b_cols, tt, a_index, b_index, o_index, out_shape, after):
    T = a.shape[0]
    nt = T // tt
    n_a = a.shape[1] // a_cols if groups == 1 else 1

    def body(a_ref, b_ref, after_ref, o_ref, acc_ref):
        del after_ref
        t = pl.program_id(2)

        @pl.when(t == 0)
        def _():
            acc_ref[...] = jnp.zeros_like(acc_ref)
        acc_ref[...] += _mm_tn(a_ref[...].astype(BF16), b_ref[...].astype(BF16))

        @pl.when(t == nt - 1)
        def _():
            o_ref[...] = acc_ref[...].astype(o_ref.dtype)

    return pl.pallas_call(
        body, name=name, grid=(groups, n_a, nt),
        in_specs=[pl.BlockSpec((tt, a_cols), a_index), pl.BlockSpec((None, tt, b_cols), b_index), HBM],
        out_specs=pl.BlockSpec((None, a_cols, b_cols), o_index),
        out_shape=SDS(out_shape, BF16),
        scratch_shapes=[pltpu.VMEM((a_cols, b_cols), F32)],
        compiler_params=_params(3))(a, b, after)


def _wgrad_dense(a, b, name, tt, after, a_cols=None):
    ka, nb = a.shape[1], b.shape[1]
    a_cols = ka if a_cols is None else a_cols
    out = _wgrad(a, b[None], name=name, groups=1, a_cols=a_cols, b_cols=nb, tt=tt,
                 a_index=lambda g, k, t: (t, k), b_index=lambda g, k, t: (0, t, 0),
                 o_index=lambda g, k, t: (k, 0, 0), out_shape=(ka // a_cols, a_cols, nb), after=after)
    return out.reshape(ka, nb)


def _bwd_merge(dx1, proj, ya, yp, yx, pooled, pscale, w_o, w_co, w_xo, w_pool, tm, after):
    T = dx1.shape[0]
    nt = T // tm

    def body(dx1_ref, ga_ref, gp_ref, gx_ref, ya_ref, yp_ref, yx_ref, pooled_ref, ps_ref, wo_ref, wco_ref, wxo_ref, wp_ref,
             after_ref, dgates_ref, dya_ref, dyx_ref, dza_ref, do_ref, dpooled_ref, dps_ref, dwp_ref, acc_ref):
        del after_ref

        @pl.when(pl.program_id(0) == 0)
        def _():
            dps_ref[...] = jnp.zeros_like(dps_ref)
            acc_ref[...] = jnp.zeros_like(acc_ref)
        dmerged = _mm_nt(dx1_ref[...].astype(BF16), wo_ref[...])
        scale = ps_ref[...]
        sa, sp, sx = (_sigmoid(r[...].astype(F32)) for r in (ga_ref, gp_ref, gx_ref))
        ya, yp_pre, yx = (r[...].astype(F32) for r in (ya_ref, yp_ref, yx_ref))
        dgates_ref[0] = (dmerged * ya * sa * (1.0 - sa)).astype(BF16)
        dgates_ref[1] = (dmerged * (yp_pre * scale) * sp * (1.0 - sp)).astype(BF16)
        dgates_ref[2] = (dmerged * yx * sx * (1.0 - sx)).astype(BF16)
        dya = (dmerged * sa).astype(BF16)
        dyx = (dmerged * sx).astype(BF16)
        dyp = dmerged * sp
        dyps = (dyp * scale).astype(BF16)
        dps_ref[...] += jnp.sum(dyp * yp_pre, axis=0, keepdims=True)
        dya_ref[...] = dya
        dyx_ref[...] = dyx
        dza_ref[...] = _mm_nt(dya, wco_ref[...]).astype(BF16)
        do_ref[...] = _mm_nt(dyx, wxo_ref[...]).astype(BF16)
        for g in range(NPOOL):
            cols = slice(g * HD, (g + 1) * HD)
            dpooled_ref[:, cols] = _mm_nt(dyps[:, cols], wp_ref[g]).astype(BF16)
            acc_ref[g] += _mm_tn(pooled_ref[:, cols], dyps[:, cols])

        @pl.when(pl.program_id(0) == nt - 1)
        def _():
            dwp_ref[...] = acc_ref[...].astype(BF16)

    tile = lambda s: pl.BlockSpec((None, tm, D), lambda i: (s, i, 0))
    row = pl.BlockSpec((1, D), lambda i: (0, 0))
    act = pl.BlockSpec((tm, D), lambda i: (i, 0))
    full = _whole((D, D))
    return pl.pallas_call(
        body, name="bwd_merge", grid=(T // tm,),
        in_specs=[act, tile(5), tile(6), tile(7), act, act, act, act, row, full, full, full,
                  _whole((NPOOL, HD, HD)), HBM],
        out_specs=[pl.BlockSpec((3, tm, D), lambda i: (0, i, 0))] + [act] * 5
        + [row, pl.BlockSpec((NPOOL, HD, HD), lambda i: (0, 0, 0))],
        out_shape=[SDS((NSPLIT, T, D), BF16)] + [SDS((T, D), BF16)] * 5 + [SDS((1, D), F32), SDS((NPOOL, HD, HD), BF16)],
        scratch_shapes=[pltpu.VMEM((NPOOL, HD, HD), F32)],
        compiler_params=_params(1))(dx1, proj, proj, proj, ya, yp, yx, pooled, pscale, w_o, w_co, w_xo, w_pool, after)


def _bwd_attn(dproj, proj, do, kv, memn, w_kv, mem, gain_mem, tm):
    T = do.shape[0]
    M = kv.shape[1]
    nt = T // tm

    def body(dproj_hbm, q_ref, do_ref, kv_ref, memn_ref, wkv_ref, mem_ref, gm_ref, dq_ref, dw_ref, dgain_ref, dkv_ref):
        del dproj_hbm

        @pl.when(pl.program_id(0) == 0)
        def _():
            dkv_ref[...] = jnp.zeros_like(dkv_ref)
        for h in range(NH):
            cols = slice(h * HD, (h + 1) * HD)
            q = q_ref[:, cols]
            do_h = do_ref[:, cols]
            p = _softmax_rows(_mm_nt(q, kv_ref[h]) * ATT_SCALE)
            dp = _mm_nt(do_h, kv_ref[NH + h])
            ds = (p * (dp - jnp.sum(dp * p, axis=-1, keepdims=True)) * ATT_SCALE).astype(BF16)
            dq_ref[:, cols] = _mm(ds, kv_ref[h]).astype(BF16)
            dkv_ref[h] += _mm_tn(ds, q)
            dkv_ref[NH + h] += _mm_tn(p.astype(BF16), do_h)

        @pl.when(pl.program_id(0) == nt - 1)
        def _():
            dmemn = jnp.zeros((M, D), F32)
            for j in range(2 * NH):
                dkv_j = dkv_ref[j].astype(BF16)
                dw_ref[j] = _mm_tn(memn_ref[...], dkv_j).astype(BF16)
                dmemn = dmemn + _mm_nt(dkv_j, wkv_ref[j])
            dgain_ref[...] = _norm_bwd(dmemn, mem_ref[...], gm_ref[...])[1]

    row = pl.BlockSpec((1, D), lambda i: (0, 0))
    return pl.pallas_call(
        body, name="bwd_attn", grid=(nt,),
        in_specs=[HBM, pl.BlockSpec((None, tm, D), lambda i: (4, i, 0)), pl.BlockSpec((tm, D), lambda i: (i, 0)),
                  _whole((2 * NH, M, HD)), _whole((M, D)), _whole((2 * NH, D, HD)), _whole((M, D)), row],
        out_specs=[pl.BlockSpec((None, tm, D), lambda i: (3, i, 0)),
                   pl.BlockSpec((2 * NH, D, HD), lambda i: (0, 0, 0)), row],
        out_shape=[SDS(dproj.shape, BF16), SDS((2 * NH, D, HD), BF16), SDS((1, D), F32)],
        scratch_shapes=[pltpu.VMEM((2 * NH, M, HD), F32)],
        input_output_aliases={0: 0},
        compiler_params=_params(1))(dproj, proj, do, kv, memn, w_kv, mem, gain_mem)


def _bwd_mix(dproj, proj, conv, dza, dpooled, cw0, cw1, cw2, tm, after):
    T = dza.shape[0]
    nt = T // tm

    def halo_after(split_or_none):
        idx = lambda i: jnp.minimum((i + 1) * (tm // HALO), T // HALO - 1)
        if split_or_none is None:
            return pl.BlockSpec((HALO, D), lambda i: (idx(i), 0))
        return pl.BlockSpec((None, HALO, D), lambda i: (split_or_none, idx(i), 0))

    def body(dproj_hbm, b_ref, c_ref, ua_ref, conv_ref, dza_ref, dpo_ref, bn_ref, dzan_ref, dpon_ref,
             cw0_ref, cw1_ref, cw2_ref, after_ref, dabcu_ref, dcw_ref):
        del dproj_hbm, after_ref
        i = pl.program_id(0)

        @pl.when(i == 0)
        def _():
            dcw_ref[...] = jnp.zeros_like(dcw_ref)
        keep_next = jnp.where(i < nt - 1, 1.0, 0.0).astype(F32)
        dza = dza_ref[...].astype(F32)
        c = c_ref[...].astype(F32)
        ua = ua_ref[...].astype(F32)
        dconv = dza * b_ref[...].astype(F32)
        dconv_n = dzan_ref[...].astype(F32) * bn_ref[...].astype(F32) * keep_next
        ext = jnp.concatenate([dconv, dconv_n], axis=0)
        dconv_1, dconv_2 = _shift_up(ext, 1)[:tm], _shift_up(ext, 2)[:tm]
        dcu = cw2_ref[...] * dconv + cw1_ref[...] * dconv_1 + cw0_ref[...] * dconv_2
        dabcu_ref[0] = (dza * conv_ref[...].astype(F32)).astype(BF16)
        dabcu_ref[1] = (dcu * ua).astype(BF16)
        dabcu_ref[2] = (dcu * c).astype(BF16)

        cu = c * ua
        dcw_ref[2:3, :] += jnp.sum(dconv * cu, axis=0, keepdims=True)
        dcw_ref[1:2, :] += jnp.sum(dconv_1 * cu, axis=0, keepdims=True)
        dcw_ref[0:1, :] += jnp.sum(dconv_2 * cu, axis=0, keepdims=True)

        dpo = dpo_ref[...].astype(F32)
        ext_dpo = jnp.concatenate([dpo, dpon_ref[...].astype(F32) * keep_next], axis=0)
        pos = i * tm + lax.broadcasted_iota(jnp.int32, (tm + HALO, HD), 0)
        for g in range(NPOOL):
            cols = slice(g * HD, (g + 1) * HD)
            s = ext_dpo[:, cols] / jnp.minimum(pos + 1, 2 << g).astype(F32)
            for k in range(g + 1):
                s = s + _shift_up(s, 1 << k)
            dabcu_ref[3, :, cols] = (s[:tm] - dpo[:, cols]).astype(BF16)

    tile = lambda s: pl.BlockSpec((None, tm, D), lambda i: (s, i, 0))
    act = pl.BlockSpec((tm, D), lambda i: (i, 0))
    row = pl.BlockSpec((1, D), lambda i: (0, 0))
    return pl.pallas_call(
        body, name="bwd_mix", grid=(nt,),
        in_specs=[HBM, tile(0), tile(1), tile(2), act, act, act, halo_after(0), halo_after(None), halo_after(None),
                  row, row, row, HBM],
        out_specs=[pl.BlockSpec((4, tm, D), lambda i: (1, i, 0)), pl.BlockSpec((8, D), lambda i: (0, 0))],
        out_shape=[SDS(dproj.shape, BF16), SDS((8, D), F32)],
        input_output_aliases={0: 0},
        compiler_params=_params(1))(dproj, proj, proj, proj, conv, dza, dpooled, proj, dza, dpooled, cw0, cw1, cw2, after)


def _bwd_proj(dproj, w_in_g, x, dx1, gain, tm, after):
    T = x.shape[0]

    def body(dp_ref, w_ref, x_ref, dx1_ref, g_ref, after_ref, dx_ref, dgain_ref, acc_ref):
        del after_ref
        i, s = pl.program_id(0), pl.program_id(1)

        @pl.when((i == 0) & (s == 0))
        def _():
            dgain_ref[...] = jnp.zeros_like(dgain_ref)

        @pl.when(s == 0)
        def _():
            acc_ref[...] = jnp.zeros_like(acc_ref)
        acc_ref[...] += _mm_nt(dp_ref[...], w_ref[...])

        @pl.when(s == NSPLIT - 1)
        def _():
            dx, dgain = _norm_bwd(acc_ref[...], x_ref[...], g_ref[...])
            dx_ref[...] = dx1_ref[...] + dx
            dgain_ref[...] += dgain

    act = pl.BlockSpec((tm, D), lambda i, s: (i, 0))
    row = pl.BlockSpec((1, D), lambda i, s: (0, 0))
    return pl.pallas_call(
        body, name="bwd_proj", grid=(T // tm, NSPLIT),
        in_specs=[pl.BlockSpec((None, tm, D), lambda i, s: (s, i, 0)),
                  pl.BlockSpec((None, D, D), lambda i, s: (_slot_group(s), 0, 0)), act, act, row, HBM],
        out_specs=[act, row], out_shape=[SDS((T, D), F32), SDS((1, D), F32)],
        scratch_shapes=[pltpu.VMEM((tm, D), F32)],
        compiler_params=_params(2))(dproj, w_in_g, x, dx1, gain, after)


def _adamw_math(w, g, m, v):
    m = ADAM_B1 * m + (1.0 - ADAM_B1) * g
    v = ADAM_B2 * v + (1.0 - ADAM_B2) * (g * g)
    m_hat = m / (1.0 - ADAM_B1 ** ADAM_STEP)
    v_hat = v / (1.0 - ADAM_B2 ** ADAM_STEP)
    delta = -ADAM_LR * (m_hat / (jnp.sqrt(v_hat) + ADAM_EPS) + ADAM_WD * w)
    return delta, m, v


def _row_tile(rows):
    return 256 if rows % 256 == 0 else rows


def _sum_parts(parts, name):
    n_parts, rows, cols = parts.shape
    tr = _row_tile(rows)

    def body(p_ref, g_ref):
        g = p_ref[0].astype(F32)
        for k in range(1, n_parts):
            g = g + p_ref[k].astype(F32)
        g_ref[...] = g

    blk = pl.BlockSpec((tr, cols), lambda i: (i, 0))
    return pl.pallas_call(
        body, name=name, grid=(rows // tr,),
        in_specs=[pl.BlockSpec((n_parts, tr, cols), lambda i: (0, i, 0))], out_specs=blk,
        out_shape=SDS((rows, cols), F32), compiler_params=_params(1))(parts)


def _adamw(ws, gs, ms, vs, name, from_parts, steps):
    n = len(ws)

    def body(*refs):
        for a in range(n):
            w_ref, g_ref, m_ref, v_ref = refs[4 * a:4 * a + 4]
            go_ref, d_ref, mo_ref, vo_ref = refs[4 * n + 4 * a:4 * n + 4 * a + 4]
            if from_parts:
                g = g_ref[0].astype(F32)
                for k in range(1, g_ref.shape[0]):
                    g = g + g_ref[k].astype(F32)
            else:
                g = g_ref[...]
            go_ref[...] = g
            d_ref[...], mo_ref[...], vo_ref[...] = _adamw_math(w_ref[...], g, m_ref[...], v_ref[...])

    in_specs, out_specs, out_shape, operands = [], [], [], []
    for w, g, m, v in zip(ws, gs, ms, vs):
        rows, cols = w.shape
        blk = pl.BlockSpec((rows // steps, cols), lambda i: (i, 0))
        g_spec = pl.BlockSpec((g.shape[0], rows // steps, cols), lambda i: (0, i, 0)) if from_parts else blk
        in_specs += [blk, g_spec, blk, blk]
        out_specs += [blk] * 4
        out_shape += [SDS((rows, cols), F32)] * 4
        operands += [w, g, m, v]
    outs = pl.pallas_call(body, name=name, grid=(steps,), in_specs=in_specs, out_specs=out_specs, out_shape=out_shape,
                          compiler_params=_params(1))(*operands)
    return [outs[4 * a:4 * a + 4] for a in range(n)]


def _peer(k, x, y, c):
    return ((1 - x) if k & 4 else x, (1 - y) if k & 2 else y, (1 - c) if k & 1 else c)


SEM = pl.BlockSpec(memory_space=pltpu.SEMAPHORE)
IN_HBM = pl.BlockSpec(memory_space=pltpu.HBM)
DATAFLOW = pltpu.SideEffectType.DATAFLOW_SIDE_EFFECTING
TOKEN_SHAPE = (8, 128)


OTHER_CHIPS = (2, 4, 6)


def _place(x, y, c):
    return 4 * x + 2 * y + c


def _plan_gather_chips(n, ks=(1,) + OTHER_CHIPS):
    def plan(refs, x, y, c, arriving):
        out = []
        for a in range(n):
            for k in ks:
                there = _place(*_peer(k, x, y, c))
                out.append((refs[a], refs[n + a].at[there if arriving else _place(x, y, c)], k))
        return out
    return plan, n * len(ks)


def _plan_gather_sibling(n, ks=OTHER_CHIPS):
    def plan(refs, x, y, c, arriving):
        out = []
        for a in range(n):
            for k in ks:
                px, py, pc = _peer(k, x, y, c)
                mine, theirs = _place(px, py, pc), _place(px, py, 1 - pc)
                out.append((refs[a].at[mine], refs[a].at[theirs if arriving else mine], 1))
        return out
    return plan, n * len(ks)


def _plan_pair():
    def plan(refs, x, y, c, arriving):
        return [(refs[0], refs[1].at[(1 - c) if arriving else c], 1)]
    return plan, 1


def _plan_far_chip():
    def plan(refs, x, y, c, arriving):
        return [(refs[0], refs[1].at[c], 6)]
    return plan, 1


def _plan_far_sibling():
    def plan(refs, x, y, c, arriving):
        return [(refs[0].at[c], refs[0].at[(1 - c) if arriving else c], 1)]
    return plan, 1


def _plan_scatter_sibling(n):
    def plan(refs, x, y, c, arriving):
        out = []
        for a in range(n):
            for q in range(4):
                out.append((refs[a].at[2 * q + (1 - c)], refs[n + a].at[q], 1))
        return out
    return plan, n * 4


def _plan_scatter_chips(n):
    def plan(refs, x, y, c, arriving):
        out = []
        for a in range(n):
            for k in OTHER_CHIPS:
                px, py, _ = _peer(k, x, y, c)
                out.append((refs[a].at[2 * px + py], refs[n + a].at[(2 * px + py) if arriving else (2 * x + y)], k))
        return out
    return plan, n * 3


def _remote(src, dst, send_sems, recv_sems, i, k):
    x, y, c = (lax.axis_index(n) for n in AXES)
    return pltpu.make_async_remote_copy(src_ref=src, dst_ref=dst, send_sem=send_sems.at[i], recv_sem=recv_sems.at[i],
                                        device_id=_peer(k, x, y, c), device_id_type=pl.DeviceIdType.MESH)


def _copies_start(groups, name, after):
    ng = len(groups)
    total = sum(len(bufs) for bufs, _ in groups)

    def body(*refs):
        sems = refs[1 + total:1 + total + 2 * ng]
        x, y, c = (lax.axis_index(n) for n in AXES)
        off = 1
        for gi, (bufs, (plan, _)) in enumerate(groups):
            for i, (src, dst, k) in enumerate(plan(refs[off:off + len(bufs)], x, y, c, False)):
                _remote(src, dst, sems[2 * gi], sems[2 * gi + 1], i, k).start()
            off += len(bufs)
        refs[-1][...] = jnp.zeros(TOKEN_SHAPE, F32)

    sem_shapes = [pltpu.SemaphoreType.DMA((count,)) for _, (_, count) in groups for _ in range(2)]
    flat = [b for bufs, _ in groups for b in bufs]
    outs = pl.pallas_call(
        body, name=name,
        in_specs=[HBM] + [IN_HBM] * total,
        out_specs=[SEM] * (2 * ng) + [IN_HBM] * total + [pl.BlockSpec(memory_space=pltpu.VMEM)],
        out_shape=sem_shapes + [pltpu.HBM(b.shape, b.dtype) for b in flat] + [SDS(TOKEN_SHAPE, F32)],
        input_output_aliases={1 + i: 2 * ng + i for i in range(total)},
        compiler_params=pltpu.CompilerParams(has_side_effects=DATAFLOW),
    )(after, *[pltpu.with_memory_space_constraint(b, pltpu.HBM) for b in flat])
    handles, off = [], 2 * ng
    for gi, (bufs, _) in enumerate(groups):
        handles.append((outs[2 * gi], outs[2 * gi + 1], list(outs[off:off + len(bufs)])))
        off += len(bufs)
    return handles, outs[-1]


def _copies_wait_start(handle, plan, pass_on, more, name, after):
    send_sems, recv_sems, bufs = handle
    n = len(bufs)
    idx, (pass_plan, pass_count) = pass_on
    total = sum(len(b) for b, _ in more)
    ng = 1 + len(more)

    def body(*refs):
        x, y, c = (lax.axis_index(a) for a in AXES)
        waited = refs[1:1 + n]
        outs = refs[3 + n + total:]
        new_sems = outs[n + total:n + total + 2 * ng]
        for i, (src, dst, k) in enumerate(plan[0](waited, x, y, c, True)):
            copy = _remote(src, dst, refs[1 + n + total], refs[2 + n + total], i, k)
            copy.wait_send()
            copy.wait_recv()
        for i, (src, dst, k) in enumerate(pass_plan([waited[j] for j in idx], x, y, c, False)):
            _remote(src, dst, new_sems[0], new_sems[1], i, k).start()
        off = 1 + n
        for gi, (b, (p, _)) in enumerate(more):
            for i, (src, dst, k) in enumerate(p(refs[off:off + len(b)], x, y, c, False)):
                _remote(src, dst, new_sems[2 + 2 * gi], new_sems[3 + 2 * gi], i, k).start()
            off += len(b)
        outs[-1][...] = jnp.zeros(TOKEN_SHAPE, F32)

    flat = list(bufs) + [a for b, _ in more for a in b]
    sem_shapes = [pltpu.SemaphoreType.DMA((count,)) for count in [pass_count] + [cnt for _, (_, cnt) in more] for _ in range(2)]
    outs = pl.pallas_call(
        body, name=name,
        in_specs=[HBM] + [IN_HBM] * (n + total) + [SEM, SEM],
        out_specs=[IN_HBM] * (n + total) + [SEM] * (2 * ng) + [pl.BlockSpec(memory_space=pltpu.VMEM)],
        out_shape=[pltpu.HBM(b.shape, b.dtype) for b in flat] + sem_shapes + [SDS(TOKEN_SHAPE, F32)],
        input_output_aliases={1 + i: i for i in range(n + total)},
        compiler_params=pltpu.CompilerParams(has_side_effects=DATAFLOW),
    )(after, *[pltpu.with_memory_space_constraint(b, pltpu.HBM) for b in flat], send_sems, recv_sems)
    thru = list(outs[:n])
    sems_out = outs[n + total:n + total + 2 * ng]
    handles = [(sems_out[0], sems_out[1], [thru[j] for j in idx])]
    off = n
    for gi, (b, _) in enumerate(more):
        handles.append((sems_out[2 + 2 * gi], sems_out[3 + 2 * gi], list(outs[off:off + len(b)])))
        off += len(b)
    return thru, handles, outs[-1]


def _copies_wait(handle, plan, name, after):
    send_sems, recv_sems, bufs = handle
    n = len(bufs)

    def body(*refs):
        x, y, c = (lax.axis_index(a) for a in AXES)
        for i, (src, dst, k) in enumerate(plan[0](refs[:n], x, y, c, True)):
            copy = _remote(src, dst, refs[n], refs[n + 1], i, k)
            copy.wait_send()
            copy.wait_recv()

    return pl.pallas_call(
        body, name=name,
        in_specs=[IN_HBM] * n + [SEM, SEM, HBM], out_specs=[IN_HBM] * n,
        out_shape=[pltpu.HBM(b.shape, b.dtype) for b in bufs],
        input_output_aliases={i: i for i in range(n)},
        compiler_params=pltpu.CompilerParams(has_side_effects=DATAFLOW),
    )(*bufs, send_sems, recv_sems, after)


def _pair_sums(mine, theirs, c, chip, name):
    n = len(mine)

    def body(where_ref, *refs):
        q = pl.program_id(0)
        for a in range(n):
            total = (refs[a][...].astype(F32) + refs[n + a][...].astype(F32)).astype(BF16)
            refs[2 * n + a][...] = total

            @pl.when(q == where_ref[1])
            def _():
                refs[3 * n + a][...] = total

    block = lambda t: (None,) + t.shape[1:]
    zeros = lambda t: (0,) * (t.ndim - 1)
    outs = pl.pallas_call(
        body, name=name,
        grid_spec=pltpu.PrefetchScalarGridSpec(
            num_scalar_prefetch=1, grid=(4,),
            in_specs=[pl.BlockSpec(block(t), lambda q, w, z=zeros(t): (2 * q + w[0],) + z) for t in theirs]
            + [pl.BlockSpec(block(t), lambda q, w, z=zeros(t): (q,) + z) for t in theirs],
            out_specs=[pl.BlockSpec(block(t), lambda q, w, z=zeros(t): (q,) + z) for t in theirs]
            + [pl.BlockSpec(block(t), lambda q, w, z=zeros(t): (w[1],) + z) for t in theirs]),
        out_shape=[SDS(t.shape, BF16) for t in theirs] * 2,
        compiler_params=_params(1))(jnp.stack([c, chip]).astype(jnp.int32), *mine, *theirs)
    return list(outs[:n]), list(outs[n:])


def _local_step(x, mem, target, gains, get, put, flush, tm_huge=2048, tm_big=1024, tm_mid=512, tm_small=256):
    g_mix, pscale, g_mem, g_ffn, g_fin = gains
    T = x.shape[0]
    tm_huge, tm_big, tm_mid, tm_small = min(tm_huge, T), min(tm_big, T), min(tm_mid, T), min(tm_small, T)
    tn = DFF // 2

    w_pair, w_ids, p_ids = get("in_pair", x)
    proj, h = _fwd_proj(x, g_mix, w_pair, w_ids, p_ids, tm_huge)
    w_near, w_ids, p_ids = get("in_near", h)
    proj = _fwd_proj_more(h, w_near, proj, w_ids, p_ids, tm_huge, "fwd_proj_near")
    w_far, w_ids, p_ids = get("in_far", proj)
    proj = _fwd_proj_more(h, w_far, proj, w_ids, p_ids, tm_huge, "fwd_proj_far")
    w_in = get("in_whole", (w_pair, w_near, w_far))
    cw0, cw1, cw2, w_co, w_pool, w_kv = get("mix", proj)
    za, conv, pooled, ya, yp, kv, memn = _fwd_mix(proj, cw0, cw1, cw2, w_co, w_pool, mem, g_mem, w_kv, tm_mid)
    w_xo, w_o = get("merge", ya)
    o, yx, merged, x1, h2 = _fwd_merge(proj, ya, yp, x, kv, w_xo, w_o, pscale, g_ffn, tm_mid)
    wg_t, wu_t = get("gate_up", x1)
    get("down", x1, early=True)
    gate, up, act = _fwd_ffn_up(h2, wg_t, wu_t, tm_big, tn)
    (w_d,) = get("down", gate)
    dx2, loss, dg_fin = _fwd_ffn_down_loss(act, w_d, x1, target, g_fin, tm_mid)

    dgate, dup = _bwd_ffn_down(dx2, w_d, gate, up, tm_big, tn)
    dx1, dg_ffn = _bwd_ffn_up(dgate, dup, wg_t, wu_t, x1, dx2, g_ffn, tm_small)
    dw_d = _wgrad_dense(act, dx2, "wgrad_down", tm_big, g_mix)
    dwg_t = _wgrad_dense(dgate, h2, "wgrad_gate", tm_big, g_mix)
    dwu_t = _wgrad_dense(dup, h2, "wgrad_up", tm_big, g_mix)
    token = put("ffn", (dwg_t, dwu_t, dw_d))

    dproj, dya, dyx, dza, do, dpooled, dpscale, dw_pool = _bwd_merge(
        dx1, proj, ya, yp, yx, pooled, pscale, w_o, w_co, w_xo, w_pool, tm_mid, token)
    token = flush(dya)
    dw_o = _wgrad_dense(merged, dx1, "wgrad_out", tm_big, token)
    dw_co = _wgrad_dense(za, dya, "wgrad_conv_out", tm_big, token)
    dw_xo = _wgrad_dense(o, dyx, "wgrad_xattn_out", tm_big, token)
    dproj, dw_kv, dg_mem = _bwd_attn(dproj, proj, do, kv, memn, w_kv, mem, g_mem, tm_big)
    token = put("mix", (dw_co, dw_xo, dw_o, dw_pool, dw_kv))

    dproj, dcw = _bwd_mix(dproj, proj, conv, dza, dpooled, cw0, cw1, cw2, tm_mid, token)
    token = flush(dcw)
    dw_in = _wgrad(h, dproj, name="wgrad_in", groups=NSPLIT, a_cols=D, b_cols=D, tt=tm_huge,
                   a_index=lambda g, k, t: (t, 0), b_index=lambda g, k, t: (g, t, 0),
                   o_index=lambda g, k, t: (_slot_group(g), 0, 0), out_shape=(NSPLIT, D, D), after=token)
    token = flush(put("in", (dw_in,)))
    grad_x, dg_mix = _bwd_proj(dproj, w_in, x, dx1, g_mix, tm_big, token)

    small = jnp.concatenate([dg_mix, dpscale, dg_mem, dg_ffn, dg_fin, dcw[0:3], loss], axis=0)
    return grad_x, small


def kernel(x, mem, norm_mix, w_in, conv_w, w_conv_out, w_pool, pool_scale, norm_mem, w_kv, w_xattn_out, w_out, norm_ffn, w_gate, w_up, w_down, norm_final, loss_target, m_norm_mix, m_w_in, m_conv_w, m_w_conv_out, m_w_pool, m_pool_scale, m_norm_mem, m_w_kv, m_w_xattn_out, m_w_out, m_norm_ffn, m_w_gate, m_w_up, m_w_down, m_norm_final, v_norm_mix, v_w_in, v_conv_w, v_w_conv_out, v_w_pool, v_pool_scale, v_norm_mem, v_w_kv, v_w_xattn_out, v_w_out, v_norm_ffn, v_w_gate, v_w_up, v_w_down, v_norm_final):
    T = x.shape[1]
    rows = D // NDEV
    ffb = DFF // NDEV
    prow = HD // NDEV
    me = 4 * lax.axis_index("x") + 2 * lax.axis_index("y") + lax.axis_index("c")

    shards = [w_in[0].astype(BF16), w_conv_out[0].astype(BF16), w_xattn_out[0].astype(BF16), w_out[0].astype(BF16),
              w_pool[0].astype(BF16).reshape(NPOOL * prow, HD), w_kv[0].astype(BF16),
              w_gate[0].T.astype(BF16), w_up[0].T.astype(BF16), w_down[0].astype(BF16),
              jnp.pad(conv_w[0], ((0, 5), (0, 0)))]

    cx, cy, cc = (lax.axis_index(n) for n in AXES)
    chip = 2 * cx + cy

    def land(own, index, slots):
        return lax.dynamic_update_index_in_dim(lax.empty((slots,) + own.shape, own.dtype), own, index, 0)

    needed = ["in_pair", "in_near", "in_far", "mix", "merge", "gate_up", "down"]
    members = {"mix": [9, 1, 4, 5], "merge": [2, 3], "gate_up": [6, 7], "down": [8]}
    near = (2, 4)
    plans = {"in_pair": _plan_pair(), "in_near": _plan_gather_chips(1, near), "in_far": _plan_far_chip()}
    plans.update({n: _plan_gather_chips(len(members[n])) for n in members})
    g_bufs = {"in_pair": [shards[0], land(shards[0], cc, 2)],
              "in_near": [w_in[0].astype(BF16), lax.empty((NDEV, D, D), BF16)],
              "in_far": [w_in[0].astype(BF16), lax.empty((2, D, D), BF16)]}
    g_bufs.update({n: [shards[i] for i in members[n]] + [land(shards[i], me, NDEV) for i in members[n]] for n in members})
    first_handles, _ = _copies_start([(g_bufs[n], plans[n]) for n in needed[:2]], "gather_start", x)
    g_handles = dict(zip(needed[:2], first_handles))
    pair_ids = jnp.array([0, 1], jnp.int32)

    on_last_leg = {}

    def get(group, after, early=False):
        if group == "in_whole":
            w_pair, w_near, w_far = after
            w_whole = lax.dynamic_update_slice_in_dim(w_near, w_pair, 2 * chip, 0)
            return lax.dynamic_update_slice_in_dim(w_whole, w_far, 2 * (3 - chip), 0)
        if group == "in_pair":
            bufs = _copies_wait(g_handles[group], plans[group], "gather_wait_" + group, after)
            return bufs[1], pair_ids, (2 * chip + pair_ids).astype(jnp.int32)
        if group not in on_last_leg:
            n_bufs = len(g_bufs[group])
            landed = list(range(n_bufs // 2, n_bufs))
            if group == "in_near":
                plan, more = _plan_gather_sibling(1, near), [(g_bufs[n], plans[n]) for n in needed[2:]]
            elif group == "in_far":
                plan, more = _plan_far_sibling(), []
            else:
                plan, more = _plan_gather_sibling(n_bufs // 2), []
            _, handles, token = _copies_wait_start(g_handles[group], plans[group], (landed, plan), more,
                                                   "gather_pass_" + group, after)
            g_handles.update(zip(needed[2:], handles[1:]))
            on_last_leg[group] = (handles[0], plan, token)
        if early:
            return None
        handle, plan, token = on_last_leg[group]
        got = _copies_wait(handle, plan, "gather_passed_" + group, after if group in ("gate_up", "down") else token)
        if group == "in_near":
            groups = jnp.stack([me ^ k for k in (2, 3, 4, 5)]).astype(jnp.int32)
            return got[0], groups, groups
        if group == "in_far":
            return got[0], pair_ids, (2 * (3 - chip) + pair_ids).astype(jnp.int32)
        if group == "mix":
            cw_g, w_co_g, w_pool_g, w_kv_g = got
            cw_full = cw_g.transpose(1, 0, 2).reshape(8, D)
            w_pool_full = w_pool_g.reshape(NDEV, NPOOL, prow, HD).transpose(1, 0, 2, 3).reshape(NPOOL, HD, HD)
            return cw_full[0:1], cw_full[1:2], cw_full[2:3], w_co_g.reshape(D, D), w_pool_full, w_kv_g
        if group == "merge":
            return got[0].reshape(D, D), got[1].reshape(D, D)
        return [g.reshape(DFF, D) for g in got]

    started = {}

    def put(group, grads):
        if group == "ffn":
            sends = [g.reshape(NDEV, ffb, D) for g in grads]
        elif group == "mix":
            dw_co, dw_xo, dw_o, dw_pool, dw_kv = grads
            sends = [dw_co.reshape(NDEV, rows, D), dw_xo.reshape(NDEV, rows, D), dw_o.reshape(NDEV, rows, D),
                     dw_pool.reshape(NPOOL, NDEV, prow, HD).transpose(1, 0, 2, 3).reshape(NDEV, NPOOL * prow, HD), dw_kv]
        else:
            sends = list(grads)
        n = len(sends)
        halves = [lax.empty((4,) + s.shape[1:], s.dtype) for s in sends]
        (handle,), token = _copies_start([(sends + halves, _plan_scatter_sibling(n))], "scatter_swap_" + group, norm_mix)
        swapping.append((group, handle, n))
        return token

    swapping = []

    def flush(after):
        group, handle, n = swapping.pop()
        bufs = _copies_wait(handle, _plan_scatter_sibling(n), "scatter_swapped_" + group, after)
        sums, lands = _pair_sums(bufs[:n], bufs[n:], cc, chip, "pair_sums_" + group)
        (handle,), token = _copies_start([(sums + lands, _plan_scatter_chips(n))], "scatter_start_" + group, norm_mix)
        started[group] = (handle, _plan_scatter_chips(n))
        return token

    def take(group, after):
        handle, plan = started[group]
        return _copies_wait(handle, plan, "scatter_wait_" + group, after)[len(handle[2]) // 2:]

    gains = (norm_mix, pool_scale, norm_mem, norm_ffn, norm_final.reshape(1, D))
    grad_x, small = _local_step(x[0], mem[0], loss_target[0], gains, get, put, flush)

    everyone = _plan_gather_chips(1, tuple(range(1, NDEV)))
    (small_handle,), token = _copies_start([([small, land(small, me, NDEV)], everyone)], "small_start", norm_mix)

    res = {}

    def update(group, names, ws, gs, ms, vs, from_parts, steps, transposed=()):
        view = lambda a, name: a[0].T if name in transposed else a
        flat = [[view(a, name).reshape(g.shape[-2:]) for a in (w, m, v)] for name, w, g, m, v in zip(names, ws, gs, ms, vs)]
        outs = _adamw([f[0] for f in flat], gs, [f[1] for f in flat], [f[2] for f in flat], "adamw_" + group,
                      from_parts, steps)
        for name, w, four in zip(names, ws, outs):
            res[name] = [(o.T if name in transposed else o).reshape(w.shape) for o in four]

    p_g, p_u, p_d = take("ffn", token)
    update("ffn", ["w_gate", "w_up", "w_down"], [w_gate, w_up, w_down], [p_g, p_u, p_d],
           [m_w_gate, m_w_up, m_w_down], [v_w_gate, v_w_up, v_w_down], True, 2, transposed=("w_gate", "w_up"))

    small_all = _copies_wait(small_handle, everyone, "small_wait", res["w_down"][1])[1]
    small_sum = _sum_parts(small_all, "sum_small")
    loss = small_sum[8, 0]
    g_cw = lax.dynamic_slice_in_dim(small_sum[5:8], me * rows, rows, axis=1)
    update("replicated", ["norm_mix", "pool_scale", "norm_mem", "norm_ffn", "norm_final", "conv_w"],
           [norm_mix, pool_scale, norm_mem, norm_ffn, norm_final, conv_w], [small_sum[k:k + 1] for k in range(5)] + [g_cw],
           [m_norm_mix, m_pool_scale, m_norm_mem, m_norm_ffn, m_norm_final, m_conv_w],
           [v_norm_mix, v_pool_scale, v_norm_mem, v_norm_ffn, v_norm_final, v_conv_w], False, 1)

    p_co, p_xo, p_o, p_pool, p_kv = take("mix", res["conv_w"][1])
    update("mix", ["w_conv_out", "w_xattn_out", "w_out", "w_pool", "w_kv"], [w_conv_out, w_xattn_out, w_out, w_pool, w_kv],
           [p_co, p_xo, p_o, p_pool, p_kv], [m_w_conv_out, m_w_xattn_out, m_w_out, m_w_pool, m_w_kv],
           [v_w_conv_out, v_w_xattn_out, v_w_out, v_w_pool, v_w_kv], True, 2)
    (p_in,) = take("in", res["w_out"][1])
    update("in", ["w_in"], [w_in], [p_in], [m_w_in], [v_w_in], True, 4)
    order = ["norm_mix", "w_in", "conv_w", "w_conv_out", "w_pool", "pool_scale", "norm_mem", "w_kv", "w_xattn_out", "w_out",
             "norm_ffn", "w_gate", "w_up", "w_down", "norm_final"]
    return (loss, grad_x[None], *[res[n][0] for n in order], *[res[n][1] for n in order],
            *[res[n][2] for n in order], *[res[n][3] for n in order])
```

```python
import jax
import jax.numpy as jnp
from jax import lax
from jax.experimental import pallas as pl
from jax.experimental.pallas import tpu as pltpu

F32 = jnp.float32
BF16 = jnp.bfloat16
SDS = jax.ShapeDtypeStruct

AXES = ("x", "y", "c")
NDEV = 8
D = 1024
NSPLIT = 8
NH = 4
HD = D // NH
NPOOL = 4
DFF = 2816
EPS = 1e-6
ATT_SCALE = HD ** -0.5
HALO = 16


def _slot_group(s):
    return jnp.where(s < 3, s + 5, jnp.where(s == 3, 4, s - 4))


ADAM_LR = 0.001
ADAM_B1 = 0.9
ADAM_B2 = 0.999
ADAM_EPS = 1e-08
ADAM_WD = 0.01
ADAM_STEP = 10

V7X_VMEM_BYTES = 64 * 1024 * 1024
VMEM_LIMIT = V7X_VMEM_BYTES - 8 * 1024 * 1024
HBM = pl.BlockSpec(memory_space=pl.ANY)


def _whole(shape):
    return pl.BlockSpec(shape, lambda *_: (0,) * len(shape), pipeline_mode=pl.Buffered(1))


def _params(n_grid):
    return pltpu.CompilerParams(dimension_semantics=("arbitrary",) * n_grid, vmem_limit_bytes=VMEM_LIMIT)


def _mm(a, b):
    return jnp.dot(a, b, preferred_element_type=F32)


def _mm_nt(a, b):
    return lax.dot_general(a, b, (((1,), (1,)), ((), ())), preferred_element_type=F32)


def _mm_tn(a, b):
    return lax.dot_general(a, b, (((0,), (0,)), ((), ())), preferred_element_type=F32)


def _sigmoid(x):
    return 1.0 / (1.0 + jnp.exp(-x))


def _rms(x):
    return lax.rsqrt(jnp.mean(x * x, axis=-1, keepdims=True) + EPS)


def _norm_bwd(dh, x, gain):
    r = _rms(x)
    xh = x * r
    dxh = dh * gain
    dx = r * (dxh - xh * jnp.mean(dxh * xh, axis=-1, keepdims=True))
    return dx, jnp.sum(dh * xh, axis=0, keepdims=True)


def _col_chunks(n, width=512):
    return [slice(c, min(c + width, n)) for c in range(0, n, width)]


def _shift_down(v, k):
    return pltpu.roll(v, k, 0)


def _shift_up(v, k):
    return pltpu.roll(v, v.shape[0] - k, 0)


def _fwd_norm(x, gain, tm):
    T = x.shape[0]

    def body(x_ref, g_ref, h_ref):
        xf = x_ref[...]
        h_ref[...] = (xf * _rms(xf) * g_ref[...]).astype(BF16)

    return pl.pallas_call(
        body, name="fwd_norm", grid=(T // tm,),
        in_specs=[pl.BlockSpec((tm, D), lambda i: (i, 0)), pl.BlockSpec((1, D), lambda i: (0, 0))],
        out_specs=pl.BlockSpec((tm, D), lambda i: (i, 0)), out_shape=SDS((T, D), BF16),
        compiler_params=_params(1))(x, gain)


def _fwd_proj(h, w_blocks, proj, w_ids, p_ids, tm, name):
    T = h.shape[0]

    def body(w_ids_ref, p_ids_ref, h_ref, w_ref, proj_hbm, proj_ref):
        del w_ids_ref, p_ids_ref, proj_hbm
        proj_ref[...] = _mm(h_ref[...], w_ref[...]).astype(BF16)

    return pl.pallas_call(
        body, name=name,
        grid_spec=pltpu.PrefetchScalarGridSpec(
            num_scalar_prefetch=2, grid=(T // tm, w_ids.shape[0]),
            in_specs=[pl.BlockSpec((tm, D), lambda i, j, w, p: (i, 0)),
                      pl.BlockSpec((None, D, D), lambda i, j, w, p: (w[j], 0, 0)), HBM],
            out_specs=pl.BlockSpec((None, tm, D), lambda i, j, w, p: (p[j], i, 0))),
        out_shape=SDS(proj.shape, BF16), input_output_aliases={4: 0},
        compiler_params=_params(2))(w_ids, p_ids, h, w_blocks, proj)


def _halo_before(split, tm):
    return pl.BlockSpec((None, HALO, D), lambda i: (split, jnp.maximum(i * (tm // HALO) - 1, 0), 0))


def _fwd_mix(proj, cw0, cw1, cw2, w_co, w_pool, mem, gain_mem, w_kv, tm):
    T = proj.shape[1]
    M = mem.shape[0]

    def body(b_ref, c_ref, ua_ref, up_ref, ch_ref, uah_ref, uph_ref, cw0_ref, cw1_ref, cw2_ref, wco_ref, wp_ref,
             mem_ref, gm_ref, wkv_ref, za_ref, conv_ref, pooled_ref, ya_ref, yp_ref, kv_ref, memn_ref):
        i = pl.program_id(0)

        @pl.when(i == 0)
        def _():
            m = mem_ref[...]
            memn = (m * _rms(m) * gm_ref[...]).astype(BF16)
            memn_ref[...] = memn
            for j in range(2 * NH):
                kv_ref[j] = _mm(memn, wkv_ref[j]).astype(BF16)
        keep = jnp.where(i > 0, 1.0, 0.0).astype(F32)
        cu = c_ref[...].astype(F32) * ua_ref[...].astype(F32)
        cu_h = ch_ref[...].astype(F32) * uah_ref[...].astype(F32) * keep
        ext = jnp.concatenate([cu_h, cu], axis=0)
        conv = (cw2_ref[...] * ext + cw1_ref[...] * _shift_down(ext, 1) + cw0_ref[...] * _shift_down(ext, 2))[HALO:]
        za = (b_ref[...].astype(F32) * conv).astype(BF16)
        conv_ref[...] = conv.astype(BF16)
        za_ref[...] = za
        ya_ref[...] = _mm(za, wco_ref[...]).astype(BF16)

        up = up_ref[...].astype(F32)
        ext_u = jnp.concatenate([uph_ref[...].astype(F32) * keep, up], axis=0)
        pos = i * tm + lax.broadcasted_iota(jnp.int32, (tm, HD), 0)
        for g in range(NPOOL):
            cols = slice(g * HD, (g + 1) * HD)
            s = ext_u[:, cols]
            for k in range(g + 1):
                s = s + _shift_down(s, 1 << k)
            cnt = jnp.minimum(pos + 1, 2 << g).astype(F32)
            pooled = (s[HALO:] / cnt - up[:, cols]).astype(BF16)
            pooled_ref[:, cols] = pooled
            yp_ref[:, cols] = _mm(pooled, wp_ref[g]).astype(BF16)

    tile = lambda s: pl.BlockSpec((None, tm, D), lambda i: (s, i, 0))
    row = pl.BlockSpec((1, D), lambda i: (0, 0))
    out = pl.BlockSpec((tm, D), lambda i: (i, 0))
    return pl.pallas_call(
        body, name="fwd_mix", grid=(T // tm,),
        in_specs=[tile(0), tile(1), tile(2), tile(3), _halo_before(1, tm), _halo_before(2, tm), _halo_before(3, tm),
                  row, row, row, _whole((D, D)), _whole((NPOOL, HD, HD)), _whole((M, D)), row, _whole((2 * NH, D, HD))],
        out_specs=[out] * 5 + [pl.BlockSpec((2 * NH, M, HD), lambda i: (0, 0, 0)), pl.BlockSpec((M, D), lambda i: (0, 0))],
        out_shape=[SDS((T, D), BF16)] * 5 + [SDS((2 * NH, M, HD), BF16), SDS((M, D), BF16)],
        compiler_params=_params(1))(proj, proj, proj, proj, proj, proj, proj, cw0, cw1, cw2, w_co, w_pool, mem, gain_mem, w_kv)


def _softmax_rows(s):
    e = jnp.exp(s - jnp.max(s, axis=-1, keepdims=True))
    return e / jnp.sum(e, axis=-1, keepdims=True)


def _fwd_merge(proj, ya, yp, x, kv, w_xo, w_o, pscale, gain_ffn, tm):
    T = x.shape[0]

    def body(q_ref, ga_ref, gp_ref, gx_ref, ya_ref, yp_ref, x_ref, kv_ref, wxo_ref, wo_ref, ps_ref, gf_ref,
             o_ref, yx_ref, merged_ref, x1_ref, h2_ref):
        for h in range(NH):
            cols = slice(h * HD, (h + 1) * HD)
            p = _softmax_rows(_mm_nt(q_ref[:, cols], kv_ref[h]) * ATT_SCALE)
            o_ref[:, cols] = _mm(p.astype(BF16), kv_ref[NH + h]).astype(BF16)
        yx = _mm(o_ref[...], wxo_ref[...])
        yx_ref[...] = yx.astype(BF16)
        merged = (_sigmoid(ga_ref[...].astype(F32)) * ya_ref[...].astype(F32)
                  + _sigmoid(gp_ref[...].astype(F32)) * (yp_ref[...].astype(F32) * ps_ref[...])
                  + _sigmoid(gx_ref[...].astype(F32)) * yx).astype(BF16)
        merged_ref[...] = merged
        x1 = x_ref[...] + _mm(merged, wo_ref[...])
        x1_ref[...] = x1
        h2_ref[...] = (x1 * _rms(x1) * gf_ref[...]).astype(BF16)

    tile = lambda s: pl.BlockSpec((None, tm, D), lambda i: (s, i, 0))
    row = pl.BlockSpec((1, D), lambda i: (0, 0))
    act = pl.BlockSpec((tm, D), lambda i: (i, 0))
    full = _whole((D, D))
    return pl.pallas_call(
        body, name="fwd_merge", grid=(T // tm,),
        in_specs=[tile(4), tile(5), tile(6), tile(7), act, act, act,
                  _whole((2 * NH, kv.shape[1], HD)), full, full, row, row],
        out_specs=[act] * 5,
        out_shape=[SDS((T, D), BF16), SDS((T, D), BF16), SDS((T, D), BF16), SDS((T, D), F32), SDS((T, D), BF16)],
        compiler_params=_params(1))(proj, proj, proj, proj, ya, yp, x, kv, w_xo, w_o, pscale, gain_ffn)


def _fwd_ffn_up(h2, wg_t, wu_t, tm, tn):
    T = h2.shape[0]

    def body(h_ref, wg_ref, wu_ref, gate_ref, up_ref, act_ref):
        for cols in _col_chunks(tn):
            gate = _mm_nt(h_ref[...], wg_ref[cols, :])
            up = _mm_nt(h_ref[...], wu_ref[cols, :])
            gate_ref[:, cols] = gate.astype(BF16)
            up_ref[:, cols] = up.astype(BF16)
            act_ref[:, cols] = (gate * _sigmoid(gate) * up).astype(BF16)

    w = pl.BlockSpec((tn, D), lambda n, i: (n, 0))
    o = pl.BlockSpec((tm, tn), lambda n, i: (i, n))
    return pl.pallas_call(
        body, name="fwd_ffn_up", grid=(DFF // tn, T // tm),
        in_specs=[pl.BlockSpec((tm, D), lambda n, i: (i, 0)), w, w],
        out_specs=[o] * 3, out_shape=[SDS((T, DFF), BF16)] * 3,
        compiler_params=_params(2))(h2, wg_t, wu_t)


def _fwd_ffn_down_loss(act, w_d, x1, target, gain_final, tm):
    T = x1.shape[0]

    def body(act_ref, wd_ref, x1_ref, tgt_ref, g_ref, dx2_ref, loss_ref, dgain_ref):
        @pl.when(pl.program_id(0) == 0)
        def _():
            loss_ref[...] = jnp.zeros_like(loss_ref)
            dgain_ref[...] = jnp.zeros_like(dgain_ref)
        x2 = x1_ref[...] + _mm(act_ref[...], wd_ref[...])
        gain = g_ref[...]
        y = x2 * _rms(x2) * gain
        err = y - tgt_ref[...]
        loss_ref[...] += 0.5 * jnp.sum(jnp.mean(err * err, axis=-1, keepdims=True))
        dx2, dgain = _norm_bwd(err * (1.0 / D), x2, gain)
        dx2_ref[...] = dx2
        dgain_ref[...] += dgain

    act_spec = pl.BlockSpec((tm, D), lambda i: (i, 0))
    row = pl.BlockSpec((1, D), lambda i: (0, 0))
    return pl.pallas_call(
        body, name="fwd_ffn_down_loss", grid=(T // tm,),
        in_specs=[pl.BlockSpec((tm, DFF), lambda i: (i, 0)), _whole((DFF, D)), act_spec, act_spec, row],
        out_specs=[act_spec, pl.BlockSpec((8, D), lambda i: (0, 0)), row],
        out_shape=[SDS((T, D), F32), SDS((8, D), F32), SDS((1, D), F32)],
        compiler_params=_params(1))(act, w_d, x1, target, gain_final)


def _bwd_ffn_down(dx2, w_d, gate, up, tm, tn):
    T = dx2.shape[0]

    def body(dx_ref, wd_ref, gate_ref, up_ref, dgate_ref, dup_ref):
        dx = dx_ref[...].astype(BF16)
        for cols in _col_chunks(tn):
            dact = _mm_nt(dx, wd_ref[cols, :])
            gate = gate_ref[:, cols].astype(F32)
            sg = _sigmoid(gate)
            dgate_ref[:, cols] = (dact * up_ref[:, cols].astype(F32) * (sg * (1.0 + gate * (1.0 - sg)))).astype(BF16)
            dup_ref[:, cols] = (dact * gate * sg).astype(BF16)

    o = pl.BlockSpec((tm, tn), lambda n, i: (i, n))
    return pl.pallas_call(
        body, name="bwd_ffn_down", grid=(DFF // tn, T // tm),
        in_specs=[pl.BlockSpec((tm, D), lambda n, i: (i, 0)), pl.BlockSpec((tn, D), lambda n, i: (n, 0)), o, o],
        out_specs=[o] * 2, out_shape=[SDS((T, DFF), BF16)] * 2,
        compiler_params=_params(2))(dx2, w_d, gate, up)


def _bwd_ffn_up(dgate, dup, wg_t, wu_t, x1, dx2, gain_ffn, tm):
    T = x1.shape[0]

    def body(dg_ref, du_ref, wg_ref, wu_ref, x1_ref, dx2_ref, g_ref, dx1_ref, dgain_ref):
        @pl.when(pl.program_id(0) == 0)
        def _():
            dgain_ref[...] = jnp.zeros_like(dgain_ref)
        dh2 = _mm(dg_ref[...], wg_ref[...]) + _mm(du_ref[...], wu_ref[...])
        dx, dgain = _norm_bwd(dh2, x1_ref[...], g_ref[...])
        dx1_ref[...] = dx2_ref[...] + dx
        dgain_ref[...] += dgain

    wide = pl.BlockSpec((tm, DFF), lambda i: (i, 0))
    w = _whole((DFF, D))
    act = pl.BlockSpec((tm, D), lambda i: (i, 0))
    row = pl.BlockSpec((1, D), lambda i: (0, 0))
    return pl.pallas_call(
        body, name="bwd_ffn_up", grid=(T // tm,),
        in_specs=[wide, wide, w, w, act, act, row], out_specs=[act, row],
        out_shape=[SDS((T, D), F32), SDS((1, D), F32)],
        compiler_params=_params(1))(dgate, dup, wg_t, wu_t, x1, dx2, gain_ffn)


def _wgrad(a, b, *, name, groups, a_cols, b_cols, tt, a_index, b_index, o_index, out_shape, after):
    T = a.shape[0]
    nt = T // tt
    n_a = a.shape[1] // a_cols if groups == 1 else 1

    def body(a_ref, b_ref, after_ref, o_ref, acc_ref):
        del after_ref
        t = pl.program_id(2)

        @pl.when(t == 0)
        def _():
            acc_ref[...] = jnp.zeros_like(acc_ref)
        acc_ref[...] += _mm_tn(a_ref[...].astype(BF16), b_ref[...].astype(BF16))

        @pl.when(t == nt - 1)
        def _():
            o_ref[...] = acc_ref[...].astype(o_ref.dtype)

    return pl.pallas_call(
        body, name=name, grid=(groups, n_a, nt),
        in_specs=[pl.BlockSpec((tt, a_cols), a_index), pl.BlockSpec((None, tt, b_cols), b_index), HBM],
        out_specs=pl.BlockSpec((None, a_cols, b_cols), o_index),
        out_shape=SDS(out_shape, BF16),
        scratch_shapes=[pltpu.VMEM((a_cols, b_cols), F32)],
        compiler_params=_params(3))(a, b, after)


def _wgrad_dense(a, b, name, tt, after, a_cols=None):
    ka, nb = a.shape[1], b.shape[1]
    a_cols = ka if a_cols is None else a_cols
    out = _wgrad(a, b[None], name=name, groups=1, a_cols=a_cols, b_cols=nb, tt=tt,
                 a_index=lambda g, k, t: (t, k), b_index=lambda g, k, t: (0, t, 0),
                 o_index=lambda g, k, t: (k, 0, 0), out_shape=(ka // a_cols, a_cols, nb), after=after)
    return out.reshape(ka, nb)


def _bwd_merge(dx1, proj, ya, yp, yx, pooled, pscale, w_o, w_co, w_xo, w_pool, tm, after):
    T = dx1.shape[0]
    nt = T // tm

    def body(dx1_ref, ga_ref, gp_ref, gx_ref, ya_ref, yp_ref, yx_ref, pooled_ref, ps_ref, wo_ref, wco_ref, wxo_ref, wp_ref,
             after_ref, dgates_ref, dya_ref, dyx_ref, dza_ref, do_ref, dpooled_ref, dps_ref, dwp_ref, acc_ref):
        del after_ref

        @pl.when(pl.program_id(0) == 0)
        def _():
            dps_ref[...] = jnp.zeros_like(dps_ref)
            acc_ref[...] = jnp.zeros_like(acc_ref)
        dmerged = _mm_nt(dx1_ref[...].astype(BF16), wo_ref[...])
        scale = ps_ref[...]
        sa, sp, sx = (_sigmoid(r[...].astype(F32)) for r in (ga_ref, gp_ref, gx_ref))
        ya, yp_pre, yx = (r[...].astype(F32) for r in (ya_ref, yp_ref, yx_ref))
        dgates_ref[0] = (dmerged * ya * sa * (1.0 - sa)).astype(BF16)
        dgates_ref[1] = (dmerged * (yp_pre * scale) * sp * (1.0 - sp)).astype(BF16)
        dgates_ref[2] = (dmerged * yx * sx * (1.0 - sx)).astype(BF16)
        dya = (dmerged * sa).astype(BF16)
        dyx = (dmerged * sx).astype(BF16)
        dyp = dmerged * sp
        dyps = (dyp * scale).astype(BF16)
        dps_ref[...] += jnp.sum(dyp * yp_pre, axis=0, keepdims=True)
        dya_ref[...] = dya
        dyx_ref[...] = dyx
        dza_ref[...] = _mm_nt(dya, wco_ref[...]).astype(BF16)
        do_ref[...] = _mm_nt(dyx, wxo_ref[...]).astype(BF16)
        for g in range(NPOOL):
            cols = slice(g * HD, (g + 1) * HD)
            dpooled_ref[:, cols] = _mm_nt(dyps[:, cols], wp_ref[g]).astype(BF16)
            acc_ref[g] += _mm_tn(pooled_ref[:, cols], dyps[:, cols])

        @pl.when(pl.program_id(0) == nt - 1)
        def _():
            dwp_ref[...] = acc_ref[...].astype(BF16)

    tile = lambda s: pl.BlockSpec((None, tm, D), lambda i: (s, i, 0))
    row = pl.BlockSpec((1, D), lambda i: (0, 0))
    act = pl.BlockSpec((tm, D), lambda i: (i, 0))
    full = _whole((D, D))
    return pl.pallas_call(
        body, name="bwd_merge", grid=(T // tm,),
        in_specs=[act, tile(5), tile(6), tile(7), act, act, act, act, row, full, full, full,
                  _whole((NPOOL, HD, HD)), HBM],
        out_specs=[pl.BlockSpec((3, tm, D), lambda i: (0, i, 0))] + [act] * 5
        + [row, pl.BlockSpec((NPOOL, HD, HD), lambda i: (0, 0, 0))],
        out_shape=[SDS((NSPLIT, T, D), BF16)] + [SDS((T, D), BF16)] * 5 + [SDS((1, D), F32), SDS((NPOOL, HD, HD), BF16)],
        scratch_shapes=[pltpu.VMEM((NPOOL, HD, HD), F32)],
        compiler_params=_params(1))(dx1, proj, proj, proj, ya, yp, yx, pooled, pscale, w_o, w_co, w_xo, w_pool, after)


def _bwd_attn(dproj, proj, do, kv, memn, w_kv, mem, gain_mem, tm):
    T = do.shape[0]
    M = kv.shape[1]
    nt = T // tm

    def body(dproj_hbm, q_ref, do_ref, kv_ref, memn_ref, wkv_ref, mem_ref, gm_ref, dq_ref, dw_ref, dgain_ref, dkv_ref):
        del dproj_hbm

        @pl.when(pl.program_id(0) == 0)
        def _():
            dkv_ref[...] = jnp.zeros_like(dkv_ref)
        for h in range(NH):
            cols = slice(h * HD, (h + 1) * HD)
            q = q_ref[:, cols]
            do_h = do_ref[:, cols]
            p = _softmax_rows(_mm_nt(q, kv_ref[h]) * ATT_SCALE)
            dp = _mm_nt(do_h, kv_ref[NH + h])
            ds = (p * (dp - jnp.sum(dp * p, axis=-1, keepdims=True)) * ATT_SCALE).astype(BF16)
            dq_ref[:, cols] = _mm(ds, kv_ref[h]).astype(BF16)
            dkv_ref[h] += _mm_tn(ds, q)
            dkv_ref[NH + h] += _mm_tn(p.astype(BF16), do_h)

        @pl.when(pl.program_id(0) == nt - 1)
        def _():
            dmemn = jnp.zeros((M, D), F32)
            for j in range(2 * NH):
                dkv_j = dkv_ref[j].astype(BF16)
                dw_ref[j] = _mm_tn(memn_ref[...], dkv_j).astype(BF16)
                dmemn = dmemn + _mm_nt(dkv_j, wkv_ref[j])
            dgain_ref[...] = _norm_bwd(dmemn, mem_ref[...], gm_ref[...])[1]

    row = pl.BlockSpec((1, D), lambda i: (0, 0))
    return pl.pallas_call(
        body, name="bwd_attn", grid=(nt,),
        in_specs=[HBM, pl.BlockSpec((None, tm, D), lambda i: (4, i, 0)), pl.BlockSpec((tm, D), lambda i: (i, 0)),
                  _whole((2 * NH, M, HD)), _whole((M, D)), _whole((2 * NH, D, HD)), _whole((M, D)), row],
        out_specs=[pl.BlockSpec((None, tm, D), lambda i: (3, i, 0)),
                   pl.BlockSpec((2 * NH, D, HD), lambda i: (0, 0, 0)), row],
        out_shape=[SDS(dproj.shape, BF16), SDS((2 * NH, D, HD), BF16), SDS((1, D), F32)],
        scratch_shapes=[pltpu.VMEM((2 * NH, M, HD), F32)],
        input_output_aliases={0: 0},
        compiler_params=_params(1))(dproj, proj, do, kv, memn, w_kv, mem, gain_mem)


def _bwd_mix(dproj, proj, conv, dza, dpooled, cw0, cw1, cw2, tm, after):
    T = dza.shape[0]
    nt = T // tm

    def halo_after(split_or_none):
        idx = lambda i: jnp.minimum((i + 1) * (tm // HALO), T // HALO - 1)
        if split_or_none is None:
            return pl.BlockSpec((HALO, D), lambda i: (idx(i), 0))
        return pl.BlockSpec((None, HALO, D), lambda i: (split_or_none, idx(i), 0))

    def body(dproj_hbm, b_ref, c_ref, ua_ref, conv_ref, dza_ref, dpo_ref, bn_ref, dzan_ref, dpon_ref,
             cw0_ref, cw1_ref, cw2_ref, after_ref, dabcu_ref, dcw_ref):
        del dproj_hbm, after_ref
        i = pl.program_id(0)

        @pl.when(i == 0)
        def _():
            dcw_ref[...] = jnp.zeros_like(dcw_ref)
        keep_next = jnp.where(i < nt - 1, 1.0, 0.0).astype(F32)
        dza = dza_ref[...].astype(F32)
        c = c_ref[...].astype(F32)
        ua = ua_ref[...].astype(F32)
        dconv = dza * b_ref[...].astype(F32)
        dconv_n = dzan_ref[...].astype(F32) * bn_ref[...].astype(F32) * keep_next
        ext = jnp.concatenate([dconv, dconv_n], axis=0)
        dconv_1, dconv_2 = _shift_up(ext, 1)[:tm], _shift_up(ext, 2)[:tm]
        dcu = cw2_ref[...] * dconv + cw1_ref[...] * dconv_1 + cw0_ref[...] * dconv_2
        dabcu_ref[0] = (dza * conv_ref[...].astype(F32)).astype(BF16)
        dabcu_ref[1] = (dcu * ua).astype(BF16)
        dabcu_ref[2] = (dcu * c).astype(BF16)

        cu = c * ua
        dcw_ref[2:3, :] += jnp.sum(dconv * cu, axis=0, keepdims=True)
        dcw_ref[1:2, :] += jnp.sum(dconv_1 * cu, axis=0, keepdims=True)
        dcw_ref[0:1, :] += jnp.sum(dconv_2 * cu, axis=0, keepdims=True)

        dpo = dpo_ref[...].astype(F32)
        ext_dpo = jnp.concatenate([dpo, dpon_ref[...].astype(F32) * keep_next], axis=0)
        pos = i * tm + lax.broadcasted_iota(jnp.int32, (tm + HALO, HD), 0)
        for g in range(NPOOL):
            cols = slice(g * HD, (g + 1) * HD)
            s = ext_dpo[:, cols] / jnp.minimum(pos + 1, 2 << g).astype(F32)
            for k in range(g + 1):
                s = s + _shift_up(s, 1 << k)
            dabcu_ref[3, :, cols] = (s[:tm] - dpo[:, cols]).astype(BF16)

    tile = lambda s: pl.BlockSpec((None, tm, D), lambda i: (s, i, 0))
    act = pl.BlockSpec((tm, D), lambda i: (i, 0))
    row = pl.BlockSpec((1, D), lambda i: (0, 0))
    return pl.pallas_call(
        body, name="bwd_mix", grid=(nt,),
        in_specs=[HBM, tile(0), tile(1), tile(2), act, act, act, halo_after(0), halo_after(None), halo_after(None),
                  row, row, row, HBM],
        out_specs=[pl.BlockSpec((4, tm, D), lambda i: (1, i, 0)), pl.BlockSpec((8, D), lambda i: (0, 0))],
        out_shape=[SDS(dproj.shape, BF16), SDS((8, D), F32)],
        input_output_aliases={0: 0},
        compiler_params=_params(1))(dproj, proj, proj, proj, conv, dza, dpooled, proj, dza, dpooled, cw0, cw1, cw2, after)


def _bwd_proj(dproj, w_in_g, x, dx1, gain, tm, after):
    T = x.shape[0]

    def body(dp_ref, w_ref, x_ref, dx1_ref, g_ref, after_ref, dx_ref, dgain_ref, acc_ref):
        del after_ref
        i, s = pl.program_id(0), pl.program_id(1)

        @pl.when((i == 0) & (s == 0))
        def _():
            dgain_ref[...] = jnp.zeros_like(dgain_ref)

        @pl.when(s == 0)
        def _():
            acc_ref[...] = jnp.zeros_like(acc_ref)
        acc_ref[...] += _mm_nt(dp_ref[...], w_ref[...])

        @pl.when(s == NSPLIT - 1)
        def _():
            dx, dgain = _norm_bwd(acc_ref[...], x_ref[...], g_ref[...])
            dx_ref[...] = dx1_ref[...] + dx
            dgain_ref[...] += dgain

    act = pl.BlockSpec((tm, D), lambda i, s: (i, 0))
    row = pl.BlockSpec((1, D), lambda i, s: (0, 0))
    return pl.pallas_call(
        body, name="bwd_proj", grid=(T // tm, NSPLIT),
        in_specs=[pl.BlockSpec((None, tm, D), lambda i, s: (s, i, 0)),
                  pl.BlockSpec((None, D, D), lambda i, s: (_slot_group(s), 0, 0)), act, act, row, HBM],
        out_specs=[act, row], out_shape=[SDS((T, D), F32), SDS((1, D), F32)],
        scratch_shapes=[pltpu.VMEM((tm, D), F32)],
        compiler_params=_params(2))(dproj, w_in_g, x, dx1, gain, after)


def _adamw_math(w, g, m, v):
    m = ADAM_B1 * m + (1.0 - ADAM_B1) * g
    v = ADAM_B2 * v + (1.0 - ADAM_B2) * (g * g)
    m_hat = m / (1.0 - ADAM_B1 ** ADAM_STEP)
    v_hat = v / (1.0 - ADAM_B2 ** ADAM_STEP)
    delta = -ADAM_LR * (m_hat / (jnp.sqrt(v_hat) + ADAM_EPS) + ADAM_WD * w)
    return delta, m, v


def _row_tile(rows):
    return 256 if rows % 256 == 0 else rows


def _sum_parts(parts, name):
    n_parts, rows, cols = parts.shape
    tr = _row_tile(rows)

    def body(p_ref, g_ref):
        g = p_ref[0].astype(F32)
        for k in range(1, n_parts):
            g = g + p_ref[k].astype(F32)
        g_ref[...] = g

    blk = pl.BlockSpec((tr, cols), lambda i: (i, 0))
    return pl.pallas_call(
        body, name=name, grid=(rows // tr,),
        in_specs=[pl.BlockSpec((n_parts, tr, cols), lambda i: (0, i, 0))], out_specs=blk,
        out_shape=SDS((rows, cols), F32), compiler_params=_params(1))(parts)


def _adamw(ws, gs, ms, vs, name, from_parts, steps):
    n = len(ws)

    def body(*refs):
        for a in range(n):
            w_ref, g_ref, m_ref, v_ref = refs[4 * a:4 * a + 4]
            go_ref, d_ref, mo_ref, vo_ref = refs[4 * n + 4 * a:4 * n + 4 * a + 4]
            if from_parts:
                g = g_ref[0].astype(F32)
                for k in range(1, g_ref.shape[0]):
                    g = g + g_ref[k].astype(F32)
            else:
                g = g_ref[...]
            go_ref[...] = g
            d_ref[...], mo_ref[...], vo_ref[...] = _adamw_math(w_ref[...], g, m_ref[...], v_ref[...])

    in_specs, out_specs, out_shape, operands = [], [], [], []
    for w, g, m, v in zip(ws, gs, ms, vs):
        rows, cols = w.shape
        blk = pl.BlockSpec((rows // steps, cols), lambda i: (i, 0))
        g_spec = pl.BlockSpec((g.shape[0], rows // steps, cols), lambda i: (0, i, 0)) if from_parts else blk
        in_specs += [blk, g_spec, blk, blk]
        out_specs += [blk] * 4
        out_shape += [SDS((rows, cols), F32)] * 4
        operands += [w, g, m, v]
    outs = pl.pallas_call(body, name=name, grid=(steps,), in_specs=in_specs, out_specs=out_specs, out_shape=out_shape,
                          compiler_params=_params(1))(*operands)
    return [outs[4 * a:4 * a + 4] for a in range(n)]


def _peer(k, x, y, c):
    return ((1 - x) if k & 4 else x, (1 - y) if k & 2 else y, (1 - c) if k & 1 else c)


SEM = pl.BlockSpec(memory_space=pltpu.SEMAPHORE)
IN_HBM = pl.BlockSpec(memory_space=pltpu.HBM)
DATAFLOW = pltpu.SideEffectType.DATAFLOW_SIDE_EFFECTING
TOKEN_SHAPE = (8, 128)


OTHER_CHIPS = (2, 4, 6)


def _place(x, y, c):
    return 4 * x + 2 * y + c


def _plan_gather_chips(n, ks=(1,) + OTHER_CHIPS):
    def plan(refs, x, y, c, arriving):
        out = []
        for a in range(n):
            for k in ks:
                there = _place(*_peer(k, x, y, c))
                out.append((refs[a], refs[n + a].at[there if arriving else _place(x, y, c)], k))
        return out
    return plan, n * len(ks)


def _plan_gather_sibling(n, ks=OTHER_CHIPS):
    def plan(refs, x, y, c, arriving):
        out = []
        for a in range(n):
            for k in ks:
                px, py, pc = _peer(k, x, y, c)
                mine, theirs = _place(px, py, pc), _place(px, py, 1 - pc)
                out.append((refs[a].at[mine], refs[a].at[theirs if arriving else mine], 1))
        return out
    return plan, n * len(ks)


def _plan_pair():
    def plan(refs, x, y, c, arriving):
        return [(refs[0], refs[1].at[(1 - c) if arriving else c], 1)]
    return plan, 1


def _plan_far_chip():
    def plan(refs, x, y, c, arriving):
        return [(refs[0], refs[1].at[c], 6)]
    return plan, 1


def _plan_far_sibling():
    def plan(refs, x, y, c, arriving):
        return [(refs[0].at[c], refs[0].at[(1 - c) if arriving else c], 1)]
    return plan, 1


def _plan_scatter_sibling(n):
    def plan(refs, x, y, c, arriving):
        out = []
        for a in range(n):
            for q in range(4):
                out.append((refs[a].at[2 * q + (1 - c)], refs[n + a].at[q], 1))
        return out
    return plan, n * 4


def _plan_scatter_chips(n):
    def plan(refs, x, y, c, arriving):
        out = []
        for a in range(n):
            for k in OTHER_CHIPS:
                px, py, _ = _peer(k, x, y, c)
                out.append((refs[a].at[2 * px + py], refs[n + a].at[(2 * px + py) if arriving else (2 * x + y)], k))
        return out
    return plan, n * 3


def _remote(src, dst, send_sems, recv_sems, i, k):
    x, y, c = (lax.axis_index(n) for n in AXES)
    return pltpu.make_async_remote_copy(src_ref=src, dst_ref=dst, send_sem=send_sems.at[i], recv_sem=recv_sems.at[i],
                                        device_id=_peer(k, x, y, c), device_id_type=pl.DeviceIdType.MESH)


def _copies_start(groups, name, after):
    ng = len(groups)
    total = sum(len(bufs) for bufs, _ in groups)

    def body(*refs):
        sems = refs[1 + total:1 + total + 2 * ng]
        x, y, c = (lax.axis_index(n) for n in AXES)
        off = 1
        for gi, (bufs, (plan, _)) in enumerate(groups):
            for i, (src, dst, k) in enumerate(plan(refs[off:off + len(bufs)], x, y, c, False)):
                _remote(src, dst, sems[2 * gi], sems[2 * gi + 1], i, k).start()
            off += len(bufs)
        refs[-1][...] = jnp.zeros(TOKEN_SHAPE, F32)

    sem_shapes = [pltpu.SemaphoreType.DMA((count,)) for _, (_, count) in groups for _ in range(2)]
    flat = [b for bufs, _ in groups for b in bufs]
    outs = pl.pallas_call(
        body, name=name,
        in_specs=[HBM] + [IN_HBM] * total,
        out_specs=[SEM] * (2 * ng) + [IN_HBM] * total + [pl.BlockSpec(memory_space=pltpu.VMEM)],
        out_shape=sem_shapes + [pltpu.HBM(b.shape, b.dtype) for b in flat] + [SDS(TOKEN_SHAPE, F32)],
        input_output_aliases={1 + i: 2 * ng + i for i in range(total)},
        compiler_params=pltpu.CompilerParams(has_side_effects=DATAFLOW),
    )(after, *[pltpu.with_memory_space_constraint(b, pltpu.HBM) for b in flat])
    handles, off = [], 2 * ng
    for gi, (bufs, _) in enumerate(groups):
        handles.append((outs[2 * gi], outs[2 * gi + 1], list(outs[off:off + len(bufs)])))
        off += len(bufs)
    return handles, outs[-1]


def _copies_wait_start(handle, plan, pass_on, more, name, after):
    send_sems, recv_sems, bufs = handle
    n = len(bufs)
    idx, (pass_plan, pass_count) = pass_on
    total = sum(len(b) for b, _ in more)
    ng = 1 + len(more)

    def body(*refs):
        x, y, c = (lax.axis_index(a) for a in AXES)
        waited = refs[1:1 + n]
        outs = refs[3 + n + total:]
        new_sems = outs[n + total:n + total + 2 * ng]
        for i, (src, dst, k) in enumerate(plan[0](waited, x, y, c, True)):
            copy = _remote(src, dst, refs[1 + n + total], refs[2 + n + total], i, k)
            copy.wait_send()
            copy.wait_recv()
        for i, (src, dst, k) in enumerate(pass_plan([waited[j] for j in idx], x, y, c, False)):
            _remote(src, dst, new_sems[0], new_sems[1], i, k).start()
        off = 1 + n
        for gi, (b, (p, _)) in enumerate(more):
            for i, (src, dst, k) in enumerate(p(refs[off:off + len(b)], x, y, c, False)):
                _remote(src, dst, new_sems[2 + 2 * gi], new_sems[3 + 2 * gi], i, k).start()
            off += len(b)
        outs[-1][...] = jnp.zeros(TOKEN_SHAPE, F32)

    flat = list(bufs) + [a for b, _ in more for a in b]
    sem_shapes = [pltpu.SemaphoreType.DMA((count,)) for count in [pass_count] + [cnt for _, (_, cnt) in more] for _ in range(2)]
    outs = pl.pallas_call(
        body, name=name,
        in_specs=[HBM] + [IN_HBM] * (n + total) + [SEM, SEM],
        out_specs=[IN_HBM] * (n + total) + [SEM] * (2 * ng) + [pl.BlockSpec(memory_space=pltpu.VMEM)],
        out_shape=[pltpu.HBM(b.shape, b.dtype) for b in flat] + sem_shapes + [SDS(TOKEN_SHAPE, F32)],
        input_output_aliases={1 + i: i for i in range(n + total)},
        compiler_params=pltpu.CompilerParams(has_side_effects=DATAFLOW),
    )(after, *[pltpu.with_memory_space_constraint(b, pltpu.HBM) for b in flat], send_sems, recv_sems)
    thru = list(outs[:n])
    sems_out = outs[n + total:n + total + 2 * ng]
    handles = [(sems_out[0], sems_out[1], [thru[j] for j in idx])]
    off = n
    for gi, (b, _) in enumerate(more):
        handles.append((sems_out[2 + 2 * gi], sems_out[3 + 2 * gi], list(outs[off:off + len(b)])))
        off += len(b)
    return thru, handles, outs[-1]


def _copies_wait(handle, plan, name, after):
    send_sems, recv_sems, bufs = handle
    n = len(bufs)

    def body(*refs):
        x, y, c = (lax.axis_index(a) for a in AXES)
        for i, (src, dst, k) in enumerate(plan[0](refs[:n], x, y, c, True)):
            copy = _remote(src, dst, refs[n], refs[n + 1], i, k)
            copy.wait_send()
            copy.wait_recv()

    return pl.pallas_call(
        body, name=name,
        in_specs=[IN_HBM] * n + [SEM, SEM, HBM], out_specs=[IN_HBM] * n,
        out_shape=[pltpu.HBM(b.shape, b.dtype) for b in bufs],
        input_output_aliases={i: i for i in range(n)},
        compiler_params=pltpu.CompilerParams(has_side_effects=DATAFLOW),
    )(*bufs, send_sems, recv_sems, after)


def _pair_sums(mine, theirs, c, chip, name):
    n = len(mine)

    def body(where_ref, *refs):
        q = pl.program_id(0)
        for a in range(n):
            total = (refs[a][...].astype(F32) + refs[n + a][...].astype(F32)).astype(BF16)
            refs[2 * n + a][...] = total

            @pl.when(q == where_ref[1])
            def _():
                refs[3 * n + a][...] = total

    block = lambda t: (None,) + t.shape[1:]
    zeros = lambda t: (0,) * (t.ndim - 1)
    outs = pl.pallas_call(
        body, name=name,
        grid_spec=pltpu.PrefetchScalarGridSpec(
            num_scalar_prefetch=1, grid=(4,),
            in_specs=[pl.BlockSpec(block(t), lambda q, w, z=zeros(t): (2 * q + w[0],) + z) for t in theirs]
            + [pl.BlockSpec(block(t), lambda q, w, z=zeros(t): (q,) + z) for t in theirs],
            out_specs=[pl.BlockSpec(block(t), lambda q, w, z=zeros(t): (q,) + z) for t in theirs]
            + [pl.BlockSpec(block(t), lambda q, w, z=zeros(t): (w[1],) + z) for t in theirs]),
        out_shape=[SDS(t.shape, BF16) for t in theirs] * 2,
        compiler_params=_params(1))(jnp.stack([c, chip]).astype(jnp.int32), *mine, *theirs)
    return list(outs[:n]), list(outs[n:])


def _local_step(x, mem, target, gains, get, put, flush, tm_huge=2048, tm_big=1024, tm_mid=512, tm_small=256):
    g_mix, pscale, g_mem, g_ffn, g_fin = gains
    T = x.shape[0]
    tm_huge, tm_big, tm_mid, tm_small = min(tm_huge, T), min(tm_big, T), min(tm_mid, T), min(tm_small, T)
    tn = DFF // 2

    h = _fwd_norm(x, g_mix, tm_big)
    w_pair, w_ids, p_ids = get("in_pair", h)
    proj = _fwd_proj(h, w_pair, lax.empty((NSPLIT, T, D), BF16), w_ids, p_ids, tm_huge, "fwd_proj_pair")
    w_near, w_ids, p_ids = get("in_near", proj)
    proj = _fwd_proj(h, w_near, proj, w_ids, p_ids, tm_huge, "fwd_proj_near")
    w_far, w_ids, p_ids = get("in_far", proj)
    proj = _fwd_proj(h, w_far, proj, w_ids, p_ids, tm_huge, "fwd_proj_far")
    w_in = get("in_whole", (w_pair, w_near, w_far))
    cw0, cw1, cw2, w_co, w_pool, w_kv = get("mix", proj)
    za, conv, pooled, ya, yp, kv, memn = _fwd_mix(proj, cw0, cw1, cw2, w_co, w_pool, mem, g_mem, w_kv, tm_mid)
    w_xo, w_o = get("merge", ya)
    o, yx, merged, x1, h2 = _fwd_merge(proj, ya, yp, x, kv, w_xo, w_o, pscale, g_ffn, tm_mid)
    wg_t, wu_t = get("gate_up", x1)
    get("down", x1, early=True)
    gate, up, act = _fwd_ffn_up(h2, wg_t, wu_t, tm_big, tn)
    (w_d,) = get("down", gate)
    dx2, loss, dg_fin = _fwd_ffn_down_loss(act, w_d, x1, target, g_fin, tm_mid)

    dgate, dup = _bwd_ffn_down(dx2, w_d, gate, up, tm_big, tn)
    dx1, dg_ffn = _bwd_ffn_up(dgate, dup, wg_t, wu_t, x1, dx2, g_ffn, tm_small)
    dw_d = _wgrad_dense(act, dx2, "wgrad_down", tm_big, g_mix)
    dwg_t = _wgrad_dense(dgate, h2, "wgrad_gate", tm_big, g_mix)
    dwu_t = _wgrad_dense(dup, h2, "wgrad_up", tm_big, g_mix)
    token = put("ffn", (dwg_t, dwu_t, dw_d))

    dproj, dya, dyx, dza, do, dpooled, dpscale, dw_pool = _bwd_merge(
        dx1, proj, ya, yp, yx, pooled, pscale, w_o, w_co, w_xo, w_pool, tm_mid, token)
    token = flush(dya)
    dw_o = _wgrad_dense(merged, dx1, "wgrad_out", tm_big, token)
    dw_co = _wgrad_dense(za, dya, "wgrad_conv_out", tm_big, token)
    dw_xo = _wgrad_dense(o, dyx, "wgrad_xattn_out", tm_big, token)
    dproj, dw_kv, dg_mem = _bwd_attn(dproj, proj, do, kv, memn, w_kv, mem, g_mem, tm_big)
    token = put("mix", (dw_co, dw_xo, dw_o, dw_pool, dw_kv))

    dproj, dcw = _bwd_mix(dproj, proj, conv, dza, dpooled, cw0, cw1, cw2, tm_mid, token)
    token = flush(dcw)
    dw_in = _wgrad(h, dproj, name="wgrad_in", groups=NSPLIT, a_cols=D, b_cols=D, tt=tm_huge,
                   a_index=lambda g, k, t: (t, 0), b_index=lambda g, k, t: (g, t, 0),
                   o_index=lambda g, k, t: (_slot_group(g), 0, 0), out_shape=(NSPLIT, D, D), after=token)
    token = flush(put("in", (dw_in,)))
    grad_x, dg_mix = _bwd_proj(dproj, w_in, x, dx1, g_mix, tm_big, token)

    small = jnp.concatenate([dg_mix, dpscale, dg_mem, dg_ffn, dg_fin, dcw[0:3], loss], axis=0)
    return grad_x, small


def kernel(x, mem, norm_mix, w_in, conv_w, w_conv_out, w_pool, pool_scale, norm_mem, w_kv, w_xattn_out, w_out, norm_ffn, w_gate, w_up, w_down, norm_final, loss_target, m_norm_mix, m_w_in, m_conv_w, m_w_conv_out, m_w_pool, m_pool_scale, m_norm_mem, m_w_kv, m_w_xattn_out, m_w_out, m_norm_ffn, m_w_gate, m_w_up, m_w_down, m_norm_final, v_norm_mix, v_w_in, v_conv_w, v_w_conv_out, v_w_pool, v_pool_scale, v_norm_mem, v_w_kv, v_w_xattn_out, v_w_out, v_norm_ffn, v_w_gate, v_w_up, v_w_down, v_norm_final):
    T = x.shape[1]
    rows = D // NDEV
    ffb = DFF // NDEV
    prow = HD // NDEV
    me = 4 * lax.axis_index("x") + 2 * lax.axis_index("y") + lax.axis_index("c")

    shards = [w_in[0].astype(BF16), w_conv_out[0].astype(BF16), w_xattn_out[0].astype(BF16), w_out[0].astype(BF16),
              w_pool[0].astype(BF16).reshape(NPOOL * prow, HD), w_kv[0].astype(BF16),
              w_gate[0].T.astype(BF16), w_up[0].T.astype(BF16), w_down[0].astype(BF16),
              jnp.pad(conv_w[0], ((0, 5), (0, 0)))]

    cx, cy, cc = (lax.axis_index(n) for n in AXES)
    chip = 2 * cx + cy

    def land(own, index, slots):
        return lax.dynamic_update_index_in_dim(lax.empty((slots,) + own.shape, own.dtype), own, index, 0)

    needed = ["in_pair", "in_near", "in_far", "mix", "merge", "gate_up", "down"]
    members = {"mix": [9, 1, 4, 5], "merge": [2, 3], "gate_up": [6, 7], "down": [8]}
    near = (2, 4)
    plans = {"in_pair": _plan_pair(), "in_near": _plan_gather_chips(1, near), "in_far": _plan_far_chip()}
    plans.update({n: _plan_gather_chips(len(members[n])) for n in members})
    g_bufs = {"in_pair": [shards[0], land(shards[0], cc, 2)],
              "in_near": [w_in[0].astype(BF16), lax.empty((NDEV, D, D), BF16)],
              "in_far": [w_in[0].astype(BF16), lax.empty((2, D, D), BF16)]}
    g_bufs.update({n: [shards[i] for i in members[n]] + [land(shards[i], me, NDEV) for i in members[n]] for n in members})
    first_handles, _ = _copies_start([(g_bufs[n], plans[n]) for n in needed[:2]], "gather_start", x)
    g_handles = dict(zip(needed[:2], first_handles))
    pair_ids = jnp.array([0, 1], jnp.int32)

    on_last_leg = {}

    def get(group, after, early=False):
        if group == "in_whole":
            w_pair, w_near, w_far = after
            w_whole = lax.dynamic_update_slice_in_dim(w_near, w_pair, 2 * chip, 0)
            return lax.dynamic_update_slice_in_dim(w_whole, w_far, 2 * (3 - chip), 0)
        if group == "in_pair":
            bufs = _copies_wait(g_handles[group], plans[group], "gather_wait_" + group, after)
            return bufs[1], pair_ids, (2 * chip + pair_ids).astype(jnp.int32)
        if group not in on_last_leg:
            n_bufs = len(g_bufs[group])
            landed = list(range(n_bufs // 2, n_bufs))
            if group == "in_near":
                plan, more = _plan_gather_sibling(1, near), [(g_bufs[n], plans[n]) for n in needed[2:]]
            elif group == "in_far":
                plan, more = _plan_far_sibling(), []
            else:
                plan, more = _plan_gather_sibling(n_bufs // 2), []
            _, handles, token = _copies_wait_start(g_handles[group], plans[group], (landed, plan), more,
                                                   "gather_pass_" + group, after)
            g_handles.update(zip(needed[2:], handles[1:]))
            on_last_leg[group] = (handles[0], plan, token)
        if early:
            return None
        handle, plan, token = on_last_leg[group]
        got = _copies_wait(handle, plan, "gather_passed_" + group, after if group in ("gate_up", "down") else token)
        if group == "in_near":
            groups = jnp.stack([me ^ k for k in (2, 3, 4, 5)]).astype(jnp.int32)
            return got[0], groups, groups
        if group == "in_far":
            return got[0], pair_ids, (2 * (3 - chip) + pair_ids).astype(jnp.int32)
        if group == "mix":
            cw_g, w_co_g, w_pool_g, w_kv_g = got
            cw_full = cw_g.transpose(1, 0, 2).reshape(8, D)
            w_pool_full = w_pool_g.reshape(NDEV, NPOOL, prow, HD).transpose(1, 0, 2, 3).reshape(NPOOL, HD, HD)
            return cw_full[0:1], cw_full[1:2], cw_full[2:3], w_co_g.reshape(D, D), w_pool_full, w_kv_g
        if group == "merge":
            return got[0].reshape(D, D), got[1].reshape(D, D)
        return [g.reshape(DFF, D) for g in got]

    started = {}

    def put(group, grads):
        if group == "ffn":
            sends = [g.reshape(NDEV, ffb, D) for g in grads]
        elif group == "mix":
            dw_co, dw_xo, dw_o, dw_pool, dw_kv = grads
            sends = [dw_co.reshape(NDEV, rows, D), dw_xo.reshape(NDEV, rows, D), dw_o.reshape(NDEV, rows, D),
                     dw_pool.reshape(NPOOL, NDEV, prow, HD).transpose(1, 0, 2, 3).reshape(NDEV, NPOOL * prow, HD), dw_kv]
        else:
            sends = list(grads)
        n = len(sends)
        halves = [lax.empty((4,) + s.shape[1:], s.dtype) for s in sends]
        (handle,), token = _copies_start([(sends + halves, _plan_scatter_sibling(n))], "scatter_swap_" + group, norm_mix)
        swapping.append((group, handle, n))
        return token

    swapping = []

    def flush(after):
        group, handle, n = swapping.pop()
        bufs = _copies_wait(handle, _plan_scatter_sibling(n), "scatter_swapped_" + group, after)
        sums, lands = _pair_sums(bufs[:n], bufs[n:], cc, chip, "pair_sums_" + group)
        (handle,), token = _copies_start([(sums + lands, _plan_scatter_chips(n))], "scatter_start_" + group, norm_mix)
        started[group] = (handle, _plan_scatter_chips(n))
        return token

    def take(group, after):
        handle, plan = started[group]
        return _copies_wait(handle, plan, "scatter_wait_" + group, after)[len(handle[2]) // 2:]

    gains = (norm_mix, pool_scale, norm_mem, norm_ffn, norm_final.reshape(1, D))
    grad_x, small = _local_step(x[0], mem[0], loss_target[0], gains, get, put, flush)

    everyone = _plan_gather_chips(1, tuple(range(1, NDEV)))
    (small_handle,), token = _copies_start([([small, land(small, me, NDEV)], everyone)], "small_start", norm_mix)

    res = {}

    def update(group, names, ws, gs, ms, vs, from_parts, steps, transposed=()):
        view = lambda a, name: a[0].T if name in transposed else a
        flat = [[view(a, name).reshape(g.shape[-2:]) for a in (w, m, v)] for name, w, g, m, v in zip(names, ws, gs, ms, vs)]
        outs = _adamw([f[0] for f in flat], gs, [f[1] for f in flat], [f[2] for f in flat], "adamw_" + group,
                      from_parts, steps)
        for name, w, four in zip(names, ws, outs):
            res[name] = [(o.T if name in transposed else o).reshape(w.shape) for o in four]

    p_g, p_u, p_d = take("ffn", token)
    update("ffn", ["w_gate", "w_up", "w_down"], [w_gate, w_up, w_down], [p_g, p_u, p_d],
           [m_w_gate, m_w_up, m_w_down], [v_w_gate, v_w_up, v_w_down], True, 2, transposed=("w_gate", "w_up"))

    small_all = _copies_wait(small_handle, everyone, "small_wait", res["w_down"][1])[1]
    small_sum = _sum_parts(small_all, "sum_small")
    loss = small_sum[8, 0]
    g_cw = lax.dynamic_slice_in_dim(small_sum[5:8], me * rows, rows, axis=1)
    update("replicated", ["norm_mix", "pool_scale", "norm_mem", "norm_ffn", "norm_final", "conv_w"],
           [norm_mix, pool_scale, norm_mem, norm_ffn, norm_final, conv_w], [small_sum[k:k + 1] for k in range(5)] + [g_cw],
           [m_norm_mix, m_pool_scale, m_norm_mem, m_norm_ffn, m_norm_final, m_conv_w],
           [v_norm_mix, v_pool_scale, v_norm_mem, v_norm_ffn, v_norm_final, v_conv_w], False, 1)

    p_co, p_xo, p_o, p_pool, p_kv = take("mix", res["conv_w"][1])
    update("mix", ["w_conv_out", "w_xattn_out", "w_out", "w_pool", "w_kv"], [w_conv_out, w_xattn_out, w_out, w_pool, w_kv],
           [p_co, p_xo, p_o, p_pool, p_kv], [m_w_conv_out, m_w_xattn_out, m_w_out, m_w_pool, m_w_kv],
           [v_w_conv_out, v_w_xattn_out, v_w_out, v_w_pool, v_w_kv], True, 2)
    (p_in,) = take("in", res["w_out"][1])
    update("in", ["w_in"], [w_in], [p_in], [m_w_in], [v_w_in], True, 4)
    order = ["norm_mix", "w_in", "conv_w", "w_conv_out", "w_pool", "pool_scale", "norm_mem", "w_kv", "w_xattn_out", "w_out",
             "norm_ffn", "w_gate", "w_up", "w_down", "norm_final"]
    return (loss, grad_x[None], *[res[n][0] for n in order], *[res[n][1] for n in order],
            *[res[n][2] for n in order], *[res[n][3] for n in order])
```

```python
import jax
import jax.numpy as jnp
from jax import lax
from jax.experimental import pallas as pl
from jax.experimental.pallas import tpu as pltpu

F32 = jnp.float32
BF16 = jnp.bfloat16
SDS = jax.ShapeDtypeStruct

AXES = ("x", "y", "c")
NDEV = 8
D = 1024
NSPLIT = 8
NH = 4
HD = D // NH
NPOOL = 4
DFF = 2816
EPS = 1e-6
ATT_SCALE = HD ** -0.5
HALO = 16


def _slot_group(s):
    return jnp.where(s < 3, s + 5, jnp.where(s == 3, 4, s - 4))


SLOT_GROUPS = tuple(s + 5 if s < 3 else 4 if s == 3 else s - 4 for s in range(NSPLIT))


def _w_in_places():
    table = []
    for chip in range(4):
        source = [0 if g // 2 == chip else 2 if g // 2 == 3 - chip else 1 for g in SLOT_GROUPS]
        rows = [source]
        for k in range(3):
            blocks = [g if k == 1 else g % 2 for g in SLOT_GROUPS]
            held = [b for b, src in zip(blocks, source) if src == k][-1]
            rows.append([(held := b if src == k else held) for b, src in zip(blocks, source)])
        table.append([v for row in rows for v in row])
    return jnp.array(table, jnp.int32)


ADAM_LR = 0.001
ADAM_B1 = 0.9
ADAM_B2 = 0.999
ADAM_EPS = 1e-08
ADAM_WD = 0.01
ADAM_STEP = 10

V7X_VMEM_BYTES = 64 * 1024 * 1024
VMEM_LIMIT = V7X_VMEM_BYTES - 8 * 1024 * 1024
HBM = pl.BlockSpec(memory_space=pl.ANY)


def _whole(shape):
    return pl.BlockSpec(shape, lambda *_: (0,) * len(shape), pipeline_mode=pl.Buffered(1))


def _params(n_grid):
    return pltpu.CompilerParams(dimension_semantics=("arbitrary",) * n_grid, vmem_limit_bytes=VMEM_LIMIT)


def _mm(a, b):
    return jnp.dot(a, b, preferred_element_type=F32)


def _mm_nt(a, b):
    return lax.dot_general(a, b, (((1,), (1,)), ((), ())), preferred_element_type=F32)


def _mm_tn(a, b):
    return lax.dot_general(a, b, (((0,), (0,)), ((), ())), preferred_element_type=F32)


def _sigmoid(x):
    return 1.0 / (1.0 + jnp.exp(-x))


def _rms(x):
    return lax.rsqrt(jnp.mean(x * x, axis=-1, keepdims=True) + EPS)


def _norm_bwd(dh, x, gain):
    r = _rms(x)
    xh = x * r
    dxh = dh * gain
    dx = r * (dxh - xh * jnp.mean(dxh * xh, axis=-1, keepdims=True))
    return dx, jnp.sum(dh * xh, axis=0, keepdims=True)


def _col_chunks(n, width=512):
    return [slice(c, min(c + width, n)) for c in range(0, n, width)]


def _shift_down(v, k):
    return pltpu.roll(v, k, 0)


def _shift_up(v, k):
    return pltpu.roll(v, v.shape[0] - k, 0)


def _fwd_proj(x, gain, w_blocks, w_ids, p_ids, tm):
    T = x.shape[0]

    def body(w_ids_ref, p_ids_ref, x_ref, g_ref, w_ref, proj_ref, h_ref):
        del w_ids_ref, p_ids_ref

        @pl.when(pl.program_id(1) == 0)
        def _():
            xf = x_ref[...]
            h_ref[...] = (xf * _rms(xf) * g_ref[...]).astype(BF16)
        proj_ref[...] = _mm(h_ref[...], w_ref[...]).astype(BF16)

    return pl.pallas_call(
        body, name="fwd_proj",
        grid_spec=pltpu.PrefetchScalarGridSpec(
            num_scalar_prefetch=2, grid=(T // tm, w_ids.shape[0]),
            in_specs=[pl.BlockSpec((tm, D), lambda i, j, w, p: (i, 0)), pl.BlockSpec((1, D), lambda i, j, w, p: (0, 0)),
                      pl.BlockSpec((None, D, D), lambda i, j, w, p: (w[j], 0, 0))],
            out_specs=[pl.BlockSpec((None, tm, D), lambda i, j, w, p: (p[j], i, 0)),
                       pl.BlockSpec((tm, D), lambda i, j, w, p: (i, 0))]),
        out_shape=[SDS((NSPLIT, T, D), BF16), SDS((T, D), BF16)],
        compiler_params=_params(2))(w_ids, p_ids, x, gain, w_blocks)


def _fwd_proj_more(h, w_blocks, proj, w_ids, p_ids, tm, name):
    T = h.shape[0]

    def body(w_ids_ref, p_ids_ref, h_ref, w_ref, proj_hbm, proj_ref):
        del w_ids_ref, p_ids_ref, proj_hbm
        proj_ref[...] = _mm(h_ref[...], w_ref[...]).astype(BF16)

    return pl.pallas_call(
        body, name=name,
        grid_spec=pltpu.PrefetchScalarGridSpec(
            num_scalar_prefetch=2, grid=(T // tm, w_ids.shape[0]),
            in_specs=[pl.BlockSpec((tm, D), lambda i, j, w, p: (i, 0)),
                      pl.BlockSpec((None, D, D), lambda i, j, w, p: (w[j], 0, 0)), HBM],
            out_specs=pl.BlockSpec((None, tm, D), lambda i, j, w, p: (p[j], i, 0))),
        out_shape=SDS(proj.shape, BF16), input_output_aliases={4: 0},
        compiler_params=_params(2))(w_ids, p_ids, h, w_blocks, proj)


def _halo_before(split, tm):
    return pl.BlockSpec((None, HALO, D), lambda i: (split, jnp.maximum(i * (tm // HALO) - 1, 0), 0))


def _fwd_mix(proj, cw0, cw1, cw2, w_co, w_pool, mem, gain_mem, w_kv, tm):
    T = proj.shape[1]
    M = mem.shape[0]

    def body(b_ref, c_ref, ua_ref, up_ref, ch_ref, uah_ref, uph_ref, cw0_ref, cw1_ref, cw2_ref, wco_ref, wp_ref,
             mem_ref, gm_ref, wkv_ref, za_ref, conv_ref, pooled_ref, ya_ref, yp_ref, kv_ref, memn_ref):
        i = pl.program_id(0)

        @pl.when(i == 0)
        def _():
            m = mem_ref[...]
            memn = (m * _rms(m) * gm_ref[...]).astype(BF16)
            memn_ref[...] = memn
            for j in range(2 * NH):
                kv_ref[j] = _mm(memn, wkv_ref[j]).astype(BF16)
        keep = jnp.where(i > 0, 1.0, 0.0).astype(F32)
        cu = c_ref[...].astype(F32) * ua_ref[...].astype(F32)
        cu_h = ch_ref[...].astype(F32) * uah_ref[...].astype(F32) * keep
        ext = jnp.concatenate([cu_h, cu], axis=0)
        conv = (cw2_ref[...] * ext + cw1_ref[...] * _shift_down(ext, 1) + cw0_ref[...] * _shift_down(ext, 2))[HALO:]
        za = (b_ref[...].astype(F32) * conv).astype(BF16)
        conv_ref[...] = conv.astype(BF16)
        za_ref[...] = za
        ya_ref[...] = _mm(za, wco_ref[...]).astype(BF16)

        up = up_ref[...].astype(F32)
        ext_u = jnp.concatenate([uph_ref[...].astype(F32) * keep, up], axis=0)
        pos = i * tm + lax.broadcasted_iota(jnp.int32, (tm, HD), 0)
        for g in range(NPOOL):
            cols = slice(g * HD, (g + 1) * HD)
            s = ext_u[:, cols]
            for k in range(g + 1):
                s = s + _shift_down(s, 1 << k)
            cnt = jnp.minimum(pos + 1, 2 << g).astype(F32)
            pooled = (s[HALO:] / cnt - up[:, cols]).astype(BF16)
            pooled_ref[:, cols] = pooled
            yp_ref[:, cols] = _mm(pooled, wp_ref[g]).astype(BF16)

    tile = lambda s: pl.BlockSpec((None, tm, D), lambda i: (s, i, 0))
    row = pl.BlockSpec((1, D), lambda i: (0, 0))
    out = pl.BlockSpec((tm, D), lambda i: (i, 0))
    return pl.pallas_call(
        body, name="fwd_mix", grid=(T // tm,),
        in_specs=[tile(0), tile(1), tile(2), tile(3), _halo_before(1, tm), _halo_before(2, tm), _halo_before(3, tm),
                  row, row, row, _whole((D, D)), _whole((NPOOL, HD, HD)), _whole((M, D)), row, _whole((2 * NH, D, HD))],
        out_specs=[out] * 5 + [pl.BlockSpec((2 * NH, M, HD), lambda i: (0, 0, 0)), pl.BlockSpec((M, D), lambda i: (0, 0))],
        out_shape=[SDS((T, D), BF16)] * 5 + [SDS((2 * NH, M, HD), BF16), SDS((M, D), BF16)],
        compiler_params=_params(1))(proj, proj, proj, proj, proj, proj, proj, cw0, cw1, cw2, w_co, w_pool, mem, gain_mem, w_kv)


def _softmax_rows(s):
    e = jnp.exp(s - jnp.max(s, axis=-1, keepdims=True))
    return e / jnp.sum(e, axis=-1, keepdims=True)


def _fwd_merge(proj, ya, yp, x, kv, w_xo, w_o, pscale, gain_ffn, tm):
    T = x.shape[0]

    def body(q_ref, ga_ref, gp_ref, gx_ref, ya_ref, yp_ref, x_ref, kv_ref, wxo_ref, wo_ref, ps_ref, gf_ref,
             o_ref, yx_ref, merged_ref, x1_ref, h2_ref):
        for h in range(NH):
            cols = slice(h * HD, (h + 1) * HD)
            p = _softmax_rows(_mm_nt(q_ref[:, cols], kv_ref[h]) * ATT_SCALE)
            o_ref[:, cols] = _mm(p.astype(BF16), kv_ref[NH + h]).astype(BF16)
        yx = _mm(o_ref[...], wxo_ref[...])
        yx_ref[...] = yx.astype(BF16)
        merged = (_sigmoid(ga_ref[...].astype(F32)) * ya_ref[...].astype(F32)
                  + _sigmoid(gp_ref[...].astype(F32)) * (yp_ref[...].astype(F32) * ps_ref[...])
                  + _sigmoid(gx_ref[...].astype(F32)) * yx).astype(BF16)
        merged_ref[...] = merged
        x1 = x_ref[...] + _mm(merged, wo_ref[...])
        x1_ref[...] = x1
        h2_ref[...] = (x1 * _rms(x1) * gf_ref[...]).astype(BF16)

    tile = lambda s: pl.BlockSpec((None, tm, D), lambda i: (s, i, 0))
    row = pl.BlockSpec((1, D), lambda i: (0, 0))
    act = pl.BlockSpec((tm, D), lambda i: (i, 0))
    full = _whole((D, D))
    return pl.pallas_call(
        body, name="fwd_merge", grid=(T // tm,),
        in_specs=[tile(4), tile(5), tile(6), tile(7), act, act, act,
                  _whole((2 * NH, kv.shape[1], HD)), full, full, row, row],
        out_specs=[act] * 5,
        out_shape=[SDS((T, D), BF16), SDS((T, D), BF16), SDS((T, D), BF16), SDS((T, D), F32), SDS((T, D), BF16)],
        compiler_params=_params(1))(proj, proj, proj, proj, ya, yp, x, kv, w_xo, w_o, pscale, gain_ffn)


def _fwd_ffn_up(h2, wg_t, wu_t, tm, tn):
    T = h2.shape[0]

    def body(h_ref, wg_ref, wu_ref, gate_ref, up_ref, act_ref):
        for cols in _col_chunks(tn):
            gate = _mm_nt(h_ref[...], wg_ref[cols, :])
            up = _mm_nt(h_ref[...], wu_ref[cols, :])
            gate_ref[:, cols] = gate.astype(BF16)
            up_ref[:, cols] = up.astype(BF16)
            act_ref[:, cols] = (gate * _sigmoid(gate) * up).astype(BF16)

    w = pl.BlockSpec((tn, D), lambda n, i: (n, 0))
    o = pl.BlockSpec((tm, tn), lambda n, i: (i, n))
    return pl.pallas_call(
        body, name="fwd_ffn_up", grid=(DFF // tn, T // tm),
        in_specs=[pl.BlockSpec((tm, D), lambda n, i: (i, 0)), w, w],
        out_specs=[o] * 3, out_shape=[SDS((T, DFF), BF16)] * 3,
        compiler_params=_params(2))(h2, wg_t, wu_t)


def _fwd_ffn_down_loss(act, w_d, x1, target, gain_final, tm):
    T = x1.shape[0]

    def body(act_ref, wd_ref, x1_ref, tgt_ref, g_ref, dx2_ref, loss_ref, dgain_ref):
        @pl.when(pl.program_id(0) == 0)
        def _():
            loss_ref[...] = jnp.zeros_like(loss_ref)
            dgain_ref[...] = jnp.zeros_like(dgain_ref)
        x2 = x1_ref[...] + _mm(act_ref[...], wd_ref[...])
        gain = g_ref[...]
        y = x2 * _rms(x2) * gain
        err = y - tgt_ref[...]
        loss_ref[...] += 0.5 * jnp.sum(jnp.mean(err * err, axis=-1, keepdims=True))
        dx2, dgain = _norm_bwd(err * (1.0 / D), x2, gain)
        dx2_ref[...] = dx2
        dgain_ref[...] += dgain

    act_spec = pl.BlockSpec((tm, D), lambda i: (i, 0))
    row = pl.BlockSpec((1, D), lambda i: (0, 0))
    return pl.pallas_call(
        body, name="fwd_ffn_down_loss", grid=(T // tm,),
        in_specs=[pl.BlockSpec((tm, DFF), lambda i: (i, 0)), _whole((DFF, D)), act_spec, act_spec, row],
        out_specs=[act_spec, pl.BlockSpec((8, D), lambda i: (0, 0)), row],
        out_shape=[SDS((T, D), F32), SDS((8, D), F32), SDS((1, D), F32)],
        compiler_params=_params(1))(act, w_d, x1, target, gain_final)


def _bwd_ffn_down(dx2, w_d, gate, up, tm, tn):
    T = dx2.shape[0]

    def body(dx_ref, wd_ref, gate_ref, up_ref, dgate_ref, dup_ref):
        dx = dx_ref[...].astype(BF16)
        for cols in _col_chunks(tn):
            dact = _mm_nt(dx, wd_ref[cols, :])
            gate = gate_ref[:, cols].astype(F32)
            sg = _sigmoid(gate)
            dgate_ref[:, cols] = (dact * up_ref[:, cols].astype(F32) * (sg * (1.0 + gate * (1.0 - sg)))).astype(BF16)
            dup_ref[:, cols] = (dact * gate * sg).astype(BF16)

    o = pl.BlockSpec((tm, tn), lambda n, i: (i, n))
    return pl.pallas_call(
        body, name="bwd_ffn_down", grid=(DFF // tn, T // tm),
        in_specs=[pl.BlockSpec((tm, D), lambda n, i: (i, 0)), pl.BlockSpec((tn, D), lambda n, i: (n, 0)), o, o],
        out_specs=[o] * 2, out_shape=[SDS((T, DFF), BF16)] * 2,
        compiler_params=_params(2))(dx2, w_d, gate, up)


def _bwd_ffn_up(dgate, dup, wg_t, wu_t, x1, dx2, gain_ffn, tm):
    T = x1.shape[0]

    def body(dg_ref, du_ref, wg_ref, wu_ref, x1_ref, dx2_ref, g_ref, dx1_ref, dgain_ref):
        @pl.when(pl.program_id(0) == 0)
        def _():
            dgain_ref[...] = jnp.zeros_like(dgain_ref)
        dh2 = _mm(dg_ref[...], wg_ref[...]) + _mm(du_ref[...], wu_ref[...])
        dx, dgain = _norm_bwd(dh2, x1_ref[...], g_ref[...])
        dx1_ref[...] = dx2_ref[...] + dx
        dgain_ref[...] += dgain

    wide = pl.BlockSpec((tm, DFF), lambda i: (i, 0))
    w = _whole((DFF, D))
    act = pl.BlockSpec((tm, D), lambda i: (i, 0))
    row = pl.BlockSpec((1, D), lambda i: (0, 0))
    return pl.pallas_call(
        body, name="bwd_ffn_up", grid=(T // tm,),
        in_specs=[wide, wide, w, w, act, act, row], out_specs=[act, row],
        out_shape=[SDS((T, D), F32), SDS((1, D), F32)],
        compiler_params=_params(1))(dgate, dup, wg_t, wu_t, x1, dx2, gain_ffn)


def _wgrad(a, b, *, name, groups, a_cols, b_cols, tt, a_index, b_index, o_index, out_shape, after):
    T = a.shape[0]
    nt = T // tt
    n_a = a.shape[1] // a_cols if groups == 1 else 1

    def body(a_ref, b_ref, after_ref, o_ref, acc_ref):
        del after_ref
        t = pl.program_id(2)

        @pl.when(t == 0)
        def _():
            acc_ref[...] = jnp.zeros_like(acc_ref)
        acc_ref[...] += _mm_tn(a_ref[...].astype(BF16), b_ref[...].astype(BF16))

        @pl.when(t == nt - 1)
        def _():
            o_ref[...] = acc_ref[...].astype(o_ref.dtype)

    return pl.pallas_call(
        body, name=name, grid=(groups, n_a, nt),
        in_specs=[pl.BlockSpec((tt, a_cols), a_index), pl.BlockSpec((None, tt, b_cols), b_index), HBM],
        out_specs=pl.BlockSpec((None, a_cols, b_cols), o_index),
        out_shape=SDS(out_shape, BF16),
        scratch_shapes=[pltpu.VMEM((a_cols, b_cols), F32)],
        compiler_params=_params(3))(a, b, after)


def _wgrad_dense(a, b, name, tt, after, a_cols=None):
    ka, nb = a.shape[1], b.shape[1]
    a_cols = ka if a_cols is None else a_cols
    out = _wgrad(a, b[None], name=name, groups=1, a_cols=a_cols, b_cols=nb, tt=tt,
                 a_index=lambda g, k, t: (t, k), b_index=lambda g, k, t: (0, t, 0),
                 o_index=lambda g, k, t: (k, 0, 0), out_shape=(ka // a_cols, a_cols, nb), after=after)
    return out.reshape(ka, nb)


def _bwd_merge(dx1, proj, ya, yp, yx, pooled, pscale, w_o, w_co, w_xo, w_pool, tm, after):
    T = dx1.shape[0]
    nt = T // tm

    def body(dx1_ref, ga_ref, gp_ref, gx_ref, ya_ref, yp_ref, yx_ref, pooled_ref, ps_ref, wo_ref, wco_ref, wxo_ref, wp_ref,
             after_ref, dgates_ref, dya_ref, dyx_ref, dza_ref, do_ref, dpooled_ref, dps_ref, dwp_ref, acc_ref):
        del after_ref

        @pl.when(pl.program_id(0) == 0)
        def _():
            dps_ref[...] = jnp.zeros_like(dps_ref)
            acc_ref[...] = jnp.zeros_like(acc_ref)
        dmerged = _mm_nt(dx1_ref[...].astype(BF16), wo_ref[...])
        scale = ps_ref[...]
        sa, sp, sx = (_sigmoid(r[...].astype(F32)) for r in (ga_ref, gp_ref, gx_ref))
        ya, yp_pre, yx = (r[...].astype(F32) for r in (ya_ref, yp_ref, yx_ref))
        dgates_ref[0] = (dmerged * ya * sa * (1.0 - sa)).astype(BF16)
        dgates_ref[1] = (dmerged * (yp_pre * scale) * sp * (1.0 - sp)).astype(BF16)
        dgates_ref[2] = (dmerged * yx * sx * (1.0 - sx)).astype(BF16)
        dya = (dmerged * sa).astype(BF16)
        dyx = (dmerged * sx).astype(BF16)
        dyp = dmerged * sp
        dyps = (dyp * scale).astype(BF16)
        dps_ref[...] += jnp.sum(dyp * yp_pre, axis=0, keepdims=True)
        dya_ref[...] = dya
        dyx_ref[...] = dyx
        dza_ref[...] = _mm_nt(dya, wco_ref[...]).astype(BF16)
        do_ref[...] = _mm_nt(dyx, wxo_ref[...]).astype(BF16)
        for g in range(NPOOL):
            cols = slice(g * HD, (g + 1) * HD)
            dpooled_ref[:, cols] = _mm_nt(dyps[:, cols], wp_ref[g]).astype(BF16)
            acc_ref[g] += _mm_tn(pooled_ref[:, cols], dyps[:, cols])

        @pl.when(pl.program_id(0) == nt - 1)
        def _():
            dwp_ref[...] = acc_ref[...].astype(BF16)

    tile = lambda s: pl.BlockSpec((None, tm, D), lambda i: (s, i, 0))
    row = pl.BlockSpec((1, D), lambda i: (0, 0))
    act = pl.BlockSpec((tm, D), lambda i: (i, 0))
    full = _whole((D, D))
    return pl.pallas_call(
        body, name="bwd_merge", grid=(T // tm,),
        in_specs=[act, tile(5), tile(6), tile(7), act, act, act, act, row, full, full, full,
                  _whole((NPOOL, HD, HD)), HBM],
        out_specs=[pl.BlockSpec((3, tm, D), lambda i: (0, i, 0))] + [act] * 5
        + [row, pl.BlockSpec((NPOOL, HD, HD), lambda i: (0, 0, 0))],
        out_shape=[SDS((NSPLIT, T, D), BF16)] + [SDS((T, D), BF16)] * 5 + [SDS((1, D), F32), SDS((NPOOL, HD, HD), BF16)],
        scratch_shapes=[pltpu.VMEM((NPOOL, HD, HD), F32)],
        compiler_params=_params(1))(dx1, proj, proj, proj, ya, yp, yx, pooled, pscale, w_o, w_co, w_xo, w_pool, after)


def _bwd_attn(dproj, proj, do, kv, memn, w_kv, mem, gain_mem, tm):
    T = do.shape[0]
    M = kv.shape[1]
    nt = T // tm

    def body(dproj_hbm, q_ref, do_ref, kv_ref, memn_ref, wkv_ref, mem_ref, gm_ref, dq_ref, dw_ref, dgain_ref, dkv_ref):
        del dproj_hbm

        @pl.when(pl.program_id(0) == 0)
        def _():
            dkv_ref[...] = jnp.zeros_like(dkv_ref)
        for h in range(NH):
            cols = slice(h * HD, (h + 1) * HD)
            q = q_ref[:, cols]
            do_h = do_ref[:, cols]
            p = _softmax_rows(_mm_nt(q, kv_ref[h]) * ATT_SCALE)
            dp = _mm_nt(do_h, kv_ref[NH + h])
            ds = (p * (dp - jnp.sum(dp * p, axis=-1, keepdims=True)) * ATT_SCALE).astype(BF16)
            dq_ref[:, cols] = _mm(ds, kv_ref[h]).astype(BF16)
            dkv_ref[h] += _mm_tn(ds, q)
            dkv_ref[NH + h] += _mm_tn(p.astype(BF16), do_h)

        @pl.when(pl.program_id(0) == nt - 1)
        def _():
            dmemn = jnp.zeros((M, D), F32)
            for j in range(2 * NH):
                dkv_j = dkv_ref[j].astype(BF16)
                dw_ref[j] = _mm_tn(memn_ref[...], dkv_j).astype(BF16)
                dmemn = dmemn + _mm_nt(dkv_j, wkv_ref[j])
            dgain_ref[...] = _norm_bwd(dmemn, mem_ref[...], gm_ref[...])[1]

    row = pl.BlockSpec((1, D), lambda i: (0, 0))
    return pl.pallas_call(
        body, name="bwd_attn", grid=(nt,),
        in_specs=[HBM, pl.BlockSpec((None, tm, D), lambda i: (4, i, 0)), pl.BlockSpec((tm, D), lambda i: (i, 0)),
                  _whole((2 * NH, M, HD)), _whole((M, D)), _whole((2 * NH, D, HD)), _whole((M, D)), row],
        out_specs=[pl.BlockSpec((None, tm, D), lambda i: (3, i, 0)),
                   pl.BlockSpec((2 * NH, D, HD), lambda i: (0, 0, 0)), row],
        out_shape=[SDS(dproj.shape, BF16), SDS((2 * NH, D, HD), BF16), SDS((1, D), F32)],
        scratch_shapes=[pltpu.VMEM((2 * NH, M, HD), F32)],
        input_output_aliases={0: 0},
        compiler_params=_params(1))(dproj, proj, do, kv, memn, w_kv, mem, gain_mem)


def _bwd_mix(dproj, proj, conv, dza, dpooled, cw0, cw1, cw2, tm, after):
    T = dza.shape[0]
    nt = T // tm

    def halo_after(split_or_none):
        idx = lambda i: jnp.minimum((i + 1) * (tm // HALO), T // HALO - 1)
        if split_or_none is None:
            return pl.BlockSpec((HALO, D), lambda i: (idx(i), 0))
        return pl.BlockSpec((None, HALO, D), lambda i: (split_or_none, idx(i), 0))

    def body(dproj_hbm, b_ref, c_ref, ua_ref, conv_ref, dza_ref, dpo_ref, bn_ref, dzan_ref, dpon_ref,
             cw0_ref, cw1_ref, cw2_ref, after_ref, dabcu_ref, dcw_ref):
        del dproj_hbm, after_ref
        i = pl.program_id(0)

        @pl.when(i == 0)
        def _():
            dcw_ref[...] = jnp.zeros_like(dcw_ref)
        keep_next = jnp.where(i < nt - 1, 1.0, 0.0).astype(F32)
        dza = dza_ref[...].astype(F32)
        c = c_ref[...].astype(F32)
        ua = ua_ref[...].astype(F32)
        dconv = dza * b_ref[...].astype(F32)
        dconv_n = dzan_ref[...].astype(F32) * bn_ref[...].astype(F32) * keep_next
        ext = jnp.concatenate([dconv, dconv_n], axis=0)
        dconv_1, dconv_2 = _shift_up(ext, 1)[:tm], _shift_up(ext, 2)[:tm]
        dcu = cw2_ref[...] * dconv + cw1_ref[...] * dconv_1 + cw0_ref[...] * dconv_2
        dabcu_ref[0] = (dza * conv_ref[...].astype(F32)).astype(BF16)
        dabcu_ref[1] = (dcu * ua).astype(BF16)
        dabcu_ref[2] = (dcu * c).astype(BF16)

        cu = c * ua
        dcw_ref[2:3, :] += jnp.sum(dconv * cu, axis=0, keepdims=True)
        dcw_ref[1:2, :] += jnp.sum(dconv_1 * cu, axis=0, keepdims=True)
        dcw_ref[0:1, :] += jnp.sum(dconv_2 * cu, axis=0, keepdims=True)

        dpo = dpo_ref[...].astype(F32)
        ext_dpo = jnp.concatenate([dpo, dpon_ref[...].astype(F32) * keep_next], axis=0)
        pos = i * tm + lax.broadcasted_iota(jnp.int32, (tm + HALO, HD), 0)
        for g in range(NPOOL):
            cols = slice(g * HD, (g + 1) * HD)
            s = ext_dpo[:, cols] / jnp.minimum(pos + 1, 2 << g).astype(F32)
            for k in range(g + 1):
                s = s + _shift_up(s, 1 << k)
            dabcu_ref[3, :, cols] = (s[:tm] - dpo[:, cols]).astype(BF16)

    tile = lambda s: pl.BlockSpec((None, tm, D), lambda i: (s, i, 0))
    act = pl.BlockSpec((tm, D), lambda i: (i, 0))
    row = pl.BlockSpec((1, D), lambda i: (0, 0))
    return pl.pallas_call(
        body, name="bwd_mix", grid=(nt,),
        in_specs=[HBM, tile(0), tile(1), tile(2), act, act, act, halo_after(0), halo_after(None), halo_after(None),
                  row, row, row, HBM],
        out_specs=[pl.BlockSpec((4, tm, D), lambda i: (1, i, 0)), pl.BlockSpec((8, D), lambda i: (0, 0))],
        out_shape=[SDS(dproj.shape, BF16), SDS((8, D), F32)],
        input_output_aliases={0: 0},
        compiler_params=_params(1))(dproj, proj, proj, proj, conv, dza, dpooled, proj, dza, dpooled, cw0, cw1, cw2, after)


def _bwd_proj(dproj, w_parts, places, x, dx1, gain, tm, after):
    T = x.shape[0]

    def body(places_ref, dp_ref, w0_ref, w1_ref, w2_ref, x_ref, dx1_ref, g_ref, after_ref, dx_ref, dgain_ref, acc_ref):
        del after_ref
        i, s = pl.program_id(0), pl.program_id(1)

        @pl.when((i == 0) & (s == 0))
        def _():
            dgain_ref[...] = jnp.zeros_like(dgain_ref)

        @pl.when(s == 0)
        def _():
            acc_ref[...] = jnp.zeros_like(acc_ref)

        for k, w_ref in enumerate((w0_ref, w1_ref, w2_ref)):
            @pl.when(places_ref[s] == k)
            def _(w_ref=w_ref):
                acc_ref[...] += _mm_nt(dp_ref[...], w_ref[...])

        @pl.when(s == NSPLIT - 1)
        def _():
            dx, dgain = _norm_bwd(acc_ref[...], x_ref[...], g_ref[...])
            dx_ref[...] = dx1_ref[...] + dx
            dgain_ref[...] += dgain

    act = pl.BlockSpec((tm, D), lambda i, s, p: (i, 0))
    row = pl.BlockSpec((1, D), lambda i, s, p: (0, 0))
    weight = lambda k: pl.BlockSpec((None, D, D), lambda i, s, p: (p[NSPLIT * (1 + k) + s], 0, 0))
    return pl.pallas_call(
        body, name="bwd_proj",
        grid_spec=pltpu.PrefetchScalarGridSpec(
            num_scalar_prefetch=1, grid=(T // tm, NSPLIT),
            in_specs=[pl.BlockSpec((None, tm, D), lambda i, s, p: (s, i, 0)), weight(0), weight(1), weight(2),
                      act, act, row, HBM],
            out_specs=[act, row], scratch_shapes=[pltpu.VMEM((tm, D), F32)]),
        out_shape=[SDS((T, D), F32), SDS((1, D), F32)],
        compiler_params=_params(2))(places, dproj, *w_parts, x, dx1, gain, after)


def _adamw_math(w, g, m, v):
    m = ADAM_B1 * m + (1.0 - ADAM_B1) * g
    v = ADAM_B2 * v + (1.0 - ADAM_B2) * (g * g)
    m_hat = m / (1.0 - ADAM_B1 ** ADAM_STEP)
    v_hat = v / (1.0 - ADAM_B2 ** ADAM_STEP)
    delta = -ADAM_LR * (m_hat / (jnp.sqrt(v_hat) + ADAM_EPS) + ADAM_WD * w)
    return delta, m, v


def _row_tile(rows):
    return 256 if rows % 256 == 0 else rows


def _sum_parts(parts, name):
    n_parts, rows, cols = parts.shape
    tr = _row_tile(rows)

    def body(p_ref, g_ref):
        g = p_ref[0].astype(F32)
        for k in range(1, n_parts):
            g = g + p_ref[k].astype(F32)
        g_ref[...] = g

    blk = pl.BlockSpec((tr, cols), lambda i: (i, 0))
    return pl.pallas_call(
        body, name=name, grid=(rows // tr,),
        in_specs=[pl.BlockSpec((n_parts, tr, cols), lambda i: (0, i, 0))], out_specs=blk,
        out_shape=SDS((rows, cols), F32), compiler_params=_params(1))(parts)


def _adamw(ws, gs, ms, vs, name, from_parts, steps):
    n = len(ws)

    def body(*refs):
        for a in range(n):
            w_ref, g_ref, m_ref, v_ref = refs[4 * a:4 * a + 4]
            go_ref, d_ref, mo_ref, vo_ref = refs[4 * n + 4 * a:4 * n + 4 * a + 4]
            if from_parts:
                g = g_ref[0].astype(F32)
                for k in range(1, g_ref.shape[0]):
                    g = g + g_ref[k].astype(F32)
            else:
                g = g_ref[...]
            go_ref[...] = g
            d_ref[...], mo_ref[...], vo_ref[...] = _adamw_math(w_ref[...], g, m_ref[...], v_ref[...])

    in_specs, out_specs, out_shape, operands = [], [], [], []
    for w, g, m, v in zip(ws, gs, ms, vs):
        rows, cols = w.shape
        blk = pl.BlockSpec((rows // steps, cols), lambda i: (i, 0))
        g_spec = pl.BlockSpec((g.shape[0], rows // steps, cols), lambda i: (0, i, 0)) if from_parts else blk
        in_specs += [blk, g_spec, blk, blk]
        out_specs += [blk] * 4
        out_shape += [SDS((rows, cols), F32)] * 4
        operands += [w, g, m, v]
    outs = pl.pallas_call(body, name=name, grid=(steps,), in_specs=in_specs, out_specs=out_specs, out_shape=out_shape,
                          compiler_params=_params(1))(*operands)
    return [outs[4 * a:4 * a + 4] for a in range(n)]


def _peer(k, x, y, c):
    return ((1 - x) if k & 4 else x, (1 - y) if k & 2 else y, (1 - c) if k & 1 else c)


SEM = pl.BlockSpec(memory_space=pltpu.SEMAPHORE)
IN_HBM = pl.BlockSpec(memory_space=pltpu.HBM)
DATAFLOW = pltpu.SideEffectType.DATAFLOW_SIDE_EFFECTING
TOKEN_SHAPE = (8, 128)


OTHER_CHIPS = (2, 4, 6)


def _place(x, y, c):
    return 4 * x + 2 * y + c


def _plan_gather_chips(n, ks=(1,) + OTHER_CHIPS):
    def plan(refs, x, y, c, arriving):
        out = []
        for a in range(n):
            for k in ks:
                there = _place(*_peer(k, x, y, c))
                out.append((refs[a], refs[n + a].at[there if arriving else _place(x, y, c)], k))
        return out
    return plan, n * len(ks)


def _plan_gather_sibling(n, ks=OTHER_CHIPS):
    def plan(refs, x, y, c, arriving):
        out = []
        for a in range(n):
            for k in ks:
                px, py, pc = _peer(k, x, y, c)
                mine, theirs = _place(px, py, pc), _place(px, py, 1 - pc)
                out.append((refs[a].at[mine], refs[a].at[theirs if arriving else mine], 1))
        return out
    return plan, n * len(ks)


def _plan_pair():
    def plan(refs, x, y, c, arriving):
        return [(refs[0], refs[1].at[(1 - c) if arriving else c], 1)]
    return plan, 1


def _plan_far_chip():
    def plan(refs, x, y, c, arriving):
        return [(refs[0], refs[1].at[c], 6)]
    return plan, 1


def _plan_far_sibling():
    def plan(refs, x, y, c, arriving):
        return [(refs[0].at[c], refs[0].at[(1 - c) if arriving else c], 1)]
    return plan, 1


def _plan_scatter_sibling(n):
    def plan(refs, x, y, c, arriving):
        out = []
        for a in range(n):
            for q in range(4):
                out.append((refs[a].at[2 * q + (1 - c)], refs[n + a].at[q], 1))
        return out
    return plan, n * 4


def _plan_scatter_chips(n):
    def plan(refs, x, y, c, arriving):
        out = []
        for a in range(n):
            for k in OTHER_CHIPS:
                px, py, _ = _peer(k, x, y, c)
                out.append((refs[a].at[2 * px + py], refs[n + a].at[(2 * px + py) if arriving else (2 * x + y)], k))
        return out
    return plan, n * 3


def _remote(src, dst, send_sems, recv_sems, i, k):
    x, y, c = (lax.axis_index(n) for n in AXES)
    return pltpu.make_async_remote_copy(src_ref=src, dst_ref=dst, send_sem=send_sems.at[i], recv_sem=recv_sems.at[i],
                                        device_id=_peer(k, x, y, c), device_id_type=pl.DeviceIdType.MESH)


def _copies_start(groups, name, after):
    ng = len(groups)
    total = sum(len(bufs) for bufs, _ in groups)

    def body(*refs):
        sems = refs[1 + total:1 + total + 2 * ng]
        x, y, c = (lax.axis_index(n) for n in AXES)
        off = 1
        for gi, (bufs, (plan, _)) in enumerate(groups):
            for i, (src, dst, k) in enumerate(plan(refs[off:off + len(bufs)], x, y, c, False)):
                _remote(src, dst, sems[2 * gi], sems[2 * gi + 1], i, k).start()
            off += len(bufs)
        refs[-1][...] = jnp.zeros(TOKEN_SHAPE, F32)

    sem_shapes = [pltpu.SemaphoreType.DMA((count,)) for _, (_, count) in groups for _ in range(2)]
    flat = [b for bufs, _ in groups for b in bufs]
    outs = pl.pallas_call(
        body, name=name,
        in_specs=[HBM] + [IN_HBM] * total,
        out_specs=[SEM] * (2 * ng) + [IN_HBM] * total + [pl.BlockSpec(memory_space=pltpu.VMEM)],
        out_shape=sem_shapes + [pltpu.HBM(b.shape, b.dtype) for b in flat] + [SDS(TOKEN_SHAPE, F32)],
        input_output_aliases={1 + i: 2 * ng + i for i in range(total)},
        compiler_params=pltpu.CompilerParams(has_side_effects=DATAFLOW),
    )(after, *[pltpu.with_memory_space_constraint(b, pltpu.HBM) for b in flat])
    handles, off = [], 2 * ng
    for gi, (bufs, _) in enumerate(groups):
        handles.append((outs[2 * gi], outs[2 * gi + 1], list(outs[off:off + len(bufs)])))
        off += len(bufs)
    return handles, outs[-1]


def _copies_wait_start(handle, plan, pass_on, more, name, after):
    send_sems, recv_sems, bufs = handle
    n = len(bufs)
    idx, (pass_plan, pass_count) = pass_on
    total = sum(len(b) for b, _ in more)
    ng = 1 + len(more)

    def body(*refs):
        x, y, c = (lax.axis_index(a) for a in AXES)
        waited = refs[1:1 + n]
        outs = refs[3 + n + total:]
        new_sems = outs[n + total:n + total + 2 * ng]
        for i, (src, dst, k) in enumerate(plan[0](waited, x, y, c, True)):
            copy = _remote(src, dst, refs[1 + n + total], refs[2 + n + total], i, k)
            copy.wait_send()
            copy.wait_recv()
        for i, (src, dst, k) in enumerate(pass_plan([waited[j] for j in idx], x, y, c, False)):
            _remote(src, dst, new_sems[0], new_sems[1], i, k).start()
        off = 1 + n
        for gi, (b, (p, _)) in enumerate(more):
            for i, (src, dst, k) in enumerate(p(refs[off:off + len(b)], x, y, c, False)):
                _remote(src, dst, new_sems[2 + 2 * gi], new_sems[3 + 2 * gi], i, k).start()
            off += len(b)
        outs[-1][...] = jnp.zeros(TOKEN_SHAPE, F32)

    flat = list(bufs) + [a for b, _ in more for a in b]
    sem_shapes = [pltpu.SemaphoreType.DMA((count,)) for count in [pass_count] + [cnt for _, (_, cnt) in more] for _ in range(2)]
    outs = pl.pallas_call(
        body, name=name,
        in_specs=[HBM] + [IN_HBM] * (n + total) + [SEM, SEM],
        out_specs=[IN_HBM] * (n + total) + [SEM] * (2 * ng) + [pl.BlockSpec(memory_space=pltpu.VMEM)],
        out_shape=[pltpu.HBM(b.shape, b.dtype) for b in flat] + sem_shapes + [SDS(TOKEN_SHAPE, F32)],
        input_output_aliases={1 + i: i for i in range(n + total)},
        compiler_params=pltpu.CompilerParams(has_side_effects=DATAFLOW),
    )(after, *[pltpu.with_memory_space_constraint(b, pltpu.HBM) for b in flat], send_sems, recv_sems)
    thru = list(outs[:n])
    sems_out = outs[n + total:n + total + 2 * ng]
    handles = [(sems_out[0], sems_out[1], [thru[j] for j in idx])]
    off = n
    for gi, (b, _) in enumerate(more):
        handles.append((sems_out[2 + 2 * gi], sems_out[3 + 2 * gi], list(outs[off:off + len(b)])))
        off += len(b)
    return thru, handles, outs[-1]


def _copies_wait(handle, plan, name, after):
    send_sems, recv_sems, bufs = handle
    n = len(bufs)

    def body(*refs):
        x, y, c = (lax.axis_index(a) for a in AXES)
        for i, (src, dst, k) in enumerate(plan[0](refs[:n], x, y, c, True)):
            copy = _remote(src, dst, refs[n], refs[n + 1], i, k)
            copy.wait_send()
            copy.wait_recv()

    return pl.pallas_call(
        body, name=name,
        in_specs=[IN_HBM] * n + [SEM, SEM, HBM], out_specs=[IN_HBM] * n,
        out_shape=[pltpu.HBM(b.shape, b.dtype) for b in bufs],
        input_output_aliases={i: i for i in range(n)},
        compiler_params=pltpu.CompilerParams(has_side_effects=DATAFLOW),
    )(*bufs, send_sems, recv_sems, after)


def _pair_sums(mine, theirs, c, chip, name):
    n = len(mine)

    def body(where_ref, *refs):
        q = pl.program_id(0)
        for a in range(n):
            total = (refs[a][...].astype(F32) + refs[n + a][...].astype(F32)).astype(BF16)
            refs[2 * n + a][...] = total

            @pl.when(q == where_ref[1])
            def _():
                refs[3 * n + a][...] = total

    block = lambda t: (None,) + t.shape[1:]
    zeros = lambda t: (0,) * (t.ndim - 1)
    outs = pl.pallas_call(
        body, name=name,
        grid_spec=pltpu.PrefetchScalarGridSpec(
            num_scalar_prefetch=1, grid=(4,),
            in_specs=[pl.BlockSpec(block(t), lambda q, w, z=zeros(t): (2 * q + w[0],) + z) for t in theirs]
            + [pl.BlockSpec(block(t), lambda q, w, z=zeros(t): (q,) + z) for t in theirs],
            out_specs=[pl.BlockSpec(block(t), lambda q, w, z=zeros(t): (q,) + z) for t in theirs]
            + [pl.BlockSpec(block(t), lambda q, w, z=zeros(t): (w[1],) + z) for t in theirs]),
        out_shape=[SDS(t.shape, BF16) for t in theirs] * 2,
        compiler_params=_params(1))(jnp.stack([c, chip]).astype(jnp.int32), *mine, *theirs)
    return list(outs[:n]), list(outs[n:])


def _local_step(x, mem, target, gains, get, put, flush, tm_huge=2048, tm_big=1024, tm_mid=512, tm_small=256):
    g_mix, pscale, g_mem, g_ffn, g_fin = gains
    T = x.shape[0]
    tm_huge, tm_big, tm_mid, tm_small = min(tm_huge, T), min(tm_big, T), min(tm_mid, T), min(tm_small, T)
    tn = DFF // 2

    w_pair, w_ids, p_ids = get("in_pair", x)
    proj, h = _fwd_proj(x, g_mix, w_pair, w_ids, p_ids, tm_huge)
    w_near, w_ids, p_ids = get("in_near", h)
    proj = _fwd_proj_more(h, w_near, proj, w_ids, p_ids, tm_huge, "fwd_proj_near")
    w_far, w_ids, p_ids = get("in_far", proj)
    proj = _fwd_proj_more(h, w_far, proj, w_ids, p_ids, tm_huge, "fwd_proj_far")
    cw0, cw1, cw2, w_co, w_pool, w_kv = get("mix", proj)
    za, conv, pooled, ya, yp, kv, memn = _fwd_mix(proj, cw0, cw1, cw2, w_co, w_pool, mem, g_mem, w_kv, tm_mid)
    w_xo, w_o = get("merge", ya)
    o, yx, merged, x1, h2 = _fwd_merge(proj, ya, yp, x, kv, w_xo, w_o, pscale, g_ffn, tm_mid)
    wg_t, wu_t = get("gate_up", x1)
    get("down", x1, early=True)
    gate, up, act = _fwd_ffn_up(h2, wg_t, wu_t, tm_big, tn)
    (w_d,) = get("down", gate)
    dx2, loss, dg_fin = _fwd_ffn_down_loss(act, w_d, x1, target, g_fin, tm_mid)

    dgate, dup = _bwd_ffn_down(dx2, w_d, gate, up, tm_big, tn)
    dx1, dg_ffn = _bwd_ffn_up(dgate, dup, wg_t, wu_t, x1, dx2, g_ffn, tm_small)
    dw_d = _wgrad_dense(act, dx2, "wgrad_down", tm_big, g_mix)
    dwg_t = _wgrad_dense(dgate, h2, "wgrad_gate", tm_big, g_mix)
    dwu_t = _wgrad_dense(dup, h2, "wgrad_up", tm_big, g_mix)
    token = put("ffn", (dwg_t, dwu_t, dw_d))

    dproj, dya, dyx, dza, do, dpooled, dpscale, dw_pool = _bwd_merge(
        dx1, proj, ya, yp, yx, pooled, pscale, w_o, w_co, w_xo, w_pool, tm_mid, token)
    token = flush(dya)
    dw_o = _wgrad_dense(merged, dx1, "wgrad_out", tm_big, token)
    dw_co = _wgrad_dense(za, dya, "wgrad_conv_out", tm_big, token)
    dw_xo = _wgrad_dense(o, dyx, "wgrad_xattn_out", tm_big, token)
    dproj, dw_kv, dg_mem = _bwd_attn(dproj, proj, do, kv, memn, w_kv, mem, g_mem, tm_big)
    token = put("mix", (dw_co, dw_xo, dw_o, dw_pool, dw_kv))

    dproj, dcw = _bwd_mix(dproj, proj, conv, dza, dpooled, cw0, cw1, cw2, tm_mid, token)
    token = flush(dcw)
    dw_in = _wgrad(h, dproj, name="wgrad_in", groups=NSPLIT, a_cols=D, b_cols=D, tt=tm_huge,
                   a_index=lambda g, k, t: (t, 0), b_index=lambda g, k, t: (g, t, 0),
                   o_index=lambda g, k, t: (_slot_group(g), 0, 0), out_shape=(NSPLIT, D, D), after=token)
    token = flush(put("in", (dw_in,)))
    grad_x, dg_mix = _bwd_proj(dproj, (w_pair, w_near, w_far), get("in_places", None), x, dx1, g_mix, tm_big, token)

    small = jnp.concatenate([dg_mix, dpscale, dg_mem, dg_ffn, dg_fin, dcw[0:3], loss], axis=0)
    return grad_x, small


def kernel(x, mem, norm_mix, w_in, conv_w, w_conv_out, w_pool, pool_scale, norm_mem, w_kv, w_xattn_out, w_out, norm_ffn, w_gate, w_up, w_down, norm_final, loss_target, m_norm_mix, m_w_in, m_conv_w, m_w_conv_out, m_w_pool, m_pool_scale, m_norm_mem, m_w_kv, m_w_xattn_out, m_w_out, m_norm_ffn, m_w_gate, m_w_up, m_w_down, m_norm_final, v_norm_mix, v_w_in, v_conv_w, v_w_conv_out, v_w_pool, v_pool_scale, v_norm_mem, v_w_kv, v_w_xattn_out, v_w_out, v_norm_ffn, v_w_gate, v_w_up, v_w_down, v_norm_final):
    T = x.shape[1]
    rows = D // NDEV
    ffb = DFF // NDEV
    prow = HD // NDEV
    me = 4 * lax.axis_index("x") + 2 * lax.axis_index("y") + lax.axis_index("c")

    shards = [w_in[0].astype(BF16), w_conv_out[0].astype(BF16), w_xattn_out[0].astype(BF16), w_out[0].astype(BF16),
              w_pool[0].astype(BF16).reshape(NPOOL * prow, HD), w_kv[0].astype(BF16),
              w_gate[0].T.astype(BF16), w_up[0].T.astype(BF16), w_down[0].astype(BF16),
              jnp.pad(conv_w[0], ((0, 5), (0, 0)))]

    cx, cy, cc = (lax.axis_index(n) for n in AXES)
    chip = 2 * cx + cy

    def land(own, index, slots):
        return lax.dynamic_update_index_in_dim(lax.empty((slots,) + own.shape, own.dtype), own, index, 0)

    needed = ["in_pair", "in_near", "in_far", "mix", "merge", "gate_up", "down"]
    members = {"mix": [9, 1, 4, 5], "merge": [2, 3], "gate_up": [6, 7], "down": [8]}
    near = (2, 4)
    plans = {"in_pair": _plan_pair(), "in_near": _plan_gather_chips(1, near), "in_far": _plan_far_chip()}
    plans.update({n: _plan_gather_chips(len(members[n])) for n in members})
    g_bufs = {"in_pair": [shards[0], land(shards[0], cc, 2)],
              "in_near": [w_in[0].astype(BF16), lax.empty((NDEV, D, D), BF16)],
              "in_far": [w_in[0].astype(BF16), lax.empty((2, D, D), BF16)]}
    g_bufs.update({n: [shards[i] for i in members[n]] + [land(shards[i], me, NDEV) for i in members[n]] for n in members})
    first_handles, _ = _copies_start([(g_bufs[n], plans[n]) for n in needed[:2]], "gather_start", x)
    g_handles = dict(zip(needed[:2], first_handles))
    pair_ids = jnp.array([0, 1], jnp.int32)

    on_last_leg = {}

    def get(group, after, early=False):
        if group == "in_places":
            return _w_in_places()[chip]
        if group == "in_pair":
            bufs = _copies_wait(g_handles[group], plans[group], "gather_wait_" + group, after)
            return bufs[1], pair_ids, (2 * chip + pair_ids).astype(jnp.int32)
        if group not in on_last_leg:
            n_bufs = len(g_bufs[group])
            landed = list(range(n_bufs // 2, n_bufs))
            if group == "in_near":
                plan, more = _plan_gather_sibling(1, near), [(g_bufs[n], plans[n]) for n in needed[2:]]
            elif group == "in_far":
                plan, more = _plan_far_sibling(), []
            else:
                plan, more = _plan_gather_sibling(n_bufs // 2), []
            _, handles, token = _copies_wait_start(g_handles[group], plans[group], (landed, plan), more,
                                                   "gather_pass_" + group, after)
            g_handles.update(zip(needed[2:], handles[1:]))
            on_last_leg[group] = (handles[0], plan, token)
        if early:
            return None
        handle, plan, token = on_last_leg[group]
        got = _copies_wait(handle, plan, "gather_passed_" + group, after if group in ("gate_up", "down") else token)
        if group == "in_near":
            groups = jnp.stack([me ^ k for k in (2, 3, 4, 5)]).astype(jnp.int32)
            return got[0], groups, groups
        if group == "in_far":
            return got[0], pair_ids, (2 * (3 - chip) + pair_ids).astype(jnp.int32)
        if group == "mix":
            cw_g, w_co_g, w_pool_g, w_kv_g = got
            cw_full = cw_g.transpose(1, 0, 2).reshape(8, D)
            w_pool_full = w_pool_g.reshape(NDEV, NPOOL, prow, HD).transpose(1, 0, 2, 3).reshape(NPOOL, HD, HD)
            return cw_full[0:1], cw_full[1:2], cw_full[2:3], w_co_g.reshape(D, D), w_pool_full, w_kv_g
        if group == "merge":
            return got[0].reshape(D, D), got[1].reshape(D, D)
        return [g.reshape(DFF, D) for g in got]

    started = {}

    def put(group, grads):
        if group == "ffn":
            sends = [g.reshape(NDEV, ffb, D) for g in grads]
        elif group == "mix":
            dw_co, dw_xo, dw_o, dw_pool, dw_kv = grads
            sends = [dw_co.reshape(NDEV, rows, D), dw_xo.reshape(NDEV, rows, D), dw_o.reshape(NDEV, rows, D),
                     dw_pool.reshape(NPOOL, NDEV, prow, HD).transpose(1, 0, 2, 3).reshape(NDEV, NPOOL * prow, HD), dw_kv]
        else:
            sends = list(grads)
        n = len(sends)
        halves = [lax.empty((4,) + s.shape[1:], s.dtype) for s in sends]
        (handle,), token = _copies_start([(sends + halves, _plan_scatter_sibling(n))], "scatter_swap_" + group, norm_mix)
        swapping.append((group, handle, n))
        return token

    swapping = []

    def flush(after):
        group, handle, n = swapping.pop()
        bufs = _copies_wait(handle, _plan_scatter_sibling(n), "scatter_swapped_" + group, after)
        sums, lands = _pair_sums(bufs[:n], bufs[n:], cc, chip, "pair_sums_" + group)
        (handle,), token = _copies_start([(sums + lands, _plan_scatter_chips(n))], "scatter_start_" + group, norm_mix)
        started[group] = (handle, _plan_scatter_chips(n))
        return token

    def take(group, after):
        handle, plan = started[group]
        return _copies_wait(handle, plan, "scatter_wait_" + group, after)[len(handle[2]) // 2:]

    gains = (norm_mix, pool_scale, norm_mem, norm_ffn, norm_final.reshape(1, D))
    grad_x, small = _local_step(x[0], mem[0], loss_target[0], gains, get, put, flush)

    everyone = _plan_gather_chips(1, tuple(range(1, NDEV)))
    (small_handle,), token = _copies_start([([small, land(small, me, NDEV)], everyone)], "small_start", norm_mix)

    res = {}

    def update(group, names, ws, gs, ms, vs, from_parts, steps, transposed=()):
        view = lambda a, name: a[0].T if name in transposed else a
        flat = [[view(a, name).reshape(g.shape[-2:]) for a in (w, m, v)] for name, w, g, m, v in zip(names, ws, gs, ms, vs)]
        outs = _adamw([f[0] for f in flat], gs, [f[1] for f in flat], [f[2] for f in flat], "adamw_" + group,
                      from_parts, steps)
        for name, w, four in zip(names, ws, outs):
            res[name] = [(o.T if name in transposed else o).reshape(w.shape) for o in four]

    p_g, p_u, p_d = take("ffn", token)
    update("ffn", ["w_gate", "w_up", "w_down"], [w_gate, w_up, w_down], [p_g, p_u, p_d],
           [m_w_gate, m_w_up, m_w_down], [v_w_gate, v_w_up, v_w_down], True, 2, transposed=("w_gate", "w_up"))

    small_all = _copies_wait(small_handle, everyone, "small_wait", res["w_down"][1])[1]
    small_sum = _sum_parts(small_all, "sum_small")
    loss = small_sum[8, 0]
    g_cw = lax.dynamic_slice_in_dim(small_sum[5:8], me * rows, rows, axis=1)
    update("replicated", ["norm_mix", "pool_scale", "norm_mem", "norm_ffn", "norm_final", "conv_w"],
           [norm_mix, pool_scale, norm_mem, norm_ffn, norm_final, conv_w], [small_sum[k:k + 1] for k in range(5)] + [g_cw],
           [m_norm_mix, m_pool_scale, m_norm_mem, m_norm_ffn, m_norm_final, m_conv_w],
           [v_norm_mix, v_pool_scale, v_norm_mem, v_norm_ffn, v_norm_final, v_conv_w], False, 1)

    p_co, p_xo, p_o, p_pool, p_kv = take("mix", res["conv_w"][1])
    update("mix", ["w_conv_out", "w_xattn_out", "w_out", "w_pool", "w_kv"], [w_conv_out, w_xattn_out, w_out, w_pool, w_kv],
           [p_co, p_xo, p_o, p_pool, p_kv], [m_w_conv_out, m_w_xattn_out, m_w_out, m_w_pool, m_w_kv],
           [v_w_conv_out, v_w_xattn_out, v_w_out, v_w_pool, v_w_kv], True, 2)
    (p_in,) = take("in", res["w_out"][1])
    update("in", ["w_in"], [w_in], [p_in], [m_w_in], [v_w_in], True, 4)
    order = ["norm_mix", "w_in", "conv_w", "w_conv_out", "w_pool", "pool_scale", "norm_mem", "w_kv", "w_xattn_out", "w_out",
             "norm_ffn", "w_gate", "w_up", "w_down", "norm_final"]
    return (loss, grad_x[None], *[res[n][0] for n in order], *[res[n][1] for n in order],
            *[res[n][2] for n in order], *[res[n][3] for n in order])
```

```python
import jax
import jax.numpy as jnp
from jax import lax
from jax.experimental import pallas as pl
from jax.experimental.pallas import tpu as pltpu

F32 = jnp.float32
BF16 = jnp.bfloat16
SDS = jax.ShapeDtypeStruct

AXES = ("x", "y", "c")
NDEV = 8
D = 1024
NSPLIT = 8
NH = 4
HD = D // NH
NPOOL = 4
DFF = 2816
EPS = 1e-6
ATT_SCALE = HD ** -0.5
HALO = 16


def _slot_group(s):
    return jnp.where(s < 3, s + 5, jnp.where(s == 3, 4, s - 4))


SLOT_GROUPS = tuple(s + 5 if s < 3 else 4 if s == 3 else s - 4 for s in range(NSPLIT))


def _w_in_places():
    table = []
    for chip in range(4):
        source = [0 if g // 2 == chip else 2 if g // 2 == 3 - chip else 1 for g in SLOT_GROUPS]
        rows = [source]
        for k in range(3):
            blocks = [g if k == 1 else g % 2 for g in SLOT_GROUPS]
            held = [b for b, src in zip(blocks, source) if src == k][-1]
            rows.append([(held := b if src == k else held) for b, src in zip(blocks, source)])
        table.append([v for row in rows for v in row])
    return jnp.array(table, jnp.int32)


ADAM_LR = 0.001
ADAM_B1 = 0.9
ADAM_B2 = 0.999
ADAM_EPS = 1e-08
ADAM_WD = 0.01
ADAM_STEP = 10

V7X_VMEM_BYTES = 64 * 1024 * 1024
VMEM_LIMIT = V7X_VMEM_BYTES - 8 * 1024 * 1024
HBM = pl.BlockSpec(memory_space=pl.ANY)


def _whole(shape):
    return pl.BlockSpec(shape, lambda *_: (0,) * len(shape), pipeline_mode=pl.Buffered(1))


def _params(n_grid):
    return pltpu.CompilerParams(dimension_semantics=("arbitrary",) * n_grid, vmem_limit_bytes=VMEM_LIMIT)


def _mm(a, b):
    return jnp.dot(a, b, preferred_element_type=F32)


def _mm_nt(a, b):
    return lax.dot_general(a, b, (((1,), (1,)), ((), ())), preferred_element_type=F32)


def _mm_tn(a, b):
    return lax.dot_general(a, b, (((0,), (0,)), ((), ())), preferred_element_type=F32)


def _sigmoid(x):
    return 1.0 / (1.0 + jnp.exp(-x))


def _rms(x):
    return lax.rsqrt(jnp.mean(x * x, axis=-1, keepdims=True) + EPS)


def _norm_bwd(dh, x, gain):
    r = _rms(x)
    xh = x * r
    dxh = dh * gain
    dx = r * (dxh - xh * jnp.mean(dxh * xh, axis=-1, keepdims=True))
    return dx, jnp.sum(dh * xh, axis=0, keepdims=True)


def _col_chunks(n, width=512):
    return [slice(c, min(c + width, n)) for c in range(0, n, width)]


def _shift_down(v, k):
    return pltpu.roll(v, k, 0)


def _shift_up(v, k):
    return pltpu.roll(v, v.shape[0] - k, 0)


def _fwd_proj(x, gain, w_blocks, w_ids, p_ids, tm):
    T = x.shape[0]

    def body(w_ids_ref, p_ids_ref, x_ref, g_ref, w_ref, proj_ref, h_ref):
        del w_ids_ref, p_ids_ref

        @pl.when(pl.program_id(1) == 0)
        def _():
            xf = x_ref[...]
            h_ref[...] = (xf * _rms(xf) * g_ref[...]).astype(BF16)
        proj_ref[...] = _mm(h_ref[...], w_ref[...]).astype(BF16)

    return pl.pallas_call(
        body, name="fwd_proj",
        grid_spec=pltpu.PrefetchScalarGridSpec(
            num_scalar_prefetch=2, grid=(T // tm, w_ids.shape[0]),
            in_specs=[pl.BlockSpec((tm, D), lambda i, j, w, p: (i, 0)), pl.BlockSpec((1, D), lambda i, j, w, p: (0, 0)),
                      pl.BlockSpec((None, D, D), lambda i, j, w, p: (w[j], 0, 0))],
            out_specs=[pl.BlockSpec((None, tm, D), lambda i, j, w, p: (p[j], i, 0)),
                       pl.BlockSpec((tm, D), lambda i, j, w, p: (i, 0))]),
        out_shape=[SDS((NSPLIT, T, D), BF16), SDS((T, D), BF16)],
        compiler_params=_params(2))(w_ids, p_ids, x, gain, w_blocks)


def _fwd_proj_more(h, w_blocks, proj, w_ids, p_ids, tm, name):
    T = h.shape[0]

    def body(w_ids_ref, p_ids_ref, h_ref, w_ref, proj_hbm, proj_ref):
        del w_ids_ref, p_ids_ref, proj_hbm
        proj_ref[...] = _mm(h_ref[...], w_ref[...]).astype(BF16)

    return pl.pallas_call(
        body, name=name,
        grid_spec=pltpu.PrefetchScalarGridSpec(
            num_scalar_prefetch=2, grid=(T // tm, w_ids.shape[0]),
            in_specs=[pl.BlockSpec((tm, D), lambda i, j, w, p: (i, 0)),
                      pl.BlockSpec((None, D, D), lambda i, j, w, p: (w[j], 0, 0)), HBM],
            out_specs=pl.BlockSpec((None, tm, D), lambda i, j, w, p: (p[j], i, 0))),
        out_shape=SDS(proj.shape, BF16), input_output_aliases={4: 0},
        compiler_params=_params(2))(w_ids, p_ids, h, w_blocks, proj)


def _halo_before(split, tm):
    return pl.BlockSpec((None, HALO, D), lambda i: (split, jnp.maximum(i * (tm // HALO) - 1, 0), 0))


def _fwd_mix(proj, cw0, cw1, cw2, w_co, w_pool, mem, gain_mem, w_kv, tm):
    T = proj.shape[1]
    M = mem.shape[0]

    def body(b_ref, c_ref, ua_ref, up_ref, ch_ref, uah_ref, uph_ref, cw0_ref, cw1_ref, cw2_ref, wco_ref, wp_ref,
             mem_ref, gm_ref, wkv_ref, za_ref, conv_ref, pooled_ref, ya_ref, yp_ref, kv_ref, memn_ref):
        i = pl.program_id(0)

        @pl.when(i == 0)
        def _():
            m = mem_ref[...]
            memn = (m * _rms(m) * gm_ref[...]).astype(BF16)
            memn_ref[...] = memn
            for j in range(2 * NH):
                kv_ref[j] = _mm(memn, wkv_ref[j]).astype(BF16)
        keep = jnp.where(i > 0, 1.0, 0.0).astype(F32)
        cu = c_ref[...].astype(F32) * ua_ref[...].astype(F32)
        cu_h = ch_ref[...].astype(F32) * uah_ref[...].astype(F32) * keep
        ext = jnp.concatenate([cu_h, cu], axis=0)
        conv = (cw2_ref[...] * ext + cw1_ref[...] * _shift_down(ext, 1) + cw0_ref[...] * _shift_down(ext, 2))[HALO:]
        za = (b_ref[...].astype(F32) * conv).astype(BF16)
        conv_ref[...] = conv.astype(BF16)
        za_ref[...] = za
        ya_ref[...] = _mm(za, wco_ref[...]).astype(BF16)

        up = up_ref[...].astype(F32)
        ext_u = jnp.concatenate([uph_ref[...].astype(F32) * keep, up], axis=0)
        pos = i * tm + lax.broadcasted_iota(jnp.int32, (tm, HD), 0)
        for g in range(NPOOL):
            cols = slice(g * HD, (g + 1) * HD)
            s = ext_u[:, cols]
            for k in range(g + 1):
                s = s + _shift_down(s, 1 << k)
            cnt = jnp.minimum(pos + 1, 2 << g).astype(F32)
            pooled = (s[HALO:] / cnt - up[:, cols]).astype(BF16)
            pooled_ref[:, cols] = pooled
            yp_ref[:, cols] = _mm(pooled, wp_ref[g]).astype(BF16)

    tile = lambda s: pl.BlockSpec((None, tm, D), lambda i: (s, i, 0))
    row = pl.BlockSpec((1, D), lambda i: (0, 0))
    out = pl.BlockSpec((tm, D), lambda i: (i, 0))
    return pl.pallas_call(
        body, name="fwd_mix", grid=(T // tm,),
        in_specs=[tile(0), tile(1), tile(2), tile(3), _halo_before(1, tm), _halo_before(2, tm), _halo_before(3, tm),
                  row, row, row, _whole((D, D)), _whole((NPOOL, HD, HD)), _whole((M, D)), row, _whole((2 * NH, D, HD))],
        out_specs=[out] * 5 + [pl.BlockSpec((2 * NH, M, HD), lambda i: (0, 0, 0)), pl.BlockSpec((M, D), lambda i: (0, 0))],
        out_shape=[SDS((T, D), BF16)] * 5 + [SDS((2 * NH, M, HD), BF16), SDS((M, D), BF16)],
        compiler_params=_params(1))(proj, proj, proj, proj, proj, proj, proj, cw0, cw1, cw2, w_co, w_pool, mem, gain_mem, w_kv)


def _softmax_rows(s):
    e = jnp.exp(s - jnp.max(s, axis=-1, keepdims=True))
    return e / jnp.sum(e, axis=-1, keepdims=True)


def _fwd_merge(proj, ya, yp, x, kv, w_xo, w_o, pscale, gain_ffn, tm):
    T = x.shape[0]

    def body(q_ref, ga_ref, gp_ref, gx_ref, ya_ref, yp_ref, x_ref, kv_ref, wxo_ref, wo_ref, ps_ref, gf_ref,
             o_ref, yx_ref, merged_ref, x1_ref, h2_ref):
        for h in range(NH):
            cols = slice(h * HD, (h + 1) * HD)
            p = _softmax_rows(_mm_nt(q_ref[:, cols], kv_ref[h]) * ATT_SCALE)
            o_ref[:, cols] = _mm(p.astype(BF16), kv_ref[NH + h]).astype(BF16)
        yx = _mm(o_ref[...], wxo_ref[...])
        yx_ref[...] = yx.astype(BF16)
        merged = (_sigmoid(ga_ref[...].astype(F32)) * ya_ref[...].astype(F32)
                  + _sigmoid(gp_ref[...].astype(F32)) * (yp_ref[...].astype(F32) * ps_ref[...])
                  + _sigmoid(gx_ref[...].astype(F32)) * yx).astype(BF16)
        merged_ref[...] = merged
        x1 = x_ref[...] + _mm(merged, wo_ref[...])
        x1_ref[...] = x1
        h2_ref[...] = (x1 * _rms(x1) * gf_ref[...]).astype(BF16)

    tile = lambda s: pl.BlockSpec((None, tm, D), lambda i: (s, i, 0))
    row = pl.BlockSpec((1, D), lambda i: (0, 0))
    act = pl.BlockSpec((tm, D), lambda i: (i, 0))
    full = _whole((D, D))
    return pl.pallas_call(
        body, name="fwd_merge", grid=(T // tm,),
        in_specs=[tile(4), tile(5), tile(6), tile(7), act, act, act,
                  _whole((2 * NH, kv.shape[1], HD)), full, full, row, row],
        out_specs=[act] * 5,
        out_shape=[SDS((T, D), BF16), SDS((T, D), BF16), SDS((T, D), BF16), SDS((T, D), F32), SDS((T, D), BF16)],
        compiler_params=_params(1))(proj, proj, proj, proj, ya, yp, x, kv, w_xo, w_o, pscale, gain_ffn)


def _fwd_ffn_up(h2, wg_t, wu_t, tm, tn):
    T = h2.shape[0]

    def body(h_ref, wg_ref, wu_ref, gate_ref, up_ref, act_ref):
        for cols in _col_chunks(tn):
            gate = _mm_nt(h_ref[...], wg_ref[cols, :])
            up = _mm_nt(h_ref[...], wu_ref[cols, :])
            gate_ref[:, cols] = gate.astype(BF16)
            up_ref[:, cols] = up.astype(BF16)
            act_ref[:, cols] = (gate * _sigmoid(gate) * up).astype(BF16)

    w = pl.BlockSpec((tn, D), lambda n, i: (n, 0))
    o = pl.BlockSpec((tm, tn), lambda n, i: (i, n))
    return pl.pallas_call(
        body, name="fwd_ffn_up", grid=(DFF // tn, T // tm),
        in_specs=[pl.BlockSpec((tm, D), lambda n, i: (i, 0)), w, w],
        out_specs=[o] * 3, out_shape=[SDS((T, DFF), BF16)] * 3,
        compiler_params=_params(2))(h2, wg_t, wu_t)


def _fwd_ffn_down_loss(act, w_d, x1, target, gain_final, tm):
    T = x1.shape[0]

    def body(act_ref, wd_ref, x1_ref, tgt_ref, g_ref, dx2_ref, loss_ref, dgain_ref):
        @pl.when(pl.program_id(0) == 0)
        def _():
            loss_ref[...] = jnp.zeros_like(loss_ref)
            dgain_ref[...] = jnp.zeros_like(dgain_ref)
        x2 = x1_ref[...] + _mm(act_ref[...], wd_ref[...])
        gain = g_ref[...]
        y = x2 * _rms(x2) * gain
        err = y - tgt_ref[...]
        loss_ref[...] += 0.5 * jnp.sum(jnp.mean(err * err, axis=-1, keepdims=True))
        dx2, dgain = _norm_bwd(err * (1.0 / D), x2, gain)
        dx2_ref[...] = dx2
        dgain_ref[...] += dgain

    act_spec = pl.BlockSpec((tm, D), lambda i: (i, 0))
    row = pl.BlockSpec((1, D), lambda i: (0, 0))
    return pl.pallas_call(
        body, name="fwd_ffn_down_loss", grid=(T // tm,),
        in_specs=[pl.BlockSpec((tm, DFF), lambda i: (i, 0)), _whole((DFF, D)), act_spec, act_spec, row],
        out_specs=[act_spec, pl.BlockSpec((8, D), lambda i: (0, 0)), row],
        out_shape=[SDS((T, D), F32), SDS((8, D), F32), SDS((1, D), F32)],
        compiler_params=_params(1))(act, w_d, x1, target, gain_final)


def _bwd_ffn_down(dx2, w_d, gate, up, tm, tn):
    T = dx2.shape[0]

    def body(dx_ref, wd_ref, gate_ref, up_ref, dgate_ref, dup_ref):
        dx = dx_ref[...].astype(BF16)
        for cols in _col_chunks(tn):
            dact = _mm_nt(dx, wd_ref[cols, :])
            gate = gate_ref[:, cols].astype(F32)
            sg = _sigmoid(gate)
            dgate_ref[:, cols] = (dact * up_ref[:, cols].astype(F32) * (sg * (1.0 + gate * (1.0 - sg)))).astype(BF16)
            dup_ref[:, cols] = (dact * gate * sg).astype(BF16)

    o = pl.BlockSpec((tm, tn), lambda n, i: (i, n))
    return pl.pallas_call(
        body, name="bwd_ffn_down", grid=(DFF // tn, T // tm),
        in_specs=[pl.BlockSpec((tm, D), lambda n, i: (i, 0)), pl.BlockSpec((tn, D), lambda n, i: (n, 0)), o, o],
        out_specs=[o] * 2, out_shape=[SDS((T, DFF), BF16)] * 2,
        compiler_params=_params(2))(dx2, w_d, gate, up)


def _bwd_ffn_up(dgate, dup, wg_t, wu_t, x1, dx2, gain_ffn, tm):
    T = x1.shape[0]

    def body(dg_ref, du_ref, wg_ref, wu_ref, x1_ref, dx2_ref, g_ref, dx1_ref, dgain_ref):
        @pl.when(pl.program_id(0) == 0)
        def _():
            dgain_ref[...] = jnp.zeros_like(dgain_ref)
        dh2 = _mm(dg_ref[...], wg_ref[...]) + _mm(du_ref[...], wu_ref[...])
        dx, dgain = _norm_bwd(dh2, x1_ref[...], g_ref[...])
        dx1_ref[...] = dx2_ref[...] + dx
        dgain_ref[...] += dgain

    wide = pl.BlockSpec((tm, DFF), lambda i: (i, 0))
    w = _whole((DFF, D))
    act = pl.BlockSpec((tm, D), lambda i: (i, 0))
    row = pl.BlockSpec((1, D), lambda i: (0, 0))
    return pl.pallas_call(
        body, name="bwd_ffn_up", grid=(T // tm,),
        in_specs=[wide, wide, w, w, act, act, row], out_specs=[act, row],
        out_shape=[SDS((T, D), F32), SDS((1, D), F32)],
        compiler_params=_params(1))(dgate, dup, wg_t, wu_t, x1, dx2, gain_ffn)


def _wgrad(a, b, *, name, groups, a_cols, b_cols, tt, a_index, b_index, o_index, out_shape, after):
    T = a.shape[0]
    nt = T // tt
    n_a = a.shape[1] // a_cols if groups == 1 else 1

    def body(a_ref, b_ref, after_ref, o_ref, acc_ref):
        del after_ref
        t = pl.program_id(2)

        @pl.when(t == 0)
        def _():
            acc_ref[...] = jnp.zeros_like(acc_ref)
        acc_ref[...] += _mm_tn(a_ref[...].astype(BF16), b_ref[...].astype(BF16))

        @pl.when(t == nt - 1)
        def _():
            o_ref[...] = acc_ref[...].astype(o_ref.dtype)

    return pl.pallas_call(
        body, name=name, grid=(groups, n_a, nt),
        in_specs=[pl.BlockSpec((tt, a_cols), a_index), pl.BlockSpec((None, tt, b_cols), b_index), HBM],
        out_specs=pl.BlockSpec((None, a_cols, b_cols), o_index),
        out_shape=SDS(out_shape, BF16),
        scratch_shapes=[pltpu.VMEM((a_cols, b_cols), F32)],
        compiler_params=_params(3))(a, b, after)


def _wgrad_dense(a, b, name, tt, after, a_cols=None):
    ka, nb = a.shape[1], b.shape[1]
    a_cols = ka if a_cols is None else a_cols
    out = _wgrad(a, b[None], name=name, groups=1, a_cols=a_cols, b_cols=nb, tt=tt,
                 a_index=lambda g, k, t: (t, k), b_index=lambda g, k, t: (0, t, 0),
                 o_index=lambda g, k, t: (k, 0, 0), out_shape=(ka // a_cols, a_cols, nb), after=after)
    return out.reshape(ka, nb)


def _bwd_merge(dx1, proj, ya, yp, yx, pooled, pscale, w_o, w_co, w_xo, w_pool, tm, after):
    T = dx1.shape[0]
    nt = T // tm

    def body(dx1_ref, ga_ref, gp_ref, gx_ref, ya_ref, yp_ref, yx_ref, pooled_ref, ps_ref, wo_ref, wco_ref, wxo_ref, wp_ref,
             after_ref, dgates_ref, dya_ref, dyx_ref, dza_ref, do_ref, dpooled_ref, dps_ref, dwp_ref, acc_ref):
        del after_ref

        @pl.when(pl.program_id(0) == 0)
        def _():
            dps_ref[...] = jnp.zeros_like(dps_ref)
            acc_ref[...] = jnp.zeros_like(acc_ref)
        dmerged = _mm_nt(dx1_ref[...].astype(BF16), wo_ref[...])
        scale = ps_ref[...]
        sa, sp, sx = (_sigmoid(r[...].astype(F32)) for r in (ga_ref, gp_ref, gx_ref))
        ya, yp_pre, yx = (r[...].astype(F32) for r in (ya_ref, yp_ref, yx_ref))
        dgates_ref[0] = (dmerged * ya * sa * (1.0 - sa)).astype(BF16)
        dgates_ref[1] = (dmerged * (yp_pre * scale) * sp * (1.0 - sp)).astype(BF16)
        dgates_ref[2] = (dmerged * yx * sx * (1.0 - sx)).astype(BF16)
        dya = (dmerged * sa).astype(BF16)
        dyx = (dmerged * sx).astype(BF16)
        dyp = dmerged * sp
        dyps = (dyp * scale).astype(BF16)
        dps_ref[...] += jnp.sum(dyp * yp_pre, axis=0, keepdims=True)
        dya_ref[...] = dya
        dyx_ref[...] = dyx
        dza_ref[...] = _mm_nt(dya, wco_ref[...]).astype(BF16)
        do_ref[...] = _mm_nt(dyx, wxo_ref[...]).astype(BF16)
        for g in range(NPOOL):
            cols = slice(g * HD, (g + 1) * HD)
            dpooled_ref[:, cols] = _mm_nt(dyps[:, cols], wp_ref[g]).astype(BF16)
            acc_ref[g] += _mm_tn(pooled_ref[:, cols], dyps[:, cols])

        @pl.when(pl.program_id(0) == nt - 1)
        def _():
            dwp_ref[...] = acc_ref[...].astype(BF16)

    tile = lambda s: pl.BlockSpec((None, tm, D), lambda i: (s, i, 0))
    row = pl.BlockSpec((1, D), lambda i: (0, 0))
    act = pl.BlockSpec((tm, D), lambda i: (i, 0))
    full = _whole((D, D))
    return pl.pallas_call(
        body, name="bwd_merge", grid=(T // tm,),
        in_specs=[act, tile(5), tile(6), tile(7), act, act, act, act, row, full, full, full,
                  _whole((NPOOL, HD, HD)), HBM],
        out_specs=[pl.BlockSpec((3, tm, D), lambda i: (0, i, 0))] + [act] * 5
        + [row, pl.BlockSpec((NPOOL, HD, HD), lambda i: (0, 0, 0))],
        out_shape=[SDS((NSPLIT, T, D), BF16)] + [SDS((T, D), BF16)] * 5 + [SDS((1, D), F32), SDS((NPOOL, HD, HD), BF16)],
        scratch_shapes=[pltpu.VMEM((NPOOL, HD, HD), F32)],
        compiler_params=_params(1))(dx1, proj, proj, proj, ya, yp, yx, pooled, pscale, w_o, w_co, w_xo, w_pool, after)


def _bwd_attn(dproj, proj, do, kv, memn, w_kv, mem, gain_mem, tm):
    T = do.shape[0]
    M = kv.shape[1]
    nt = T // tm

    def body(dproj_hbm, q_ref, do_ref, kv_ref, memn_ref, wkv_ref, mem_ref, gm_ref, dq_ref, dw_ref, dgain_ref, dkv_ref):
        del dproj_hbm

        @pl.when(pl.program_id(0) == 0)
        def _():
            dkv_ref[...] = jnp.zeros_like(dkv_ref)
        for h in range(NH):
            cols = slice(h * HD, (h + 1) * HD)
            q = q_ref[:, cols]
            do_h = do_ref[:, cols]
            p = _softmax_rows(_mm_nt(q, kv_ref[h]) * ATT_SCALE)
            dp = _mm_nt(do_h, kv_ref[NH + h])
            ds = (p * (dp - jnp.sum(dp * p, axis=-1, keepdims=True)) * ATT_SCALE).astype(BF16)
            dq_ref[:, cols] = _mm(ds, kv_ref[h]).astype(BF16)
            dkv_ref[h] += _mm_tn(ds, q)
            dkv_ref[NH + h] += _mm_tn(p.astype(BF16), do_h)

        @pl.when(pl.program_id(0) == nt - 1)
        def _():
            dmemn = jnp.zeros((M, D), F32)
            for j in range(2 * NH):
                dkv_j = dkv_ref[j].astype(BF16)
                dw_ref[j] = _mm_tn(memn_ref[...], dkv_j).astype(BF16)
                dmemn = dmemn + _mm_nt(dkv_j, wkv_ref[j])
            dgain_ref[...] = _norm_bwd(dmemn, mem_ref[...], gm_ref[...])[1]

    row = pl.BlockSpec((1, D), lambda i: (0, 0))
    return pl.pallas_call(
        body, name="bwd_attn", grid=(nt,),
        in_specs=[HBM, pl.BlockSpec((None, tm, D), lambda i: (4, i, 0)), pl.BlockSpec((tm, D), lambda i: (i, 0)),
                  _whole((2 * NH, M, HD)), _whole((M, D)), _whole((2 * NH, D, HD)), _whole((M, D)), row],
        out_specs=[pl.BlockSpec((None, tm, D), lambda i: (3, i, 0)),
                   pl.BlockSpec((2 * NH, D, HD), lambda i: (0, 0, 0)), row],
        out_shape=[SDS(dproj.shape, BF16), SDS((2 * NH, D, HD), BF16), SDS((1, D), F32)],
        scratch_shapes=[pltpu.VMEM((2 * NH, M, HD), F32)],
        input_output_aliases={0: 0},
        compiler_params=_params(1))(dproj, proj, do, kv, memn, w_kv, mem, gain_mem)


def _bwd_mix(dproj, proj, conv, dza, dpooled, cw0, cw1, cw2, tm, after):
    T = dza.shape[0]
    nt = T // tm

    def halo_after(split_or_none):
        idx = lambda i: jnp.minimum((i + 1) * (tm // HALO), T // HALO - 1)
        if split_or_none is None:
            return pl.BlockSpec((HALO, D), lambda i: (idx(i), 0))
        return pl.BlockSpec((None, HALO, D), lambda i: (split_or_none, idx(i), 0))

    def body(dproj_hbm, b_ref, c_ref, ua_ref, conv_ref, dza_ref, dpo_ref, bn_ref, dzan_ref, dpon_ref,
             cw0_ref, cw1_ref, cw2_ref, after_ref, dabcu_ref, dcw_ref):
        del dproj_hbm, after_ref
        i = pl.program_id(0)

        @pl.when(i == 0)
        def _():
            dcw_ref[...] = jnp.zeros_like(dcw_ref)
        keep_next = jnp.where(i < nt - 1, 1.0, 0.0).astype(F32)
        dza = dza_ref[...].astype(F32)
        c = c_ref[...].astype(F32)
        ua = ua_ref[...].astype(F32)
        dconv = dza * b_ref[...].astype(F32)
        dconv_n = dzan_ref[...].astype(F32) * bn_ref[...].astype(F32) * keep_next
        ext = jnp.concatenate([dconv, dconv_n], axis=0)
        dconv_1, dconv_2 = _shift_up(ext, 1)[:tm], _shift_up(ext, 2)[:tm]
        dcu = cw2_ref[...] * dconv + cw1_ref[...] * dconv_1 + cw0_ref[...] * dconv_2
        dabcu_ref[0] = (dza * conv_ref[...].astype(F32)).astype(BF16)
        dabcu_ref[1] = (dcu * ua).astype(BF16)
        dabcu_ref[2] = (dcu * c).astype(BF16)

        cu = c * ua
        dcw_ref[2:3, :] += jnp.sum(dconv * cu, axis=0, keepdims=True)
        dcw_ref[1:2, :] += jnp.sum(dconv_1 * cu, axis=0, keepdims=True)
        dcw_ref[0:1, :] += jnp.sum(dconv_2 * cu, axis=0, keepdims=True)

        dpo = dpo_ref[...].astype(F32)
        ext_dpo = jnp.concatenate([dpo, dpon_ref[...].astype(F32) * keep_next], axis=0)
        pos = i * tm + lax.broadcasted_iota(jnp.int32, (tm + HALO, HD), 0)
        for g in range(NPOOL):
            cols = slice(g * HD, (g + 1) * HD)
            s = ext_dpo[:, cols] / jnp.minimum(pos + 1, 2 << g).astype(F32)
            for k in range(g + 1):
                s = s + _shift_up(s, 1 << k)
            dabcu_ref[3, :, cols] = (s[:tm] - dpo[:, cols]).astype(BF16)

    tile = lambda s: pl.BlockSpec((None, tm, D), lambda i: (s, i, 0))
    act = pl.BlockSpec((tm, D), lambda i: (i, 0))
    row = pl.BlockSpec((1, D), lambda i: (0, 0))
    return pl.pallas_call(
        body, name="bwd_mix", grid=(nt,),
        in_specs=[HBM, tile(0), tile(1), tile(2), act, act, act, halo_after(0), halo_after(None), halo_after(None),
                  row, row, row, HBM],
        out_specs=[pl.BlockSpec((4, tm, D), lambda i: (1, i, 0)), pl.BlockSpec((8, D), lambda i: (0, 0))],
        out_shape=[SDS(dproj.shape, BF16), SDS((8, D), F32)],
        input_output_aliases={0: 0},
        compiler_params=_params(1))(dproj, proj, proj, proj, conv, dza, dpooled, proj, dza, dpooled, cw0, cw1, cw2, after)


def _bwd_proj(dproj, w_parts, places, x, dx1, gain, tm, after):
    T = x.shape[0]

    def body(places_ref, dp_ref, w0_ref, w1_ref, w2_ref, x_ref, dx1_ref, g_ref, after_ref, dx_ref, dgain_ref, acc_ref):
        del after_ref
        i, s = pl.program_id(0), pl.program_id(1)

        @pl.when((i == 0) & (s == 0))
        def _():
            dgain_ref[...] = jnp.zeros_like(dgain_ref)

        @pl.when(s == 0)
        def _():
            acc_ref[...] = jnp.zeros_like(acc_ref)

        for k, w_ref in enumerate((w0_ref, w1_ref, w2_ref)):
            @pl.when(places_ref[s] == k)
            def _(w_ref=w_ref):
                acc_ref[...] += _mm_nt(dp_ref[...], w_ref[...])

        @pl.when(s == NSPLIT - 1)
        def _():
            dx, dgain = _norm_bwd(acc_ref[...], x_ref[...], g_ref[...])
            dx_ref[...] = dx1_ref[...] + dx
            dgain_ref[...] += dgain

    act = pl.BlockSpec((tm, D), lambda i, s, p: (i, 0))
    row = pl.BlockSpec((1, D), lambda i, s, p: (0, 0))
    weight = lambda k: pl.BlockSpec((None, D, D), lambda i, s, p: (p[NSPLIT * (1 + k) + s], 0, 0))
    return pl.pallas_call(
        body, name="bwd_proj",
        grid_spec=pltpu.PrefetchScalarGridSpec(
            num_scalar_prefetch=1, grid=(T // tm, NSPLIT),
            in_specs=[pl.BlockSpec((None, tm, D), lambda i, s, p: (s, i, 0)), weight(0), weight(1), weight(2),
                      act, act, row, HBM],
            out_specs=[act, row], scratch_shapes=[pltpu.VMEM((tm, D), F32)]),
        out_shape=[SDS((T, D), F32), SDS((1, D), F32)],
        compiler_params=_params(2))(places, dproj, *w_parts, x, dx1, gain, after)


def _adamw_math(w, g, m, v):
    m = ADAM_B1 * m + (1.0 - ADAM_B1) * g
    v = ADAM_B2 * v + (1.0 - ADAM_B2) * (g * g)
    m_hat = m / (1.0 - ADAM_B1 ** ADAM_STEP)
    v_hat = v / (1.0 - ADAM_B2 ** ADAM_STEP)
    delta = -ADAM_LR * (m_hat / (jnp.sqrt(v_hat) + ADAM_EPS) + ADAM_WD * w)
    return delta, m, v


def _row_tile(rows):
    return 256 if rows % 256 == 0 else rows


def _sum_parts(parts, name):
    n_parts, rows, cols = parts.shape
    tr = _row_tile(rows)

    def body(p_ref, g_ref):
        g = p_ref[0].astype(F32)
        for k in range(1, n_parts):
            g = g + p_ref[k].astype(F32)
        g_ref[...] = g

    blk = pl.BlockSpec((tr, cols), lambda i: (i, 0))
    return pl.pallas_call(
        body, name=name, grid=(rows // tr,),
        in_specs=[pl.BlockSpec((n_parts, tr, cols), lambda i: (0, i, 0))], out_specs=blk,
        out_shape=SDS((rows, cols), F32), compiler_params=_params(1))(parts)


def _adamw(ws, gs, ms, vs, name, from_parts, steps):
    n = len(ws)

    def body(*refs):
        for a in range(n):
            w_ref, g_ref, m_ref, v_ref = refs[4 * a:4 * a + 4]
            go_ref, d_ref, mo_ref, vo_ref = refs[4 * n + 4 * a:4 * n + 4 * a + 4]
            if from_parts:
                g = g_ref[0].astype(F32)
                for k in range(1, g_ref.shape[0]):
                    g = g + g_ref[k].astype(F32)
            else:
                g = g_ref[...]
            go_ref[...] = g
            d_ref[...], mo_ref[...], vo_ref[...] = _adamw_math(w_ref[...], g, m_ref[...], v_ref[...])

    in_specs, out_specs, out_shape, operands = [], [], [], []
    for w, g, m, v in zip(ws, gs, ms, vs):
        rows, cols = w.shape
        blk = pl.BlockSpec((rows // steps, cols), lambda i: (i, 0))
        g_spec = pl.BlockSpec((g.shape[0], rows // steps, cols), lambda i: (0, i, 0)) if from_parts else blk
        in_specs += [blk, g_spec, blk, blk]
        out_specs += [blk] * 4
        out_shape += [SDS((rows, cols), F32)] * 4
        operands += [w, g, m, v]
    outs = pl.pallas_call(body, name=name, grid=(steps,), in_specs=in_specs, out_specs=out_specs, out_shape=out_shape,
                          compiler_params=_params(1))(*operands)
    return [outs[4 * a:4 * a + 4] for a in range(n)]


def _peer(k, x, y, c):
    return ((1 - x) if k & 4 else x, (1 - y) if k & 2 else y, (1 - c) if k & 1 else c)


SEM = pl.BlockSpec(memory_space=pltpu.SEMAPHORE)
IN_HBM = pl.BlockSpec(memory_space=pltpu.HBM)
DATAFLOW = pltpu.SideEffectType.DATAFLOW_SIDE_EFFECTING
TOKEN_SHAPE = (8, 128)


OTHER_CHIPS = (2, 4, 6)


def _place(x, y, c):
    return 4 * x + 2 * y + c


def _plan_gather_chips(n, ks=(1,) + OTHER_CHIPS):
    def plan(refs, x, y, c, arriving):
        out = []
        for a in range(n):
            for k in ks:
                there = _place(*_peer(k, x, y, c))
                out.append((refs[a].at[_place(x, y, c)], refs[a].at[there if arriving else _place(x, y, c)], k))
        return out
    return plan, n * len(ks)


def _plan_gather_sibling(n, ks=OTHER_CHIPS):
    def plan(refs, x, y, c, arriving):
        out = []
        for a in range(n):
            for k in ks:
                px, py, pc = _peer(k, x, y, c)
                mine, theirs = _place(px, py, pc), _place(px, py, 1 - pc)
                out.append((refs[a].at[mine], refs[a].at[theirs if arriving else mine], 1))
        return out
    return plan, n * len(ks)


def _plan_pair():
    def plan(refs, x, y, c, arriving):
        return [(refs[0].at[c], refs[0].at[(1 - c) if arriving else c], 1)]
    return plan, 1


def _plan_far_chip():
    def plan(refs, x, y, c, arriving):
        return [(refs[0].at[c], refs[1].at[c], 6)]
    return plan, 1


def _plan_far_sibling():
    def plan(refs, x, y, c, arriving):
        return [(refs[0].at[c], refs[0].at[(1 - c) if arriving else c], 1)]
    return plan, 1


def _plan_scatter_sibling(n):
    def plan(refs, x, y, c, arriving):
        out = []
        for a in range(n):
            for q in range(4):
                out.append((refs[a].at[2 * q + (1 - c)], refs[n + a].at[q], 1))
        return out
    return plan, n * 4


def _plan_scatter_chips(n):
    def plan(refs, x, y, c, arriving):
        out = []
        for a in range(n):
            for k in OTHER_CHIPS:
                px, py, _ = _peer(k, x, y, c)
                out.append((refs[a].at[2 * px + py], refs[n + a].at[(2 * px + py) if arriving else (2 * x + y)], k))
        return out
    return plan, n * 3


def _remote(src, dst, send_sems, recv_sems, i, k):
    x, y, c = (lax.axis_index(n) for n in AXES)
    return pltpu.make_async_remote_copy(src_ref=src, dst_ref=dst, send_sem=send_sems.at[i], recv_sem=recv_sems.at[i],
                                        device_id=_peer(k, x, y, c), device_id_type=pl.DeviceIdType.MESH)


def _copies_start(groups, name, after):
    ng = len(groups)
    total = sum(len(bufs) for bufs, _ in groups)

    def body(*refs):
        sems = refs[1 + total:1 + total + 2 * ng]
        x, y, c = (lax.axis_index(n) for n in AXES)
        off = 1
        for gi, (bufs, (plan, _)) in enumerate(groups):
            for i, (src, dst, k) in enumerate(plan(refs[off:off + len(bufs)], x, y, c, False)):
                _remote(src, dst, sems[2 * gi], sems[2 * gi + 1], i, k).start()
            off += len(bufs)
        refs[-1][...] = jnp.zeros(TOKEN_SHAPE, F32)

    sem_shapes = [pltpu.SemaphoreType.DMA((count,)) for _, (_, count) in groups for _ in range(2)]
    flat = [b for bufs, _ in groups for b in bufs]
    outs = pl.pallas_call(
        body, name=name,
        in_specs=[HBM] + [IN_HBM] * total,
        out_specs=[SEM] * (2 * ng) + [IN_HBM] * total + [pl.BlockSpec(memory_space=pltpu.VMEM)],
        out_shape=sem_shapes + [pltpu.HBM(b.shape, b.dtype) for b in flat] + [SDS(TOKEN_SHAPE, F32)],
        input_output_aliases={1 + i: 2 * ng + i for i in range(total)},
        compiler_params=pltpu.CompilerParams(has_side_effects=DATAFLOW),
    )(after, *[pltpu.with_memory_space_constraint(b, pltpu.HBM) for b in flat])
    handles, off = [], 2 * ng
    for gi, (bufs, _) in enumerate(groups):
        handles.append((outs[2 * gi], outs[2 * gi + 1], list(outs[off:off + len(bufs)])))
        off += len(bufs)
    return handles, outs[-1]


def _copies_wait_start(handle, plan, pass_on, more, name, after):
    send_sems, recv_sems, bufs = handle
    n = len(bufs)
    idx, (pass_plan, pass_count) = pass_on
    total = sum(len(b) for b, _ in more)
    ng = 1 + len(more)

    def body(*refs):
        x, y, c = (lax.axis_index(a) for a in AXES)
        waited = refs[1:1 + n]
        outs = refs[3 + n + total:]
        new_sems = outs[n + total:n + total + 2 * ng]
        for i, (src, dst, k) in enumerate(plan[0](waited, x, y, c, True)):
            copy = _remote(src, dst, refs[1 + n + total], refs[2 + n + total], i, k)
            copy.wait_send()
            copy.wait_recv()
        for i, (src, dst, k) in enumerate(pass_plan([waited[j] for j in idx], x, y, c, False)):
            _remote(src, dst, new_sems[0], new_sems[1], i, k).start()
        off = 1 + n
        for gi, (b, (p, _)) in enumerate(more):
            for i, (src, dst, k) in enumerate(p(refs[off:off + len(b)], x, y, c, False)):
                _remote(src, dst, new_sems[2 + 2 * gi], new_sems[3 + 2 * gi], i, k).start()
            off += len(b)
        outs[-1][...] = jnp.zeros(TOKEN_SHAPE, F32)

    flat = list(bufs) + [a for b, _ in more for a in b]
    sem_shapes = [pltpu.SemaphoreType.DMA((count,)) for count in [pass_count] + [cnt for _, (_, cnt) in more] for _ in range(2)]
    outs = pl.pallas_call(
        body, name=name,
        in_specs=[HBM] + [IN_HBM] * (n + total) + [SEM, SEM],
        out_specs=[IN_HBM] * (n + total) + [SEM] * (2 * ng) + [pl.BlockSpec(memory_space=pltpu.VMEM)],
        out_shape=[pltpu.HBM(b.shape, b.dtype) for b in flat] + sem_shapes + [SDS(TOKEN_SHAPE, F32)],
        input_output_aliases={1 + i: i for i in range(n + total)},
        compiler_params=pltpu.CompilerParams(has_side_effects=DATAFLOW),
    )(after, *[pltpu.with_memory_space_constraint(b, pltpu.HBM) for b in flat], send_sems, recv_sems)
    thru = list(outs[:n])
    sems_out = outs[n + total:n + total + 2 * ng]
    handles = [(sems_out[0], sems_out[1], [thru[j] for j in idx])]
    off = n
    for gi, (b, _) in enumerate(more):
        handles.append((sems_out[2 + 2 * gi], sems_out[3 + 2 * gi], list(outs[off:off + len(b)])))
        off += len(b)
    return thru, handles, outs[-1]


def _copies_wait(handle, plan, name, *after):
    send_sems, recv_sems, bufs = handle
    n = len(bufs)

    def body(*refs):
        x, y, c = (lax.axis_index(a) for a in AXES)
        for i, (src, dst, k) in enumerate(plan[0](refs[:n], x, y, c, True)):
            copy = _remote(src, dst, refs[n], refs[n + 1], i, k)
            copy.wait_send()
            copy.wait_recv()

    return pl.pallas_call(
        body, name=name,
        in_specs=[IN_HBM] * n + [SEM, SEM] + [HBM] * len(after), out_specs=[IN_HBM] * n,
        out_shape=[pltpu.HBM(b.shape, b.dtype) for b in bufs],
        input_output_aliases={i: i for i in range(n)},
        compiler_params=pltpu.CompilerParams(has_side_effects=DATAFLOW),
    )(*bufs, send_sems, recv_sems, *after)


def _pair_sums(mine, theirs, c, chip, name):
    n = len(mine)

    def body(where_ref, *refs):
        q = pl.program_id(0)
        for a in range(n):
            total = (refs[a][...].astype(F32) + refs[n + a][...].astype(F32)).astype(BF16)
            refs[2 * n + a][...] = total

            @pl.when(q == where_ref[1])
            def _():
                refs[3 * n + a][...] = total

    block = lambda t: (None,) + t.shape[1:]
    zeros = lambda t: (0,) * (t.ndim - 1)
    outs = pl.pallas_call(
        body, name=name,
        grid_spec=pltpu.PrefetchScalarGridSpec(
            num_scalar_prefetch=1, grid=(4,),
            in_specs=[pl.BlockSpec(block(t), lambda q, w, z=zeros(t): (2 * q + w[0],) + z) for t in theirs]
            + [pl.BlockSpec(block(t), lambda q, w, z=zeros(t): (q,) + z) for t in theirs],
            out_specs=[pl.BlockSpec(block(t), lambda q, w, z=zeros(t): (q,) + z) for t in theirs]
            + [pl.BlockSpec(block(t), lambda q, w, z=zeros(t): (w[1],) + z) for t in theirs]),
        out_shape=[SDS(t.shape, BF16) for t in theirs] * 2,
        compiler_params=_params(1))(jnp.stack([c, chip]).astype(jnp.int32), *mine, *theirs)
    return list(outs[:n]), list(outs[n:])


def _local_step(x, mem, target, gains, get, put, flush, tm_huge=2048, tm_big=1024, tm_mid=512, tm_small=256):
    g_mix, pscale, g_mem, g_ffn, g_fin = gains
    T = x.shape[0]
    tm_huge, tm_big, tm_mid, tm_small = min(tm_huge, T), min(tm_big, T), min(tm_mid, T), min(tm_small, T)
    tn = DFF // 2

    w_pair, w_ids, p_ids = get("in_pair", x)
    proj, h = _fwd_proj(x, g_mix, w_pair, w_ids, p_ids, tm_huge)
    w_near, w_ids, p_ids = get("in_near", h)
    proj = _fwd_proj_more(h, w_near, proj, w_ids, p_ids, tm_huge, "fwd_proj_near")
    w_far, w_ids, p_ids = get("in_far", proj)
    proj = _fwd_proj_more(h, w_far, proj, w_ids, p_ids, tm_huge, "fwd_proj_far")
    cw0, cw1, cw2, w_co, w_pool, w_kv = get("mix", proj)
    za, conv, pooled, ya, yp, kv, memn = _fwd_mix(proj, cw0, cw1, cw2, w_co, w_pool, mem, g_mem, w_kv, tm_mid)
    w_xo, w_o = get("merge", ya)
    o, yx, merged, x1, h2 = _fwd_merge(proj, ya, yp, x, kv, w_xo, w_o, pscale, g_ffn, tm_mid)
    wg_t, wu_t = get("gate_up", x1)
    get("down", x1, early=True)
    gate, up, act = _fwd_ffn_up(h2, wg_t, wu_t, tm_big, tn)
    (w_d,) = get("down", gate)
    dx2, loss, dg_fin = _fwd_ffn_down_loss(act, w_d, x1, target, g_fin, tm_mid)

    dgate, dup = _bwd_ffn_down(dx2, w_d, gate, up, tm_big, tn)
    dx1, dg_ffn = _bwd_ffn_up(dgate, dup, wg_t, wu_t, x1, dx2, g_ffn, tm_small)
    dw_d = _wgrad_dense(act, dx2, "wgrad_down", tm_big, g_mix)
    dwg_t = _wgrad_dense(dgate, h2, "wgrad_gate", tm_big, g_mix)
    dwu_t = _wgrad_dense(dup, h2, "wgrad_up", tm_big, g_mix)
    token = put("ffn", (dwg_t, dwu_t, dw_d))

    dproj, dya, dyx, dza, do, dpooled, dpscale, dw_pool = _bwd_merge(
        dx1, proj, ya, yp, yx, pooled, pscale, w_o, w_co, w_xo, w_pool, tm_mid, token)
    token = flush(dya)
    dw_o = _wgrad_dense(merged, dx1, "wgrad_out", tm_big, token)
    dw_co = _wgrad_dense(za, dya, "wgrad_conv_out", tm_big, token)
    dw_xo = _wgrad_dense(o, dyx, "wgrad_xattn_out", tm_big, token)
    dproj, dw_kv, dg_mem = _bwd_attn(dproj, proj, do, kv, memn, w_kv, mem, g_mem, tm_big)
    token = put("mix", (dw_co, dw_xo, dw_o, dw_pool, dw_kv))

    dproj, dcw = _bwd_mix(dproj, proj, conv, dza, dpooled, cw0, cw1, cw2, tm_mid, token)
    token = flush(dcw)
    dw_in = _wgrad(h, dproj, name="wgrad_in", groups=NSPLIT, a_cols=D, b_cols=D, tt=tm_huge,
                   a_index=lambda g, k, t: (t, 0), b_index=lambda g, k, t: (g, t, 0),
                   o_index=lambda g, k, t: (_slot_group(g), 0, 0), out_shape=(NSPLIT, D, D), after=token)
    token = flush(put("in", (dw_in,)))
    w_pair, places = get("in_places", None)
    grad_x, dg_mix = _bwd_proj(dproj, (w_pair, w_near, w_far), places, x, dx1, g_mix, tm_big, token)

    small = jnp.concatenate([dg_mix, dpscale, dg_mem, dg_ffn, dg_fin, dcw[0:3], loss], axis=0)
    return grad_x, small


def kernel(x, mem, norm_mix, w_in, conv_w, w_conv_out, w_pool, pool_scale, norm_mem, w_kv, w_xattn_out, w_out, norm_ffn, w_gate, w_up, w_down, norm_final, loss_target, m_norm_mix, m_w_in, m_conv_w, m_w_conv_out, m_w_pool, m_pool_scale, m_norm_mem, m_w_kv, m_w_xattn_out, m_w_out, m_norm_ffn, m_w_gate, m_w_up, m_w_down, m_norm_final, v_norm_mix, v_w_in, v_conv_w, v_w_conv_out, v_w_pool, v_pool_scale, v_norm_mem, v_w_kv, v_w_xattn_out, v_w_out, v_norm_ffn, v_w_gate, v_w_up, v_w_down, v_norm_final):
    T = x.shape[1]
    rows = D // NDEV
    ffb = DFF // NDEV
    prow = HD // NDEV
    me = 4 * lax.axis_index("x") + 2 * lax.axis_index("y") + lax.axis_index("c")

    shards = [w_in[0].astype(BF16), w_conv_out[0].astype(BF16), w_xattn_out[0].astype(BF16), w_out[0].astype(BF16),
              w_pool[0].astype(BF16).reshape(NPOOL * prow, HD), w_kv[0].astype(BF16),
              w_gate[0].T.astype(BF16), w_up[0].T.astype(BF16), w_down[0].astype(BF16),
              jnp.pad(conv_w[0], ((0, 5), (0, 0)))]

    cx, cy, cc = (lax.axis_index(n) for n in AXES)
    chip = 2 * cx + cy

    def land(own, index, slots):
        return lax.dynamic_update_index_in_dim(lax.empty((slots,) + own.shape, own.dtype), own, index, 0)

    needed = ["in_pair", "in_near", "in_far", "mix", "merge", "gate_up", "down"]
    members = {"mix": [9, 1, 4, 5], "merge": [2, 3], "gate_up": [6, 7], "down": [8]}
    near = (2, 4)
    plans = {"in_pair": _plan_pair(), "in_near": _plan_gather_chips(1, near), "in_far": _plan_far_chip()}
    plans.update({n: _plan_gather_chips(len(members[n])) for n in members})
    g_bufs = {"in_pair": [land(shards[0], cc, 2)], "in_near": [land(shards[0], me, NDEV)],
              "in_far": [None, lax.empty((2, D, D), BF16)]}
    g_bufs.update({n: [land(shards[i], me, NDEV) for i in members[n]] for n in members})
    first_handles, _ = _copies_start([(g_bufs[n], plans[n]) for n in needed[:2]], "gather_start", x)
    g_handles = dict(zip(needed[:2], first_handles))
    pair_ids = jnp.array([0, 1], jnp.int32)

    on_last_leg = {}

    def get(group, after, early=False):
        if group == "in_places":
            return g_bufs["in_pair"][0], _w_in_places()[chip]
        if group == "in_pair":
            rest = [b for n in members for b in g_bufs[n]]
            g_bufs["in_far"][0], = _copies_wait(g_handles[group], plans[group], "gather_wait_" + group, after, *rest)
            return g_bufs["in_far"][0], pair_ids, (2 * chip + pair_ids).astype(jnp.int32)
        if group not in on_last_leg:
            n_bufs = len(g_bufs[group])
            if group == "in_near":
                landed, plan, more = [0], _plan_gather_sibling(1, near), [(g_bufs[n], plans[n]) for n in needed[2:]]
            elif group == "in_far":
                landed, plan, more = [1], _plan_far_sibling(), []
            else:
                landed, plan, more = list(range(n_bufs)), _plan_gather_sibling(n_bufs), []
            thru, handles, token = _copies_wait_start(g_handles[group], plans[group], (landed, plan), more,
                                                      "gather_pass_" + group, after)
            if group == "in_far":
                g_bufs["in_pair"] = thru[:1]
            g_handles.update(zip(needed[2:], handles[1:]))
            on_last_leg[group] = (handles[0], plan, token)
        if early:
            return None
        handle, plan, token = on_last_leg[group]
        got = _copies_wait(handle, plan, "gather_passed_" + group, after if group in ("gate_up", "down") else token)
        if group == "in_near":
            groups = jnp.stack([me ^ k for k in (2, 3, 4, 5)]).astype(jnp.int32)
            return got[0], groups, groups
        if group == "in_far":
            return got[0], pair_ids, (2 * (3 - chip) + pair_ids).astype(jnp.int32)
        if group == "mix":
            cw_g, w_co_g, w_pool_g, w_kv_g = got
            cw_full = cw_g.transpose(1, 0, 2).reshape(8, D)
            w_pool_full = w_pool_g.reshape(NDEV, NPOOL, prow, HD).transpose(1, 0, 2, 3).reshape(NPOOL, HD, HD)
            return cw_full[0:1], cw_full[1:2], cw_full[2:3], w_co_g.reshape(D, D), w_pool_full, w_kv_g
        if group == "merge":
            return got[0].reshape(D, D), got[1].reshape(D, D)
        return [g.reshape(DFF, D) for g in got]

    started = {}

    def put(group, grads):
        if group == "ffn":
            sends = [g.reshape(NDEV, ffb, D) for g in grads]
        elif group == "mix":
            dw_co, dw_xo, dw_o, dw_pool, dw_kv = grads
            sends = [dw_co.reshape(NDEV, rows, D), dw_xo.reshape(NDEV, rows, D), dw_o.reshape(NDEV, rows, D),
                     dw_pool.reshape(NPOOL, NDEV, prow, HD).transpose(1, 0, 2, 3).reshape(NDEV, NPOOL * prow, HD), dw_kv]
        else:
            sends = list(grads)
        n = len(sends)
        halves = [lax.empty((4,) + s.shape[1:], s.dtype) for s in sends]
        (handle,), token = _copies_start([(sends + halves, _plan_scatter_sibling(n))], "scatter_swap_" + group, norm_mix)
        swapping.append((group, handle, n))
        return token

    swapping = []

    def flush(after):
        group, handle, n = swapping.pop()
        bufs = _copies_wait(handle, _plan_scatter_sibling(n), "scatter_swapped_" + group, after)
        sums, lands = _pair_sums(bufs[:n], bufs[n:], cc, chip, "pair_sums_" + group)
        (handle,), token = _copies_start([(sums + lands, _plan_scatter_chips(n))], "scatter_start_" + group, norm_mix)
        started[group] = (handle, _plan_scatter_chips(n))
        return token

    def take(group, after):
        handle, plan = started[group]
        return _copies_wait(handle, plan, "scatter_wait_" + group, after)[len(handle[2]) // 2:]

    gains = (norm_mix, pool_scale, norm_mem, norm_ffn, norm_final.reshape(1, D))
    grad_x, small = _local_step(x[0], mem[0], loss_target[0], gains, get, put, flush)

    everyone = _plan_gather_chips(1, tuple(range(1, NDEV)))
    (small_handle,), token = _copies_start([([land(small, me, NDEV)], everyone)], "small_start", norm_mix)

    res = {}

    def update(group, names, ws, gs, ms, vs, from_parts, steps, transposed=()):
        view = lambda a, name: a[0].T if name in transposed else a
        flat = [[view(a, name).reshape(g.shape[-2:]) for a in (w, m, v)] for name, w, g, m, v in zip(names, ws, gs, ms, vs)]
        outs = _adamw([f[0] for f in flat], gs, [f[1] for f in flat], [f[2] for f in flat], "adamw_" + group,
                      from_parts, steps)
        for name, w, four in zip(names, ws, outs):
            res[name] = [(o.T if name in transposed else o).reshape(w.shape) for o in four]

    p_g, p_u, p_d = take("ffn", token)
    update("ffn", ["w_gate", "w_up", "w_down"], [w_gate, w_up, w_down], [p_g, p_u, p_d],
           [m_w_gate, m_w_up, m_w_down], [v_w_gate, v_w_up, v_w_down], True, 2, transposed=("w_gate", "w_up"))

    small_all, = _copies_wait(small_handle, everyone, "small_wait", res["w_down"][1])
    small_sum = _sum_parts(small_all, "sum_small")
    loss = small_sum[8, 0]
    g_cw = lax.dynamic_slice_in_dim(small_sum[5:8], me * rows, rows, axis=1)
    update("replicated", ["norm_mix", "pool_scale", "norm_mem", "norm_ffn", "norm_final", "conv_w"],
           [norm_mix, pool_scale, norm_mem, norm_ffn, norm_final, conv_w], [small_sum[k:k + 1] for k in range(5)] + [g_cw],
           [m_norm_mix, m_pool_scale, m_norm_mem, m_norm_ffn, m_norm_final, m_conv_w],
           [v_norm_mix, v_pool_scale, v_norm_mem, v_norm_ffn, v_norm_final, v_conv_w], False, 1)

    p_co, p_xo, p_o, p_pool, p_kv = take("mix", res["conv_w"][1])
    update("mix", ["w_conv_out", "w_xattn_out", "w_out", "w_pool", "w_kv"], [w_conv_out, w_xattn_out, w_out, w_pool, w_kv],
           [p_co, p_xo, p_o, p_pool, p_kv], [m_w_conv_out, m_w_xattn_out, m_w_out, m_w_pool, m_w_kv],
           [v_w_conv_out, v_w_xattn_out, v_w_out, v_w_pool, v_w_kv], True, 2)
    (p_in,) = take("in", res["w_out"][1])
    update("in", ["w_in"], [w_in], [p_in], [m_w_in], [v_w_in], True, 4)
    order = ["norm_mix", "w_in", "conv_w", "w_conv_out", "w_pool", "pool_scale", "norm_mem", "w_kv", "w_xattn_out", "w_out",
             "norm_ffn", "w_gate", "w_up", "w_down", "norm_final"]
    return (loss, grad_x[None], *[res[n][0] for n in order], *[res[n][1] for n in order],
            *[res[n][2] for n in order], *[res[n][3] for n in order])
```

```python
import jax
import jax.numpy as jnp
from jax import lax
from jax.experimental import pallas as pl
from jax.experimental.pallas import tpu as pltpu

F32 = jnp.float32
BF16 = jnp.bfloat16
SDS = jax.ShapeDtypeStruct

AXES = ("x", "y", "c")
NDEV = 8
D = 1024
NSPLIT = 8
NH = 4
HD = D // NH
NPOOL = 4
DFF = 2816
EPS = 1e-6
ATT_SCALE = HD ** -0.5
HALO = 16


def _slot_group(s):
    return jnp.where(s < 3, s + 5, jnp.where(s == 3, 4, s - 4))


SLOT_GROUPS = tuple(s + 5 if s < 3 else 4 if s == 3 else s - 4 for s in range(NSPLIT))


def _w_in_places():
    table = []
    for chip in range(4):
        source = [0 if g // 2 == chip else 2 if g // 2 == 3 - chip else 1 for g in SLOT_GROUPS]
        rows = [source]
        for k in range(3):
            blocks = [g if k == 1 else g % 2 for g in SLOT_GROUPS]
            held = [b for b, src in zip(blocks, source) if src == k][-1]
            rows.append([(held := b if src == k else held) for b, src in zip(blocks, source)])
        table.append([v for row in rows for v in row])
    return jnp.array(table, jnp.int32)


ADAM_LR = 0.001
ADAM_B1 = 0.9
ADAM_B2 = 0.999
ADAM_EPS = 1e-08
ADAM_WD = 0.01
ADAM_STEP = 10

V7X_VMEM_BYTES = 64 * 1024 * 1024
VMEM_LIMIT = V7X_VMEM_BYTES - 8 * 1024 * 1024
HBM = pl.BlockSpec(memory_space=pl.ANY)


def _whole(shape):
    return pl.BlockSpec(shape, lambda *_: (0,) * len(shape), pipeline_mode=pl.Buffered(1))


def _params(n_grid):
    return pltpu.CompilerParams(dimension_semantics=("arbitrary",) * n_grid, vmem_limit_bytes=VMEM_LIMIT)


def _mm(a, b):
    return jnp.dot(a, b, preferred_element_type=F32)


def _mm_nt(a, b):
    return lax.dot_general(a, b, (((1,), (1,)), ((), ())), preferred_element_type=F32)


def _mm_tn(a, b):
    return lax.dot_general(a, b, (((0,), (0,)), ((), ())), preferred_element_type=F32)


def _sigmoid(x):
    return 1.0 / (1.0 + jnp.exp(-x))


def _rms(x):
    return lax.rsqrt(jnp.mean(x * x, axis=-1, keepdims=True) + EPS)


def _norm_bwd(dh, x, gain):
    r = _rms(x)
    xh = x * r
    dxh = dh * gain
    dx = r * (dxh - xh * jnp.mean(dxh * xh, axis=-1, keepdims=True))
    return dx, jnp.sum(dh * xh, axis=0, keepdims=True)


def _col_chunks(n, width=512):
    return [slice(c, min(c + width, n)) for c in range(0, n, width)]


def _shift_down(v, k):
    return pltpu.roll(v, k, 0)


def _shift_up(v, k):
    return pltpu.roll(v, v.shape[0] - k, 0)


def _fwd_proj(x, gain, w_blocks, w_ids, p_ids, tm):
    T = x.shape[0]

    def body(w_ids_ref, p_ids_ref, x_ref, g_ref, w_ref, proj_ref, h_ref):
        del w_ids_ref, p_ids_ref

        @pl.when(pl.program_id(1) == 0)
        def _():
            xf = x_ref[...]
            h_ref[...] = (xf * _rms(xf) * g_ref[...]).astype(BF16)
        proj_ref[...] = _mm(h_ref[...], w_ref[...]).astype(BF16)

    return pl.pallas_call(
        body, name="fwd_proj",
        grid_spec=pltpu.PrefetchScalarGridSpec(
            num_scalar_prefetch=2, grid=(T // tm, w_ids.shape[0]),
            in_specs=[pl.BlockSpec((tm, D), lambda i, j, w, p: (i, 0)), pl.BlockSpec((1, D), lambda i, j, w, p: (0, 0)),
                      pl.BlockSpec((None, D, D), lambda i, j, w, p: (w[j], 0, 0))],
            out_specs=[pl.BlockSpec((None, tm, D), lambda i, j, w, p: (p[j], i, 0)),
                       pl.BlockSpec((tm, D), lambda i, j, w, p: (i, 0))]),
        out_shape=[SDS((NSPLIT, T, D), BF16), SDS((T, D), BF16)],
        compiler_params=_params(2))(w_ids, p_ids, x, gain, w_blocks)


def _fwd_proj_more(h, w_blocks, proj, w_ids, p_ids, tm, name):
    T = h.shape[0]

    def body(w_ids_ref, p_ids_ref, h_ref, w_ref, proj_hbm, proj_ref):
        del w_ids_ref, p_ids_ref, proj_hbm
        proj_ref[...] = _mm(h_ref[...], w_ref[...]).astype(BF16)

    return pl.pallas_call(
        body, name=name,
        grid_spec=pltpu.PrefetchScalarGridSpec(
            num_scalar_prefetch=2, grid=(T // tm, w_ids.shape[0]),
            in_specs=[pl.BlockSpec((tm, D), lambda i, j, w, p: (i, 0)),
                      pl.BlockSpec((None, D, D), lambda i, j, w, p: (w[j], 0, 0)), HBM],
            out_specs=pl.BlockSpec((None, tm, D), lambda i, j, w, p: (p[j], i, 0))),
        out_shape=SDS(proj.shape, BF16), input_output_aliases={4: 0},
        compiler_params=_params(2))(w_ids, p_ids, h, w_blocks, proj)


def _halo_before(split, tm):
    return pl.BlockSpec((None, HALO, D), lambda i: (split, jnp.maximum(i * (tm // HALO) - 1, 0), 0))


def _fwd_mix(proj, cw0, cw1, cw2, w_co, w_pool, mem, gain_mem, w_kv, tm):
    T = proj.shape[1]
    M = mem.shape[0]

    def body(b_ref, c_ref, ua_ref, up_ref, ch_ref, uah_ref, uph_ref, cw0_ref, cw1_ref, cw2_ref, wco_ref, wp_ref,
             mem_ref, gm_ref, wkv_ref, za_ref, conv_ref, pooled_ref, ya_ref, yp_ref, kv_ref, memn_ref):
        i = pl.program_id(0)

        @pl.when(i == 0)
        def _():
            m = mem_ref[...]
            memn = (m * _rms(m) * gm_ref[...]).astype(BF16)
            memn_ref[...] = memn
            for j in range(2 * NH):
                kv_ref[j] = _mm(memn, wkv_ref[j]).astype(BF16)
        keep = jnp.where(i > 0, 1.0, 0.0).astype(F32)
        cu = c_ref[...].astype(F32) * ua_ref[...].astype(F32)
        cu_h = ch_ref[...].astype(F32) * uah_ref[...].astype(F32) * keep
        ext = jnp.concatenate([cu_h, cu], axis=0)
        conv = (cw2_ref[...] * ext + cw1_ref[...] * _shift_down(ext, 1) + cw0_ref[...] * _shift_down(ext, 2))[HALO:]
        za = (b_ref[...].astype(F32) * conv).astype(BF16)
        conv_ref[...] = conv.astype(BF16)
        za_ref[...] = za
        ya_ref[...] = _mm(za, wco_ref[...]).astype(BF16)

        up = up_ref[...].astype(F32)
        ext_u = jnp.concatenate([uph_ref[...].astype(F32) * keep, up], axis=0)
        pos = i * tm + lax.broadcasted_iota(jnp.int32, (tm, HD), 0)
        for g in range(NPOOL):
            cols = slice(g * HD, (g + 1) * HD)
            s = ext_u[:, cols]
            for k in range(g + 1):
                s = s + _shift_down(s, 1 << k)
            cnt = jnp.minimum(pos + 1, 2 << g).astype(F32)
            pooled = (s[HALO:] / cnt - up[:, cols]).astype(BF16)
            pooled_ref[:, cols] = pooled
            yp_ref[:, cols] = _mm(pooled, wp_ref[g]).astype(BF16)

    tile = lambda s: pl.BlockSpec((None, tm, D), lambda i: (s, i, 0))
    row = pl.BlockSpec((1, D), lambda i: (0, 0))
    out = pl.BlockSpec((tm, D), lambda i: (i, 0))
    return pl.pallas_call(
        body, name="fwd_mix", grid=(T // tm,),
        in_specs=[tile(0), tile(1), tile(2), tile(3), _halo_before(1, tm), _halo_before(2, tm), _halo_before(3, tm),
                  row, row, row, _whole((D, D)), _whole((NPOOL, HD, HD)), _whole((M, D)), row, _whole((2 * NH, D, HD))],
        out_specs=[out] * 5 + [pl.BlockSpec((2 * NH, M, HD), lambda i: (0, 0, 0)), pl.BlockSpec((M, D), lambda i: (0, 0))],
        out_shape=[SDS((T, D), BF16)] * 5 + [SDS((2 * NH, M, HD), BF16), SDS((M, D), BF16)],
        compiler_params=_params(1))(proj, proj, proj, proj, proj, proj, proj, cw0, cw1, cw2, w_co, w_pool, mem, gain_mem, w_kv)


def _softmax_rows(s):
    e = jnp.exp(s - jnp.max(s, axis=-1, keepdims=True))
    return e / jnp.sum(e, axis=-1, keepdims=True)


def _fwd_merge(proj, ya, yp, x, kv, w_xo, w_o, pscale, gain_ffn, tm):
    T = x.shape[0]

    def body(q_ref, ga_ref, gp_ref, gx_ref, ya_ref, yp_ref, x_ref, kv_ref, wxo_ref, wo_ref, ps_ref, gf_ref,
             o_ref, yx_ref, merged_ref, x1_ref, h2_ref):
        for h in range(NH):
            cols = slice(h * HD, (h + 1) * HD)
            p = _softmax_rows(_mm_nt(q_ref[:, cols], kv_ref[h]) * ATT_SCALE)
            o_ref[:, cols] = _mm(p.astype(BF16), kv_ref[NH + h]).astype(BF16)
        yx = _mm(o_ref[...], wxo_ref[...])
        yx_ref[...] = yx.astype(BF16)
        merged = (_sigmoid(ga_ref[...].astype(F32)) * ya_ref[...].astype(F32)
                  + _sigmoid(gp_ref[...].astype(F32)) * (yp_ref[...].astype(F32) * ps_ref[...])
                  + _sigmoid(gx_ref[...].astype(F32)) * yx).astype(BF16)
        merged_ref[...] = merged
        x1 = x_ref[...] + _mm(merged, wo_ref[...])
        x1_ref[...] = x1
        h2_ref[...] = (x1 * _rms(x1) * gf_ref[...]).astype(BF16)

    tile = lambda s: pl.BlockSpec((None, tm, D), lambda i: (s, i, 0))
    row = pl.BlockSpec((1, D), lambda i: (0, 0))
    act = pl.BlockSpec((tm, D), lambda i: (i, 0))
    full = _whole((D, D))
    return pl.pallas_call(
        body, name="fwd_merge", grid=(T // tm,),
        in_specs=[tile(4), tile(5), tile(6), tile(7), act, act, act,
                  _whole((2 * NH, kv.shape[1], HD)), full, full, row, row],
        out_specs=[act] * 5,
        out_shape=[SDS((T, D), BF16), SDS((T, D), BF16), SDS((T, D), BF16), SDS((T, D), F32), SDS((T, D), BF16)],
        compiler_params=_params(1))(proj, proj, proj, proj, ya, yp, x, kv, w_xo, w_o, pscale, gain_ffn)


def _fwd_ffn_up(h2, wg_t, wu_t, tm, tn):
    T = h2.shape[0]

    def body(h_ref, wg_ref, wu_ref, gate_ref, up_ref, act_ref):
        for cols in _col_chunks(tn):
            gate = _mm_nt(h_ref[...], wg_ref[cols, :])
            up = _mm_nt(h_ref[...], wu_ref[cols, :])
            gate_ref[:, cols] = gate.astype(BF16)
            up_ref[:, cols] = up.astype(BF16)
            act_ref[:, cols] = (gate * _sigmoid(gate) * up).astype(BF16)

    w = pl.BlockSpec((tn, D), lambda n, i: (n, 0))
    o = pl.BlockSpec((tm, tn), lambda n, i: (i, n))
    return pl.pallas_call(
        body, name="fwd_ffn_up", grid=(DFF // tn, T // tm),
        in_specs=[pl.BlockSpec((tm, D), lambda n, i: (i, 0)), w, w],
        out_specs=[o] * 3, out_shape=[SDS((T, DFF), BF16)] * 3,
        compiler_params=_params(2))(h2, wg_t, wu_t)


def _fwd_ffn_down_loss(act, w_d, x1, target, gain_final, tm):
    T = x1.shape[0]

    def body(act_ref, wd_ref, x1_ref, tgt_ref, g_ref, dx2_ref, loss_ref, dgain_ref):
        @pl.when(pl.program_id(0) == 0)
        def _():
            loss_ref[...] = jnp.zeros_like(loss_ref)
            dgain_ref[...] = jnp.zeros_like(dgain_ref)
        x2 = x1_ref[...] + _mm(act_ref[...], wd_ref[...])
        gain = g_ref[...]
        y = x2 * _rms(x2) * gain
        err = y - tgt_ref[...]
        loss_ref[...] += 0.5 * jnp.sum(jnp.mean(err * err, axis=-1, keepdims=True))
        dx2, dgain = _norm_bwd(err * (1.0 / D), x2, gain)
        dx2_ref[...] = dx2
        dgain_ref[...] += dgain

    act_spec = pl.BlockSpec((tm, D), lambda i: (i, 0))
    row = pl.BlockSpec((1, D), lambda i: (0, 0))
    return pl.pallas_call(
        body, name="fwd_ffn_down_loss", grid=(T // tm,),
        in_specs=[pl.BlockSpec((tm, DFF), lambda i: (i, 0)), _whole((DFF, D)), act_spec, act_spec, row],
        out_specs=[act_spec, pl.BlockSpec((8, D), lambda i: (0, 0)), row],
        out_shape=[SDS((T, D), F32), SDS((8, D), F32), SDS((1, D), F32)],
        compiler_params=_params(1))(act, w_d, x1, target, gain_final)


def _bwd_ffn_down(dx2, w_d, gate, up, tm, tn):
    T = dx2.shape[0]

    def body(dx_ref, wd_ref, gate_ref, up_ref, dgate_ref, dup_ref):
        dx = dx_ref[...].astype(BF16)
        for cols in _col_chunks(tn):
            dact = _mm_nt(dx, wd_ref[cols, :])
            gate = gate_ref[:, cols].astype(F32)
            sg = _sigmoid(gate)
            dgate_ref[:, cols] = (dact * up_ref[:, cols].astype(F32) * (sg * (1.0 + gate * (1.0 - sg)))).astype(BF16)
            dup_ref[:, cols] = (dact * gate * sg).astype(BF16)

    o = pl.BlockSpec((tm, tn), lambda n, i: (i, n))
    return pl.pallas_call(
        body, name="bwd_ffn_down", grid=(DFF // tn, T // tm),
        in_specs=[pl.BlockSpec((tm, D), lambda n, i: (i, 0)), pl.BlockSpec((tn, D), lambda n, i: (n, 0)), o, o],
        out_specs=[o] * 2, out_shape=[SDS((T, DFF), BF16)] * 2,
        compiler_params=_params(2))(dx2, w_d, gate, up)


def _bwd_ffn_up(dgate, dup, wg_t, wu_t, x1, dx2, gain_ffn, tm):
    T = x1.shape[0]

    def body(dg_ref, du_ref, wg_ref, wu_ref, x1_ref, dx2_ref, g_ref, dx1_ref, dgain_ref):
        @pl.when(pl.program_id(0) == 0)
        def _():
            dgain_ref[...] = jnp.zeros_like(dgain_ref)
        dh2 = _mm(dg_ref[...], wg_ref[...]) + _mm(du_ref[...], wu_ref[...])
        dx, dgain = _norm_bwd(dh2, x1_ref[...], g_ref[...])
        dx1_ref[...] = dx2_ref[...] + dx
        dgain_ref[...] += dgain

    wide = pl.BlockSpec((tm, DFF), lambda i: (i, 0))
    w = _whole((DFF, D))
    act = pl.BlockSpec((tm, D), lambda i: (i, 0))
    row = pl.BlockSpec((1, D), lambda i: (0, 0))
    return pl.pallas_call(
        body, name="bwd_ffn_up", grid=(T // tm,),
        in_specs=[wide, wide, w, w, act, act, row], out_specs=[act, row],
        out_shape=[SDS((T, D), F32), SDS((1, D), F32)],
        compiler_params=_params(1))(dgate, dup, wg_t, wu_t, x1, dx2, gain_ffn)


def _wgrad(a, b, *, name, groups, a_cols, b_cols, tt, a_index, b_index, o_index, out_shape, after):
    T = a.shape[0]
    nt = T // tt
    n_a = a.shape[1] // a_cols if groups == 1 else 1

    def body(a_ref, b_ref, after_ref, o_ref, acc_ref):
        del after_ref
        t = pl.program_id(2)

        @pl.when(t == 0)
        def _():
            acc_ref[...] = jnp.zeros_like(acc_ref)
        acc_ref[...] += _mm_tn(a_ref[...].astype(BF16), b_ref[...].astype(BF16))

        @pl.when(t == nt - 1)
        def _():
            o_ref[...] = acc_ref[...].astype(o_ref.dtype)

    return pl.pallas_call(
        body, name=name, grid=(groups, n_a, nt),
        in_specs=[pl.BlockSpec((tt, a_cols), a_index), pl.BlockSpec((None, tt, b_cols), b_index), HBM],
        out_specs=pl.BlockSpec((None, a_cols, b_cols), o_index),
        out_shape=SDS(out_shape, BF16),
        scratch_shapes=[pltpu.VMEM((a_cols, b_cols), F32)],
        compiler_params=_params(3))(a, b, after)


def _wgrad_dense(a, b, name, tt, after, a_cols=None):
    ka, nb = a.shape[1], b.shape[1]
    a_cols = ka if a_cols is None else a_cols
    out = _wgrad(a, b[None], name=name, groups=1, a_cols=a_cols, b_cols=nb, tt=tt,
                 a_index=lambda g, k, t: (t, k), b_index=lambda g, k, t: (0, t, 0),
                 o_index=lambda g, k, t: (k, 0, 0), out_shape=(ka // a_cols, a_cols, nb), after=after)
    return out.reshape(ka, nb)


def _bwd_merge(dx1, proj, ya, yp, yx, pooled, pscale, w_o, w_co, w_xo, w_pool, tm, after):
    T = dx1.shape[0]
    nt = T // tm

    def body(dx1_ref, ga_ref, gp_ref, gx_ref, ya_ref, yp_ref, yx_ref, pooled_ref, ps_ref, wo_ref, wco_ref, wxo_ref, wp_ref,
             after_ref, dgates_ref, dya_ref, dyx_ref, dza_ref, do_ref, dpooled_ref, dps_ref, dwp_ref, acc_ref):
        del after_ref

        @pl.when(pl.program_id(0) == 0)
        def _():
            dps_ref[...] = jnp.zeros_like(dps_ref)
            acc_ref[...] = jnp.zeros_like(acc_ref)
        dmerged = _mm_nt(dx1_ref[...].astype(BF16), wo_ref[...])
        scale = ps_ref[...]
        sa, sp, sx = (_sigmoid(r[...].astype(F32)) for r in (ga_ref, gp_ref, gx_ref))
        ya, yp_pre, yx = (r[...].astype(F32) for r in (ya_ref, yp_ref, yx_ref))
        dgates_ref[0] = (dmerged * ya * sa * (1.0 - sa)).astype(BF16)
        dgates_ref[1] = (dmerged * (yp_pre * scale) * sp * (1.0 - sp)).astype(BF16)
        dgates_ref[2] = (dmerged * yx * sx * (1.0 - sx)).astype(BF16)
        dya = (dmerged * sa).astype(BF16)
        dyx = (dmerged * sx).astype(BF16)
        dyp = dmerged * sp
        dyps = (dyp * scale).astype(BF16)
        dps_ref[...] += jnp.sum(dyp * yp_pre, axis=0, keepdims=True)
        dya_ref[...] = dya
        dyx_ref[...] = dyx
        dza_ref[...] = _mm_nt(dya, wco_ref[...]).astype(BF16)
        do_ref[...] = _mm_nt(dyx, wxo_ref[...]).astype(BF16)
        for g in range(NPOOL):
            cols = slice(g * HD, (g + 1) * HD)
            dpooled_ref[:, cols] = _mm_nt(dyps[:, cols], wp_ref[g]).astype(BF16)
            acc_ref[g] += _mm_tn(pooled_ref[:, cols], dyps[:, cols])

        @pl.when(pl.program_id(0) == nt - 1)
        def _():
            dwp_ref[...] = acc_ref[...].astype(BF16)

    tile = lambda s: pl.BlockSpec((None, tm, D), lambda i: (s, i, 0))
    row = pl.BlockSpec((1, D), lambda i: (0, 0))
    act = pl.BlockSpec((tm, D), lambda i: (i, 0))
    full = _whole((D, D))
    return pl.pallas_call(
        body, name="bwd_merge", grid=(T // tm,),
        in_specs=[act, tile(5), tile(6), tile(7), act, act, act, act, row, full, full, full,
                  _whole((NPOOL, HD, HD)), HBM],
        out_specs=[pl.BlockSpec((3, tm, D), lambda i: (0, i, 0))] + [act] * 5
        + [row, pl.BlockSpec((NPOOL, HD, HD), lambda i: (0, 0, 0))],
        out_shape=[SDS((NSPLIT, T, D), BF16)] + [SDS((T, D), BF16)] * 5 + [SDS((1, D), F32), SDS((NPOOL, HD, HD), BF16)],
        scratch_shapes=[pltpu.VMEM((NPOOL, HD, HD), F32)],
        compiler_params=_params(1))(dx1, proj, proj, proj, ya, yp, yx, pooled, pscale, w_o, w_co, w_xo, w_pool, after)


def _bwd_attn(dproj, proj, do, kv, memn, w_kv, mem, gain_mem, tm):
    T = do.shape[0]
    M = kv.shape[1]
    nt = T // tm

    def body(dproj_hbm, q_ref, do_ref, kv_ref, memn_ref, wkv_ref, mem_ref, gm_ref, dq_ref, dw_ref, dgain_ref, dkv_ref):
        del dproj_hbm

        @pl.when(pl.program_id(0) == 0)
        def _():
            dkv_ref[...] = jnp.zeros_like(dkv_ref)
        for h in range(NH):
            cols = slice(h * HD, (h + 1) * HD)
            q = q_ref[:, cols]
            do_h = do_ref[:, cols]
            p = _softmax_rows(_mm_nt(q, kv_ref[h]) * ATT_SCALE)
            dp = _mm_nt(do_h, kv_ref[NH + h])
            ds = (p * (dp - jnp.sum(dp * p, axis=-1, keepdims=True)) * ATT_SCALE).astype(BF16)
            dq_ref[:, cols] = _mm(ds, kv_ref[h]).astype(BF16)
            dkv_ref[h] += _mm_tn(ds, q)
            dkv_ref[NH + h] += _mm_tn(p.astype(BF16), do_h)

        @pl.when(pl.program_id(0) == nt - 1)
        def _():
            dmemn = jnp.zeros((M, D), F32)
            for j in range(2 * NH):
                dkv_j = dkv_ref[j].astype(BF16)
                dw_ref[j] = _mm_tn(memn_ref[...], dkv_j).astype(BF16)
                dmemn = dmemn + _mm_nt(dkv_j, wkv_ref[j])
            dgain_ref[...] = _norm_bwd(dmemn, mem_ref[...], gm_ref[...])[1]

    row = pl.BlockSpec((1, D), lambda i: (0, 0))
    return pl.pallas_call(
        body, name="bwd_attn", grid=(nt,),
        in_specs=[HBM, pl.BlockSpec((None, tm, D), lambda i: (4, i, 0)), pl.BlockSpec((tm, D), lambda i: (i, 0)),
                  _whole((2 * NH, M, HD)), _whole((M, D)), _whole((2 * NH, D, HD)), _whole((M, D)), row],
        out_specs=[pl.BlockSpec((None, tm, D), lambda i: (3, i, 0)),
                   pl.BlockSpec((2 * NH, D, HD), lambda i: (0, 0, 0)), row],
        out_shape=[SDS(dproj.shape, BF16), SDS((2 * NH, D, HD), BF16), SDS((1, D), F32)],
        scratch_shapes=[pltpu.VMEM((2 * NH, M, HD), F32)],
        input_output_aliases={0: 0},
        compiler_params=_params(1))(dproj, proj, do, kv, memn, w_kv, mem, gain_mem)


def _bwd_mix(dproj, proj, conv, dza, dpooled, cw0, cw1, cw2, tm, after):
    T = dza.shape[0]
    nt = T // tm

    def halo_after(split_or_none):
        idx = lambda i: jnp.minimum((i + 1) * (tm // HALO), T // HALO - 1)
        if split_or_none is None:
            return pl.BlockSpec((HALO, D), lambda i: (idx(i), 0))
        return pl.BlockSpec((None, HALO, D), lambda i: (split_or_none, idx(i), 0))

    def body(dproj_hbm, b_ref, c_ref, ua_ref, conv_ref, dza_ref, dpo_ref, bn_ref, dzan_ref, dpon_ref,
             cw0_ref, cw1_ref, cw2_ref, after_ref, dabcu_ref, dcw_ref):
        del dproj_hbm, after_ref
        i = pl.program_id(0)

        @pl.when(i == 0)
        def _():
            dcw_ref[...] = jnp.zeros_like(dcw_ref)
        keep_next = jnp.where(i < nt - 1, 1.0, 0.0).astype(F32)
        dza = dza_ref[...].astype(F32)
        c = c_ref[...].astype(F32)
        ua = ua_ref[...].astype(F32)
        dconv = dza * b_ref[...].astype(F32)
        dconv_n = dzan_ref[...].astype(F32) * bn_ref[...].astype(F32) * keep_next
        ext = jnp.concatenate([dconv, dconv_n], axis=0)
        dconv_1, dconv_2 = _shift_up(ext, 1)[:tm], _shift_up(ext, 2)[:tm]
        dcu = cw2_ref[...] * dconv + cw1_ref[...] * dconv_1 + cw0_ref[...] * dconv_2
        dabcu_ref[0] = (dza * conv_ref[...].astype(F32)).astype(BF16)
        dabcu_ref[1] = (dcu * ua).astype(BF16)
        dabcu_ref[2] = (dcu * c).astype(BF16)

        cu = c * ua
        dcw_ref[2:3, :] += jnp.sum(dconv * cu, axis=0, keepdims=True)
        dcw_ref[1:2, :] += jnp.sum(dconv_1 * cu, axis=0, keepdims=True)
        dcw_ref[0:1, :] += jnp.sum(dconv_2 * cu, axis=0, keepdims=True)

        dpo = dpo_ref[...].astype(F32)
        ext_dpo = jnp.concatenate([dpo, dpon_ref[...].astype(F32) * keep_next], axis=0)
        pos = i * tm + lax.broadcasted_iota(jnp.int32, (tm + HALO, HD), 0)
        for g in range(NPOOL):
            cols = slice(g * HD, (g + 1) * HD)
            s = ext_dpo[:, cols] / jnp.minimum(pos + 1, 2 << g).astype(F32)
            for k in range(g + 1):
                s = s + _shift_up(s, 1 << k)
            dabcu_ref[3, :, cols] = (s[:tm] - dpo[:, cols]).astype(BF16)

    tile = lambda s: pl.BlockSpec((None, tm, D), lambda i: (s, i, 0))
    act = pl.BlockSpec((tm, D), lambda i: (i, 0))
    row = pl.BlockSpec((1, D), lambda i: (0, 0))
    return pl.pallas_call(
        body, name="bwd_mix", grid=(nt,),
        in_specs=[HBM, tile(0), tile(1), tile(2), act, act, act, halo_after(0), halo_after(None), halo_after(None),
                  row, row, row, HBM],
        out_specs=[pl.BlockSpec((4, tm, D), lambda i: (1, i, 0)), pl.BlockSpec((8, D), lambda i: (0, 0))],
        out_shape=[SDS(dproj.shape, BF16), SDS((8, D), F32)],
        input_output_aliases={0: 0},
        compiler_params=_params(1))(dproj, proj, proj, proj, conv, dza, dpooled, proj, dza, dpooled, cw0, cw1, cw2, after)


def _bwd_proj(dproj, w_parts, places, x, dx1, gain, tm, after):
    T = x.shape[0]

    def body(places_ref, dp_ref, w0_ref, w1_ref, w2_ref, x_ref, dx1_ref, g_ref, after_ref, dx_ref, dgain_ref, acc_ref):
        del after_ref
        i, s = pl.program_id(0), pl.program_id(1)

        @pl.when((i == 0) & (s == 0))
        def _():
            dgain_ref[...] = jnp.zeros_like(dgain_ref)

        @pl.when(s == 0)
        def _():
            acc_ref[...] = jnp.zeros_like(acc_ref)

        for k, w_ref in enumerate((w0_ref, w1_ref, w2_ref)):
            @pl.when(places_ref[s] == k)
            def _(w_ref=w_ref):
                acc_ref[...] += _mm_nt(dp_ref[...], w_ref[...])

        @pl.when(s == NSPLIT - 1)
        def _():
            dx, dgain = _norm_bwd(acc_ref[...], x_ref[...], g_ref[...])
            dx_ref[...] = dx1_ref[...] + dx
            dgain_ref[...] += dgain

    act = pl.BlockSpec((tm, D), lambda i, s, p: (i, 0))
    row = pl.BlockSpec((1, D), lambda i, s, p: (0, 0))
    weight = lambda k: pl.BlockSpec((None, D, D), lambda i, s, p: (p[NSPLIT * (1 + k) + s], 0, 0))
    return pl.pallas_call(
        body, name="bwd_proj",
        grid_spec=pltpu.PrefetchScalarGridSpec(
            num_scalar_prefetch=1, grid=(T // tm, NSPLIT),
            in_specs=[pl.BlockSpec((None, tm, D), lambda i, s, p: (s, i, 0)), weight(0), weight(1), weight(2),
                      act, act, row, HBM],
            out_specs=[act, row], scratch_shapes=[pltpu.VMEM((tm, D), F32)]),
        out_shape=[SDS((T, D), F32), SDS((1, D), F32)],
        compiler_params=_params(2))(places, dproj, *w_parts, x, dx1, gain, after)


def _adamw_math(w, g, m, v):
    m = ADAM_B1 * m + (1.0 - ADAM_B1) * g
    v = ADAM_B2 * v + (1.0 - ADAM_B2) * (g * g)
    m_hat = m / (1.0 - ADAM_B1 ** ADAM_STEP)
    v_hat = v / (1.0 - ADAM_B2 ** ADAM_STEP)
    delta = -ADAM_LR * (m_hat / (jnp.sqrt(v_hat) + ADAM_EPS) + ADAM_WD * w)
    return delta, m, v


def _adamw_small(parts, me, gains, taps):
    n = len(gains)
    cols = D // NDEV

    def body(me_ref, all_ref, mine_ref, *refs):
        del me_ref
        everywhere, here = all_ref[0], mine_ref[0]
        for k in range(1, NDEV):
            everywhere, here = everywhere + all_ref[k], here + mine_ref[k]
        ins, outs = refs[:3 * (n + 1)], refs[3 * (n + 1):]
        for a in range(n + 1):
            w_ref, m_ref, v_ref = ins[3 * a:3 * a + 3]
            go_ref, d_ref, mo_ref, vo_ref = outs[4 * a:4 * a + 4]
            g = everywhere[a:a + 1] if a < n else here[n:n + 3]
            go_ref[...] = g
            d_ref[...], mo_ref[...], vo_ref[...] = _adamw_math(w_ref[...], g, m_ref[...], v_ref[...])
        outs[-1][...] = everywhere[8:9, 0:1]

    whole = lambda shape: pl.BlockSpec(shape, lambda i, me_ref: (0,) * len(shape))
    shapes = [(1, D)] * n + [(3, cols)]
    outs = pl.pallas_call(
        body, name="adamw_replicated",
        grid_spec=pltpu.PrefetchScalarGridSpec(
            num_scalar_prefetch=1, grid=(1,),
            in_specs=[whole(parts.shape), pl.BlockSpec((NDEV, 16, cols), lambda i, me_ref: (0, 0, me_ref[0]))]
            + [whole(s) for s in shapes for _ in range(3)],
            out_specs=[whole(s) for s in shapes for _ in range(4)] + [whole((1, 1))]),
        out_shape=[SDS(s, F32) for s in shapes for _ in range(4)] + [SDS((1, 1), F32)],
        compiler_params=_params(1))(me.reshape(1).astype(jnp.int32), parts, parts, *[a for three in gains + [taps] for a in three])
    return [outs[4 * a:4 * a + 4] for a in range(n + 1)], outs[-1]


def _adamw(ws, gs, ms, vs, name, from_parts, steps):
    n = len(ws)

    def body(*refs):
        for a in range(n):
            w_ref, g_ref, m_ref, v_ref = refs[4 * a:4 * a + 4]
            go_ref, d_ref, mo_ref, vo_ref = refs[4 * n + 4 * a:4 * n + 4 * a + 4]
            if from_parts:
                g = g_ref[0].astype(F32)
                for k in range(1, g_ref.shape[0]):
                    g = g + g_ref[k].astype(F32)
            else:
                g = g_ref[...]
            go_ref[...] = g
            d_ref[...], mo_ref[...], vo_ref[...] = _adamw_math(w_ref[...], g, m_ref[...], v_ref[...])

    in_specs, out_specs, out_shape, operands = [], [], [], []
    for w, g, m, v in zip(ws, gs, ms, vs):
        rows, cols = w.shape
        blk = pl.BlockSpec((rows // steps, cols), lambda i: (i, 0))
        g_spec = pl.BlockSpec((g.shape[0], rows // steps, cols), lambda i: (0, i, 0)) if from_parts else blk
        in_specs += [blk, g_spec, blk, blk]
        out_specs += [blk] * 4
        out_shape += [SDS((rows, cols), F32)] * 4
        operands += [w, g, m, v]
    outs = pl.pallas_call(body, name=name, grid=(steps,), in_specs=in_specs, out_specs=out_specs, out_shape=out_shape,
                          compiler_params=_params(1))(*operands)
    return [outs[4 * a:4 * a + 4] for a in range(n)]


def _peer(k, x, y, c):
    return ((1 - x) if k & 4 else x, (1 - y) if k & 2 else y, (1 - c) if k & 1 else c)


SEM = pl.BlockSpec(memory_space=pltpu.SEMAPHORE)
IN_HBM = pl.BlockSpec(memory_space=pltpu.HBM)
DATAFLOW = pltpu.SideEffectType.DATAFLOW_SIDE_EFFECTING
TOKEN_SHAPE = (8, 128)


OTHER_CHIPS = (2, 4, 6)


def _place(x, y, c):
    return 4 * x + 2 * y + c


def _plan_gather_chips(n, ks=(1,) + OTHER_CHIPS):
    def plan(refs, x, y, c, arriving):
        out = []
        for a in range(n):
            for k in ks:
                there = _place(*_peer(k, x, y, c))
                out.append((refs[a].at[_place(x, y, c)], refs[a].at[there if arriving else _place(x, y, c)], k))
        return out
    return plan, n * len(ks)


def _plan_gather_sibling(n, ks=OTHER_CHIPS):
    def plan(refs, x, y, c, arriving):
        out = []
        for a in range(n):
            for k in ks:
                px, py, pc = _peer(k, x, y, c)
                mine, theirs = _place(px, py, pc), _place(px, py, 1 - pc)
                out.append((refs[a].at[mine], refs[a].at[theirs if arriving else mine], 1))
        return out
    return plan, n * len(ks)


def _plan_pair():
    def plan(refs, x, y, c, arriving):
        return [(refs[0].at[c], refs[0].at[(1 - c) if arriving else c], 1)]
    return plan, 1


def _plan_far_chip():
    def plan(refs, x, y, c, arriving):
        return [(refs[0].at[c], refs[1].at[c], 6)]
    return plan, 1


def _plan_far_sibling():
    def plan(refs, x, y, c, arriving):
        return [(refs[0].at[c], refs[0].at[(1 - c) if arriving else c], 1)]
    return plan, 1


def _plan_scatter_sibling(n):
    def plan(refs, x, y, c, arriving):
        out = []
        for a in range(n):
            for q in range(4):
                out.append((refs[a].at[2 * q + (1 - c)], refs[n + a].at[q], 1))
        return out
    return plan, n * 4


def _plan_scatter_chips(n):
    def plan(refs, x, y, c, arriving):
        out = []
        for a in range(n):
            for k in OTHER_CHIPS:
                px, py, _ = _peer(k, x, y, c)
                out.append((refs[a].at[2 * px + py], refs[n + a].at[(2 * px + py) if arriving else (2 * x + y)], k))
        return out
    return plan, n * 3


def _remote(src, dst, send_sems, recv_sems, i, k):
    x, y, c = (lax.axis_index(n) for n in AXES)
    return pltpu.make_async_remote_copy(src_ref=src, dst_ref=dst, send_sem=send_sems.at[i], recv_sem=recv_sems.at[i],
                                        device_id=_peer(k, x, y, c), device_id_type=pl.DeviceIdType.MESH)


def _copies_start(groups, name, after):
    ng = len(groups)
    total = sum(len(bufs) for bufs, _ in groups)

    def body(*refs):
        sems = refs[1 + total:1 + total + 2 * ng]
        x, y, c = (lax.axis_index(n) for n in AXES)
        off = 1
        for gi, (bufs, (plan, _)) in enumerate(groups):
            for i, (src, dst, k) in enumerate(plan(refs[off:off + len(bufs)], x, y, c, False)):
                _remote(src, dst, sems[2 * gi], sems[2 * gi + 1], i, k).start()
            off += len(bufs)
        refs[-1][...] = jnp.zeros(TOKEN_SHAPE, F32)

    sem_shapes = [pltpu.SemaphoreType.DMA((count,)) for _, (_, count) in groups for _ in range(2)]
    flat = [b for bufs, _ in groups for b in bufs]
    outs = pl.pallas_call(
        body, name=name,
        in_specs=[HBM] + [IN_HBM] * total,
        out_specs=[SEM] * (2 * ng) + [IN_HBM] * total + [pl.BlockSpec(memory_space=pltpu.VMEM)],
        out_shape=sem_shapes + [pltpu.HBM(b.shape, b.dtype) for b in flat] + [SDS(TOKEN_SHAPE, F32)],
        input_output_aliases={1 + i: 2 * ng + i for i in range(total)},
        compiler_params=pltpu.CompilerParams(has_side_effects=DATAFLOW),
    )(after, *[pltpu.with_memory_space_constraint(b, pltpu.HBM) for b in flat])
    handles, off = [], 2 * ng
    for gi, (bufs, _) in enumerate(groups):
        handles.append((outs[2 * gi], outs[2 * gi + 1], list(outs[off:off + len(bufs)])))
        off += len(bufs)
    return handles, outs[-1]


def _copies_wait_start(handle, plan, pass_on, more, name, after):
    send_sems, recv_sems, bufs = handle
    n = len(bufs)
    idx, (pass_plan, pass_count) = pass_on
    total = sum(len(b) for b, _ in more)
    ng = 1 + len(more)

    def body(*refs):
        x, y, c = (lax.axis_index(a) for a in AXES)
        waited = refs[1:1 + n]
        outs = refs[3 + n + total:]
        new_sems = outs[n + total:n + total + 2 * ng]
        for i, (src, dst, k) in enumerate(plan[0](waited, x, y, c, True)):
            copy = _remote(src, dst, refs[1 + n + total], refs[2 + n + total], i, k)
            copy.wait_send()
            copy.wait_recv()
        for i, (src, dst, k) in enumerate(pass_plan([waited[j] for j in idx], x, y, c, False)):
            _remote(src, dst, new_sems[0], new_sems[1], i, k).start()
        off = 1 + n
        for gi, (b, (p, _)) in enumerate(more):
            for i, (src, dst, k) in enumerate(p(refs[off:off + len(b)], x, y, c, False)):
                _remote(src, dst, new_sems[2 + 2 * gi], new_sems[3 + 2 * gi], i, k).start()
            off += len(b)
        outs[-1][...] = jnp.zeros(TOKEN_SHAPE, F32)

    flat = list(bufs) + [a for b, _ in more for a in b]
    sem_shapes = [pltpu.SemaphoreType.DMA((count,)) for count in [pass_count] + [cnt for _, (_, cnt) in more] for _ in range(2)]
    outs = pl.pallas_call(
        body, name=name,
        in_specs=[HBM] + [IN_HBM] * (n + total) + [SEM, SEM],
        out_specs=[IN_HBM] * (n + total) + [SEM] * (2 * ng) + [pl.BlockSpec(memory_space=pltpu.VMEM)],
        out_shape=[pltpu.HBM(b.shape, b.dtype) for b in flat] + sem_shapes + [SDS(TOKEN_SHAPE, F32)],
        input_output_aliases={1 + i: i for i in range(n + total)},
        compiler_params=pltpu.CompilerParams(has_side_effects=DATAFLOW),
    )(after, *[pltpu.with_memory_space_constraint(b, pltpu.HBM) for b in flat], send_sems, recv_sems)
    thru = list(outs[:n])
    sems_out = outs[n + total:n + total + 2 * ng]
    handles = [(sems_out[0], sems_out[1], [thru[j] for j in idx])]
    off = n
    for gi, (b, _) in enumerate(more):
        handles.append((sems_out[2 + 2 * gi], sems_out[3 + 2 * gi], list(outs[off:off + len(b)])))
        off += len(b)
    return thru, handles, outs[-1]


def _copies_wait(handle, plan, name, *after):
    send_sems, recv_sems, bufs = handle
    n = len(bufs)

    def body(*refs):
        x, y, c = (lax.axis_index(a) for a in AXES)
        for i, (src, dst, k) in enumerate(plan[0](refs[:n], x, y, c, True)):
            copy = _remote(src, dst, refs[n], refs[n + 1], i, k)
            copy.wait_send()
            copy.wait_recv()

    return pl.pallas_call(
        body, name=name,
        in_specs=[IN_HBM] * n + [SEM, SEM] + [HBM] * len(after), out_specs=[IN_HBM] * n,
        out_shape=[pltpu.HBM(b.shape, b.dtype) for b in bufs],
        input_output_aliases={i: i for i in range(n)},
        compiler_params=pltpu.CompilerParams(has_side_effects=DATAFLOW),
    )(*bufs, send_sems, recv_sems, *after)


def _pair_sums(mine, theirs, c, chip, name):
    n = len(mine)

    def body(where_ref, *refs):
        q = pl.program_id(0)
        for a in range(n):
            total = (refs[a][...].astype(F32) + refs[n + a][...].astype(F32)).astype(BF16)
            refs[2 * n + a][...] = total

            @pl.when(q == where_ref[1])
            def _():
                refs[3 * n + a][...] = total

    block = lambda t: (None,) + t.shape[1:]
    zeros = lambda t: (0,) * (t.ndim - 1)
    outs = pl.pallas_call(
        body, name=name,
        grid_spec=pltpu.PrefetchScalarGridSpec(
            num_scalar_prefetch=1, grid=(4,),
            in_specs=[pl.BlockSpec(block(t), lambda q, w, z=zeros(t): (2 * q + w[0],) + z) for t in theirs]
            + [pl.BlockSpec(block(t), lambda q, w, z=zeros(t): (q,) + z) for t in theirs],
            out_specs=[pl.BlockSpec(block(t), lambda q, w, z=zeros(t): (q,) + z) for t in theirs]
            + [pl.BlockSpec(block(t), lambda q, w, z=zeros(t): (w[1],) + z) for t in theirs]),
        out_shape=[SDS(t.shape, BF16) for t in theirs] * 2,
        compiler_params=_params(1))(jnp.stack([c, chip]).astype(jnp.int32), *mine, *theirs)
    return list(outs[:n]), list(outs[n:])


def _local_step(x, mem, target, gains, get, put, flush, tm_huge=2048, tm_big=1024, tm_mid=512, tm_small=256):
    g_mix, pscale, g_mem, g_ffn, g_fin = gains
    T = x.shape[0]
    tm_huge, tm_big, tm_mid, tm_small = min(tm_huge, T), min(tm_big, T), min(tm_mid, T), min(tm_small, T)
    tn = DFF // 2

    w_pair, w_ids, p_ids = get("in_pair", x)
    proj, h = _fwd_proj(x, g_mix, w_pair, w_ids, p_ids, tm_huge)
    w_near, w_ids, p_ids = get("in_near", h)
    proj = _fwd_proj_more(h, w_near, proj, w_ids, p_ids, tm_huge, "fwd_proj_near")
    w_far, w_ids, p_ids = get("in_far", proj)
    proj = _fwd_proj_more(h, w_far, proj, w_ids, p_ids, tm_huge, "fwd_proj_far")
    cw0, cw1, cw2, w_co, w_pool, w_kv = get("mix", proj)
    za, conv, pooled, ya, yp, kv, memn = _fwd_mix(proj, cw0, cw1, cw2, w_co, w_pool, mem, g_mem, w_kv, tm_mid)
    w_xo, w_o = get("merge", ya)
    o, yx, merged, x1, h2 = _fwd_merge(proj, ya, yp, x, kv, w_xo, w_o, pscale, g_ffn, tm_mid)
    wg_t, wu_t = get("gate_up", x1)
    get("down", x1, early=True)
    gate, up, act = _fwd_ffn_up(h2, wg_t, wu_t, tm_big, tn)
    (w_d,) = get("down", gate)
    dx2, loss, dg_fin = _fwd_ffn_down_loss(act, w_d, x1, target, g_fin, tm_mid)

    dgate, dup = _bwd_ffn_down(dx2, w_d, gate, up, tm_big, tn)
    dx1, dg_ffn = _bwd_ffn_up(dgate, dup, wg_t, wu_t, x1, dx2, g_ffn, tm_small)
    dw_d = _wgrad_dense(act, dx2, "wgrad_down", tm_big, g_mix)
    dwg_t = _wgrad_dense(dgate, h2, "wgrad_gate", tm_big, g_mix)
    dwu_t = _wgrad_dense(dup, h2, "wgrad_up", tm_big, g_mix)
    token = put("ffn", (dwg_t, dwu_t, dw_d))

    dproj, dya, dyx, dza, do, dpooled, dpscale, dw_pool = _bwd_merge(
        dx1, proj, ya, yp, yx, pooled, pscale, w_o, w_co, w_xo, w_pool, tm_mid, token)
    token = flush(dya)
    dw_o = _wgrad_dense(merged, dx1, "wgrad_out", tm_big, token)
    dw_co = _wgrad_dense(za, dya, "wgrad_conv_out", tm_big, token)
    dw_xo = _wgrad_dense(o, dyx, "wgrad_xattn_out", tm_big, token)
    dproj, dw_kv, dg_mem = _bwd_attn(dproj, proj, do, kv, memn, w_kv, mem, g_mem, tm_big)
    token = put("mix", (dw_co, dw_xo, dw_o, dw_pool, dw_kv))

    dproj, dcw = _bwd_mix(dproj, proj, conv, dza, dpooled, cw0, cw1, cw2, tm_mid, token)
    token = flush(dcw)
    dw_in = _wgrad(h, dproj, name="wgrad_in", groups=NSPLIT, a_cols=D, b_cols=D, tt=tm_huge,
                   a_index=lambda g, k, t: (t, 0), b_index=lambda g, k, t: (g, t, 0),
                   o_index=lambda g, k, t: (_slot_group(g), 0, 0), out_shape=(NSPLIT, D, D), after=token)
    token = flush(put("in", (dw_in,)))
    w_pair, places = get("in_places", None)
    grad_x, dg_mix = _bwd_proj(dproj, (w_pair, w_near, w_far), places, x, dx1, g_mix, tm_big, token)

    small = jnp.concatenate([dg_mix, dpscale, dg_mem, dg_ffn, dg_fin, dcw[0:3], loss], axis=0)
    return grad_x, small


def kernel(x, mem, norm_mix, w_in, conv_w, w_conv_out, w_pool, pool_scale, norm_mem, w_kv, w_xattn_out, w_out, norm_ffn, w_gate, w_up, w_down, norm_final, loss_target, m_norm_mix, m_w_in, m_conv_w, m_w_conv_out, m_w_pool, m_pool_scale, m_norm_mem, m_w_kv, m_w_xattn_out, m_w_out, m_norm_ffn, m_w_gate, m_w_up, m_w_down, m_norm_final, v_norm_mix, v_w_in, v_conv_w, v_w_conv_out, v_w_pool, v_pool_scale, v_norm_mem, v_w_kv, v_w_xattn_out, v_w_out, v_norm_ffn, v_w_gate, v_w_up, v_w_down, v_norm_final):
    T = x.shape[1]
    rows = D // NDEV
    ffb = DFF // NDEV
    prow = HD // NDEV
    me = 4 * lax.axis_index("x") + 2 * lax.axis_index("y") + lax.axis_index("c")

    shards = [w_in[0].astype(BF16), w_conv_out[0].astype(BF16), w_xattn_out[0].astype(BF16), w_out[0].astype(BF16),
              w_pool[0].astype(BF16).reshape(NPOOL * prow, HD), w_kv[0].astype(BF16),
              w_gate[0].T.astype(BF16), w_up[0].T.astype(BF16), w_down[0].astype(BF16),
              jnp.pad(conv_w[0], ((0, 5), (0, 0)))]

    cx, cy, cc = (lax.axis_index(n) for n in AXES)
    chip = 2 * cx + cy

    def land(own, index, slots):
        return lax.dynamic_update_index_in_dim(lax.empty((slots,) + own.shape, own.dtype), own, index, 0)

    needed = ["in_pair", "in_near", "in_far", "mix", "merge", "gate_up", "down"]
    members = {"mix": [9, 1, 4, 5], "merge": [2, 3], "gate_up": [6, 7], "down": [8]}
    near = (2, 4)
    plans = {"in_pair": _plan_pair(), "in_near": _plan_gather_chips(1, near), "in_far": _plan_far_chip()}
    plans.update({n: _plan_gather_chips(len(members[n])) for n in members})
    g_bufs = {"in_pair": [land(shards[0], cc, 2)], "in_near": [land(shards[0], me, NDEV)],
              "in_far": [None, lax.empty((2, D, D), BF16)]}
    g_bufs.update({n: [land(shards[i], me, NDEV) for i in members[n]] for n in members})
    first_handles, _ = _copies_start([(g_bufs[n], plans[n]) for n in needed[:2]], "gather_start", x)
    g_handles = dict(zip(needed[:2], first_handles))
    pair_ids = jnp.array([0, 1], jnp.int32)

    on_last_leg = {}

    def get(group, after, early=False):
        if group == "in_places":
            return g_bufs["in_pair"][0], _w_in_places()[chip]
        if group == "in_pair":
            rest = [b for n in members for b in g_bufs[n]]
            g_bufs["in_far"][0], = _copies_wait(g_handles[group], plans[group], "gather_wait_" + group, after, *rest)
            return g_bufs["in_far"][0], pair_ids, (2 * chip + pair_ids).astype(jnp.int32)
        if group not in on_last_leg:
            n_bufs = len(g_bufs[group])
            if group == "in_near":
                landed, plan, more = [0], _plan_gather_sibling(1, near), [(g_bufs[n], plans[n]) for n in needed[2:]]
            elif group == "in_far":
                landed, plan, more = [1], _plan_far_sibling(), []
            else:
                landed, plan, more = list(range(n_bufs)), _plan_gather_sibling(n_bufs), []
            thru, handles, token = _copies_wait_start(g_handles[group], plans[group], (landed, plan), more,
                                                      "gather_pass_" + group, after)
            if group == "in_far":
                g_bufs["in_pair"] = thru[:1]
            g_handles.update(zip(needed[2:], handles[1:]))
            on_last_leg[group] = (handles[0], plan, token)
        if early:
            return None
        handle, plan, token = on_last_leg[group]
        got = _copies_wait(handle, plan, "gather_passed_" + group, after if group in ("gate_up", "down") else token)
        if group == "in_near":
            groups = jnp.stack([me ^ k for k in (2, 3, 4, 5)]).astype(jnp.int32)
            return got[0], groups, groups
        if group == "in_far":
            return got[0], pair_ids, (2 * (3 - chip) + pair_ids).astype(jnp.int32)
        if group == "mix":
            cw_g, w_co_g, w_pool_g, w_kv_g = got
            cw_full = cw_g.transpose(1, 0, 2).reshape(8, D)
            w_pool_full = w_pool_g.reshape(NDEV, NPOOL, prow, HD).transpose(1, 0, 2, 3).reshape(NPOOL, HD, HD)
            return cw_full[0:1], cw_full[1:2], cw_full[2:3], w_co_g.reshape(D, D), w_pool_full, w_kv_g
        if group == "merge":
            return got[0].reshape(D, D), got[1].reshape(D, D)
        return [g.reshape(DFF, D) for g in got]

    started = {}

    def put(group, grads):
        if group == "ffn":
            sends = [g.reshape(NDEV, ffb, D) for g in grads]
        elif group == "mix":
            dw_co, dw_xo, dw_o, dw_pool, dw_kv = grads
            sends = [dw_co.reshape(NDEV, rows, D), dw_xo.reshape(NDEV, rows, D), dw_o.reshape(NDEV, rows, D),
                     dw_pool.reshape(NPOOL, NDEV, prow, HD).transpose(1, 0, 2, 3).reshape(NDEV, NPOOL * prow, HD), dw_kv]
        else:
            sends = list(grads)
        n = len(sends)
        halves = [lax.empty((4,) + s.shape[1:], s.dtype) for s in sends]
        (handle,), token = _copies_start([(sends + halves, _plan_scatter_sibling(n))], "scatter_swap_" + group, norm_mix)
        swapping.append((group, handle, n))
        return token

    swapping = []

    def flush(after):
        group, handle, n = swapping.pop()
        bufs = _copies_wait(handle, _plan_scatter_sibling(n), "scatter_swapped_" + group, after)
        sums, lands = _pair_sums(bufs[:n], bufs[n:], cc, chip, "pair_sums_" + group)
        (handle,), token = _copies_start([(sums + lands, _plan_scatter_chips(n))], "scatter_start_" + group, norm_mix)
        started[group] = (handle, _plan_scatter_chips(n))
        return token

    def take(group, after):
        handle, plan = started[group]
        return _copies_wait(handle, plan, "scatter_wait_" + group, after)[len(handle[2]) // 2:]

    gains = (norm_mix, pool_scale, norm_mem, norm_ffn, norm_final.reshape(1, D))
    grad_x, small = _local_step(x[0], mem[0], loss_target[0], gains, get, put, flush)

    everyone = _plan_gather_chips(1, tuple(range(1, NDEV)))
    (small_handle,), token = _copies_start([([land(small, me, NDEV)], everyone)], "small_start", norm_mix)

    res = {}

    def update(group, names, ws, gs, ms, vs, from_parts, steps, transposed=()):
        view = lambda a, name: a[0].T if name in transposed else a
        flat = [[view(a, name).reshape(g.shape[-2:]) for a in (w, m, v)] for name, w, g, m, v in zip(names, ws, gs, ms, vs)]
        outs = _adamw([f[0] for f in flat], gs, [f[1] for f in flat], [f[2] for f in flat], "adamw_" + group,
                      from_parts, steps)
        for name, w, four in zip(names, ws, outs):
            res[name] = [(o.T if name in transposed else o).reshape(w.shape) for o in four]

    p_g, p_u, p_d = take("ffn", token)
    update("ffn", ["w_gate", "w_up", "w_down"], [w_gate, w_up, w_down], [p_g, p_u, p_d],
           [m_w_gate, m_w_up, m_w_down], [v_w_gate, v_w_up, v_w_down], True, 2, transposed=("w_gate", "w_up"))

    small_all, = _copies_wait(small_handle, everyone, "small_wait", res["w_down"][1])
    replicated = {"norm_mix": (norm_mix, m_norm_mix, v_norm_mix), "pool_scale": (pool_scale, m_pool_scale, v_pool_scale),
                  "norm_mem": (norm_mem, m_norm_mem, v_norm_mem), "norm_ffn": (norm_ffn, m_norm_ffn, v_norm_ffn),
                  "norm_final": (norm_final, m_norm_final, v_norm_final), "conv_w": (conv_w, m_conv_w, v_conv_w)}
    rows_of = lambda three, shape: [a.reshape(shape) for a in three]
    outs, loss = _adamw_small(small_all, me, [rows_of(replicated[n], (1, D)) for n in list(replicated)[:5]],
                              rows_of(replicated["conv_w"], (3, rows)))
    for (name, three), four in zip(replicated.items(), outs):
        res[name] = [o.reshape(three[0].shape) for o in four]
    loss = loss.reshape(())

    p_co, p_xo, p_o, p_pool, p_kv = take("mix", res["conv_w"][1])
    update("mix", ["w_conv_out", "w_xattn_out", "w_out", "w_pool", "w_kv"], [w_conv_out, w_xattn_out, w_out, w_pool, w_kv],
           [p_co, p_xo, p_o, p_pool, p_kv], [m_w_conv_out, m_w_xattn_out, m_w_out, m_w_pool, m_w_kv],
           [v_w_conv_out, v_w_xattn_out, v_w_out, v_w_pool, v_w_kv], True, 2)
    (p_in,) = take("in", res["w_out"][1])
    update("in", ["w_in"], [w_in], [p_in], [m_w_in], [v_w_in], True, 4)
    order = ["norm_mix", "w_in", "conv_w", "w_conv_out", "w_pool", "pool_scale", "norm_mem", "w_kv", "w_xattn_out", "w_out",
             "norm_ffn", "w_gate", "w_up", "w_down", "norm_final"]
    return (loss, grad_x[None], *[res[n][0] for n in order], *[res[n][1] for n in order],
            *[res[n][2] for n in order], *[res[n][3] for n in order])
```

```python
import jax
import jax.numpy as jnp
from jax import lax
from jax.experimental import pallas as pl
from jax.experimental.pallas import tpu as pltpu

F32 = jnp.float32
BF16 = jnp.bfloat16
SDS = jax.ShapeDtypeStruct

AXES = ("x", "y", "c")
NDEV = 8
D = 1024
NSPLIT = 8
NH = 4
HD = D // NH
NPOOL = 4
DFF = 2816
EPS = 1e-6
ATT_SCALE = HD ** -0.5
HALO = 16


def _slot_group(s):
    return jnp.where(s < 3, s + 5, jnp.where(s == 3, 4, s - 4))


SLOT_GROUPS = tuple(s + 5 if s < 3 else 4 if s == 3 else s - 4 for s in range(NSPLIT))


def _w_in_places():
    table = []
    for chip in range(4):
        source = [0 if g // 2 == chip else 2 if g // 2 == 3 - chip else 1 for g in SLOT_GROUPS]
        rows = [source]
        for k in range(3):
            blocks = [g if k == 1 else g % 2 for g in SLOT_GROUPS]
            held = [b for b, src in zip(blocks, source) if src == k][-1]
            rows.append([(held := b if src == k else held) for b, src in zip(blocks, source)])
        table.append([v for row in rows for v in row])
    return jnp.array(table, jnp.int32)


ADAM_LR = 0.001
ADAM_B1 = 0.9
ADAM_B2 = 0.999
ADAM_EPS = 1e-08
ADAM_WD = 0.01
ADAM_STEP = 10

V7X_VMEM_BYTES = 64 * 1024 * 1024
VMEM_LIMIT = V7X_VMEM_BYTES - 8 * 1024 * 1024
HBM = pl.BlockSpec(memory_space=pl.ANY)


def _tap(cw_ref, t):
    return jnp.concatenate([cw_ref[d, t:t + 1, :] for d in range(NDEV)], axis=1)


def _pool_map(wp_ref, g):
    rows = HD // NDEV
    return jnp.concatenate([wp_ref[d, g * rows:(g + 1) * rows, :] for d in range(NDEV)], axis=0)


def _whole(shape):
    return pl.BlockSpec(shape, lambda *_: (0,) * len(shape), pipeline_mode=pl.Buffered(1))


def _params(n_grid):
    return pltpu.CompilerParams(dimension_semantics=("arbitrary",) * n_grid, vmem_limit_bytes=VMEM_LIMIT)


def _mm(a, b):
    return jnp.dot(a, b, preferred_element_type=F32)


def _mm_nt(a, b):
    return lax.dot_general(a, b, (((1,), (1,)), ((), ())), preferred_element_type=F32)


def _mm_tn(a, b):
    return lax.dot_general(a, b, (((0,), (0,)), ((), ())), preferred_element_type=F32)


def _sigmoid(x):
    return 1.0 / (1.0 + jnp.exp(-x))


def _rms(x):
    return lax.rsqrt(jnp.mean(x * x, axis=-1, keepdims=True) + EPS)


def _norm_bwd(dh, x, gain):
    r = _rms(x)
    xh = x * r
    dxh = dh * gain
    dx = r * (dxh - xh * jnp.mean(dxh * xh, axis=-1, keepdims=True))
    return dx, jnp.sum(dh * xh, axis=0, keepdims=True)


def _col_chunks(n, width=512):
    return [slice(c, min(c + width, n)) for c in range(0, n, width)]


def _shift_down(v, k):
    return pltpu.roll(v, k, 0)


def _shift_up(v, k):
    return pltpu.roll(v, v.shape[0] - k, 0)


def _fwd_proj(x, gain, w_blocks, w_ids, p_ids, tm):
    T = x.shape[0]

    def body(w_ids_ref, p_ids_ref, x_ref, g_ref, w_ref, proj_ref, h_ref):
        del w_ids_ref, p_ids_ref

        @pl.when(pl.program_id(1) == 0)
        def _():
            xf = x_ref[...]
            h_ref[...] = (xf * _rms(xf) * g_ref[...]).astype(BF16)
        proj_ref[...] = _mm(h_ref[...], w_ref[...]).astype(BF16)

    return pl.pallas_call(
        body, name="fwd_proj",
        grid_spec=pltpu.PrefetchScalarGridSpec(
            num_scalar_prefetch=2, grid=(T // tm, w_ids.shape[0]),
            in_specs=[pl.BlockSpec((tm, D), lambda i, j, w, p: (i, 0)), pl.BlockSpec((1, D), lambda i, j, w, p: (0, 0)),
                      pl.BlockSpec((None, D, D), lambda i, j, w, p: (w[j], 0, 0))],
            out_specs=[pl.BlockSpec((None, tm, D), lambda i, j, w, p: (p[j], i, 0)),
                       pl.BlockSpec((tm, D), lambda i, j, w, p: (i, 0))]),
        out_shape=[SDS((NSPLIT, T, D), BF16), SDS((T, D), BF16)],
        compiler_params=_params(2))(w_ids, p_ids, x, gain, w_blocks)


def _fwd_proj_more(h, w_blocks, proj, w_ids, p_ids, tm, name):
    T = h.shape[0]

    def body(w_ids_ref, p_ids_ref, h_ref, w_ref, proj_hbm, proj_ref):
        del w_ids_ref, p_ids_ref, proj_hbm
        proj_ref[...] = _mm(h_ref[...], w_ref[...]).astype(BF16)

    return pl.pallas_call(
        body, name=name,
        grid_spec=pltpu.PrefetchScalarGridSpec(
            num_scalar_prefetch=2, grid=(T // tm, w_ids.shape[0]),
            in_specs=[pl.BlockSpec((tm, D), lambda i, j, w, p: (i, 0)),
                      pl.BlockSpec((None, D, D), lambda i, j, w, p: (w[j], 0, 0)), HBM],
            out_specs=pl.BlockSpec((None, tm, D), lambda i, j, w, p: (p[j], i, 0))),
        out_shape=SDS(proj.shape, BF16), input_output_aliases={4: 0},
        compiler_params=_params(2))(w_ids, p_ids, h, w_blocks, proj)


def _halo_before(split, tm):
    return pl.BlockSpec((None, HALO, D), lambda i: (split, jnp.maximum(i * (tm // HALO) - 1, 0), 0))


def _fwd_mix(proj, cw, w_co, w_pool, mem, gain_mem, w_kv, tm):
    T = proj.shape[1]
    M = mem.shape[0]

    def body(b_ref, c_ref, ua_ref, up_ref, ch_ref, uah_ref, uph_ref, cw_ref, wco_ref, wp_ref,
             mem_ref, gm_ref, wkv_ref, za_ref, conv_ref, pooled_ref, ya_ref, yp_ref, kv_ref, memn_ref):
        i = pl.program_id(0)

        @pl.when(i == 0)
        def _():
            m = mem_ref[...]
            memn = (m * _rms(m) * gm_ref[...]).astype(BF16)
            memn_ref[...] = memn
            for j in range(2 * NH):
                kv_ref[j] = _mm(memn, wkv_ref[j]).astype(BF16)
        keep = jnp.where(i > 0, 1.0, 0.0).astype(F32)
        cu = c_ref[...].astype(F32) * ua_ref[...].astype(F32)
        cu_h = ch_ref[...].astype(F32) * uah_ref[...].astype(F32) * keep
        ext = jnp.concatenate([cu_h, cu], axis=0)
        conv = (_tap(cw_ref, 2) * ext + _tap(cw_ref, 1) * _shift_down(ext, 1) + _tap(cw_ref, 0) * _shift_down(ext, 2))[HALO:]
        za = (b_ref[...].astype(F32) * conv).astype(BF16)
        conv_ref[...] = conv.astype(BF16)
        za_ref[...] = za
        ya_ref[...] = _mm(za, wco_ref[...]).astype(BF16)

        up = up_ref[...].astype(F32)
        ext_u = jnp.concatenate([uph_ref[...].astype(F32) * keep, up], axis=0)
        pos = i * tm + lax.broadcasted_iota(jnp.int32, (tm, HD), 0)
        for g in range(NPOOL):
            cols = slice(g * HD, (g + 1) * HD)
            s = ext_u[:, cols]
            for k in range(g + 1):
                s = s + _shift_down(s, 1 << k)
            cnt = jnp.minimum(pos + 1, 2 << g).astype(F32)
            pooled = (s[HALO:] / cnt - up[:, cols]).astype(BF16)
            pooled_ref[:, cols] = pooled
            yp_ref[:, cols] = _mm(pooled, _pool_map(wp_ref, g)).astype(BF16)

    tile = lambda s: pl.BlockSpec((None, tm, D), lambda i: (s, i, 0))
    row = pl.BlockSpec((1, D), lambda i: (0, 0))
    out = pl.BlockSpec((tm, D), lambda i: (i, 0))
    return pl.pallas_call(
        body, name="fwd_mix", grid=(T // tm,),
        in_specs=[tile(0), tile(1), tile(2), tile(3), _halo_before(1, tm), _halo_before(2, tm), _halo_before(3, tm),
                  _whole(cw.shape), _whole((D, D)), _whole(w_pool.shape), _whole((M, D)), row, _whole((2 * NH, D, HD))],
        out_specs=[out] * 5 + [pl.BlockSpec((2 * NH, M, HD), lambda i: (0, 0, 0)), pl.BlockSpec((M, D), lambda i: (0, 0))],
        out_shape=[SDS((T, D), BF16)] * 5 + [SDS((2 * NH, M, HD), BF16), SDS((M, D), BF16)],
        compiler_params=_params(1))(proj, proj, proj, proj, proj, proj, proj, cw, w_co, w_pool, mem, gain_mem, w_kv)


def _softmax_rows(s):
    e = jnp.exp(s - jnp.max(s, axis=-1, keepdims=True))
    return e / jnp.sum(e, axis=-1, keepdims=True)


def _fwd_merge(proj, ya, yp, x, kv, w_xo, w_o, pscale, gain_ffn, tm):
    T = x.shape[0]

    def body(q_ref, ga_ref, gp_ref, gx_ref, ya_ref, yp_ref, x_ref, kv_ref, wxo_ref, wo_ref, ps_ref, gf_ref,
             o_ref, yx_ref, merged_ref, x1_ref, h2_ref):
        for h in range(NH):
            cols = slice(h * HD, (h + 1) * HD)
            p = _softmax_rows(_mm_nt(q_ref[:, cols], kv_ref[h]) * ATT_SCALE)
            o_ref[:, cols] = _mm(p.astype(BF16), kv_ref[NH + h]).astype(BF16)
        yx = _mm(o_ref[...], wxo_ref[...])
        yx_ref[...] = yx.astype(BF16)
        merged = (_sigmoid(ga_ref[...].astype(F32)) * ya_ref[...].astype(F32)
                  + _sigmoid(gp_ref[...].astype(F32)) * (yp_ref[...].astype(F32) * ps_ref[...])
                  + _sigmoid(gx_ref[...].astype(F32)) * yx).astype(BF16)
        merged_ref[...] = merged
        x1 = x_ref[...] + _mm(merged, wo_ref[...])
        x1_ref[...] = x1
        h2_ref[...] = (x1 * _rms(x1) * gf_ref[...]).astype(BF16)

    tile = lambda s: pl.BlockSpec((None, tm, D), lambda i: (s, i, 0))
    row = pl.BlockSpec((1, D), lambda i: (0, 0))
    act = pl.BlockSpec((tm, D), lambda i: (i, 0))
    full = _whole((D, D))
    return pl.pallas_call(
        body, name="fwd_merge", grid=(T // tm,),
        in_specs=[tile(4), tile(5), tile(6), tile(7), act, act, act,
                  _whole((2 * NH, kv.shape[1], HD)), full, full, row, row],
        out_specs=[act] * 5,
        out_shape=[SDS((T, D), BF16), SDS((T, D), BF16), SDS((T, D), BF16), SDS((T, D), F32), SDS((T, D), BF16)],
        compiler_params=_params(1))(proj, proj, proj, proj, ya, yp, x, kv, w_xo, w_o, pscale, gain_ffn)


def _fwd_ffn_up(h2, wg_t, wu_t, tm, tn):
    T = h2.shape[0]

    def body(h_ref, wg_ref, wu_ref, gate_ref, up_ref, act_ref):
        for cols in _col_chunks(tn):
            gate = _mm_nt(h_ref[...], wg_ref[cols, :])
            up = _mm_nt(h_ref[...], wu_ref[cols, :])
            gate_ref[:, cols] = gate.astype(BF16)
            up_ref[:, cols] = up.astype(BF16)
            act_ref[:, cols] = (gate * _sigmoid(gate) * up).astype(BF16)

    w = pl.BlockSpec((tn, D), lambda n, i: (n, 0))
    o = pl.BlockSpec((tm, tn), lambda n, i: (i, n))
    return pl.pallas_call(
        body, name="fwd_ffn_up", grid=(DFF // tn, T // tm),
        in_specs=[pl.BlockSpec((tm, D), lambda n, i: (i, 0)), w, w],
        out_specs=[o] * 3, out_shape=[SDS((T, DFF), BF16)] * 3,
        compiler_params=_params(2))(h2, wg_t, wu_t)


def _fwd_ffn_down_loss(act, w_d, x1, target, gain_final, tm):
    T = x1.shape[0]

    def body(act_ref, wd_ref, x1_ref, tgt_ref, g_ref, dx2_ref, loss_ref, dgain_ref):
        @pl.when(pl.program_id(0) == 0)
        def _():
            loss_ref[...] = jnp.zeros_like(loss_ref)
            dgain_ref[...] = jnp.zeros_like(dgain_ref)
        x2 = x1_ref[...] + _mm(act_ref[...], wd_ref[...])
        gain = g_ref[...]
        y = x2 * _rms(x2) * gain
        err = y - tgt_ref[...]
        loss_ref[...] += 0.5 * jnp.sum(jnp.mean(err * err, axis=-1, keepdims=True))
        dx2, dgain = _norm_bwd(err * (1.0 / D), x2, gain)
        dx2_ref[...] = dx2
        dgain_ref[...] += dgain

    act_spec = pl.BlockSpec((tm, D), lambda i: (i, 0))
    row = pl.BlockSpec((1, D), lambda i: (0, 0))
    return pl.pallas_call(
        body, name="fwd_ffn_down_loss", grid=(T // tm,),
        in_specs=[pl.BlockSpec((tm, DFF), lambda i: (i, 0)), _whole((DFF, D)), act_spec, act_spec, row],
        out_specs=[act_spec, pl.BlockSpec((8, D), lambda i: (0, 0)), row],
        out_shape=[SDS((T, D), F32), SDS((8, D), F32), SDS((1, D), F32)],
        compiler_params=_params(1))(act, w_d, x1, target, gain_final)


def _bwd_ffn_down(dx2, w_d, gate, up, tm, tn):
    T = dx2.shape[0]

    def body(dx_ref, wd_ref, gate_ref, up_ref, dgate_ref, dup_ref):
        dx = dx_ref[...].astype(BF16)
        for cols in _col_chunks(tn):
            dact = _mm_nt(dx, wd_ref[cols, :])
            gate = gate_ref[:, cols].astype(F32)
            sg = _sigmoid(gate)
            dgate_ref[:, cols] = (dact * up_ref[:, cols].astype(F32) * (sg * (1.0 + gate * (1.0 - sg)))).astype(BF16)
            dup_ref[:, cols] = (dact * gate * sg).astype(BF16)

    o = pl.BlockSpec((tm, tn), lambda n, i: (i, n))
    return pl.pallas_call(
        body, name="bwd_ffn_down", grid=(DFF // tn, T // tm),
        in_specs=[pl.BlockSpec((tm, D), lambda n, i: (i, 0)), pl.BlockSpec((tn, D), lambda n, i: (n, 0)), o, o],
        out_specs=[o] * 2, out_shape=[SDS((T, DFF), BF16)] * 2,
        compiler_params=_params(2))(dx2, w_d, gate, up)


def _bwd_ffn_up(dgate, dup, wg_t, wu_t, x1, dx2, gain_ffn, tm):
    T = x1.shape[0]

    def body(dg_ref, du_ref, wg_ref, wu_ref, x1_ref, dx2_ref, g_ref, dx1_ref, dgain_ref):
        @pl.when(pl.program_id(0) == 0)
        def _():
            dgain_ref[...] = jnp.zeros_like(dgain_ref)
        dh2 = _mm(dg_ref[...], wg_ref[...]) + _mm(du_ref[...], wu_ref[...])
        dx, dgain = _norm_bwd(dh2, x1_ref[...], g_ref[...])
        dx1_ref[...] = dx2_ref[...] + dx
        dgain_ref[...] += dgain

    wide = pl.BlockSpec((tm, DFF), lambda i: (i, 0))
    w = _whole((DFF, D))
    act = pl.BlockSpec((tm, D), lambda i: (i, 0))
    row = pl.BlockSpec((1, D), lambda i: (0, 0))
    return pl.pallas_call(
        body, name="bwd_ffn_up", grid=(T // tm,),
        in_specs=[wide, wide, w, w, act, act, row], out_specs=[act, row],
        out_shape=[SDS((T, D), F32), SDS((1, D), F32)],
        compiler_params=_params(1))(dgate, dup, wg_t, wu_t, x1, dx2, gain_ffn)


def _wgrad(a, b, *, name, groups, a_cols, b_cols, tt, a_index, b_index, o_index, out_shape, after):
    T = a.shape[0]
    nt = T // tt
    n_a = a.shape[1] // a_cols if groups == 1 else 1

    def body(a_ref, b_ref, after_ref, o_ref, acc_ref):
        del after_ref
        t = pl.program_id(2)

        @pl.when(t == 0)
        def _():
            acc_ref[...] = jnp.zeros_like(acc_ref)
        acc_ref[...] += _mm_tn(a_ref[...].astype(BF16), b_ref[...].astype(BF16))

        @pl.when(t == nt - 1)
        def _():
            o_ref[...] = acc_ref[...].astype(o_ref.dtype)

    return pl.pallas_call(
        body, name=name, grid=(groups, n_a, nt),
        in_specs=[pl.BlockSpec((tt, a_cols), a_index), pl.BlockSpec((None, tt, b_cols), b_index), HBM],
        out_specs=pl.BlockSpec((None, a_cols, b_cols), o_index),
        out_shape=SDS(out_shape, BF16),
        scratch_shapes=[pltpu.VMEM((a_cols, b_cols), F32)],
        compiler_params=_params(3))(a, b, after)


def _wgrad_dense(a, b, name, tt, after, a_cols=None):
    ka, nb = a.shape[1], b.shape[1]
    a_cols = ka if a_cols is None else a_cols
    out = _wgrad(a, b[None], name=name, groups=1, a_cols=a_cols, b_cols=nb, tt=tt,
                 a_index=lambda g, k, t: (t, k), b_index=lambda g, k, t: (0, t, 0),
                 o_index=lambda g, k, t: (k, 0, 0), out_shape=(ka // a_cols, a_cols, nb), after=after)
    return out.reshape(ka, nb)


def _bwd_merge(dx1, proj, ya, yp, yx, pooled, pscale, w_o, w_co, w_xo, w_pool, tm, after):
    T = dx1.shape[0]
    nt = T // tm

    def body(dx1_ref, ga_ref, gp_ref, gx_ref, ya_ref, yp_ref, yx_ref, pooled_ref, ps_ref, wo_ref, wco_ref, wxo_ref, wp_ref,
             after_ref, dgates_ref, dya_ref, dyx_ref, dza_ref, do_ref, dpooled_ref, dps_ref, dwp_ref, acc_ref):
        del after_ref

        @pl.when(pl.program_id(0) == 0)
        def _():
            dps_ref[...] = jnp.zeros_like(dps_ref)
            acc_ref[...] = jnp.zeros_like(acc_ref)
        dmerged = _mm_nt(dx1_ref[...].astype(BF16), wo_ref[...])
        scale = ps_ref[...]
        sa, sp, sx = (_sigmoid(r[...].astype(F32)) for r in (ga_ref, gp_ref, gx_ref))
        ya, yp_pre, yx = (r[...].astype(F32) for r in (ya_ref, yp_ref, yx_ref))
        dgates_ref[0] = (dmerged * ya * sa * (1.0 - sa)).astype(BF16)
        dgates_ref[1] = (dmerged * (yp_pre * scale) * sp * (1.0 - sp)).astype(BF16)
        dgates_ref[2] = (dmerged * yx * sx * (1.0 - sx)).astype(BF16)
        dya = (dmerged * sa).astype(BF16)
        dyx = (dmerged * sx).astype(BF16)
        dyp = dmerged * sp
        dyps = (dyp * scale).astype(BF16)
        dps_ref[...] += jnp.sum(dyp * yp_pre, axis=0, keepdims=True)
        dya_ref[...] = dya
        dyx_ref[...] = dyx
        dza_ref[...] = _mm_nt(dya, wco_ref[...]).astype(BF16)
        do_ref[...] = _mm_nt(dyx, wxo_ref[...]).astype(BF16)
        for g in range(NPOOL):
            cols = slice(g * HD, (g + 1) * HD)
            dpooled_ref[:, cols] = _mm_nt(dyps[:, cols], _pool_map(wp_ref, g)).astype(BF16)
            acc_ref[g] += _mm_tn(pooled_ref[:, cols], dyps[:, cols])

        @pl.when(pl.program_id(0) == nt - 1)
        def _():
            dwp_ref[...] = acc_ref[...].astype(BF16)

    tile = lambda s: pl.BlockSpec((None, tm, D), lambda i: (s, i, 0))
    row = pl.BlockSpec((1, D), lambda i: (0, 0))
    act = pl.BlockSpec((tm, D), lambda i: (i, 0))
    full = _whole((D, D))
    return pl.pallas_call(
        body, name="bwd_merge", grid=(T // tm,),
        in_specs=[act, tile(5), tile(6), tile(7), act, act, act, act, row, full, full, full,
                  _whole(w_pool.shape), HBM],
        out_specs=[pl.BlockSpec((3, tm, D), lambda i: (0, i, 0))] + [act] * 5
        + [row, pl.BlockSpec((NPOOL, HD, HD), lambda i: (0, 0, 0))],
        out_shape=[SDS((NSPLIT, T, D), BF16)] + [SDS((T, D), BF16)] * 5 + [SDS((1, D), F32), SDS((NPOOL, HD, HD), BF16)],
        scratch_shapes=[pltpu.VMEM((NPOOL, HD, HD), F32)],
        compiler_params=_params(1))(dx1, proj, proj, proj, ya, yp, yx, pooled, pscale, w_o, w_co, w_xo, w_pool, after)


def _bwd_attn(dproj, proj, do, kv, memn, w_kv, mem, gain_mem, tm):
    T = do.shape[0]
    M = kv.shape[1]
    nt = T // tm

    def body(dproj_hbm, q_ref, do_ref, kv_ref, memn_ref, wkv_ref, mem_ref, gm_ref, dq_ref, dw_ref, dgain_ref, dkv_ref):
        del dproj_hbm

        @pl.when(pl.program_id(0) == 0)
        def _():
            dkv_ref[...] = jnp.zeros_like(dkv_ref)
        for h in range(NH):
            cols = slice(h * HD, (h + 1) * HD)
            q = q_ref[:, cols]
            do_h = do_ref[:, cols]
            p = _softmax_rows(_mm_nt(q, kv_ref[h]) * ATT_SCALE)
            dp = _mm_nt(do_h, kv_ref[NH + h])
            ds = (p * (dp - jnp.sum(dp * p, axis=-1, keepdims=True)) * ATT_SCALE).astype(BF16)
            dq_ref[:, cols] = _mm(ds, kv_ref[h]).astype(BF16)
            dkv_ref[h] += _mm_tn(ds, q)
            dkv_ref[NH + h] += _mm_tn(p.astype(BF16), do_h)

        @pl.when(pl.program_id(0) == nt - 1)
        def _():
            dmemn = jnp.zeros((M, D), F32)
            for j in range(2 * NH):
                dkv_j = dkv_ref[j].astype(BF16)
                dw_ref[j] = _mm_tn(memn_ref[...], dkv_j).astype(BF16)
                dmemn = dmemn + _mm_nt(dkv_j, wkv_ref[j])
            dgain_ref[...] = _norm_bwd(dmemn, mem_ref[...], gm_ref[...])[1]

    row = pl.BlockSpec((1, D), lambda i: (0, 0))
    return pl.pallas_call(
        body, name="bwd_attn", grid=(nt,),
        in_specs=[HBM, pl.BlockSpec((None, tm, D), lambda i: (4, i, 0)), pl.BlockSpec((tm, D), lambda i: (i, 0)),
                  _whole((2 * NH, M, HD)), _whole((M, D)), _whole((2 * NH, D, HD)), _whole((M, D)), row],
        out_specs=[pl.BlockSpec((None, tm, D), lambda i: (3, i, 0)),
                   pl.BlockSpec((2 * NH, D, HD), lambda i: (0, 0, 0)), row],
        out_shape=[SDS(dproj.shape, BF16), SDS((2 * NH, D, HD), BF16), SDS((1, D), F32)],
        scratch_shapes=[pltpu.VMEM((2 * NH, M, HD), F32)],
        input_output_aliases={0: 0},
        compiler_params=_params(1))(dproj, proj, do, kv, memn, w_kv, mem, gain_mem)


def _bwd_mix(dproj, proj, conv, dza, dpooled, cw, tm, after):
    T = dza.shape[0]
    nt = T // tm

    def halo_after(split_or_none):
        idx = lambda i: jnp.minimum((i + 1) * (tm // HALO), T // HALO - 1)
        if split_or_none is None:
            return pl.BlockSpec((HALO, D), lambda i: (idx(i), 0))
        return pl.BlockSpec((None, HALO, D), lambda i: (split_or_none, idx(i), 0))

    def body(dproj_hbm, b_ref, c_ref, ua_ref, conv_ref, dza_ref, dpo_ref, bn_ref, dzan_ref, dpon_ref,
             cw_ref, after_ref, dabcu_ref, dcw_ref):
        del dproj_hbm, after_ref
        i = pl.program_id(0)

        @pl.when(i == 0)
        def _():
            dcw_ref[...] = jnp.zeros_like(dcw_ref)
        keep_next = jnp.where(i < nt - 1, 1.0, 0.0).astype(F32)
        dza = dza_ref[...].astype(F32)
        c = c_ref[...].astype(F32)
        ua = ua_ref[...].astype(F32)
        dconv = dza * b_ref[...].astype(F32)
        dconv_n = dzan_ref[...].astype(F32) * bn_ref[...].astype(F32) * keep_next
        ext = jnp.concatenate([dconv, dconv_n], axis=0)
        dconv_1, dconv_2 = _shift_up(ext, 1)[:tm], _shift_up(ext, 2)[:tm]
        dcu = _tap(cw_ref, 2) * dconv + _tap(cw_ref, 1) * dconv_1 + _tap(cw_ref, 0) * dconv_2
        dabcu_ref[0] = (dza * conv_ref[...].astype(F32)).astype(BF16)
        dabcu_ref[1] = (dcu * ua).astype(BF16)
        dabcu_ref[2] = (dcu * c).astype(BF16)

        cu = c * ua
        dcw_ref[2:3, :] += jnp.sum(dconv * cu, axis=0, keepdims=True)
        dcw_ref[1:2, :] += jnp.sum(dconv_1 * cu, axis=0, keepdims=True)
        dcw_ref[0:1, :] += jnp.sum(dconv_2 * cu, axis=0, keepdims=True)

        dpo = dpo_ref[...].astype(F32)
        ext_dpo = jnp.concatenate([dpo, dpon_ref[...].astype(F32) * keep_next], axis=0)
        pos = i * tm + lax.broadcasted_iota(jnp.int32, (tm + HALO, HD), 0)
        for g in range(NPOOL):
            cols = slice(g * HD, (g + 1) * HD)
            s = ext_dpo[:, cols] / jnp.minimum(pos + 1, 2 << g).astype(F32)
            for k in range(g + 1):
                s = s + _shift_up(s, 1 << k)
            dabcu_ref[3, :, cols] = (s[:tm] - dpo[:, cols]).astype(BF16)

    tile = lambda s: pl.BlockSpec((None, tm, D), lambda i: (s, i, 0))
    act = pl.BlockSpec((tm, D), lambda i: (i, 0))
    return pl.pallas_call(
        body, name="bwd_mix", grid=(nt,),
        in_specs=[HBM, tile(0), tile(1), tile(2), act, act, act, halo_after(0), halo_after(None), halo_after(None),
                  _whole(cw.shape), HBM],
        out_specs=[pl.BlockSpec((4, tm, D), lambda i: (1, i, 0)), pl.BlockSpec((8, D), lambda i: (0, 0))],
        out_shape=[SDS(dproj.shape, BF16), SDS((8, D), F32)],
        input_output_aliases={0: 0},
        compiler_params=_params(1))(dproj, proj, proj, proj, conv, dza, dpooled, proj, dza, dpooled, cw, after)


def _bwd_proj(dproj, w_parts, places, x, dx1, gain, tm, after):
    T = x.shape[0]

    def body(places_ref, dp_ref, w0_ref, w1_ref, w2_ref, x_ref, dx1_ref, g_ref, after_ref, dx_ref, dgain_ref, acc_ref):
        del after_ref
        i, s = pl.program_id(0), pl.program_id(1)

        @pl.when((i == 0) & (s == 0))
        def _():
            dgain_ref[...] = jnp.zeros_like(dgain_ref)

        @pl.when(s == 0)
        def _():
            acc_ref[...] = jnp.zeros_like(acc_ref)

        for k, w_ref in enumerate((w0_ref, w1_ref, w2_ref)):
            @pl.when(places_ref[s] == k)
            def _(w_ref=w_ref):
                acc_ref[...] += _mm_nt(dp_ref[...], w_ref[...])

        @pl.when(s == NSPLIT - 1)
        def _():
            dx, dgain = _norm_bwd(acc_ref[...], x_ref[...], g_ref[...])
            dx_ref[...] = dx1_ref[...] + dx
            dgain_ref[...] += dgain

    act = pl.BlockSpec((tm, D), lambda i, s, p: (i, 0))
    row = pl.BlockSpec((1, D), lambda i, s, p: (0, 0))
    weight = lambda k: pl.BlockSpec((None, D, D), lambda i, s, p: (p[NSPLIT * (1 + k) + s], 0, 0))
    return pl.pallas_call(
        body, name="bwd_proj",
        grid_spec=pltpu.PrefetchScalarGridSpec(
            num_scalar_prefetch=1, grid=(T // tm, NSPLIT),
            in_specs=[pl.BlockSpec((None, tm, D), lambda i, s, p: (s, i, 0)), weight(0), weight(1), weight(2),
                      act, act, row, HBM],
            out_specs=[act, row], scratch_shapes=[pltpu.VMEM((tm, D), F32)]),
        out_shape=[SDS((T, D), F32), SDS((1, D), F32)],
        compiler_params=_params(2))(places, dproj, *w_parts, x, dx1, gain, after)


def _adamw_math(w, g, m, v):
    m = ADAM_B1 * m + (1.0 - ADAM_B1) * g
    v = ADAM_B2 * v + (1.0 - ADAM_B2) * (g * g)
    m_hat = m / (1.0 - ADAM_B1 ** ADAM_STEP)
    v_hat = v / (1.0 - ADAM_B2 ** ADAM_STEP)
    delta = -ADAM_LR * (m_hat / (jnp.sqrt(v_hat) + ADAM_EPS) + ADAM_WD * w)
    return delta, m, v


def _adamw_small(parts, me, gains, taps):
    n = len(gains)
    cols = D // NDEV

    def body(me_ref, all_ref, mine_ref, *refs):
        del me_ref
        everywhere, here = all_ref[0], mine_ref[0]
        for k in range(1, NDEV):
            everywhere, here = everywhere + all_ref[k], here + mine_ref[k]
        ins, outs = refs[:3 * (n + 1)], refs[3 * (n + 1):]
        for a in range(n):
            w_ref, m_ref, v_ref = ins[3 * a:3 * a + 3]
            go_ref, d_ref, mo_ref, vo_ref = outs[4 * a:4 * a + 4]
            g = everywhere[a:a + 1]
            go_ref[...] = g
            d_ref[...], mo_ref[...], vo_ref[...] = _adamw_math(w_ref[...], g, m_ref[...], v_ref[...])
        (w_ref, m_ref, v_ref), (go_ref, d_ref, mo_ref, vo_ref) = ins[3 * n:], outs[4 * n:4 * n + 4]
        for t in range(3):
            g = here[n + t:n + t + 1]
            go_ref[t] = g
            d_ref[t], mo_ref[t], vo_ref[t] = _adamw_math(w_ref[t], g, m_ref[t], v_ref[t])
        outs[-1][...] = everywhere[8:9, 0:1]

    whole = lambda shape: pl.BlockSpec(shape, lambda i, me_ref: (0,) * len(shape))
    shapes = [(1, D)] * n + [(3, 1, cols)]
    outs = pl.pallas_call(
        body, name="adamw_replicated",
        grid_spec=pltpu.PrefetchScalarGridSpec(
            num_scalar_prefetch=1, grid=(1,),
            in_specs=[whole(parts.shape), pl.BlockSpec((NDEV, 16, cols), lambda i, me_ref: (0, 0, me_ref[0]))]
            + [whole(s) for s in shapes for _ in range(3)],
            out_specs=[whole(s) for s in shapes for _ in range(4)] + [whole((1, 1))]),
        out_shape=[SDS(s, F32) for s in shapes for _ in range(4)] + [SDS((1, 1), F32)],
        compiler_params=_params(1))(me.reshape(1).astype(jnp.int32), parts, parts, *[a for three in gains + [taps] for a in three])
    return [outs[4 * a:4 * a + 4] for a in range(n + 1)], outs[-1]


def _adamw(ws, gs, ms, vs, name, from_parts, steps):
    n = len(ws)

    def body(*refs):
        for a in range(n):
            w_ref, g_ref, m_ref, v_ref = refs[4 * a:4 * a + 4]
            go_ref, d_ref, mo_ref, vo_ref = refs[4 * n + 4 * a:4 * n + 4 * a + 4]
            if from_parts:
                g = g_ref[0].astype(F32)
                for k in range(1, g_ref.shape[0]):
                    g = g + g_ref[k].astype(F32)
            else:
                g = g_ref[...]
            go_ref[...] = g
            d_ref[...], mo_ref[...], vo_ref[...] = _adamw_math(w_ref[...], g, m_ref[...], v_ref[...])

    in_specs, out_specs, out_shape, operands = [], [], [], []
    for w, g, m, v in zip(ws, gs, ms, vs):
        rows, cols = w.shape
        blk = pl.BlockSpec((rows // steps, cols), lambda i: (i, 0))
        g_spec = pl.BlockSpec((g.shape[0], rows // steps, cols), lambda i: (0, i, 0)) if from_parts else blk
        in_specs += [blk, g_spec, blk, blk]
        out_specs += [blk] * 4
        out_shape += [SDS((rows, cols), F32)] * 4
        operands += [w, g, m, v]
    outs = pl.pallas_call(body, name=name, grid=(steps,), in_specs=in_specs, out_specs=out_specs, out_shape=out_shape,
                          compiler_params=_params(1))(*operands)
    return [outs[4 * a:4 * a + 4] for a in range(n)]


def _peer(k, x, y, c):
    return ((1 - x) if k & 4 else x, (1 - y) if k & 2 else y, (1 - c) if k & 1 else c)


SEM = pl.BlockSpec(memory_space=pltpu.SEMAPHORE)
IN_HBM = pl.BlockSpec(memory_space=pltpu.HBM)
DATAFLOW = pltpu.SideEffectType.DATAFLOW_SIDE_EFFECTING
TOKEN_SHAPE = (8, 128)


OTHER_CHIPS = (2, 4, 6)


def _place(x, y, c):
    return 4 * x + 2 * y + c


def _plan_gather_chips(n, ks=(1,) + OTHER_CHIPS):
    def plan(refs, x, y, c, arriving):
        out = []
        for a in range(n):
            for k in ks:
                there = _place(*_peer(k, x, y, c))
                out.append((refs[a].at[_place(x, y, c)], refs[a].at[there if arriving else _place(x, y, c)], k))
        return out
    return plan, n * len(ks)


def _plan_gather_sibling(n, ks=OTHER_CHIPS):
    def plan(refs, x, y, c, arriving):
        out = []
        for a in range(n):
            for k in ks:
                px, py, pc = _peer(k, x, y, c)
                mine, theirs = _place(px, py, pc), _place(px, py, 1 - pc)
                out.append((refs[a].at[mine], refs[a].at[theirs if arriving else mine], 1))
        return out
    return plan, n * len(ks)


def _plan_pair():
    def plan(refs, x, y, c, arriving):
        return [(refs[0].at[c], refs[0].at[(1 - c) if arriving else c], 1)]
    return plan, 1


def _plan_far_chip():
    def plan(refs, x, y, c, arriving):
        return [(refs[0].at[c], refs[1].at[c], 6)]
    return plan, 1


def _plan_far_sibling():
    def plan(refs, x, y, c, arriving):
        return [(refs[0].at[c], refs[0].at[(1 - c) if arriving else c], 1)]
    return plan, 1


def _plan_scatter_sibling(n):
    def plan(refs, x, y, c, arriving):
        out = []
        for a in range(n):
            for q in range(4):
                out.append((refs[a].at[2 * q + (1 - c)], refs[n + a].at[q], 1))
        return out
    return plan, n * 4


def _plan_scatter_chips(n):
    def plan(refs, x, y, c, arriving):
        out = []
        for a in range(n):
            for k in OTHER_CHIPS:
                px, py, _ = _peer(k, x, y, c)
                out.append((refs[a].at[2 * px + py], refs[n + a].at[(2 * px + py) if arriving else (2 * x + y)], k))
        return out
    return plan, n * 3


def _remote(src, dst, send_sems, recv_sems, i, k):
    x, y, c = (lax.axis_index(n) for n in AXES)
    return pltpu.make_async_remote_copy(src_ref=src, dst_ref=dst, send_sem=send_sems.at[i], recv_sem=recv_sems.at[i],
                                        device_id=_peer(k, x, y, c), device_id_type=pl.DeviceIdType.MESH)


def _copies_start(groups, name, after):
    ng = len(groups)
    total = sum(len(bufs) for bufs, _ in groups)

    def body(*refs):
        sems = refs[1 + total:1 + total + 2 * ng]
        x, y, c = (lax.axis_index(n) for n in AXES)
        off = 1
        for gi, (bufs, (plan, _)) in enumerate(groups):
            for i, (src, dst, k) in enumerate(plan(refs[off:off + len(bufs)], x, y, c, False)):
                _remote(src, dst, sems[2 * gi], sems[2 * gi + 1], i, k).start()
            off += len(bufs)
        refs[-1][...] = jnp.zeros(TOKEN_SHAPE, F32)

    sem_shapes = [pltpu.SemaphoreType.DMA((count,)) for _, (_, count) in groups for _ in range(2)]
    flat = [b for bufs, _ in groups for b in bufs]
    outs = pl.pallas_call(
        body, name=name,
        in_specs=[HBM] + [IN_HBM] * total,
        out_specs=[SEM] * (2 * ng) + [IN_HBM] * total + [pl.BlockSpec(memory_space=pltpu.VMEM)],
        out_shape=sem_shapes + [pltpu.HBM(b.shape, b.dtype) for b in flat] + [SDS(TOKEN_SHAPE, F32)],
        input_output_aliases={1 + i: 2 * ng + i for i in range(total)},
        compiler_params=pltpu.CompilerParams(has_side_effects=DATAFLOW),
    )(after, *[pltpu.with_memory_space_constraint(b, pltpu.HBM) for b in flat])
    handles, off = [], 2 * ng
    for gi, (bufs, _) in enumerate(groups):
        handles.append((outs[2 * gi], outs[2 * gi + 1], list(outs[off:off + len(bufs)])))
        off += len(bufs)
    return handles, outs[-1]


def _copies_wait_start(handle, plan, pass_on, more, name, after):
    send_sems, recv_sems, bufs = handle
    n = len(bufs)
    idx, (pass_plan, pass_count) = pass_on
    total = sum(len(b) for b, _ in more)
    ng = 1 + len(more)

    def body(*refs):
        x, y, c = (lax.axis_index(a) for a in AXES)
        waited = refs[1:1 + n]
        outs = refs[3 + n + total:]
        new_sems = outs[n + total:n + total + 2 * ng]
        for i, (src, dst, k) in enumerate(plan[0](waited, x, y, c, True)):
            copy = _remote(src, dst, refs[1 + n + total], refs[2 + n + total], i, k)
            copy.wait_send()
            copy.wait_recv()
        for i, (src, dst, k) in enumerate(pass_plan([waited[j] for j in idx], x, y, c, False)):
            _remote(src, dst, new_sems[0], new_sems[1], i, k).start()
        off = 1 + n
        for gi, (b, (p, _)) in enumerate(more):
            for i, (src, dst, k) in enumerate(p(refs[off:off + len(b)], x, y, c, False)):
                _remote(src, dst, new_sems[2 + 2 * gi], new_sems[3 + 2 * gi], i, k).start()
            off += len(b)
        outs[-1][...] = jnp.zeros(TOKEN_SHAPE, F32)

    flat = list(bufs) + [a for b, _ in more for a in b]
    sem_shapes = [pltpu.SemaphoreType.DMA((count,)) for count in [pass_count] + [cnt for _, (_, cnt) in more] for _ in range(2)]
    outs = pl.pallas_call(
        body, name=name,
        in_specs=[HBM] + [IN_HBM] * (n + total) + [SEM, SEM],
        out_specs=[IN_HBM] * (n + total) + [SEM] * (2 * ng) + [pl.BlockSpec(memory_space=pltpu.VMEM)],
        out_shape=[pltpu.HBM(b.shape, b.dtype) for b in flat] + sem_shapes + [SDS(TOKEN_SHAPE, F32)],
        input_output_aliases={1 + i: i for i in range(n + total)},
        compiler_params=pltpu.CompilerParams(has_side_effects=DATAFLOW),
    )(after, *[pltpu.with_memory_space_constraint(b, pltpu.HBM) for b in flat], send_sems, recv_sems)
    thru = list(outs[:n])
    sems_out = outs[n + total:n + total + 2 * ng]
    handles = [(sems_out[0], sems_out[1], [thru[j] for j in idx])]
    off = n
    for gi, (b, _) in enumerate(more):
        handles.append((sems_out[2 + 2 * gi], sems_out[3 + 2 * gi], list(outs[off:off + len(b)])))
        off += len(b)
    return thru, handles, outs[-1]


def _copies_wait(handle, plan, name, *after):
    send_sems, recv_sems, bufs = handle
    n = len(bufs)

    def body(*refs):
        x, y, c = (lax.axis_index(a) for a in AXES)
        for i, (src, dst, k) in enumerate(plan[0](refs[:n], x, y, c, True)):
            copy = _remote(src, dst, refs[n], refs[n + 1], i, k)
            copy.wait_send()
            copy.wait_recv()

    return pl.pallas_call(
        body, name=name,
        in_specs=[IN_HBM] * n + [SEM, SEM] + [HBM] * len(after), out_specs=[IN_HBM] * n,
        out_shape=[pltpu.HBM(b.shape, b.dtype) for b in bufs],
        input_output_aliases={i: i for i in range(n)},
        compiler_params=pltpu.CompilerParams(has_side_effects=DATAFLOW),
    )(*bufs, send_sems, recv_sems, *after)


def _pair_sums(mine, theirs, c, chip, name):
    n = len(mine)

    def body(where_ref, *refs):
        q = pl.program_id(0)
        for a in range(n):
            total = (refs[a][...].astype(F32) + refs[n + a][...].astype(F32)).astype(BF16)
            refs[2 * n + a][...] = total

            @pl.when(q == where_ref[1])
            def _():
                refs[3 * n + a][...] = total

    block = lambda t: (None,) + t.shape[1:]
    zeros = lambda t: (0,) * (t.ndim - 1)
    outs = pl.pallas_call(
        body, name=name,
        grid_spec=pltpu.PrefetchScalarGridSpec(
            num_scalar_prefetch=1, grid=(4,),
            in_specs=[pl.BlockSpec(block(t), lambda q, w, z=zeros(t): (2 * q + w[0],) + z) for t in theirs]
            + [pl.BlockSpec(block(t), lambda q, w, z=zeros(t): (q,) + z) for t in theirs],
            out_specs=[pl.BlockSpec(block(t), lambda q, w, z=zeros(t): (q,) + z) for t in theirs]
            + [pl.BlockSpec(block(t), lambda q, w, z=zeros(t): (w[1],) + z) for t in theirs]),
        out_shape=[SDS(t.shape, BF16) for t in theirs] * 2,
        compiler_params=_params(1))(jnp.stack([c, chip]).astype(jnp.int32), *mine, *theirs)
    return list(outs[:n]), list(outs[n:])


def _local_step(x, mem, target, gains, get, put, flush, tm_huge=2048, tm_big=1024, tm_mid=512, tm_small=256):
    g_mix, pscale, g_mem, g_ffn, g_fin = gains
    T = x.shape[0]
    tm_huge, tm_big, tm_mid, tm_small = min(tm_huge, T), min(tm_big, T), min(tm_mid, T), min(tm_small, T)
    tn = DFF // 2

    w_pair, w_ids, p_ids = get("in_pair", x)
    proj, h = _fwd_proj(x, g_mix, w_pair, w_ids, p_ids, tm_huge)
    w_near, w_ids, p_ids = get("in_near", h)
    proj = _fwd_proj_more(h, w_near, proj, w_ids, p_ids, tm_huge, "fwd_proj_near")
    w_far, w_ids, p_ids = get("in_far", proj)
    proj = _fwd_proj_more(h, w_far, proj, w_ids, p_ids, tm_huge, "fwd_proj_far")
    cw, w_co, w_pool, w_kv = get("mix", proj)
    za, conv, pooled, ya, yp, kv, memn = _fwd_mix(proj, cw, w_co, w_pool, mem, g_mem, w_kv, tm_mid)
    w_xo, w_o = get("merge", ya)
    o, yx, merged, x1, h2 = _fwd_merge(proj, ya, yp, x, kv, w_xo, w_o, pscale, g_ffn, tm_mid)
    wg_t, wu_t = get("gate_up", x1)
    get("down", x1, early=True)
    gate, up, act = _fwd_ffn_up(h2, wg_t, wu_t, tm_big, tn)
    (w_d,) = get("down", gate)
    dx2, loss, dg_fin = _fwd_ffn_down_loss(act, w_d, x1, target, g_fin, tm_mid)

    dgate, dup = _bwd_ffn_down(dx2, w_d, gate, up, tm_big, tn)
    dx1, dg_ffn = _bwd_ffn_up(dgate, dup, wg_t, wu_t, x1, dx2, g_ffn, tm_small)
    dw_d = _wgrad_dense(act, dx2, "wgrad_down", tm_big, g_mix)
    dwg_t = _wgrad_dense(dgate, h2, "wgrad_gate", tm_big, g_mix)
    dwu_t = _wgrad_dense(dup, h2, "wgrad_up", tm_big, g_mix)
    token = put("ffn", (dwg_t, dwu_t, dw_d))

    dproj, dya, dyx, dza, do, dpooled, dpscale, dw_pool = _bwd_merge(
        dx1, proj, ya, yp, yx, pooled, pscale, w_o, w_co, w_xo, w_pool, tm_mid, token)
    token = flush(dya)
    dw_o = _wgrad_dense(merged, dx1, "wgrad_out", tm_big, token)
    dw_co = _wgrad_dense(za, dya, "wgrad_conv_out", tm_big, token)
    dw_xo = _wgrad_dense(o, dyx, "wgrad_xattn_out", tm_big, token)
    dproj, dw_kv, dg_mem = _bwd_attn(dproj, proj, do, kv, memn, w_kv, mem, g_mem, tm_big)
    token = put("mix", (dw_co, dw_xo, dw_o, dw_pool, dw_kv))

    dproj, dcw = _bwd_mix(dproj, proj, conv, dza, dpooled, cw, tm_mid, token)
    token = flush(dcw)
    dw_in = _wgrad(h, dproj, name="wgrad_in", groups=NSPLIT, a_cols=D, b_cols=D, tt=tm_huge,
                   a_index=lambda g, k, t: (t, 0), b_index=lambda g, k, t: (g, t, 0),
                   o_index=lambda g, k, t: (_slot_group(g), 0, 0), out_shape=(NSPLIT, D, D), after=token)
    token = flush(put("in", (dw_in,)))
    w_pair, places = get("in_places", None)
    grad_x, dg_mix = _bwd_proj(dproj, (w_pair, w_near, w_far), places, x, dx1, g_mix, tm_big, token)

    small = jnp.concatenate([dg_mix, dpscale, dg_mem, dg_ffn, dg_fin, dcw[0:3], loss], axis=0)
    return grad_x, small


def kernel(x, mem, norm_mix, w_in, conv_w, w_conv_out, w_pool, pool_scale, norm_mem, w_kv, w_xattn_out, w_out, norm_ffn, w_gate, w_up, w_down, norm_final, loss_target, m_norm_mix, m_w_in, m_conv_w, m_w_conv_out, m_w_pool, m_pool_scale, m_norm_mem, m_w_kv, m_w_xattn_out, m_w_out, m_norm_ffn, m_w_gate, m_w_up, m_w_down, m_norm_final, v_norm_mix, v_w_in, v_conv_w, v_w_conv_out, v_w_pool, v_pool_scale, v_norm_mem, v_w_kv, v_w_xattn_out, v_w_out, v_norm_ffn, v_w_gate, v_w_up, v_w_down, v_norm_final):
    T = x.shape[1]
    rows = D // NDEV
    ffb = DFF // NDEV
    prow = HD // NDEV
    me = 4 * lax.axis_index("x") + 2 * lax.axis_index("y") + lax.axis_index("c")

    shards = [w_in[0].astype(BF16), w_conv_out[0].astype(BF16), w_xattn_out[0].astype(BF16), w_out[0].astype(BF16),
              w_pool[0].astype(BF16).reshape(NPOOL * prow, HD), w_kv[0].astype(BF16),
              w_gate[0].T.astype(BF16), w_up[0].T.astype(BF16), w_down[0].astype(BF16),
              jnp.pad(conv_w[0], ((0, 5), (0, 0)))]

    cx, cy, cc = (lax.axis_index(n) for n in AXES)
    chip = 2 * cx + cy

    def land(own, index, slots):
        return lax.dynamic_update_index_in_dim(lax.empty((slots,) + own.shape, own.dtype), own, index, 0)

    needed = ["in_pair", "in_near", "in_far", "mix", "merge", "gate_up", "down"]
    members = {"mix": [9, 1, 4, 5], "merge": [2, 3], "gate_up": [6, 7], "down": [8]}
    near = (2, 4)
    plans = {"in_pair": _plan_pair(), "in_near": _plan_gather_chips(1, near), "in_far": _plan_far_chip()}
    plans.update({n: _plan_gather_chips(len(members[n])) for n in members})
    g_bufs = {"in_pair": [land(shards[0], cc, 2)], "in_near": [land(shards[0], me, NDEV)],
              "in_far": [None, lax.empty((2, D, D), BF16)]}
    g_bufs.update({n: [land(shards[i], me, NDEV) for i in members[n]] for n in members})
    first_handles, _ = _copies_start([(g_bufs[n], plans[n]) for n in needed[:2]], "gather_start", x)
    g_handles = dict(zip(needed[:2], first_handles))
    pair_ids = jnp.array([0, 1], jnp.int32)

    on_last_leg = {}

    def get(group, after, early=False):
        if group == "in_places":
            return g_bufs["in_pair"][0], _w_in_places()[chip]
        if group == "in_pair":
            rest = [b for n in members for b in g_bufs[n]]
            g_bufs["in_far"][0], = _copies_wait(g_handles[group], plans[group], "gather_wait_" + group, after, *rest)
            return g_bufs["in_far"][0], pair_ids, (2 * chip + pair_ids).astype(jnp.int32)
        if group not in on_last_leg:
            n_bufs = len(g_bufs[group])
            if group == "in_near":
                landed, plan, more = [0], _plan_gather_sibling(1, near), [(g_bufs[n], plans[n]) for n in needed[2:]]
            elif group == "in_far":
                landed, plan, more = [1], _plan_far_sibling(), []
            else:
                landed, plan, more = list(range(n_bufs)), _plan_gather_sibling(n_bufs), []
            thru, handles, token = _copies_wait_start(g_handles[group], plans[group], (landed, plan), more,
                                                      "gather_pass_" + group, after)
            if group == "in_far":
                g_bufs["in_pair"] = thru[:1]
            g_handles.update(zip(needed[2:], handles[1:]))
            on_last_leg[group] = (handles[0], plan, token)
        if early:
            return None
        handle, plan, token = on_last_leg[group]
        got = _copies_wait(handle, plan, "gather_passed_" + group, after if group in ("gate_up", "down") else token)
        if group == "in_near":
            groups = jnp.stack([me ^ k for k in (2, 3, 4, 5)]).astype(jnp.int32)
            return got[0], groups, groups
        if group == "in_far":
            return got[0], pair_ids, (2 * (3 - chip) + pair_ids).astype(jnp.int32)
        if group == "mix":
            cw_g, w_co_g, w_pool_g, w_kv_g = got
            return cw_g, w_co_g.reshape(D, D), w_pool_g, w_kv_g
        if group == "merge":
            return got[0].reshape(D, D), got[1].reshape(D, D)
        return [g.reshape(DFF, D) for g in got]

    started = {}

    def put(group, grads):
        if group == "ffn":
            sends = [g.reshape(NDEV, ffb, D) for g in grads]
        elif group == "mix":
            dw_co, dw_xo, dw_o, dw_pool, dw_kv = grads
            sends = [dw_co.reshape(NDEV, rows, D), dw_xo.reshape(NDEV, rows, D), dw_o.reshape(NDEV, rows, D),
                     dw_pool.reshape(NPOOL, NDEV, prow, HD).transpose(1, 0, 2, 3).reshape(NDEV, NPOOL * prow, HD), dw_kv]
        else:
            sends = list(grads)
        n = len(sends)
        halves = [lax.empty((4,) + s.shape[1:], s.dtype) for s in sends]
        (handle,), token = _copies_start([(sends + halves, _plan_scatter_sibling(n))], "scatter_swap_" + group, norm_mix)
        swapping.append((group, handle, n))
        return token

    swapping = []

    def flush(after):
        group, handle, n = swapping.pop()
        bufs = _copies_wait(handle, _plan_scatter_sibling(n), "scatter_swapped_" + group, after)
        sums, lands = _pair_sums(bufs[:n], bufs[n:], cc, chip, "pair_sums_" + group)
        (handle,), token = _copies_start([(sums + lands, _plan_scatter_chips(n))], "scatter_start_" + group, norm_mix)
        started[group] = (handle, _plan_scatter_chips(n))
        return token

    def take(group, after):
        handle, plan = started[group]
        return _copies_wait(handle, plan, "scatter_wait_" + group, after)[len(handle[2]) // 2:]

    gains = (norm_mix, pool_scale, norm_mem, norm_ffn, norm_final.reshape(1, D))
    grad_x, small = _local_step(x[0], mem[0], loss_target[0], gains, get, put, flush)

    everyone = _plan_gather_chips(1, tuple(range(1, NDEV)))
    (small_handle,), token = _copies_start([([land(small, me, NDEV)], everyone)], "small_start", norm_mix)

    res = {}

    def update(group, names, ws, gs, ms, vs, from_parts, steps, transposed=()):
        view = lambda a, name: a[0].T if name in transposed else a
        flat = [[view(a, name).reshape(g.shape[-2:]) for a in (w, m, v)] for name, w, g, m, v in zip(names, ws, gs, ms, vs)]
        outs = _adamw([f[0] for f in flat], gs, [f[1] for f in flat], [f[2] for f in flat], "adamw_" + group,
                      from_parts, steps)
        for name, w, four in zip(names, ws, outs):
            res[name] = [(o.T if name in transposed else o).reshape(w.shape) for o in four]

    p_g, p_u, p_d = take("ffn", token)
    update("ffn", ["w_gate", "w_up", "w_down"], [w_gate, w_up, w_down], [p_g, p_u, p_d],
           [m_w_gate, m_w_up, m_w_down], [v_w_gate, v_w_up, v_w_down], True, 2, transposed=("w_gate", "w_up"))

    small_all, = _copies_wait(small_handle, everyone, "small_wait", res["w_down"][1])
    replicated = {"norm_mix": (norm_mix, m_norm_mix, v_norm_mix), "pool_scale": (pool_scale, m_pool_scale, v_pool_scale),
                  "norm_mem": (norm_mem, m_norm_mem, v_norm_mem), "norm_ffn": (norm_ffn, m_norm_ffn, v_norm_ffn),
                  "norm_final": (norm_final, m_norm_final, v_norm_final), "conv_w": (conv_w, m_conv_w, v_conv_w)}
    outs, loss = _adamw_small(small_all, me, [[a.reshape(1, D) for a in replicated[n]] for n in list(replicated)[:5]],
                              [a.transpose(1, 0, 2) for a in replicated["conv_w"]])
    for (name, three), four in zip(replicated.items(), outs):
        res[name] = [o.transpose(1, 0, 2) if name == "conv_w" else o.reshape(three[0].shape) for o in four]
    loss = loss.reshape(())

    p_co, p_xo, p_o, p_pool, p_kv = take("mix", res["conv_w"][1])
    update("mix", ["w_conv_out", "w_xattn_out", "w_out", "w_pool", "w_kv"], [w_conv_out, w_xattn_out, w_out, w_pool, w_kv],
           [p_co, p_xo, p_o, p_pool, p_kv], [m_w_conv_out, m_w_xattn_out, m_w_out, m_w_pool, m_w_kv],
           [v_w_conv_out, v_w_xattn_out, v_w_out, v_w_pool, v_w_kv], True, 2)
    (p_in,) = take("in", res["w_out"][1])
    update("in", ["w_in"], [w_in], [p_in], [m_w_in], [v_w_in], True, 4)
    order = ["norm_mix", "w_in", "conv_w", "w_conv_out", "w_pool", "pool_scale", "norm_mem", "w_kv", "w_xattn_out", "w_out",
             "norm_ffn", "w_gate", "w_up", "w_down", "norm_final"]
    return (loss, grad_x[None], *[res[n][0] for n in order], *[res[n][1] for n in order],
            *[res[n][2] for n in order], *[res[n][3] for n in order])
```

```python
import jax
import jax.numpy as jnp
from jax import lax
from jax.experimental import pallas as pl
from jax.experimental.pallas import tpu as pltpu

F32 = jnp.float32
BF16 = jnp.bfloat16
SDS = jax.ShapeDtypeStruct

AXES = ("x", "y", "c")
NDEV = 8
D = 1024
NSPLIT = 8
NH = 4
HD = D // NH
NPOOL = 4
DFF = 2816
EPS = 1e-6
ATT_SCALE = HD ** -0.5
HALO = 16


def _slot_group(s):
    return jnp.where(s < 3, s + 5, jnp.where(s == 3, 4, s - 4))


SLOT_GROUPS = tuple(s + 5 if s < 3 else 4 if s == 3 else s - 4 for s in range(NSPLIT))


def _w_in_places():
    table = []
    for chip in range(4):
        source = [0 if g // 2 == chip else 2 if g // 2 == 3 - chip else 1 for g in SLOT_GROUPS]
        rows = [source]
        for k in range(3):
            blocks = [g if k == 1 else g % 2 for g in SLOT_GROUPS]
            held = [b for b, src in zip(blocks, source) if src == k][-1]
            rows.append([(held := b if src == k else held) for b, src in zip(blocks, source)])
        table.append([v for row in rows for v in row])
    return jnp.array(table, jnp.int32)


ADAM_LR = 0.001
ADAM_B1 = 0.9
ADAM_B2 = 0.999
ADAM_EPS = 1e-08
ADAM_WD = 0.01
ADAM_STEP = 10

V7X_VMEM_BYTES = 64 * 1024 * 1024
VMEM_LIMIT = V7X_VMEM_BYTES - 8 * 1024 * 1024
HBM = pl.BlockSpec(memory_space=pl.ANY)


def _tap(cw_ref, t):
    return jnp.concatenate([cw_ref[d, t:t + 1, :] for d in range(NDEV)], axis=1)


def _pool_map(wp_ref, g):
    rows = HD // NDEV
    return jnp.concatenate([wp_ref[d, g * rows:(g + 1) * rows, :] for d in range(NDEV)], axis=0)


def _whole(shape):
    return pl.BlockSpec(shape, lambda *_: (0,) * len(shape), pipeline_mode=pl.Buffered(1))


def _params(n_grid):
    return pltpu.CompilerParams(dimension_semantics=("arbitrary",) * n_grid, vmem_limit_bytes=VMEM_LIMIT)


def _mm(a, b):
    return jnp.dot(a, b, preferred_element_type=F32)


def _mm_nt(a, b):
    return lax.dot_general(a, b, (((1,), (1,)), ((), ())), preferred_element_type=F32)


def _mm_tn(a, b):
    return lax.dot_general(a, b, (((0,), (0,)), ((), ())), preferred_element_type=F32)


def _sigmoid(x):
    return 1.0 / (1.0 + jnp.exp(-x))


def _rms(x):
    return lax.rsqrt(jnp.mean(x * x, axis=-1, keepdims=True) + EPS)


def _norm_bwd(dh, x, gain):
    r = _rms(x)
    xh = x * r
    dxh = dh * gain
    dx = r * (dxh - xh * jnp.mean(dxh * xh, axis=-1, keepdims=True))
    return dx, jnp.sum(dh * xh, axis=0, keepdims=True)


def _col_chunks(n, width=512):
    return [slice(c, min(c + width, n)) for c in range(0, n, width)]


def _shift_down(v, k):
    return pltpu.roll(v, k, 0)


def _shift_up(v, k):
    return pltpu.roll(v, v.shape[0] - k, 0)


def _fwd_proj(x, gain, w_blocks, w_ids, p_ids, tm):
    T = x.shape[0]

    def body(w_ids_ref, p_ids_ref, x_ref, g_ref, w_ref, proj_ref, h_ref):
        del w_ids_ref, p_ids_ref

        @pl.when(pl.program_id(1) == 0)
        def _():
            xf = x_ref[...]
            h_ref[...] = (xf * _rms(xf) * g_ref[...]).astype(BF16)
        proj_ref[...] = _mm(h_ref[...], w_ref[...]).astype(BF16)

    return pl.pallas_call(
        body, name="fwd_proj",
        grid_spec=pltpu.PrefetchScalarGridSpec(
            num_scalar_prefetch=2, grid=(T // tm, w_ids.shape[0]),
            in_specs=[pl.BlockSpec((tm, D), lambda i, j, w, p: (i, 0)), pl.BlockSpec((1, D), lambda i, j, w, p: (0, 0)),
                      pl.BlockSpec((None, D, D), lambda i, j, w, p: (w[j], 0, 0))],
            out_specs=[pl.BlockSpec((None, tm, D), lambda i, j, w, p: (p[j], i, 0)),
                       pl.BlockSpec((tm, D), lambda i, j, w, p: (i, 0))]),
        out_shape=[SDS((NSPLIT, T, D), BF16), SDS((T, D), BF16)],
        compiler_params=_params(2))(w_ids, p_ids, x, gain, w_blocks)


def _fwd_proj_more(h, w_blocks, proj, w_ids, p_ids, tm, name):
    T = h.shape[0]

    def body(w_ids_ref, p_ids_ref, h_ref, w_ref, proj_hbm, proj_ref):
        del w_ids_ref, p_ids_ref, proj_hbm
        proj_ref[...] = _mm(h_ref[...], w_ref[...]).astype(BF16)

    return pl.pallas_call(
        body, name=name,
        grid_spec=pltpu.PrefetchScalarGridSpec(
            num_scalar_prefetch=2, grid=(T // tm, w_ids.shape[0]),
            in_specs=[pl.BlockSpec((tm, D), lambda i, j, w, p: (i, 0)),
                      pl.BlockSpec((None, D, D), lambda i, j, w, p: (w[j], 0, 0)), HBM],
            out_specs=pl.BlockSpec((None, tm, D), lambda i, j, w, p: (p[j], i, 0))),
        out_shape=SDS(proj.shape, BF16), input_output_aliases={4: 0},
        compiler_params=_params(2))(w_ids, p_ids, h, w_blocks, proj)


def _halo_before(split, tm):
    return pl.BlockSpec((None, HALO, D), lambda i: (split, jnp.maximum(i * (tm // HALO) - 1, 0), 0))


def _fwd_mix(proj, cw, w_co, w_pool, mem, gain_mem, w_kv, tm):
    T = proj.shape[1]
    M = mem.shape[0]

    def body(b_ref, c_ref, ua_ref, up_ref, ch_ref, uah_ref, uph_ref, cw_ref, wco_ref, wp_ref,
             mem_ref, gm_ref, wkv_ref, za_ref, conv_ref, pooled_ref, ya_ref, yp_ref, kv_ref, memn_ref):
        i = pl.program_id(0)

        @pl.when(i == 0)
        def _():
            m = mem_ref[...]
            memn = (m * _rms(m) * gm_ref[...]).astype(BF16)
            memn_ref[...] = memn
            for j in range(2 * NH):
                kv_ref[j] = _mm(memn, wkv_ref[j]).astype(BF16)
        keep = jnp.where(i > 0, 1.0, 0.0).astype(F32)
        cu = c_ref[...].astype(F32) * ua_ref[...].astype(F32)
        cu_h = ch_ref[...].astype(F32) * uah_ref[...].astype(F32) * keep
        ext = jnp.concatenate([cu_h, cu], axis=0)
        conv = (_tap(cw_ref, 2) * ext + _tap(cw_ref, 1) * _shift_down(ext, 1) + _tap(cw_ref, 0) * _shift_down(ext, 2))[HALO:]
        za = (b_ref[...].astype(F32) * conv).astype(BF16)
        conv_ref[...] = conv.astype(BF16)
        za_ref[...] = za
        ya_ref[...] = _mm(za, wco_ref[...]).astype(BF16)

        up = up_ref[...].astype(F32)
        ext_u = jnp.concatenate([uph_ref[...].astype(F32) * keep, up], axis=0)
        pos = i * tm + lax.broadcasted_iota(jnp.int32, (tm, HD), 0)
        for g in range(NPOOL):
            cols = slice(g * HD, (g + 1) * HD)
            s = ext_u[:, cols]
            for k in range(g + 1):
                s = s + _shift_down(s, 1 << k)
            cnt = jnp.minimum(pos + 1, 2 << g).astype(F32)
            pooled = (s[HALO:] / cnt - up[:, cols]).astype(BF16)
            pooled_ref[:, cols] = pooled
            yp_ref[:, cols] = _mm(pooled, _pool_map(wp_ref, g)).astype(BF16)

    tile = lambda s: pl.BlockSpec((None, tm, D), lambda i: (s, i, 0))
    row = pl.BlockSpec((1, D), lambda i: (0, 0))
    out = pl.BlockSpec((tm, D), lambda i: (i, 0))
    return pl.pallas_call(
        body, name="fwd_mix", grid=(T // tm,),
        in_specs=[tile(0), tile(1), tile(2), tile(3), _halo_before(1, tm), _halo_before(2, tm), _halo_before(3, tm),
                  _whole(cw.shape), _whole((D, D)), _whole(w_pool.shape), _whole((M, D)), row, _whole((2 * NH, D, HD))],
        out_specs=[out] * 5 + [pl.BlockSpec((2 * NH, M, HD), lambda i: (0, 0, 0)), pl.BlockSpec((M, D), lambda i: (0, 0))],
        out_shape=[SDS((T, D), BF16)] * 5 + [SDS((2 * NH, M, HD), BF16), SDS((M, D), BF16)],
        compiler_params=_params(1))(proj, proj, proj, proj, proj, proj, proj, cw, w_co, w_pool, mem, gain_mem, w_kv)


def _softmax_rows(s):
    e = jnp.exp(s - jnp.max(s, axis=-1, keepdims=True))
    return e / jnp.sum(e, axis=-1, keepdims=True)


def _fwd_merge(proj, ya, yp, x, kv, w_xo, w_o, pscale, gain_ffn, tm):
    T = x.shape[0]

    def body(q_ref, ga_ref, gp_ref, gx_ref, ya_ref, yp_ref, x_ref, kv_ref, wxo_ref, wo_ref, ps_ref, gf_ref,
             o_ref, yx_ref, merged_ref, x1_ref, h2_ref):
        for h in range(NH):
            cols = slice(h * HD, (h + 1) * HD)
            p = _softmax_rows(_mm_nt(q_ref[:, cols], kv_ref[h]) * ATT_SCALE)
            o_ref[:, cols] = _mm(p.astype(BF16), kv_ref[NH + h]).astype(BF16)
        yx = _mm(o_ref[...], wxo_ref[...])
        yx_ref[...] = yx.astype(BF16)
        merged = (_sigmoid(ga_ref[...].astype(F32)) * ya_ref[...].astype(F32)
                  + _sigmoid(gp_ref[...].astype(F32)) * (yp_ref[...].astype(F32) * ps_ref[...])
                  + _sigmoid(gx_ref[...].astype(F32)) * yx).astype(BF16)
        merged_ref[...] = merged
        x1 = x_ref[...] + _mm(merged, wo_ref[...])
        x1_ref[...] = x1
        h2_ref[...] = (x1 * _rms(x1) * gf_ref[...]).astype(BF16)

    tile = lambda s: pl.BlockSpec((None, tm, D), lambda i: (s, i, 0))
    row = pl.BlockSpec((1, D), lambda i: (0, 0))
    act = pl.BlockSpec((tm, D), lambda i: (i, 0))
    full = _whole((D, D))
    return pl.pallas_call(
        body, name="fwd_merge", grid=(T // tm,),
        in_specs=[tile(4), tile(5), tile(6), tile(7), act, act, act,
                  _whole((2 * NH, kv.shape[1], HD)), full, full, row, row],
        out_specs=[act] * 5,
        out_shape=[SDS((T, D), BF16), SDS((T, D), BF16), SDS((T, D), BF16), SDS((T, D), F32), SDS((T, D), BF16)],
        compiler_params=_params(1))(proj, proj, proj, proj, ya, yp, x, kv, w_xo, w_o, pscale, gain_ffn)


def _fwd_ffn_up(h2, wg_t, wu_t, tm, tn):
    T = h2.shape[0]

    def body(h_ref, wg_ref, wu_ref, gate_ref, up_ref, act_ref):
        for cols in _col_chunks(tn):
            gate = _mm_nt(h_ref[...], wg_ref[cols, :])
            up = _mm_nt(h_ref[...], wu_ref[cols, :])
            gate_ref[:, cols] = gate.astype(BF16)
            up_ref[:, cols] = up.astype(BF16)
            act_ref[:, cols] = (gate * _sigmoid(gate) * up).astype(BF16)

    w = pl.BlockSpec((tn, D), lambda n, i: (n, 0))
    o = pl.BlockSpec((tm, tn), lambda n, i: (i, n))
    return pl.pallas_call(
        body, name="fwd_ffn_up", grid=(DFF // tn, T // tm),
        in_specs=[pl.BlockSpec((tm, D), lambda n, i: (i, 0)), w, w],
        out_specs=[o] * 3, out_shape=[SDS((T, DFF), BF16)] * 3,
        compiler_params=_params(2))(h2, wg_t, wu_t)


def _fwd_ffn_down_loss(act, w_d, x1, target, gain_final, tm):
    T = x1.shape[0]

    def body(act_ref, wd_ref, x1_ref, tgt_ref, g_ref, dx2_ref, loss_ref, dgain_ref):
        @pl.when(pl.program_id(0) == 0)
        def _():
            loss_ref[...] = jnp.zeros_like(loss_ref)
            dgain_ref[...] = jnp.zeros_like(dgain_ref)
        x2 = x1_ref[...] + _mm(act_ref[...], wd_ref[...])
        gain = g_ref[...]
        y = x2 * _rms(x2) * gain
        err = y - tgt_ref[...]
        loss_ref[...] += 0.5 * jnp.sum(jnp.mean(err * err, axis=-1, keepdims=True))
        dx2, dgain = _norm_bwd(err * (1.0 / D), x2, gain)
        dx2_ref[...] = dx2
        dgain_ref[...] += dgain

    act_spec = pl.BlockSpec((tm, D), lambda i: (i, 0))
    row = pl.BlockSpec((1, D), lambda i: (0, 0))
    return pl.pallas_call(
        body, name="fwd_ffn_down_loss", grid=(T // tm,),
        in_specs=[pl.BlockSpec((tm, DFF), lambda i: (i, 0)), _whole((DFF, D)), act_spec, act_spec, row],
        out_specs=[act_spec, pl.BlockSpec((8, D), lambda i: (0, 0)), row],
        out_shape=[SDS((T, D), F32), SDS((8, D), F32), SDS((1, D), F32)],
        compiler_params=_params(1))(act, w_d, x1, target, gain_final)


def _bwd_ffn_down(dx2, w_d, gate, up, tm, tn):
    T = dx2.shape[0]

    def body(dx_ref, wd_ref, gate_ref, up_ref, dgate_ref, dup_ref):
        dx = dx_ref[...].astype(BF16)
        for cols in _col_chunks(tn):
            dact = _mm_nt(dx, wd_ref[cols, :])
            gate = gate_ref[:, cols].astype(F32)
            sg = _sigmoid(gate)
            dgate_ref[:, cols] = (dact * up_ref[:, cols].astype(F32) * (sg * (1.0 + gate * (1.0 - sg)))).astype(BF16)
            dup_ref[:, cols] = (dact * gate * sg).astype(BF16)

    o = pl.BlockSpec((tm, tn), lambda n, i: (i, n))
    return pl.pallas_call(
        body, name="bwd_ffn_down", grid=(DFF // tn, T // tm),
        in_specs=[pl.BlockSpec((tm, D), lambda n, i: (i, 0)), pl.BlockSpec((tn, D), lambda n, i: (n, 0)), o, o],
        out_specs=[o] * 2, out_shape=[SDS((T, DFF), BF16)] * 2,
        compiler_params=_params(2))(dx2, w_d, gate, up)


def _bwd_ffn_up(dgate, dup, wg_t, wu_t, x1, dx2, gain_ffn, tm):
    T = x1.shape[0]

    def body(dg_ref, du_ref, wg_ref, wu_ref, x1_ref, dx2_ref, g_ref, dx1_ref, dgain_ref):
        @pl.when(pl.program_id(0) == 0)
        def _():
            dgain_ref[...] = jnp.zeros_like(dgain_ref)
        dh2 = _mm(dg_ref[...], wg_ref[...]) + _mm(du_ref[...], wu_ref[...])
        dx, dgain = _norm_bwd(dh2, x1_ref[...], g_ref[...])
        dx1_ref[...] = dx2_ref[...] + dx
        dgain_ref[...] += dgain

    wide = pl.BlockSpec((tm, DFF), lambda i: (i, 0))
    w = _whole((DFF, D))
    act = pl.BlockSpec((tm, D), lambda i: (i, 0))
    row = pl.BlockSpec((1, D), lambda i: (0, 0))
    return pl.pallas_call(
        body, name="bwd_ffn_up", grid=(T // tm,),
        in_specs=[wide, wide, w, w, act, act, row], out_specs=[act, row],
        out_shape=[SDS((T, D), F32), SDS((1, D), F32)],
        compiler_params=_params(1))(dgate, dup, wg_t, wu_t, x1, dx2, gain_ffn)


def _wgrad(a, b, *, name, groups, a_cols, b_cols, tt, a_index, b_index, o_index, out_shape, after):
    T = a.shape[0]
    nt = T // tt
    n_a = a.shape[1] // a_cols if groups == 1 else 1

    def body(a_ref, b_ref, after_ref, o_ref, acc_ref):
        del after_ref
        t = pl.program_id(2)

        @pl.when(t == 0)
        def _():
            acc_ref[...] = jnp.zeros_like(acc_ref)
        acc_ref[...] += _mm_tn(a_ref[...].astype(BF16), b_ref[...].astype(BF16))

        @pl.when(t == nt - 1)
        def _():
            o_ref[...] = acc_ref[...].astype(o_ref.dtype)

    return pl.pallas_call(
        body, name=name, grid=(groups, n_a, nt),
        in_specs=[pl.BlockSpec((tt, a_cols), a_index), pl.BlockSpec((None, tt, b_cols), b_index), HBM],
        out_specs=pl.BlockSpec((None, a_cols, b_cols), o_index),
        out_shape=SDS(out_shape, BF16),
        scratch_shapes=[pltpu.VMEM((a_cols, b_cols), F32)],
        compiler_params=_params(3))(a, b, after)


def _wgrad_dense(a, b, name, tt, after, a_cols=None):
    ka, nb = a.shape[1], b.shape[1]
    a_cols = ka if a_cols is None else a_cols
    out = _wgrad(a, b[None], name=name, groups=1, a_cols=a_cols, b_cols=nb, tt=tt,
                 a_index=lambda g, k, t: (t, k), b_index=lambda g, k, t: (0, t, 0),
                 o_index=lambda g, k, t: (k, 0, 0), out_shape=(ka // a_cols, a_cols, nb), after=after)
    return out.reshape(ka, nb)


def _bwd_merge(dx1, proj, ya, yp, yx, pooled, pscale, w_o, w_co, w_xo, w_pool, tm, after):
    T = dx1.shape[0]
    nt = T // tm

    def body(dx1_ref, ga_ref, gp_ref, gx_ref, ya_ref, yp_ref, yx_ref, pooled_ref, ps_ref, wo_ref, wco_ref, wxo_ref, wp_ref,
             after_ref, dgates_ref, dya_ref, dyx_ref, dza_ref, do_ref, dpooled_ref, dps_ref, dwp_ref, acc_ref):
        del after_ref

        @pl.when(pl.program_id(0) == 0)
        def _():
            dps_ref[...] = jnp.zeros_like(dps_ref)
            acc_ref[...] = jnp.zeros_like(acc_ref)
        dmerged = _mm_nt(dx1_ref[...].astype(BF16), wo_ref[...])
        scale = ps_ref[...]
        sa, sp, sx = (_sigmoid(r[...].astype(F32)) for r in (ga_ref, gp_ref, gx_ref))
        ya, yp_pre, yx = (r[...].astype(F32) for r in (ya_ref, yp_ref, yx_ref))
        dgates_ref[0] = (dmerged * ya * sa * (1.0 - sa)).astype(BF16)
        dgates_ref[1] = (dmerged * (yp_pre * scale) * sp * (1.0 - sp)).astype(BF16)
        dgates_ref[2] = (dmerged * yx * sx * (1.0 - sx)).astype(BF16)
        dya = (dmerged * sa).astype(BF16)
        dyx = (dmerged * sx).astype(BF16)
        dyp = dmerged * sp
        dyps = (dyp * scale).astype(BF16)
        dps_ref[...] += jnp.sum(dyp * yp_pre, axis=0, keepdims=True)
        dya_ref[...] = dya
        dyx_ref[...] = dyx
        dza_ref[...] = _mm_nt(dya, wco_ref[...]).astype(BF16)
        do_ref[...] = _mm_nt(dyx, wxo_ref[...]).astype(BF16)
        for g in range(NPOOL):
            cols = slice(g * HD, (g + 1) * HD)
            dpooled_ref[:, cols] = _mm_nt(dyps[:, cols], _pool_map(wp_ref, g)).astype(BF16)
            acc_ref[g] += _mm_tn(pooled_ref[:, cols], dyps[:, cols])

        @pl.when(pl.program_id(0) == nt - 1)
        def _():
            dwp_ref[...] = acc_ref[...].astype(BF16)

    tile = lambda s: pl.BlockSpec((None, tm, D), lambda i: (s, i, 0))
    row = pl.BlockSpec((1, D), lambda i: (0, 0))
    act = pl.BlockSpec((tm, D), lambda i: (i, 0))
    full = _whole((D, D))
    return pl.pallas_call(
        body, name="bwd_merge", grid=(T // tm,),
        in_specs=[act, tile(5), tile(6), tile(7), act, act, act, act, row, full, full, full,
                  _whole(w_pool.shape), HBM],
        out_specs=[pl.BlockSpec((3, tm, D), lambda i: (0, i, 0))] + [act] * 5
        + [row, pl.BlockSpec((NPOOL, HD, HD), lambda i: (0, 0, 0))],
        out_shape=[SDS((NSPLIT, T, D), BF16)] + [SDS((T, D), BF16)] * 5 + [SDS((1, D), F32), SDS((NPOOL, HD, HD), BF16)],
        scratch_shapes=[pltpu.VMEM((NPOOL, HD, HD), F32)],
        compiler_params=_params(1))(dx1, proj, proj, proj, ya, yp, yx, pooled, pscale, w_o, w_co, w_xo, w_pool, after)


def _bwd_attn(dproj, proj, do, kv, memn, w_kv, mem, gain_mem, tm):
    T = do.shape[0]
    M = kv.shape[1]
    nt = T // tm

    def body(dproj_hbm, q_ref, do_ref, kv_ref, memn_ref, wkv_ref, mem_ref, gm_ref, dq_ref, dw_ref, dgain_ref, dkv_ref):
        del dproj_hbm

        @pl.when(pl.program_id(0) == 0)
        def _():
            dkv_ref[...] = jnp.zeros_like(dkv_ref)
        for h in range(NH):
            cols = slice(h * HD, (h + 1) * HD)
            q = q_ref[:, cols]
            do_h = do_ref[:, cols]
            p = _softmax_rows(_mm_nt(q, kv_ref[h]) * ATT_SCALE)
            dp = _mm_nt(do_h, kv_ref[NH + h])
            ds = (p * (dp - jnp.sum(dp * p, axis=-1, keepdims=True)) * ATT_SCALE).astype(BF16)
            dq_ref[:, cols] = _mm(ds, kv_ref[h]).astype(BF16)
            dkv_ref[h] += _mm_tn(ds, q)
            dkv_ref[NH + h] += _mm_tn(p.astype(BF16), do_h)

        @pl.when(pl.program_id(0) == nt - 1)
        def _():
            dmemn = jnp.zeros((M, D), F32)
            for j in range(2 * NH):
                dkv_j = dkv_ref[j].astype(BF16)
                dw_ref[j] = _mm_tn(memn_ref[...], dkv_j).astype(BF16)
                dmemn = dmemn + _mm_nt(dkv_j, wkv_ref[j])
            dgain_ref[...] = _norm_bwd(dmemn, mem_ref[...], gm_ref[...])[1]

    row = pl.BlockSpec((1, D), lambda i: (0, 0))
    return pl.pallas_call(
        body, name="bwd_attn", grid=(nt,),
        in_specs=[HBM, pl.BlockSpec((None, tm, D), lambda i: (4, i, 0)), pl.BlockSpec((tm, D), lambda i: (i, 0)),
                  _whole((2 * NH, M, HD)), _whole((M, D)), _whole((2 * NH, D, HD)), _whole((M, D)), row],
        out_specs=[pl.BlockSpec((None, tm, D), lambda i: (3, i, 0)),
                   pl.BlockSpec((2 * NH, D, HD), lambda i: (0, 0, 0)), row],
        out_shape=[SDS(dproj.shape, BF16), SDS((2 * NH, D, HD), BF16), SDS((1, D), F32)],
        scratch_shapes=[pltpu.VMEM((2 * NH, M, HD), F32)],
        input_output_aliases={0: 0},
        compiler_params=_params(1))(dproj, proj, do, kv, memn, w_kv, mem, gain_mem)


def _bwd_mix(dproj, proj, conv, dza, dpooled, cw, tm, after):
    T = dza.shape[0]
    nt = T // tm

    def halo_after(split_or_none):
        idx = lambda i: jnp.minimum((i + 1) * (tm // HALO), T // HALO - 1)
        if split_or_none is None:
            return pl.BlockSpec((HALO, D), lambda i: (idx(i), 0))
        return pl.BlockSpec((None, HALO, D), lambda i: (split_or_none, idx(i), 0))

    def body(dproj_hbm, b_ref, c_ref, ua_ref, conv_ref, dza_ref, dpo_ref, bn_ref, dzan_ref, dpon_ref,
             cw_ref, after_ref, dabcu_ref, dcw_ref):
        del dproj_hbm, after_ref
        i = pl.program_id(0)

        @pl.when(i == 0)
        def _():
            dcw_ref[...] = jnp.zeros_like(dcw_ref)
        keep_next = jnp.where(i < nt - 1, 1.0, 0.0).astype(F32)
        dza = dza_ref[...].astype(F32)
        c = c_ref[...].astype(F32)
        ua = ua_ref[...].astype(F32)
        dconv = dza * b_ref[...].astype(F32)
        dconv_n = dzan_ref[...].astype(F32) * bn_ref[...].astype(F32) * keep_next
        ext = jnp.concatenate([dconv, dconv_n], axis=0)
        dconv_1, dconv_2 = _shift_up(ext, 1)[:tm], _shift_up(ext, 2)[:tm]
        dcu = _tap(cw_ref, 2) * dconv + _tap(cw_ref, 1) * dconv_1 + _tap(cw_ref, 0) * dconv_2
        dabcu_ref[0] = (dza * conv_ref[...].astype(F32)).astype(BF16)
        dabcu_ref[1] = (dcu * ua).astype(BF16)
        dabcu_ref[2] = (dcu * c).astype(BF16)

        cu = c * ua
        dcw_ref[2:3, :] += jnp.sum(dconv * cu, axis=0, keepdims=True)
        dcw_ref[1:2, :] += jnp.sum(dconv_1 * cu, axis=0, keepdims=True)
        dcw_ref[0:1, :] += jnp.sum(dconv_2 * cu, axis=0, keepdims=True)

        dpo = dpo_ref[...].astype(F32)
        ext_dpo = jnp.concatenate([dpo, dpon_ref[...].astype(F32) * keep_next], axis=0)
        pos = i * tm + lax.broadcasted_iota(jnp.int32, (tm + HALO, HD), 0)
        for g in range(NPOOL):
            cols = slice(g * HD, (g + 1) * HD)
            s = ext_dpo[:, cols] / jnp.minimum(pos + 1, 2 << g).astype(F32)
            for k in range(g + 1):
                s = s + _shift_up(s, 1 << k)
            dabcu_ref[3, :, cols] = (s[:tm] - dpo[:, cols]).astype(BF16)

    tile = lambda s: pl.BlockSpec((None, tm, D), lambda i: (s, i, 0))
    act = pl.BlockSpec((tm, D), lambda i: (i, 0))
    return pl.pallas_call(
        body, name="bwd_mix", grid=(nt,),
        in_specs=[HBM, tile(0), tile(1), tile(2), act, act, act, halo_after(0), halo_after(None), halo_after(None),
                  _whole(cw.shape), HBM],
        out_specs=[pl.BlockSpec((4, tm, D), lambda i: (1, i, 0)), pl.BlockSpec((8, D), lambda i: (0, 0))],
        out_shape=[SDS(dproj.shape, BF16), SDS((8, D), F32)],
        input_output_aliases={0: 0},
        compiler_params=_params(1))(dproj, proj, proj, proj, conv, dza, dpooled, proj, dza, dpooled, cw, after)


def _bwd_proj(dproj, w_parts, places, x, dx1, gain, tm, after):
    T = x.shape[0]

    def body(places_ref, dp_ref, w0_ref, w1_ref, w2_ref, x_ref, dx1_ref, g_ref, after_ref, dx_ref, dgain_ref, acc_ref):
        del after_ref
        i, s = pl.program_id(0), pl.program_id(1)

        @pl.when((i == 0) & (s == 0))
        def _():
            dgain_ref[...] = jnp.zeros_like(dgain_ref)

        @pl.when(s == 0)
        def _():
            acc_ref[...] = jnp.zeros_like(acc_ref)

        for k, w_ref in enumerate((w0_ref, w1_ref, w2_ref)):
            @pl.when(places_ref[s] == k)
            def _(w_ref=w_ref):
                acc_ref[...] += _mm_nt(dp_ref[...], w_ref[...])

        @pl.when(s == NSPLIT - 1)
        def _():
            dx, dgain = _norm_bwd(acc_ref[...], x_ref[...], g_ref[...])
            dx_ref[...] = dx1_ref[...] + dx
            dgain_ref[...] += dgain

    act = pl.BlockSpec((tm, D), lambda i, s, p: (i, 0))
    row = pl.BlockSpec((1, D), lambda i, s, p: (0, 0))
    weight = lambda k: pl.BlockSpec((None, D, D), lambda i, s, p: (p[NSPLIT * (1 + k) + s], 0, 0))
    return pl.pallas_call(
        body, name="bwd_proj",
        grid_spec=pltpu.PrefetchScalarGridSpec(
            num_scalar_prefetch=1, grid=(T // tm, NSPLIT),
            in_specs=[pl.BlockSpec((None, tm, D), lambda i, s, p: (s, i, 0)), weight(0), weight(1), weight(2),
                      act, act, row, HBM],
            out_specs=[act, row], scratch_shapes=[pltpu.VMEM((tm, D), F32)]),
        out_shape=[SDS((T, D), F32), SDS((1, D), F32)],
        compiler_params=_params(2))(places, dproj, *w_parts, x, dx1, gain, after)


def _adamw_math(w, g, m, v):
    m = ADAM_B1 * m + (1.0 - ADAM_B1) * g
    v = ADAM_B2 * v + (1.0 - ADAM_B2) * (g * g)
    m_hat = m / (1.0 - ADAM_B1 ** ADAM_STEP)
    v_hat = v / (1.0 - ADAM_B2 ** ADAM_STEP)
    delta = -ADAM_LR * (m_hat / (jnp.sqrt(v_hat) + ADAM_EPS) + ADAM_WD * w)
    return delta, m, v


def _adamw_small(parts, me, gains, taps):
    n = len(gains)
    cols = D // NDEV

    def body(me_ref, all_ref, mine_ref, *refs):
        del me_ref
        everywhere, here = all_ref[0], mine_ref[0]
        for k in range(1, NDEV):
            everywhere, here = everywhere + all_ref[k], here + mine_ref[k]
        ins, outs = refs[:3 * (n + 1)], refs[3 * (n + 1):]
        for a in range(n):
            w_ref, m_ref, v_ref = ins[3 * a:3 * a + 3]
            go_ref, d_ref, mo_ref, vo_ref = outs[4 * a:4 * a + 4]
            g = everywhere[a:a + 1]
            go_ref[...] = g
            d_ref[...], mo_ref[...], vo_ref[...] = _adamw_math(w_ref[...], g, m_ref[...], v_ref[...])
        (w_ref, m_ref, v_ref), (go_ref, d_ref, mo_ref, vo_ref) = ins[3 * n:], outs[4 * n:4 * n + 4]
        for t in range(3):
            g = here[n + t:n + t + 1]
            go_ref[t] = g
            d_ref[t], mo_ref[t], vo_ref[t] = _adamw_math(w_ref[t], g, m_ref[t], v_ref[t])
        outs[-1][...] = everywhere[8:9, 0:1]

    whole = lambda shape: pl.BlockSpec(shape, lambda i, me_ref: (0,) * len(shape))
    shapes = [(1, D)] * n + [(3, 1, cols)]
    outs = pl.pallas_call(
        body, name="adamw_replicated",
        grid_spec=pltpu.PrefetchScalarGridSpec(
            num_scalar_prefetch=1, grid=(1,),
            in_specs=[whole(parts.shape), pl.BlockSpec((NDEV, 16, cols), lambda i, me_ref: (0, 0, me_ref[0]))]
            + [whole(s) for s in shapes for _ in range(3)],
            out_specs=[whole(s) for s in shapes for _ in range(4)] + [whole((1, 1))]),
        out_shape=[SDS(s, F32) for s in shapes for _ in range(4)] + [SDS((1, 1), F32)],
        compiler_params=_params(1))(me.reshape(1).astype(jnp.int32), parts, parts, *[a for three in gains + [taps] for a in three])
    return [outs[4 * a:4 * a + 4] for a in range(n + 1)], outs[-1]


def _adamw(ws, gs, ms, vs, name, from_parts, steps):
    n = len(ws)

    def body(*refs):
        for a in range(n):
            w_ref, g_ref, m_ref, v_ref = refs[4 * a:4 * a + 4]
            go_ref, d_ref, mo_ref, vo_ref = refs[4 * n + 4 * a:4 * n + 4 * a + 4]
            if from_parts:
                g = g_ref[0].astype(F32)
                for k in range(1, g_ref.shape[0]):
                    g = g + g_ref[k].astype(F32)
            else:
                g = g_ref[...]
            go_ref[...] = g
            d_ref[...], mo_ref[...], vo_ref[...] = _adamw_math(w_ref[...], g, m_ref[...], v_ref[...])

    in_specs, out_specs, out_shape, operands = [], [], [], []
    for w, g, m, v in zip(ws, gs, ms, vs):
        rows, cols = w.shape
        blk = pl.BlockSpec((rows // steps, cols), lambda i: (i, 0))
        g_spec = pl.BlockSpec((g.shape[0], rows // steps, cols), lambda i: (0, i, 0)) if from_parts else blk
        in_specs += [blk, g_spec, blk, blk]
        out_specs += [blk] * 4
        out_shape += [SDS((rows, cols), F32)] * 4
        operands += [w, g, m, v]
    outs = pl.pallas_call(body, name=name, grid=(steps,), in_specs=in_specs, out_specs=out_specs, out_shape=out_shape,
                          compiler_params=_params(1))(*operands)
    return [outs[4 * a:4 * a + 4] for a in range(n)]


def _peer(k, x, y, c):
    return ((1 - x) if k & 4 else x, (1 - y) if k & 2 else y, (1 - c) if k & 1 else c)


SEM = pl.BlockSpec(memory_space=pltpu.SEMAPHORE)
IN_HBM = pl.BlockSpec(memory_space=pltpu.HBM)
DATAFLOW = pltpu.SideEffectType.DATAFLOW_SIDE_EFFECTING
TOKEN_SHAPE = (8, 128)


OTHER_CHIPS = (2, 4, 6)


def _place(x, y, c):
    return 4 * x + 2 * y + c


def _plan_gather_chips(n, ks=(1,) + OTHER_CHIPS):
    def plan(refs, x, y, c, arriving):
        out = []
        for a in range(n):
            for k in ks:
                there = _place(*_peer(k, x, y, c))
                out.append((refs[a].at[_place(x, y, c)], refs[a].at[there if arriving else _place(x, y, c)], k))
        return out
    return plan, n * len(ks)


def _plan_gather_sibling(n, ks=OTHER_CHIPS):
    def plan(refs, x, y, c, arriving):
        out = []
        for a in range(n):
            for k in ks:
                px, py, pc = _peer(k, x, y, c)
                mine, theirs = _place(px, py, pc), _place(px, py, 1 - pc)
                out.append((refs[a].at[mine], refs[a].at[theirs if arriving else mine], 1))
        return out
    return plan, n * len(ks)


def _plan_pair():
    def plan(refs, x, y, c, arriving):
        return [(refs[0].at[c], refs[0].at[(1 - c) if arriving else c], 1)]
    return plan, 1


def _plan_far_chip():
    def plan(refs, x, y, c, arriving):
        return [(refs[0].at[c], refs[1].at[c], 6)]
    return plan, 1


def _plan_far_sibling():
    def plan(refs, x, y, c, arriving):
        return [(refs[0].at[c], refs[0].at[(1 - c) if arriving else c], 1)]
    return plan, 1


def _plan_scatter_sibling(n):
    def plan(refs, x, y, c, arriving):
        out = []
        for a in range(n):
            for q in range(4):
                out.append((refs[a].at[2 * q + (1 - c)], refs[n + a].at[q], 1))
        return out
    return plan, n * 4


def _plan_scatter_chips(n):
    def plan(refs, x, y, c, arriving):
        out = []
        for a in range(n):
            for k in OTHER_CHIPS:
                px, py, _ = _peer(k, x, y, c)
                out.append((refs[a].at[2 * px + py], refs[n + a].at[(2 * px + py) if arriving else (2 * x + y)], k))
        return out
    return plan, n * 3


def _remote(src, dst, send_sems, recv_sems, i, k):
    x, y, c = (lax.axis_index(n) for n in AXES)
    return pltpu.make_async_remote_copy(src_ref=src, dst_ref=dst, send_sem=send_sems.at[i], recv_sem=recv_sems.at[i],
                                        device_id=_peer(k, x, y, c), device_id_type=pl.DeviceIdType.MESH)


def _copies_start(groups, name, after):
    ng = len(groups)
    total = sum(len(bufs) for bufs, _ in groups)

    def body(*refs):
        sems = refs[1 + total:1 + total + 2 * ng]
        x, y, c = (lax.axis_index(n) for n in AXES)
        off = 1
        for gi, (bufs, (plan, _)) in enumerate(groups):
            for i, (src, dst, k) in enumerate(plan(refs[off:off + len(bufs)], x, y, c, False)):
                _remote(src, dst, sems[2 * gi], sems[2 * gi + 1], i, k).start()
            off += len(bufs)
        refs[-1][...] = jnp.zeros(TOKEN_SHAPE, F32)

    sem_shapes = [pltpu.SemaphoreType.DMA((count,)) for _, (_, count) in groups for _ in range(2)]
    flat = [b for bufs, _ in groups for b in bufs]
    outs = pl.pallas_call(
        body, name=name,
        in_specs=[HBM] + [IN_HBM] * total,
        out_specs=[SEM] * (2 * ng) + [IN_HBM] * total + [pl.BlockSpec(memory_space=pltpu.VMEM)],
        out_shape=sem_shapes + [pltpu.HBM(b.shape, b.dtype) for b in flat] + [SDS(TOKEN_SHAPE, F32)],
        input_output_aliases={1 + i: 2 * ng + i for i in range(total)},
        compiler_params=pltpu.CompilerParams(has_side_effects=DATAFLOW),
    )(after, *[pltpu.with_memory_space_constraint(b, pltpu.HBM) for b in flat])
    handles, off = [], 2 * ng
    for gi, (bufs, _) in enumerate(groups):
        handles.append((outs[2 * gi], outs[2 * gi + 1], list(outs[off:off + len(bufs)])))
        off += len(bufs)
    return handles, outs[-1]


def _copies_wait_start(handle, plan, pass_on, more, name, after):
    send_sems, recv_sems, bufs = handle
    n = len(bufs)
    idx, (pass_plan, pass_count) = pass_on
    total = sum(len(b) for b, _ in more)
    ng = 1 + len(more)

    def body(*refs):
        x, y, c = (lax.axis_index(a) for a in AXES)
        waited = refs[1:1 + n]
        outs = refs[3 + n + total:]
        new_sems = outs[n + total:n + total + 2 * ng]
        for i, (src, dst, k) in enumerate(plan[0](waited, x, y, c, True)):
            copy = _remote(src, dst, refs[1 + n + total], refs[2 + n + total], i, k)
            copy.wait_send()
            copy.wait_recv()
        for i, (src, dst, k) in enumerate(pass_plan([waited[j] for j in idx], x, y, c, False)):
            _remote(src, dst, new_sems[0], new_sems[1], i, k).start()
        off = 1 + n
        for gi, (b, (p, _)) in enumerate(more):
            for i, (src, dst, k) in enumerate(p(refs[off:off + len(b)], x, y, c, False)):
                _remote(src, dst, new_sems[2 + 2 * gi], new_sems[3 + 2 * gi], i, k).start()
            off += len(b)
        outs[-1][...] = jnp.zeros(TOKEN_SHAPE, F32)

    flat = list(bufs) + [a for b, _ in more for a in b]
    sem_shapes = [pltpu.SemaphoreType.DMA((count,)) for count in [pass_count] + [cnt for _, (_, cnt) in more] for _ in range(2)]
    outs = pl.pallas_call(
        body, name=name,
        in_specs=[HBM] + [IN_HBM] * (n + total) + [SEM, SEM],
        out_specs=[IN_HBM] * (n + total) + [SEM] * (2 * ng) + [pl.BlockSpec(memory_space=pltpu.VMEM)],
        out_shape=[pltpu.HBM(b.shape, b.dtype) for b in flat] + sem_shapes + [SDS(TOKEN_SHAPE, F32)],
        input_output_aliases={1 + i: i for i in range(n + total)},
        compiler_params=pltpu.CompilerParams(has_side_effects=DATAFLOW),
    )(after, *[pltpu.with_memory_space_constraint(b, pltpu.HBM) for b in flat], send_sems, recv_sems)
    thru = list(outs[:n])
    sems_out = outs[n + total:n + total + 2 * ng]
    handles = [(sems_out[0], sems_out[1], [thru[j] for j in idx])]
    off = n
    for gi, (b, _) in enumerate(more):
        handles.append((sems_out[2 + 2 * gi], sems_out[3 + 2 * gi], list(outs[off:off + len(b)])))
        off += len(b)
    return thru, handles, outs[-1]


def _copies_wait(handle, plan, name, *after):
    send_sems, recv_sems, bufs = handle
    n = len(bufs)

    def body(*refs):
        x, y, c = (lax.axis_index(a) for a in AXES)
        for i, (src, dst, k) in enumerate(plan[0](refs[:n], x, y, c, True)):
            copy = _remote(src, dst, refs[n], refs[n + 1], i, k)
            copy.wait_send()
            copy.wait_recv()

    return pl.pallas_call(
        body, name=name,
        in_specs=[IN_HBM] * n + [SEM, SEM] + [HBM] * len(after), out_specs=[IN_HBM] * n,
        out_shape=[pltpu.HBM(b.shape, b.dtype) for b in bufs],
        input_output_aliases={i: i for i in range(n)},
        compiler_params=pltpu.CompilerParams(has_side_effects=DATAFLOW),
    )(*bufs, send_sems, recv_sems, *after)


def _pair_sums(mine, theirs, c, chip, name):
    n = len(mine)

    def body(where_ref, *refs):
        q = pl.program_id(0)
        for a in range(n):
            total = (refs[a][...].astype(F32) + refs[n + a][...].astype(F32)).astype(BF16)
            refs[2 * n + a][...] = total

            @pl.when(q == where_ref[1])
            def _():
                refs[3 * n + a][...] = total

    block = lambda t: (None,) + t.shape[1:]
    zeros = lambda t: (0,) * (t.ndim - 1)
    outs = pl.pallas_call(
        body, name=name,
        grid_spec=pltpu.PrefetchScalarGridSpec(
            num_scalar_prefetch=1, grid=(4,),
            in_specs=[pl.BlockSpec(block(t), lambda q, w, z=zeros(t): (2 * q + w[0],) + z) for t in theirs]
            + [pl.BlockSpec(block(t), lambda q, w, z=zeros(t): (q,) + z) for t in theirs],
            out_specs=[pl.BlockSpec(block(t), lambda q, w, z=zeros(t): (q,) + z) for t in theirs]
            + [pl.BlockSpec(block(t), lambda q, w, z=zeros(t): (w[1],) + z) for t in theirs]),
        out_shape=[SDS(t.shape, BF16) for t in theirs] * 2,
        compiler_params=_params(1))(jnp.stack([c, chip]).astype(jnp.int32), *mine, *theirs)
    return list(outs[:n]), list(outs[n:])


def _wgrad_in_swap(h, dproj, order, tt, after):
    T = h.shape[0]
    nt = T // tt
    swapped = NSPLIT // 2

    def body(order_ref, a_ref, b_ref, after_ref, o_ref, land_ref, acc_ref, sent_ref, send_sems, recv_sems):
        del order_ref, after_ref
        g, t = pl.program_id(0), pl.program_id(1)

        @pl.when(t == 0)
        def _():
            acc_ref[...] = jnp.zeros_like(acc_ref)
        acc_ref[...] += _mm_tn(a_ref[...], b_ref[...])

        @pl.when(t == nt - 1)
        def _():
            o_ref[...] = acc_ref[...].astype(BF16)

        for i in range(swapped):
            copy = _remote(sent_ref.at[i], land_ref.at[i], send_sems, recv_sems, i, 1)

            @pl.when((g == i) & (t == nt - 1))
            def _(i=i, copy=copy):
                sent_ref[i] = acc_ref[...].astype(BF16)
                copy.start()

            @pl.when((g == NSPLIT - 1) & (t == nt - 1))
            def _(copy=copy):
                copy.wait_send()
                copy.wait_recv()

    return pl.pallas_call(
        body, name="wgrad_in",
        grid_spec=pltpu.PrefetchScalarGridSpec(
            num_scalar_prefetch=1, grid=(NSPLIT, nt),
            in_specs=[pl.BlockSpec((tt, D), lambda g, t, order: (t, 0)),
                      pl.BlockSpec((None, tt, D), lambda g, t, order: (order[g], t, 0)), HBM],
            out_specs=[pl.BlockSpec((None, D, D), lambda g, t, order: (_slot_group(order[g]), 0, 0)), HBM],
            scratch_shapes=[pltpu.VMEM((D, D), F32), pltpu.VMEM((swapped, D, D), BF16),
                            pltpu.SemaphoreType.DMA((swapped,)), pltpu.SemaphoreType.DMA((swapped,))]),
        out_shape=[SDS((NSPLIT, D, D), BF16), SDS((swapped, D, D), BF16)],
        compiler_params=_params(2))(order, h, dproj, after)


def _local_step(x, mem, target, gains, get, put, flush, tm_huge=2048, tm_big=1024, tm_mid=512, tm_small=256):
    g_mix, pscale, g_mem, g_ffn, g_fin = gains
    T = x.shape[0]
    tm_huge, tm_big, tm_mid, tm_small = min(tm_huge, T), min(tm_big, T), min(tm_mid, T), min(tm_small, T)
    tn = DFF // 2

    w_pair, w_ids, p_ids = get("in_pair", x)
    proj, h = _fwd_proj(x, g_mix, w_pair, w_ids, p_ids, tm_huge)
    w_near, w_ids, p_ids = get("in_near", h)
    proj = _fwd_proj_more(h, w_near, proj, w_ids, p_ids, tm_huge, "fwd_proj_near")
    w_far, w_ids, p_ids = get("in_far", proj)
    proj = _fwd_proj_more(h, w_far, proj, w_ids, p_ids, tm_huge, "fwd_proj_far")
    cw, w_co, w_pool, w_kv = get("mix", proj)
    za, conv, pooled, ya, yp, kv, memn = _fwd_mix(proj, cw, w_co, w_pool, mem, g_mem, w_kv, tm_mid)
    w_xo, w_o = get("merge", ya)
    o, yx, merged, x1, h2 = _fwd_merge(proj, ya, yp, x, kv, w_xo, w_o, pscale, g_ffn, tm_mid)
    wg_t, wu_t = get("gate_up", x1)
    get("down", x1, early=True)
    gate, up, act = _fwd_ffn_up(h2, wg_t, wu_t, tm_big, tn)
    (w_d,) = get("down", gate)
    dx2, loss, dg_fin = _fwd_ffn_down_loss(act, w_d, x1, target, g_fin, tm_mid)

    dgate, dup = _bwd_ffn_down(dx2, w_d, gate, up, tm_big, tn)
    dx1, dg_ffn = _bwd_ffn_up(dgate, dup, wg_t, wu_t, x1, dx2, g_ffn, tm_small)
    dw_d = _wgrad_dense(act, dx2, "wgrad_down", tm_big, g_mix)
    dwg_t = _wgrad_dense(dgate, h2, "wgrad_gate", tm_big, g_mix)
    dwu_t = _wgrad_dense(dup, h2, "wgrad_up", tm_big, g_mix)
    token = put("ffn", (dwg_t, dwu_t, dw_d))

    dproj, dya, dyx, dza, do, dpooled, dpscale, dw_pool = _bwd_merge(
        dx1, proj, ya, yp, yx, pooled, pscale, w_o, w_co, w_xo, w_pool, tm_mid, token)
    token = flush(dya)
    dw_o = _wgrad_dense(merged, dx1, "wgrad_out", tm_big, token)
    dw_co = _wgrad_dense(za, dya, "wgrad_conv_out", tm_big, token)
    dw_xo = _wgrad_dense(o, dyx, "wgrad_xattn_out", tm_big, token)
    dproj, dw_kv, dg_mem = _bwd_attn(dproj, proj, do, kv, memn, w_kv, mem, g_mem, tm_big)
    token = put("mix", (dw_co, dw_xo, dw_o, dw_pool, dw_kv))

    dproj, dcw = _bwd_mix(dproj, proj, conv, dza, dpooled, cw, tm_mid, token)
    token = flush(dcw)
    token = flush(put("in", (h, dproj, tm_huge, token)))
    w_pair, places = get("in_places", None)
    grad_x, dg_mix = _bwd_proj(dproj, (w_pair, w_near, w_far), places, x, dx1, g_mix, tm_big, token)

    small = jnp.concatenate([dg_mix, dpscale, dg_mem, dg_ffn, dg_fin, dcw[0:3], loss], axis=0)
    return grad_x, small


def kernel(x, mem, norm_mix, w_in, conv_w, w_conv_out, w_pool, pool_scale, norm_mem, w_kv, w_xattn_out, w_out, norm_ffn, w_gate, w_up, w_down, norm_final, loss_target, m_norm_mix, m_w_in, m_conv_w, m_w_conv_out, m_w_pool, m_pool_scale, m_norm_mem, m_w_kv, m_w_xattn_out, m_w_out, m_norm_ffn, m_w_gate, m_w_up, m_w_down, m_norm_final, v_norm_mix, v_w_in, v_conv_w, v_w_conv_out, v_w_pool, v_pool_scale, v_norm_mem, v_w_kv, v_w_xattn_out, v_w_out, v_norm_ffn, v_w_gate, v_w_up, v_w_down, v_norm_final):
    T = x.shape[1]
    rows = D // NDEV
    ffb = DFF // NDEV
    prow = HD // NDEV
    me = 4 * lax.axis_index("x") + 2 * lax.axis_index("y") + lax.axis_index("c")

    shards = [w_in[0].astype(BF16), w_conv_out[0].astype(BF16), w_xattn_out[0].astype(BF16), w_out[0].astype(BF16),
              w_pool[0].astype(BF16).reshape(NPOOL * prow, HD), w_kv[0].astype(BF16),
              w_gate[0].T.astype(BF16), w_up[0].T.astype(BF16), w_down[0].astype(BF16),
              jnp.pad(conv_w[0], ((0, 5), (0, 0)))]

    cx, cy, cc = (lax.axis_index(n) for n in AXES)
    chip = 2 * cx + cy

    def land(own, index, slots):
        return lax.dynamic_update_index_in_dim(lax.empty((slots,) + own.shape, own.dtype), own, index, 0)

    needed = ["in_pair", "in_near", "in_far", "mix", "merge", "gate_up", "down"]
    members = {"mix": [9, 1, 4, 5], "merge": [2, 3], "gate_up": [6, 7], "down": [8]}
    near = (2, 4)
    plans = {"in_pair": _plan_pair(), "in_near": _plan_gather_chips(1, near), "in_far": _plan_far_chip()}
    plans.update({n: _plan_gather_chips(len(members[n])) for n in members})
    g_bufs = {"in_pair": [land(shards[0], cc, 2)], "in_near": [land(shards[0], me, NDEV)],
              "in_far": [None, lax.empty((2, D, D), BF16)]}
    g_bufs.update({n: [land(shards[i], me, NDEV) for i in members[n]] for n in members})
    first_handles, _ = _copies_start([(g_bufs[n], plans[n]) for n in needed[:2]], "gather_start", x)
    g_handles = dict(zip(needed[:2], first_handles))
    pair_ids = jnp.array([0, 1], jnp.int32)

    on_last_leg = {}

    def get(group, after, early=False):
        if group == "in_places":
            return g_bufs["in_pair"][0], _w_in_places()[chip]
        if group == "in_pair":
            rest = [b for n in members for b in g_bufs[n]]
            g_bufs["in_far"][0], = _copies_wait(g_handles[group], plans[group], "gather_wait_" + group, after, *rest)
            return g_bufs["in_far"][0], pair_ids, (2 * chip + pair_ids).astype(jnp.int32)
        if group not in on_last_leg:
            n_bufs = len(g_bufs[group])
            if group == "in_near":
                landed, plan, more = [0], _plan_gather_sibling(1, near), [(g_bufs[n], plans[n]) for n in needed[2:]]
            elif group == "in_far":
                landed, plan, more = [1], _plan_far_sibling(), []
            else:
                landed, plan, more = list(range(n_bufs)), _plan_gather_sibling(n_bufs), []
            thru, handles, token = _copies_wait_start(g_handles[group], plans[group], (landed, plan), more,
                                                      "gather_pass_" + group, after)
            if group == "in_far":
                g_bufs["in_pair"] = thru[:1]
            g_handles.update(zip(needed[2:], handles[1:]))
            on_last_leg[group] = (handles[0], plan, token)
        if early:
            return None
        handle, plan, token = on_last_leg[group]
        got = _copies_wait(handle, plan, "gather_passed_" + group, after if group in ("gate_up", "down") else token)
        if group == "in_near":
            groups = jnp.stack([me ^ k for k in (2, 3, 4, 5)]).astype(jnp.int32)
            return got[0], groups, groups
        if group == "in_far":
            return got[0], pair_ids, (2 * (3 - chip) + pair_ids).astype(jnp.int32)
        if group == "mix":
            cw_g, w_co_g, w_pool_g, w_kv_g = got
            return cw_g, w_co_g.reshape(D, D), w_pool_g, w_kv_g
        if group == "merge":
            return got[0].reshape(D, D), got[1].reshape(D, D)
        return [g.reshape(DFF, D) for g in got]

    started = {}

    def put(group, grads):
        if group == "ffn":
            sends = [g.reshape(NDEV, ffb, D) for g in grads]
        elif group == "mix":
            dw_co, dw_xo, dw_o, dw_pool, dw_kv = grads
            sends = [dw_co.reshape(NDEV, rows, D), dw_xo.reshape(NDEV, rows, D), dw_o.reshape(NDEV, rows, D),
                     dw_pool.reshape(NPOOL, NDEV, prow, HD).transpose(1, 0, 2, 3).reshape(NDEV, NPOOL * prow, HD), dw_kv]
        else:
            slot_of = [SLOT_GROUPS.index(g) for g in range(NSPLIT)]
            order = jnp.array([[slot_of[2 * q + 1 - c] for q in range(4)] + [slot_of[2 * q + c] for q in range(4)]
                               for c in range(2)], jnp.int32)[cc]
            swapping.append((group, None, 1, list(_wgrad_in_swap(grads[0], grads[1], order, *grads[2:]))))
            return swapping[-1][3][0]
        n = len(sends)
        halves = [lax.empty((4,) + s.shape[1:], s.dtype) for s in sends]
        (handle,), token = _copies_start([(sends + halves, _plan_scatter_sibling(n))], "scatter_swap_" + group, norm_mix)
        swapping.append((group, handle, n, None))
        return token

    swapping = []

    def flush(after):
        group, handle, n, bufs = swapping.pop()
        if handle is not None:
            bufs = _copies_wait(handle, _plan_scatter_sibling(n), "scatter_swapped_" + group, after)
        sums, lands = _pair_sums(bufs[:n], bufs[n:], cc, chip, "pair_sums_" + group)
        (handle,), token = _copies_start([(sums + lands, _plan_scatter_chips(n))], "scatter_start_" + group, norm_mix)
        started[group] = (handle, _plan_scatter_chips(n))
        return token

    def take(group, after):
        handle, plan = started[group]
        return _copies_wait(handle, plan, "scatter_wait_" + group, after)[len(handle[2]) // 2:]

    gains = (norm_mix, pool_scale, norm_mem, norm_ffn, norm_final.reshape(1, D))
    grad_x, small = _local_step(x[0], mem[0], loss_target[0], gains, get, put, flush)

    everyone = _plan_gather_chips(1, tuple(range(1, NDEV)))
    (small_handle,), token = _copies_start([([land(small, me, NDEV)], everyone)], "small_start", norm_mix)

    res = {}

    def update(group, names, ws, gs, ms, vs, from_parts, steps, transposed=()):
        view = lambda a, name: a[0].T if name in transposed else a
        flat = [[view(a, name).reshape(g.shape[-2:]) for a in (w, m, v)] for name, w, g, m, v in zip(names, ws, gs, ms, vs)]
        outs = _adamw([f[0] for f in flat], gs, [f[1] for f in flat], [f[2] for f in flat], "adamw_" + group,
                      from_parts, steps)
        for name, w, four in zip(names, ws, outs):
            res[name] = [(o.T if name in transposed else o).reshape(w.shape) for o in four]

    p_g, p_u, p_d = take("ffn", token)
    update("ffn", ["w_gate", "w_up", "w_down"], [w_gate, w_up, w_down], [p_g, p_u, p_d],
           [m_w_gate, m_w_up, m_w_down], [v_w_gate, v_w_up, v_w_down], True, 2, transposed=("w_gate", "w_up"))

    small_all, = _copies_wait(small_handle, everyone, "small_wait", res["w_down"][1])
    replicated = {"norm_mix": (norm_mix, m_norm_mix, v_norm_mix), "pool_scale": (pool_scale, m_pool_scale, v_pool_scale),
                  "norm_mem": (norm_mem, m_norm_mem, v_norm_mem), "norm_ffn": (norm_ffn, m_norm_ffn, v_norm_ffn),
                  "norm_final": (norm_final, m_norm_final, v_norm_final), "conv_w": (conv_w, m_conv_w, v_conv_w)}
    outs, loss = _adamw_small(small_all, me, [[a.reshape(1, D) for a in replicated[n]] for n in list(replicated)[:5]],
                              [a.transpose(1, 0, 2) for a in replicated["conv_w"]])
    for (name, three), four in zip(replicated.items(), outs):
        res[name] = [o.transpose(1, 0, 2) if name == "conv_w" else o.reshape(three[0].shape) for o in four]
    loss = loss.reshape(())

    p_co, p_xo, p_o, p_pool, p_kv = take("mix", res["conv_w"][1])
    update("mix", ["w_conv_out", "w_xattn_out", "w_out", "w_pool", "w_kv"], [w_conv_out, w_xattn_out, w_out, w_pool, w_kv],
           [p_co, p_xo, p_o, p_pool, p_kv], [m_w_conv_out, m_w_xattn_out, m_w_out, m_w_pool, m_w_kv],
           [v_w_conv_out, v_w_xattn_out, v_w_out, v_w_pool, v_w_kv], True, 2)
    (p_in,) = take("in", res["w_out"][1])
    update("in", ["w_in"], [w_in], [p_in], [m_w_in], [v_w_in], True, 4)
    order = ["norm_mix", "w_in", "conv_w", "w_conv_out", "w_pool", "pool_scale", "norm_mem", "w_kv", "w_xattn_out", "w_out",
             "norm_ffn", "w_gate", "w_up", "w_down", "norm_final"]
    return (loss, grad_x[None], *[res[n][0] for n in order], *[res[n][1] for n in order],
            *[res[n][2] for n in order], *[res[n][3] for n in order])
```

```python
import jax
import jax.numpy as jnp
from jax import lax
from jax.experimental import pallas as pl
from jax.experimental.pallas import tpu as pltpu

F32 = jnp.float32
BF16 = jnp.bfloat16
SDS = jax.ShapeDtypeStruct

AXES = ("x", "y", "c")
NDEV = 8
D = 1024
NSPLIT = 8
NH = 4
HD = D // NH
NPOOL = 4
DFF = 2816
EPS = 1e-6
ATT_SCALE = HD ** -0.5
HALO = 16


def _slot_group(s):
    return jnp.where(s < 3, s + 5, jnp.where(s == 3, 4, s - 4))


SLOT_GROUPS = tuple(s + 5 if s < 3 else 4 if s == 3 else s - 4 for s in range(NSPLIT))


def _w_in_places():
    table = []
    for chip in range(4):
        source = [0 if g // 2 == chip else 2 if g // 2 == 3 - chip else 1 for g in SLOT_GROUPS]
        rows = [source]
        for k in range(3):
            blocks = [g if k == 1 else g % 2 for g in SLOT_GROUPS]
            held = [b for b, src in zip(blocks, source) if src == k][-1]
            rows.append([(held := b if src == k else held) for b, src in zip(blocks, source)])
        table.append([v for row in rows for v in row])
    return jnp.array(table, jnp.int32)


ADAM_LR = 0.001
ADAM_B1 = 0.9
ADAM_B2 = 0.999
ADAM_EPS = 1e-08
ADAM_WD = 0.01
ADAM_STEP = 10

V7X_VMEM_BYTES = 64 * 1024 * 1024
VMEM_LIMIT = V7X_VMEM_BYTES - 8 * 1024 * 1024
HBM = pl.BlockSpec(memory_space=pl.ANY)


def _tap(cw_ref, t):
    return jnp.concatenate([cw_ref[d, t:t + 1, :] for d in range(NDEV)], axis=1)


def _pool_map(wp_ref, g):
    rows = HD // NDEV
    return jnp.concatenate([wp_ref[d, g * rows:(g + 1) * rows, :] for d in range(NDEV)], axis=0)


def _whole(shape):
    return pl.BlockSpec(shape, lambda *_: (0,) * len(shape), pipeline_mode=pl.Buffered(1))


def _params(n_grid):
    return pltpu.CompilerParams(dimension_semantics=("arbitrary",) * n_grid, vmem_limit_bytes=VMEM_LIMIT)


def _mm(a, b):
    return jnp.dot(a, b, preferred_element_type=F32)


def _mm_nt(a, b):
    return lax.dot_general(a, b, (((1,), (1,)), ((), ())), preferred_element_type=F32)


def _mm_tn(a, b):
    return lax.dot_general(a, b, (((0,), (0,)), ((), ())), preferred_element_type=F32)


def _sigmoid(x):
    return 1.0 / (1.0 + jnp.exp(-x))


def _rms(x):
    return lax.rsqrt(jnp.mean(x * x, axis=-1, keepdims=True) + EPS)


def _norm_bwd(dh, x, gain):
    r = _rms(x)
    xh = x * r
    dxh = dh * gain
    dx = r * (dxh - xh * jnp.mean(dxh * xh, axis=-1, keepdims=True))
    return dx, jnp.sum(dh * xh, axis=0, keepdims=True)


def _col_chunks(n, width=512):
    return [slice(c, min(c + width, n)) for c in range(0, n, width)]


def _shift_down(v, k):
    return pltpu.roll(v, k, 0)


def _shift_up(v, k):
    return pltpu.roll(v, v.shape[0] - k, 0)


def _fwd_proj(x, gain, w_blocks, w_ids, p_ids, tm):
    T = x.shape[0]

    def body(w_ids_ref, p_ids_ref, x_ref, g_ref, w_ref, proj_ref, h_ref):
        del w_ids_ref, p_ids_ref

        @pl.when(pl.program_id(1) == 0)
        def _():
            xf = x_ref[...]
            h_ref[...] = (xf * _rms(xf) * g_ref[...]).astype(BF16)
        proj_ref[...] = _mm(h_ref[...], w_ref[...]).astype(BF16)

    return pl.pallas_call(
        body, name="fwd_proj",
        grid_spec=pltpu.PrefetchScalarGridSpec(
            num_scalar_prefetch=2, grid=(T // tm, w_ids.shape[0]),
            in_specs=[pl.BlockSpec((tm, D), lambda i, j, w, p: (i, 0)), pl.BlockSpec((1, D), lambda i, j, w, p: (0, 0)),
                      pl.BlockSpec((None, D, D), lambda i, j, w, p: (w[j], 0, 0))],
            out_specs=[pl.BlockSpec((None, tm, D), lambda i, j, w, p: (p[j], i, 0)),
                       pl.BlockSpec((tm, D), lambda i, j, w, p: (i, 0))]),
        out_shape=[SDS((NSPLIT, T, D), BF16), SDS((T, D), BF16)],
        compiler_params=_params(2))(w_ids, p_ids, x, gain, w_blocks)


def _fwd_proj_more(h, w_blocks, proj, w_ids, p_ids, tm, name):
    T = h.shape[0]

    def body(w_ids_ref, p_ids_ref, h_ref, w_ref, proj_hbm, proj_ref):
        del w_ids_ref, p_ids_ref, proj_hbm
        proj_ref[...] = _mm(h_ref[...], w_ref[...]).astype(BF16)

    return pl.pallas_call(
        body, name=name,
        grid_spec=pltpu.PrefetchScalarGridSpec(
            num_scalar_prefetch=2, grid=(T // tm, w_ids.shape[0]),
            in_specs=[pl.BlockSpec((tm, D), lambda i, j, w, p: (i, 0)),
                      pl.BlockSpec((None, D, D), lambda i, j, w, p: (w[j], 0, 0)), HBM],
            out_specs=pl.BlockSpec((None, tm, D), lambda i, j, w, p: (p[j], i, 0))),
        out_shape=SDS(proj.shape, BF16), input_output_aliases={4: 0},
        compiler_params=_params(2))(w_ids, p_ids, h, w_blocks, proj)


def _halo_before(split, tm):
    return pl.BlockSpec((None, HALO, D), lambda i: (split, jnp.maximum(i * (tm // HALO) - 1, 0), 0))


def _fwd_mix(proj, cw, w_co, w_pool, mem, gain_mem, w_kv, tm):
    T = proj.shape[1]
    M = mem.shape[0]

    def body(b_ref, c_ref, ua_ref, up_ref, ch_ref, uah_ref, uph_ref, cw_ref, wco_ref, wp_ref,
             mem_ref, gm_ref, wkv_ref, za_ref, conv_ref, pooled_ref, ya_ref, yp_ref, kv_ref, memn_ref):
        i = pl.program_id(0)

        @pl.when(i == 0)
        def _():
            m = mem_ref[...]
            memn = (m * _rms(m) * gm_ref[...]).astype(BF16)
            memn_ref[...] = memn
            for j in range(2 * NH):
                kv_ref[j] = _mm(memn, wkv_ref[j]).astype(BF16)
        keep = jnp.where(i > 0, 1.0, 0.0).astype(F32)
        cu = c_ref[...].astype(F32) * ua_ref[...].astype(F32)
        cu_h = ch_ref[...].astype(F32) * uah_ref[...].astype(F32) * keep
        ext = jnp.concatenate([cu_h, cu], axis=0)
        conv = (_tap(cw_ref, 2) * ext + _tap(cw_ref, 1) * _shift_down(ext, 1) + _tap(cw_ref, 0) * _shift_down(ext, 2))[HALO:]
        za = (b_ref[...].astype(F32) * conv).astype(BF16)
        conv_ref[...] = conv.astype(BF16)
        za_ref[...] = za
        ya_ref[...] = _mm(za, wco_ref[...]).astype(BF16)

        up = up_ref[...].astype(F32)
        ext_u = jnp.concatenate([uph_ref[...].astype(F32) * keep, up], axis=0)
        pos = i * tm + lax.broadcasted_iota(jnp.int32, (tm, HD), 0)
        for g in range(NPOOL):
            cols = slice(g * HD, (g + 1) * HD)
            s = ext_u[:, cols]
            for k in range(g + 1):
                s = s + _shift_down(s, 1 << k)
            cnt = jnp.minimum(pos + 1, 2 << g).astype(F32)
            pooled = (s[HALO:] / cnt - up[:, cols]).astype(BF16)
            pooled_ref[:, cols] = pooled
            yp_ref[:, cols] = _mm(pooled, _pool_map(wp_ref, g)).astype(BF16)

    tile = lambda s: pl.BlockSpec((None, tm, D), lambda i: (s, i, 0))
    row = pl.BlockSpec((1, D), lambda i: (0, 0))
    out = pl.BlockSpec((tm, D), lambda i: (i, 0))
    return pl.pallas_call(
        body, name="fwd_mix", grid=(T // tm,),
        in_specs=[tile(0), tile(1), tile(2), tile(3), _halo_before(1, tm), _halo_before(2, tm), _halo_before(3, tm),
                  _whole(cw.shape), _whole((D, D)), _whole(w_pool.shape), _whole((M, D)), row, _whole((2 * NH, D, HD))],
        out_specs=[out] * 5 + [pl.BlockSpec((2 * NH, M, HD), lambda i: (0, 0, 0)), pl.BlockSpec((M, D), lambda i: (0, 0))],
        out_shape=[SDS((T, D), BF16)] * 5 + [SDS((2 * NH, M, HD), BF16), SDS((M, D), BF16)],
        compiler_params=_params(1))(proj, proj, proj, proj, proj, proj, proj, cw, w_co, w_pool, mem, gain_mem, w_kv)


def _softmax_rows(s):
    e = jnp.exp(s - jnp.max(s, axis=-1, keepdims=True))
    return e / jnp.sum(e, axis=-1, keepdims=True)


def _fwd_merge(proj, ya, yp, x, kv, w_xo, w_o, pscale, gain_ffn, tm):
    T = x.shape[0]

    def body(q_ref, ga_ref, gp_ref, gx_ref, ya_ref, yp_ref, x_ref, kv_ref, wxo_ref, wo_ref, ps_ref, gf_ref,
             o_ref, yx_ref, merged_ref, x1_ref, h2_ref):
        for h in range(NH):
            cols = slice(h * HD, (h + 1) * HD)
            p = _softmax_rows(_mm_nt(q_ref[:, cols], kv_ref[h]) * ATT_SCALE)
            o_ref[:, cols] = _mm(p.astype(BF16), kv_ref[NH + h]).astype(BF16)
        yx = _mm(o_ref[...], wxo_ref[...])
        yx_ref[...] = yx.astype(BF16)
        merged = (_sigmoid(ga_ref[...].astype(F32)) * ya_ref[...].astype(F32)
                  + _sigmoid(gp_ref[...].astype(F32)) * (yp_ref[...].astype(F32) * ps_ref[...])
                  + _sigmoid(gx_ref[...].astype(F32)) * yx).astype(BF16)
        merged_ref[...] = merged
        x1 = x_ref[...] + _mm(merged, wo_ref[...])
        x1_ref[...] = x1
        h2_ref[...] = (x1 * _rms(x1) * gf_ref[...]).astype(BF16)

    tile = lambda s: pl.BlockSpec((None, tm, D), lambda i: (s, i, 0))
    row = pl.BlockSpec((1, D), lambda i: (0, 0))
    act = pl.BlockSpec((tm, D), lambda i: (i, 0))
    full = _whole((D, D))
    return pl.pallas_call(
        body, name="fwd_merge", grid=(T // tm,),
        in_specs=[tile(4), tile(5), tile(6), tile(7), act, act, act,
                  _whole((2 * NH, kv.shape[1], HD)), full, full, row, row],
        out_specs=[act] * 5,
        out_shape=[SDS((T, D), BF16), SDS((T, D), BF16), SDS((T, D), BF16), SDS((T, D), F32), SDS((T, D), BF16)],
        compiler_params=_params(1))(proj, proj, proj, proj, ya, yp, x, kv, w_xo, w_o, pscale, gain_ffn)


def _fwd_ffn_up(h2, wg_t, wu_t, tm, tn):
    T = h2.shape[0]

    def body(h_ref, wg_ref, wu_ref, gate_ref, up_ref, act_ref):
        for cols in _col_chunks(tn):
            gate = _mm_nt(h_ref[...], wg_ref[cols, :])
            up = _mm_nt(h_ref[...], wu_ref[cols, :])
            gate_ref[:, cols] = gate.astype(BF16)
            up_ref[:, cols] = up.astype(BF16)
            act_ref[:, cols] = (gate * _sigmoid(gate) * up).astype(BF16)

    w = pl.BlockSpec((tn, D), lambda n, i: (n, 0))
    o = pl.BlockSpec((tm, tn), lambda n, i: (i, n))
    return pl.pallas_call(
        body, name="fwd_ffn_up", grid=(DFF // tn, T // tm),
        in_specs=[pl.BlockSpec((tm, D), lambda n, i: (i, 0)), w, w],
        out_specs=[o] * 3, out_shape=[SDS((T, DFF), BF16)] * 3,
        compiler_params=_params(2))(h2, wg_t, wu_t)


def _fwd_ffn_down_loss(act, w_d, x1, target, gain_final, tm):
    T = x1.shape[0]

    def body(act_ref, wd_ref, x1_ref, tgt_ref, g_ref, dx2_ref, loss_ref, dgain_ref):
        @pl.when(pl.program_id(0) == 0)
        def _():
            loss_ref[...] = jnp.zeros_like(loss_ref)
            dgain_ref[...] = jnp.zeros_like(dgain_ref)
        x2 = x1_ref[...] + _mm(act_ref[...], wd_ref[...])
        gain = g_ref[...]
        y = x2 * _rms(x2) * gain
        err = y - tgt_ref[...]
        loss_ref[...] += 0.5 * jnp.sum(jnp.mean(err * err, axis=-1, keepdims=True))
        dx2, dgain = _norm_bwd(err * (1.0 / D), x2, gain)
        dx2_ref[...] = dx2
        dgain_ref[...] += dgain

    act_spec = pl.BlockSpec((tm, D), lambda i: (i, 0))
    row = pl.BlockSpec((1, D), lambda i: (0, 0))
    return pl.pallas_call(
        body, name="fwd_ffn_down_loss", grid=(T // tm,),
        in_specs=[pl.BlockSpec((tm, DFF), lambda i: (i, 0)), _whole((DFF, D)), act_spec, act_spec, row],
        out_specs=[act_spec, pl.BlockSpec((8, D), lambda i: (0, 0)), row],
        out_shape=[SDS((T, D), F32), SDS((8, D), F32), SDS((1, D), F32)],
        compiler_params=_params(1))(act, w_d, x1, target, gain_final)


def _bwd_ffn_down(dx2, w_d, gate, up, tm, tn):
    T = dx2.shape[0]

    def body(dx_ref, wd_ref, gate_ref, up_ref, dgate_ref, dup_ref):
        dx = dx_ref[...].astype(BF16)
        for cols in _col_chunks(tn):
            dact = _mm_nt(dx, wd_ref[cols, :])
            gate = gate_ref[:, cols].astype(F32)
            sg = _sigmoid(gate)
            dgate_ref[:, cols] = (dact * up_ref[:, cols].astype(F32) * (sg * (1.0 + gate * (1.0 - sg)))).astype(BF16)
            dup_ref[:, cols] = (dact * gate * sg).astype(BF16)

    o = pl.BlockSpec((tm, tn), lambda n, i: (i, n))
    return pl.pallas_call(
        body, name="bwd_ffn_down", grid=(DFF // tn, T // tm),
        in_specs=[pl.BlockSpec((tm, D), lambda n, i: (i, 0)), pl.BlockSpec((tn, D), lambda n, i: (n, 0)), o, o],
        out_specs=[o] * 2, out_shape=[SDS((T, DFF), BF16)] * 2,
        compiler_params=_params(2))(dx2, w_d, gate, up)


def _bwd_ffn_up(dgate, dup, wg_t, wu_t, x1, dx2, gain_ffn, tm):
    T = x1.shape[0]

    def body(dg_ref, du_ref, wg_ref, wu_ref, x1_ref, dx2_ref, g_ref, dx1_ref, dgain_ref):
        @pl.when(pl.program_id(0) == 0)
        def _():
            dgain_ref[...] = jnp.zeros_like(dgain_ref)
        dh2 = _mm(dg_ref[...], wg_ref[...]) + _mm(du_ref[...], wu_ref[...])
        dx, dgain = _norm_bwd(dh2, x1_ref[...], g_ref[...])
        dx1_ref[...] = dx2_ref[...] + dx
        dgain_ref[...] += dgain

    wide = pl.BlockSpec((tm, DFF), lambda i: (i, 0))
    w = _whole((DFF, D))
    act = pl.BlockSpec((tm, D), lambda i: (i, 0))
    row = pl.BlockSpec((1, D), lambda i: (0, 0))
    return pl.pallas_call(
        body, name="bwd_ffn_up", grid=(T // tm,),
        in_specs=[wide, wide, w, w, act, act, row], out_specs=[act, row],
        out_shape=[SDS((T, D), F32), SDS((1, D), F32)],
        compiler_params=_params(1))(dgate, dup, wg_t, wu_t, x1, dx2, gain_ffn)


def _wgrad(a, b, *, name, groups, a_cols, b_cols, tt, a_index, b_index, o_index, out_shape, after):
    T = a.shape[0]
    nt = T // tt
    n_a = a.shape[1] // a_cols if groups == 1 else 1

    def body(a_ref, b_ref, after_ref, o_ref, acc_ref):
        del after_ref
        t = pl.program_id(2)

        @pl.when(t == 0)
        def _():
            acc_ref[...] = jnp.zeros_like(acc_ref)
        acc_ref[...] += _mm_tn(a_ref[...].astype(BF16), b_ref[...].astype(BF16))

        @pl.when(t == nt - 1)
        def _():
            o_ref[...] = acc_ref[...].astype(o_ref.dtype)

    return pl.pallas_call(
        body, name=name, grid=(groups, n_a, nt),
        in_specs=[pl.BlockSpec((tt, a_cols), a_index), pl.BlockSpec((None, tt, b_cols), b_index), HBM],
        out_specs=pl.BlockSpec((None, a_cols, b_cols), o_index),
        out_shape=SDS(out_shape, BF16),
        scratch_shapes=[pltpu.VMEM((a_cols, b_cols), F32)],
        compiler_params=_params(3))(a, b, after)


def _wgrad_dense(a, b, name, tt, after, a_cols=None):
    ka, nb = a.shape[1], b.shape[1]
    a_cols = ka if a_cols is None else a_cols
    out = _wgrad(a, b[None], name=name, groups=1, a_cols=a_cols, b_cols=nb, tt=tt,
                 a_index=lambda g, k, t: (t, k), b_index=lambda g, k, t: (0, t, 0),
                 o_index=lambda g, k, t: (k, 0, 0), out_shape=(ka // a_cols, a_cols, nb), after=after)
    return out.reshape(ka, nb)


def _bwd_merge(dx1, proj, ya, yp, yx, pooled, pscale, w_o, w_co, w_xo, w_pool, tm, after):
    T = dx1.shape[0]
    nt = T // tm

    def body(dx1_ref, ga_ref, gp_ref, gx_ref, ya_ref, yp_ref, yx_ref, pooled_ref, ps_ref, wo_ref, wco_ref, wxo_ref, wp_ref,
             after_ref, dgates_ref, dya_ref, dyx_ref, dza_ref, do_ref, dpooled_ref, dps_ref, dwp_ref, acc_ref):
        del after_ref

        @pl.when(pl.program_id(0) == 0)
        def _():
            dps_ref[...] = jnp.zeros_like(dps_ref)
            acc_ref[...] = jnp.zeros_like(acc_ref)
        dmerged = _mm_nt(dx1_ref[...].astype(BF16), wo_ref[...])
        scale = ps_ref[...]
        sa, sp, sx = (_sigmoid(r[...].astype(F32)) for r in (ga_ref, gp_ref, gx_ref))
        ya, yp_pre, yx = (r[...].astype(F32) for r in (ya_ref, yp_ref, yx_ref))
        dgates_ref[0] = (dmerged * ya * sa * (1.0 - sa)).astype(BF16)
        dgates_ref[1] = (dmerged * (yp_pre * scale) * sp * (1.0 - sp)).astype(BF16)
        dgates_ref[2] = (dmerged * yx * sx * (1.0 - sx)).astype(BF16)
        dya = (dmerged * sa).astype(BF16)
        dyx = (dmerged * sx).astype(BF16)
        dyp = dmerged * sp
        dyps = (dyp * scale).astype(BF16)
        dps_ref[...] += jnp.sum(dyp * yp_pre, axis=0, keepdims=True)
        dya_ref[...] = dya
        dyx_ref[...] = dyx
        dza_ref[...] = _mm_nt(dya, wco_ref[...]).astype(BF16)
        do_ref[...] = _mm_nt(dyx, wxo_ref[...]).astype(BF16)
        for g in range(NPOOL):
            cols = slice(g * HD, (g + 1) * HD)
            dpooled_ref[:, cols] = _mm_nt(dyps[:, cols], _pool_map(wp_ref, g)).astype(BF16)
            acc_ref[g] += _mm_tn(pooled_ref[:, cols], dyps[:, cols])

        @pl.when(pl.program_id(0) == nt - 1)
        def _():
            dwp_ref[...] = acc_ref[...].astype(BF16)

    tile = lambda s: pl.BlockSpec((None, tm, D), lambda i: (s, i, 0))
    row = pl.BlockSpec((1, D), lambda i: (0, 0))
    act = pl.BlockSpec((tm, D), lambda i: (i, 0))
    full = _whole((D, D))
    return pl.pallas_call(
        body, name="bwd_merge", grid=(T // tm,),
        in_specs=[act, tile(5), tile(6), tile(7), act, act, act, act, row, full, full, full,
                  _whole(w_pool.shape), HBM],
        out_specs=[pl.BlockSpec((3, tm, D), lambda i: (0, i, 0))] + [act] * 5
        + [row, pl.BlockSpec((NPOOL, HD, HD), lambda i: (0, 0, 0))],
        out_shape=[SDS((NSPLIT, T, D), BF16)] + [SDS((T, D), BF16)] * 5 + [SDS((1, D), F32), SDS((NPOOL, HD, HD), BF16)],
        scratch_shapes=[pltpu.VMEM((NPOOL, HD, HD), F32)],
        compiler_params=_params(1))(dx1, proj, proj, proj, ya, yp, yx, pooled, pscale, w_o, w_co, w_xo, w_pool, after)


def _bwd_attn(dproj, proj, do, kv, memn, w_kv, mem, gain_mem, tm):
    T = do.shape[0]
    M = kv.shape[1]
    nt = T // tm

    def body(dproj_hbm, q_ref, do_ref, kv_ref, memn_ref, wkv_ref, mem_ref, gm_ref, dq_ref, dw_ref, dgain_ref, dkv_ref):
        del dproj_hbm

        @pl.when(pl.program_id(0) == 0)
        def _():
            dkv_ref[...] = jnp.zeros_like(dkv_ref)
        for h in range(NH):
            cols = slice(h * HD, (h + 1) * HD)
            q = q_ref[:, cols]
            do_h = do_ref[:, cols]
            p = _softmax_rows(_mm_nt(q, kv_ref[h]) * ATT_SCALE)
            dp = _mm_nt(do_h, kv_ref[NH + h])
            ds = (p * (dp - jnp.sum(dp * p, axis=-1, keepdims=True)) * ATT_SCALE).astype(BF16)
            dq_ref[:, cols] = _mm(ds, kv_ref[h]).astype(BF16)
            dkv_ref[h] += _mm_tn(ds, q)
            dkv_ref[NH + h] += _mm_tn(p.astype(BF16), do_h)

        @pl.when(pl.program_id(0) == nt - 1)
        def _():
            dmemn = jnp.zeros((M, D), F32)
            for j in range(2 * NH):
                dkv_j = dkv_ref[j].astype(BF16)
                dw_ref[j] = _mm_tn(memn_ref[...], dkv_j).astype(BF16)
                dmemn = dmemn + _mm_nt(dkv_j, wkv_ref[j])
            dgain_ref[...] = _norm_bwd(dmemn, mem_ref[...], gm_ref[...])[1]

    row = pl.BlockSpec((1, D), lambda i: (0, 0))
    return pl.pallas_call(
        body, name="bwd_attn", grid=(nt,),
        in_specs=[HBM, pl.BlockSpec((None, tm, D), lambda i: (4, i, 0)), pl.BlockSpec((tm, D), lambda i: (i, 0)),
                  _whole((2 * NH, M, HD)), _whole((M, D)), _whole((2 * NH, D, HD)), _whole((M, D)), row],
        out_specs=[pl.BlockSpec((None, tm, D), lambda i: (3, i, 0)),
                   pl.BlockSpec((2 * NH, D, HD), lambda i: (0, 0, 0)), row],
        out_shape=[SDS(dproj.shape, BF16), SDS((2 * NH, D, HD), BF16), SDS((1, D), F32)],
        scratch_shapes=[pltpu.VMEM((2 * NH, M, HD), F32)],
        input_output_aliases={0: 0},
        compiler_params=_params(1))(dproj, proj, do, kv, memn, w_kv, mem, gain_mem)


def _bwd_mix(dproj, proj, conv, dza, dpooled, cw, tm, after):
    T = dza.shape[0]
    nt = T // tm

    def halo_after(split_or_none):
        idx = lambda i: jnp.minimum((i + 1) * (tm // HALO), T // HALO - 1)
        if split_or_none is None:
            return pl.BlockSpec((HALO, D), lambda i: (idx(i), 0))
        return pl.BlockSpec((None, HALO, D), lambda i: (split_or_none, idx(i), 0))

    def body(dproj_hbm, b_ref, c_ref, ua_ref, conv_ref, dza_ref, dpo_ref, bn_ref, dzan_ref, dpon_ref,
             cw_ref, after_ref, dabcu_ref, dcw_ref):
        del dproj_hbm, after_ref
        i = pl.program_id(0)

        @pl.when(i == 0)
        def _():
            dcw_ref[...] = jnp.zeros_like(dcw_ref)
        keep_next = jnp.where(i < nt - 1, 1.0, 0.0).astype(F32)
        dza = dza_ref[...].astype(F32)
        c = c_ref[...].astype(F32)
        ua = ua_ref[...].astype(F32)
        dconv = dza * b_ref[...].astype(F32)
        dconv_n = dzan_ref[...].astype(F32) * bn_ref[...].astype(F32) * keep_next
        ext = jnp.concatenate([dconv, dconv_n], axis=0)
        dconv_1, dconv_2 = _shift_up(ext, 1)[:tm], _shift_up(ext, 2)[:tm]
        dcu = _tap(cw_ref, 2) * dconv + _tap(cw_ref, 1) * dconv_1 + _tap(cw_ref, 0) * dconv_2
        dabcu_ref[0] = (dza * conv_ref[...].astype(F32)).astype(BF16)
        dabcu_ref[1] = (dcu * ua).astype(BF16)
        dabcu_ref[2] = (dcu * c).astype(BF16)

        cu = c * ua
        dcw_ref[2:3, :] += jnp.sum(dconv * cu, axis=0, keepdims=True)
        dcw_ref[1:2, :] += jnp.sum(dconv_1 * cu, axis=0, keepdims=True)
        dcw_ref[0:1, :] += jnp.sum(dconv_2 * cu, axis=0, keepdims=True)

        dpo = dpo_ref[...].astype(F32)
        ext_dpo = jnp.concatenate([dpo, dpon_ref[...].astype(F32) * keep_next], axis=0)
        pos = i * tm + lax.broadcasted_iota(jnp.int32, (tm + HALO, HD), 0)
        for g in range(NPOOL):
            cols = slice(g * HD, (g + 1) * HD)
            s = ext_dpo[:, cols] / jnp.minimum(pos + 1, 2 << g).astype(F32)
            for k in range(g + 1):
                s = s + _shift_up(s, 1 << k)
            dabcu_ref[3, :, cols] = (s[:tm] - dpo[:, cols]).astype(BF16)

    tile = lambda s: pl.BlockSpec((None, tm, D), lambda i: (s, i, 0))
    act = pl.BlockSpec((tm, D), lambda i: (i, 0))
    return pl.pallas_call(
        body, name="bwd_mix", grid=(nt,),
        in_specs=[HBM, tile(0), tile(1), tile(2), act, act, act, halo_after(0), halo_after(None), halo_after(None),
                  _whole(cw.shape), HBM],
        out_specs=[pl.BlockSpec((4, tm, D), lambda i: (1, i, 0)), pl.BlockSpec((8, D), lambda i: (0, 0))],
        out_shape=[SDS(dproj.shape, BF16), SDS((8, D), F32)],
        input_output_aliases={0: 0},
        compiler_params=_params(1))(dproj, proj, proj, proj, conv, dza, dpooled, proj, dza, dpooled, cw, after)


def _bwd_proj(dproj, w_parts, places, x, dx1, gain, tm, after):
    T = x.shape[0]

    def body(places_ref, dp_ref, w0_ref, w1_ref, w2_ref, x_ref, dx1_ref, g_ref, after_ref, dx_ref, dgain_ref, acc_ref):
        del after_ref
        i, s = pl.program_id(0), pl.program_id(1)

        @pl.when((i == 0) & (s == 0))
        def _():
            dgain_ref[...] = jnp.zeros_like(dgain_ref)

        @pl.when(s == 0)
        def _():
            acc_ref[...] = jnp.zeros_like(acc_ref)

        for k, w_ref in enumerate((w0_ref, w1_ref, w2_ref)):
            @pl.when(places_ref[s] == k)
            def _(w_ref=w_ref):
                acc_ref[...] += _mm_nt(dp_ref[...], w_ref[...])

        @pl.when(s == NSPLIT - 1)
        def _():
            dx, dgain = _norm_bwd(acc_ref[...], x_ref[...], g_ref[...])
            dx_ref[...] = dx1_ref[...] + dx
            dgain_ref[...] += dgain

    act = pl.BlockSpec((tm, D), lambda i, s, p: (i, 0))
    row = pl.BlockSpec((1, D), lambda i, s, p: (0, 0))
    weight = lambda k: pl.BlockSpec((None, D, D), lambda i, s, p: (p[NSPLIT * (1 + k) + s], 0, 0))
    return pl.pallas_call(
        body, name="bwd_proj",
        grid_spec=pltpu.PrefetchScalarGridSpec(
            num_scalar_prefetch=1, grid=(T // tm, NSPLIT),
            in_specs=[pl.BlockSpec((None, tm, D), lambda i, s, p: (s, i, 0)), weight(0), weight(1), weight(2),
                      act, act, row, HBM],
            out_specs=[act, row], scratch_shapes=[pltpu.VMEM((tm, D), F32)]),
        out_shape=[SDS((T, D), F32), SDS((1, D), F32)],
        compiler_params=_params(2))(places, dproj, *w_parts, x, dx1, gain, after)


def _adamw_math(w, g, m, v):
    m = ADAM_B1 * m + (1.0 - ADAM_B1) * g
    v = ADAM_B2 * v + (1.0 - ADAM_B2) * (g * g)
    m_hat = m / (1.0 - ADAM_B1 ** ADAM_STEP)
    v_hat = v / (1.0 - ADAM_B2 ** ADAM_STEP)
    delta = -ADAM_LR * (m_hat / (jnp.sqrt(v_hat) + ADAM_EPS) + ADAM_WD * w)
    return delta, m, v


def _adamw_small(parts, me, gains, taps):
    n = len(gains)
    cols = D // NDEV

    def body(me_ref, all_ref, mine_ref, *refs):
        del me_ref
        everywhere, here = all_ref[0], mine_ref[0]
        for k in range(1, NDEV):
            everywhere, here = everywhere + all_ref[k], here + mine_ref[k]
        ins, outs = refs[:3 * (n + 1)], refs[3 * (n + 1):]
        for a in range(n):
            w_ref, m_ref, v_ref = ins[3 * a:3 * a + 3]
            go_ref, d_ref, mo_ref, vo_ref = outs[4 * a:4 * a + 4]
            g = everywhere[a:a + 1]
            go_ref[...] = g
            d_ref[...], mo_ref[...], vo_ref[...] = _adamw_math(w_ref[...], g, m_ref[...], v_ref[...])
        (w_ref, m_ref, v_ref), (go_ref, d_ref, mo_ref, vo_ref) = ins[3 * n:], outs[4 * n:4 * n + 4]
        for t in range(3):
            g = here[n + t:n + t + 1]
            go_ref[t] = g
            d_ref[t], mo_ref[t], vo_ref[t] = _adamw_math(w_ref[t], g, m_ref[t], v_ref[t])
        outs[-1][...] = everywhere[8:9, 0:1]

    whole = lambda shape: pl.BlockSpec(shape, lambda i, me_ref: (0,) * len(shape))
    shapes = [(1, D)] * n + [(3, 1, cols)]
    outs = pl.pallas_call(
        body, name="adamw_replicated",
        grid_spec=pltpu.PrefetchScalarGridSpec(
            num_scalar_prefetch=1, grid=(1,),
            in_specs=[whole(parts.shape), pl.BlockSpec((NDEV, 16, cols), lambda i, me_ref: (0, 0, me_ref[0]))]
            + [whole(s) for s in shapes for _ in range(3)],
            out_specs=[whole(s) for s in shapes for _ in range(4)] + [whole((1, 1))]),
        out_shape=[SDS(s, F32) for s in shapes for _ in range(4)] + [SDS((1, 1), F32)],
        compiler_params=_params(1))(me.reshape(1).astype(jnp.int32), parts, parts, *[a for three in gains + [taps] for a in three])
    return [outs[4 * a:4 * a + 4] for a in range(n + 1)], outs[-1]


def _adamw(ws, gs, ms, vs, name, from_parts, steps):
    n = len(ws)

    def body(*refs):
        for a in range(n):
            w_ref, g_ref, m_ref, v_ref = refs[4 * a:4 * a + 4]
            go_ref, d_ref, mo_ref, vo_ref = refs[4 * n + 4 * a:4 * n + 4 * a + 4]
            if from_parts:
                g = g_ref[0].astype(F32)
                for k in range(1, g_ref.shape[0]):
                    g = g + g_ref[k].astype(F32)
            else:
                g = g_ref[...]
            go_ref[...] = g
            d_ref[...], mo_ref[...], vo_ref[...] = _adamw_math(w_ref[...], g, m_ref[...], v_ref[...])

    in_specs, out_specs, out_shape, operands = [], [], [], []
    for w, g, m, v in zip(ws, gs, ms, vs):
        rows, cols = w.shape
        blk = pl.BlockSpec((rows // steps, cols), lambda i: (i, 0))
        g_spec = pl.BlockSpec((g.shape[0], rows // steps, cols), lambda i: (0, i, 0)) if from_parts else blk
        in_specs += [blk, g_spec, blk, blk]
        out_specs += [blk] * 4
        out_shape += [SDS((rows, cols), F32)] * 4
        operands += [w, g, m, v]
    outs = pl.pallas_call(body, name=name, grid=(steps,), in_specs=in_specs, out_specs=out_specs, out_shape=out_shape,
                          compiler_params=_params(1))(*operands)
    return [outs[4 * a:4 * a + 4] for a in range(n)]


def _peer(k, x, y, c):
    return ((1 - x) if k & 4 else x, (1 - y) if k & 2 else y, (1 - c) if k & 1 else c)


SEM = pl.BlockSpec(memory_space=pltpu.SEMAPHORE)
IN_HBM = pl.BlockSpec(memory_space=pltpu.HBM)
DATAFLOW = pltpu.SideEffectType.DATAFLOW_SIDE_EFFECTING
TOKEN_SHAPE = (8, 128)


OTHER_CHIPS = (2, 4, 6)


def _place(x, y, c):
    return 4 * x + 2 * y + c


def _plan_gather_chips(n, ks=(1,) + OTHER_CHIPS):
    def plan(refs, x, y, c, arriving):
        out = []
        for a in range(n):
            for k in ks:
                there = _place(*_peer(k, x, y, c))
                out.append((refs[a].at[_place(x, y, c)], refs[a].at[there if arriving else _place(x, y, c)], k))
        return out
    return plan, n * len(ks)


def _plan_gather_sibling(n, ks=OTHER_CHIPS):
    def plan(refs, x, y, c, arriving):
        out = []
        for a in range(n):
            for k in ks:
                px, py, pc = _peer(k, x, y, c)
                mine, theirs = _place(px, py, pc), _place(px, py, 1 - pc)
                out.append((refs[a].at[mine], refs[a].at[theirs if arriving else mine], 1))
        return out
    return plan, n * len(ks)


def _plan_pair():
    def plan(refs, x, y, c, arriving):
        return [(refs[0].at[c], refs[0].at[(1 - c) if arriving else c], 1)]
    return plan, 1


def _plan_far_chip():
    def plan(refs, x, y, c, arriving):
        return [(refs[0].at[c], refs[1].at[c], 6)]
    return plan, 1


def _plan_far_sibling():
    def plan(refs, x, y, c, arriving):
        return [(refs[0].at[c], refs[0].at[(1 - c) if arriving else c], 1)]
    return plan, 1


def _plan_scatter_sibling(n):
    def plan(refs, x, y, c, arriving):
        out = []
        for a in range(n):
            for q in range(4):
                out.append((refs[a].at[2 * q + (1 - c)], refs[n + a].at[q], 1))
        return out
    return plan, n * 4


def _plan_scatter_chips(n):
    def plan(refs, x, y, c, arriving):
        out = []
        for a in range(n):
            for k in OTHER_CHIPS:
                px, py, _ = _peer(k, x, y, c)
                out.append((refs[a].at[2 * px + py], refs[n + a].at[(2 * px + py) if arriving else (2 * x + y)], k))
        return out
    return plan, n * 3


def _remote(src, dst, send_sems, recv_sems, i, k):
    x, y, c = (lax.axis_index(n) for n in AXES)
    return pltpu.make_async_remote_copy(src_ref=src, dst_ref=dst, send_sem=send_sems.at[i], recv_sem=recv_sems.at[i],
                                        device_id=_peer(k, x, y, c), device_id_type=pl.DeviceIdType.MESH)


def _copies_start(groups, name, after):
    ng = len(groups)
    total = sum(len(bufs) for bufs, _ in groups)

    def body(*refs):
        sems = refs[1 + total:1 + total + 2 * ng]
        x, y, c = (lax.axis_index(n) for n in AXES)
        off = 1
        for gi, (bufs, (plan, _)) in enumerate(groups):
            for i, (src, dst, k) in enumerate(plan(refs[off:off + len(bufs)], x, y, c, False)):
                _remote(src, dst, sems[2 * gi], sems[2 * gi + 1], i, k).start()
            off += len(bufs)
        refs[-1][...] = jnp.zeros(TOKEN_SHAPE, F32)

    sem_shapes = [pltpu.SemaphoreType.DMA((count,)) for _, (_, count) in groups for _ in range(2)]
    flat = [b for bufs, _ in groups for b in bufs]
    outs = pl.pallas_call(
        body, name=name,
        in_specs=[HBM] + [IN_HBM] * total,
        out_specs=[SEM] * (2 * ng) + [IN_HBM] * total + [pl.BlockSpec(memory_space=pltpu.VMEM)],
        out_shape=sem_shapes + [pltpu.HBM(b.shape, b.dtype) for b in flat] + [SDS(TOKEN_SHAPE, F32)],
        input_output_aliases={1 + i: 2 * ng + i for i in range(total)},
        compiler_params=pltpu.CompilerParams(has_side_effects=DATAFLOW),
    )(after, *[pltpu.with_memory_space_constraint(b, pltpu.HBM) for b in flat])
    handles, off = [], 2 * ng
    for gi, (bufs, _) in enumerate(groups):
        handles.append((outs[2 * gi], outs[2 * gi + 1], list(outs[off:off + len(bufs)])))
        off += len(bufs)
    return handles, outs[-1]


def _copies_wait_start(handle, plan, pass_on, more, name, after):
    send_sems, recv_sems, bufs = handle
    n = len(bufs)
    idx, (pass_plan, pass_count) = pass_on
    total = sum(len(b) for b, _ in more)
    ng = 1 + len(more)

    def body(*refs):
        x, y, c = (lax.axis_index(a) for a in AXES)
        waited = refs[1:1 + n]
        outs = refs[3 + n + total:]
        new_sems = outs[n + total:n + total + 2 * ng]
        for i, (src, dst, k) in enumerate(plan[0](waited, x, y, c, True)):
            copy = _remote(src, dst, refs[1 + n + total], refs[2 + n + total], i, k)
            copy.wait_send()
            copy.wait_recv()
        for i, (src, dst, k) in enumerate(pass_plan([waited[j] for j in idx], x, y, c, False)):
            _remote(src, dst, new_sems[0], new_sems[1], i, k).start()
        off = 1 + n
        for gi, (b, (p, _)) in enumerate(more):
            for i, (src, dst, k) in enumerate(p(refs[off:off + len(b)], x, y, c, False)):
                _remote(src, dst, new_sems[2 + 2 * gi], new_sems[3 + 2 * gi], i, k).start()
            off += len(b)
        outs[-1][...] = jnp.zeros(TOKEN_SHAPE, F32)

    flat = list(bufs) + [a for b, _ in more for a in b]
    sem_shapes = [pltpu.SemaphoreType.DMA((count,)) for count in [pass_count] + [cnt for _, (_, cnt) in more] for _ in range(2)]
    outs = pl.pallas_call(
        body, name=name,
        in_specs=[HBM] + [IN_HBM] * (n + total) + [SEM, SEM],
        out_specs=[IN_HBM] * (n + total) + [SEM] * (2 * ng) + [pl.BlockSpec(memory_space=pltpu.VMEM)],
        out_shape=[pltpu.HBM(b.shape, b.dtype) for b in flat] + sem_shapes + [SDS(TOKEN_SHAPE, F32)],
        input_output_aliases={1 + i: i for i in range(n + total)},
        compiler_params=pltpu.CompilerParams(has_side_effects=DATAFLOW),
    )(after, *[pltpu.with_memory_space_constraint(b, pltpu.HBM) for b in flat], send_sems, recv_sems)
    thru = list(outs[:n])
    sems_out = outs[n + total:n + total + 2 * ng]
    handles = [(sems_out[0], sems_out[1], [thru[j] for j in idx])]
    off = n
    for gi, (b, _) in enumerate(more):
        handles.append((sems_out[2 + 2 * gi], sems_out[3 + 2 * gi], list(outs[off:off + len(b)])))
        off += len(b)
    return thru, handles, outs[-1]


def _copies_wait(handle, plan, name, *after):
    send_sems, recv_sems, bufs = handle
    n = len(bufs)

    def body(*refs):
        x, y, c = (lax.axis_index(a) for a in AXES)
        for i, (src, dst, k) in enumerate(plan[0](refs[:n], x, y, c, True)):
            copy = _remote(src, dst, refs[n], refs[n + 1], i, k)
            copy.wait_send()
            copy.wait_recv()

    return pl.pallas_call(
        body, name=name,
        in_specs=[IN_HBM] * n + [SEM, SEM] + [HBM] * len(after), out_specs=[IN_HBM] * n,
        out_shape=[pltpu.HBM(b.shape, b.dtype) for b in bufs],
        input_output_aliases={i: i for i in range(n)},
        compiler_params=pltpu.CompilerParams(has_side_effects=DATAFLOW),
    )(*bufs, send_sems, recv_sems, *after)


def _pair_sums(mine, theirs, c, chip, name):
    n = len(mine)

    def body(where_ref, *refs):
        q = pl.program_id(0)
        for a in range(n):
            total = (refs[a][...].astype(F32) + refs[n + a][...].astype(F32)).astype(BF16)
            refs[2 * n + a][...] = total

            @pl.when(q == where_ref[1])
            def _():
                refs[3 * n + a][...] = total

    block = lambda t: (None,) + t.shape[1:]
    zeros = lambda t: (0,) * (t.ndim - 1)
    outs = pl.pallas_call(
        body, name=name,
        grid_spec=pltpu.PrefetchScalarGridSpec(
            num_scalar_prefetch=1, grid=(4,),
            in_specs=[pl.BlockSpec(block(t), lambda q, w, z=zeros(t): (2 * q + w[0],) + z) for t in theirs]
            + [pl.BlockSpec(block(t), lambda q, w, z=zeros(t): (q,) + z) for t in theirs],
            out_specs=[pl.BlockSpec(block(t), lambda q, w, z=zeros(t): (q,) + z) for t in theirs]
            + [pl.BlockSpec(block(t), lambda q, w, z=zeros(t): (w[1],) + z) for t in theirs]),
        out_shape=[SDS(t.shape, BF16) for t in theirs] * 2,
        compiler_params=_params(1))(jnp.stack([c, chip]).astype(jnp.int32), *mine, *theirs)
    return list(outs[:n]), list(outs[n:])


def _wgrad_in_swap(h, dproj, order, tt, after):
    T = h.shape[0]
    nt = T // tt
    swapped = NSPLIT // 2

    def body(order_ref, a_ref, b_ref, after_ref, sum_ref, own_ref, land_ref, acc_ref, sent_ref, theirs_ref,
             send_sems, recv_sems, fetch_sem):
        del after_ref
        g, t = pl.program_id(0), pl.program_id(1)

        @pl.when(t == 0)
        def _():
            acc_ref[...] = jnp.zeros_like(acc_ref)
        acc_ref[...] += _mm_tn(a_ref[...], b_ref[...])

        for i in range(swapped):
            copy = _remote(sent_ref.at[i], land_ref.at[i], send_sems, recv_sems, i, 1)
            fetch = pltpu.make_async_copy(land_ref.at[i], theirs_ref, fetch_sem.at[0])

            @pl.when((g == i) & (t == nt - 1))
            def _(i=i, copy=copy):
                sent_ref[i] = acc_ref[...].astype(BF16)
                copy.start()

            @pl.when((g == swapped + i) & (t == 0))
            def _(copy=copy, fetch=fetch):
                copy.wait_recv()
                fetch.start()

            @pl.when((g == swapped + i) & (t == nt - 1))
            def _(i=i, fetch=fetch):
                fetch.wait()
                total = (acc_ref[...] + theirs_ref[...].astype(F32)).astype(BF16)
                sum_ref[...] = total

                @pl.when(order_ref[NSPLIT] == i)
                def _():
                    own_ref[...] = total

            @pl.when((g == NSPLIT - 1) & (t == nt - 1))
            def _(copy=copy):
                copy.wait_send()

    block = lambda index: pl.BlockSpec((None, D, D), index)
    return pl.pallas_call(
        body, name="wgrad_in",
        grid_spec=pltpu.PrefetchScalarGridSpec(
            num_scalar_prefetch=1, grid=(NSPLIT, nt),
            in_specs=[pl.BlockSpec((tt, D), lambda g, t, order: (t, 0)),
                      pl.BlockSpec((None, tt, D), lambda g, t, order: (order[g], t, 0)), HBM],
            out_specs=[block(lambda g, t, order: (jnp.maximum(g - swapped, 0), 0, 0)),
                       block(lambda g, t, order: (order[NSPLIT], 0, 0)), HBM],
            scratch_shapes=[pltpu.VMEM((D, D), F32), pltpu.VMEM((swapped, D, D), BF16), pltpu.VMEM((D, D), BF16),
                            pltpu.SemaphoreType.DMA((swapped,)), pltpu.SemaphoreType.DMA((swapped,)),
                            pltpu.SemaphoreType.DMA((1,))]),
        out_shape=[SDS((swapped, D, D), BF16)] * 3,
        compiler_params=_params(2))(order, h, dproj, after)[:2]


def _local_step(x, mem, target, gains, get, put, flush, tm_huge=2048, tm_big=1024, tm_mid=512, tm_small=256):
    g_mix, pscale, g_mem, g_ffn, g_fin = gains
    T = x.shape[0]
    tm_huge, tm_big, tm_mid, tm_small = min(tm_huge, T), min(tm_big, T), min(tm_mid, T), min(tm_small, T)
    tn = DFF // 2

    w_pair, w_ids, p_ids = get("in_pair", x)
    proj, h = _fwd_proj(x, g_mix, w_pair, w_ids, p_ids, tm_huge)
    w_near, w_ids, p_ids = get("in_near", h)
    proj = _fwd_proj_more(h, w_near, proj, w_ids, p_ids, tm_huge, "fwd_proj_near")
    w_far, w_ids, p_ids = get("in_far", proj)
    proj = _fwd_proj_more(h, w_far, proj, w_ids, p_ids, tm_huge, "fwd_proj_far")
    cw, w_co, w_pool, w_kv = get("mix", proj)
    za, conv, pooled, ya, yp, kv, memn = _fwd_mix(proj, cw, w_co, w_pool, mem, g_mem, w_kv, tm_mid)
    w_xo, w_o = get("merge", ya)
    o, yx, merged, x1, h2 = _fwd_merge(proj, ya, yp, x, kv, w_xo, w_o, pscale, g_ffn, tm_mid)
    wg_t, wu_t = get("gate_up", x1)
    get("down", x1, early=True)
    gate, up, act = _fwd_ffn_up(h2, wg_t, wu_t, tm_big, tn)
    (w_d,) = get("down", gate)
    dx2, loss, dg_fin = _fwd_ffn_down_loss(act, w_d, x1, target, g_fin, tm_mid)

    dgate, dup = _bwd_ffn_down(dx2, w_d, gate, up, tm_big, tn)
    dx1, dg_ffn = _bwd_ffn_up(dgate, dup, wg_t, wu_t, x1, dx2, g_ffn, tm_small)
    dw_d = _wgrad_dense(act, dx2, "wgrad_down", tm_big, g_mix)
    dwg_t = _wgrad_dense(dgate, h2, "wgrad_gate", tm_big, g_mix)
    dwu_t = _wgrad_dense(dup, h2, "wgrad_up", tm_big, g_mix)
    token = put("ffn", (dwg_t, dwu_t, dw_d))

    dproj, dya, dyx, dza, do, dpooled, dpscale, dw_pool = _bwd_merge(
        dx1, proj, ya, yp, yx, pooled, pscale, w_o, w_co, w_xo, w_pool, tm_mid, token)
    token = flush(dya)
    dw_o = _wgrad_dense(merged, dx1, "wgrad_out", tm_big, token)
    dw_co = _wgrad_dense(za, dya, "wgrad_conv_out", tm_big, token)
    dw_xo = _wgrad_dense(o, dyx, "wgrad_xattn_out", tm_big, token)
    dproj, dw_kv, dg_mem = _bwd_attn(dproj, proj, do, kv, memn, w_kv, mem, g_mem, tm_big)
    token = put("mix", (dw_co, dw_xo, dw_o, dw_pool, dw_kv))

    dproj, dcw = _bwd_mix(dproj, proj, conv, dza, dpooled, cw, tm_mid, token)
    token = flush(dcw)
    token = flush(put("in", (h, dproj, tm_huge, token)))
    w_pair, places = get("in_places", None)
    grad_x, dg_mix = _bwd_proj(dproj, (w_pair, w_near, w_far), places, x, dx1, g_mix, tm_big, token)

    small = jnp.concatenate([dg_mix, dpscale, dg_mem, dg_ffn, dg_fin, dcw[0:3], loss], axis=0)
    return grad_x, small


def kernel(x, mem, norm_mix, w_in, conv_w, w_conv_out, w_pool, pool_scale, norm_mem, w_kv, w_xattn_out, w_out, norm_ffn, w_gate, w_up, w_down, norm_final, loss_target, m_norm_mix, m_w_in, m_conv_w, m_w_conv_out, m_w_pool, m_pool_scale, m_norm_mem, m_w_kv, m_w_xattn_out, m_w_out, m_norm_ffn, m_w_gate, m_w_up, m_w_down, m_norm_final, v_norm_mix, v_w_in, v_conv_w, v_w_conv_out, v_w_pool, v_pool_scale, v_norm_mem, v_w_kv, v_w_xattn_out, v_w_out, v_norm_ffn, v_w_gate, v_w_up, v_w_down, v_norm_final):
    T = x.shape[1]
    rows = D // NDEV
    ffb = DFF // NDEV
    prow = HD // NDEV
    me = 4 * lax.axis_index("x") + 2 * lax.axis_index("y") + lax.axis_index("c")

    shards = [w_in[0].astype(BF16), w_conv_out[0].astype(BF16), w_xattn_out[0].astype(BF16), w_out[0].astype(BF16),
              w_pool[0].astype(BF16).reshape(NPOOL * prow, HD), w_kv[0].astype(BF16),
              w_gate[0].T.astype(BF16), w_up[0].T.astype(BF16), w_down[0].astype(BF16),
              jnp.pad(conv_w[0], ((0, 5), (0, 0)))]

    cx, cy, cc = (lax.axis_index(n) for n in AXES)
    chip = 2 * cx + cy

    def land(own, index, slots):
        return lax.dynamic_update_index_in_dim(lax.empty((slots,) + own.shape, own.dtype), own, index, 0)

    needed = ["in_pair", "in_near", "in_far", "mix", "merge", "gate_up", "down"]
    members = {"mix": [9, 1, 4, 5], "merge": [2, 3], "gate_up": [6, 7], "down": [8]}
    near = (2, 4)
    plans = {"in_pair": _plan_pair(), "in_near": _plan_gather_chips(1, near), "in_far": _plan_far_chip()}
    plans.update({n: _plan_gather_chips(len(members[n])) for n in members})
    g_bufs = {"in_pair": [land(shards[0], cc, 2)], "in_near": [land(shards[0], me, NDEV)],
              "in_far": [None, lax.empty((2, D, D), BF16)]}
    g_bufs.update({n: [land(shards[i], me, NDEV) for i in members[n]] for n in members})
    first_handles, _ = _copies_start([(g_bufs[n], plans[n]) for n in needed[:2]], "gather_start", x)
    g_handles = dict(zip(needed[:2], first_handles))
    pair_ids = jnp.array([0, 1], jnp.int32)

    on_last_leg = {}

    def get(group, after, early=False):
        if group == "in_places":
            return g_bufs["in_pair"][0], _w_in_places()[chip]
        if group == "in_pair":
            rest = [b for n in members for b in g_bufs[n]]
            g_bufs["in_far"][0], = _copies_wait(g_handles[group], plans[group], "gather_wait_" + group, after, *rest)
            return g_bufs["in_far"][0], pair_ids, (2 * chip + pair_ids).astype(jnp.int32)
        if group not in on_last_leg:
            n_bufs = len(g_bufs[group])
            if group == "in_near":
                landed, plan, more = [0], _plan_gather_sibling(1, near), [(g_bufs[n], plans[n]) for n in needed[2:]]
            elif group == "in_far":
                landed, plan, more = [1], _plan_far_sibling(), []
            else:
                landed, plan, more = list(range(n_bufs)), _plan_gather_sibling(n_bufs), []
            thru, handles, token = _copies_wait_start(g_handles[group], plans[group], (landed, plan), more,
                                                      "gather_pass_" + group, after)
            if group == "in_far":
                g_bufs["in_pair"] = thru[:1]
            g_handles.update(zip(needed[2:], handles[1:]))
            on_last_leg[group] = (handles[0], plan, token)
        if early:
            return None
        handle, plan, token = on_last_leg[group]
        got = _copies_wait(handle, plan, "gather_passed_" + group, after if group in ("gate_up", "down") else token)
        if group == "in_near":
            groups = jnp.stack([me ^ k for k in (2, 3, 4, 5)]).astype(jnp.int32)
            return got[0], groups, groups
        if group == "in_far":
            return got[0], pair_ids, (2 * (3 - chip) + pair_ids).astype(jnp.int32)
        if group == "mix":
            cw_g, w_co_g, w_pool_g, w_kv_g = got
            return cw_g, w_co_g.reshape(D, D), w_pool_g, w_kv_g
        if group == "merge":
            return got[0].reshape(D, D), got[1].reshape(D, D)
        return [g.reshape(DFF, D) for g in got]

    started = {}

    def put(group, grads):
        if group == "ffn":
            sends = [g.reshape(NDEV, ffb, D) for g in grads]
        elif group == "mix":
            dw_co, dw_xo, dw_o, dw_pool, dw_kv = grads
            sends = [dw_co.reshape(NDEV, rows, D), dw_xo.reshape(NDEV, rows, D), dw_o.reshape(NDEV, rows, D),
                     dw_pool.reshape(NPOOL, NDEV, prow, HD).transpose(1, 0, 2, 3).reshape(NDEV, NPOOL * prow, HD), dw_kv]
        else:
            slot_of = [SLOT_GROUPS.index(g) for g in range(NSPLIT)]
            order = jnp.array([[slot_of[2 * q + 1 - c] for q in range(4)] + [slot_of[2 * q + c] for q in range(4)]
                               for c in range(2)], jnp.int32)[cc]
            order = jnp.concatenate([order, chip.reshape(1).astype(jnp.int32)])
            swapping.append((group, None, 1, list(_wgrad_in_swap(grads[0], grads[1], order, *grads[2:]))))
            return swapping[-1][3][0]
        n = len(sends)
        halves = [lax.empty((4,) + s.shape[1:], s.dtype) for s in sends]
        (handle,), token = _copies_start([(sends + halves, _plan_scatter_sibling(n))], "scatter_swap_" + group, norm_mix)
        swapping.append((group, handle, n, None))
        return token

    swapping = []

    def flush(after):
        group, handle, n, bufs = swapping.pop()
        if handle is None:
            sums, lands = bufs[:n], bufs[n:]
        else:
            bufs = _copies_wait(handle, _plan_scatter_sibling(n), "scatter_swapped_" + group, after)
            sums, lands = _pair_sums(bufs[:n], bufs[n:], cc, chip, "pair_sums_" + group)
        (handle,), token = _copies_start([(sums + lands, _plan_scatter_chips(n))], "scatter_start_" + group, norm_mix)
        started[group] = (handle, _plan_scatter_chips(n))
        return token

    def take(group, after):
        handle, plan = started[group]
        return _copies_wait(handle, plan, "scatter_wait_" + group, after)[len(handle[2]) // 2:]

    gains = (norm_mix, pool_scale, norm_mem, norm_ffn, norm_final.reshape(1, D))
    grad_x, small = _local_step(x[0], mem[0], loss_target[0], gains, get, put, flush)

    everyone = _plan_gather_chips(1, tuple(range(1, NDEV)))
    (small_handle,), token = _copies_start([([land(small, me, NDEV)], everyone)], "small_start", norm_mix)

    res = {}

    def update(group, names, ws, gs, ms, vs, from_parts, steps, transposed=()):
        view = lambda a, name: a[0].T if name in transposed else a
        flat = [[view(a, name).reshape(g.shape[-2:]) for a in (w, m, v)] for name, w, g, m, v in zip(names, ws, gs, ms, vs)]
        outs = _adamw([f[0] for f in flat], gs, [f[1] for f in flat], [f[2] for f in flat], "adamw_" + group,
                      from_parts, steps)
        for name, w, four in zip(names, ws, outs):
            res[name] = [(o.T if name in transposed else o).reshape(w.shape) for o in four]

    p_g, p_u, p_d = take("ffn", token)
    update("ffn", ["w_gate", "w_up", "w_down"], [w_gate, w_up, w_down], [p_g, p_u, p_d],
           [m_w_gate, m_w_up, m_w_down], [v_w_gate, v_w_up, v_w_down], True, 2, transposed=("w_gate", "w_up"))

    small_all, = _copies_wait(small_handle, everyone, "small_wait", res["w_down"][1])
    replicated = {"norm_mix": (norm_mix, m_norm_mix, v_norm_mix), "pool_scale": (pool_scale, m_pool_scale, v_pool_scale),
                  "norm_mem": (norm_mem, m_norm_mem, v_norm_mem), "norm_ffn": (norm_ffn, m_norm_ffn, v_norm_ffn),
                  "norm_final": (norm_final, m_norm_final, v_norm_final), "conv_w": (conv_w, m_conv_w, v_conv_w)}
    outs, loss = _adamw_small(small_all, me, [[a.reshape(1, D) for a in replicated[n]] for n in list(replicated)[:5]],
                              [a.transpose(1, 0, 2) for a in replicated["conv_w"]])
    for (name, three), four in zip(replicated.items(), outs):
        res[name] = [o.transpose(1, 0, 2) if name == "conv_w" else o.reshape(three[0].shape) for o in four]
    loss = loss.reshape(())

    p_co, p_xo, p_o, p_pool, p_kv = take("mix", res["conv_w"][1])
    update("mix", ["w_conv_out", "w_xattn_out", "w_out", "w_pool", "w_kv"], [w_conv_out, w_xattn_out, w_out, w_pool, w_kv],
           [p_co, p_xo, p_o, p_pool, p_kv], [m_w_conv_out, m_w_xattn_out, m_w_out, m_w_pool, m_w_kv],
           [v_w_conv_out, v_w_xattn_out, v_w_out, v_w_pool, v_w_kv], True, 2)
    (p_in,) = take("in", res["w_out"][1])
    update("in", ["w_in"], [w_in], [p_in], [m_w_in], [v_w_in], True, 4)
    order = ["norm_mix", "w_in", "conv_w", "w_conv_out", "w_pool", "pool_scale", "norm_mem", "w_kv", "w_xattn_out", "w_out",
             "norm_ffn", "w_gate", "w_up", "w_down", "norm_final"]
    return (loss, grad_x[None], *[res[n][0] for n in order], *[res[n][1] for n in order],
            *[res[n][2] for n in order], *[res[n][3] for n in order])
```
